```python
import math
import jax, jax.numpy as jnp
from jax import lax
import numpy as np

D_MODEL = 1024
BATCH = 32
SEQ = 2048
DEPTH = 2

HEAD_DIM = 128
HEADS_PER_GROUP = 4
ATTN_PATTERNS = ((128, 1), (512, 4), (2048, 16))
N_ATTN_GROUPS = len(ATTN_PATTERNS)
ATTN_WIDTH = N_ATTN_GROUPS * HEADS_PER_GROUP * HEAD_DIM
ATTN_OUT_WIDTH = HEADS_PER_GROUP * HEAD_DIM
ROPE_DIM = HEAD_DIM // 4
ROPE_THETA = 500000.0
BLOCK = 128
NEG_INF = -1e30
POOL_WINDOWS = (2, 4, 8, 16)
POOL_GROUP_WIDTH = D_MODEL // 4
POOL_WIDTH = len(POOL_WINDOWS) * POOL_GROUP_WIDTH
IN_WIDTH = 3 * ATTN_WIDTH + POOL_WIDTH + 2 * D_MODEL
D_FF = 2816
CONV_WIDTH = 3
PLE_DIM = 256
RMS_EPS = 1e-6

kernel_name = 'hybrid_dilated_attn_pool_gated_merge'


def rmsnorm(x, g):
    x32 = x.astype(jnp.float32)
    y = x32 * lax.rsqrt(jnp.mean(x32 * x32, axis=-1, keepdims=True) + RMS_EPS)
    return (y * g.astype(jnp.float32)).astype(x.dtype)


def partial_rotary(t, cos, sin):
    half = ROPE_DIM // 2
    t1 = t[..., :half].astype(jnp.float32)
    t2 = t[..., half:ROPE_DIM].astype(jnp.float32)
    c = cos[:, None, None, :]
    s = sin[:, None, None, :]
    rot = jnp.concatenate([t1 * c - t2 * s, t2 * c + t1 * s], axis=-1).astype(t.dtype)
    return jnp.concatenate([rot, t[..., ROPE_DIM:]], axis=-1)


def dilated_window_attention(q, k, v, window, dilation):
    B, S, H, hd = q.shape
    span = BLOCK * dilation
    s_pad = -(-S // span) * span
    L = s_pad // dilation
    nb = L // BLOCK
    w_sub = window // dilation

    def to_blocks(t):
        t = jnp.pad(t, ((0, 0), (0, s_pad - S), (0, 0), (0, 0)))
        t = t.reshape(B, L, dilation, H, hd).transpose(0, 2, 1, 3, 4)
        return t.reshape(B, dilation, nb, BLOCK, H, hd)

    def with_prev(t):
        prev = jnp.pad(t, ((0, 0), (0, 0), (1, 0), (0, 0), (0, 0), (0, 0)))[:, :, :-1]
        return jnp.concatenate([prev, t], axis=3)

    qb = to_blocks(q)
    kk = with_prev(to_blocks(k))
    vv = with_prev(to_blocks(v))
    scores = jnp.einsum('brnqhd,brnkhd->brnhqk', qb, kk,
                        preferred_element_type=jnp.float32) * (hd ** -0.5)
    qi = jnp.arange(BLOCK)[:, None]
    ki = jnp.arange(2 * BLOCK)[None, :]
    diff = BLOCK + qi - ki
    band = (diff >= 0) & (diff <= w_sub)
    blk = jnp.arange(nb)[:, None, None]
    mask = band[None] & ((blk > 0) | (ki[None] >= BLOCK))
    scores = jnp.where(mask[None, None, :, None], scores, NEG_INF)
    lse = jax.nn.logsumexp(scores, axis=-1)
    probs = jnp.exp(scores - lse[..., None])
    out = jnp.einsum('brnhqk,brnkhd->brnqhd', probs.astype(v.dtype), vv,
                     preferred_element_type=jnp.float32)
    out = out.reshape(B, dilation, L, H, hd).transpose(0, 2, 1, 3, 4)
    out = out.reshape(B, s_pad, H, hd)[:, :S]
    lse = lse.transpose(0, 1, 2, 4, 3).reshape(B, dilation, L, H).transpose(0, 2, 1, 3)
    lse = lse.reshape(B, s_pad, H)[:, :S]
    return out, lse


def multiscale_pool_mixer(u, pool_w, pool_scale):
    B, S, _ = u.shape
    u32 = u.astype(jnp.float32)
    csum = jnp.cumsum(u32, axis=1)
    t = jnp.arange(S)
    groups = []
    for g, w in enumerate(POOL_WINDOWS):
        sl = slice(g * POOL_GROUP_WIDTH, (g + 1) * POOL_GROUP_WIDTH)
        cg = csum[..., sl]
        shifted = jnp.pad(cg, ((0, 0), (w, 0), (0, 0)))[:, :S]
        count = jnp.minimum(t + 1, w).astype(jnp.float32)
        groups.append((cg - shifted) / count[None, :, None] - u32[..., sl])
    pooled = jnp.stack(groups, axis=2).astype(u.dtype)
    mixed = jnp.einsum('bsgc,gcd->bsgd', pooled, pool_w).reshape(B, S, POOL_WIDTH)
    return mixed * pool_scale


def conv_gated_mlp(h, w_up, conv_w, conv_b, w_down):
    S = h.shape[1]
    u = h @ w_up
    y = conv_b
    for tap in range(CONV_WIDTH):
        shift = CONV_WIDTH - 1 - tap
        y = y + conv_w[tap] * jnp.pad(u, ((0, 0), (shift, 0), (0, 0)))[:, :S]
    gate, val = jnp.split(y, 2, axis=-1)
    return (jax.nn.silu(gate) * val) @ w_down


def _fwd_setup_inputs(seed: int = 0) -> dict:
    key = jax.random.key(seed)
    ks = jax.random.split(key, 20)
    f32 = jnp.float32

    def nrm(k, shape, fan_in):
        return jax.random.normal(k, shape, f32) * (fan_in ** -0.5)

    def gain(k, shape):
        return 1.0 + 0.02 * jax.random.normal(k, shape, f32)

    return {
        'x': jax.random.normal(ks[0], (BATCH, SEQ, D_MODEL), f32),
        'p': jax.random.normal(ks[1], (DEPTH, BATCH, SEQ, PLE_DIM), f32),
        'g_mix': gain(ks[2], (DEPTH, D_MODEL)),
        'w_in': nrm(ks[3], (DEPTH, D_MODEL, IN_WIDTH), D_MODEL),
        'w_ya': nrm(ks[4], (DEPTH, ATTN_OUT_WIDTH, D_MODEL), ATTN_OUT_WIDTH),
        'w_yb': nrm(ks[5], (DEPTH, POOL_WIDTH, D_MODEL), POOL_WIDTH),
        'pool_w': nrm(ks[6], (DEPTH, len(POOL_WINDOWS), POOL_GROUP_WIDTH, POOL_GROUP_WIDTH), POOL_GROUP_WIDTH),
        'pool_scale': gain(ks[7], (DEPTH, POOL_WIDTH)),
        'w_o': nrm(ks[8], (DEPTH, D_MODEL, D_MODEL), D_MODEL),
        'g_ffn': gain(ks[9], (DEPTH, D_MODEL)),
        'w_up': nrm(ks[10], (DEPTH, D_MODEL, 2 * D_FF), D_MODEL),
        'conv_w': nrm(ks[11], (DEPTH, CONV_WIDTH, 2 * D_FF), CONV_WIDTH),
        'conv_b': 0.01 * jax.random.normal(ks[12], (DEPTH, 2 * D_FF), f32),
        'w_down': nrm(ks[13], (DEPTH, D_FF, D_MODEL), D_FF),
        'g_ple': gain(ks[14], (DEPTH, D_MODEL)),
        'w_ple': nrm(ks[15], (DEPTH, PLE_DIM, D_MODEL), PLE_DIM),
        'w_ple_gate': nrm(ks[16], (DEPTH, D_MODEL, D_MODEL), D_MODEL),
        'g_final': gain(ks[17], (D_MODEL,)),
    }


def _fwd_reference(x, p, g_mix, w_in, w_ya, w_yb, pool_w, pool_scale, w_o, g_ffn,
              w_up, conv_w, conv_b, w_down, g_ple, w_ple, w_ple_gate, g_final):
    B, S, _ = x.shape
    pos = jnp.arange(S, dtype=jnp.float32)
    inv_freq = jnp.exp(jnp.arange(0, ROPE_DIM, 2, dtype=jnp.float32)
                       * (-math.log(ROPE_THETA) / ROPE_DIM))
    ang = pos[:, None] * inv_freq[None, :]
    cos, sin = jnp.cos(ang), jnp.sin(ang)
    split_at = [ATTN_WIDTH, 2 * ATTN_WIDTH, 3 * ATTN_WIDTH,
                3 * ATTN_WIDTH + POOL_WIDTH, 3 * ATTN_WIDTH + POOL_WIDTH + D_MODEL]
    head_shape = (B, S, N_ATTN_GROUPS, HEADS_PER_GROUP, HEAD_DIM)

    for i in range(DEPTH):
        h = rmsnorm(x, g_mix[i])
        z = h @ w_in[i]
        q, k, v, u_pool, gate_a, gate_b = jnp.split(z, split_at, axis=-1)
        q = partial_rotary(q.reshape(head_shape), cos, sin)
        k = partial_rotary(k.reshape(head_shape), cos, sin)
        v = v.reshape(head_shape)

        outs, lses = [], []
        for g, (window, dilation) in enumerate(ATTN_PATTERNS):
            o_g, lse_g = dilated_window_attention(q[:, :, g], k[:, :, g], v[:, :, g],
                                                  window, dilation)
            outs.append(o_g)
            lses.append(lse_g)
        weights = jax.nn.softmax(jnp.stack(lses, axis=0), axis=0)
        attn = jnp.sum(weights[..., None] * jnp.stack(outs, axis=0), axis=0)
        y_a = attn.reshape(B, S, ATTN_OUT_WIDTH).astype(x.dtype) @ w_ya[i]

        y_b = multiscale_pool_mixer(u_pool, pool_w[i], pool_scale[i]) @ w_yb[i]

        merged = jax.nn.sigmoid(gate_a) * y_a + jax.nn.sigmoid(gate_b) * y_b
        x = x + merged @ w_o[i]

        x = x + conv_gated_mlp(rmsnorm(x, g_ffn[i]), w_up[i], conv_w[i], conv_b[i], w_down[i])

        ple_gate = jax.nn.sigmoid(rmsnorm(x, g_ple[i]) @ w_ple_gate[i])
        x = x + (p[i] @ w_ple[i]) * ple_gate

    return rmsnorm(x, g_final)


import jax as _jax
import jax.numpy as _jnp

TWIN_FORMAT = 'train_step'
FWD_PARAMS = ['x', 'p', 'g_mix', 'w_in', 'w_ya', 'w_yb', 'pool_w', 'pool_scale', 'w_o', 'g_ffn', 'w_up', 'conv_w', 'conv_b', 'w_down', 'g_ple', 'w_ple', 'w_ple_gate', 'g_final']
TWIN_WEIGHTS = ['g_mix', 'w_in', 'w_ya', 'w_yb', 'pool_w', 'pool_scale', 'w_o', 'g_ffn', 'w_up', 'conv_w', 'conv_b', 'w_down', 'g_ple', 'w_ple', 'w_ple_gate', 'g_final']
TWIN_DIFF_INPUT = 'x'
TWIN_INPUTS = ['x', 'p', 'g_mix', 'w_in', 'w_ya', 'w_yb', 'pool_w', 'pool_scale', 'w_o', 'g_ffn', 'w_up', 'conv_w', 'conv_b', 'w_down', 'g_ple', 'w_ple', 'w_ple_gate', 'g_final', 'loss_target', 'm_g_mix', 'm_w_in', 'm_w_ya', 'm_w_yb', 'm_pool_w', 'm_pool_scale', 'm_w_o', 'm_g_ffn', 'm_w_up', 'm_conv_w', 'm_conv_b', 'm_w_down', 'm_g_ple', 'm_w_ple', 'm_w_ple_gate', 'm_g_final', 'v_g_mix', 'v_w_in', 'v_w_ya', 'v_w_yb', 'v_pool_w', 'v_pool_scale', 'v_w_o', 'v_g_ffn', 'v_w_up', 'v_conv_w', 'v_conv_b', 'v_w_down', 'v_g_ple', 'v_w_ple', 'v_w_ple_gate', 'v_g_final']
TWIN_OUTPUTS = ['loss', 'grad_x', 'grad_g_mix', 'grad_w_in', 'grad_w_ya', 'grad_w_yb', 'grad_pool_w', 'grad_pool_scale', 'grad_w_o', 'grad_g_ffn', 'grad_w_up', 'grad_conv_w', 'grad_conv_b', 'grad_w_down', 'grad_g_ple', 'grad_w_ple', 'grad_w_ple_gate', 'grad_g_final', 'delta_g_mix', 'delta_w_in', 'delta_w_ya', 'delta_w_yb', 'delta_pool_w', 'delta_pool_scale', 'delta_w_o', 'delta_g_ffn', 'delta_w_up', 'delta_conv_w', 'delta_conv_b', 'delta_w_down', 'delta_g_ple', 'delta_w_ple', 'delta_w_ple_gate', 'delta_g_final', 'new_m_g_mix', 'new_m_w_in', 'new_m_w_ya', 'new_m_w_yb', 'new_m_pool_w', 'new_m_pool_scale', 'new_m_w_o', 'new_m_g_ffn', 'new_m_w_up', 'new_m_conv_w', 'new_m_conv_b', 'new_m_w_down', 'new_m_g_ple', 'new_m_w_ple', 'new_m_w_ple_gate', 'new_m_g_final', 'new_v_g_mix', 'new_v_w_in', 'new_v_w_ya', 'new_v_w_yb', 'new_v_pool_w', 'new_v_pool_scale', 'new_v_w_o', 'new_v_g_ffn', 'new_v_w_up', 'new_v_conv_w', 'new_v_conv_b', 'new_v_w_down', 'new_v_g_ple', 'new_v_w_ple', 'new_v_w_ple_gate', 'new_v_g_final']
TWIN_LEAF_KINDS = {'loss': 'loss', 'grad_x': 'grad_x', 'grad_g_mix': 'grad_w', 'grad_w_in': 'grad_w', 'grad_w_ya': 'grad_w', 'grad_w_yb': 'grad_w', 'grad_pool_w': 'grad_w', 'grad_pool_scale': 'grad_w', 'grad_w_o': 'grad_w', 'grad_g_ffn': 'grad_w', 'grad_w_up': 'grad_w', 'grad_conv_w': 'grad_w', 'grad_conv_b': 'grad_w', 'grad_w_down': 'grad_w', 'grad_g_ple': 'grad_w', 'grad_w_ple': 'grad_w', 'grad_w_ple_gate': 'grad_w', 'grad_g_final': 'grad_w', 'delta_g_mix': 'delta_w', 'delta_w_in': 'delta_w', 'delta_w_ya': 'delta_w', 'delta_w_yb': 'delta_w', 'delta_pool_w': 'delta_w', 'delta_pool_scale': 'delta_w', 'delta_w_o': 'delta_w', 'delta_g_ffn': 'delta_w', 'delta_w_up': 'delta_w', 'delta_conv_w': 'delta_w', 'delta_conv_b': 'delta_w', 'delta_w_down': 'delta_w', 'delta_g_ple': 'delta_w', 'delta_w_ple': 'delta_w', 'delta_w_ple_gate': 'delta_w', 'delta_g_final': 'delta_w', 'new_m_g_mix': 'new_m', 'new_m_w_in': 'new_m', 'new_m_w_ya': 'new_m', 'new_m_w_yb': 'new_m', 'new_m_pool_w': 'new_m', 'new_m_pool_scale': 'new_m', 'new_m_w_o': 'new_m', 'new_m_g_ffn': 'new_m', 'new_m_w_up': 'new_m', 'new_m_conv_w': 'new_m', 'new_m_conv_b': 'new_m', 'new_m_w_down': 'new_m', 'new_m_g_ple': 'new_m', 'new_m_w_ple': 'new_m', 'new_m_w_ple_gate': 'new_m', 'new_m_g_final': 'new_m', 'new_v_g_mix': 'new_v', 'new_v_w_in': 'new_v', 'new_v_w_ya': 'new_v', 'new_v_w_yb': 'new_v', 'new_v_pool_w': 'new_v', 'new_v_pool_scale': 'new_v', 'new_v_w_o': 'new_v', 'new_v_g_ffn': 'new_v', 'new_v_w_up': 'new_v', 'new_v_conv_w': 'new_v', 'new_v_conv_b': 'new_v', 'new_v_w_down': 'new_v', 'new_v_g_ple': 'new_v', 'new_v_w_ple': 'new_v', 'new_v_w_ple_gate': 'new_v', 'new_v_g_final': 'new_v'}


def _forward(args):
    return _fwd_reference(*[args[k] for k in FWD_PARAMS])


def _output_shape():
    out = _jax.eval_shape(lambda: _forward(_fwd_setup_inputs(0)))
    return out.shape, out.dtype

N_MICROBATCH = 1
ADAM_LR = 0.001
ADAM_B1 = 0.9
ADAM_B2 = 0.999
ADAM_EPS = 1e-08
ADAM_WD = 0.01
ADAM_STEP = 10
PER_EXAMPLE_BATCH_AXIS = {'x': 0, 'p': 1, 'loss_target': 0}
SHARED_INPUTS = []
_WEIGHT_DTYPES = {'g_mix': _jnp.float32, 'w_in': _jnp.float32, 'w_ya': _jnp.float32, 'w_yb': _jnp.float32, 'pool_w': _jnp.float32, 'pool_scale': _jnp.float32, 'w_o': _jnp.float32, 'g_ffn': _jnp.float32, 'w_up': _jnp.float32, 'conv_w': _jnp.float32, 'conv_b': _jnp.float32, 'w_down': _jnp.float32, 'g_ple': _jnp.float32, 'w_ple': _jnp.float32, 'w_ple_gate': _jnp.float32, 'g_final': _jnp.float32}
MOMENT_SCALE = {'g_mix': 1.110089e-01, 'w_in': 4.166509e-02, 'w_ya': 2.188727e-02, 'w_yb': 1.021710e-01, 'pool_w': 1.028922e-01, 'pool_scale': 1.051310e-01, 'w_o': 1.031048e-01, 'g_ffn': 1.564491e-01, 'w_up': 6.592274e-02, 'conv_w': 6.466619e-02, 'conv_b': 6.379307e-02, 'w_down': 1.069928e-01, 'g_ple': 3.692876e-02, 'w_ple': 9.216379e-02, 'w_ple_gate': 3.600379e-02, 'g_final': 6.400813e+01}


def _to_microbatches(a, axis):
    t = _jnp.moveaxis(a, axis, 0)
    t = t.reshape((N_MICROBATCH, t.shape[0] // N_MICROBATCH) + t.shape[1:])
    return _jnp.moveaxis(t, 1, axis + 1)


def setup_inputs(seed: int = 0) -> dict:
    inp = _fwd_setup_inputs(seed)
    key = _jax.random.fold_in(_jax.random.key(seed), 7919)
    shape, _ = _output_shape()
    out = dict(inp)
    out["loss_target"] = _jax.random.normal(_jax.random.fold_in(key, 0), shape, _jnp.float32)
    for i, name in enumerate(TWIN_WEIGHTS):
        w = inp[name].astype(_jnp.float32)
        if MOMENT_SCALE is None:
            s = _jnp.sqrt(_jnp.mean(_jnp.square(w)) + 1e-30)
        else:
            s = MOMENT_SCALE[name]
        km, kv = _jax.random.split(_jax.random.fold_in(key, i + 1))
        out[name] = w
        out["m_" + name] = s * _jax.random.normal(km, w.shape, _jnp.float32)
        out["v_" + name] = (s * s) * _jax.random.uniform(kv, w.shape, _jnp.float32, 0.5, 1.5)
    if N_MICROBATCH > 1:
        for name, axis in PER_EXAMPLE_BATCH_AXIS.items():
            out[name] = _to_microbatches(out[name], axis)
    return {'x': out['x'], 'p': out['p'], 'g_mix': out['g_mix'], 'w_in': out['w_in'], 'w_ya': out['w_ya'], 'w_yb': out['w_yb'], 'pool_w': out['pool_w'], 'pool_scale': out['pool_scale'], 'w_o': out['w_o'], 'g_ffn': out['g_ffn'], 'w_up': out['w_up'], 'conv_w': out['conv_w'], 'conv_b': out['conv_b'], 'w_down': out['w_down'], 'g_ple': out['g_ple'], 'w_ple': out['w_ple'], 'w_ple_gate': out['w_ple_gate'], 'g_final': out['g_final'], 'loss_target': out['loss_target'], 'm_g_mix': out['m_g_mix'], 'm_w_in': out['m_w_in'], 'm_w_ya': out['m_w_ya'], 'm_w_yb': out['m_w_yb'], 'm_pool_w': out['m_pool_w'], 'm_pool_scale': out['m_pool_scale'], 'm_w_o': out['m_w_o'], 'm_g_ffn': out['m_g_ffn'], 'm_w_up': out['m_w_up'], 'm_conv_w': out['m_conv_w'], 'm_conv_b': out['m_conv_b'], 'm_w_down': out['m_w_down'], 'm_g_ple': out['m_g_ple'], 'm_w_ple': out['m_w_ple'], 'm_w_ple_gate': out['m_w_ple_gate'], 'm_g_final': out['m_g_final'], 'v_g_mix': out['v_g_mix'], 'v_w_in': out['v_w_in'], 'v_w_ya': out['v_w_ya'], 'v_w_yb': out['v_w_yb'], 'v_pool_w': out['v_pool_w'], 'v_pool_scale': out['v_pool_scale'], 'v_w_o': out['v_w_o'], 'v_g_ffn': out['v_g_ffn'], 'v_w_up': out['v_w_up'], 'v_conv_w': out['v_conv_w'], 'v_conv_b': out['v_conv_b'], 'v_w_down': out['v_w_down'], 'v_g_ple': out['v_g_ple'], 'v_w_ple': out['v_w_ple'], 'v_w_ple_gate': out['v_w_ple_gate'], 'v_g_final': out['v_g_final']}


def _loss(weights, diff, rest, loss_target):
    with _jax.named_scope("forward"):
        args = {**rest, TWIN_DIFF_INPUT: diff, **{k: w.astype(_WEIGHT_DTYPES[k]) for k, w in weights.items()}}
        y = _forward(args)
    with _jax.named_scope("loss_head"):
        err = _jnp.square(y.astype(_jnp.float32) - loss_target)
        return 0.5 * _jnp.sum(_jnp.mean(err, axis=-1)) if err.ndim else 0.5 * err


def _adamw(w, g, m, v):
    m = ADAM_B1 * m + (1.0 - ADAM_B1) * g
    v = ADAM_B2 * v + (1.0 - ADAM_B2) * _jnp.square(g)
    m_hat = m / (1.0 - ADAM_B1 ** ADAM_STEP)
    v_hat = v / (1.0 - ADAM_B2 ** ADAM_STEP)
    delta = -ADAM_LR * (m_hat / (_jnp.sqrt(v_hat) + ADAM_EPS) + ADAM_WD * w)
    return delta, m, v


def reference(x, p, g_mix, w_in, w_ya, w_yb, pool_w, pool_scale, w_o, g_ffn, w_up, conv_w, conv_b, w_down, g_ple, w_ple, w_ple_gate, g_final, loss_target, m_g_mix, m_w_in, m_w_ya, m_w_yb, m_pool_w, m_pool_scale, m_w_o, m_g_ffn, m_w_up, m_conv_w, m_conv_b, m_w_down, m_g_ple, m_w_ple, m_w_ple_gate, m_g_final, v_g_mix, v_w_in, v_w_ya, v_w_yb, v_pool_w, v_pool_scale, v_w_o, v_g_ffn, v_w_up, v_conv_w, v_conv_b, v_w_down, v_g_ple, v_w_ple, v_w_ple_gate, v_g_final):
    given = dict(x=x, p=p, g_mix=g_mix, w_in=w_in, w_ya=w_ya, w_yb=w_yb, pool_w=pool_w, pool_scale=pool_scale, w_o=w_o, g_ffn=g_ffn, w_up=w_up, conv_w=conv_w, conv_b=conv_b, w_down=w_down, g_ple=g_ple, w_ple=w_ple, w_ple_gate=w_ple_gate, g_final=g_final, loss_target=loss_target, m_g_mix=m_g_mix, m_w_in=m_w_in, m_w_ya=m_w_ya, m_w_yb=m_w_yb, m_pool_w=m_pool_w, m_pool_scale=m_pool_scale, m_w_o=m_w_o, m_g_ffn=m_g_ffn, m_w_up=m_w_up, m_conv_w=m_conv_w, m_conv_b=m_conv_b, m_w_down=m_w_down, m_g_ple=m_g_ple, m_w_ple=m_w_ple, m_w_ple_gate=m_w_ple_gate, m_g_final=m_g_final, v_g_mix=v_g_mix, v_w_in=v_w_in, v_w_ya=v_w_ya, v_w_yb=v_w_yb, v_pool_w=v_pool_w, v_pool_scale=v_pool_scale, v_w_o=v_w_o, v_g_ffn=v_g_ffn, v_w_up=v_w_up, v_conv_w=v_conv_w, v_conv_b=v_conv_b, v_w_down=v_w_down, v_g_ple=v_g_ple, v_w_ple=v_w_ple, v_w_ple_gate=v_w_ple_gate, v_g_final=v_g_final)
    weights = {n: given[n] for n in TWIN_WEIGHTS}
    shared = {n: given[n] for n in SHARED_INPUTS}
    per_example = {n: given[n] for n in ['x', 'p']}
    grad_fn = _jax.value_and_grad(_loss, argnums=(0, 1))

    def one_microbatch(ex, loss_target):
        ex = dict(ex)
        diff = ex.pop(TWIN_DIFF_INPUT)
        return grad_fn(weights, diff, {**shared, **ex}, loss_target)

    if N_MICROBATCH == 1:
        loss, (grad_w, grad_x) = one_microbatch(per_example, given["loss_target"])
    else:
        def body(carry, xs):
            loss_sum, grad_sum = carry
            l_k, (gw_k, gx_k) = one_microbatch(xs[0], xs[1])
            with _jax.named_scope("update"):
                return (loss_sum + l_k, _jax.tree.map(_jnp.add, grad_sum, gw_k)), gx_k

        init = (_jnp.zeros((), _jnp.float32), _jax.tree.map(_jnp.zeros_like, weights))
        (loss, grad_w), grad_x = _jax.lax.scan(body, init, (per_example, given["loss_target"]))
    with _jax.named_scope("update"):
        delta_w, new_m, new_v = {}, {}, {}
        for n in TWIN_WEIGHTS:
            delta_w[n], new_m[n], new_v[n] = _adamw(weights[n], grad_w[n], given["m_" + n], given["v_" + n])
    return (loss, grad_x, *[grad_w[n] for n in TWIN_WEIGHTS], *[delta_w[n] for n in TWIN_WEIGHTS],
            *[new_m[n] for n in TWIN_WEIGHTS], *[new_v[n] for n in TWIN_WEIGHTS])
```

```python
import functools
import math

import jax
import jax.numpy as jnp
from jax import lax
from jax.experimental import pallas as pl
from jax.experimental.pallas import tpu as pltpu

F32 = jnp.float32
BF16 = jnp.bfloat16
MESH = pl.DeviceIdType.MESH

D = 1024
SEQ = 2048
DEPTH = 2
HEAD = 128
GROUP_W = 512
DILATIONS = (1, 4, 16)
ROPE_DIM = 32
ROPE_THETA = 500000.0
NEG_INF = -1e30
ZW = 7680
OFF_K, OFF_V, OFF_U, OFF_GA, OFF_GB = 1536, 3072, 4608, 5632, 6656
FF = 2816
UW = 2 * FF
PLE = 256
NCHIP = 4
IN_S, UP_S, DN_S = ZW // NCHIP, UW // NCHIP, FF // NCHIP
RMS_EPS = 1e-6
LR, B1, B2, ADAM_EPS, WD, STEP = 0.001, 0.9, 0.999, 1e-08, 0.01, 10
SMALL_ROWS = 56
VMEM_CAP = 48 * 1024 * 1024


def _params(n_grid, vmem=VMEM_CAP):
    return pltpu.CompilerParams(dimension_semantics=("arbitrary",) * n_grid, vmem_limit_bytes=vmem)


def _sigmoid(v):
    return 1.0 / (1.0 + jnp.exp(-v))


def _rows8(v):
    return jnp.sum(v.reshape(v.shape[0] // 8, 8, v.shape[1]), axis=0)


def _mm(name, a, b, *, grid, a_spec, b_spec, o_spec, out_shape, ta=False, tb=False, k_axis=None, nk=1,
        acc_shape=None, res=None, res_spec=None, buf=None):
    dims = (((0,) if ta else (1,), (1,) if tb else (0,)), ((), ()))
    has_res, has_buf = res is not None, buf is not None

    def body(*refs):
        a_ref, b_ref = refs[0], refs[1]
        pos = 2
        r_ref = None
        if has_res:
            r_ref = refs[pos]
            pos += 1
        if has_buf:
            pos += 1
        o_ref = refs[pos]
        av = a_ref[...]
        av = av.reshape(-1, av.shape[-1]).astype(BF16)
        bv = b_ref[...]
        bv = bv.reshape(-1, bv.shape[-1]).astype(BF16)
        part = lax.dot_general(av, bv, dims, preferred_element_type=F32)

        def finish(val):
            if r_ref is not None:
                val = val + r_ref[...]
            o_ref[...] = val.reshape(o_ref.shape).astype(o_ref.dtype)

        if nk == 1:
            finish(part)
        else:
            acc_ref = refs[pos + 1]
            k = pl.program_id(k_axis)

            @pl.when(k == 0)
            def _():
                acc_ref[...] = part

            @pl.when(k > 0)
            def _():
                acc_ref[...] += part

            @pl.when(k == nk - 1)
            def _():
                finish(acc_ref[...])

    ins, in_specs = [a, b], [a_spec, b_spec]
    if has_res:
        ins.append(res)
        in_specs.append(res_spec)
    aliases = {}
    if has_buf:
        aliases = {len(ins): 0}
        ins.append(buf)
        in_specs.append(pl.BlockSpec(memory_space=pl.ANY))
    scratch = [pltpu.VMEM(acc_shape, F32)] if nk > 1 else []
    return pl.pallas_call(
        body, name=name, grid=grid, in_specs=in_specs, out_specs=o_spec, out_shape=out_shape,
        scratch_shapes=scratch, input_output_aliases=aliases, compiler_params=_params(len(grid)),
    )(*ins)


def _sds(shape, dtype):
    return jax.ShapeDtypeStruct(shape, dtype)


def _rms_fwd(name, x, g, tr=512):
    T = x.shape[0]

    def body(x_ref, g_ref, h_ref):
        xv = x_ref[...]
        r = lax.rsqrt(jnp.mean(xv * xv, axis=-1, keepdims=True) + RMS_EPS)
        h_ref[...] = (xv * r * g_ref[...]).astype(BF16)

    return pl.pallas_call(
        body, name=name, grid=(T // tr,),
        in_specs=[pl.BlockSpec((tr, D), lambda i: (i, 0)), pl.BlockSpec((1, D), lambda i: (0, 0))],
        out_specs=pl.BlockSpec((tr, D), lambda i: (i, 0)), out_shape=_sds((T, D), BF16),
        compiler_params=_params(1),
    )(x, g)


def _rms_bwd(name, x, dh, g, dres, tr=512):
    T = x.shape[0]

    def body(x_ref, dh_ref, g_ref, dres_ref, dx_ref, dg_ref):
        xv = x_ref[...]
        r = lax.rsqrt(jnp.mean(xv * xv, axis=-1, keepdims=True) + RMS_EPS)
        xh = xv * r
        dhv = dh_ref[...]
        part = _rows8(dhv * xh)

        @pl.when(pl.program_id(0) == 0)
        def _():
            dg_ref[...] = part

        @pl.when(pl.program_id(0) > 0)
        def _():
            dg_ref[...] += part

        dxh = dhv * g_ref[...]
        dx_ref[...] = dres_ref[...] + r * (dxh - xh * jnp.mean(dxh * xh, axis=-1, keepdims=True))

    row = pl.BlockSpec((tr, D), lambda i: (i, 0))
    return pl.pallas_call(
        body, name=name, grid=(T // tr,),
        in_specs=[row, row, pl.BlockSpec((1, D), lambda i: (0, 0)), row],
        out_specs=[row, pl.BlockSpec((8, D), lambda i: (0, 0))],
        out_shape=[_sds((T, D), F32), _sds((8, D), F32)],
        compiler_params=_params(1),
    )(x, dh, g, dres)


def _final_loss(x, g, tgt, tr=512):
    T = x.shape[0]

    def body(x_ref, g_ref, t_ref, dx_ref, dg_ref, sq_ref):
        xv = x_ref[...]
        r = lax.rsqrt(jnp.mean(xv * xv, axis=-1, keepdims=True) + RMS_EPS)
        xh = xv * r
        gv = g_ref[...]
        e = xh * gv - t_ref[...]
        dy = e * (1.0 / D)
        pg = _rows8(dy * xh)
        ps = _rows8(e * e)

        @pl.when(pl.program_id(0) == 0)
        def _():
            dg_ref[...] = pg
            sq_ref[...] = ps

        @pl.when(pl.program_id(0) > 0)
        def _():
            dg_ref[...] += pg
            sq_ref[...] += ps

        dxh = dy * gv
        dx_ref[...] = r * (dxh - xh * jnp.mean(dxh * xh, axis=-1, keepdims=True))

    row = pl.BlockSpec((tr, D), lambda i: (i, 0))
    acc = pl.BlockSpec((8, D), lambda i: (0, 0))
    return pl.pallas_call(
        body, name="final_loss", grid=(T // tr,),
        in_specs=[row, pl.BlockSpec((1, D), lambda i: (0, 0)), row],
        out_specs=[row, acc, acc],
        out_shape=[_sds((T, D), F32), _sds((8, D), F32), _sds((8, D), F32)],
        compiler_params=_params(1),
    )(x, g, tgt)


def _ple_fwd(name, x, pe, pg, tr=512):
    T = x.shape[0]

    def body(x_ref, pe_ref, pg_ref, o_ref):
        o_ref[...] = x_ref[...] + pe_ref[...] * _sigmoid(pg_ref[...])

    row = pl.BlockSpec((tr, D), lambda i: (i, 0))
    return pl.pallas_call(
        body, name=name, grid=(T // tr,), in_specs=[row, row, row], out_specs=row,
        out_shape=_sds((T, D), F32), compiler_params=_params(1),
    )(x, pe, pg)


def _ple_bwd(name, dx, pe, pg, tr=512):
    T = dx.shape[0]

    def body(dx_ref, pe_ref, pg_ref, dpe_ref, dpg_ref):
        s = _sigmoid(pg_ref[...])
        dxv = dx_ref[...]
        dpe_ref[...] = (dxv * s).astype(BF16)
        dpg_ref[...] = (dxv * pe_ref[...] * s * (1.0 - s)).astype(BF16)

    row = pl.BlockSpec((tr, D), lambda i: (i, 0))
    return pl.pallas_call(
        body, name=name, grid=(T // tr,), in_specs=[row, row, row], out_specs=[row, row],
        out_shape=[_sds((T, D), BF16), _sds((T, D), BF16)], compiler_params=_params(1),
    )(dx, pe, pg)


def _gate_fwd(name, z, ya, yb, tr=512):
    T = z.shape[0]
    w = 512

    def body(ga_ref, gb_ref, ya_ref, yb_ref, o_ref):
        o_ref[...] = (_sigmoid(ga_ref[...]) * ya_ref[...] + _sigmoid(gb_ref[...]) * yb_ref[...]).astype(BF16)

    col = pl.BlockSpec((tr, w), lambda i, j: (i, j))
    return pl.pallas_call(
        body, name=name, grid=(T // tr, D // w),
        in_specs=[pl.BlockSpec((tr, w), lambda i, j: (i, OFF_GA // w + j)),
                  pl.BlockSpec((tr, w), lambda i, j: (i, OFF_GB // w + j)), col, col],
        out_specs=col, out_shape=_sds((T, D), BF16), compiler_params=_params(2),
    )(z, z, ya, yb)


def _gate_bwd(name, z, off, y, dm, dz, tr=512):
    T = z.shape[0]
    w = 512
    has_dz = dz is not None

    def body(*refs):
        g_ref, y_ref, dm_ref = refs[:3]
        dy_ref, dz_ref = refs[-2:]
        s = _sigmoid(g_ref[...])
        dmv = dm_ref[...]
        dy_ref[...] = (dmv * s).astype(BF16)
        dz_ref[...] = (dmv * y_ref[...] * s * (1.0 - s)).astype(BF16)

    col = pl.BlockSpec((tr, w), lambda i, j: (i, j))
    gcol = pl.BlockSpec((tr, w), lambda i, j: (i, off // w + j))
    ins, in_specs, aliases = [z, y, dm], [gcol, col, col], {}
    if has_dz:
        ins.append(dz)
        in_specs.append(pl.BlockSpec(memory_space=pl.ANY))
        aliases = {3: 1}
    return pl.pallas_call(
        body, name=name, grid=(T // tr, D // w), in_specs=in_specs, out_specs=[col, gcol],
        out_shape=[_sds((T, D), BF16), _sds((T, ZW), BF16)], input_output_aliases=aliases,
        compiler_params=_params(2),
    )(*ins)


def _shift_down(v, k, rows):
    return jnp.where(rows >= k, pltpu.roll(v, k, 0), 0.0)


def _shift_up(v, k, rows):
    n = v.shape[0]
    return jnp.where(rows < n - k, pltpu.roll(v, n - k, 0), 0.0)


def _pool_window(v, g, rows, shift):
    s2 = v + shift(v, 1, rows)
    s4 = s2 + shift(s2, 2, rows)
    s8 = s4 + shift(s4, 4, rows)
    s16 = s8 + shift(s8, 8, rows)
    return jnp.where(g == 0, s2, jnp.where(g == 1, s4, jnp.where(g == 2, s8, s16)))


def _pool_count(g, rows):
    wlen = jnp.left_shift(2, g).astype(F32)
    return jnp.minimum(rows.astype(F32) + 1.0, wlen)


def _pool_fwd(name, z3, g_sc, layer, scale):
    Bn = z3.shape[0]
    gw = 256

    def body(u_ref, pw_ref, sc_ref, pooled_ref, ms_ref):
        g = pl.program_id(1)
        u = u_ref[...]
        rows = lax.broadcasted_iota(jnp.int32, u.shape, 0)
        pooled = (_pool_window(u, g, rows, _shift_down) / _pool_count(g, rows) - u).astype(BF16)
        pooled_ref[...] = pooled
        pw = pw_ref[...].reshape(gw, gw)
        mixed = jnp.dot(pooled, pw, preferred_element_type=F32)
        ms_ref[...] = (mixed * sc_ref[...]).astype(BF16)

    blk = pl.BlockSpec((None, SEQ, gw), lambda b, g: (b, 0, g))
    return pl.pallas_call(
        body, name=name, grid=(Bn, 4),
        in_specs=[pl.BlockSpec((None, SEQ, gw), lambda b, g: (b, 0, OFF_U // gw + g)),
                  pl.BlockSpec((NCHIP, None, 64, gw), lambda b, g: (0, layer, 12 + g, 0)),
                  pl.BlockSpec((1, gw), lambda b, g: (0, g))],
        out_specs=[blk, blk],
        out_shape=[_sds((Bn, SEQ, D), BF16), _sds((Bn, SEQ, D), BF16)],
        compiler_params=_params(2),
    )(z3, g_sc, scale)


def _pool_bwd(name, dms3, pooled3, g_sc, layer, scale, dz3, gg_sc):
    Bn = dms3.shape[0]
    gw = 256
    has_gg = gg_sc is not None

    def body(*refs):
        dms_ref, pooled_ref, pw_ref, sc_ref = refs[:4]
        dz_ref, dpw_ref, dsc_ref = refs[-3:]
        g, b = pl.program_id(0), pl.program_id(1)
        pooled = pooled_ref[...]
        pw = pw_ref[...].reshape(gw, gw)
        dms = dms_ref[...]
        mixed = jnp.dot(pooled, pw, preferred_element_type=F32)
        psc = _rows8(dms * mixed)
        dmixed = (dms * sc_ref[...]).astype(BF16)
        dpw = lax.dot_general(pooled, dmixed, (((0,), (0,)), ((), ())), preferred_element_type=F32)
        dpw = dpw.reshape(NCHIP, 64, gw)

        @pl.when(b == 0)
        def _():
            dsc_ref[...] = psc
            dpw_ref[...] = dpw

        @pl.when(b > 0)
        def _():
            dsc_ref[...] += psc
            dpw_ref[...] += dpw

        dpooled = lax.dot_general(dmixed, pw, (((1,), (1,)), ((), ())), preferred_element_type=F32)
        rows = lax.broadcasted_iota(jnp.int32, dpooled.shape, 0)
        dq = dpooled / _pool_count(g, rows)
        dz_ref[...] = (_pool_window(dq, g, rows, _shift_up) - dpooled).astype(BF16)

    ins = [dms3, pooled3, g_sc, scale, dz3]
    in_specs = [pl.BlockSpec((None, SEQ, gw), lambda g, b: (b, 0, g)),
                pl.BlockSpec((None, SEQ, gw), lambda g, b: (b, 0, g)),
                pl.BlockSpec((NCHIP, None, 64, gw), lambda g, b: (0, layer, 12 + g, 0)),
                pl.BlockSpec((1, gw), lambda g, b: (0, g)),
                pl.BlockSpec(memory_space=pl.ANY)]
    aliases = {4: 0}
    if has_gg:
        ins.append(gg_sc)
        in_specs.append(pl.BlockSpec(memory_space=pl.ANY))
        aliases[5] = 1
    return pl.pallas_call(
        body, name=name, grid=(4, Bn), in_specs=in_specs,
        out_specs=[pl.BlockSpec((None, SEQ, gw), lambda g, b: (b, 0, OFF_U // gw + g)),
                   pl.BlockSpec((NCHIP, None, 64, gw), lambda g, b: (0, layer, 12 + g, 0)),
                   pl.BlockSpec((8, gw), lambda g, b: (0, g))],
        out_shape=[_sds(dz3.shape, BF16), _sds((NCHIP, DEPTH, D, 256), F32), _sds((8, D), F32)],
        input_output_aliases=aliases, compiler_params=_params(2),
    )(*ins)


CT = 256
NCT = FF // CT


def _conv_pre(u, cw_ref, cb_ref, rows):
    return (cb_ref[...] + cw_ref[0:1, :] * _shift_down(u, 2, rows) + cw_ref[1:2, :] * _shift_down(u, 1, rows)
            + cw_ref[2:3, :] * u)


def _conv_fwd(name, u3, cw, cb):
    Bn = u3.shape[0]

    def body(ug_ref, uv_ref, cwg_ref, cwv_ref, cbg_ref, cbv_ref, a_ref):
        ug, uv = ug_ref[...], uv_ref[...]
        rows = lax.broadcasted_iota(jnp.int32, ug.shape, 0)
        yg = _conv_pre(ug, cwg_ref, cbg_ref, rows)
        yv = _conv_pre(uv, cwv_ref, cbv_ref, rows)
        a_ref[...] = (yg * _sigmoid(yg) * yv).astype(BF16)

    def blk(off):
        return pl.BlockSpec((None, SEQ, CT), lambda b, c: (b, 0, off + c))

    return pl.pallas_call(
        body, name=name, grid=(Bn, NCT),
        in_specs=[blk(0), blk(NCT),
                  pl.BlockSpec((3, CT), lambda b, c: (0, c)), pl.BlockSpec((3, CT), lambda b, c: (0, NCT + c)),
                  pl.BlockSpec((1, CT), lambda b, c: (0, c)), pl.BlockSpec((1, CT), lambda b, c: (0, NCT + c))],
        out_specs=blk(0), out_shape=_sds((Bn, SEQ, FF), BF16), compiler_params=_params(2),
    )(u3, u3, cw, cw, cb, cb)


def _conv_bwd(name, da3, u3, cw, cb):
    Bn = u3.shape[0]

    def body(da_ref, ug_ref, uv_ref, cwg_ref, cwv_ref, cbg_ref, cbv_ref, du_ref, dcw_ref, dcb_ref):
        c, b = pl.program_id(0), pl.program_id(1)
        ug, uv, da = ug_ref[...], uv_ref[...], da_ref[...]
        rows = lax.broadcasted_iota(jnp.int32, ug.shape, 0)
        yg = _conv_pre(ug, cwg_ref, cbg_ref, rows)
        s = _sigmoid(yg)

        def emit(dy, u, cw_ref):
            du = (cw_ref[2:3, :] * dy + cw_ref[1:2, :] * _shift_up(dy, 1, rows)
                  + cw_ref[0:1, :] * _shift_up(dy, 2, rows))
            du_ref[...] = du.astype(BF16)
            dcw = jnp.concatenate(
                [jnp.sum(dy * _shift_down(u, 2, rows), axis=0, keepdims=True),
                 jnp.sum(dy * _shift_down(u, 1, rows), axis=0, keepdims=True),
                 jnp.sum(dy * u, axis=0, keepdims=True)], axis=0)
            dcb = jnp.sum(dy, axis=0, keepdims=True)

            @pl.when(b == 0)
            def _():
                dcw_ref[...] = dcw
                dcb_ref[...] = dcb

            @pl.when(b > 0)
            def _():
                dcw_ref[...] += dcw
                dcb_ref[...] += dcb

        @pl.when(c < NCT)
        def _():
            yv = _conv_pre(uv, cwv_ref, cbv_ref, rows)
            emit(da * yv * (s * (1.0 + yg * (1.0 - s))), ug, cwg_ref)

        @pl.when(c >= NCT)
        def _():
            emit(da * (yg * s), uv, cwv_ref)

    def blk(fn):
        return pl.BlockSpec((None, SEQ, CT), lambda c, b: (b, 0, fn(c)))

    return pl.pallas_call(
        body, name=name, grid=(2 * NCT, Bn),
        in_specs=[blk(lambda c: c % NCT), blk(lambda c: c % NCT), blk(lambda c: NCT + c % NCT),
                  pl.BlockSpec((3, CT), lambda c, b: (0, c % NCT)),
                  pl.BlockSpec((3, CT), lambda c, b: (0, NCT + c % NCT)),
                  pl.BlockSpec((1, CT), lambda c, b: (0, c % NCT)),
                  pl.BlockSpec((1, CT), lambda c, b: (0, NCT + c % NCT))],
        out_specs=[blk(lambda c: c), pl.BlockSpec((3, CT), lambda c, b: (0, c)),
                   pl.BlockSpec((1, CT), lambda c, b: (0, c))],
        out_shape=[_sds((Bn, SEQ, UW), BF16), _sds((3, UW), F32), _sds((1, UW), F32)],
        compiler_params=_params(2),
    )(da3, u3, u3, cw, cw, cb, cb)


def _rope_tables():
    pos = jnp.arange(SEQ, dtype=F32)
    inv_freq = jnp.exp(jnp.arange(0, ROPE_DIM, 2, dtype=F32) * (-math.log(ROPE_THETA) / ROPE_DIM))
    ang = pos[:, None] * inv_freq[None, :]
    cos, sin = jnp.cos(ang), jnp.sin(ang)
    half = ROPE_DIM // 2
    zeros = jnp.zeros((SEQ, HEAD - ROPE_DIM), F32)
    zh = jnp.zeros((SEQ, half), F32)
    tab_c = jnp.concatenate([cos, cos, zeros + 1.0], axis=1)
    tab_a = jnp.concatenate([-sin, zh, zeros], axis=1)
    tab_b = jnp.concatenate([zh, sin, zeros], axis=1)
    return tab_c, tab_a, tab_b


def _rot(v, tc, ta, tb):
    half = ROPE_DIM // 2
    return v * tc + pltpu.roll(v, HEAD - half, 1) * ta + pltpu.roll(v, half, 1) * tb


def _rot_t(dv, tc, ta, tb):
    half = ROPE_DIM // 2
    return dv * tc + pltpu.roll(dv * ta, half, 1) + pltpu.roll(dv * tb, HEAD - half, 1)


def _band_masks():
    qi = lax.broadcasted_iota(jnp.int32, (HEAD, 2 * HEAD), 0)
    ki = lax.broadcasted_iota(jnp.int32, (HEAD, 2 * HEAD), 1)
    diff = HEAD + qi - ki
    both = (diff >= 0) & (diff <= HEAD)
    q1 = lax.broadcasted_iota(jnp.int32, (HEAD, HEAD), 0)
    k1 = lax.broadcasted_iota(jnp.int32, (HEAD, HEAD), 1)
    return q1 >= k1, both


_NT = (((1,), (1,)), ((), ()))
_TN = (((0,), (0,)), ((), ()))
_SCALE = HEAD ** -0.5


def _attn_geometry(d):
    L = SEQ // d
    hp = 1 if d == 1 else 4
    return L, L // HEAD, hp, hp * HEAD


def _attn_fwd(name, z, tabs, g, d):
    T = z.shape[0]
    Bn = T // SEQ
    L, nb, hp, W = _attn_geometry(d)
    z4 = z.reshape(Bn, L, d * ZW)
    tabs4 = [t.reshape(L, d * HEAD) for t in tabs]
    cpr, nh, gpr = ZW // W, GROUP_W // W, GROUP_W // W

    def body(q_ref, k_ref, v_ref, tc_ref, ta_ref, tb_ref, o_ref, l_ref, qs, ks, vs):
        tc, ta, tb = tc_ref[...], ta_ref[...], tb_ref[...]
        m_first, m_both = _band_masks()
        for hh in range(hp):
            sl = slice(hh * HEAD, (hh + 1) * HEAD)
            qs[:, sl] = _rot(q_ref[:, sl], tc, ta, tb).astype(BF16)
            ks[:, sl] = _rot(k_ref[:, sl], tc, ta, tb).astype(BF16)
            vs[:, sl] = v_ref[:, sl].astype(BF16)
        for hh in range(hp):
            sl = slice(hh * HEAD, (hh + 1) * HEAD)
            for n in range(nb):
                rq = slice(n * HEAD, (n + 1) * HEAD)
                rk = slice(max(n - 1, 0) * HEAD, (n + 1) * HEAD)
                s = lax.dot_general(qs[rq, sl], ks[rk, sl], _NT, preferred_element_type=F32) * _SCALE
                s = jnp.where(m_first if n == 0 else m_both, s, NEG_INF)
                m = jnp.max(s, axis=-1, keepdims=True)
                e = jnp.exp(s - m)
                den = jnp.sum(e, axis=-1, keepdims=True)
                p = (e / den).astype(BF16)
                o_ref[rq, sl] = jnp.dot(p, vs[rk, sl], preferred_element_type=F32)
                l_ref[rq, sl] = jnp.broadcast_to(m + jnp.log(den), (HEAD, HEAD))

    def zcol(off):
        return pl.BlockSpec((None, L, W), lambda b, r, h: (b, 0, r * cpr + (off + g * GROUP_W) // W + h))

    tab = pl.BlockSpec((L, HEAD), lambda b, r, h: (0, r))
    out = pl.BlockSpec((None, L, W), lambda b, r, h: (b, 0, r * gpr + h))
    o, lse = pl.pallas_call(
        body, name=name, grid=(Bn, d, nh),
        in_specs=[zcol(0), zcol(OFF_K), zcol(OFF_V), tab, tab, tab],
        out_specs=[out, out],
        out_shape=[_sds((Bn, L, d * GROUP_W), F32), _sds((Bn, L, d * GROUP_W), F32)],
        scratch_shapes=[pltpu.VMEM((L, W), BF16)] * 3,
        compiler_params=_params(3),
    )(z4, z4, z4, *tabs4)
    return o.reshape(T, GROUP_W), lse.reshape(T, GROUP_W)


def _attn_bwd(name, z, tabs, g, d, do, lse, delta, dz):
    T = z.shape[0]
    Bn = T // SEQ
    L, nb, hp, W = _attn_geometry(d)
    z4 = z.reshape(Bn, L, d * ZW)
    tabs4 = [t.reshape(L, d * HEAD) for t in tabs]
    cpr, nh, gpr = ZW // W, GROUP_W // W, GROUP_W // W
    view = lambda t: t.reshape(Bn, L, d * GROUP_W)

    def body(q_ref, k_ref, v_ref, tc_ref, ta_ref, tb_ref, do_ref, l_ref, dl_ref, dz_in, dz_ref,
             qs, ks, vs, dqs, dks, dvs, oq, ok, ov, sems):
        b, r, h = pl.program_id(0), pl.program_id(1), pl.program_id(2)
        tc, ta, tb = tc_ref[...], ta_ref[...], tb_ref[...]
        m_first, m_both = _band_masks()
        for hh in range(hp):
            sl = slice(hh * HEAD, (hh + 1) * HEAD)
            qs[:, sl] = _rot(q_ref[:, sl], tc, ta, tb).astype(BF16)
            ks[:, sl] = _rot(k_ref[:, sl], tc, ta, tb).astype(BF16)
            vs[:, sl] = v_ref[:, sl].astype(BF16)
        dks[...] = jnp.zeros_like(dks)
        dvs[...] = jnp.zeros_like(dvs)
        for hh in range(hp):
            sl = slice(hh * HEAD, (hh + 1) * HEAD)
            for n in range(nb):
                rq = slice(n * HEAD, (n + 1) * HEAD)
                rk = slice(max(n - 1, 0) * HEAD, (n + 1) * HEAD)
                qb, kk, vv, dob = qs[rq, sl], ks[rk, sl], vs[rk, sl], do_ref[rq, sl]
                s = lax.dot_general(qb, kk, _NT, preferred_element_type=F32) * _SCALE
                s = jnp.where(m_first if n == 0 else m_both, s, NEG_INF)
                p = jnp.exp(s - l_ref[rq, sl][:, 0:1])
                dp = lax.dot_general(dob, vv, _NT, preferred_element_type=F32)
                ds = (p * (dp - dl_ref[rq, sl][:, 0:1]) * _SCALE).astype(BF16)
                dqs[rq, sl] = jnp.dot(ds, kk, preferred_element_type=F32)
                dks[rk, sl] += lax.dot_general(ds, qb, _TN, preferred_element_type=F32)
                dvs[rk, sl] += lax.dot_general(p.astype(BF16), dob, _TN, preferred_element_type=F32)
        for hh in range(hp):
            sl = slice(hh * HEAD, (hh + 1) * HEAD)
            oq[:, sl] = _rot_t(dqs[:, sl], tc, ta, tb).astype(BF16)
            ok[:, sl] = _rot_t(dks[:, sl], tc, ta, tb).astype(BF16)
            ov[:, sl] = dvs[:, sl].astype(BF16)
        base = r * ZW + g * GROUP_W + h * W
        copies = []
        for i, (src, off) in enumerate(((oq, 0), (ok, OFF_K), (ov, OFF_V))):
            col = pl.multiple_of(base + off, HEAD)
            cp = pltpu.make_async_copy(src, dz_ref.at[b, :, pl.ds(col, W)], sems.at[i])
            cp.start()
            copies.append(cp)
        for cp in copies:
            cp.wait()

    def zcol(off):
        return pl.BlockSpec((None, L, W), lambda b, r, h: (b, 0, r * cpr + (off + g * GROUP_W) // W + h))

    tab = pl.BlockSpec((L, HEAD), lambda b, r, h: (0, r))
    gcol = pl.BlockSpec((None, L, W), lambda b, r, h: (b, 0, r * gpr + h))
    any_spec = pl.BlockSpec(memory_space=pl.ANY)
    dz4 = pl.pallas_call(
        body, name=name, grid=(Bn, d, nh),
        in_specs=[zcol(0), zcol(OFF_K), zcol(OFF_V), tab, tab, tab, gcol, gcol, gcol, any_spec],
        out_specs=any_spec,
        out_shape=_sds((Bn, L, d * ZW), BF16),
        scratch_shapes=[pltpu.VMEM((L, W), BF16)] * 3 + [pltpu.VMEM((L, W), F32)] * 3
        + [pltpu.VMEM((L, W), BF16)] * 3 + [pltpu.SemaphoreType.DMA((3,))],
        input_output_aliases={9: 0}, compiler_params=_params(3),
    )(z4, z4, z4, *tabs4, view(do), view(lse), view(delta), dz.reshape(Bn, L, d * ZW))
    return dz4.reshape(T, ZW)


def _merge_weights(l0, l1, l2):
    m = jnp.maximum(jnp.maximum(l0, l1), l2)
    e0, e1, e2 = jnp.exp(l0 - m), jnp.exp(l1 - m), jnp.exp(l2 - m)
    inv = 1.0 / (e0 + e1 + e2)
    return e0 * inv, e1 * inv, e2 * inv


def _merge_fwd(name, outs, lses, tr=512):
    T = outs[0].shape[0]

    def body(o0, o1, o2, l0, l1, l2, a_ref):
        w0, w1, w2 = _merge_weights(l0[...], l1[...], l2[...])
        a_ref[...] = (w0 * o0[...] + w1 * o1[...] + w2 * o2[...]).astype(BF16)

    row = pl.BlockSpec((tr, GROUP_W), lambda i: (i, 0))
    return pl.pallas_call(
        body, name=name, grid=(T // tr,), in_specs=[row] * 6, out_specs=row,
        out_shape=_sds((T, GROUP_W), BF16), compiler_params=_params(1),
    )(*outs, *lses)


def _merge_bwd(name, outs, lses, dattn, tr=512):
    T = outs[0].shape[0]

    def body(o0, o1, o2, l0, l1, l2, da_ref, d0, d1, d2, e0, e1, e2):
        w = _merge_weights(l0[...], l1[...], l2[...])
        da = da_ref[...]
        attn = w[0] * o0[...] + w[1] * o1[...] + w[2] * o2[...]
        prod = da * attn
        csum = jnp.concatenate(
            [jnp.broadcast_to(jnp.sum(prod[:, hh * HEAD:(hh + 1) * HEAD], axis=-1, keepdims=True), (tr, HEAD))
             for hh in range(GROUP_W // HEAD)], axis=1)
        for wg, d_ref, e_ref in zip(w, (d0, d1, d2), (e0, e1, e2)):
            d_ref[...] = (wg * da).astype(BF16)
            e_ref[...] = wg * csum

    row = pl.BlockSpec((tr, GROUP_W), lambda i: (i, 0))
    res = pl.pallas_call(
        body, name=name, grid=(T // tr,), in_specs=[row] * 7, out_specs=[row] * 6,
        out_shape=[_sds((T, GROUP_W), BF16)] * 3 + [_sds((T, GROUP_W), F32)] * 3,
        compiler_params=_params(1),
    )(*outs, *lses, dattn)
    return res[:3], res[3:]


def _local_step(x3, p4, tgt3, vecs, G):
    Bn = x3.shape[0]
    T = Bn * SEQ
    x = x3.reshape(T, D)
    tgt = tgt3.reshape(T, D)
    pb = p4.astype(BF16).reshape(DEPTH, T, PLE)
    tabs = _rope_tables()
    tm = 1024 if T % 1024 == 0 else 512
    nt = T // tm
    tk = 1024 if T % 1024 == 0 else 512
    ntk = T // tk
    tm5 = 512
    f32o = lambda n: _sds((T, n), F32)

    def spec(shape, fn):
        return pl.BlockSpec(shape, fn)

    saved = []
    for l in range(DEPTH):
        L = str(l)
        g_mix, g_ffn, g_ple = (vecs[k][l:l + 1] for k in ("g_mix", "g_ffn", "g_ple"))
        pscale, cb, cw = vecs["pool_scale"][l:l + 1], vecs["conv_b"][l:l + 1], vecs["conv_w"][l]
        h = _rms_fwd("rms_mix" + L, x, g_mix)
        z = _mm("mm_z" + L, h, G["in"], grid=(nt, 12),
                a_spec=spec((tm, D), lambda i, n: (i, 0)),
                b_spec=spec((None, None, D, 640), lambda i, n: (n // 3, l, 0, n % 3)),
                o_spec=spec((tm, 640), lambda i, n: (i, n)), out_shape=f32o(ZW))
        outs, lses = [], []
        for g, d in enumerate(DILATIONS):
            o_g, l_g = _attn_fwd("attn_fwd%d_%d" % (g, l), z, tabs, g, d)
            outs.append(o_g)
            lses.append(l_g)
        attn = _merge_fwd("merge_fwd" + L, outs, lses)
        ya = _mm("mm_ya" + L, attn, G["sc"], grid=(nt, NCHIP),
                 a_spec=spec((tm, GROUP_W), lambda i, n: (i, 0)),
                 b_spec=spec((None, None, GROUP_W, 256), lambda i, n: (n, l, 0, 0)),
                 o_spec=spec((tm, 256), lambda i, n: (i, n)), out_shape=f32o(D))
        z3 = z.reshape(Bn, SEQ, ZW)
        pooled3, ms3 = _pool_fwd("pool_fwd" + L, z3, G["sc"], l, pscale)
        ms = ms3.reshape(T, D)

        def row_sharded(name, a, rb, res=None, kdim=D):
            if rb is None:
                b_arr, b_spec = G["dn"], spec((NCHIP, None, DN_S, 512), lambda i, n: (0, l, 0, n))
            else:
                b_arr, b_spec = G["r3"], spec((NCHIP, None, 256, 512), lambda i, n: (0, l, rb, n))
            return _mm(name, a, b_arr, grid=(T // tm5, 2),
                       a_spec=spec((tm5, kdim), lambda i, n: (i, 0)), b_spec=b_spec,
                       o_spec=spec((tm5, 512), lambda i, n: (i, n)), out_shape=f32o(D),
                       res=res, res_spec=None if res is None else spec((tm5, 512), lambda i, n: (i, n)))

        yb = row_sharded("mm_yb" + L, ms, 0)
        merged = _gate_fwd("gate_fwd" + L, z, ya, yb)
        x1 = row_sharded("mm_o" + L, merged, 1, res=x)
        h2 = _rms_fwd("rms_ffn" + L, x1, g_ffn)
        u = _mm("mm_up" + L, h2, G["up"], grid=(T // tm5, NCHIP),
                a_spec=spec((tm5, D), lambda i, n: (i, 0)),
                b_spec=spec((None, None, D, UP_S), lambda i, n: (n, l, 0, 0)),
                o_spec=spec((tm5, UP_S), lambda i, n: (i, n)), out_shape=f32o(UW))
        u3 = u.reshape(Bn, SEQ, UW)
        act = _conv_fwd("conv_fwd" + L, u3, cw, cb).reshape(T, FF)
        x2 = row_sharded("mm_down" + L, act, None, res=x1, kdim=FF)
        h3 = _rms_fwd("rms_ple" + L, x2, g_ple)
        pg = row_sharded("mm_pg" + L, h3, 2)
        pe = _mm("mm_pe" + L, pb[l], G["sc"], grid=(nt, NCHIP),
                 a_spec=spec((tm, PLE), lambda i, n: (i, 0)),
                 b_spec=spec((None, None, 256, 256), lambda i, n: (n, l, 2, 0)),
                 o_spec=spec((tm, 256), lambda i, n: (i, n)), out_shape=f32o(D))
        x3n = _ple_fwd("ple_fwd" + L, x2, pe, pg)
        saved.append(dict(x=x, h=h, z=z, outs=outs, lses=lses, attn=attn, ya=ya, yb=yb, pooled3=pooled3, ms=ms,
                          merged=merged, x1=x1, h2=h2, u3=u3, act=act, x2=x2, h3=h3, pg=pg, pe=pe))
        x = x3n

    dx, dg_final8, sq8 = _final_loss(x, vecs["g_final"].reshape(1, D), tgt)

    GG = dict.fromkeys(("in", "up", "sc", "r3", "dn"))
    gg_shape = {k: _sds(G[k].shape, F32) for k in G}
    small = {"g_final": dg_final8}

    for l in reversed(range(DEPTH)):
        L = str(l)
        sv = saved[l]
        g_mix, g_ffn, g_ple = (vecs[k][l:l + 1] for k in ("g_mix", "g_ffn", "g_ple"))
        pscale, cb, cw = vecs["pool_scale"][l:l + 1], vecs["conv_b"][l:l + 1], vecs["conv_w"][l]

        def wgrad_rows(name, a, b_arr, key, rb, a_cols=256):
            GG[key] = _mm(name, a, b_arr, grid=(NCHIP, 2, ntk), ta=True, k_axis=2, nk=ntk, acc_shape=(256, 512),
                          a_spec=spec((tk, 256), lambda m, n, k: (k, m)),
                          b_spec=spec((tk, 512), lambda m, n, k: (k, n)),
                          o_spec=spec((None, None, 256, 512), lambda m, n, k: (m, l, rb, n)),
                          out_shape=gg_shape[key], buf=GG[key])

        def dgrad_rows(name, dy, rb):
            return _mm(name, dy, G["r3"], grid=(T // tm5, NCHIP), tb=True,
                       a_spec=spec((tm5, D), lambda i, n: (i, 0)),
                       b_spec=spec((None, None, 256, D), lambda i, n: (n, l, rb, 0)),
                       o_spec=spec((tm5, 256), lambda i, n: (i, n)), out_shape=f32o(D))

        dpe, dpg = _ple_bwd("ple_bwd" + L, dx, sv["pe"], sv["pg"])
        GG["sc"] = _mm("wg_ple" + L, pb[l], dpe, grid=(NCHIP, ntk), ta=True, k_axis=1, nk=ntk, acc_shape=(256, 256),
                       a_spec=spec((tk, PLE), lambda n, k: (k, 0)), b_spec=spec((tk, 256), lambda n, k: (k, n)),
                       o_spec=spec((None, None, 256, 256), lambda n, k: (n, l, 2, 0)),
                       out_shape=gg_shape["sc"], buf=GG["sc"])
        wgrad_rows("wg_pg" + L, sv["h3"], dpg, "r3", 2)
        dh3 = dgrad_rows("dg_pg" + L, dpg, 2)
        dx, small["g_ple" + L] = _rms_bwd("rms_ple_bwd" + L, sv["x2"], dh3, g_ple, dx)

        da = _mm("dg_down" + L, dx, G["dn"], grid=(T // 256,), tb=True,
                 a_spec=spec((256, D), lambda i: (i, 0)),
                 b_spec=spec((NCHIP, None, DN_S, D), lambda i: (0, l, 0, 0)),
                 o_spec=spec((256, FF), lambda i: (i, 0)), out_shape=f32o(FF))
        GG["dn"] = _mm("wg_down" + L, sv["act"], dx, grid=(2, T // 512), ta=True, k_axis=1, nk=T // 512,
                       acc_shape=(FF, 512),
                       a_spec=spec((512, FF), lambda n, k: (k, 0)), b_spec=spec((512, 512), lambda n, k: (k, n)),
                       o_spec=spec((NCHIP, None, DN_S, 512), lambda n, k: (0, l, 0, n)),
                       out_shape=gg_shape["dn"], buf=GG["dn"])
        du3, dcw, dcb = _conv_bwd("conv_bwd" + L, da.reshape(Bn, SEQ, FF), sv["u3"], cw, cb)
        small["conv_w" + L], small["conv_b" + L] = dcw, dcb
        du = du3.reshape(T, UW)
        dh2 = _mm("dg_up" + L, du, G["up"], grid=(T // tm5, NCHIP), tb=True, k_axis=1, nk=NCHIP, acc_shape=(tm5, D),
                  a_spec=spec((tm5, UP_S), lambda i, k: (i, k)),
                  b_spec=spec((None, None, D, UP_S), lambda i, k: (k, l, 0, 0)),
                  o_spec=spec((tm5, D), lambda i, k: (i, 0)), out_shape=f32o(D))
        GG["up"] = _mm("wg_up" + L, sv["h2"], du, grid=(2, NCHIP, T // 512), ta=True, k_axis=2, nk=T // 512,
                       acc_shape=(512, UP_S),
                       a_spec=spec((512, 512), lambda m, j, k: (k, m)),
                       b_spec=spec((512, UP_S), lambda m, j, k: (k, j)),
                       o_spec=spec((None, None, 512, UP_S), lambda m, j, k: (j, l, m, 0)),
                       out_shape=gg_shape["up"], buf=GG["up"])
        dx, small["g_ffn" + L] = _rms_bwd("rms_ffn_bwd" + L, sv["x1"], dh2, g_ffn, dx)

        dmerged = dgrad_rows("dg_o" + L, dx, 1)
        wgrad_rows("wg_o" + L, sv["merged"], dx, "r3", 1)
        dya, dz = _gate_bwd("gate_bwd_a" + L, sv["z"], OFF_GA, sv["ya"], dmerged, None)
        dyb, dz = _gate_bwd("gate_bwd_b" + L, sv["z"], OFF_GB, sv["yb"], dmerged, dz)
        dms = dgrad_rows("dg_yb" + L, dyb, 0)
        wgrad_rows("wg_yb" + L, sv["ms"], dyb, "r3", 0)
        dz3, GG["sc"], small["pool_scale" + L] = _pool_bwd(
            "pool_bwd" + L, dms.reshape(Bn, SEQ, D), sv["pooled3"], G["sc"], l, pscale,
            dz.reshape(Bn, SEQ, ZW), GG["sc"])
        dz = dz3.reshape(T, ZW)
        dattn = _mm("dg_ya" + L, dya, G["sc"], grid=(nt, NCHIP), tb=True, k_axis=1, nk=NCHIP,
                    acc_shape=(tm, GROUP_W),
                    a_spec=spec((tm, 256), lambda i, k: (i, k)),
                    b_spec=spec((None, None, GROUP_W, 256), lambda i, k: (k, l, 0, 0)),
                    o_spec=spec((tm, GROUP_W), lambda i, k: (i, 0)), out_shape=f32o(GROUP_W))
        GG["sc"] = _mm("wg_ya" + L, sv["attn"], dya, grid=(NCHIP, ntk), ta=True, k_axis=1, nk=ntk,
                       acc_shape=(GROUP_W, 256),
                       a_spec=spec((tk, GROUP_W), lambda n, k: (k, 0)), b_spec=spec((tk, 256), lambda n, k: (k, n)),
                       o_spec=spec((None, None, GROUP_W, 256), lambda n, k: (n, l, 0, 0)),
                       out_shape=gg_shape["sc"], buf=GG["sc"])
        dos, deltas = _merge_bwd("merge_bwd" + L, sv["outs"], sv["lses"], dattn)
        for g, d in enumerate(DILATIONS):
            dz = _attn_bwd("attn_bwd%d_%d" % (g, l), sv["z"], tabs, g, d, dos[g], sv["lses"][g], deltas[g], dz)
        dh = _mm("dg_z" + L, dz, G["in"], grid=(nt, 12), tb=True, k_axis=1, nk=12, acc_shape=(tm, D),
                 a_spec=spec((tm, 640), lambda i, k: (i, k)),
                 b_spec=spec((None, None, D, 640), lambda i, k: (k // 3, l, 0, k % 3)),
                 o_spec=spec((tm, D), lambda i, k: (i, 0)), out_shape=f32o(D))
        GG["in"] = _mm("wg_z" + L, sv["h"], dz, grid=(12, ntk), ta=True, k_axis=1, nk=ntk, acc_shape=(D, 640),
                       a_spec=spec((tk, D), lambda n, k: (k, 0)), b_spec=spec((tk, 640), lambda n, k: (k, n)),
                       o_spec=spec((None, None, D, 640), lambda n, k: (n // 3, l, 0, n % 3)),
                       out_shape=gg_shape["in"], buf=GG["in"])
        dx, small["g_mix" + L] = _rms_bwd("rms_mix_bwd" + L, sv["x"], dh, g_mix, dx)

    return sq8, dx.reshape(Bn, SEQ, D), GG, small


_ANY = pl.BlockSpec(memory_space=pl.ANY)
_KEYS = ("in", "up", "sc", "r3", "dn")


def _place():
    x, y, c = lax.axis_index("x"), lax.axis_index("y"), lax.axis_index("c")
    chips = [(1 - x, y), (x, 1 - y), (1 - x, 1 - y)]
    return x, y, c, 2 * x + y, chips


def _gather_weights(shards, cw):
    n = len(shards)

    def body(*refs):
        s_refs, cw_ref, g_refs, cwg_ref = refs[:n], refs[n], refs[n + 1:2 * n + 1], refs[2 * n + 1]
        send_sems, recv_sems, loc_sems = refs[2 * n + 2:]
        x, y, c, me, chips = _place()

        def cw_copy(j, slot, chip):
            return pltpu.make_async_remote_copy(
                src_ref=cw_ref, dst_ref=cwg_ref.at[slot], send_sem=send_sems.at[6 * n + j],
                recv_sem=recv_sems.at[6 * n + j], device_id=(*chip, c), device_id_type=MESH)

        def half(k, cc):
            rh = shards[k].shape[1] // 2
            return pl.ds(cc * rh, rh)

        def copy(k, j, slot, cc, to):
            part = g_refs[k].at[slot, :, half(k, cc)]
            return pltpu.make_async_remote_copy(
                src_ref=part, dst_ref=part, send_sem=send_sems.at[6 * k + j], recv_sem=recv_sems.at[6 * k + j],
                device_id=to, device_id_type=MESH)

        local = [pltpu.make_async_copy(s_refs[k], g_refs[k].at[me], loc_sems.at[k]) for k in range(n)]
        local.append(pltpu.make_async_copy(cw_ref, cwg_ref.at[me], loc_sems.at[n]))
        for cp in local:
            cp.start()
        sent = []
        for k in range(n):
            for j, chip in enumerate(chips):
                cp = pltpu.make_async_remote_copy(
                    src_ref=s_refs[k].at[:, half(k, c)], dst_ref=g_refs[k].at[me, :, half(k, c)],
                    send_sem=send_sems.at[6 * k + j], recv_sem=recv_sems.at[6 * k + j],
                    device_id=(*chip, c), device_id_type=MESH)
                cp.start()
                sent.append(cp)
        for j, chip in enumerate(chips):
            cp = cw_copy(j, me, chip)
            cp.start()
            sent.append(cp)
        for k in range(n):
            for j, chip in enumerate(chips):
                slot = 2 * chip[0] + chip[1]
                copy(k, j, slot, c, (*chip, c)).wait_recv()
                fwd = copy(k, 3 + j, slot, c, (x, y, 1 - c))
                fwd.start()
                sent.append(fwd)
        for k in range(n):
            for j, chip in enumerate(chips):
                copy(k, 3 + j, 2 * chip[0] + chip[1], 1 - c, (x, y, 1 - c)).wait_recv()
        for j, chip in enumerate(chips):
            cw_copy(j, 2 * chip[0] + chip[1], chip).wait_recv()
        for cp in sent:
            cp.wait_send()
        for cp in local:
            cp.wait()

    outs = pl.pallas_call(
        body, name="gather_weights", in_specs=[_ANY] * (n + 1), out_specs=[_ANY] * (n + 1),
        out_shape=[_sds((NCHIP,) + s.shape, s.dtype) for s in shards] + [_sds((NCHIP,) + cw.shape, cw.dtype)],
        scratch_shapes=[pltpu.SemaphoreType.DMA((6 * n + 3,)), pltpu.SemaphoreType.DMA((6 * n + 3,)),
                        pltpu.SemaphoreType.DMA((n + 1,))],
    )(*shards, cw)
    return outs[:n], outs[n]


def _exchange_halves(grads, small):
    n = len(grads)

    def body(*refs):
        g_refs, small_ref = refs[:n], refs[n]
        r_refs, red_ref = refs[n + 1:2 * n + 1], refs[2 * n + 1]
        gath, send_sems, recv_sems, s_send, s_recv = refs[2 * n + 2:]
        x, y, c, me, chips = _place()
        dev = 4 * x + 2 * y + c
        gath[dev] = small_ref[...]
        sent = []
        for k in range(n):
            cp = pltpu.make_async_remote_copy(
                src_ref=g_refs[k].at[:, :, 1 - c], dst_ref=r_refs[k], send_sem=send_sems.at[k],
                recv_sem=recv_sems.at[k], device_id=(x, y, 1 - c), device_id_type=MESH)
            cp.start()
            sent.append(cp)
        for r in range(1, 8):
            rx, ry, rc = r >> 2, (r >> 1) & 1, r & 1
            peer = (x ^ rx, y ^ ry, c ^ rc)
            cp = pltpu.make_async_remote_copy(
                src_ref=small_ref, dst_ref=gath.at[dev], send_sem=s_send.at[r - 1], recv_sem=s_recv.at[r - 1],
                device_id=peer, device_id_type=MESH)
            cp.start()
            sent.append(cp)
        for r in range(1, 8):
            rx, ry, rc = r >> 2, (r >> 1) & 1, r & 1
            src = 4 * (x ^ rx) + 2 * (y ^ ry) + (c ^ rc)
            pltpu.make_async_remote_copy(
                src_ref=small_ref, dst_ref=gath.at[src], send_sem=s_send.at[r - 1], recv_sem=s_recv.at[r - 1],
                device_id=(x ^ rx, y ^ ry, c ^ rc), device_id_type=MESH).wait_recv()
        total = gath[0]
        for i in range(1, 8):
            total = total + gath[i]
        red_ref[...] = total
        for k in range(n):
            pltpu.make_async_remote_copy(
                src_ref=g_refs[k].at[:, :, 1 - c], dst_ref=r_refs[k], send_sem=send_sems.at[k],
                recv_sem=recv_sems.at[k], device_id=(x, y, 1 - c), device_id_type=MESH).wait_recv()
        for cp in sent:
            cp.wait_send()

    vm = pl.BlockSpec(memory_space=pltpu.VMEM)
    outs = pl.pallas_call(
        body, name="exchange_halves", in_specs=[_ANY] * n + [vm], out_specs=[_ANY] * n + [vm],
        out_shape=[_sds((g.shape[0], g.shape[1]) + g.shape[3:], F32) for g in grads] + [_sds(small.shape, F32)],
        scratch_shapes=[pltpu.VMEM((8,) + small.shape, F32), pltpu.SemaphoreType.DMA((n,)),
                        pltpu.SemaphoreType.DMA((n,)), pltpu.SemaphoreType.DMA((7,)), pltpu.SemaphoreType.DMA((7,))],
    )(*grads, small)
    return outs[:n], outs[n]


def _add_halves(name, g5, recv, c_arr):
    _, _, _, rh, cols = g5.shape
    tr = rh
    for cand in (512, 384, 352, 256, 128):
        if rh % cand == 0:
            tr = cand
            break

    def body(c_ref, g_ref, r_ref, o_ref):
        o_ref[...] = (g_ref[...] + r_ref[...]).astype(BF16)

    grid_spec = pltpu.PrefetchScalarGridSpec(
        num_scalar_prefetch=1, grid=(NCHIP, DEPTH, rh // tr),
        in_specs=[pl.BlockSpec((None, None, None, tr, cols), lambda j, l, i, c_ref: (j, l, c_ref[0], i, 0)),
                  pl.BlockSpec((None, None, tr, cols), lambda j, l, i, c_ref: (j, l, i, 0))],
        out_specs=pl.BlockSpec((None, None, tr, cols), lambda j, l, i, c_ref: (j, l, i, 0)))
    return pl.pallas_call(
        body, name=name, grid_spec=grid_spec, out_shape=_sds(recv.shape, BF16), compiler_params=_params(3),
    )(c_arr, g5, recv)


def _exchange_chips(parts):
    n = len(parts)

    def body(*refs):
        p_refs, r_refs = refs[:n], refs[n:2 * n]
        send_sems, recv_sems, loc_sems = refs[2 * n:]
        x, y, c, me, chips = _place()
        local = [pltpu.make_async_copy(p_refs[k].at[me], r_refs[k].at[me], loc_sems.at[k]) for k in range(n)]
        for cp in local:
            cp.start()
        sent = []
        for k in range(n):
            for j, chip in enumerate(chips):
                cp = pltpu.make_async_remote_copy(
                    src_ref=p_refs[k].at[2 * chip[0] + chip[1]], dst_ref=r_refs[k].at[me],
                    send_sem=send_sems.at[3 * k + j], recv_sem=recv_sems.at[3 * k + j],
                    device_id=(*chip, c), device_id_type=MESH)
                cp.start()
                sent.append(cp)
        for k in range(n):
            for j, chip in enumerate(chips):
                slot = r_refs[k].at[2 * chip[0] + chip[1]]
                pltpu.make_async_remote_copy(
                    src_ref=slot, dst_ref=slot, send_sem=send_sems.at[3 * k + j], recv_sem=recv_sems.at[3 * k + j],
                    device_id=(*chip, c), device_id_type=MESH).wait_recv()
        for cp in sent:
            cp.wait_send()
        for cp in local:
            cp.wait()

    return pl.pallas_call(
        body, name="exchange_chips", in_specs=[_ANY] * n, out_specs=[_ANY] * n,
        out_shape=[_sds(p.shape, BF16) for p in parts],
        scratch_shapes=[pltpu.SemaphoreType.DMA((3 * n,)), pltpu.SemaphoreType.DMA((3 * n,)),
                        pltpu.SemaphoreType.DMA((n,))],
    )(*parts)


def _sum_chips(name, recv):
    _, _, rh, cols = recv.shape
    tr = rh
    for cand in (512, 384, 352, 256, 128):
        if rh % cand == 0:
            tr = cand
            break

    def body(r_ref, o_ref):
        total = r_ref[0].astype(F32)
        for j in range(1, NCHIP):
            total = total + r_ref[j].astype(F32)
        o_ref[...] = total

    return pl.pallas_call(
        body, name=name, grid=(DEPTH, rh // tr),
        in_specs=[pl.BlockSpec((NCHIP, None, tr, cols), lambda l, i: (0, l, i, 0))],
        out_specs=pl.BlockSpec((None, tr, cols), lambda l, i: (l, i, 0)),
        out_shape=_sds(recv.shape[1:], F32), compiler_params=_params(2),
    )(recv)


def _share_halves(halves):
    n = len(halves)

    def body(*refs):
        h_refs, f_refs = refs[:n], refs[n:2 * n]
        send_sems, recv_sems, loc_sems = refs[2 * n:]
        x, y, c, me, chips = _place()
        local = [pltpu.make_async_copy(h_refs[k], f_refs[k].at[:, c], loc_sems.at[k]) for k in range(n)]
        for cp in local:
            cp.start()
        sent = []
        for k in range(n):
            cp = pltpu.make_async_remote_copy(
                src_ref=h_refs[k], dst_ref=f_refs[k].at[:, c], send_sem=send_sems.at[k], recv_sem=recv_sems.at[k],
                device_id=(x, y, 1 - c), device_id_type=MESH)
            cp.start()
            sent.append(cp)
        for k in range(n):
            pltpu.make_async_remote_copy(
                src_ref=h_refs[k], dst_ref=f_refs[k].at[:, 1 - c], send_sem=send_sems.at[k], recv_sem=recv_sems.at[k],
                device_id=(x, y, 1 - c), device_id_type=MESH).wait_recv()
        for cp in sent:
            cp.wait_send()
        for cp in local:
            cp.wait()

    return pl.pallas_call(
        body, name="share_halves", in_specs=[_ANY] * n, out_specs=[_ANY] * n,
        out_shape=[_sds((h.shape[0], 2) + h.shape[1:], F32) for h in halves],
        scratch_shapes=[pltpu.SemaphoreType.DMA((n,)), pltpu.SemaphoreType.DMA((n,)), pltpu.SemaphoreType.DMA((n,))],
    )(*halves)


def _adamw(name, w, g, m, v):
    shape = w.shape
    cols = shape[-1]
    rows = 1
    for s in shape[:-1]:
        rows *= s
    tr = rows
    for cand in (256, 128, 64):
        if rows > cand and rows % cand == 0:
            tr = cand
            break
    c1 = 1.0 / (1.0 - B1 ** STEP)
    c2 = 1.0 / (1.0 - B2 ** STEP)

    def body(w_ref, g_ref, m_ref, v_ref, d_ref, nm_ref, nv_ref):
        gv = g_ref[...]
        nm = B1 * m_ref[...] + (1.0 - B1) * gv
        nv = B2 * v_ref[...] + (1.0 - B2) * (gv * gv)
        nm_ref[...] = nm
        nv_ref[...] = nv
        d_ref[...] = -LR * ((nm * c1) / (jnp.sqrt(nv * c2) + ADAM_EPS) + WD * w_ref[...])

    blk = pl.BlockSpec((tr, cols), lambda i: (i, 0))
    outs = pl.pallas_call(
        body, name=name, grid=(rows // tr,), in_specs=[blk] * 4, out_specs=[blk] * 3,
        out_shape=[_sds((rows, cols), F32)] * 3, compiler_params=_params(1),
    )(*(t.reshape(rows, cols) for t in (w, g, m, v)))
    return tuple(o.reshape(shape) for o in outs)


def _pack_small(small):
    rows = [jnp.sum(small["g_mix%d" % l], axis=0, keepdims=True) for l in range(DEPTH)]
    rows += [jnp.sum(small["pool_scale%d" % l], axis=0, keepdims=True) for l in range(DEPTH)]
    rows += [jnp.sum(small["g_ffn%d" % l], axis=0, keepdims=True) for l in range(DEPTH)]
    rows += [jnp.sum(small["g_ple%d" % l], axis=0, keepdims=True) for l in range(DEPTH)]
    rows += [jnp.sum(small["g_final"], axis=0, keepdims=True)]
    flat = [small["conv_b%d" % l].reshape(-1) for l in range(DEPTH)]
    flat += [small["conv_w%d" % l].reshape(-1) for l in range(DEPTH)]
    flat = jnp.concatenate(flat).reshape(-1, D)
    packed = jnp.concatenate(rows + [flat], axis=0)
    return jnp.pad(packed, ((0, SMALL_ROWS - packed.shape[0]), (0, 0)))


def _unpack_small(red):
    g_mix, pool_scale, g_ffn, g_ple = red[0:2], red[2:4], red[4:6], red[6:8]
    g_final = red[8]
    nb = DEPTH * UW // D
    conv_b = red[9:9 + nb].reshape(DEPTH, UW)
    conv_w = red[9 + nb:9 + 4 * nb].reshape(DEPTH, 3, UW)
    return g_mix, pool_scale, g_ffn, g_ple, g_final, conv_b, conv_w


def kernel(x, p, g_mix, w_in, w_ya, w_yb, pool_w, pool_scale, w_o, g_ffn, w_up, conv_w, conv_b, w_down, g_ple, w_ple, w_ple_gate, g_final, loss_target, m_g_mix, m_w_in, m_w_ya, m_w_yb, m_pool_w, m_pool_scale, m_w_o, m_g_ffn, m_w_up, m_conv_w, m_conv_b, m_w_down, m_g_ple, m_w_ple, m_w_ple_gate, m_g_final, v_g_mix, v_w_in, v_w_ya, v_w_yb, v_pool_w, v_pool_scale, v_w_o, v_g_ffn, v_w_up, v_conv_w, v_conv_b, v_w_down, v_g_ple, v_w_ple, v_w_ple_gate, v_g_final):
    me = 2 * lax.axis_index("x") + lax.axis_index("y")
    c_arr = lax.axis_index("c").astype(jnp.int32).reshape(1)

    shards = [
        w_in.astype(BF16), w_up.astype(BF16),
        jnp.concatenate([w_ya, w_ple, pool_w.reshape(DEPTH, 256, 256)], axis=1).astype(BF16),
        jnp.concatenate([w_yb, w_o, w_ple_gate], axis=1).astype(BF16),
        w_down.astype(BF16),
    ]
    gathered, cw4 = _gather_weights(shards, conv_w.reshape(DEPTH * 3, UP_S))
    G = dict(zip(_KEYS, gathered))
    cw_full = cw4.reshape(NCHIP, DEPTH, 3, UP_S).transpose(1, 2, 0, 3).reshape(DEPTH, 3, UW)

    vecs = dict(g_mix=g_mix, pool_scale=pool_scale, g_ffn=g_ffn, g_ple=g_ple, g_final=g_final, conv_b=conv_b,
                conv_w=cw_full)
    sq8, grad_x, GG, small = _local_step(x, p, loss_target, vecs, G)
    loss = lax.psum(jnp.sum(sq8) * (0.5 / D), ("x", "y", "c"))

    g5 = [GG[k].reshape(GG[k].shape[:2] + (2, GG[k].shape[2] // 2, GG[k].shape[3])) for k in _KEYS]
    recv1, small_red = _exchange_halves(g5, _pack_small(small))
    parts = [_add_halves("add_halves_" + k, g, r, c_arr) for k, g, r in zip(_KEYS, g5, recv1)]
    recv2 = _exchange_chips(parts)
    halves = [_sum_chips("sum_chips_" + k, r) for k, r in zip(_KEYS, recv2)]
    full = [f.reshape(DEPTH, -1, f.shape[-1]) for f in _share_halves(halves)]
    r_in, r_up, r_sc, r_r3, r_dn = full
    d_g_mix, d_pool_scale, d_g_ffn, d_g_ple, d_g_final, d_conv_b, d_conv_w_full = _unpack_small(small_red)
    d_conv_w = lax.dynamic_slice_in_dim(d_conv_w_full, me * UP_S, UP_S, axis=2)

    grads = dict(
        g_mix=d_g_mix, w_in=r_in, w_ya=r_sc[:, 0:512], w_yb=r_r3[:, 0:256],
        pool_w=r_sc[:, 768:1024].reshape(DEPTH, 4, 64, 256), pool_scale=d_pool_scale, w_o=r_r3[:, 256:512],
        g_ffn=d_g_ffn, w_up=r_up, conv_w=d_conv_w, conv_b=d_conv_b, w_down=r_dn, g_ple=d_g_ple,
        w_ple=r_sc[:, 512:768], w_ple_gate=r_r3[:, 512:768], g_final=d_g_final)
    weights = dict(g_mix=g_mix, w_in=w_in, w_ya=w_ya, w_yb=w_yb, pool_w=pool_w, pool_scale=pool_scale, w_o=w_o,
                   g_ffn=g_ffn, w_up=w_up, conv_w=conv_w, conv_b=conv_b, w_down=w_down, g_ple=g_ple, w_ple=w_ple,
                   w_ple_gate=w_ple_gate, g_final=g_final)
    m_in = dict(g_mix=m_g_mix, w_in=m_w_in, w_ya=m_w_ya, w_yb=m_w_yb, pool_w=m_pool_w, pool_scale=m_pool_scale,
                w_o=m_w_o, g_ffn=m_g_ffn, w_up=m_w_up, conv_w=m_conv_w, conv_b=m_conv_b, w_down=m_w_down,
                g_ple=m_g_ple, w_ple=m_w_ple, w_ple_gate=m_w_ple_gate, g_final=m_g_final)
    v_in = dict(g_mix=v_g_mix, w_in=v_w_in, w_ya=v_w_ya, w_yb=v_w_yb, pool_w=v_pool_w, pool_scale=v_pool_scale,
                w_o=v_w_o, g_ffn=v_g_ffn, w_up=v_w_up, conv_w=v_conv_w, conv_b=v_conv_b, w_down=v_w_down,
                g_ple=v_g_ple, w_ple=v_w_ple, w_ple_gate=v_w_ple_gate, g_final=v_g_final)
    names = ["g_mix", "w_in", "w_ya", "w_yb", "pool_w", "pool_scale", "w_o", "g_ffn", "w_up", "conv_w", "conv_b",
             "w_down", "g_ple", "w_ple", "w_ple_gate", "g_final"]
    deltas, new_m, new_v = [], [], []
    for nme in names:
        gr = grads[nme].reshape(weights[nme].shape)
        grads[nme] = gr
        dlt, nm, nv = _adamw("adamw_" + nme, weights[nme], gr, m_in[nme], v_in[nme])
        deltas.append(dlt)
        new_m.append(nm)
        new_v.append(nv)
    return (loss, grad_x, *[grads[nme] for nme in names], *deltas, *new_m, *new_v)
```

```python
import functools
import math

import jax
import jax.numpy as jnp
from jax import lax
from jax.experimental import pallas as pl
from jax.experimental.pallas import tpu as pltpu

F32 = jnp.float32
BF16 = jnp.bfloat16
MESH = pl.DeviceIdType.MESH

D = 1024
SEQ = 2048
DEPTH = 2
HEAD = 128
GROUP_W = 512
DILATIONS = (1, 4, 16)
ROPE_DIM = 32
ROPE_THETA = 500000.0
NEG_INF = -1e30
ZW = 7680
OFF_K, OFF_V, OFF_U, OFF_GA, OFF_GB = 1536, 3072, 4608, 5632, 6656
FF = 2816
UW = 2 * FF
PLE = 256
NCHIP = 4
IN_S, UP_S, DN_S = ZW // NCHIP, UW // NCHIP, FF // NCHIP
RMS_EPS = 1e-6
LR, B1, B2, ADAM_EPS, WD, STEP = 0.001, 0.9, 0.999, 1e-08, 0.01, 10
SMALL_ROWS = 56
VMEM_CAP = 48 * 1024 * 1024


def _params(n_grid, vmem=VMEM_CAP):
    return pltpu.CompilerParams(dimension_semantics=("arbitrary",) * n_grid, vmem_limit_bytes=vmem)


def _sigmoid(v):
    return 1.0 / (1.0 + jnp.exp(-v))


def _rows8(v):
    return jnp.sum(v.reshape(v.shape[0] // 8, 8, v.shape[1]), axis=0)


def _mm(name, a, b, *, grid, a_spec, b_spec, o_spec, out_shape, ta=False, tb=False, k_axis=None, nk=1,
        acc_shape=None, res=None, res_spec=None, buf=None):
    dims = (((0,) if ta else (1,), (1,) if tb else (0,)), ((), ()))
    has_res, has_buf = res is not None, buf is not None

    def body(*refs):
        a_ref, b_ref = refs[0], refs[1]
        pos = 2
        r_ref = None
        if has_res:
            r_ref = refs[pos]
            pos += 1
        if has_buf:
            pos += 1
        o_ref = refs[pos]
        av = a_ref[...]
        av = av.reshape(-1, av.shape[-1]).astype(BF16)
        bv = b_ref[...]
        bv = bv.reshape(-1, bv.shape[-1]).astype(BF16)
        part = lax.dot_general(av, bv, dims, preferred_element_type=F32)

        def finish(val):
            if r_ref is not None:
                val = val + r_ref[...]
            o_ref[...] = val.reshape(o_ref.shape).astype(o_ref.dtype)

        if nk == 1:
            finish(part)
        else:
            acc_ref = refs[pos + 1]
            k = pl.program_id(k_axis)

            @pl.when(k == 0)
            def _():
                acc_ref[...] = part

            @pl.when(k > 0)
            def _():
                acc_ref[...] += part

            @pl.when(k == nk - 1)
            def _():
                finish(acc_ref[...])

    ins, in_specs = [a, b], [a_spec, b_spec]
    if has_res:
        ins.append(res)
        in_specs.append(res_spec)
    aliases = {}
    if has_buf:
        aliases = {len(ins): 0}
        ins.append(buf)
        in_specs.append(pl.BlockSpec(memory_space=pl.ANY))
    scratch = [pltpu.VMEM(acc_shape, F32)] if nk > 1 else []
    return pl.pallas_call(
        body, name=name, grid=grid, in_specs=in_specs, out_specs=o_spec, out_shape=out_shape,
        scratch_shapes=scratch, input_output_aliases=aliases, compiler_params=_params(len(grid)),
    )(*ins)


def _sds(shape, dtype):
    return jax.ShapeDtypeStruct(shape, dtype)


def _rms_fwd(name, x, g, tr=512):
    T = x.shape[0]

    def body(x_ref, g_ref, h_ref):
        xv = x_ref[...]
        r = lax.rsqrt(jnp.mean(xv * xv, axis=-1, keepdims=True) + RMS_EPS)
        h_ref[...] = (xv * r * g_ref[...]).astype(BF16)

    return pl.pallas_call(
        body, name=name, grid=(T // tr,),
        in_specs=[pl.BlockSpec((tr, D), lambda i: (i, 0)), pl.BlockSpec((1, D), lambda i: (0, 0))],
        out_specs=pl.BlockSpec((tr, D), lambda i: (i, 0)), out_shape=_sds((T, D), BF16),
        compiler_params=_params(1),
    )(x, g)


def _rms_bwd(name, x, dh, g, dres, tr=512):
    T = x.shape[0]

    def body(x_ref, dh_ref, g_ref, dres_ref, dx_ref, dg_ref):
        xv = x_ref[...]
        r = lax.rsqrt(jnp.mean(xv * xv, axis=-1, keepdims=True) + RMS_EPS)
        xh = xv * r
        dhv = dh_ref[...]
        part = _rows8(dhv * xh)

        @pl.when(pl.program_id(0) == 0)
        def _():
            dg_ref[...] = part

        @pl.when(pl.program_id(0) > 0)
        def _():
            dg_ref[...] += part

        dxh = dhv * g_ref[...]
        dx_ref[...] = dres_ref[...] + r * (dxh - xh * jnp.mean(dxh * xh, axis=-1, keepdims=True))

    row = pl.BlockSpec((tr, D), lambda i: (i, 0))
    return pl.pallas_call(
        body, name=name, grid=(T // tr,),
        in_specs=[row, row, pl.BlockSpec((1, D), lambda i: (0, 0)), row],
        out_specs=[row, pl.BlockSpec((8, D), lambda i: (0, 0))],
        out_shape=[_sds((T, D), F32), _sds((8, D), F32)],
        compiler_params=_params(1),
    )(x, dh, g, dres)


def _final_loss(x, g, tgt, tr=512):
    T = x.shape[0]

    def body(x_ref, g_ref, t_ref, dx_ref, dg_ref, sq_ref):
        xv = x_ref[...]
        r = lax.rsqrt(jnp.mean(xv * xv, axis=-1, keepdims=True) + RMS_EPS)
        xh = xv * r
        gv = g_ref[...]
        e = xh * gv - t_ref[...]
        dy = e * (1.0 / D)
        pg = _rows8(dy * xh)
        ps = _rows8(e * e)

        @pl.when(pl.program_id(0) == 0)
        def _():
            dg_ref[...] = pg
            sq_ref[...] = ps

        @pl.when(pl.program_id(0) > 0)
        def _():
            dg_ref[...] += pg
            sq_ref[...] += ps

        dxh = dy * gv
        dx_ref[...] = r * (dxh - xh * jnp.mean(dxh * xh, axis=-1, keepdims=True))

    row = pl.BlockSpec((tr, D), lambda i: (i, 0))
    acc = pl.BlockSpec((8, D), lambda i: (0, 0))
    return pl.pallas_call(
        body, name="final_loss", grid=(T // tr,),
        in_specs=[row, pl.BlockSpec((1, D), lambda i: (0, 0)), row],
        out_specs=[row, acc, acc],
        out_shape=[_sds((T, D), F32), _sds((8, D), F32), _sds((8, D), F32)],
        compiler_params=_params(1),
    )(x, g, tgt)


def _ple_fwd(name, x, pe, pg, tr=512):
    T = x.shape[0]

    def body(x_ref, pe_ref, pg_ref, o_ref):
        o_ref[...] = x_ref[...] + pe_ref[...] * _sigmoid(pg_ref[...])

    row = pl.BlockSpec((tr, D), lambda i: (i, 0))
    return pl.pallas_call(
        body, name=name, grid=(T // tr,), in_specs=[row, row, row], out_specs=row,
        out_shape=_sds((T, D), F32), compiler_params=_params(1),
    )(x, pe, pg)


def _ple_bwd(name, dx, pe, pg, tr=512):
    T = dx.shape[0]

    def body(dx_ref, pe_ref, pg_ref, dpe_ref, dpg_ref):
        s = _sigmoid(pg_ref[...])
        dxv = dx_ref[...]
        dpe_ref[...] = (dxv * s).astype(BF16)
        dpg_ref[...] = (dxv * pe_ref[...] * s * (1.0 - s)).astype(BF16)

    row = pl.BlockSpec((tr, D), lambda i: (i, 0))
    return pl.pallas_call(
        body, name=name, grid=(T // tr,), in_specs=[row, row, row], out_specs=[row, row],
        out_shape=[_sds((T, D), BF16), _sds((T, D), BF16)], compiler_params=_params(1),
    )(dx, pe, pg)


def _gate_fwd(name, z, ya, yb, tr=512):
    T = z.shape[0]
    w = 512

    def body(ga_ref, gb_ref, ya_ref, yb_ref, o_ref):
        o_ref[...] = (_sigmoid(ga_ref[...]) * ya_ref[...] + _sigmoid(gb_ref[...]) * yb_ref[...]).astype(BF16)

    col = pl.BlockSpec((tr, w), lambda i, j: (i, j))
    return pl.pallas_call(
        body, name=name, grid=(T // tr, D // w),
        in_specs=[pl.BlockSpec((tr, w), lambda i, j: (i, OFF_GA // w + j)),
                  pl.BlockSpec((tr, w), lambda i, j: (i, OFF_GB // w + j)), col, col],
        out_specs=col, out_shape=_sds((T, D), BF16), compiler_params=_params(2),
    )(z, z, ya, yb)


def _gate_bwd(name, z, off, y, dm, dz, tr=512):
    T = z.shape[0]
    w = 512
    has_dz = dz is not None

    def body(*refs):
        g_ref, y_ref, dm_ref = refs[:3]
        dy_ref, dz_ref = refs[-2:]
        s = _sigmoid(g_ref[...])
        dmv = dm_ref[...]
        dy_ref[...] = (dmv * s).astype(BF16)
        dz_ref[...] = (dmv * y_ref[...] * s * (1.0 - s)).astype(BF16)

    col = pl.BlockSpec((tr, w), lambda i, j: (i, j))
    gcol = pl.BlockSpec((tr, w), lambda i, j: (i, off // w + j))
    ins, in_specs, aliases = [z, y, dm], [gcol, col, col], {}
    if has_dz:
        ins.append(dz)
        in_specs.append(pl.BlockSpec(memory_space=pl.ANY))
        aliases = {3: 1}
    return pl.pallas_call(
        body, name=name, grid=(T // tr, D // w), in_specs=in_specs, out_specs=[col, gcol],
        out_shape=[_sds((T, D), BF16), _sds((T, ZW), BF16)], input_output_aliases=aliases,
        compiler_params=_params(2),
    )(*ins)


def _shift_down(v, k, rows):
    return jnp.where(rows >= k, pltpu.roll(v, k, 0), 0.0)


def _shift_up(v, k, rows):
    n = v.shape[0]
    return jnp.where(rows < n - k, pltpu.roll(v, n - k, 0), 0.0)


def _pool_window(v, g, rows, shift):
    s2 = v + shift(v, 1, rows)
    s4 = s2 + shift(s2, 2, rows)
    s8 = s4 + shift(s4, 4, rows)
    s16 = s8 + shift(s8, 8, rows)
    return jnp.where(g == 0, s2, jnp.where(g == 1, s4, jnp.where(g == 2, s8, s16)))


def _pool_count(g, rows):
    wlen = jnp.left_shift(2, g).astype(F32)
    return jnp.minimum(rows.astype(F32) + 1.0, wlen)


def _pool_fwd(name, z3, g_sc, layer, scale):
    Bn = z3.shape[0]
    gw = 256

    def body(u_ref, pw_ref, sc_ref, pooled_ref, ms_ref):
        g = pl.program_id(1)
        u = u_ref[...]
        rows = lax.broadcasted_iota(jnp.int32, u.shape, 0)
        pooled = (_pool_window(u, g, rows, _shift_down) / _pool_count(g, rows) - u).astype(BF16)
        pooled_ref[...] = pooled
        pw = pw_ref[...].reshape(gw, gw)
        mixed = jnp.dot(pooled, pw, preferred_element_type=F32)
        ms_ref[...] = (mixed * sc_ref[...]).astype(BF16)

    blk = pl.BlockSpec((None, SEQ, gw), lambda b, g: (b, 0, g))
    return pl.pallas_call(
        body, name=name, grid=(Bn, 4),
        in_specs=[pl.BlockSpec((None, SEQ, gw), lambda b, g: (b, 0, OFF_U // gw + g)),
                  pl.BlockSpec((NCHIP, None, 64, gw), lambda b, g: (0, layer, 12 + g, 0)),
                  pl.BlockSpec((1, gw), lambda b, g: (0, g))],
        out_specs=[blk, blk],
        out_shape=[_sds((Bn, SEQ, D), BF16), _sds((Bn, SEQ, D), BF16)],
        compiler_params=_params(2),
    )(z3, g_sc, scale)


def _pool_bwd(name, dms3, pooled3, g_sc, layer, scale, dz3, gg_sc):
    Bn = dms3.shape[0]
    gw = 256
    has_gg = gg_sc is not None

    def body(*refs):
        dms_ref, pooled_ref, pw_ref, sc_ref = refs[:4]
        dz_ref, dpw_ref, dsc_ref = refs[-3:]
        g, b = pl.program_id(0), pl.program_id(1)
        pooled = pooled_ref[...]
        pw = pw_ref[...].reshape(gw, gw)
        dms = dms_ref[...]
        mixed = jnp.dot(pooled, pw, preferred_element_type=F32)
        psc = _rows8(dms * mixed)
        dmixed = (dms * sc_ref[...]).astype(BF16)
        dpw = lax.dot_general(pooled, dmixed, (((0,), (0,)), ((), ())), preferred_element_type=F32)
        dpw = dpw.reshape(NCHIP, 64, gw)

        @pl.when(b == 0)
        def _():
            dsc_ref[...] = psc
            dpw_ref[...] = dpw

        @pl.when(b > 0)
        def _():
            dsc_ref[...] += psc
            dpw_ref[...] += dpw

        dpooled = lax.dot_general(dmixed, pw, (((1,), (1,)), ((), ())), preferred_element_type=F32)
        rows = lax.broadcasted_iota(jnp.int32, dpooled.shape, 0)
        dq = dpooled / _pool_count(g, rows)
        dz_ref[...] = (_pool_window(dq, g, rows, _shift_up) - dpooled).astype(BF16)

    ins = [dms3, pooled3, g_sc, scale, dz3]
    in_specs = [pl.BlockSpec((None, SEQ, gw), lambda g, b: (b, 0, g)),
                pl.BlockSpec((None, SEQ, gw), lambda g, b: (b, 0, g)),
                pl.BlockSpec((NCHIP, None, 64, gw), lambda g, b: (0, layer, 12 + g, 0)),
                pl.BlockSpec((1, gw), lambda g, b: (0, g)),
                pl.BlockSpec(memory_space=pl.ANY)]
    aliases = {4: 0}
    if has_gg:
        ins.append(gg_sc)
        in_specs.append(pl.BlockSpec(memory_space=pl.ANY))
        aliases[5] = 1
    return pl.pallas_call(
        body, name=name, grid=(4, Bn), in_specs=in_specs,
        out_specs=[pl.BlockSpec((None, SEQ, gw), lambda g, b: (b, 0, OFF_U // gw + g)),
                   pl.BlockSpec((NCHIP, None, 64, gw), lambda g, b: (0, layer, 12 + g, 0)),
                   pl.BlockSpec((8, gw), lambda g, b: (0, g))],
        out_shape=[_sds(dz3.shape, BF16), _sds((NCHIP, DEPTH, D, 256), F32), _sds((8, D), F32)],
        input_output_aliases=aliases, compiler_params=_params(2),
    )(*ins)


CT = 256
NCT = FF // CT


def _conv_pre(u, cw_ref, cb_ref, rows):
    return (cb_ref[...] + cw_ref[0:1, :] * _shift_down(u, 2, rows) + cw_ref[1:2, :] * _shift_down(u, 1, rows)
            + cw_ref[2:3, :] * u)


def _conv_fwd(name, u3, cw, cb):
    Bn = u3.shape[0]

    def body(ug_ref, uv_ref, cwg_ref, cwv_ref, cbg_ref, cbv_ref, a_ref):
        ug, uv = ug_ref[...], uv_ref[...]
        rows = lax.broadcasted_iota(jnp.int32, ug.shape, 0)
        yg = _conv_pre(ug, cwg_ref, cbg_ref, rows)
        yv = _conv_pre(uv, cwv_ref, cbv_ref, rows)
        a_ref[...] = (yg * _sigmoid(yg) * yv).astype(BF16)

    def blk(off):
        return pl.BlockSpec((None, SEQ, CT), lambda b, c: (b, 0, off + c))

    return pl.pallas_call(
        body, name=name, grid=(Bn, NCT),
        in_specs=[blk(0), blk(NCT),
                  pl.BlockSpec((3, CT), lambda b, c: (0, c)), pl.BlockSpec((3, CT), lambda b, c: (0, NCT + c)),
                  pl.BlockSpec((1, CT), lambda b, c: (0, c)), pl.BlockSpec((1, CT), lambda b, c: (0, NCT + c))],
        out_specs=blk(0), out_shape=_sds((Bn, SEQ, FF), BF16), compiler_params=_params(2),
    )(u3, u3, cw, cw, cb, cb)


def _conv_bwd(name, da3, u3, cw, cb):
    Bn = u3.shape[0]

    def body(da_ref, ug_ref, uv_ref, cwg_ref, cwv_ref, cbg_ref, cbv_ref, du_ref, dcw_ref, dcb_ref):
        c, b = pl.program_id(0), pl.program_id(1)
        ug, uv, da = ug_ref[...], uv_ref[...], da_ref[...]
        rows = lax.broadcasted_iota(jnp.int32, ug.shape, 0)
        yg = _conv_pre(ug, cwg_ref, cbg_ref, rows)
        s = _sigmoid(yg)

        def emit(dy, u, cw_ref):
            du = (cw_ref[2:3, :] * dy + cw_ref[1:2, :] * _shift_up(dy, 1, rows)
                  + cw_ref[0:1, :] * _shift_up(dy, 2, rows))
            du_ref[...] = du.astype(BF16)
            dcw = jnp.concatenate(
                [jnp.sum(dy * _shift_down(u, 2, rows), axis=0, keepdims=True),
                 jnp.sum(dy * _shift_down(u, 1, rows), axis=0, keepdims=True),
                 jnp.sum(dy * u, axis=0, keepdims=True)], axis=0)
            dcb = jnp.sum(dy, axis=0, keepdims=True)

            @pl.when(b == 0)
            def _():
                dcw_ref[...] = dcw
                dcb_ref[...] = dcb

            @pl.when(b > 0)
            def _():
                dcw_ref[...] += dcw
                dcb_ref[...] += dcb

        @pl.when(c < NCT)
        def _():
            yv = _conv_pre(uv, cwv_ref, cbv_ref, rows)
            emit(da * yv * (s * (1.0 + yg * (1.0 - s))), ug, cwg_ref)

        @pl.when(c >= NCT)
        def _():
            emit(da * (yg * s), uv, cwv_ref)

    def blk(fn):
        return pl.BlockSpec((None, SEQ, CT), lambda c, b: (b, 0, fn(c)))

    return pl.pallas_call(
        body, name=name, grid=(2 * NCT, Bn),
        in_specs=[blk(lambda c: c % NCT), blk(lambda c: c % NCT), blk(lambda c: NCT + c % NCT),
                  pl.BlockSpec((3, CT), lambda c, b: (0, c % NCT)),
                  pl.BlockSpec((3, CT), lambda c, b: (0, NCT + c % NCT)),
                  pl.BlockSpec((1, CT), lambda c, b: (0, c % NCT)),
                  pl.BlockSpec((1, CT), lambda c, b: (0, NCT + c % NCT))],
        out_specs=[blk(lambda c: c), pl.BlockSpec((3, CT), lambda c, b: (0, c)),
                   pl.BlockSpec((1, CT), lambda c, b: (0, c))],
        out_shape=[_sds((Bn, SEQ, UW), BF16), _sds((3, UW), F32), _sds((1, UW), F32)],
        compiler_params=_params(2),
    )(da3, u3, u3, cw, cw, cb, cb)


def _rope_tables():
    pos = jnp.arange(SEQ, dtype=F32)
    inv_freq = jnp.exp(jnp.arange(0, ROPE_DIM, 2, dtype=F32) * (-math.log(ROPE_THETA) / ROPE_DIM))
    ang = pos[:, None] * inv_freq[None, :]
    cos, sin = jnp.cos(ang), jnp.sin(ang)
    half = ROPE_DIM // 2
    zeros = jnp.zeros((SEQ, HEAD - ROPE_DIM), F32)
    zh = jnp.zeros((SEQ, half), F32)
    tab_c = jnp.concatenate([cos, cos, zeros + 1.0], axis=1)
    tab_a = jnp.concatenate([-sin, zh, zeros], axis=1)
    tab_b = jnp.concatenate([zh, sin, zeros], axis=1)
    return tab_c, tab_a, tab_b


def _rot(v, tc, ta, tb):
    half = ROPE_DIM // 2
    return v * tc + pltpu.roll(v, HEAD - half, 1) * ta + pltpu.roll(v, half, 1) * tb


def _rot_t(dv, tc, ta, tb):
    half = ROPE_DIM // 2
    return dv * tc + pltpu.roll(dv * ta, half, 1) + pltpu.roll(dv * tb, HEAD - half, 1)


def _band_masks():
    qi = lax.broadcasted_iota(jnp.int32, (HEAD, 2 * HEAD), 0)
    ki = lax.broadcasted_iota(jnp.int32, (HEAD, 2 * HEAD), 1)
    diff = HEAD + qi - ki
    both = (diff >= 0) & (diff <= HEAD)
    q1 = lax.broadcasted_iota(jnp.int32, (HEAD, HEAD), 0)
    k1 = lax.broadcasted_iota(jnp.int32, (HEAD, HEAD), 1)
    return q1 >= k1, both


_NT = (((1,), (1,)), ((), ()))
_TN = (((0,), (0,)), ((), ()))
_SCALE = HEAD ** -0.5


ATT_W = HEAD
ATT_HP = ATT_W // HEAD


def _res_rows(r, n, d, base=0):
    return pl.ds(base * d + r, n, stride=d) if d > 1 else pl.ds(base, n)


def _attn_load(q_ref, k_ref, v_ref, tc_ref, ta_ref, tb_ref, qs, ks, vs, d):
    L = SEQ // d
    for r in range(d):
        rows = _res_rows(r, L, d)
        tc, ta, tb = tc_ref[rows, :], ta_ref[rows, :], tb_ref[rows, :]
        dst = slice(r * L, (r + 1) * L)
        for hh in range(ATT_HP):
            sl = slice(hh * HEAD, (hh + 1) * HEAD)
            qs[dst, sl] = _rot(q_ref[rows, sl], tc, ta, tb).astype(BF16)
            ks[dst, sl] = _rot(k_ref[rows, sl], tc, ta, tb).astype(BF16)
            vs[dst, sl] = v_ref[rows, sl].astype(BF16)


def _attn_fwd(name, z3, tabs, g, d):
    Bn = z3.shape[0]
    L = SEQ // d
    nb = L // HEAD
    W, nh = ATT_W, GROUP_W // ATT_W

    def body(q_ref, k_ref, v_ref, tc_ref, ta_ref, tb_ref, o_ref, l_ref, qs, ks, vs):
        m_first, m_both = _band_masks()
        _attn_load(q_ref, k_ref, v_ref, tc_ref, ta_ref, tb_ref, qs, ks, vs, d)
        for r in range(d):
            for hh in range(ATT_HP):
                sl = slice(hh * HEAD, (hh + 1) * HEAD)
                for n in range(nb):
                    rq = slice(r * L + n * HEAD, r * L + (n + 1) * HEAD)
                    rk = slice(r * L + max(n - 1, 0) * HEAD, r * L + (n + 1) * HEAD)
                    s = lax.dot_general(qs[rq, sl], ks[rk, sl], _NT, preferred_element_type=F32) * _SCALE
                    s = jnp.where(m_first if n == 0 else m_both, s, NEG_INF)
                    m = jnp.max(s, axis=-1, keepdims=True)
                    e = jnp.exp(s - m)
                    den = jnp.sum(e, axis=-1, keepdims=True)
                    p = (e * (1.0 / den)).astype(BF16)
                    rows = _res_rows(r, HEAD, d, n * HEAD)
                    o_ref[rows, sl] = jnp.dot(p, vs[rk, sl], preferred_element_type=F32)
                    l_ref[rows, sl] = jnp.broadcast_to(m + jnp.log(den), (HEAD, HEAD))

    def zcol(off):
        return pl.BlockSpec((None, SEQ, W), lambda b, h: (b, 0, (off + g * GROUP_W) // W + h))

    tab = pl.BlockSpec((SEQ, HEAD), lambda b, h: (0, 0))
    out = pl.BlockSpec((None, SEQ, W), lambda b, h: (b, 0, h))
    return pl.pallas_call(
        body, name=name, grid=(Bn, nh),
        in_specs=[zcol(0), zcol(OFF_K), zcol(OFF_V), tab, tab, tab],
        out_specs=[out, out],
        out_shape=[_sds((Bn, SEQ, GROUP_W), F32), _sds((Bn, SEQ, GROUP_W), F32)],
        scratch_shapes=[pltpu.VMEM((SEQ, W), BF16)] * 3,
        compiler_params=_params(2),
    )(z3, z3, z3, *tabs)


def _attn_bwd(name, z3, tabs, g, d, do3, lse3, delta3, dz3):
    Bn = z3.shape[0]
    L = SEQ // d
    nb = L // HEAD
    W, nh = ATT_W, GROUP_W // ATT_W

    def body(q_ref, k_ref, v_ref, tc_ref, ta_ref, tb_ref, do_ref, l_ref, dl_ref, dz_in, dz_ref,
             qs, ks, vs, dos, dqs, dks, dvs, nat, oq, ok, ov, sems):
        b, h = pl.program_id(0), pl.program_id(1)
        m_first, m_both = _band_masks()
        _attn_load(q_ref, k_ref, v_ref, tc_ref, ta_ref, tb_ref, qs, ks, vs, d)
        for r in range(d):
            dos[r * L:(r + 1) * L, :] = do_ref[_res_rows(r, L, d), :].astype(BF16)
        dks[...] = jnp.zeros_like(dks)
        dvs[...] = jnp.zeros_like(dvs)
        for r in range(d):
            for hh in range(ATT_HP):
                sl = slice(hh * HEAD, (hh + 1) * HEAD)
                for n in range(nb):
                    rq = slice(r * L + n * HEAD, r * L + (n + 1) * HEAD)
                    rk = slice(r * L + max(n - 1, 0) * HEAD, r * L + (n + 1) * HEAD)
                    rows = _res_rows(r, HEAD, d, n * HEAD)
                    qb, kk, vv, dob = qs[rq, sl], ks[rk, sl], vs[rk, sl], dos[rq, sl]
                    s = lax.dot_general(qb, kk, _NT, preferred_element_type=F32) * _SCALE
                    s = jnp.where(m_first if n == 0 else m_both, s, NEG_INF)
                    p = jnp.exp(s - l_ref[rows, sl][:, 0:1])
                    dp = lax.dot_general(dob, vv, _NT, preferred_element_type=F32)
                    ds = (p * (dp - dl_ref[rows, sl][:, 0:1]) * _SCALE).astype(BF16)
                    dqs[rq, sl] = jnp.dot(ds, kk, preferred_element_type=F32)
                    dks[rk, sl] += lax.dot_general(ds, qb, _TN, preferred_element_type=F32)
                    dvs[rk, sl] += lax.dot_general(p.astype(BF16), dob, _TN, preferred_element_type=F32)
        tc, ta, tb = tc_ref[...], ta_ref[...], tb_ref[...]
        for src, dst, rotate in ((dqs, oq, True), (dks, ok, True), (dvs, ov, False)):
            for r in range(d):
                nat[_res_rows(r, L, d), :] = src[r * L:(r + 1) * L, :]
            for hh in range(ATT_HP):
                sl = slice(hh * HEAD, (hh + 1) * HEAD)
                val = nat[:, sl]
                dst[:, sl] = (_rot_t(val, tc, ta, tb) if rotate else val).astype(BF16)
        base = g * GROUP_W + h * W
        copies = []
        for i, (src, off) in enumerate(((oq, 0), (ok, OFF_K), (ov, OFF_V))):
            col = pl.multiple_of(base + off, HEAD)
            cp = pltpu.make_async_copy(src, dz_ref.at[b, :, pl.ds(col, W)], sems.at[i])
            cp.start()
            copies.append(cp)
        for cp in copies:
            cp.wait()

    def zcol(off):
        return pl.BlockSpec((None, SEQ, W), lambda b, h: (b, 0, (off + g * GROUP_W) // W + h))

    tab = pl.BlockSpec((SEQ, HEAD), lambda b, h: (0, 0))
    gcol = pl.BlockSpec((None, SEQ, W), lambda b, h: (b, 0, h))
    any_spec = pl.BlockSpec(memory_space=pl.ANY)
    return pl.pallas_call(
        body, name=name, grid=(Bn, nh),
        in_specs=[zcol(0), zcol(OFF_K), zcol(OFF_V), tab, tab, tab, gcol, gcol, gcol, any_spec],
        out_specs=any_spec,
        out_shape=_sds((Bn, SEQ, ZW), BF16),
        scratch_shapes=[pltpu.VMEM((SEQ, W), BF16)] * 4 + [pltpu.VMEM((SEQ, W), F32)] * 4
        + [pltpu.VMEM((SEQ, W), BF16)] * 3 + [pltpu.SemaphoreType.DMA((3,))],
        input_output_aliases={9: 0}, compiler_params=_params(2),
    )(z3, z3, z3, *tabs, do3, lse3, delta3, dz3)


def _merge_weights(l0, l1, l2):
    m = jnp.maximum(jnp.maximum(l0, l1), l2)
    e0, e1, e2 = jnp.exp(l0 - m), jnp.exp(l1 - m), jnp.exp(l2 - m)
    inv = 1.0 / (e0 + e1 + e2)
    return e0 * inv, e1 * inv, e2 * inv


def _merge_fwd(name, outs, lses, tr=512):
    T = outs[0].shape[0]

    def body(o0, o1, o2, l0, l1, l2, a_ref):
        w0, w1, w2 = _merge_weights(l0[...], l1[...], l2[...])
        a_ref[...] = (w0 * o0[...] + w1 * o1[...] + w2 * o2[...]).astype(BF16)

    row = pl.BlockSpec((tr, GROUP_W), lambda i: (i, 0))
    return pl.pallas_call(
        body, name=name, grid=(T // tr,), in_specs=[row] * 6, out_specs=row,
        out_shape=_sds((T, GROUP_W), BF16), compiler_params=_params(1),
    )(*outs, *lses)


def _merge_bwd(name, outs, lses, dattn, tr=512):
    T = outs[0].shape[0]

    def body(o0, o1, o2, l0, l1, l2, da_ref, d0, d1, d2, e0, e1, e2):
        w = _merge_weights(l0[...], l1[...], l2[...])
        da = da_ref[...]
        attn = w[0] * o0[...] + w[1] * o1[...] + w[2] * o2[...]
        prod = da * attn
        csum = jnp.concatenate(
            [jnp.broadcast_to(jnp.sum(prod[:, hh * HEAD:(hh + 1) * HEAD], axis=-1, keepdims=True), (tr, HEAD))
             for hh in range(GROUP_W // HEAD)], axis=1)
        for wg, d_ref, e_ref in zip(w, (d0, d1, d2), (e0, e1, e2)):
            d_ref[...] = wg * da
            e_ref[...] = wg * csum

    row = pl.BlockSpec((tr, GROUP_W), lambda i: (i, 0))
    res = pl.pallas_call(
        body, name=name, grid=(T // tr,), in_specs=[row] * 7, out_specs=[row] * 6,
        out_shape=[_sds((T, GROUP_W), F32)] * 6,
        compiler_params=_params(1),
    )(*outs, *lses, dattn)
    return res[:3], res[3:]


def _local_step(x3, p4, tgt3, vecs, G):
    Bn = x3.shape[0]
    T = Bn * SEQ
    x = x3.reshape(T, D)
    tgt = tgt3.reshape(T, D)
    pb = p4.astype(BF16).reshape(DEPTH, T, PLE)
    tabs = _rope_tables()
    tm = 1024 if T % 1024 == 0 else 512
    nt = T // tm
    tk = 1024 if T % 1024 == 0 else 512
    ntk = T // tk
    tm5 = 512
    f32o = lambda n: _sds((T, n), F32)

    def spec(shape, fn):
        return pl.BlockSpec(shape, fn)

    saved = []
    for l in range(DEPTH):
        L = str(l)
        g_mix, g_ffn, g_ple = (vecs[k][l:l + 1] for k in ("g_mix", "g_ffn", "g_ple"))
        pscale, cb, cw = vecs["pool_scale"][l:l + 1], vecs["conv_b"][l:l + 1], vecs["conv_w"][l]
        h = _rms_fwd("rms_mix" + L, x, g_mix)
        z = _mm("mm_z" + L, h, G["in"], grid=(nt, 12),
                a_spec=spec((tm, D), lambda i, n: (i, 0)),
                b_spec=spec((None, None, D, 640), lambda i, n: (n // 3, l, 0, n % 3)),
                o_spec=spec((tm, 640), lambda i, n: (i, n)), out_shape=f32o(ZW))
        z3 = z.reshape(Bn, SEQ, ZW)
        outs, lses = [], []
        for g, d in enumerate(DILATIONS):
            o_g, l_g = _attn_fwd("attn_fwd%d_%d" % (g, l), z3, tabs, g, d)
            outs.append(o_g.reshape(T, GROUP_W))
            lses.append(l_g.reshape(T, GROUP_W))
        attn = _merge_fwd("merge_fwd" + L, outs, lses)
        ya = _mm("mm_ya" + L, attn, G["sc"], grid=(nt, NCHIP),
                 a_spec=spec((tm, GROUP_W), lambda i, n: (i, 0)),
                 b_spec=spec((None, None, GROUP_W, 256), lambda i, n: (n, l, 0, 0)),
                 o_spec=spec((tm, 256), lambda i, n: (i, n)), out_shape=f32o(D))
        pooled3, ms3 = _pool_fwd("pool_fwd" + L, z3, G["sc"], l, pscale)
        ms = ms3.reshape(T, D)

        def row_sharded(name, a, rb, res=None, kdim=D):
            if rb is None:
                b_arr, b_spec = G["dn"], spec((NCHIP, None, DN_S, 512), lambda i, n: (0, l, 0, n))
            else:
                b_arr, b_spec = G["r3"], spec((NCHIP, None, 256, 512), lambda i, n: (0, l, rb, n))
            return _mm(name, a, b_arr, grid=(T // tm5, 2),
                       a_spec=spec((tm5, kdim), lambda i, n: (i, 0)), b_spec=b_spec,
                       o_spec=spec((tm5, 512), lambda i, n: (i, n)), out_shape=f32o(D),
                       res=res, res_spec=None if res is None else spec((tm5, 512), lambda i, n: (i, n)))

        yb = row_sharded("mm_yb" + L, ms, 0)
        merged = _gate_fwd("gate_fwd" + L, z, ya, yb)
        x1 = row_sharded("mm_o" + L, merged, 1, res=x)
        h2 = _rms_fwd("rms_ffn" + L, x1, g_ffn)
        u = _mm("mm_up" + L, h2, G["up"], grid=(T // tm5, NCHIP),
                a_spec=spec((tm5, D), lambda i, n: (i, 0)),
                b_spec=spec((None, None, D, UP_S), lambda i, n: (n, l, 0, 0)),
                o_spec=spec((tm5, UP_S), lambda i, n: (i, n)), out_shape=f32o(UW))
        u3 = u.reshape(Bn, SEQ, UW)
        act = _conv_fwd("conv_fwd" + L, u3, cw, cb).reshape(T, FF)
        x2 = row_sharded("mm_down" + L, act, None, res=x1, kdim=FF)
        h3 = _rms_fwd("rms_ple" + L, x2, g_ple)
        pg = row_sharded("mm_pg" + L, h3, 2)
        pe = _mm("mm_pe" + L, pb[l], G["sc"], grid=(nt, NCHIP),
                 a_spec=spec((tm, PLE), lambda i, n: (i, 0)),
                 b_spec=spec((None, None, 256, 256), lambda i, n: (n, l, 2, 0)),
                 o_spec=spec((tm, 256), lambda i, n: (i, n)), out_shape=f32o(D))
        x3n = _ple_fwd("ple_fwd" + L, x2, pe, pg)
        saved.append(dict(x=x, h=h, z=z, outs=outs, lses=lses, attn=attn, ya=ya, yb=yb, pooled3=pooled3, ms=ms,
                          merged=merged, x1=x1, h2=h2, u3=u3, act=act, x2=x2, h3=h3, pg=pg, pe=pe))
        x = x3n

    dx, dg_final8, sq8 = _final_loss(x, vecs["g_final"].reshape(1, D), tgt)

    GG = dict.fromkeys(("in", "up", "sc", "r3", "dn"))
    gg_shape = {k: _sds(G[k].shape, F32) for k in G}
    small = {"g_final": dg_final8}

    for l in reversed(range(DEPTH)):
        L = str(l)
        sv = saved[l]
        g_mix, g_ffn, g_ple = (vecs[k][l:l + 1] for k in ("g_mix", "g_ffn", "g_ple"))
        pscale, cb, cw = vecs["pool_scale"][l:l + 1], vecs["conv_b"][l:l + 1], vecs["conv_w"][l]

        def wgrad_rows(name, a, b_arr, key, rb, a_cols=256):
            GG[key] = _mm(name, a, b_arr, grid=(NCHIP, 2, ntk), ta=True, k_axis=2, nk=ntk, acc_shape=(256, 512),
                          a_spec=spec((tk, 256), lambda m, n, k: (k, m)),
                          b_spec=spec((tk, 512), lambda m, n, k: (k, n)),
                          o_spec=spec((None, None, 256, 512), lambda m, n, k: (m, l, rb, n)),
                          out_shape=gg_shape[key], buf=GG[key])

        def dgrad_rows(name, dy, rb):
            return _mm(name, dy, G["r3"], grid=(T // tm5, NCHIP), tb=True,
                       a_spec=spec((tm5, D), lambda i, n: (i, 0)),
                       b_spec=spec((None, None, 256, D), lambda i, n: (n, l, rb, 0)),
                       o_spec=spec((tm5, 256), lambda i, n: (i, n)), out_shape=f32o(D))

        dpe, dpg = _ple_bwd("ple_bwd" + L, dx, sv["pe"], sv["pg"])
        GG["sc"] = _mm("wg_ple" + L, pb[l], dpe, grid=(NCHIP, ntk), ta=True, k_axis=1, nk=ntk, acc_shape=(256, 256),
                       a_spec=spec((tk, PLE), lambda n, k: (k, 0)), b_spec=spec((tk, 256), lambda n, k: (k, n)),
                       o_spec=spec((None, None, 256, 256), lambda n, k: (n, l, 2, 0)),
                       out_shape=gg_shape["sc"], buf=GG["sc"])
        wgrad_rows("wg_pg" + L, sv["h3"], dpg, "r3", 2)
        dh3 = dgrad_rows("dg_pg" + L, dpg, 2)
        dx, small["g_ple" + L] = _rms_bwd("rms_ple_bwd" + L, sv["x2"], dh3, g_ple, dx)

        da = _mm("dg_down" + L, dx, G["dn"], grid=(T // 256,), tb=True,
                 a_spec=spec((256, D), lambda i: (i, 0)),
                 b_spec=spec((NCHIP, None, DN_S, D), lambda i: (0, l, 0, 0)),
                 o_spec=spec((256, FF), lambda i: (i, 0)), out_shape=f32o(FF))
        GG["dn"] = _mm("wg_down" + L, sv["act"], dx, grid=(2, T // 512), ta=True, k_axis=1, nk=T // 512,
                       acc_shape=(FF, 512),
                       a_spec=spec((512, FF), lambda n, k: (k, 0)), b_spec=spec((512, 512), lambda n, k: (k, n)),
                       o_spec=spec((NCHIP, None, DN_S, 512), lambda n, k: (0, l, 0, n)),
                       out_shape=gg_shape["dn"], buf=GG["dn"])
        du3, dcw, dcb = _conv_bwd("conv_bwd" + L, da.reshape(Bn, SEQ, FF), sv["u3"], cw, cb)
        small["conv_w" + L], small["conv_b" + L] = dcw, dcb
        du = du3.reshape(T, UW)
        dh2 = _mm("dg_up" + L, du, G["up"], grid=(T // tm5, NCHIP), tb=True, k_axis=1, nk=NCHIP, acc_shape=(tm5, D),
                  a_spec=spec((tm5, UP_S), lambda i, k: (i, k)),
                  b_spec=spec((None, None, D, UP_S), lambda i, k: (k, l, 0, 0)),
                  o_spec=spec((tm5, D), lambda i, k: (i, 0)), out_shape=f32o(D))
        GG["up"] = _mm("wg_up" + L, sv["h2"], du, grid=(2, NCHIP, T // 512), ta=True, k_axis=2, nk=T // 512,
                       acc_shape=(512, UP_S),
                       a_spec=spec((512, 512), lambda m, j, k: (k, m)),
                       b_spec=spec((512, UP_S), lambda m, j, k: (k, j)),
                       o_spec=spec((None, None, 512, UP_S), lambda m, j, k: (j, l, m, 0)),
                       out_shape=gg_shape["up"], buf=GG["up"])
        dx, small["g_ffn" + L] = _rms_bwd("rms_ffn_bwd" + L, sv["x1"], dh2, g_ffn, dx)

        dmerged = dgrad_rows("dg_o" + L, dx, 1)
        wgrad_rows("wg_o" + L, sv["merged"], dx, "r3", 1)
        dya, dz = _gate_bwd("gate_bwd_a" + L, sv["z"], OFF_GA, sv["ya"], dmerged, None)
        dyb, dz = _gate_bwd("gate_bwd_b" + L, sv["z"], OFF_GB, sv["yb"], dmerged, dz)
        dms = dgrad_rows("dg_yb" + L, dyb, 0)
        wgrad_rows("wg_yb" + L, sv["ms"], dyb, "r3", 0)
        dz3, GG["sc"], small["pool_scale" + L] = _pool_bwd(
            "pool_bwd" + L, dms.reshape(Bn, SEQ, D), sv["pooled3"], G["sc"], l, pscale,
            dz.reshape(Bn, SEQ, ZW), GG["sc"])
        dattn = _mm("dg_ya" + L, dya, G["sc"], grid=(nt, NCHIP), tb=True, k_axis=1, nk=NCHIP,
                    acc_shape=(tm, GROUP_W),
                    a_spec=spec((tm, 256), lambda i, k: (i, k)),
                    b_spec=spec((None, None, GROUP_W, 256), lambda i, k: (k, l, 0, 0)),
                    o_spec=spec((tm, GROUP_W), lambda i, k: (i, 0)), out_shape=f32o(GROUP_W))
        GG["sc"] = _mm("wg_ya" + L, sv["attn"], dya, grid=(NCHIP, ntk), ta=True, k_axis=1, nk=ntk,
                       acc_shape=(GROUP_W, 256),
                       a_spec=spec((tk, GROUP_W), lambda n, k: (k, 0)), b_spec=spec((tk, 256), lambda n, k: (k, n)),
                       o_spec=spec((None, None, GROUP_W, 256), lambda n, k: (n, l, 0, 0)),
                       out_shape=gg_shape["sc"], buf=GG["sc"])
        dos, deltas = _merge_bwd("merge_bwd" + L, sv["outs"], sv["lses"], dattn)
        view3 = lambda t: t.reshape(Bn, SEQ, GROUP_W)
        sz3 = sv["z"].reshape(Bn, SEQ, ZW)
        for g, d in enumerate(DILATIONS):
            dz3 = _attn_bwd("attn_bwd%d_%d" % (g, l), sz3, tabs, g, d, view3(dos[g]), view3(sv["lses"][g]),
                            view3(deltas[g]), dz3)
        dz = dz3.reshape(T, ZW)
        dh = _mm("dg_z" + L, dz, G["in"], grid=(nt, 12), tb=True, k_axis=1, nk=12, acc_shape=(tm, D),
                 a_spec=spec((tm, 640), lambda i, k: (i, k)),
                 b_spec=spec((None, None, D, 640), lambda i, k: (k // 3, l, 0, k % 3)),
                 o_spec=spec((tm, D), lambda i, k: (i, 0)), out_shape=f32o(D))
        GG["in"] = _mm("wg_z" + L, sv["h"], dz, grid=(12, ntk), ta=True, k_axis=1, nk=ntk, acc_shape=(D, 640),
                       a_spec=spec((tk, D), lambda n, k: (k, 0)), b_spec=spec((tk, 640), lambda n, k: (k, n)),
                       o_spec=spec((None, None, D, 640), lambda n, k: (n // 3, l, 0, n % 3)),
                       out_shape=gg_shape["in"], buf=GG["in"])
        dx, small["g_mix" + L] = _rms_bwd("rms_mix_bwd" + L, sv["x"], dh, g_mix, dx)

    return sq8, dx.reshape(Bn, SEQ, D), GG, small


_ANY = pl.BlockSpec(memory_space=pl.ANY)
_KEYS = ("in", "up", "sc", "r3", "dn")


def _place():
    x, y, c = lax.axis_index("x"), lax.axis_index("y"), lax.axis_index("c")
    chips = [(1 - x, y), (x, 1 - y), (1 - x, 1 - y)]
    return x, y, c, 2 * x + y, chips


def _gather_weights(stacks, cw4):
    n = len(stacks)

    def body(*refs):
        g_refs, cwg_ref = refs[n + 1:2 * n + 1], refs[2 * n + 1]
        send_sems, recv_sems = refs[2 * n + 2:]
        x, y, c, me, chips = _place()

        def cw_copy(j, slot, chip):
            part = cwg_ref.at[slot]
            return pltpu.make_async_remote_copy(
                src_ref=part, dst_ref=part, send_sem=send_sems.at[6 * n + j],
                recv_sem=recv_sems.at[6 * n + j], device_id=(*chip, c), device_id_type=MESH)

        def half(k, cc):
            rh = stacks[k].shape[2] // 2
            return pl.ds(cc * rh, rh)

        def copy(k, j, slot, cc, to):
            part = g_refs[k].at[slot, :, half(k, cc)]
            return pltpu.make_async_remote_copy(
                src_ref=part, dst_ref=part, send_sem=send_sems.at[6 * k + j], recv_sem=recv_sems.at[6 * k + j],
                device_id=to, device_id_type=MESH)

        sent = []
        for k in range(n):
            for j, chip in enumerate(chips):
                cp = copy(k, j, me, c, (*chip, c))
                cp.start()
                sent.append(cp)
        for j, chip in enumerate(chips):
            cp = cw_copy(j, me, chip)
            cp.start()
            sent.append(cp)
        for k in range(n):
            for j, chip in enumerate(chips):
                slot = 2 * chip[0] + chip[1]
                copy(k, j, slot, c, (*chip, c)).wait_recv()
                fwd = copy(k, 3 + j, slot, c, (x, y, 1 - c))
                fwd.start()
                sent.append(fwd)
        for k in range(n):
            for j, chip in enumerate(chips):
                copy(k, 3 + j, 2 * chip[0] + chip[1], 1 - c, (x, y, 1 - c)).wait_recv()
        for j, chip in enumerate(chips):
            cw_copy(j, 2 * chip[0] + chip[1], chip).wait_recv()
        for cp in sent:
            cp.wait_send()

    outs = pl.pallas_call(
        body, name="gather_weights", in_specs=[_ANY] * (n + 1), out_specs=[_ANY] * (n + 1),
        out_shape=[_sds(s.shape, s.dtype) for s in stacks] + [_sds(cw4.shape, cw4.dtype)],
        scratch_shapes=[pltpu.SemaphoreType.DMA((6 * n + 3,)), pltpu.SemaphoreType.DMA((6 * n + 3,))],
        input_output_aliases={i: i for i in range(n + 1)},
    )(*stacks, cw4)
    return outs[:n], outs[n]


def _exchange_halves(grads, small):
    n = len(grads)

    def body(*refs):
        g_refs, small_ref = refs[:n], refs[n]
        r_refs, red_ref = refs[n + 1:2 * n + 1], refs[2 * n + 1]
        gath, send_sems, recv_sems, s_send, s_recv = refs[2 * n + 2:]
        x, y, c, me, chips = _place()
        dev = 4 * x + 2 * y + c
        gath[dev] = small_ref[...]
        sent = []
        for k in range(n):
            cp = pltpu.make_async_remote_copy(
                src_ref=g_refs[k].at[:, :, 1 - c], dst_ref=r_refs[k], send_sem=send_sems.at[k],
                recv_sem=recv_sems.at[k], device_id=(x, y, 1 - c), device_id_type=MESH)
            cp.start()
            sent.append(cp)
        for r in range(1, 8):
            rx, ry, rc = r >> 2, (r >> 1) & 1, r & 1
            peer = (x ^ rx, y ^ ry, c ^ rc)
            cp = pltpu.make_async_remote_copy(
                src_ref=small_ref, dst_ref=gath.at[dev], send_sem=s_send.at[r - 1], recv_sem=s_recv.at[r - 1],
                device_id=peer, device_id_type=MESH)
            cp.start()
            sent.append(cp)
        for r in range(1, 8):
            rx, ry, rc = r >> 2, (r >> 1) & 1, r & 1
            src = 4 * (x ^ rx) + 2 * (y ^ ry) + (c ^ rc)
            pltpu.make_async_remote_copy(
                src_ref=small_ref, dst_ref=gath.at[src], send_sem=s_send.at[r - 1], recv_sem=s_recv.at[r - 1],
                device_id=(x ^ rx, y ^ ry, c ^ rc), device_id_type=MESH).wait_recv()
        total = gath[0]
        for i in range(1, 8):
            total = total + gath[i]
        red_ref[...] = total
        for k in range(n):
            pltpu.make_async_remote_copy(
                src_ref=g_refs[k].at[:, :, 1 - c], dst_ref=r_refs[k], send_sem=send_sems.at[k],
                recv_sem=recv_sems.at[k], device_id=(x, y, 1 - c), device_id_type=MESH).wait_recv()
        for cp in sent:
            cp.wait_send()

    vm = pl.BlockSpec(memory_space=pltpu.VMEM)
    outs = pl.pallas_call(
        body, name="exchange_halves", in_specs=[_ANY] * n + [vm], out_specs=[_ANY] * n + [vm],
        out_shape=[_sds((g.shape[0], g.shape[1]) + g.shape[3:], F32) for g in grads] + [_sds(small.shape, F32)],
        scratch_shapes=[pltpu.VMEM((8,) + small.shape, F32), pltpu.SemaphoreType.DMA((n,)),
                        pltpu.SemaphoreType.DMA((n,)), pltpu.SemaphoreType.DMA((7,)), pltpu.SemaphoreType.DMA((7,))],
    )(*grads, small)
    return outs[:n], outs[n]


def _row_tile(rh):
    for cand in (512, 384, 352, 256, 128):
        if rh % cand == 0:
            return cand
    return rh


def _add_halves(name, g5, recv, place):
    _, _, _, rh, cols = g5.shape
    tr = _row_tile(rh)

    def body(place_ref, g_ref, r_ref, o_ref, own_ref):
        val = (g_ref[...] + r_ref[...]).astype(BF16)
        o_ref[...] = val

        @pl.when(pl.program_id(2) == place_ref[1])
        def _():
            own_ref[...] = val

    grid_spec = pltpu.PrefetchScalarGridSpec(
        num_scalar_prefetch=1, grid=(DEPTH, rh // tr, NCHIP),
        in_specs=[pl.BlockSpec((None, None, None, tr, cols), lambda l, i, j, pr: (j, l, pr[0], i, 0)),
                  pl.BlockSpec((None, None, tr, cols), lambda l, i, j, pr: (j, l, i, 0))],
        out_specs=[pl.BlockSpec((None, None, tr, cols), lambda l, i, j, pr: (j, l, i, 0)),
                   pl.BlockSpec((None, None, tr, cols), lambda l, i, j, pr: (pr[1], l, i, 0))])
    return pl.pallas_call(
        body, name=name, grid_spec=grid_spec, out_shape=[_sds(recv.shape, BF16)] * 2, compiler_params=_params(3),
    )(place, g5, recv)


def _exchange_chips(parts, landing):
    n = len(parts)

    def body(*refs):
        p_refs, r_refs = refs[:n], refs[3 * n:4 * n]
        send_sems, recv_sems = refs[4 * n:]
        x, y, c, me, chips = _place()
        sent = []
        for k in range(n):
            for j, chip in enumerate(chips):
                cp = pltpu.make_async_remote_copy(
                    src_ref=p_refs[k].at[2 * chip[0] + chip[1]], dst_ref=r_refs[k].at[me],
                    send_sem=send_sems.at[3 * k + j], recv_sem=recv_sems.at[3 * k + j],
                    device_id=(*chip, c), device_id_type=MESH)
                cp.start()
                sent.append(cp)
        for k in range(n):
            for j, chip in enumerate(chips):
                slot = r_refs[k].at[2 * chip[0] + chip[1]]
                pltpu.make_async_remote_copy(
                    src_ref=slot, dst_ref=slot, send_sem=send_sems.at[3 * k + j], recv_sem=recv_sems.at[3 * k + j],
                    device_id=(*chip, c), device_id_type=MESH).wait_recv()
        for cp in sent:
            cp.wait_send()

    outs = pl.pallas_call(
        body, name="exchange_chips", in_specs=[_ANY] * (2 * n), out_specs=[_ANY] * (2 * n),
        out_shape=[_sds(p.shape, BF16) for p in parts] * 2,
        scratch_shapes=[pltpu.SemaphoreType.DMA((3 * n,)), pltpu.SemaphoreType.DMA((3 * n,))],
        input_output_aliases={i: i for i in range(2 * n)},
    )(*parts, *landing)
    return outs[n:]


def _sum_chips(name, recv, place):
    _, _, rh, cols = recv.shape
    tr = _row_tile(rh)

    def body(place_ref, r_ref, o_ref):
        total = r_ref[0].astype(F32)
        for j in range(1, NCHIP):
            total = total + r_ref[j].astype(F32)
        o_ref[...] = total

    grid_spec = pltpu.PrefetchScalarGridSpec(
        num_scalar_prefetch=1, grid=(DEPTH, rh // tr),
        in_specs=[pl.BlockSpec((NCHIP, None, tr, cols), lambda l, i, pr: (0, l, i, 0))],
        out_specs=pl.BlockSpec((None, None, tr, cols), lambda l, i, pr: (l, pr[0], i, 0)))
    return pl.pallas_call(
        body, name=name, grid_spec=grid_spec, out_shape=_sds((DEPTH, 2, rh, cols), F32), compiler_params=_params(2),
    )(place, recv)


def _share_halves(full):
    n = len(full)

    def body(*refs):
        f_refs = refs[n:2 * n]
        send_sems, recv_sems = refs[2 * n:]
        x, y, c, me, chips = _place()

        def copy(k, cc):
            part = f_refs[k].at[:, cc]
            return pltpu.make_async_remote_copy(
                src_ref=part, dst_ref=part, send_sem=send_sems.at[k], recv_sem=recv_sems.at[k],
                device_id=(x, y, 1 - c), device_id_type=MESH)

        sent = [copy(k, c) for k in range(n)]
        for cp in sent:
            cp.start()
        for k in range(n):
            copy(k, 1 - c).wait_recv()
        for cp in sent:
            cp.wait_send()

    return pl.pallas_call(
        body, name="share_halves", in_specs=[_ANY] * n, out_specs=[_ANY] * n,
        out_shape=[_sds(f.shape, F32) for f in full],
        scratch_shapes=[pltpu.SemaphoreType.DMA((n,)), pltpu.SemaphoreType.DMA((n,))],
        input_output_aliases={i: i for i in range(n)},
    )(*full)


def _adamw(name, w, g, m, v):
    shape = w.shape
    cols = shape[-1]
    rows = 1
    for s in shape[:-1]:
        rows *= s
    tr = rows
    for cand in (256, 128, 64):
        if rows > cand and rows % cand == 0:
            tr = cand
            break
    c1 = 1.0 / (1.0 - B1 ** STEP)
    c2 = 1.0 / (1.0 - B2 ** STEP)

    def body(w_ref, g_ref, m_ref, v_ref, d_ref, nm_ref, nv_ref):
        gv = g_ref[...]
        nm = B1 * m_ref[...] + (1.0 - B1) * gv
        nv = B2 * v_ref[...] + (1.0 - B2) * (gv * gv)
        nm_ref[...] = nm
        nv_ref[...] = nv
        d_ref[...] = -LR * ((nm * c1) / (jnp.sqrt(nv * c2) + ADAM_EPS) + WD * w_ref[...])

    blk = pl.BlockSpec((tr, cols), lambda i: (i, 0))
    outs = pl.pallas_call(
        body, name=name, grid=(rows // tr,), in_specs=[blk] * 4, out_specs=[blk] * 3,
        out_shape=[_sds((rows, cols), F32)] * 3, compiler_params=_params(1),
    )(*(t.reshape(rows, cols) for t in (w, g, m, v)))
    return tuple(o.reshape(shape) for o in outs)


def _pack_small(small):
    rows = [jnp.sum(small["g_mix%d" % l], axis=0, keepdims=True) for l in range(DEPTH)]
    rows += [jnp.sum(small["pool_scale%d" % l], axis=0, keepdims=True) for l in range(DEPTH)]
    rows += [jnp.sum(small["g_ffn%d" % l], axis=0, keepdims=True) for l in range(DEPTH)]
    rows += [jnp.sum(small["g_ple%d" % l], axis=0, keepdims=True) for l in range(DEPTH)]
    rows += [jnp.sum(small["g_final"], axis=0, keepdims=True)]
    flat = [small["conv_b%d" % l].reshape(-1) for l in range(DEPTH)]
    flat += [small["conv_w%d" % l].reshape(-1) for l in range(DEPTH)]
    flat = jnp.concatenate(flat).reshape(-1, D)
    packed = jnp.concatenate(rows + [flat], axis=0)
    return jnp.pad(packed, ((0, SMALL_ROWS - packed.shape[0]), (0, 0)))


def _unpack_small(red):
    g_mix, pool_scale, g_ffn, g_ple = red[0:2], red[2:4], red[4:6], red[6:8]
    g_final = red[8]
    nb = DEPTH * UW // D
    conv_b = red[9:9 + nb].reshape(DEPTH, UW)
    conv_w = red[9 + nb:9 + 4 * nb].reshape(DEPTH, 3, UW)
    return g_mix, pool_scale, g_ffn, g_ple, g_final, conv_b, conv_w


def kernel(x, p, g_mix, w_in, w_ya, w_yb, pool_w, pool_scale, w_o, g_ffn, w_up, conv_w, conv_b, w_down, g_ple, w_ple, w_ple_gate, g_final, loss_target, m_g_mix, m_w_in, m_w_ya, m_w_yb, m_pool_w, m_pool_scale, m_w_o, m_g_ffn, m_w_up, m_conv_w, m_conv_b, m_w_down, m_g_ple, m_w_ple, m_w_ple_gate, m_g_final, v_g_mix, v_w_in, v_w_ya, v_w_yb, v_pool_w, v_pool_scale, v_w_o, v_g_ffn, v_w_up, v_conv_w, v_conv_b, v_w_down, v_g_ple, v_w_ple, v_w_ple_gate, v_g_final):
    me = 2 * lax.axis_index("x") + lax.axis_index("y")
    place = jnp.stack([lax.axis_index("c"), me]).astype(jnp.int32)

    shards = [
        w_in.astype(BF16), w_up.astype(BF16),
        jnp.concatenate([w_ya, w_ple, pool_w.reshape(DEPTH, 256, 256)], axis=1).astype(BF16),
        jnp.concatenate([w_yb, w_o, w_ple_gate], axis=1).astype(BF16),
        w_down.astype(BF16),
        conv_w.reshape(DEPTH * 3, UP_S),
    ]
    slotted = [lax.dynamic_update_index_in_dim(lax.empty((NCHIP,) + s.shape, s.dtype), s, me, 0) for s in shards]
    gathered, cw4 = _gather_weights(slotted[:5], slotted[5])
    G = dict(zip(_KEYS, gathered))
    cw_full = cw4.reshape(NCHIP, DEPTH, 3, UP_S).transpose(1, 2, 0, 3).reshape(DEPTH, 3, UW)

    vecs = dict(g_mix=g_mix, pool_scale=pool_scale, g_ffn=g_ffn, g_ple=g_ple, g_final=g_final, conv_b=conv_b,
                conv_w=cw_full)
    sq8, grad_x, GG, small = _local_step(x, p, loss_target, vecs, G)
    loss = lax.psum(jnp.sum(sq8) * (0.5 / D), ("x", "y", "c"))

    g5 = [GG[k].reshape(GG[k].shape[:2] + (2, GG[k].shape[2] // 2, GG[k].shape[3])) for k in _KEYS]
    recv1, small_red = _exchange_halves(g5, _pack_small(small))
    summed = [_add_halves("add_halves_" + k, g, r, place) for k, g, r in zip(_KEYS, g5, recv1)]
    recv2 = _exchange_chips([s[0] for s in summed], [s[1] for s in summed])
    halves = [_sum_chips("sum_chips_" + k, r, place) for k, r in zip(_KEYS, recv2)]
    full = [f.reshape(DEPTH, -1, f.shape[-1]) for f in _share_halves(halves)]
    r_in, r_up, r_sc, r_r3, r_dn = full
    d_g_mix, d_pool_scale, d_g_ffn, d_g_ple, d_g_final, d_conv_b, d_conv_w_full = _unpack_small(small_red)
    d_conv_w = lax.dynamic_slice_in_dim(d_conv_w_full, me * UP_S, UP_S, axis=2)

    grads = dict(
        g_mix=d_g_mix, w_in=r_in, w_ya=r_sc[:, 0:512], w_yb=r_r3[:, 0:256],
        pool_w=r_sc[:, 768:1024].reshape(DEPTH, 4, 64, 256), pool_scale=d_pool_scale, w_o=r_r3[:, 256:512],
        g_ffn=d_g_ffn, w_up=r_up, conv_w=d_conv_w, conv_b=d_conv_b, w_down=r_dn, g_ple=d_g_ple,
        w_ple=r_sc[:, 512:768], w_ple_gate=r_r3[:, 512:768], g_final=d_g_final)
    weights = dict(g_mix=g_mix, w_in=w_in, w_ya=w_ya, w_yb=w_yb, pool_w=pool_w, pool_scale=pool_scale, w_o=w_o,
                   g_ffn=g_ffn, w_up=w_up, conv_w=conv_w, conv_b=conv_b, w_down=w_down, g_ple=g_ple, w_ple=w_ple,
                   w_ple_gate=w_ple_gate, g_final=g_final)
    m_in = dict(g_mix=m_g_mix, w_in=m_w_in, w_ya=m_w_ya, w_yb=m_w_yb, pool_w=m_pool_w, pool_scale=m_pool_scale,
                w_o=m_w_o, g_ffn=m_g_ffn, w_up=m_w_up, conv_w=m_conv_w, conv_b=m_conv_b, w_down=m_w_down,
                g_ple=m_g_ple, w_ple=m_w_ple, w_ple_gate=m_w_ple_gate, g_final=m_g_final)
    v_in = dict(g_mix=v_g_mix, w_in=v_w_in, w_ya=v_w_ya, w_yb=v_w_yb, pool_w=v_pool_w, pool_scale=v_pool_scale,
                w_o=v_w_o, g_ffn=v_g_ffn, w_up=v_w_up, conv_w=v_conv_w, conv_b=v_conv_b, w_down=v_w_down,
                g_ple=v_g_ple, w_ple=v_w_ple, w_ple_gate=v_w_ple_gate, g_final=v_g_final)
    names = ["g_mix", "w_in", "w_ya", "w_yb", "pool_w", "pool_scale", "w_o", "g_ffn", "w_up", "conv_w", "conv_b",
             "w_down", "g_ple", "w_ple", "w_ple_gate", "g_final"]
    deltas, new_m, new_v = [], [], []
    for nme in names:
        gr = grads[nme].reshape(weights[nme].shape)
        grads[nme] = gr
        dlt, nm, nv = _adamw("adamw_" + nme, weights[nme], gr, m_in[nme], v_in[nme])
        deltas.append(dlt)
        new_m.append(nm)
        new_v.append(nv)
    return (loss, grad_x, *[grads[nme] for nme in names], *deltas, *new_m, *new_v)
```

```python
import functools
import math

import jax
import jax.numpy as jnp
from jax import lax
from jax.experimental import pallas as pl
from jax.experimental.pallas import tpu as pltpu

F32 = jnp.float32
BF16 = jnp.bfloat16
MESH = pl.DeviceIdType.MESH

D = 1024
SEQ = 2048
DEPTH = 2
HEAD = 128
GROUP_W = 512
DILATIONS = (1, 4, 16)
ROPE_DIM = 32
ROPE_THETA = 500000.0
NEG_INF = -1e30
ZW = 7680
OFF_K, OFF_V, OFF_U, OFF_GA, OFF_GB = 1536, 3072, 4608, 5632, 6656
FF = 2816
UW = 2 * FF
PLE = 256
NCHIP = 4
IN_S, UP_S, DN_S = ZW // NCHIP, UW // NCHIP, FF // NCHIP
RMS_EPS = 1e-6
LR, B1, B2, ADAM_EPS, WD, STEP = 0.001, 0.9, 0.999, 1e-08, 0.01, 10
SMALL_ROWS = 56
VMEM_CAP = 48 * 1024 * 1024


def _params(n_grid, vmem=VMEM_CAP):
    return pltpu.CompilerParams(dimension_semantics=("arbitrary",) * n_grid, vmem_limit_bytes=vmem)


def _sigmoid(v):
    return 1.0 / (1.0 + jnp.exp(-v))


def _rows8(v):
    return jnp.sum(v.reshape(v.shape[0] // 8, 8, v.shape[1]), axis=0)


def _dot(av, bv, ta=False, tb=False):
    dims = (((0,) if ta else (1,), (1,) if tb else (0,)), ((), ()))
    return lax.dot_general(av.astype(BF16), bv.astype(BF16), dims, preferred_element_type=F32)


def _mm(name, a, b, *, grid, a_spec, b_spec, o_spec, out_shape, ta=False, tb=False, k_axis=None, nk=1,
        acc_shape=None, res=None, res_spec=None, buf=None, compute=None):
    has_res, has_buf = res is not None, buf is not None

    def body(*refs):
        a_ref, b_ref = refs[0], refs[1]
        pos = 2
        r_ref = None
        if has_res:
            r_ref = refs[pos]
            pos += 1
        if has_buf:
            pos += 1
        o_ref = refs[pos]
        if compute is None:
            av = a_ref[...]
            bv = b_ref[...]
            part = _dot(av.reshape(-1, av.shape[-1]), bv.reshape(-1, bv.shape[-1]), ta, tb)
        else:
            part = compute(a_ref, b_ref)

        def finish(val):
            if r_ref is not None:
                val = val + r_ref[...]
            o_ref[...] = val.reshape(o_ref.shape).astype(o_ref.dtype)

        if nk == 1:
            finish(part)
        else:
            acc_ref = refs[pos + 1]
            k = pl.program_id(k_axis)

            @pl.when(k == 0)
            def _():
                acc_ref[...] = part

            @pl.when(k > 0)
            def _():
                acc_ref[...] += part

            @pl.when(k == nk - 1)
            def _():
                finish(acc_ref[...])

    ins, in_specs = [a, b], [a_spec, b_spec]
    if has_res:
        ins.append(res)
        in_specs.append(res_spec)
    aliases = {}
    if has_buf:
        aliases = {len(ins): 0}
        ins.append(buf)
        in_specs.append(pl.BlockSpec(memory_space=pl.ANY))
    scratch = [pltpu.VMEM(acc_shape, F32)] if nk > 1 else []
    return pl.pallas_call(
        body, name=name, grid=grid, in_specs=in_specs, out_specs=o_spec, out_shape=out_shape,
        scratch_shapes=scratch, input_output_aliases=aliases, compiler_params=_params(len(grid)),
    )(*ins)


def _sds(shape, dtype):
    return jax.ShapeDtypeStruct(shape, dtype)


def _rms_fwd(name, x, g, tr=512):
    T = x.shape[0]

    def body(x_ref, g_ref, h_ref):
        xv = x_ref[...]
        r = lax.rsqrt(jnp.mean(xv * xv, axis=-1, keepdims=True) + RMS_EPS)
        h_ref[...] = (xv * r * g_ref[...]).astype(BF16)

    return pl.pallas_call(
        body, name=name, grid=(T // tr,),
        in_specs=[pl.BlockSpec((tr, D), lambda i: (i, 0)), pl.BlockSpec((1, D), lambda i: (0, 0))],
        out_specs=pl.BlockSpec((tr, D), lambda i: (i, 0)), out_shape=_sds((T, D), BF16),
        compiler_params=_params(1),
    )(x, g)


def _rms_bwd(name, x, dh, g, dres, tr=512):
    T = x.shape[0]

    def body(x_ref, dh_ref, g_ref, dres_ref, dx_ref, dg_ref):
        xv = x_ref[...]
        r = lax.rsqrt(jnp.mean(xv * xv, axis=-1, keepdims=True) + RMS_EPS)
        xh = xv * r
        dhv = dh_ref[...]
        part = _rows8(dhv * xh)

        @pl.when(pl.program_id(0) == 0)
        def _():
            dg_ref[...] = part

        @pl.when(pl.program_id(0) > 0)
        def _():
            dg_ref[...] += part

        dxh = dhv * g_ref[...]
        dx_ref[...] = dres_ref[...] + r * (dxh - xh * jnp.mean(dxh * xh, axis=-1, keepdims=True))

    row = pl.BlockSpec((tr, D), lambda i: (i, 0))
    return pl.pallas_call(
        body, name=name, grid=(T // tr,),
        in_specs=[row, row, pl.BlockSpec((1, D), lambda i: (0, 0)), row],
        out_specs=[row, pl.BlockSpec((8, D), lambda i: (0, 0))],
        out_shape=[_sds((T, D), F32), _sds((8, D), F32)],
        compiler_params=_params(1),
    )(x, dh, g, dres)


def _final_loss(x, g, tgt, tr=512):
    T = x.shape[0]

    def body(x_ref, g_ref, t_ref, dx_ref, dg_ref, sq_ref):
        xv = x_ref[...]
        r = lax.rsqrt(jnp.mean(xv * xv, axis=-1, keepdims=True) + RMS_EPS)
        xh = xv * r
        gv = g_ref[...]
        e = xh * gv - t_ref[...]
        dy = e * (1.0 / D)
        pg = _rows8(dy * xh)
        ps = _rows8(e * e)

        @pl.when(pl.program_id(0) == 0)
        def _():
            dg_ref[...] = pg
            sq_ref[...] = ps

        @pl.when(pl.program_id(0) > 0)
        def _():
            dg_ref[...] += pg
            sq_ref[...] += ps

        dxh = dy * gv
        dx_ref[...] = r * (dxh - xh * jnp.mean(dxh * xh, axis=-1, keepdims=True))

    row = pl.BlockSpec((tr, D), lambda i: (i, 0))
    acc = pl.BlockSpec((8, D), lambda i: (0, 0))
    return pl.pallas_call(
        body, name="final_loss", grid=(T // tr,),
        in_specs=[row, pl.BlockSpec((1, D), lambda i: (0, 0)), row],
        out_specs=[row, acc, acc],
        out_shape=[_sds((T, D), F32), _sds((8, D), F32), _sds((8, D), F32)],
        compiler_params=_params(1),
    )(x, g, tgt)


def _ple_fwd(name, x, pe, pg, tr=512):
    T = x.shape[0]

    def body(x_ref, pe_ref, pg_ref, o_ref):
        o_ref[...] = x_ref[...] + pe_ref[...] * _sigmoid(pg_ref[...])

    row = pl.BlockSpec((tr, D), lambda i: (i, 0))
    return pl.pallas_call(
        body, name=name, grid=(T // tr,), in_specs=[row, row, row], out_specs=row,
        out_shape=_sds((T, D), F32), compiler_params=_params(1),
    )(x, pe, pg)


def _ple_bwd(name, dx, pe, pg, tr=512):
    T = dx.shape[0]

    def body(dx_ref, pe_ref, pg_ref, dpe_ref, dpg_ref):
        s = _sigmoid(pg_ref[...])
        dxv = dx_ref[...]
        dpe_ref[...] = (dxv * s).astype(BF16)
        dpg_ref[...] = (dxv * pe_ref[...] * s * (1.0 - s)).astype(BF16)

    row = pl.BlockSpec((tr, D), lambda i: (i, 0))
    return pl.pallas_call(
        body, name=name, grid=(T // tr,), in_specs=[row, row, row], out_specs=[row, row],
        out_shape=[_sds((T, D), BF16), _sds((T, D), BF16)], compiler_params=_params(1),
    )(dx, pe, pg)


def _gate_fwd(name, z, ya, yb, tr=512):
    T = z.shape[0]
    w = 512

    def body(ga_ref, gb_ref, ya_ref, yb_ref, o_ref):
        o_ref[...] = (_sigmoid(ga_ref[...]) * ya_ref[...] + _sigmoid(gb_ref[...]) * yb_ref[...]).astype(BF16)

    col = pl.BlockSpec((tr, w), lambda i, j: (i, j))
    return pl.pallas_call(
        body, name=name, grid=(T // tr, D // w),
        in_specs=[pl.BlockSpec((tr, w), lambda i, j: (i, OFF_GA // w + j)),
                  pl.BlockSpec((tr, w), lambda i, j: (i, OFF_GB // w + j)), col, col],
        out_specs=col, out_shape=_sds((T, D), BF16), compiler_params=_params(2),
    )(z, z, ya, yb)


def _gate_bwd(name, z, off, y, dm, dz, tr=512):
    T = z.shape[0]
    w = 512
    has_dz = dz is not None

    def body(*refs):
        g_ref, y_ref, dm_ref = refs[:3]
        dy_ref, dz_ref = refs[-2:]
        s = _sigmoid(g_ref[...])
        dmv = dm_ref[...]
        dy_ref[...] = (dmv * s).astype(BF16)
        dz_ref[...] = (dmv * y_ref[...] * s * (1.0 - s)).astype(BF16)

    col = pl.BlockSpec((tr, w), lambda i, j: (i, j))
    gcol = pl.BlockSpec((tr, w), lambda i, j: (i, off // w + j))
    ins, in_specs, aliases = [z, y, dm], [gcol, col, col], {}
    if has_dz:
        ins.append(dz)
        in_specs.append(pl.BlockSpec(memory_space=pl.ANY))
        aliases = {3: 1}
    return pl.pallas_call(
        body, name=name, grid=(T // tr, D // w), in_specs=in_specs, out_specs=[col, gcol],
        out_shape=[_sds((T, D), BF16), _sds((T, ZW), BF16)], input_output_aliases=aliases,
        compiler_params=_params(2),
    )(*ins)


def _shift_down(v, k, rows):
    return jnp.where(rows >= k, pltpu.roll(v, k, 0), 0.0)


def _shift_up(v, k, rows):
    n = v.shape[0]
    return jnp.where(rows < n - k, pltpu.roll(v, n - k, 0), 0.0)


def _pool_window(v, g, rows, shift):
    s2 = v + shift(v, 1, rows)
    s4 = s2 + shift(s2, 2, rows)
    s8 = s4 + shift(s4, 4, rows)
    s16 = s8 + shift(s8, 8, rows)
    return jnp.where(g == 0, s2, jnp.where(g == 1, s4, jnp.where(g == 2, s8, s16)))


def _pool_count(g, rows):
    wlen = jnp.left_shift(2, g).astype(F32)
    return jnp.minimum(rows.astype(F32) + 1.0, wlen)


def _pool_fwd(name, z3, g_sc, layer, scale):
    Bn = z3.shape[0]
    gw = 256

    def body(u_ref, pw_ref, sc_ref, pooled_ref, ms_ref):
        g = pl.program_id(1)
        u = u_ref[...]
        rows = lax.broadcasted_iota(jnp.int32, u.shape, 0)
        pooled = (_pool_window(u, g, rows, _shift_down) / _pool_count(g, rows) - u).astype(BF16)
        pooled_ref[...] = pooled
        pw = pw_ref[...].reshape(gw, gw)
        mixed = jnp.dot(pooled, pw, preferred_element_type=F32)
        ms_ref[...] = (mixed * sc_ref[...]).astype(BF16)

    blk = pl.BlockSpec((None, SEQ, gw), lambda b, g: (b, 0, g))
    return pl.pallas_call(
        body, name=name, grid=(Bn, 4),
        in_specs=[pl.BlockSpec((None, SEQ, gw), lambda b, g: (b, 0, OFF_U // gw + g)),
                  pl.BlockSpec((NCHIP, None, 64, gw), lambda b, g: (0, layer, 12 + g, 0)),
                  pl.BlockSpec((1, gw), lambda b, g: (0, g))],
        out_specs=[blk, blk],
        out_shape=[_sds((Bn, SEQ, D), BF16), _sds((Bn, SEQ, D), BF16)],
        compiler_params=_params(2),
    )(z3, g_sc, scale)


def _pool_bwd(name, dms3, pooled3, g_sc, layer, scale, dz3, gg_sc):
    Bn = dms3.shape[0]
    gw = 256
    has_gg = gg_sc is not None

    def body(*refs):
        dms_ref, pooled_ref, pw_ref, sc_ref = refs[:4]
        dz_ref, dpw_ref, dsc_ref = refs[-3:]
        g, b = pl.program_id(0), pl.program_id(1)
        pooled = pooled_ref[...]
        pw = pw_ref[...].reshape(gw, gw)
        dms = dms_ref[...]
        mixed = jnp.dot(pooled, pw, preferred_element_type=F32)
        psc = _rows8(dms * mixed)
        dmixed = (dms * sc_ref[...]).astype(BF16)
        dpw = lax.dot_general(pooled, dmixed, (((0,), (0,)), ((), ())), preferred_element_type=F32)
        dpw = dpw.reshape(NCHIP, 64, gw)

        @pl.when(b == 0)
        def _():
            dsc_ref[...] = psc
            dpw_ref[...] = dpw

        @pl.when(b > 0)
        def _():
            dsc_ref[...] += psc
            dpw_ref[...] += dpw

        dpooled = lax.dot_general(dmixed, pw, (((1,), (1,)), ((), ())), preferred_element_type=F32)
        rows = lax.broadcasted_iota(jnp.int32, dpooled.shape, 0)
        dq = dpooled / _pool_count(g, rows)
        dz_ref[...] = (_pool_window(dq, g, rows, _shift_up) - dpooled).astype(BF16)

    ins = [dms3, pooled3, g_sc, scale, dz3]
    in_specs = [pl.BlockSpec((None, SEQ, gw), lambda g, b: (b, 0, g)),
                pl.BlockSpec((None, SEQ, gw), lambda g, b: (b, 0, g)),
                pl.BlockSpec((NCHIP, None, 64, gw), lambda g, b: (0, layer, 12 + g, 0)),
                pl.BlockSpec((1, gw), lambda g, b: (0, g)),
                pl.BlockSpec(memory_space=pl.ANY)]
    aliases = {4: 0}
    if has_gg:
        ins.append(gg_sc)
        in_specs.append(pl.BlockSpec(memory_space=pl.ANY))
        aliases[5] = 1
    return pl.pallas_call(
        body, name=name, grid=(4, Bn), in_specs=in_specs,
        out_specs=[pl.BlockSpec((None, SEQ, gw), lambda g, b: (b, 0, OFF_U // gw + g)),
                   pl.BlockSpec((NCHIP, None, 64, gw), lambda g, b: (0, layer, 12 + g, 0)),
                   pl.BlockSpec((8, gw), lambda g, b: (0, g))],
        out_shape=[_sds(dz3.shape, BF16), _sds((NCHIP, DEPTH, D, 256), F32), _sds((8, D), F32)],
        input_output_aliases=aliases, compiler_params=_params(2),
    )(*ins)


CT = 256
NCT = FF // CT


def _conv_pre(u, cw_ref, cb_ref, rows):
    return (cb_ref[...] + cw_ref[0:1, :] * _shift_down(u, 2, rows) + cw_ref[1:2, :] * _shift_down(u, 1, rows)
            + cw_ref[2:3, :] * u)


def _conv_fwd(name, u3, cw, cb):
    Bn = u3.shape[0]

    def body(ug_ref, uv_ref, cwg_ref, cwv_ref, cbg_ref, cbv_ref, a_ref):
        ug, uv = ug_ref[...], uv_ref[...]
        rows = lax.broadcasted_iota(jnp.int32, ug.shape, 0)
        yg = _conv_pre(ug, cwg_ref, cbg_ref, rows)
        yv = _conv_pre(uv, cwv_ref, cbv_ref, rows)
        a_ref[...] = (yg * _sigmoid(yg) * yv).astype(BF16)

    def blk(off):
        return pl.BlockSpec((None, SEQ, CT), lambda b, c: (b, 0, off + c))

    return pl.pallas_call(
        body, name=name, grid=(Bn, NCT),
        in_specs=[blk(0), blk(NCT),
                  pl.BlockSpec((3, CT), lambda b, c: (0, c)), pl.BlockSpec((3, CT), lambda b, c: (0, NCT + c)),
                  pl.BlockSpec((1, CT), lambda b, c: (0, c)), pl.BlockSpec((1, CT), lambda b, c: (0, NCT + c))],
        out_specs=blk(0), out_shape=_sds((Bn, SEQ, FF), BF16), compiler_params=_params(2),
    )(u3, u3, cw, cw, cb, cb)


def _conv_bwd(name, da3, u3, cw, cb):
    Bn = u3.shape[0]
    last = NCT * Bn - 1

    def body(da_ref, ug_ref, uv_ref, cwg_ref, cwv_ref, cbg_ref, cbv_ref,
             du_ref, dcwg_ref, dcwv_ref, dcbg_ref, dcbv_ref, stage_g, stage_v, sems):
        c, b = pl.program_id(0), pl.program_id(1)
        step = c * Bn + b
        ug, uv, da = ug_ref[...], uv_ref[...], da_ref[...]
        rows = lax.broadcasted_iota(jnp.int32, ug.shape, 0)
        yg = _conv_pre(ug, cwg_ref, cbg_ref, rows)
        yv = _conv_pre(uv, cwv_ref, cbv_ref, rows)
        s = _sigmoid(yg)
        dyv = da * (yg * s)
        dyg = da * yv * (s * (1.0 + yg * (1.0 - s)))

        def writes(off_c, stage, sem):
            col = pl.multiple_of(off_c + c * CT, CT)
            return pltpu.make_async_copy(stage, du_ref.at[b, :, pl.ds(col, CT)], sem)

        @pl.when(step > 0)
        def _():
            writes(0, stage_g, sems.at[0]).wait()
            writes(FF, stage_v, sems.at[1]).wait()

        for dy, u, cw_ref, stage, dcw_ref, dcb_ref in ((dyg, ug, cwg_ref, stage_g, dcwg_ref, dcbg_ref),
                                                       (dyv, uv, cwv_ref, stage_v, dcwv_ref, dcbv_ref)):
            d1, d2 = _shift_up(dy, 1, rows), _shift_up(dy, 2, rows)
            stage[...] = (cw_ref[2:3, :] * dy + cw_ref[1:2, :] * d1 + cw_ref[0:1, :] * d2).astype(BF16)
            dcw = jnp.concatenate([jnp.sum(d2 * u, axis=0, keepdims=True), jnp.sum(d1 * u, axis=0, keepdims=True),
                                   jnp.sum(dy * u, axis=0, keepdims=True)], axis=0)
            dcb = jnp.sum(dy, axis=0, keepdims=True)

            @pl.when(b == 0)
            def _():
                dcw_ref[...] = dcw
                dcb_ref[...] = dcb

            @pl.when(b > 0)
            def _():
                dcw_ref[...] += dcw
                dcb_ref[...] += dcb

        writes(0, stage_g, sems.at[0]).start()
        writes(FF, stage_v, sems.at[1]).start()

        @pl.when(step == last)
        def _():
            writes(0, stage_g, sems.at[0]).wait()
            writes(FF, stage_v, sems.at[1]).wait()

    def blk(off):
        return pl.BlockSpec((None, SEQ, CT), lambda c, b: (b, 0, off + c))

    def vec(r, off):
        return pl.BlockSpec((r, CT), lambda c, b: (0, off + c))

    du3, dcwg, dcwv, dcbg, dcbv = pl.pallas_call(
        body, name=name, grid=(NCT, Bn),
        in_specs=[blk(0), blk(0), blk(NCT), vec(3, 0), vec(3, NCT), vec(1, 0), vec(1, NCT)],
        out_specs=[pl.BlockSpec(memory_space=pl.ANY), vec(3, 0), vec(3, 0), vec(1, 0), vec(1, 0)],
        out_shape=[_sds((Bn, SEQ, UW), BF16), _sds((3, FF), F32), _sds((3, FF), F32), _sds((1, FF), F32),
                   _sds((1, FF), F32)],
        scratch_shapes=[pltpu.VMEM((SEQ, CT), BF16)] * 2 + [pltpu.SemaphoreType.DMA((2,))],
        compiler_params=_params(2),
    )(da3, u3, u3, cw, cw, cb, cb)
    return du3, jnp.concatenate([dcwg, dcwv], axis=1), jnp.concatenate([dcbg, dcbv], axis=1)


def _rope_tables():
    pos = jnp.arange(SEQ, dtype=F32)
    inv_freq = jnp.exp(jnp.arange(0, ROPE_DIM, 2, dtype=F32) * (-math.log(ROPE_THETA) / ROPE_DIM))
    ang = pos[:, None] * inv_freq[None, :]
    cos, sin = jnp.cos(ang), jnp.sin(ang)
    half = ROPE_DIM // 2
    zeros = jnp.zeros((SEQ, HEAD - ROPE_DIM), F32)
    zh = jnp.zeros((SEQ, half), F32)
    tab_c = jnp.concatenate([cos, cos, zeros + 1.0], axis=1)
    tab_a = jnp.concatenate([-sin, zh, zeros], axis=1)
    tab_b = jnp.concatenate([zh, sin, zeros], axis=1)
    return tab_c, tab_a, tab_b


def _rot(v, tc, ta, tb):
    half = ROPE_DIM // 2
    return v * tc + pltpu.roll(v, HEAD - half, 1) * ta + pltpu.roll(v, half, 1) * tb


def _rot_t(dv, tc, ta, tb):
    half = ROPE_DIM // 2
    return dv * tc + pltpu.roll(dv * ta, half, 1) + pltpu.roll(dv * tb, HEAD - half, 1)


def _band_masks():
    qi = lax.broadcasted_iota(jnp.int32, (HEAD, 2 * HEAD), 0)
    ki = lax.broadcasted_iota(jnp.int32, (HEAD, 2 * HEAD), 1)
    diff = HEAD + qi - ki
    both = (diff >= 0) & (diff <= HEAD)
    q1 = lax.broadcasted_iota(jnp.int32, (HEAD, HEAD), 0)
    k1 = lax.broadcasted_iota(jnp.int32, (HEAD, HEAD), 1)
    return q1 >= k1, both


_NT = (((1,), (1,)), ((), ()))
_TN = (((0,), (0,)), ((), ()))
_SCALE = HEAD ** -0.5


ATT_W = HEAD
ATT_HP = ATT_W // HEAD


def _res_rows(r, n, d, base=0):
    return pl.ds(base * d + r, n, stride=d) if d > 1 else pl.ds(base, n)


def _attn_load(q_ref, k_ref, v_ref, tc_ref, ta_ref, tb_ref, qs, ks, vs, d):
    L = SEQ // d
    for r in range(d):
        rows = _res_rows(r, L, d)
        tc, ta, tb = tc_ref[rows, :], ta_ref[rows, :], tb_ref[rows, :]
        dst = slice(r * L, (r + 1) * L)
        for hh in range(ATT_HP):
            sl = slice(hh * HEAD, (hh + 1) * HEAD)
            qs[dst, sl] = _rot(q_ref[rows, sl], tc, ta, tb).astype(BF16)
            ks[dst, sl] = _rot(k_ref[rows, sl], tc, ta, tb).astype(BF16)
            vs[dst, sl] = v_ref[rows, sl].astype(BF16)


def _attn_fwd(name, z3, tabs, g, d):
    Bn = z3.shape[0]
    L = SEQ // d
    nb = L // HEAD
    W, nh = ATT_W, GROUP_W // ATT_W

    def body(q_ref, k_ref, v_ref, tc_ref, ta_ref, tb_ref, o_ref, l_ref, qs, ks, vs):
        m_first, m_both = _band_masks()
        _attn_load(q_ref, k_ref, v_ref, tc_ref, ta_ref, tb_ref, qs, ks, vs, d)
        for r in range(d):
            for hh in range(ATT_HP):
                sl = slice(hh * HEAD, (hh + 1) * HEAD)
                for n in range(nb):
                    rq = slice(r * L + n * HEAD, r * L + (n + 1) * HEAD)
                    rk = slice(r * L + max(n - 1, 0) * HEAD, r * L + (n + 1) * HEAD)
                    s = lax.dot_general(qs[rq, sl], ks[rk, sl], _NT, preferred_element_type=F32) * _SCALE
                    s = jnp.where(m_first if n == 0 else m_both, s, NEG_INF)
                    m = jnp.max(s, axis=-1, keepdims=True)
                    e = jnp.exp(s - m)
                    den = jnp.sum(e, axis=-1, keepdims=True)
                    p = (e * (1.0 / den)).astype(BF16)
                    rows = _res_rows(r, HEAD, d, n * HEAD)
                    o_ref[rows, sl] = jnp.dot(p, vs[rk, sl], preferred_element_type=F32)
                    l_ref[rows, sl] = jnp.broadcast_to(m + jnp.log(den), (HEAD, HEAD))

    def zcol(off):
        return pl.BlockSpec((None, SEQ, W), lambda b, h: (b, 0, (off + g * GROUP_W) // W + h))

    tab = pl.BlockSpec((SEQ, HEAD), lambda b, h: (0, 0))
    out = pl.BlockSpec((None, SEQ, W), lambda b, h: (b, 0, h))
    return pl.pallas_call(
        body, name=name, grid=(Bn, nh),
        in_specs=[zcol(0), zcol(OFF_K), zcol(OFF_V), tab, tab, tab],
        out_specs=[out, out],
        out_shape=[_sds((Bn, SEQ, GROUP_W), F32), _sds((Bn, SEQ, GROUP_W), F32)],
        scratch_shapes=[pltpu.VMEM((SEQ, W), BF16)] * 3,
        compiler_params=_params(2),
    )(z3, z3, z3, *tabs)


def _attn_bwd(name, z3, tabs, g, d, do3, lse3, delta3, dz3):
    Bn = z3.shape[0]
    L = SEQ // d
    nb = L // HEAD
    W, nh = ATT_W, GROUP_W // ATT_W

    def body(q_ref, k_ref, v_ref, tc_ref, ta_ref, tb_ref, do_ref, l_ref, dl_ref, dz_in, dz_ref,
             qs, ks, vs, dos, dqs, dks, dvs, nat, oq, ok, ov, sems):
        b, h = pl.program_id(0), pl.program_id(1)
        m_first, m_both = _band_masks()
        _attn_load(q_ref, k_ref, v_ref, tc_ref, ta_ref, tb_ref, qs, ks, vs, d)
        for r in range(d):
            dos[r * L:(r + 1) * L, :] = do_ref[_res_rows(r, L, d), :].astype(BF16)
        dks[...] = jnp.zeros_like(dks)
        dvs[...] = jnp.zeros_like(dvs)
        for r in range(d):
            for hh in range(ATT_HP):
                sl = slice(hh * HEAD, (hh + 1) * HEAD)
                for n in range(nb):
                    rq = slice(r * L + n * HEAD, r * L + (n + 1) * HEAD)
                    rk = slice(r * L + max(n - 1, 0) * HEAD, r * L + (n + 1) * HEAD)
                    rows = _res_rows(r, HEAD, d, n * HEAD)
                    qb, kk, vv, dob = qs[rq, sl], ks[rk, sl], vs[rk, sl], dos[rq, sl]
                    s = lax.dot_general(qb, kk, _NT, preferred_element_type=F32) * _SCALE
                    s = jnp.where(m_first if n == 0 else m_both, s, NEG_INF)
                    p = jnp.exp(s - l_ref[rows, sl][:, 0:1])
                    dp = lax.dot_general(dob, vv, _NT, preferred_element_type=F32)
                    ds = (p * (dp - dl_ref[rows, sl][:, 0:1]) * _SCALE).astype(BF16)
                    dqs[rq, sl] = jnp.dot(ds, kk, preferred_element_type=F32)
                    dks[rk, sl] += lax.dot_general(ds, qb, _TN, preferred_element_type=F32)
                    dvs[rk, sl] += lax.dot_general(p.astype(BF16), dob, _TN, preferred_element_type=F32)
        tc, ta, tb = tc_ref[...], ta_ref[...], tb_ref[...]
        step = b * nh + h

        def writes():
            base = g * GROUP_W + h * W
            return [pltpu.make_async_copy(src, dz_ref.at[b, :, pl.ds(pl.multiple_of(base + off, HEAD), W)],
                                          sems.at[i])
                    for i, (src, off) in enumerate(((oq, 0), (ok, OFF_K), (ov, OFF_V)))]

        @pl.when(step > 0)
        def _():
            for cp in writes():
                cp.wait()

        for src, dst, rotate in ((dqs, oq, True), (dks, ok, True), (dvs, ov, False)):
            for r in range(d):
                nat[_res_rows(r, L, d), :] = src[r * L:(r + 1) * L, :]
            for hh in range(ATT_HP):
                sl = slice(hh * HEAD, (hh + 1) * HEAD)
                val = nat[:, sl]
                dst[:, sl] = (_rot_t(val, tc, ta, tb) if rotate else val).astype(BF16)
        for cp in writes():
            cp.start()

        @pl.when(step == Bn * nh - 1)
        def _():
            for cp in writes():
                cp.wait()

    def zcol(off):
        return pl.BlockSpec((None, SEQ, W), lambda b, h: (b, 0, (off + g * GROUP_W) // W + h))

    tab = pl.BlockSpec((SEQ, HEAD), lambda b, h: (0, 0))
    gcol = pl.BlockSpec((None, SEQ, W), lambda b, h: (b, 0, h))
    any_spec = pl.BlockSpec(memory_space=pl.ANY)
    return pl.pallas_call(
        body, name=name, grid=(Bn, nh),
        in_specs=[zcol(0), zcol(OFF_K), zcol(OFF_V), tab, tab, tab, gcol, gcol, gcol, any_spec],
        out_specs=any_spec,
        out_shape=_sds((Bn, SEQ, ZW), BF16),
        scratch_shapes=[pltpu.VMEM((SEQ, W), BF16)] * 4 + [pltpu.VMEM((SEQ, W), F32)] * 4
        + [pltpu.VMEM((SEQ, W), BF16)] * 3 + [pltpu.SemaphoreType.DMA((3,))],
        input_output_aliases={9: 0}, compiler_params=_params(2),
    )(z3, z3, z3, *tabs, do3, lse3, delta3, dz3)


def _merge_weights(l0, l1, l2):
    m = jnp.maximum(jnp.maximum(l0, l1), l2)
    e0, e1, e2 = jnp.exp(l0 - m), jnp.exp(l1 - m), jnp.exp(l2 - m)
    inv = 1.0 / (e0 + e1 + e2)
    return e0 * inv, e1 * inv, e2 * inv


def _merge_fwd(name, outs, lses, tr=512):
    T = outs[0].shape[0]

    def body(o0, o1, o2, l0, l1, l2, a_ref):
        w0, w1, w2 = _merge_weights(l0[...], l1[...], l2[...])
        a_ref[...] = (w0 * o0[...] + w1 * o1[...] + w2 * o2[...]).astype(BF16)

    row = pl.BlockSpec((tr, GROUP_W), lambda i: (i, 0))
    return pl.pallas_call(
        body, name=name, grid=(T // tr,), in_specs=[row] * 6, out_specs=row,
        out_shape=_sds((T, GROUP_W), BF16), compiler_params=_params(1),
    )(*outs, *lses)


def _merge_bwd(name, outs, lses, dattn, tr=512):
    T = outs[0].shape[0]

    def body(o0, o1, o2, l0, l1, l2, da_ref, d0, d1, d2, e0, e1, e2):
        w = _merge_weights(l0[...], l1[...], l2[...])
        da = da_ref[...]
        attn = w[0] * o0[...] + w[1] * o1[...] + w[2] * o2[...]
        prod = da * attn
        csum = jnp.concatenate(
            [jnp.broadcast_to(jnp.sum(prod[:, hh * HEAD:(hh + 1) * HEAD], axis=-1, keepdims=True), (tr, HEAD))
             for hh in range(GROUP_W // HEAD)], axis=1)
        for wg, d_ref, e_ref in zip(w, (d0, d1, d2), (e0, e1, e2)):
            d_ref[...] = wg * da
            e_ref[...] = wg * csum

    row = pl.BlockSpec((tr, GROUP_W), lambda i: (i, 0))
    res = pl.pallas_call(
        body, name=name, grid=(T // tr,), in_specs=[row] * 7, out_specs=[row] * 6,
        out_shape=[_sds((T, GROUP_W), F32)] * 6,
        compiler_params=_params(1),
    )(*outs, *lses, dattn)
    return res[:3], res[3:]


def _local_step(x3, p4, tgt3, vecs, G):
    Bn = x3.shape[0]
    T = Bn * SEQ
    x = x3.reshape(T, D)
    tgt = tgt3.reshape(T, D)
    pb = p4.astype(BF16).reshape(DEPTH, T, PLE)
    tabs = _rope_tables()
    tm = 1024 if T % 1024 == 0 else 512
    nt = T // tm
    tk = 1024 if T % 1024 == 0 else 512
    ntk = T // tk
    tm5 = 512
    f32o = lambda n: _sds((T, n), F32)

    def spec(shape, fn):
        return pl.BlockSpec(shape, fn)

    saved = []
    for l in range(DEPTH):
        L = str(l)
        g_mix, g_ffn, g_ple = (vecs[k][l:l + 1] for k in ("g_mix", "g_ffn", "g_ple"))
        pscale, cb, cw = vecs["pool_scale"][l:l + 1], vecs["conv_b"][l:l + 1], vecs["conv_w"][l]
        h = _rms_fwd("rms_mix" + L, x, g_mix)
        z = _mm("mm_z" + L, h, G["in"], grid=(nt, 12),
                a_spec=spec((tm, D), lambda i, n: (i, 0)),
                b_spec=spec((None, None, D, 640), lambda i, n: (n // 3, l, 0, n % 3)),
                o_spec=spec((tm, 640), lambda i, n: (i, n)), out_shape=f32o(ZW))
        z3 = z.reshape(Bn, SEQ, ZW)
        outs, lses = [], []
        for g, d in enumerate(DILATIONS):
            o_g, l_g = _attn_fwd("attn_fwd%d_%d" % (g, l), z3, tabs, g, d)
            outs.append(o_g.reshape(T, GROUP_W))
            lses.append(l_g.reshape(T, GROUP_W))
        attn = _merge_fwd("merge_fwd" + L, outs, lses)
        def cols4(a_ref, b_ref):
            av = a_ref[...]
            return jnp.concatenate([_dot(av, b_ref[j]) for j in range(NCHIP)], axis=1)

        ya = _mm("mm_ya" + L, attn, G["sc"], grid=(nt,), compute=cols4,
                 a_spec=spec((tm, GROUP_W), lambda i: (i, 0)),
                 b_spec=spec((NCHIP, None, GROUP_W, 256), lambda i: (0, l, 0, 0)),
                 o_spec=spec((tm, D), lambda i: (i, 0)), out_shape=f32o(D))
        pooled3, ms3 = _pool_fwd("pool_fwd" + L, z3, G["sc"], l, pscale)
        ms = ms3.reshape(T, D)

        def row_sharded(name, a, rb, res=None, kdim=D):
            if rb is None:
                b_arr, b_spec = G["dn"], spec((NCHIP, None, DN_S, D), lambda i: (0, l, 0, 0))
            else:
                b_arr, b_spec = G["r3"], spec((NCHIP, None, 256, D), lambda i: (0, l, rb, 0))
            return _mm(name, a, b_arr, grid=(T // tm5,),
                       a_spec=spec((tm5, kdim), lambda i: (i, 0)), b_spec=b_spec,
                       o_spec=spec((tm5, D), lambda i: (i, 0)), out_shape=f32o(D),
                       res=res, res_spec=None if res is None else spec((tm5, D), lambda i: (i, 0)))

        yb = row_sharded("mm_yb" + L, ms, 0)
        merged = _gate_fwd("gate_fwd" + L, z, ya, yb)
        x1 = row_sharded("mm_o" + L, merged, 1, res=x)
        h2 = _rms_fwd("rms_ffn" + L, x1, g_ffn)
        u = _mm("mm_up" + L, h2, G["up"], grid=(T // tm5, NCHIP),
                a_spec=spec((tm5, D), lambda i, n: (i, 0)),
                b_spec=spec((None, None, D, UP_S), lambda i, n: (n, l, 0, 0)),
                o_spec=spec((tm5, UP_S), lambda i, n: (i, n)), out_shape=f32o(UW))
        u3 = u.reshape(Bn, SEQ, UW)
        act = _conv_fwd("conv_fwd" + L, u3, cw, cb).reshape(T, FF)
        x2 = row_sharded("mm_down" + L, act, None, res=x1, kdim=FF)
        h3 = _rms_fwd("rms_ple" + L, x2, g_ple)
        pg = row_sharded("mm_pg" + L, h3, 2)
        pe = _mm("mm_pe" + L, pb[l], G["sc"], grid=(nt,), compute=cols4,
                 a_spec=spec((tm, PLE), lambda i: (i, 0)),
                 b_spec=spec((NCHIP, None, 256, 256), lambda i: (0, l, 2, 0)),
                 o_spec=spec((tm, D), lambda i: (i, 0)), out_shape=f32o(D))
        x3n = _ple_fwd("ple_fwd" + L, x2, pe, pg)
        saved.append(dict(x=x, h=h, z=z, outs=outs, lses=lses, attn=attn, ya=ya, yb=yb, pooled3=pooled3, ms=ms,
                          merged=merged, x1=x1, h2=h2, u3=u3, act=act, x2=x2, h3=h3, pg=pg, pe=pe))
        x = x3n

    dx, dg_final8, sq8 = _final_loss(x, vecs["g_final"].reshape(1, D), tgt)

    GG = dict.fromkeys(("in", "up", "sc", "r3", "dn"))
    gg_shape = {k: _sds(G[k].shape, F32) for k in G}
    small = {"g_final": dg_final8}

    for l in reversed(range(DEPTH)):
        L = str(l)
        sv = saved[l]
        g_mix, g_ffn, g_ple = (vecs[k][l:l + 1] for k in ("g_mix", "g_ffn", "g_ple"))
        pscale, cb, cw = vecs["pool_scale"][l:l + 1], vecs["conv_b"][l:l + 1], vecs["conv_w"][l]

        def wgrad_rows(name, a, b_arr, key, rb, a_cols=256):
            GG[key] = _mm(name, a, b_arr, grid=(2, ntk), ta=True, k_axis=1, nk=ntk, acc_shape=(D, 512),
                          a_spec=spec((tk, D), lambda n, k: (k, 0)),
                          b_spec=spec((tk, 512), lambda n, k: (k, n)),
                          o_spec=spec((NCHIP, None, 256, 512), lambda n, k: (0, l, rb, n)),
                          out_shape=gg_shape[key], buf=GG[key])

        def dgrad_rows(name, dy, rb):
            return _mm(name, dy, G["r3"], grid=(T // tm5,), tb=True,
                       a_spec=spec((tm5, D), lambda i: (i, 0)),
                       b_spec=spec((NCHIP, None, 256, D), lambda i: (0, l, rb, 0)),
                       o_spec=spec((tm5, D), lambda i: (i, 0)), out_shape=f32o(D))

        def rows4(a_ref, b_ref):
            av = a_ref[...]
            return jnp.concatenate([_dot(av, b_ref[:, j * 256:(j + 1) * 256], ta=True) for j in range(NCHIP)], axis=0)

        dpe, dpg = _ple_bwd("ple_bwd" + L, dx, sv["pe"], sv["pg"])
        GG["sc"] = _mm("wg_ple" + L, pb[l], dpe, grid=(ntk,), compute=rows4, k_axis=0, nk=ntk,
                       acc_shape=(NCHIP * PLE, 256),
                       a_spec=spec((tk, PLE), lambda k: (k, 0)), b_spec=spec((tk, D), lambda k: (k, 0)),
                       o_spec=spec((NCHIP, None, 256, 256), lambda k: (0, l, 2, 0)),
                       out_shape=gg_shape["sc"], buf=GG["sc"])
        wgrad_rows("wg_pg" + L, sv["h3"], dpg, "r3", 2)
        dh3 = dgrad_rows("dg_pg" + L, dpg, 2)
        dx, small["g_ple" + L] = _rms_bwd("rms_ple_bwd" + L, sv["x2"], dh3, g_ple, dx)

        da = _mm("dg_down" + L, dx, G["dn"], grid=(T // 256,), tb=True,
                 a_spec=spec((256, D), lambda i: (i, 0)),
                 b_spec=spec((NCHIP, None, DN_S, D), lambda i: (0, l, 0, 0)),
                 o_spec=spec((256, FF), lambda i: (i, 0)), out_shape=f32o(FF))
        GG["dn"] = _mm("wg_down" + L, sv["act"], dx, grid=(2, T // 512), ta=True, k_axis=1, nk=T // 512,
                       acc_shape=(FF, 512),
                       a_spec=spec((512, FF), lambda n, k: (k, 0)), b_spec=spec((512, 512), lambda n, k: (k, n)),
                       o_spec=spec((NCHIP, None, DN_S, 512), lambda n, k: (0, l, 0, n)),
                       out_shape=gg_shape["dn"], buf=GG["dn"])
        du3, dcw, dcb = _conv_bwd("conv_bwd" + L, da.reshape(Bn, SEQ, FF), sv["u3"], cw, cb)
        small["conv_w" + L], small["conv_b" + L] = dcw, dcb
        du = du3.reshape(T, UW)
        dh2 = _mm("dg_up" + L, du, G["up"], grid=(T // tm5, NCHIP), tb=True, k_axis=1, nk=NCHIP, acc_shape=(tm5, D),
                  a_spec=spec((tm5, UP_S), lambda i, k: (i, k)),
                  b_spec=spec((None, None, D, UP_S), lambda i, k: (k, l, 0, 0)),
                  o_spec=spec((tm5, D), lambda i, k: (i, 0)), out_shape=f32o(D))
        GG["up"] = _mm("wg_up" + L, sv["h2"], du, grid=(2, NCHIP, T // 512), ta=True, k_axis=2, nk=T // 512,
                       acc_shape=(512, UP_S),
                       a_spec=spec((512, 512), lambda m, j, k: (k, m)),
                       b_spec=spec((512, UP_S), lambda m, j, k: (k, j)),
                       o_spec=spec((None, None, 512, UP_S), lambda m, j, k: (j, l, m, 0)),
                       out_shape=gg_shape["up"], buf=GG["up"])
        dx, small["g_ffn" + L] = _rms_bwd("rms_ffn_bwd" + L, sv["x1"], dh2, g_ffn, dx)

        dmerged = dgrad_rows("dg_o" + L, dx, 1)
        wgrad_rows("wg_o" + L, sv["merged"], dx, "r3", 1)
        dya, dz = _gate_bwd("gate_bwd_a" + L, sv["z"], OFF_GA, sv["ya"], dmerged, None)
        dyb, dz = _gate_bwd("gate_bwd_b" + L, sv["z"], OFF_GB, sv["yb"], dmerged, dz)
        dms = dgrad_rows("dg_yb" + L, dyb, 0)
        wgrad_rows("wg_yb" + L, sv["ms"], dyb, "r3", 0)
        dz3, GG["sc"], small["pool_scale" + L] = _pool_bwd(
            "pool_bwd" + L, dms.reshape(Bn, SEQ, D), sv["pooled3"], G["sc"], l, pscale,
            dz.reshape(Bn, SEQ, ZW), GG["sc"])
        def kchunks4(a_ref, b_ref):
            total = _dot(a_ref[:, 0:256], b_ref[0], tb=True)
            for j in range(1, NCHIP):
                total = total + _dot(a_ref[:, j * 256:(j + 1) * 256], b_ref[j], tb=True)
            return total

        dattn = _mm("dg_ya" + L, dya, G["sc"], grid=(nt,), compute=kchunks4,
                    a_spec=spec((tm, D), lambda i: (i, 0)),
                    b_spec=spec((NCHIP, None, GROUP_W, 256), lambda i: (0, l, 0, 0)),
                    o_spec=spec((tm, GROUP_W), lambda i: (i, 0)), out_shape=f32o(GROUP_W))
        GG["sc"] = _mm("wg_ya" + L, sv["attn"], dya, grid=(ntk,), compute=rows4, k_axis=0, nk=ntk,
                       acc_shape=(NCHIP * GROUP_W, 256),
                       a_spec=spec((tk, GROUP_W), lambda k: (k, 0)), b_spec=spec((tk, D), lambda k: (k, 0)),
                       o_spec=spec((NCHIP, None, GROUP_W, 256), lambda k: (0, l, 0, 0)),
                       out_shape=gg_shape["sc"], buf=GG["sc"])
        dos, deltas = _merge_bwd("merge_bwd" + L, sv["outs"], sv["lses"], dattn)
        view3 = lambda t: t.reshape(Bn, SEQ, GROUP_W)
        sz3 = sv["z"].reshape(Bn, SEQ, ZW)
        for g, d in enumerate(DILATIONS):
            dz3 = _attn_bwd("attn_bwd%d_%d" % (g, l), sz3, tabs, g, d, view3(dos[g]), view3(sv["lses"][g]),
                            view3(deltas[g]), dz3)
        dz = dz3.reshape(T, ZW)
        dh = _mm("dg_z" + L, dz, G["in"], grid=(nt, 12), tb=True, k_axis=1, nk=12, acc_shape=(tm, D),
                 a_spec=spec((tm, 640), lambda i, k: (i, k)),
                 b_spec=spec((None, None, D, 640), lambda i, k: (k // 3, l, 0, k % 3)),
                 o_spec=spec((tm, D), lambda i, k: (i, 0)), out_shape=f32o(D))
        GG["in"] = _mm("wg_z" + L, sv["h"], dz, grid=(12, ntk), ta=True, k_axis=1, nk=ntk, acc_shape=(D, 640),
                       a_spec=spec((tk, D), lambda n, k: (k, 0)), b_spec=spec((tk, 640), lambda n, k: (k, n)),
                       o_spec=spec((None, None, D, 640), lambda n, k: (n // 3, l, 0, n % 3)),
                       out_shape=gg_shape["in"], buf=GG["in"])
        dx, small["g_mix" + L] = _rms_bwd("rms_mix_bwd" + L, sv["x"], dh, g_mix, dx)

    return sq8, dx.reshape(Bn, SEQ, D), GG, small


_ANY = pl.BlockSpec(memory_space=pl.ANY)
_KEYS = ("in", "up", "sc", "r3", "dn")


def _place():
    x, y, c = lax.axis_index("x"), lax.axis_index("y"), lax.axis_index("c")
    chips = [(1 - x, y), (x, 1 - y), (1 - x, 1 - y)]
    return x, y, c, 2 * x + y, chips


def _gather_weights(stacks, cw4):
    n = len(stacks)

    def body(*refs):
        g_refs, cwg_ref = refs[n + 1:2 * n + 1], refs[2 * n + 1]
        send_sems, recv_sems = refs[2 * n + 2:]
        x, y, c, me, chips = _place()

        def cw_copy(j, slot, chip):
            part = cwg_ref.at[slot]
            return pltpu.make_async_remote_copy(
                src_ref=part, dst_ref=part, send_sem=send_sems.at[6 * n + j],
                recv_sem=recv_sems.at[6 * n + j], device_id=(*chip, c), device_id_type=MESH)

        def half(k, cc):
            rh = stacks[k].shape[2] // 2
            return pl.ds(cc * rh, rh)

        def copy(k, j, slot, cc, to):
            part = g_refs[k].at[slot, :, half(k, cc)]
            return pltpu.make_async_remote_copy(
                src_ref=part, dst_ref=part, send_sem=send_sems.at[6 * k + j], recv_sem=recv_sems.at[6 * k + j],
                device_id=to, device_id_type=MESH)

        sent = []
        for k in range(n):
            for j, chip in enumerate(chips):
                cp = copy(k, j, me, c, (*chip, c))
                cp.start()
                sent.append(cp)
        for j, chip in enumerate(chips):
            cp = cw_copy(j, me, chip)
            cp.start()
            sent.append(cp)
        for k in range(n):
            for j, chip in enumerate(chips):
                slot = 2 * chip[0] + chip[1]
                copy(k, j, slot, c, (*chip, c)).wait_recv()
                fwd = copy(k, 3 + j, slot, c, (x, y, 1 - c))
                fwd.start()
                sent.append(fwd)
        for k in range(n):
            for j, chip in enumerate(chips):
                copy(k, 3 + j, 2 * chip[0] + chip[1], 1 - c, (x, y, 1 - c)).wait_recv()
        for j, chip in enumerate(chips):
            cw_copy(j, 2 * chip[0] + chip[1], chip).wait_recv()
        for cp in sent:
            cp.wait_send()

    outs = pl.pallas_call(
        body, name="gather_weights", in_specs=[_ANY] * (n + 1), out_specs=[_ANY] * (n + 1),
        out_shape=[_sds(s.shape, s.dtype) for s in stacks] + [_sds(cw4.shape, cw4.dtype)],
        scratch_shapes=[pltpu.SemaphoreType.DMA((6 * n + 3,)), pltpu.SemaphoreType.DMA((6 * n + 3,))],
        input_output_aliases={i: i for i in range(n + 1)},
    )(*stacks, cw4)
    return outs[:n], outs[n]


def _exchange_halves(grads, small):
    n = len(grads)

    def body(*refs):
        g_refs, small_ref = refs[:n], refs[n]
        r_refs, red_ref = refs[n + 1:2 * n + 1], refs[2 * n + 1]
        gath, send_sems, recv_sems, s_send, s_recv = refs[2 * n + 2:]
        x, y, c, me, chips = _place()
        dev = 4 * x + 2 * y + c
        gath[dev] = small_ref[...]
        sent = []
        for k in range(n):
            cp = pltpu.make_async_remote_copy(
                src_ref=g_refs[k].at[:, :, 1 - c], dst_ref=r_refs[k], send_sem=send_sems.at[k],
                recv_sem=recv_sems.at[k], device_id=(x, y, 1 - c), device_id_type=MESH)
            cp.start()
            sent.append(cp)
        for r in range(1, 8):
            rx, ry, rc = r >> 2, (r >> 1) & 1, r & 1
            peer = (x ^ rx, y ^ ry, c ^ rc)
            cp = pltpu.make_async_remote_copy(
                src_ref=small_ref, dst_ref=gath.at[dev], send_sem=s_send.at[r - 1], recv_sem=s_recv.at[r - 1],
                device_id=peer, device_id_type=MESH)
            cp.start()
            sent.append(cp)
        for r in range(1, 8):
            rx, ry, rc = r >> 2, (r >> 1) & 1, r & 1
            src = 4 * (x ^ rx) + 2 * (y ^ ry) + (c ^ rc)
            pltpu.make_async_remote_copy(
                src_ref=small_ref, dst_ref=gath.at[src], send_sem=s_send.at[r - 1], recv_sem=s_recv.at[r - 1],
                device_id=(x ^ rx, y ^ ry, c ^ rc), device_id_type=MESH).wait_recv()
        total = gath[0]
        for i in range(1, 8):
            total = total + gath[i]
        red_ref[...] = total
        for k in range(n):
            pltpu.make_async_remote_copy(
                src_ref=g_refs[k].at[:, :, 1 - c], dst_ref=r_refs[k], send_sem=send_sems.at[k],
                recv_sem=recv_sems.at[k], device_id=(x, y, 1 - c), device_id_type=MESH).wait_recv()
        for cp in sent:
            cp.wait_send()

    vm = pl.BlockSpec(memory_space=pltpu.VMEM)
    outs = pl.pallas_call(
        body, name="exchange_halves", in_specs=[_ANY] * n + [vm], out_specs=[_ANY] * n + [vm],
        out_shape=[_sds((g.shape[0], g.shape[1]) + g.shape[3:], F32) for g in grads] + [_sds(small.shape, F32)],
        scratch_shapes=[pltpu.VMEM((8,) + small.shape, F32), pltpu.SemaphoreType.DMA((n,)),
                        pltpu.SemaphoreType.DMA((n,)), pltpu.SemaphoreType.DMA((7,)), pltpu.SemaphoreType.DMA((7,))],
    )(*grads, small)
    return outs[:n], outs[n]


def _row_tile(rh):
    for cand in (512, 384, 352, 256, 128):
        if rh % cand == 0:
            return cand
    return rh


def _add_halves(name, g5, recv, place):
    _, _, _, rh, cols = g5.shape
    tr = _row_tile(rh)

    def body(place_ref, g_ref, r_ref, o_ref, own_ref):
        val = (g_ref[...] + r_ref[...]).astype(BF16)
        o_ref[...] = val

        @pl.when(pl.program_id(2) == place_ref[1])
        def _():
            own_ref[...] = val

    grid_spec = pltpu.PrefetchScalarGridSpec(
        num_scalar_prefetch=1, grid=(DEPTH, rh // tr, NCHIP),
        in_specs=[pl.BlockSpec((None, None, None, tr, cols), lambda l, i, j, pr: (j, l, pr[0], i, 0)),
                  pl.BlockSpec((None, None, tr, cols), lambda l, i, j, pr: (j, l, i, 0))],
        out_specs=[pl.BlockSpec((None, None, tr, cols), lambda l, i, j, pr: (j, l, i, 0)),
                   pl.BlockSpec((None, None, tr, cols), lambda l, i, j, pr: (pr[1], l, i, 0))])
    return pl.pallas_call(
        body, name=name, grid_spec=grid_spec, out_shape=[_sds(recv.shape, BF16)] * 2, compiler_params=_params(3),
    )(place, g5, recv)


def _exchange_chips(parts, landing):
    n = len(parts)

    def body(*refs):
        p_refs, r_refs = refs[:n], refs[3 * n:4 * n]
        send_sems, recv_sems = refs[4 * n:]
        x, y, c, me, chips = _place()
        sent = []
        for k in range(n):
            for j, chip in enumerate(chips):
                cp = pltpu.make_async_remote_copy(
                    src_ref=p_refs[k].at[2 * chip[0] + chip[1]], dst_ref=r_refs[k].at[me],
                    send_sem=send_sems.at[3 * k + j], recv_sem=recv_sems.at[3 * k + j],
                    device_id=(*chip, c), device_id_type=MESH)
                cp.start()
                sent.append(cp)
        for k in range(n):
            for j, chip in enumerate(chips):
                slot = r_refs[k].at[2 * chip[0] + chip[1]]
                pltpu.make_async_remote_copy(
                    src_ref=slot, dst_ref=slot, send_sem=send_sems.at[3 * k + j], recv_sem=recv_sems.at[3 * k + j],
                    device_id=(*chip, c), device_id_type=MESH).wait_recv()
        for cp in sent:
            cp.wait_send()

    outs = pl.pallas_call(
        body, name="exchange_chips", in_specs=[_ANY] * (2 * n), out_specs=[_ANY] * (2 * n),
        out_shape=[_sds(p.shape, BF16) for p in parts] * 2,
        scratch_shapes=[pltpu.SemaphoreType.DMA((3 * n,)), pltpu.SemaphoreType.DMA((3 * n,))],
        input_output_aliases={i: i for i in range(2 * n)},
    )(*parts, *landing)
    return outs[n:]


def _sum_chips(name, recv, place):
    _, _, rh, cols = recv.shape
    tr = _row_tile(rh)

    def body(place_ref, r_ref, o_ref):
        total = r_ref[0].astype(F32)
        for j in range(1, NCHIP):
            total = total + r_ref[j].astype(F32)
        o_ref[...] = total

    grid_spec = pltpu.PrefetchScalarGridSpec(
        num_scalar_prefetch=1, grid=(DEPTH, rh // tr),
        in_specs=[pl.BlockSpec((NCHIP, None, tr, cols), lambda l, i, pr: (0, l, i, 0))],
        out_specs=pl.BlockSpec((None, None, tr, cols), lambda l, i, pr: (l, pr[0], i, 0)))
    return pl.pallas_call(
        body, name=name, grid_spec=grid_spec, out_shape=_sds((DEPTH, 2, rh, cols), F32), compiler_params=_params(2),
    )(place, recv)


def _share_halves(full):
    n = len(full)

    def body(*refs):
        f_refs = refs[n:2 * n]
        send_sems, recv_sems = refs[2 * n:]
        x, y, c, me, chips = _place()

        def copy(k, cc):
            part = f_refs[k].at[:, cc]
            return pltpu.make_async_remote_copy(
                src_ref=part, dst_ref=part, send_sem=send_sems.at[k], recv_sem=recv_sems.at[k],
                device_id=(x, y, 1 - c), device_id_type=MESH)

        sent = [copy(k, c) for k in range(n)]
        for cp in sent:
            cp.start()
        for k in range(n):
            copy(k, 1 - c).wait_recv()
        for cp in sent:
            cp.wait_send()

    return pl.pallas_call(
        body, name="share_halves", in_specs=[_ANY] * n, out_specs=[_ANY] * n,
        out_shape=[_sds(f.shape, F32) for f in full],
        scratch_shapes=[pltpu.SemaphoreType.DMA((n,)), pltpu.SemaphoreType.DMA((n,))],
        input_output_aliases={i: i for i in range(n)},
    )(*full)


def _adamw(name, w, g, m, v):
    shape = w.shape
    cols = shape[-1]
    rows = 1
    for s in shape[:-1]:
        rows *= s
    tr = rows
    for cand in (256, 128, 64):
        if rows > cand and rows % cand == 0:
            tr = cand
            break
    c1 = 1.0 / (1.0 - B1 ** STEP)
    c2 = 1.0 / (1.0 - B2 ** STEP)

    def body(w_ref, g_ref, m_ref, v_ref, d_ref, nm_ref, nv_ref):
        gv = g_ref[...]
        nm = B1 * m_ref[...] + (1.0 - B1) * gv
        nv = B2 * v_ref[...] + (1.0 - B2) * (gv * gv)
        nm_ref[...] = nm
        nv_ref[...] = nv
        d_ref[...] = -LR * ((nm * c1) / (jnp.sqrt(nv * c2) + ADAM_EPS) + WD * w_ref[...])

    blk = pl.BlockSpec((tr, cols), lambda i: (i, 0))
    outs = pl.pallas_call(
        body, name=name, grid=(rows // tr,), in_specs=[blk] * 4, out_specs=[blk] * 3,
        out_shape=[_sds((rows, cols), F32)] * 3, compiler_params=_params(1),
    )(*(t.reshape(rows, cols) for t in (w, g, m, v)))
    return tuple(o.reshape(shape) for o in outs)


def _pack_small(small):
    rows = [jnp.sum(small["g_mix%d" % l], axis=0, keepdims=True) for l in range(DEPTH)]
    rows += [jnp.sum(small["pool_scale%d" % l], axis=0, keepdims=True) for l in range(DEPTH)]
    rows += [jnp.sum(small["g_ffn%d" % l], axis=0, keepdims=True) for l in range(DEPTH)]
    rows += [jnp.sum(small["g_ple%d" % l], axis=0, keepdims=True) for l in range(DEPTH)]
    rows += [jnp.sum(small["g_final"], axis=0, keepdims=True)]
    flat = [small["conv_b%d" % l].reshape(-1) for l in range(DEPTH)]
    flat += [small["conv_w%d" % l].reshape(-1) for l in range(DEPTH)]
    flat = jnp.concatenate(flat).reshape(-1, D)
    packed = jnp.concatenate(rows + [flat], axis=0)
    return jnp.pad(packed, ((0, SMALL_ROWS - packed.shape[0]), (0, 0)))


def _unpack_small(red):
    g_mix, pool_scale, g_ffn, g_ple = red[0:2], red[2:4], red[4:6], red[6:8]
    g_final = red[8]
    nb = DEPTH * UW // D
    conv_b = red[9:9 + nb].reshape(DEPTH, UW)
    conv_w = red[9 + nb:9 + 4 * nb].reshape(DEPTH, 3, UW)
    return g_mix, pool_scale, g_ffn, g_ple, g_final, conv_b, conv_w


def kernel(x, p, g_mix, w_in, w_ya, w_yb, pool_w, pool_scale, w_o, g_ffn, w_up, conv_w, conv_b, w_down, g_ple, w_ple, w_ple_gate, g_final, loss_target, m_g_mix, m_w_in, m_w_ya, m_w_yb, m_pool_w, m_pool_scale, m_w_o, m_g_ffn, m_w_up, m_conv_w, m_conv_b, m_w_down, m_g_ple, m_w_ple, m_w_ple_gate, m_g_final, v_g_mix, v_w_in, v_w_ya, v_w_yb, v_pool_w, v_pool_scale, v_w_o, v_g_ffn, v_w_up, v_conv_w, v_conv_b, v_w_down, v_g_ple, v_w_ple, v_w_ple_gate, v_g_final):
    me = 2 * lax.axis_index("x") + lax.axis_index("y")
    place = jnp.stack([lax.axis_index("c"), me]).astype(jnp.int32)

    shards = [
        w_in.astype(BF16), w_up.astype(BF16),
        jnp.concatenate([w_ya, w_ple, pool_w.reshape(DEPTH, 256, 256)], axis=1).astype(BF16),
        jnp.concatenate([w_yb, w_o, w_ple_gate], axis=1).astype(BF16),
        w_down.astype(BF16),
        conv_w.reshape(DEPTH * 3, UP_S),
    ]
    slotted = [lax.dynamic_update_index_in_dim(lax.empty((NCHIP,) + s.shape, s.dtype), s, me, 0) for s in shards]
    gathered, cw4 = _gather_weights(slotted[:5], slotted[5])
    G = dict(zip(_KEYS, gathered))
    cw_full = cw4.reshape(NCHIP, DEPTH, 3, UP_S).transpose(1, 2, 0, 3).reshape(DEPTH, 3, UW)

    vecs = dict(g_mix=g_mix, pool_scale=pool_scale, g_ffn=g_ffn, g_ple=g_ple, g_final=g_final, conv_b=conv_b,
                conv_w=cw_full)
    sq8, grad_x, GG, small = _local_step(x, p, loss_target, vecs, G)
    loss = lax.psum(jnp.sum(sq8) * (0.5 / D), ("x", "y", "c"))

    g5 = [GG[k].reshape(GG[k].shape[:2] + (2, GG[k].shape[2] // 2, GG[k].shape[3])) for k in _KEYS]
    recv1, small_red = _exchange_halves(g5, _pack_small(small))
    summed = [_add_halves("add_halves_" + k, g, r, place) for k, g, r in zip(_KEYS, g5, recv1)]
    recv2 = _exchange_chips([s[0] for s in summed], [s[1] for s in summed])
    halves = [_sum_chips("sum_chips_" + k, r, place) for k, r in zip(_KEYS, recv2)]
    full = [f.reshape(DEPTH, -1, f.shape[-1]) for f in _share_halves(halves)]
    r_in, r_up, r_sc, r_r3, r_dn = full
    d_g_mix, d_pool_scale, d_g_ffn, d_g_ple, d_g_final, d_conv_b, d_conv_w_full = _unpack_small(small_red)
    d_conv_w = lax.dynamic_slice_in_dim(d_conv_w_full, me * UP_S, UP_S, axis=2)

    grads = dict(
        g_mix=d_g_mix, w_in=r_in, w_ya=r_sc[:, 0:512], w_yb=r_r3[:, 0:256],
        pool_w=r_sc[:, 768:1024].reshape(DEPTH, 4, 64, 256), pool_scale=d_pool_scale, w_o=r_r3[:, 256:512],
        g_ffn=d_g_ffn, w_up=r_up, conv_w=d_conv_w, conv_b=d_conv_b, w_down=r_dn, g_ple=d_g_ple,
        w_ple=r_sc[:, 512:768], w_ple_gate=r_r3[:, 512:768], g_final=d_g_final)
    weights = dict(g_mix=g_mix, w_in=w_in, w_ya=w_ya, w_yb=w_yb, pool_w=pool_w, pool_scale=pool_scale, w_o=w_o,
                   g_ffn=g_ffn, w_up=w_up, conv_w=conv_w, conv_b=conv_b, w_down=w_down, g_ple=g_ple, w_ple=w_ple,
                   w_ple_gate=w_ple_gate, g_final=g_final)
    m_in = dict(g_mix=m_g_mix, w_in=m_w_in, w_ya=m_w_ya, w_yb=m_w_yb, pool_w=m_pool_w, pool_scale=m_pool_scale,
                w_o=m_w_o, g_ffn=m_g_ffn, w_up=m_w_up, conv_w=m_conv_w, conv_b=m_conv_b, w_down=m_w_down,
                g_ple=m_g_ple, w_ple=m_w_ple, w_ple_gate=m_w_ple_gate, g_final=m_g_final)
    v_in = dict(g_mix=v_g_mix, w_in=v_w_in, w_ya=v_w_ya, w_yb=v_w_yb, pool_w=v_pool_w, pool_scale=v_pool_scale,
                w_o=v_w_o, g_ffn=v_g_ffn, w_up=v_w_up, conv_w=v_conv_w, conv_b=v_conv_b, w_down=v_w_down,
                g_ple=v_g_ple, w_ple=v_w_ple, w_ple_gate=v_w_ple_gate, g_final=v_g_final)
    names = ["g_mix", "w_in", "w_ya", "w_yb", "pool_w", "pool_scale", "w_o", "g_ffn", "w_up", "conv_w", "conv_b",
             "w_down", "g_ple", "w_ple", "w_ple_gate", "g_final"]
    deltas, new_m, new_v = [], [], []
    for nme in names:
        gr = grads[nme].reshape(weights[nme].shape)
        grads[nme] = gr
        dlt, nm, nv = _adamw("adamw_" + nme, weights[nme], gr, m_in[nme], v_in[nme])
        deltas.append(dlt)
        new_m.append(nm)
        new_v.append(nv)
    return (loss, grad_x, *[grads[nme] for nme in names], *deltas, *new_m, *new_v)
```

```python
import math

import jax
import jax.numpy as jnp
from jax import lax
from jax.experimental import pallas as pl
from jax.experimental.pallas import tpu as pltpu

F32 = jnp.float32
BF16 = jnp.bfloat16
_KEYS = ("in", "up", "sc", "r3", "dn")
MESH = pl.DeviceIdType.MESH

D = 1024
SEQ = 2048
DEPTH = 2
HEAD = 128
GROUP_W = 512
DILATIONS = (1, 4, 16)
ROPE_DIM = 32
ROPE_THETA = 500000.0
NEG_INF = -1e30
ZW = 7680
OFF_K, OFF_V, OFF_U, OFF_GA, OFF_GB = 1536, 3072, 4608, 5632, 6656
FF = 2816
UW = 2 * FF
PLE = 256
NCHIP = 4
IN_S, UP_S, DN_S = ZW // NCHIP, UW // NCHIP, FF // NCHIP
RMS_EPS = 1e-6
LR, B1, B2, ADAM_EPS, WD, STEP = 0.001, 0.9, 0.999, 1e-08, 0.01, 10
SMALL_ROWS = 56
VMEM_CAP = 48 * 1024 * 1024


def _params(n_grid, vmem=VMEM_CAP):
    return pltpu.CompilerParams(dimension_semantics=("arbitrary",) * n_grid, vmem_limit_bytes=vmem)


def _sigmoid(v):
    return 1.0 / (1.0 + jnp.exp(-v))


def _rows8(v):
    return jnp.sum(v.reshape(v.shape[0] // 8, 8, v.shape[1]), axis=0)


def _sds(shape, dtype):
    return jax.ShapeDtypeStruct(shape, dtype)


def _dot(av, bv, ta=False, tb=False):
    dims = (((0,) if ta else (1,), (1,) if tb else (0,)), ((), ()))
    return lax.dot_general(av.astype(BF16), bv.astype(BF16), dims, preferred_element_type=F32)


def _mm_call(name, a, b, *, grid, a_spec, b_spec, o_spec, out_shape, ta=False, tb=False, k_axis=None, nk=1,
             acc_shape=None, res=None, res_spec=None, buf=None, compute=None, comm=None):
    has_res, has_buf = res is not None, buf is not None
    n_comm = len(comm["arrays"]) if comm else 0
    in_place = nk > 1 and not has_res and out_shape.dtype == F32

    def body(*refs):
        a_ref, b_ref = refs[0], refs[1]
        pos = 2
        r_ref = None
        if has_res:
            r_ref = refs[pos]
            pos += 1
        if has_buf:
            pos += 1
        pos += n_comm
        o_ref = refs[pos]
        if comm:
            c_refs = refs[pos + 1:pos + 1 + n_comm]
            send_sems, recv_sems = refs[-2:]
            ids = [pl.program_id(i) for i in range(len(grid))]
            first, last = ids[0] == 0, ids[0] == grid[0] - 1
            for i in range(1, len(grid)):
                first = jnp.logical_and(first, ids[i] == 0)
                last = jnp.logical_and(last, ids[i] == grid[i] - 1)

            @pl.when(first)
            def _():
                comm["start"](c_refs, send_sems, recv_sems)

        if compute is None:
            av = a_ref[...]
            bv = b_ref[...]
            part = _dot(av.reshape(-1, av.shape[-1]), bv.reshape(-1, bv.shape[-1]), ta, tb)
        else:
            part = compute(a_ref, b_ref)

        def finish(val):
            if r_ref is not None:
                val = val + r_ref[...]
            o_ref[...] = val.reshape(o_ref.shape).astype(o_ref.dtype)

        if nk == 1:
            finish(part)
        elif in_place:
            @pl.when(pl.program_id(k_axis) == 0)
            def _():
                o_ref[...] = jnp.zeros(o_ref.shape, F32)

            o_ref[...] += part.reshape(o_ref.shape)
        else:
            acc_ref = refs[pos + 1 + n_comm]
            k = pl.program_id(k_axis)

            @pl.when(k == 0)
            def _():
                acc_ref[...] = jnp.zeros(acc_ref.shape, F32)

            acc_ref[...] += part

            @pl.when(k == nk - 1)
            def _():
                finish(acc_ref[...])

        if comm:
            @pl.when(last)
            def _():
                comm["wait"](c_refs, send_sems, recv_sems)

    ins, in_specs = [a, b], [a_spec, b_spec]
    if has_res:
        ins.append(res)
        in_specs.append(res_spec)
    aliases = {}
    if has_buf:
        aliases = {len(ins): 0}
        ins.append(buf)
        in_specs.append(pl.BlockSpec(memory_space=pl.ANY))
    scratch = [pltpu.VMEM(acc_shape, F32)] if nk > 1 and not in_place else []
    if not comm:
        return pl.pallas_call(
            body, name=name, grid=grid, in_specs=in_specs, out_specs=o_spec, out_shape=out_shape,
            scratch_shapes=scratch, input_output_aliases=aliases, compiler_params=_params(len(grid)),
        )(*ins)
    any_spec = pl.BlockSpec(memory_space=pl.ANY)
    for i, arr in enumerate(comm["arrays"]):
        aliases[len(ins)] = 1 + i
        ins.append(arr)
        in_specs.append(any_spec)
    scratch += [pltpu.SemaphoreType.DMA((comm["nsem"],))] * 2
    outs = pl.pallas_call(
        body, name=name, grid=grid, in_specs=in_specs, out_specs=[o_spec] + [any_spec] * n_comm,
        out_shape=[out_shape] + [_sds(t.shape, t.dtype) for t in comm["arrays"]],
        scratch_shapes=scratch, input_output_aliases=aliases, compiler_params=_params(len(grid)),
    )(*ins)
    comm["done"](outs[1:])
    return outs[0]


def _rms_fwd(name, x, g, tr=512):
    T = x.shape[0]

    def body(x_ref, g_ref, h_ref):
        xv = x_ref[...]
        r = lax.rsqrt(jnp.mean(xv * xv, axis=-1, keepdims=True) + RMS_EPS)
        h_ref[...] = (xv * r * g_ref[...]).astype(BF16)

    return pl.pallas_call(
        body, name=name, grid=(T // tr,),
        in_specs=[pl.BlockSpec((tr, D), lambda i: (i, 0)), pl.BlockSpec((1, D), lambda i: (0, 0))],
        out_specs=pl.BlockSpec((tr, D), lambda i: (i, 0)), out_shape=_sds((T, D), BF16),
        compiler_params=_params(1),
    )(x, g)


def _rms_bwd(name, x, dh, g, dres, tr=512):
    T = x.shape[0]

    def body(x_ref, dh_ref, g_ref, dres_ref, dx_ref, dg_ref):
        xv = x_ref[...]
        r = lax.rsqrt(jnp.mean(xv * xv, axis=-1, keepdims=True) + RMS_EPS)
        xh = xv * r
        dhv = dh_ref[...]
        part = _rows8(dhv * xh)

        @pl.when(pl.program_id(0) == 0)
        def _():
            dg_ref[...] = part

        @pl.when(pl.program_id(0) > 0)
        def _():
            dg_ref[...] += part

        dxh = dhv * g_ref[...]
        dx_ref[...] = dres_ref[...] + r * (dxh - xh * jnp.mean(dxh * xh, axis=-1, keepdims=True))

    row = pl.BlockSpec((tr, D), lambda i: (i, 0))
    return pl.pallas_call(
        body, name=name, grid=(T // tr,),
        in_specs=[row, row, pl.BlockSpec((1, D), lambda i: (0, 0)), row],
        out_specs=[row, pl.BlockSpec((8, D), lambda i: (0, 0))],
        out_shape=[_sds((T, D), F32), _sds((8, D), F32)],
        compiler_params=_params(1),
    )(x, dh, g, dres)


def _final_loss(x, g, tgt, tr=512):
    T = x.shape[0]

    def body(x_ref, g_ref, t_ref, dx_ref, dg_ref, sq_ref):
        xv = x_ref[...]
        r = lax.rsqrt(jnp.mean(xv * xv, axis=-1, keepdims=True) + RMS_EPS)
        xh = xv * r
        gv = g_ref[...]
        e = xh * gv - t_ref[...]
        dy = e * (1.0 / D)
        pg = _rows8(dy * xh)
        ps = _rows8(e * e)

        @pl.when(pl.program_id(0) == 0)
        def _():
            dg_ref[...] = pg
            sq_ref[...] = ps

        @pl.when(pl.program_id(0) > 0)
        def _():
            dg_ref[...] += pg
            sq_ref[...] += ps

        dxh = dy * gv
        dx_ref[...] = r * (dxh - xh * jnp.mean(dxh * xh, axis=-1, keepdims=True))

    row = pl.BlockSpec((tr, D), lambda i: (i, 0))
    acc = pl.BlockSpec((8, D), lambda i: (0, 0))
    return pl.pallas_call(
        body, name="final_loss", grid=(T // tr,),
        in_specs=[row, pl.BlockSpec((1, D), lambda i: (0, 0)), row],
        out_specs=[row, acc, acc],
        out_shape=[_sds((T, D), F32), _sds((8, D), F32), _sds((8, D), F32)],
        compiler_params=_params(1),
    )(x, g, tgt)


def _ple_fwd(name, x, pe, pg, tr=512):
    T = x.shape[0]

    def body(x_ref, pe_ref, pg_ref, o_ref):
        o_ref[...] = x_ref[...] + pe_ref[...] * _sigmoid(pg_ref[...])

    row = pl.BlockSpec((tr, D), lambda i: (i, 0))
    return pl.pallas_call(
        body, name=name, grid=(T // tr,), in_specs=[row, row, row], out_specs=row,
        out_shape=_sds((T, D), F32), compiler_params=_params(1),
    )(x, pe, pg)


def _ple_bwd(name, dx, pe, pg, tr=512):
    T = dx.shape[0]

    def body(dx_ref, pe_ref, pg_ref, dpe_ref, dpg_ref):
        s = _sigmoid(pg_ref[...])
        dxv = dx_ref[...]
        dpe_ref[...] = (dxv * s).astype(BF16)
        dpg_ref[...] = (dxv * pe_ref[...] * s * (1.0 - s)).astype(BF16)

    row = pl.BlockSpec((tr, D), lambda i: (i, 0))
    return pl.pallas_call(
        body, name=name, grid=(T // tr,), in_specs=[row, row, row], out_specs=[row, row],
        out_shape=[_sds((T, D), BF16), _sds((T, D), BF16)], compiler_params=_params(1),
    )(dx, pe, pg)


def _gate_fwd(name, z, ya, yb, tr=512):
    T = z.shape[0]
    w = 512

    def body(ga_ref, gb_ref, ya_ref, yb_ref, o_ref):
        o_ref[...] = (_sigmoid(ga_ref[...]) * ya_ref[...] + _sigmoid(gb_ref[...]) * yb_ref[...]).astype(BF16)

    col = pl.BlockSpec((tr, w), lambda i, j: (i, j))
    return pl.pallas_call(
        body, name=name, grid=(T // tr, D // w),
        in_specs=[pl.BlockSpec((tr, w), lambda i, j: (i, OFF_GA // w + j)),
                  pl.BlockSpec((tr, w), lambda i, j: (i, OFF_GB // w + j)), col, col],
        out_specs=col, out_shape=_sds((T, D), BF16), compiler_params=_params(2),
    )(z, z, ya, yb)


def _gate_bwd(name, z, off, y, dm, dz, tr=512):
    T = z.shape[0]
    w = 512
    has_dz = dz is not None

    def body(*refs):
        g_ref, y_ref, dm_ref = refs[:3]
        dy_ref, dz_ref = refs[-2:]
        s = _sigmoid(g_ref[...])
        dmv = dm_ref[...]
        dy_ref[...] = (dmv * s).astype(BF16)
        dz_ref[...] = (dmv * y_ref[...] * s * (1.0 - s)).astype(BF16)

    col = pl.BlockSpec((tr, w), lambda i, j: (i, j))
    gcol = pl.BlockSpec((tr, w), lambda i, j: (i, off // w + j))
    ins, in_specs, aliases = [z, y, dm], [gcol, col, col], {}
    if has_dz:
        ins.append(dz)
        in_specs.append(pl.BlockSpec(memory_space=pl.ANY))
        aliases = {3: 1}
    return pl.pallas_call(
        body, name=name, grid=(T // tr, D // w), in_specs=in_specs, out_specs=[col, gcol],
        out_shape=[_sds((T, D), BF16), _sds((T, ZW), BF16)], input_output_aliases=aliases,
        compiler_params=_params(2),
    )(*ins)


def _shift_down(v, k, rows):
    return jnp.where(rows >= k, pltpu.roll(v, k, 0), 0.0)


def _shift_up(v, k, rows):
    n = v.shape[0]
    return jnp.where(rows < n - k, pltpu.roll(v, n - k, 0), 0.0)


def _pool_window(v, g, rows, shift):
    s2 = v + shift(v, 1, rows)
    s4 = s2 + shift(s2, 2, rows)
    s8 = s4 + shift(s4, 4, rows)
    s16 = s8 + shift(s8, 8, rows)
    return jnp.where(g == 0, s2, jnp.where(g == 1, s4, jnp.where(g == 2, s8, s16)))


def _pool_count(g, rows):
    wlen = jnp.left_shift(2, g).astype(F32)
    return jnp.minimum(rows.astype(F32) + 1.0, wlen)


def _pool_fwd(name, z3, g_sc, scale):
    Bn = z3.shape[0]
    gw = 256

    def body(u_ref, pw_ref, sc_ref, pooled_ref, ms_ref):
        g = pl.program_id(1)
        u = u_ref[...]
        rows = lax.broadcasted_iota(jnp.int32, u.shape, 0)
        pooled = (_pool_window(u, g, rows, _shift_down) / _pool_count(g, rows) - u).astype(BF16)
        pooled_ref[...] = pooled
        pw = pw_ref[...].reshape(gw, gw)
        mixed = jnp.dot(pooled, pw, preferred_element_type=F32)
        ms_ref[...] = (mixed * sc_ref[...]).astype(BF16)

    blk = pl.BlockSpec((None, SEQ, gw), lambda b, g: (b, 0, g))
    return pl.pallas_call(
        body, name=name, grid=(Bn, 4),
        in_specs=[pl.BlockSpec((None, SEQ, gw), lambda b, g: (b, 0, OFF_U // gw + g)),
                  pl.BlockSpec((NCHIP, 64, gw), lambda b, g: (0, 12 + g, 0)),
                  pl.BlockSpec((1, gw), lambda b, g: (0, g))],
        out_specs=[blk, blk],
        out_shape=[_sds((Bn, SEQ, D), BF16), _sds((Bn, SEQ, D), BF16)],
        compiler_params=_params(2),
    )(z3, g_sc, scale)


def _pool_bwd(name, dms3, pooled3, g_sc, scale, dz3, gg_sc):
    Bn = dms3.shape[0]
    gw = 256
    has_gg = gg_sc is not None

    def body(*refs):
        dms_ref, pooled_ref, pw_ref, sc_ref = refs[:4]
        dz_ref, dpw_ref, dsc_ref = refs[-3:]
        g, b = pl.program_id(0), pl.program_id(1)
        pooled = pooled_ref[...]
        pw = pw_ref[...].reshape(gw, gw)
        dms = dms_ref[...]
        mixed = jnp.dot(pooled, pw, preferred_element_type=F32)
        psc = _rows8(dms * mixed)
        dmixed = (dms * sc_ref[...]).astype(BF16)
        dpw = lax.dot_general(pooled, dmixed, (((0,), (0,)), ((), ())), preferred_element_type=F32)
        dpw = dpw.reshape(NCHIP, 64, gw)

        @pl.when(b == 0)
        def _():
            dsc_ref[...] = psc
            dpw_ref[...] = dpw

        @pl.when(b > 0)
        def _():
            dsc_ref[...] += psc
            dpw_ref[...] += dpw

        dpooled = lax.dot_general(dmixed, pw, (((1,), (1,)), ((), ())), preferred_element_type=F32)
        rows = lax.broadcasted_iota(jnp.int32, dpooled.shape, 0)
        dq = dpooled / _pool_count(g, rows)
        dz_ref[...] = (_pool_window(dq, g, rows, _shift_up) - dpooled).astype(BF16)

    ins = [dms3, pooled3, g_sc, scale, dz3]
    in_specs = [pl.BlockSpec((None, SEQ, gw), lambda g, b: (b, 0, g)),
                pl.BlockSpec((None, SEQ, gw), lambda g, b: (b, 0, g)),
                pl.BlockSpec((NCHIP, 64, gw), lambda g, b: (0, 12 + g, 0)),
                pl.BlockSpec((1, gw), lambda g, b: (0, g)),
                pl.BlockSpec(memory_space=pl.ANY)]
    aliases = {4: 0}
    if has_gg:
        ins.append(gg_sc)
        in_specs.append(pl.BlockSpec(memory_space=pl.ANY))
        aliases[5] = 1
    return pl.pallas_call(
        body, name=name, grid=(4, Bn), in_specs=in_specs,
        out_specs=[pl.BlockSpec((None, SEQ, gw), lambda g, b: (b, 0, OFF_U // gw + g)),
                   pl.BlockSpec((NCHIP, 64, gw), lambda g, b: (0, 12 + g, 0)),
                   pl.BlockSpec((8, gw), lambda g, b: (0, g))],
        out_shape=[_sds(dz3.shape, BF16), _sds((NCHIP, D, 256), F32), _sds((8, D), F32)],
        input_output_aliases=aliases, compiler_params=_params(2),
    )(*ins)


CT = 256
NCT = FF // CT


def _conv_pre(u, cw_ref, cb_ref, rows):
    return (cb_ref[...] + cw_ref[0:1, :] * _shift_down(u, 2, rows) + cw_ref[1:2, :] * _shift_down(u, 1, rows)
            + cw_ref[2:3, :] * u)


def _conv_fwd(name, u3, cw, cb):
    Bn = u3.shape[0]

    def body(ug_ref, uv_ref, cwg_ref, cwv_ref, cbg_ref, cbv_ref, a_ref):
        ug, uv = ug_ref[...], uv_ref[...]
        rows = lax.broadcasted_iota(jnp.int32, ug.shape, 0)
        yg = _conv_pre(ug, cwg_ref, cbg_ref, rows)
        yv = _conv_pre(uv, cwv_ref, cbv_ref, rows)
        a_ref[...] = (yg * _sigmoid(yg) * yv).astype(BF16)

    def blk(off):
        return pl.BlockSpec((None, SEQ, CT), lambda b, c: (b, 0, off + c))

    return pl.pallas_call(
        body, name=name, grid=(Bn, NCT),
        in_specs=[blk(0), blk(NCT),
                  pl.BlockSpec((3, CT), lambda b, c: (0, c)), pl.BlockSpec((3, CT), lambda b, c: (0, NCT + c)),
                  pl.BlockSpec((1, CT), lambda b, c: (0, c)), pl.BlockSpec((1, CT), lambda b, c: (0, NCT + c))],
        out_specs=blk(0), out_shape=_sds((Bn, SEQ, FF), BF16), compiler_params=_params(2),
    )(u3, u3, cw, cw, cb, cb)


def _conv_bwd(name, da3, u3, cw, cb):
    Bn = u3.shape[0]
    last = NCT * Bn - 1

    def body(da_ref, ug_ref, uv_ref, cwg_ref, cwv_ref, cbg_ref, cbv_ref,
             du_ref, dcwg_ref, dcwv_ref, dcbg_ref, dcbv_ref, stage_g, stage_v, sems):
        c, b = pl.program_id(0), pl.program_id(1)
        step = c * Bn + b
        ug, uv, da = ug_ref[...], uv_ref[...], da_ref[...]
        rows = lax.broadcasted_iota(jnp.int32, ug.shape, 0)
        yg = _conv_pre(ug, cwg_ref, cbg_ref, rows)
        yv = _conv_pre(uv, cwv_ref, cbv_ref, rows)
        s = _sigmoid(yg)
        dyv = da * (yg * s)
        dyg = da * yv * (s * (1.0 + yg * (1.0 - s)))

        def writes(off_c, stage, sem):
            col = pl.multiple_of(off_c + c * CT, CT)
            return pltpu.make_async_copy(stage, du_ref.at[b, :, pl.ds(col, CT)], sem)

        @pl.when(step > 0)
        def _():
            writes(0, stage_g, sems.at[0]).wait()
            writes(FF, stage_v, sems.at[1]).wait()

        for dy, u, cw_ref, stage, dcw_ref, dcb_ref in ((dyg, ug, cwg_ref, stage_g, dcwg_ref, dcbg_ref),
                                                       (dyv, uv, cwv_ref, stage_v, dcwv_ref, dcbv_ref)):
            d1, d2 = _shift_up(dy, 1, rows), _shift_up(dy, 2, rows)
            stage[...] = (cw_ref[2:3, :] * dy + cw_ref[1:2, :] * d1 + cw_ref[0:1, :] * d2).astype(BF16)
            dcw = jnp.concatenate([jnp.sum(d2 * u, axis=0, keepdims=True), jnp.sum(d1 * u, axis=0, keepdims=True),
                                   jnp.sum(dy * u, axis=0, keepdims=True)], axis=0)
            dcb = jnp.sum(dy, axis=0, keepdims=True)

            @pl.when(b == 0)
            def _():
                dcw_ref[...] = dcw
                dcb_ref[...] = dcb

            @pl.when(b > 0)
            def _():
                dcw_ref[...] += dcw
                dcb_ref[...] += dcb

        writes(0, stage_g, sems.at[0]).start()
        writes(FF, stage_v, sems.at[1]).start()

        @pl.when(step == last)
        def _():
            writes(0, stage_g, sems.at[0]).wait()
            writes(FF, stage_v, sems.at[1]).wait()

    def blk(off):
        return pl.BlockSpec((None, SEQ, CT), lambda c, b: (b, 0, off + c))

    def vec(r, off):
        return pl.BlockSpec((r, CT), lambda c, b: (0, off + c))

    du3, dcwg, dcwv, dcbg, dcbv = pl.pallas_call(
        body, name=name, grid=(NCT, Bn),
        in_specs=[blk(0), blk(0), blk(NCT), vec(3, 0), vec(3, NCT), vec(1, 0), vec(1, NCT)],
        out_specs=[pl.BlockSpec(memory_space=pl.ANY), vec(3, 0), vec(3, 0), vec(1, 0), vec(1, 0)],
        out_shape=[_sds((Bn, SEQ, UW), BF16), _sds((3, FF), F32), _sds((3, FF), F32), _sds((1, FF), F32),
                   _sds((1, FF), F32)],
        scratch_shapes=[pltpu.VMEM((SEQ, CT), BF16)] * 2 + [pltpu.SemaphoreType.DMA((2,))],
        compiler_params=_params(2),
    )(da3, u3, u3, cw, cw, cb, cb)
    return du3, jnp.concatenate([dcwg, dcwv], axis=1), jnp.concatenate([dcbg, dcbv], axis=1)


def _rope_tables():
    pos = jnp.arange(SEQ, dtype=F32)
    inv_freq = jnp.exp(jnp.arange(0, ROPE_DIM, 2, dtype=F32) * (-math.log(ROPE_THETA) / ROPE_DIM))
    ang = pos[:, None] * inv_freq[None, :]
    cos, sin = jnp.cos(ang), jnp.sin(ang)
    half = ROPE_DIM // 2
    zeros = jnp.zeros((SEQ, HEAD - ROPE_DIM), F32)
    zh = jnp.zeros((SEQ, half), F32)
    tab_c = jnp.concatenate([cos, cos, zeros + 1.0], axis=1)
    tab_a = jnp.concatenate([-sin, zh, zeros], axis=1)
    tab_b = jnp.concatenate([zh, sin, zeros], axis=1)
    return tab_c, tab_a, tab_b


def _rot(v, tc, ta, tb):
    half = ROPE_DIM // 2
    return v * tc + pltpu.roll(v, HEAD - half, 1) * ta + pltpu.roll(v, half, 1) * tb


def _rot_t(dv, tc, ta, tb):
    half = ROPE_DIM // 2
    return dv * tc + pltpu.roll(dv * ta, half, 1) + pltpu.roll(dv * tb, HEAD - half, 1)


def _band_masks():
    qi = lax.broadcasted_iota(jnp.int32, (HEAD, 2 * HEAD), 0)
    ki = lax.broadcasted_iota(jnp.int32, (HEAD, 2 * HEAD), 1)
    diff = HEAD + qi - ki
    both = (diff >= 0) & (diff <= HEAD)
    q1 = lax.broadcasted_iota(jnp.int32, (HEAD, HEAD), 0)
    k1 = lax.broadcasted_iota(jnp.int32, (HEAD, HEAD), 1)
    return q1 >= k1, both


_NT = (((1,), (1,)), ((), ()))
_TN = (((0,), (0,)), ((), ()))
_SCALE = HEAD ** -0.5


ATT_W = HEAD
ATT_HP = ATT_W // HEAD


def _res_rows(r, n, d, base=0):
    return pl.ds(base * d + r, n, stride=d) if d > 1 else pl.ds(base, n)


def _attn_load(q_ref, k_ref, v_ref, tc_ref, ta_ref, tb_ref, qs, ks, vs, d):
    L = SEQ // d
    for r in range(d):
        rows = _res_rows(r, L, d)
        tc, ta, tb = tc_ref[rows, :], ta_ref[rows, :], tb_ref[rows, :]
        dst = slice(r * L, (r + 1) * L)
        for hh in range(ATT_HP):
            sl = slice(hh * HEAD, (hh + 1) * HEAD)
            qs[dst, sl] = _rot(q_ref[rows, sl], tc, ta, tb).astype(BF16)
            ks[dst, sl] = _rot(k_ref[rows, sl], tc, ta, tb).astype(BF16)
            vs[dst, sl] = v_ref[rows, sl].astype(BF16)


def _attn_fwd(name, z3, tabs, g, d):
    Bn = z3.shape[0]
    L = SEQ // d
    nb = L // HEAD
    W, nh = ATT_W, GROUP_W // ATT_W

    def body(q_ref, k_ref, v_ref, tc_ref, ta_ref, tb_ref, o_ref, l_ref, qs, ks, vs):
        m_first, m_both = _band_masks()
        _attn_load(q_ref, k_ref, v_ref, tc_ref, ta_ref, tb_ref, qs, ks, vs, d)
        for r in range(d):
            for hh in range(ATT_HP):
                sl = slice(hh * HEAD, (hh + 1) * HEAD)
                for n in range(nb):
                    rq = slice(r * L + n * HEAD, r * L + (n + 1) * HEAD)
                    rk = slice(r * L + max(n - 1, 0) * HEAD, r * L + (n + 1) * HEAD)
                    s = lax.dot_general(qs[rq, sl], ks[rk, sl], _NT, preferred_element_type=F32) * _SCALE
                    s = jnp.where(m_first if n == 0 else m_both, s, NEG_INF)
                    m = jnp.max(s, axis=-1, keepdims=True)
                    e = jnp.exp(s - m)
                    den = jnp.sum(e, axis=-1, keepdims=True)
                    p = (e * (1.0 / den)).astype(BF16)
                    rows = _res_rows(r, HEAD, d, n * HEAD)
                    o_ref[rows, sl] = jnp.dot(p, vs[rk, sl], preferred_element_type=F32)
                    l_ref[rows, sl] = jnp.broadcast_to(m + jnp.log(den), (HEAD, HEAD))

    def zcol(off):
        return pl.BlockSpec((None, SEQ, W), lambda b, h: (b, 0, (off + g * GROUP_W) // W + h))

    tab = pl.BlockSpec((SEQ, HEAD), lambda b, h: (0, 0))
    out = pl.BlockSpec((None, SEQ, W), lambda b, h: (b, 0, h))
    return pl.pallas_call(
        body, name=name, grid=(Bn, nh),
        in_specs=[zcol(0), zcol(OFF_K), zcol(OFF_V), tab, tab, tab],
        out_specs=[out, out],
        out_shape=[_sds((Bn, SEQ, GROUP_W), F32), _sds((Bn, SEQ, GROUP_W), F32)],
        scratch_shapes=[pltpu.VMEM((SEQ, W), BF16)] * 3,
        compiler_params=_params(2),
    )(z3, z3, z3, *tabs)


def _attn_bwd(name, z3, tabs, g, d, do3, lse3, delta3, dz3):
    Bn = z3.shape[0]
    L = SEQ // d
    nb = L // HEAD
    W, nh = ATT_W, GROUP_W // ATT_W

    def body(q_ref, k_ref, v_ref, tc_ref, ta_ref, tb_ref, do_ref, l_ref, dl_ref, dz_in, dz_ref,
             qs, ks, vs, dos, dqs, dks, dvs, nat, oq, ok, ov, sems):
        b, h = pl.program_id(0), pl.program_id(1)
        m_first, m_both = _band_masks()
        _attn_load(q_ref, k_ref, v_ref, tc_ref, ta_ref, tb_ref, qs, ks, vs, d)
        for r in range(d):
            dos[r * L:(r + 1) * L, :] = do_ref[_res_rows(r, L, d), :].astype(BF16)
        dks[...] = jnp.zeros_like(dks)
        dvs[...] = jnp.zeros_like(dvs)
        for r in range(d):
            for hh in range(ATT_HP):
                sl = slice(hh * HEAD, (hh + 1) * HEAD)
                for n in range(nb):
                    rq = slice(r * L + n * HEAD, r * L + (n + 1) * HEAD)
                    rk = slice(r * L + max(n - 1, 0) * HEAD, r * L + (n + 1) * HEAD)
                    rows = _res_rows(r, HEAD, d, n * HEAD)
                    qb, kk, vv, dob = qs[rq, sl], ks[rk, sl], vs[rk, sl], dos[rq, sl]
                    s = lax.dot_general(qb, kk, _NT, preferred_element_type=F32) * _SCALE
                    s = jnp.where(m_first if n == 0 else m_both, s, NEG_INF)
                    p = jnp.exp(s - l_ref[rows, sl][:, 0:1])
                    dp = lax.dot_general(dob, vv, _NT, preferred_element_type=F32)
                    ds = (p * (dp - dl_ref[rows, sl][:, 0:1]) * _SCALE).astype(BF16)
                    dqs[rq, sl] = jnp.dot(ds, kk, preferred_element_type=F32)
                    dks[rk, sl] += lax.dot_general(ds, qb, _TN, preferred_element_type=F32)
                    dvs[rk, sl] += lax.dot_general(p.astype(BF16), dob, _TN, preferred_element_type=F32)
        tc, ta, tb = tc_ref[...], ta_ref[...], tb_ref[...]
        step = b * nh + h

        def writes():
            base = g * GROUP_W + h * W
            return [pltpu.make_async_copy(src, dz_ref.at[b, :, pl.ds(pl.multiple_of(base + off, HEAD), W)],
                                          sems.at[i])
                    for i, (src, off) in enumerate(((oq, 0), (ok, OFF_K), (ov, OFF_V)))]

        @pl.when(step > 0)
        def _():
            for cp in writes():
                cp.wait()

        for src, dst, rotate in ((dqs, oq, True), (dks, ok, True), (dvs, ov, False)):
            for r in range(d):
                nat[_res_rows(r, L, d), :] = src[r * L:(r + 1) * L, :]
            for hh in range(ATT_HP):
                sl = slice(hh * HEAD, (hh + 1) * HEAD)
                val = nat[:, sl]
                dst[:, sl] = (_rot_t(val, tc, ta, tb) if rotate else val).astype(BF16)
        for cp in writes():
            cp.start()

        @pl.when(step == Bn * nh - 1)
        def _():
            for cp in writes():
                cp.wait()

    def zcol(off):
        return pl.BlockSpec((None, SEQ, W), lambda b, h: (b, 0, (off + g * GROUP_W) // W + h))

    tab = pl.BlockSpec((SEQ, HEAD), lambda b, h: (0, 0))
    gcol = pl.BlockSpec((None, SEQ, W), lambda b, h: (b, 0, h))
    any_spec = pl.BlockSpec(memory_space=pl.ANY)
    return pl.pallas_call(
        body, name=name, grid=(Bn, nh),
        in_specs=[zcol(0), zcol(OFF_K), zcol(OFF_V), tab, tab, tab, gcol, gcol, gcol, any_spec],
        out_specs=any_spec,
        out_shape=_sds((Bn, SEQ, ZW), BF16),
        scratch_shapes=[pltpu.VMEM((SEQ, W), BF16)] * 4 + [pltpu.VMEM((SEQ, W), F32)] * 4
        + [pltpu.VMEM((SEQ, W), BF16)] * 3 + [pltpu.SemaphoreType.DMA((3,))],
        input_output_aliases={9: 0}, compiler_params=_params(2),
    )(z3, z3, z3, *tabs, do3, lse3, delta3, dz3)


def _merge_weights(l0, l1, l2):
    m = jnp.maximum(jnp.maximum(l0, l1), l2)
    e0, e1, e2 = jnp.exp(l0 - m), jnp.exp(l1 - m), jnp.exp(l2 - m)
    inv = 1.0 / (e0 + e1 + e2)
    return e0 * inv, e1 * inv, e2 * inv


def _merge_fwd(name, outs, lses, tr=512):
    T = outs[0].shape[0]

    def body(o0, o1, o2, l0, l1, l2, a_ref):
        w0, w1, w2 = _merge_weights(l0[...], l1[...], l2[...])
        a_ref[...] = (w0 * o0[...] + w1 * o1[...] + w2 * o2[...]).astype(BF16)

    row = pl.BlockSpec((tr, GROUP_W), lambda i: (i, 0))
    return pl.pallas_call(
        body, name=name, grid=(T // tr,), in_specs=[row] * 6, out_specs=row,
        out_shape=_sds((T, GROUP_W), BF16), compiler_params=_params(1),
    )(*outs, *lses)


def _merge_bwd(name, outs, lses, dattn, tr=512):
    T = outs[0].shape[0]

    def body(o0, o1, o2, l0, l1, l2, da_ref, d0, d1, d2, e0, e1, e2):
        w = _merge_weights(l0[...], l1[...], l2[...])
        da = da_ref[...]
        attn = w[0] * o0[...] + w[1] * o1[...] + w[2] * o2[...]
        prod = da * attn
        csum = jnp.concatenate(
            [jnp.broadcast_to(jnp.sum(prod[:, hh * HEAD:(hh + 1) * HEAD], axis=-1, keepdims=True), (tr, HEAD))
             for hh in range(GROUP_W // HEAD)], axis=1)
        for wg, d_ref, e_ref in zip(w, (d0, d1, d2), (e0, e1, e2)):
            d_ref[...] = wg * da
            e_ref[...] = wg * csum

    row = pl.BlockSpec((tr, GROUP_W), lambda i: (i, 0))
    res = pl.pallas_call(
        body, name=name, grid=(T // tr,), in_specs=[row] * 7, out_specs=[row] * 6,
        out_shape=[_sds((T, GROUP_W), F32)] * 6,
        compiler_params=_params(1),
    )(*outs, *lses, dattn)
    return res[:3], res[3:]


def _local_step(x3, p4, tgt3, vecs, ex):
    Bn = x3.shape[0]
    T = Bn * SEQ
    x = x3.reshape(T, D)
    tgt = tgt3.reshape(T, D)
    pb = p4.astype(BF16).reshape(DEPTH, T, PLE)
    tabs = _rope_tables()
    tm = 1024 if T % 1024 == 0 else 512
    nt = T // tm
    tk = 1024 if T % 1024 == 0 else 512
    ntk = T // tk
    tm5 = 512
    f32o = lambda n: _sds((T, n), F32)

    def spec(shape, fn):
        return pl.BlockSpec(shape, fn)

    def _mm(name, *args, **kwargs):
        return _mm_call(name, *args, comm=ex.hook(name), **kwargs)

    def cols4(a_ref, b_ref):
        av = a_ref[...]
        return jnp.concatenate([_dot(av, b_ref[j]) for j in range(NCHIP)], axis=1)

    def rows4(a_ref, b_ref):
        av = a_ref[...]
        return jnp.concatenate([_dot(av, b_ref[:, j * 256:(j + 1) * 256], ta=True) for j in range(NCHIP)], axis=0)

    def kchunks4(a_ref, b_ref):
        total = _dot(a_ref[:, 0:256], b_ref[0], tb=True)
        for j in range(1, NCHIP):
            total = total + _dot(a_ref[:, j * 256:(j + 1) * 256], b_ref[j], tb=True)
        return total

    saved = []
    for l in range(DEPTH):
        L = str(l)
        G = ex.weights(l)
        g_mix, g_ffn, g_ple = (vecs[k][l:l + 1] for k in ("g_mix", "g_ffn", "g_ple"))
        pscale, cb, cw = vecs["pool_scale"][l:l + 1], vecs["conv_b"][l:l + 1], vecs["conv_w"][l]
        h = _rms_fwd("rms_mix" + L, x, g_mix)
        z = _mm("mm_z" + L, h, G["in"], grid=(T // tm5, NCHIP),
                a_spec=spec((tm5, D), lambda i, n: (i, 0)),
                b_spec=spec((None, D, IN_S), lambda i, n: (n, 0, 0)),
                o_spec=spec((tm5, IN_S), lambda i, n: (i, n)), out_shape=f32o(ZW))
        z3 = z.reshape(Bn, SEQ, ZW)
        outs, lses = [], []
        for g, d in enumerate(DILATIONS):
            o_g, l_g = _attn_fwd("attn_fwd%d_%d" % (g, l), z3, tabs, g, d)
            outs.append(o_g.reshape(T, GROUP_W))
            lses.append(l_g.reshape(T, GROUP_W))
        attn = _merge_fwd("merge_fwd" + L, outs, lses)
        ya = _mm("mm_ya" + L, attn, G["sc"], grid=(nt,), compute=cols4,
                 a_spec=spec((tm, GROUP_W), lambda i: (i, 0)),
                 b_spec=spec((NCHIP, GROUP_W, 256), lambda i: (0, 0, 0)),
                 o_spec=spec((tm, D), lambda i: (i, 0)), out_shape=f32o(D))
        pooled3, ms3 = _pool_fwd("pool_fwd" + L, z3, G["sc"], pscale)
        ms = ms3.reshape(T, D)

        def row_sharded(name, a, rb, res=None, kdim=D):
            if rb is None:
                b_arr, b_spec = G["dn"], spec((NCHIP, DN_S, D), lambda i: (0, 0, 0))
            else:
                b_arr, b_spec = G["r3"], spec((NCHIP, 256, D), lambda i: (0, rb, 0))
            return _mm(name, a, b_arr, grid=(T // tm5,),
                       a_spec=spec((tm5, kdim), lambda i: (i, 0)), b_spec=b_spec,
                       o_spec=spec((tm5, D), lambda i: (i, 0)), out_shape=f32o(D),
                       res=res, res_spec=None if res is None else spec((tm5, D), lambda i: (i, 0)))

        yb = row_sharded("mm_yb" + L, ms, 0)
        merged = _gate_fwd("gate_fwd" + L, z, ya, yb)
        x1 = row_sharded("mm_o" + L, merged, 1, res=x)
        h2 = _rms_fwd("rms_ffn" + L, x1, g_ffn)
        u = _mm("mm_up" + L, h2, G["up"], grid=(T // tm5, NCHIP),
                a_spec=spec((tm5, D), lambda i, n: (i, 0)),
                b_spec=spec((None, D, UP_S), lambda i, n: (n, 0, 0)),
                o_spec=spec((tm5, UP_S), lambda i, n: (i, n)), out_shape=f32o(UW))
        u3 = u.reshape(Bn, SEQ, UW)
        act = _conv_fwd("conv_fwd" + L, u3, cw, cb).reshape(T, FF)
        x2 = row_sharded("mm_down" + L, act, None, res=x1, kdim=FF)
        h3 = _rms_fwd("rms_ple" + L, x2, g_ple)
        pg = row_sharded("mm_pg" + L, h3, 2)
        pe = _mm("mm_pe" + L, pb[l], G["sc"], grid=(nt,), compute=cols4,
                 a_spec=spec((tm, PLE), lambda i: (i, 0)),
                 b_spec=spec((NCHIP, 256, 256), lambda i: (0, 2, 0)),
                 o_spec=spec((tm, D), lambda i: (i, 0)), out_shape=f32o(D))
        x3n = _ple_fwd("ple_fwd" + L, x2, pe, pg)
        saved.append(dict(x=x, h=h, z=z, outs=outs, lses=lses, attn=attn, ya=ya, yb=yb, pooled3=pooled3, ms=ms,
                          merged=merged, x1=x1, h2=h2, u3=u3, act=act, x2=x2, h3=h3, pg=pg, pe=pe))
        x = x3n

    dx, dg_final8, sq8 = _final_loss(x, vecs["g_final"].reshape(1, D), tgt)

    gg_shape = {k: _sds(G[k].shape, F32) for k in G}
    small = {"g_final": dg_final8}

    for l in reversed(range(DEPTH)):
        L = str(l)
        sv = saved[l]
        G = ex.weights(l)
        GG = dict.fromkeys(_KEYS)
        g_mix, g_ffn, g_ple = (vecs[k][l:l + 1] for k in ("g_mix", "g_ffn", "g_ple"))
        pscale, cb, cw = vecs["pool_scale"][l:l + 1], vecs["conv_b"][l:l + 1], vecs["conv_w"][l]

        def wgrad_rows(name, a, b_arr, key, rb):
            GG[key] = _mm(name, a, b_arr, grid=(2, ntk), ta=True, k_axis=1, nk=ntk, acc_shape=(D, 512),
                          a_spec=spec((tk, D), lambda n, k: (k, 0)),
                          b_spec=spec((tk, 512), lambda n, k: (k, n)),
                          o_spec=spec((NCHIP, 256, 512), lambda n, k: (0, rb, n)),
                          out_shape=gg_shape[key], buf=GG[key])

        def dgrad_rows(name, dy, rb):
            return _mm(name, dy, G["r3"], grid=(T // tm5,), tb=True,
                       a_spec=spec((tm5, D), lambda i: (i, 0)),
                       b_spec=spec((NCHIP, 256, D), lambda i: (0, rb, 0)),
                       o_spec=spec((tm5, D), lambda i: (i, 0)), out_shape=f32o(D))

        dpe, dpg = _ple_bwd("ple_bwd" + L, dx, sv["pe"], sv["pg"])
        GG["sc"] = _mm("wg_ple" + L, pb[l], dpe, grid=(ntk,), compute=rows4, k_axis=0, nk=ntk,
                       acc_shape=(NCHIP * PLE, 256),
                       a_spec=spec((tk, PLE), lambda k: (k, 0)), b_spec=spec((tk, D), lambda k: (k, 0)),
                       o_spec=spec((NCHIP, 256, 256), lambda k: (0, 2, 0)),
                       out_shape=gg_shape["sc"], buf=GG["sc"])
        wgrad_rows("wg_pg" + L, sv["h3"], dpg, "r3", 2)
        dh3 = dgrad_rows("dg_pg" + L, dpg, 2)
        dx, small["g_ple" + L] = _rms_bwd("rms_ple_bwd" + L, sv["x2"], dh3, g_ple, dx)

        da = _mm("dg_down" + L, dx, G["dn"], grid=(T // 256,), tb=True,
                 a_spec=spec((256, D), lambda i: (i, 0)),
                 b_spec=spec((NCHIP, DN_S, D), lambda i: (0, 0, 0)),
                 o_spec=spec((256, FF), lambda i: (i, 0)), out_shape=f32o(FF))
        GG["dn"] = _mm("wg_down" + L, sv["act"], dx, grid=(2, ntk), ta=True, k_axis=1, nk=ntk,
                       acc_shape=(FF, 512),
                       a_spec=spec((tk, FF), lambda n, k: (k, 0)), b_spec=spec((tk, 512), lambda n, k: (k, n)),
                       o_spec=spec((NCHIP, DN_S, 512), lambda n, k: (0, 0, n)),
                       out_shape=gg_shape["dn"], buf=GG["dn"])
        du3, dcw, dcb = _conv_bwd("conv_bwd" + L, da.reshape(Bn, SEQ, FF), sv["u3"], cw, cb)
        small["conv_w" + L], small["conv_b" + L] = dcw, dcb
        du = du3.reshape(T, UW)
        dh2 = _mm("dg_up" + L, du, G["up"], grid=(T // tm5, NCHIP), tb=True, k_axis=1, nk=NCHIP, acc_shape=(tm5, D),
                  a_spec=spec((tm5, UP_S), lambda i, k: (i, k)),
                  b_spec=spec((None, D, UP_S), lambda i, k: (k, 0, 0)),
                  o_spec=spec((tm5, D), lambda i, k: (i, 0)), out_shape=f32o(D))
        GG["up"] = _mm("wg_up" + L, sv["h2"], du, grid=(NCHIP, ntk), ta=True, k_axis=1, nk=ntk,
                       acc_shape=(D, UP_S),
                       a_spec=spec((tk, D), lambda j, k: (k, 0)),
                       b_spec=spec((tk, UP_S), lambda j, k: (k, j)),
                       o_spec=spec((None, D, UP_S), lambda j, k: (j, 0, 0)),
                       out_shape=gg_shape["up"], buf=GG["up"])
        dx, small["g_ffn" + L] = _rms_bwd("rms_ffn_bwd" + L, sv["x1"], dh2, g_ffn, dx)

        dmerged = dgrad_rows("dg_o" + L, dx, 1)
        wgrad_rows("wg_o" + L, sv["merged"], dx, "r3", 1)
        dya, dz = _gate_bwd("gate_bwd_a" + L, sv["z"], OFF_GA, sv["ya"], dmerged, None)
        dyb, dz = _gate_bwd("gate_bwd_b" + L, sv["z"], OFF_GB, sv["yb"], dmerged, dz)
        dms = dgrad_rows("dg_yb" + L, dyb, 0)
        wgrad_rows("wg_yb" + L, sv["ms"], dyb, "r3", 0)
        dz3, GG["sc"], small["pool_scale" + L] = _pool_bwd(
            "pool_bwd" + L, dms.reshape(Bn, SEQ, D), sv["pooled3"], G["sc"], pscale,
            dz.reshape(Bn, SEQ, ZW), GG["sc"])
        dattn = _mm("dg_ya" + L, dya, G["sc"], grid=(nt,), compute=kchunks4,
                    a_spec=spec((tm, D), lambda i: (i, 0)),
                    b_spec=spec((NCHIP, GROUP_W, 256), lambda i: (0, 0, 0)),
                    o_spec=spec((tm, GROUP_W), lambda i: (i, 0)), out_shape=f32o(GROUP_W))
        GG["sc"] = _mm("wg_ya" + L, sv["attn"], dya, grid=(ntk,), compute=rows4, k_axis=0, nk=ntk,
                       acc_shape=(NCHIP * GROUP_W, 256),
                       a_spec=spec((tk, GROUP_W), lambda k: (k, 0)), b_spec=spec((tk, D), lambda k: (k, 0)),
                       o_spec=spec((NCHIP, GROUP_W, 256), lambda k: (0, 0, 0)),
                       out_shape=gg_shape["sc"], buf=GG["sc"])
        dos, deltas = _merge_bwd("merge_bwd" + L, sv["outs"], sv["lses"], dattn)
        view3 = lambda t: t.reshape(Bn, SEQ, GROUP_W)
        sz3 = sv["z"].reshape(Bn, SEQ, ZW)
        for g, d in enumerate(DILATIONS):
            dz3 = _attn_bwd("attn_bwd%d_%d" % (g, l), sz3, tabs, g, d, view3(dos[g]), view3(sv["lses"][g]),
                            view3(deltas[g]), dz3)
        dz = dz3.reshape(T, ZW)
        dh = _mm("dg_z" + L, dz, G["in"], grid=(T // tm5, NCHIP), tb=True, k_axis=1, nk=NCHIP, acc_shape=(tm5, D),
                 a_spec=spec((tm5, IN_S), lambda i, k: (i, k)),
                 b_spec=spec((None, D, IN_S), lambda i, k: (k, 0, 0)),
                 o_spec=spec((tm5, D), lambda i, k: (i, 0)), out_shape=f32o(D))
        GG["in"] = _mm("wg_z" + L, sv["h"], dz, grid=(NCHIP, ntk), ta=True, k_axis=1, nk=ntk,
                       acc_shape=(D, IN_S),
                       a_spec=spec((tk, D), lambda n, k: (k, 0)), b_spec=spec((tk, IN_S), lambda n, k: (k, n)),
                       o_spec=spec((None, D, IN_S), lambda n, k: (n, 0, 0)),
                       out_shape=gg_shape["in"], buf=GG["in"])
        dx, small["g_mix" + L] = _rms_bwd("rms_mix_bwd" + L, sv["x"], dh, g_mix, dx)
        ex.grads_ready(l, GG)

    return sq8, dx.reshape(Bn, SEQ, D), small


_ANY = pl.BlockSpec(memory_space=pl.ANY)


def _place():
    x, y, c = lax.axis_index("x"), lax.axis_index("y"), lax.axis_index("c")
    chips = [(1 - x, y), (x, 1 - y), (1 - x, 1 - y)]
    return x, y, c, 2 * x + y, chips


def _half(rows, cc):
    return pl.ds(cc * (rows // 2), rows // 2)


def _remote(src, dst, send_sems, recv_sems, i, to):
    return pltpu.make_async_remote_copy(src_ref=src, dst_ref=dst, send_sem=send_sems.at[i], recv_sem=recv_sems.at[i],
                                        device_id=to, device_id_type=MESH)


def _exchange_gather_ici(stacks, done):
    n = len(stacks)
    rows = [t.shape[1] for t in stacks]

    def start(refs, send_sems, recv_sems):
        x, y, c, me, chips = _place()
        for k in range(n):
            part = refs[k].at[me, _half(rows[k], c)]
            for j, chip in enumerate(chips):
                _remote(part, part, send_sems, recv_sems, 3 * k + j, (*chip, c)).start()

    def wait(refs, send_sems, recv_sems):
        x, y, c, me, chips = _place()
        for k in range(n):
            for j, chip in enumerate(chips):
                part = refs[k].at[2 * chip[0] + chip[1], _half(rows[k], c)]
                _remote(part, part, send_sems, recv_sems, 3 * k + j, (*chip, c)).wait()

    return dict(arrays=list(stacks), nsem=3 * n, start=start, wait=wait, done=done)


def _exchange_gather_d2d(stacks, done):
    n = len(stacks)
    rows = [t.shape[1] for t in stacks]

    def copies(refs, send_sems, recv_sems, mine):
        x, y, c, me, chips = _place()
        cc = c if mine else 1 - c
        return [_remote(part, part, send_sems, recv_sems, 3 * k + j, (x, y, 1 - c))
                for k in range(n) for j, chip in enumerate(chips)
                for part in [refs[k].at[2 * chip[0] + chip[1], _half(rows[k], cc)]]]

    def start(refs, send_sems, recv_sems):
        for cp in copies(refs, send_sems, recv_sems, True):
            cp.start()

    def wait(refs, send_sems, recv_sems):
        for cp in copies(refs, send_sems, recv_sems, False):
            cp.wait()

    return dict(arrays=list(stacks), nsem=3 * n, start=start, wait=wait, done=done)


def _exchange_halves(g5, recv, done):
    n = len(g5)

    def copies(refs, send_sems, recv_sems):
        x, y, c, me, chips = _place()
        return [_remote(refs[k].at[:, 1 - c], refs[n + k], send_sems, recv_sems, k, (x, y, 1 - c)) for k in range(n)]

    def start(refs, send_sems, recv_sems):
        for cp in copies(refs, send_sems, recv_sems):
            cp.start()

    def wait(refs, send_sems, recv_sems):
        for cp in copies(refs, send_sems, recv_sems):
            cp.wait()

    return dict(arrays=list(g5) + list(recv), nsem=n, start=start, wait=wait, done=done)


def _exchange_chips(parts, landing, done):
    n = len(parts)

    def start(refs, send_sems, recv_sems):
        x, y, c, me, chips = _place()
        for k in range(n):
            for j, chip in enumerate(chips):
                _remote(refs[k].at[2 * chip[0] + chip[1]], refs[n + k].at[me], send_sems, recv_sems, 3 * k + j,
                        (*chip, c)).start()

    def wait(refs, send_sems, recv_sems):
        x, y, c, me, chips = _place()
        for k in range(n):
            for j, chip in enumerate(chips):
                slot = refs[n + k].at[2 * chip[0] + chip[1]]
                _remote(slot, slot, send_sems, recv_sems, 3 * k + j, (*chip, c)).wait()

    return dict(arrays=list(parts) + list(landing), nsem=3 * n, start=start, wait=wait, done=done)


def _exchange_share(full, layer, done):
    n = len(full)

    def copies(refs, send_sems, recv_sems, mine):
        x, y, c, me, chips = _place()
        cc = c if mine else 1 - c
        return [_remote(part, part, send_sems, recv_sems, k, (x, y, 1 - c))
                for k in range(n) for part in [refs[k].at[layer, cc]]]

    def start(refs, send_sems, recv_sems):
        for cp in copies(refs, send_sems, recv_sems, True):
            cp.start()

    def wait(refs, send_sems, recv_sems):
        for cp in copies(refs, send_sems, recv_sems, False):
            cp.wait()

    return dict(arrays=list(full), nsem=n, start=start, wait=wait, done=done)


def _exchange_call(name, comm):
    arrays = comm["arrays"]
    n = len(arrays)

    def body(*refs):
        outs, send_sems, recv_sems = refs[n:2 * n], refs[2 * n], refs[2 * n + 1]
        comm["start"](outs, send_sems, recv_sems)
        comm["wait"](outs, send_sems, recv_sems)

    outs = pl.pallas_call(
        body, name=name, in_specs=[_ANY] * n, out_specs=[_ANY] * n,
        out_shape=[_sds(t.shape, t.dtype) for t in arrays],
        scratch_shapes=[pltpu.SemaphoreType.DMA((comm["nsem"],))] * 2,
        input_output_aliases={i: i for i in range(n)},
    )(*arrays)
    comm["done"](outs)


def _gather_first(stacks, cw4):
    n = len(stacks)
    ici = _exchange_gather_ici(stacks, None)
    d2d = _exchange_gather_d2d(stacks, None)
    rows = [t.shape[1] for t in stacks]

    def body(*refs):
        g_refs, cwg_ref = refs[n + 1:2 * n + 1], refs[2 * n + 1]
        s_ici, r_ici, s_d2d, r_d2d, s_cw, r_cw = refs[2 * n + 2:]
        x, y, c, me, chips = _place()

        def cw_copy(j, slot, chip):
            part = cwg_ref.at[slot]
            return _remote(part, part, s_cw, r_cw, j, (*chip, c))

        ici["start"](g_refs, s_ici, r_ici)
        for j, chip in enumerate(chips):
            cw_copy(j, me, chip).start()
        for k in range(n):
            for j, chip in enumerate(chips):
                part = g_refs[k].at[2 * chip[0] + chip[1], _half(rows[k], c)]
                _remote(part, part, s_ici, r_ici, 3 * k + j, (*chip, c)).wait()
                _remote(part, part, s_d2d, r_d2d, 3 * k + j, (x, y, 1 - c)).start()
        d2d["wait"](g_refs, s_d2d, r_d2d)
        for j, chip in enumerate(chips):
            cw_copy(j, 2 * chip[0] + chip[1], chip).wait()

    outs = pl.pallas_call(
        body, name="gather_first", in_specs=[_ANY] * (n + 1), out_specs=[_ANY] * (n + 1),
        out_shape=[_sds(t.shape, t.dtype) for t in stacks] + [_sds(cw4.shape, cw4.dtype)],
        scratch_shapes=[pltpu.SemaphoreType.DMA((3 * n,))] * 4 + [pltpu.SemaphoreType.DMA((3,))] * 2,
        input_output_aliases={i: i for i in range(n + 1)},
    )(*stacks, cw4)
    return outs[:n], outs[n]


def _halves_and_small(g5, recv, small):
    n = len(g5)
    halves = _exchange_halves(g5, recv, None)

    def body(*refs):
        small_ref = refs[2 * n]
        c_refs, red_ref = refs[2 * n + 1:4 * n + 1], refs[4 * n + 1]
        gath, send_sems, recv_sems, s_send, s_recv = refs[4 * n + 2:]
        x, y, c, me, chips = _place()
        dev = 4 * x + 2 * y + c
        gath[dev] = small_ref[...]
        halves["start"](c_refs, send_sems, recv_sems)
        for r in range(1, 8):
            peer = (x ^ (r >> 2), y ^ ((r >> 1) & 1), c ^ (r & 1))
            _remote(small_ref, gath.at[dev], s_send, s_recv, r - 1, peer).start()
        for r in range(1, 8):
            peer = (x ^ (r >> 2), y ^ ((r >> 1) & 1), c ^ (r & 1))
            src = 4 * peer[0] + 2 * peer[1] + peer[2]
            _remote(small_ref, gath.at[src], s_send, s_recv, r - 1, peer).wait()
        total = gath[0]
        for i in range(1, 8):
            total = total + gath[i]
        red_ref[...] = total
        halves["wait"](c_refs, send_sems, recv_sems)

    vm = pl.BlockSpec(memory_space=pltpu.VMEM)
    arrays = list(g5) + list(recv)
    outs = pl.pallas_call(
        body, name="halves_and_small", in_specs=[_ANY] * (2 * n) + [vm], out_specs=[_ANY] * (2 * n) + [vm],
        out_shape=[_sds(t.shape, t.dtype) for t in arrays] + [_sds(small.shape, F32)],
        scratch_shapes=[pltpu.VMEM((8,) + small.shape, F32), pltpu.SemaphoreType.DMA((n,)),
                        pltpu.SemaphoreType.DMA((n,)), pltpu.SemaphoreType.DMA((7,)), pltpu.SemaphoreType.DMA((7,))],
        input_output_aliases={i: i for i in range(2 * n)},
    )(*arrays, small)
    return outs[:n], outs[n:2 * n], outs[2 * n]


def _row_tile(rh):
    for cand in (512, 384, 352, 256, 128):
        if rh % cand == 0:
            return cand
    return rh


def _add_halves(name, g5, recv, place):
    _, _, rh, cols = g5.shape
    tr = _row_tile(rh)

    def body(place_ref, g_ref, r_ref, o_ref, own_ref):
        val = (g_ref[...] + r_ref[...]).astype(BF16)
        o_ref[...] = val

        @pl.when(pl.program_id(1) == place_ref[1])
        def _():
            own_ref[...] = val

    grid_spec = pltpu.PrefetchScalarGridSpec(
        num_scalar_prefetch=1, grid=(rh // tr, NCHIP),
        in_specs=[pl.BlockSpec((None, None, tr, cols), lambda i, j, pr: (j, pr[0], i, 0)),
                  pl.BlockSpec((None, tr, cols), lambda i, j, pr: (j, i, 0))],
        out_specs=[pl.BlockSpec((None, tr, cols), lambda i, j, pr: (j, i, 0)),
                   pl.BlockSpec((None, tr, cols), lambda i, j, pr: (pr[1], i, 0))])
    return pl.pallas_call(
        body, name=name, grid_spec=grid_spec, out_shape=[_sds(recv.shape, BF16)] * 2, compiler_params=_params(2),
    )(place, g5, recv)


def _sum_chips(name, landing, place, layer, full):
    _, rh, cols = landing.shape
    tr = _row_tile(rh)
    has_full = full is not None

    def body(*refs):
        r_ref, o_ref = refs[1], refs[-1]
        total = r_ref[0].astype(F32)
        for j in range(1, NCHIP):
            total = total + r_ref[j].astype(F32)
        o_ref[...] = total

    grid_spec = pltpu.PrefetchScalarGridSpec(
        num_scalar_prefetch=1, grid=(rh // tr,),
        in_specs=[pl.BlockSpec((NCHIP, tr, cols), lambda i, pr: (0, i, 0))] + ([_ANY] if has_full else []),
        out_specs=pl.BlockSpec((None, None, tr, cols), lambda i, pr: (layer, pr[0], i, 0)))
    return pl.pallas_call(
        body, name=name, grid_spec=grid_spec, out_shape=_sds((DEPTH, 2, rh, cols), F32),
        input_output_aliases={2: 0} if has_full else {}, compiler_params=_params(1),
    )(place, landing, *([full] if has_full else []))


class _Schedule:
    def __init__(self, slotted, cw4, place):
        self.place = place
        self._w = [None, list(slotted[1])]
        self._w[0], self.cw4 = _gather_first(slotted[0], cw4)
        self._g5, self._recv, self._parts, self._landing = {}, {}, {}, {}
        self.full = None
        self._hooks = {
            "mm_z0": lambda: _exchange_gather_ici(self._w[1], self._set_w1),
            "mm_up0": lambda: _exchange_gather_d2d(self._w[1], self._set_w1),
            "dg_down0": lambda: _exchange_halves(self._g5[1], self._recv[1], self._halves_done),
            "wg_up0": lambda: _exchange_chips(self._parts[1], self._landing[1], self._chips_done),
            "dg_z0": lambda: _exchange_share(self.full, 1, self._set_full),
        }

    def weights(self, layer):
        return dict(zip(_KEYS, self._w[layer]))

    def hook(self, name):
        make = self._hooks.get(name)
        return make() if make else None

    def _set_w1(self, arrays):
        self._w[1] = list(arrays)

    def _set_full(self, arrays):
        self.full = list(arrays)

    def grads_ready(self, layer, GG):
        g5 = [GG[k].reshape(NCHIP, 2, GG[k].shape[1] // 2, GG[k].shape[2]) for k in _KEYS]
        self._g5[layer] = g5
        self._recv[layer] = [lax.empty((NCHIP,) + g.shape[2:], F32) for g in g5]

    def _halves_done(self, arrays, layer=1):
        n = len(_KEYS)
        g5, recv = arrays[:n], arrays[n:]
        summed = [_add_halves("add_halves%d_%s" % (layer, k), g, r, self.place) for k, g, r in zip(_KEYS, g5, recv)]
        self._parts[layer] = [t[0] for t in summed]
        self._landing[layer] = [t[1] for t in summed]

    def _chips_done(self, arrays, layer=1):
        n = len(_KEYS)
        landing = arrays[n:]
        prev = self.full or [None] * n
        self.full = [_sum_chips("sum_chips%d_%s" % (layer, k), t, self.place, layer, f)
                     for k, t, f in zip(_KEYS, landing, prev)]

    def finish(self, small):
        g5, recv, small_red = _halves_and_small(self._g5[0], self._recv[0], small)
        self._halves_done(list(g5) + list(recv), layer=0)
        _exchange_call("exchange_chips0", _exchange_chips(self._parts[0], self._landing[0],
                                                          lambda arrays: self._chips_done(arrays, layer=0)))
        _exchange_call("share_halves0", _exchange_share(self.full, 0, self._set_full))
        return self.full, small_red


def _adamw(name, w, g, m, v):
    shape = w.shape
    cols = shape[-1]
    rows = 1
    for s in shape[:-1]:
        rows *= s
    tr = rows
    for cand in (256, 128, 64):
        if rows > cand and rows % cand == 0:
            tr = cand
            break
    c1 = 1.0 / (1.0 - B1 ** STEP)
    c2 = 1.0 / (1.0 - B2 ** STEP)

    def body(w_ref, g_ref, m_ref, v_ref, d_ref, nm_ref, nv_ref):
        gv = g_ref[...]
        nm = B1 * m_ref[...] + (1.0 - B1) * gv
        nv = B2 * v_ref[...] + (1.0 - B2) * (gv * gv)
        nm_ref[...] = nm
        nv_ref[...] = nv
        d_ref[...] = -LR * ((nm * c1) / (jnp.sqrt(nv * c2) + ADAM_EPS) + WD * w_ref[...])

    blk = pl.BlockSpec((tr, cols), lambda i: (i, 0))
    outs = pl.pallas_call(
        body, name=name, grid=(rows // tr,), in_specs=[blk] * 4, out_specs=[blk] * 3,
        out_shape=[_sds((rows, cols), F32)] * 3, compiler_params=_params(1),
    )(*(t.reshape(rows, cols) for t in (w, g, m, v)))
    return tuple(o.reshape(shape) for o in outs)


def _pack_small(small):
    rows = [jnp.sum(small["g_mix%d" % l], axis=0, keepdims=True) for l in range(DEPTH)]
    rows += [jnp.sum(small["pool_scale%d" % l], axis=0, keepdims=True) for l in range(DEPTH)]
    rows += [jnp.sum(small["g_ffn%d" % l], axis=0, keepdims=True) for l in range(DEPTH)]
    rows += [jnp.sum(small["g_ple%d" % l], axis=0, keepdims=True) for l in range(DEPTH)]
    rows += [jnp.sum(small["g_final"], axis=0, keepdims=True)]
    flat = [small["conv_b%d" % l].reshape(-1) for l in range(DEPTH)]
    flat += [small["conv_w%d" % l].reshape(-1) for l in range(DEPTH)]
    flat = jnp.concatenate(flat).reshape(-1, D)
    packed = jnp.concatenate(rows + [flat], axis=0)
    return jnp.pad(packed, ((0, SMALL_ROWS - packed.shape[0]), (0, 0)))


def _unpack_small(red):
    g_mix, pool_scale, g_ffn, g_ple = red[0:2], red[2:4], red[4:6], red[6:8]
    g_final = red[8]
    nb = DEPTH * UW // D
    conv_b = red[9:9 + nb].reshape(DEPTH, UW)
    conv_w = red[9 + nb:9 + 4 * nb].reshape(DEPTH, 3, UW)
    return g_mix, pool_scale, g_ffn, g_ple, g_final, conv_b, conv_w


def kernel(x, p, g_mix, w_in, w_ya, w_yb, pool_w, pool_scale, w_o, g_ffn, w_up, conv_w, conv_b, w_down, g_ple, w_ple, w_ple_gate, g_final, loss_target, m_g_mix, m_w_in, m_w_ya, m_w_yb, m_pool_w, m_pool_scale, m_w_o, m_g_ffn, m_w_up, m_conv_w, m_conv_b, m_w_down, m_g_ple, m_w_ple, m_w_ple_gate, m_g_final, v_g_mix, v_w_in, v_w_ya, v_w_yb, v_pool_w, v_pool_scale, v_w_o, v_g_ffn, v_w_up, v_conv_w, v_conv_b, v_w_down, v_g_ple, v_w_ple, v_w_ple_gate, v_g_final):
    me = 2 * lax.axis_index("x") + lax.axis_index("y")
    place = jnp.stack([lax.axis_index("c"), me]).astype(jnp.int32)

    def slot(shard):
        return lax.dynamic_update_index_in_dim(lax.empty((NCHIP,) + shard.shape, shard.dtype), shard, me, 0)

    packed = [
        w_in.astype(BF16), w_up.astype(BF16),
        jnp.concatenate([w_ya, w_ple, pool_w.reshape(DEPTH, 256, 256)], axis=1).astype(BF16),
        jnp.concatenate([w_yb, w_o, w_ple_gate], axis=1).astype(BF16),
        w_down.astype(BF16),
    ]
    slotted = [[slot(t[l]) for t in packed] for l in range(DEPTH)]
    ex = _Schedule(slotted, slot(conv_w.reshape(DEPTH * 3, UP_S)), place)
    cw_full = ex.cw4.reshape(NCHIP, DEPTH, 3, UP_S).transpose(1, 2, 0, 3).reshape(DEPTH, 3, UW)

    vecs = dict(g_mix=g_mix, pool_scale=pool_scale, g_ffn=g_ffn, g_ple=g_ple, g_final=g_final, conv_b=conv_b,
                conv_w=cw_full)
    sq8, grad_x, small = _local_step(x, p, loss_target, vecs, ex)
    loss = lax.psum(jnp.sum(sq8) * (0.5 / D), ("x", "y", "c"))

    full, small_red = ex.finish(_pack_small(small))
    r_in, r_up, r_sc, r_r3, r_dn = [f.reshape(DEPTH, -1, f.shape[-1]) for f in full]
    d_g_mix, d_pool_scale, d_g_ffn, d_g_ple, d_g_final, d_conv_b, d_conv_w_full = _unpack_small(small_red)
    d_conv_w = lax.dynamic_slice_in_dim(d_conv_w_full, me * UP_S, UP_S, axis=2)

    grads = dict(
        g_mix=d_g_mix, w_in=r_in, w_ya=r_sc[:, 0:512], w_yb=r_r3[:, 0:256],
        pool_w=r_sc[:, 768:1024].reshape(DEPTH, 4, 64, 256), pool_scale=d_pool_scale, w_o=r_r3[:, 256:512],
        g_ffn=d_g_ffn, w_up=r_up, conv_w=d_conv_w, conv_b=d_conv_b, w_down=r_dn, g_ple=d_g_ple,
        w_ple=r_sc[:, 512:768], w_ple_gate=r_r3[:, 512:768], g_final=d_g_final)
    weights = dict(g_mix=g_mix, w_in=w_in, w_ya=w_ya, w_yb=w_yb, pool_w=pool_w, pool_scale=pool_scale, w_o=w_o,
                   g_ffn=g_ffn, w_up=w_up, conv_w=conv_w, conv_b=conv_b, w_down=w_down, g_ple=g_ple, w_ple=w_ple,
                   w_ple_gate=w_ple_gate, g_final=g_final)
    m_in = dict(g_mix=m_g_mix, w_in=m_w_in, w_ya=m_w_ya, w_yb=m_w_yb, pool_w=m_pool_w, pool_scale=m_pool_scale,
                w_o=m_w_o, g_ffn=m_g_ffn, w_up=m_w_up, conv_w=m_conv_w, conv_b=m_conv_b, w_down=m_w_down,
                g_ple=m_g_ple, w_ple=m_w_ple, w_ple_gate=m_w_ple_gate, g_final=m_g_final)
    v_in = dict(g_mix=v_g_mix, w_in=v_w_in, w_ya=v_w_ya, w_yb=v_w_yb, pool_w=v_pool_w, pool_scale=v_pool_scale,
                w_o=v_w_o, g_ffn=v_g_ffn, w_up=v_w_up, conv_w=v_conv_w, conv_b=v_conv_b, w_down=v_w_down,
                g_ple=v_g_ple, w_ple=v_w_ple, w_ple_gate=v_w_ple_gate, g_final=v_g_final)
    names = ["g_mix", "w_in", "w_ya", "w_yb", "pool_w", "pool_scale", "w_o", "g_ffn", "w_up", "conv_w", "conv_b",
             "w_down", "g_ple", "w_ple", "w_ple_gate", "g_final"]
    deltas, new_m, new_v = [], [], []
    for nme in names:
        gr = grads[nme].reshape(weights[nme].shape)
        grads[nme] = gr
        dlt, nm, nv = _adamw("adamw_" + nme, weights[nme], gr, m_in[nme], v_in[nme])
        deltas.append(dlt)
        new_m.append(nm)
        new_v.append(nv)
    return (loss, grad_x, *[grads[nme] for nme in names], *deltas, *new_m, *new_v)
```

```python
import math

import jax
import jax.numpy as jnp
from jax import lax
from jax.experimental import pallas as pl
from jax.experimental.pallas import tpu as pltpu

F32 = jnp.float32
BF16 = jnp.bfloat16
_KEYS = ("in", "up", "sc", "r3", "dn")
MESH = pl.DeviceIdType.MESH

D = 1024
SEQ = 2048
DEPTH = 2
HEAD = 128
GROUP_W = 512
DILATIONS = (1, 4, 16)
ROPE_DIM = 32
ROPE_THETA = 500000.0
NEG_INF = -1e30
ZW = 7680
OFF_K, OFF_V, OFF_U, OFF_GA, OFF_GB = 1536, 3072, 4608, 5632, 6656
FF = 2816
UW = 2 * FF
PLE = 256
NCHIP = 4
IN_S, UP_S, DN_S = ZW // NCHIP, UW // NCHIP, FF // NCHIP
RMS_EPS = 1e-6
LR, B1, B2, ADAM_EPS, WD, STEP = 0.001, 0.9, 0.999, 1e-08, 0.01, 10
SMALL_ROWS = 56
VMEM_CAP = 48 * 1024 * 1024


def _params(n_grid, vmem=VMEM_CAP):
    return pltpu.CompilerParams(dimension_semantics=("arbitrary",) * n_grid, vmem_limit_bytes=vmem)


def _sigmoid(v):
    return 1.0 / (1.0 + jnp.exp(-v))


def _rows8(v):
    return jnp.sum(v.reshape(v.shape[0] // 8, 8, v.shape[1]), axis=0)


def _sds(shape, dtype):
    return jax.ShapeDtypeStruct(shape, dtype)


def _dot(av, bv, ta=False, tb=False):
    dims = (((0,) if ta else (1,), (1,) if tb else (0,)), ((), ()))
    return lax.dot_general(av.astype(BF16), bv.astype(BF16), dims, preferred_element_type=F32)


def _call(body, *, name, grid, in_specs, out_specs, out_shape, scratch_shapes=(), aliases=None, comm=None):
    params = _params(len(grid))
    aliases = dict(aliases or {})
    if comm is None:
        return pl.pallas_call(body, name=name, grid=grid, in_specs=list(in_specs), out_specs=out_specs,
                              out_shape=out_shape, scratch_shapes=list(scratch_shapes),
                              input_output_aliases=aliases, compiler_params=params)
    single = not isinstance(out_shape, (list, tuple))
    out_specs_l = [out_specs] if single else list(out_specs)
    out_shape_l = [out_shape] if single else list(out_shape)
    n_in, n_out, n_c = len(in_specs), len(out_shape_l), len(comm["arrays"])

    def hosted(*refs):
        core_in, core_out = refs[:n_in], refs[n_in + n_c:n_in + n_c + n_out]
        c_refs = refs[n_in + n_c + n_out:n_in + 2 * n_c + n_out]
        scratch, (send_sems, recv_sems) = refs[n_in + 2 * n_c + n_out:-2], refs[-2:]
        ids = [pl.program_id(i) for i in range(len(grid))]
        first, last = ids[0] == 0, ids[0] == grid[0] - 1
        for i in range(1, len(grid)):
            first = jnp.logical_and(first, ids[i] == 0)
            last = jnp.logical_and(last, ids[i] == grid[i] - 1)

        @pl.when(first)
        def _():
            comm["start"](c_refs, send_sems, recv_sems)

        body(*core_in, *core_out, *scratch)

        @pl.when(last)
        def _():
            comm["wait"](c_refs, send_sems, recv_sems)

    any_spec = pl.BlockSpec(memory_space=pl.ANY)
    for i in range(n_c):
        aliases[n_in + i] = n_out + i
    call = pl.pallas_call(
        hosted, name=name, grid=grid, in_specs=list(in_specs) + [any_spec] * n_c,
        out_specs=out_specs_l + [any_spec] * n_c,
        out_shape=out_shape_l + [_sds(t.shape, t.dtype) for t in comm["arrays"]],
        scratch_shapes=list(scratch_shapes) + [pltpu.SemaphoreType.DMA((comm["nsem"],))] * 2,
        input_output_aliases=aliases, compiler_params=params)

    def run(*args):
        outs = call(*args, *comm["arrays"])
        comm["done"](outs[n_out:])
        return outs[0] if single else outs[:n_out]

    return run


def _mm_call(name, a, b, *, grid, a_spec, b_spec, o_spec, out_shape, ta=False, tb=False, k_axis=None, nk=1,
             acc_shape=None, res=None, res_spec=None, buf=None, compute=None, comm=None):
    has_res, has_buf = res is not None, buf is not None
    in_place = nk > 1 and not has_res and out_shape.dtype == F32

    def body(*refs):
        a_ref, b_ref = refs[0], refs[1]
        pos = 2
        r_ref = None
        if has_res:
            r_ref = refs[pos]
            pos += 1
        if has_buf:
            pos += 1
        o_ref = refs[pos]
        if compute is None:
            av = a_ref[...]
            bv = b_ref[...]
            part = _dot(av.reshape(-1, av.shape[-1]), bv.reshape(-1, bv.shape[-1]), ta, tb)
        else:
            part = compute(a_ref, b_ref)

        def finish(val):
            if r_ref is not None:
                val = val + r_ref[...]
            o_ref[...] = val.reshape(o_ref.shape).astype(o_ref.dtype)

        if nk == 1:
            finish(part)
        elif in_place:
            @pl.when(pl.program_id(k_axis) == 0)
            def _():
                o_ref[...] = jnp.zeros(o_ref.shape, F32)

            o_ref[...] += part.reshape(o_ref.shape)
        else:
            acc_ref = refs[pos + 1]
            k = pl.program_id(k_axis)

            @pl.when(k == 0)
            def _():
                acc_ref[...] = jnp.zeros(acc_ref.shape, F32)

            acc_ref[...] += part

            @pl.when(k == nk - 1)
            def _():
                finish(acc_ref[...])

    ins, in_specs = [a, b], [a_spec, b_spec]
    if has_res:
        ins.append(res)
        in_specs.append(res_spec)
    aliases = {}
    if has_buf:
        aliases = {len(ins): 0}
        ins.append(buf)
        in_specs.append(pl.BlockSpec(memory_space=pl.ANY))
    scratch = [pltpu.VMEM(acc_shape, F32)] if nk > 1 and not in_place else []
    return _call(body, name=name, grid=grid, in_specs=in_specs, out_specs=o_spec, out_shape=out_shape,
                 scratch_shapes=scratch, aliases=aliases, comm=comm)(*ins)


def _rms_fwd(name, x, g, tr=512):
    T = x.shape[0]

    def body(x_ref, g_ref, h_ref):
        xv = x_ref[...]
        r = lax.rsqrt(jnp.mean(xv * xv, axis=-1, keepdims=True) + RMS_EPS)
        h_ref[...] = (xv * r * g_ref[...]).astype(BF16)

    return pl.pallas_call(
        body, name=name, grid=(T // tr,),
        in_specs=[pl.BlockSpec((tr, D), lambda i: (i, 0)), pl.BlockSpec((1, D), lambda i: (0, 0))],
        out_specs=pl.BlockSpec((tr, D), lambda i: (i, 0)), out_shape=_sds((T, D), BF16),
        compiler_params=_params(1),
    )(x, g)


def _rms_bwd(name, x, dh, g, dres, tr=512):
    T = x.shape[0]

    def body(x_ref, dh_ref, g_ref, dres_ref, dx_ref, dg_ref):
        xv = x_ref[...]
        r = lax.rsqrt(jnp.mean(xv * xv, axis=-1, keepdims=True) + RMS_EPS)
        xh = xv * r
        dhv = dh_ref[...]
        part = _rows8(dhv * xh)

        @pl.when(pl.program_id(0) == 0)
        def _():
            dg_ref[...] = part

        @pl.when(pl.program_id(0) > 0)
        def _():
            dg_ref[...] += part

        dxh = dhv * g_ref[...]
        dx_ref[...] = dres_ref[...] + r * (dxh - xh * jnp.mean(dxh * xh, axis=-1, keepdims=True))

    row = pl.BlockSpec((tr, D), lambda i: (i, 0))
    return pl.pallas_call(
        body, name=name, grid=(T // tr,),
        in_specs=[row, row, pl.BlockSpec((1, D), lambda i: (0, 0)), row],
        out_specs=[row, pl.BlockSpec((8, D), lambda i: (0, 0))],
        out_shape=[_sds((T, D), F32), _sds((8, D), F32)],
        compiler_params=_params(1),
    )(x, dh, g, dres)


def _final_loss(x, g, tgt, tr=512):
    T = x.shape[0]

    def body(x_ref, g_ref, t_ref, dx_ref, dg_ref, sq_ref):
        xv = x_ref[...]
        r = lax.rsqrt(jnp.mean(xv * xv, axis=-1, keepdims=True) + RMS_EPS)
        xh = xv * r
        gv = g_ref[...]
        e = xh * gv - t_ref[...]
        dy = e * (1.0 / D)
        pg = _rows8(dy * xh)
        ps = _rows8(e * e)

        @pl.when(pl.program_id(0) == 0)
        def _():
            dg_ref[...] = pg
            sq_ref[...] = ps

        @pl.when(pl.program_id(0) > 0)
        def _():
            dg_ref[...] += pg
            sq_ref[...] += ps

        dxh = dy * gv
        dx_ref[...] = r * (dxh - xh * jnp.mean(dxh * xh, axis=-1, keepdims=True))

    row = pl.BlockSpec((tr, D), lambda i: (i, 0))
    acc = pl.BlockSpec((8, D), lambda i: (0, 0))
    return pl.pallas_call(
        body, name="final_loss", grid=(T // tr,),
        in_specs=[row, pl.BlockSpec((1, D), lambda i: (0, 0)), row],
        out_specs=[row, acc, acc],
        out_shape=[_sds((T, D), F32), _sds((8, D), F32), _sds((8, D), F32)],
        compiler_params=_params(1),
    )(x, g, tgt)


def _ple_fwd(name, x, pe, pg, tr=512):
    T = x.shape[0]

    def body(x_ref, pe_ref, pg_ref, o_ref):
        o_ref[...] = x_ref[...] + pe_ref[...] * _sigmoid(pg_ref[...])

    row = pl.BlockSpec((tr, D), lambda i: (i, 0))
    return pl.pallas_call(
        body, name=name, grid=(T // tr,), in_specs=[row, row, row], out_specs=row,
        out_shape=_sds((T, D), F32), compiler_params=_params(1),
    )(x, pe, pg)


def _ple_bwd(name, dx, pe, pg, tr=512):
    T = dx.shape[0]

    def body(dx_ref, pe_ref, pg_ref, dpe_ref, dpg_ref):
        s = _sigmoid(pg_ref[...])
        dxv = dx_ref[...]
        dpe_ref[...] = (dxv * s).astype(BF16)
        dpg_ref[...] = (dxv * pe_ref[...] * s * (1.0 - s)).astype(BF16)

    row = pl.BlockSpec((tr, D), lambda i: (i, 0))
    return pl.pallas_call(
        body, name=name, grid=(T // tr,), in_specs=[row, row, row], out_specs=[row, row],
        out_shape=[_sds((T, D), BF16), _sds((T, D), BF16)], compiler_params=_params(1),
    )(dx, pe, pg)


def _gate_fwd(name, z, ya, yb, tr=512):
    T = z.shape[0]
    w = 512

    def body(ga_ref, gb_ref, ya_ref, yb_ref, o_ref):
        o_ref[...] = (_sigmoid(ga_ref[...]) * ya_ref[...] + _sigmoid(gb_ref[...]) * yb_ref[...]).astype(BF16)

    col = pl.BlockSpec((tr, w), lambda i, j: (i, j))
    return pl.pallas_call(
        body, name=name, grid=(T // tr, D // w),
        in_specs=[pl.BlockSpec((tr, w), lambda i, j: (i, OFF_GA // w + j)),
                  pl.BlockSpec((tr, w), lambda i, j: (i, OFF_GB // w + j)), col, col],
        out_specs=col, out_shape=_sds((T, D), BF16), compiler_params=_params(2),
    )(z, z, ya, yb)


def _gate_bwd(name, z, off, y, dm, dz, tr=512):
    T = z.shape[0]
    w = 512
    has_dz = dz is not None

    def body(*refs):
        g_ref, y_ref, dm_ref = refs[:3]
        dy_ref, dz_ref = refs[-2:]
        s = _sigmoid(g_ref[...])
        dmv = dm_ref[...]
        dy_ref[...] = (dmv * s).astype(BF16)
        dz_ref[...] = (dmv * y_ref[...] * s * (1.0 - s)).astype(BF16)

    col = pl.BlockSpec((tr, w), lambda i, j: (i, j))
    gcol = pl.BlockSpec((tr, w), lambda i, j: (i, off // w + j))
    ins, in_specs, aliases = [z, y, dm], [gcol, col, col], {}
    if has_dz:
        ins.append(dz)
        in_specs.append(pl.BlockSpec(memory_space=pl.ANY))
        aliases = {3: 1}
    return pl.pallas_call(
        body, name=name, grid=(T // tr, D // w), in_specs=in_specs, out_specs=[col, gcol],
        out_shape=[_sds((T, D), BF16), _sds((T, ZW), BF16)], input_output_aliases=aliases,
        compiler_params=_params(2),
    )(*ins)


def _shift_down(v, k, rows):
    return jnp.where(rows >= k, pltpu.roll(v, k, 0), 0.0)


def _shift_up(v, k, rows):
    n = v.shape[0]
    return jnp.where(rows < n - k, pltpu.roll(v, n - k, 0), 0.0)


def _pool_window(v, g, rows, shift):
    s2 = v + shift(v, 1, rows)
    s4 = s2 + shift(s2, 2, rows)
    s8 = s4 + shift(s4, 4, rows)
    s16 = s8 + shift(s8, 8, rows)
    return jnp.where(g == 0, s2, jnp.where(g == 1, s4, jnp.where(g == 2, s8, s16)))


def _pool_count(g, rows):
    wlen = jnp.left_shift(2, g).astype(F32)
    return jnp.minimum(rows.astype(F32) + 1.0, wlen)


def _pool_fwd(name, z3, g_sc, scale):
    Bn = z3.shape[0]
    gw = 256

    def body(u_ref, pw_ref, sc_ref, pooled_ref, ms_ref):
        g = pl.program_id(1)
        u = u_ref[...]
        rows = lax.broadcasted_iota(jnp.int32, u.shape, 0)
        pooled = (_pool_window(u, g, rows, _shift_down) / _pool_count(g, rows) - u).astype(BF16)
        pooled_ref[...] = pooled
        pw = pw_ref[...].reshape(gw, gw)
        mixed = jnp.dot(pooled, pw, preferred_element_type=F32)
        ms_ref[...] = (mixed * sc_ref[...]).astype(BF16)

    blk = pl.BlockSpec((None, SEQ, gw), lambda b, g: (b, 0, g))
    return pl.pallas_call(
        body, name=name, grid=(Bn, 4),
        in_specs=[pl.BlockSpec((None, SEQ, gw), lambda b, g: (b, 0, OFF_U // gw + g)),
                  pl.BlockSpec((NCHIP, 64, gw), lambda b, g: (0, 12 + g, 0)),
                  pl.BlockSpec((1, gw), lambda b, g: (0, g))],
        out_specs=[blk, blk],
        out_shape=[_sds((Bn, SEQ, D), BF16), _sds((Bn, SEQ, D), BF16)],
        compiler_params=_params(2),
    )(z3, g_sc, scale)


def _pool_bwd(name, dms3, pooled3, g_sc, scale, dz3, gg_sc):
    Bn = dms3.shape[0]
    gw = 256
    has_gg = gg_sc is not None

    def body(*refs):
        dms_ref, pooled_ref, pw_ref, sc_ref = refs[:4]
        dz_ref, dpw_ref, dsc_ref = refs[-3:]
        g, b = pl.program_id(0), pl.program_id(1)
        pooled = pooled_ref[...]
        pw = pw_ref[...].reshape(gw, gw)
        dms = dms_ref[...]
        mixed = jnp.dot(pooled, pw, preferred_element_type=F32)
        psc = _rows8(dms * mixed)
        dmixed = (dms * sc_ref[...]).astype(BF16)
        dpw = lax.dot_general(pooled, dmixed, (((0,), (0,)), ((), ())), preferred_element_type=F32)
        dpw = dpw.reshape(NCHIP, 64, gw)

        @pl.when(b == 0)
        def _():
            dsc_ref[...] = psc
            dpw_ref[...] = dpw

        @pl.when(b > 0)
        def _():
            dsc_ref[...] += psc
            dpw_ref[...] += dpw

        dpooled = lax.dot_general(dmixed, pw, (((1,), (1,)), ((), ())), preferred_element_type=F32)
        rows = lax.broadcasted_iota(jnp.int32, dpooled.shape, 0)
        dq = dpooled / _pool_count(g, rows)
        dz_ref[...] = (_pool_window(dq, g, rows, _shift_up) - dpooled).astype(BF16)

    ins = [dms3, pooled3, g_sc, scale, dz3]
    in_specs = [pl.BlockSpec((None, SEQ, gw), lambda g, b: (b, 0, g)),
                pl.BlockSpec((None, SEQ, gw), lambda g, b: (b, 0, g)),
                pl.BlockSpec((NCHIP, 64, gw), lambda g, b: (0, 12 + g, 0)),
                pl.BlockSpec((1, gw), lambda g, b: (0, g)),
                pl.BlockSpec(memory_space=pl.ANY)]
    aliases = {4: 0}
    if has_gg:
        ins.append(gg_sc)
        in_specs.append(pl.BlockSpec(memory_space=pl.ANY))
        aliases[5] = 1
    return pl.pallas_call(
        body, name=name, grid=(4, Bn), in_specs=in_specs,
        out_specs=[pl.BlockSpec((None, SEQ, gw), lambda g, b: (b, 0, OFF_U // gw + g)),
                   pl.BlockSpec((NCHIP, 64, gw), lambda g, b: (0, 12 + g, 0)),
                   pl.BlockSpec((8, gw), lambda g, b: (0, g))],
        out_shape=[_sds(dz3.shape, BF16), _sds((NCHIP, D, 256), F32), _sds((8, D), F32)],
        input_output_aliases=aliases, compiler_params=_params(2),
    )(*ins)


CT = 256
NCT = FF // CT


def _conv_pre(u, cw_ref, cb_ref, rows):
    return (cb_ref[...] + cw_ref[0:1, :] * _shift_down(u, 2, rows) + cw_ref[1:2, :] * _shift_down(u, 1, rows)
            + cw_ref[2:3, :] * u)


def _conv_fwd(name, u3, cw, cb, comm=None):
    Bn = u3.shape[0]

    def body(ug_ref, uv_ref, cwg_ref, cwv_ref, cbg_ref, cbv_ref, a_ref):
        ug, uv = ug_ref[...], uv_ref[...]
        rows = lax.broadcasted_iota(jnp.int32, ug.shape, 0)
        yg = _conv_pre(ug, cwg_ref, cbg_ref, rows)
        yv = _conv_pre(uv, cwv_ref, cbv_ref, rows)
        a_ref[...] = (yg * _sigmoid(yg) * yv).astype(BF16)

    def blk(off):
        return pl.BlockSpec((None, SEQ, CT), lambda b, c: (b, 0, off + c))

    return _call(
        body, name=name, grid=(Bn, NCT),
        in_specs=[blk(0), blk(NCT),
                  pl.BlockSpec((3, CT), lambda b, c: (0, c)), pl.BlockSpec((3, CT), lambda b, c: (0, NCT + c)),
                  pl.BlockSpec((1, CT), lambda b, c: (0, c)), pl.BlockSpec((1, CT), lambda b, c: (0, NCT + c))],
        out_specs=blk(0), out_shape=_sds((Bn, SEQ, FF), BF16), comm=comm,
    )(u3, u3, cw, cw, cb, cb)


def _conv_bwd(name, da3, u3, cw, cb, comm=None):
    Bn = u3.shape[0]
    last = NCT * Bn - 1

    def body(da_ref, ug_ref, uv_ref, cwg_ref, cwv_ref, cbg_ref, cbv_ref,
             du_ref, dcwg_ref, dcwv_ref, dcbg_ref, dcbv_ref, stage_g, stage_v, sems):
        c, b = pl.program_id(0), pl.program_id(1)
        step = c * Bn + b
        ug, uv, da = ug_ref[...], uv_ref[...], da_ref[...]
        rows = lax.broadcasted_iota(jnp.int32, ug.shape, 0)
        yg = _conv_pre(ug, cwg_ref, cbg_ref, rows)
        yv = _conv_pre(uv, cwv_ref, cbv_ref, rows)
        s = _sigmoid(yg)
        dyv = da * (yg * s)
        dyg = da * yv * (s * (1.0 + yg * (1.0 - s)))

        def writes(off_c, stage, sem):
            col = pl.multiple_of(off_c + c * CT, CT)
            return pltpu.make_async_copy(stage, du_ref.at[b, :, pl.ds(col, CT)], sem)

        @pl.when(step > 0)
        def _():
            writes(0, stage_g, sems.at[0]).wait()
            writes(FF, stage_v, sems.at[1]).wait()

        for dy, u, cw_ref, stage, dcw_ref, dcb_ref in ((dyg, ug, cwg_ref, stage_g, dcwg_ref, dcbg_ref),
                                                       (dyv, uv, cwv_ref, stage_v, dcwv_ref, dcbv_ref)):
            d1, d2 = _shift_up(dy, 1, rows), _shift_up(dy, 2, rows)
            stage[...] = (cw_ref[2:3, :] * dy + cw_ref[1:2, :] * d1 + cw_ref[0:1, :] * d2).astype(BF16)
            dcw = jnp.concatenate([jnp.sum(d2 * u, axis=0, keepdims=True), jnp.sum(d1 * u, axis=0, keepdims=True),
                                   jnp.sum(dy * u, axis=0, keepdims=True)], axis=0)
            dcb = jnp.sum(dy, axis=0, keepdims=True)

            @pl.when(b == 0)
            def _():
                dcw_ref[...] = dcw
                dcb_ref[...] = dcb

            @pl.when(b > 0)
            def _():
                dcw_ref[...] += dcw
                dcb_ref[...] += dcb

        writes(0, stage_g, sems.at[0]).start()
        writes(FF, stage_v, sems.at[1]).start()

        @pl.when(step == last)
        def _():
            writes(0, stage_g, sems.at[0]).wait()
            writes(FF, stage_v, sems.at[1]).wait()

    def blk(off):
        return pl.BlockSpec((None, SEQ, CT), lambda c, b: (b, 0, off + c))

    def vec(r, off):
        return pl.BlockSpec((r, CT), lambda c, b: (0, off + c))

    du3, dcwg, dcwv, dcbg, dcbv = _call(
        body, name=name, grid=(NCT, Bn),
        in_specs=[blk(0), blk(0), blk(NCT), vec(3, 0), vec(3, NCT), vec(1, 0), vec(1, NCT)],
        out_specs=[pl.BlockSpec(memory_space=pl.ANY), vec(3, 0), vec(3, 0), vec(1, 0), vec(1, 0)],
        out_shape=[_sds((Bn, SEQ, UW), BF16), _sds((3, FF), F32), _sds((3, FF), F32), _sds((1, FF), F32),
                   _sds((1, FF), F32)],
        scratch_shapes=[pltpu.VMEM((SEQ, CT), BF16)] * 2 + [pltpu.SemaphoreType.DMA((2,))], comm=comm,
    )(da3, u3, u3, cw, cw, cb, cb)
    return du3, jnp.concatenate([dcwg, dcwv], axis=1), jnp.concatenate([dcbg, dcbv], axis=1)


def _rope_tables():
    pos = jnp.arange(SEQ, dtype=F32)
    inv_freq = jnp.exp(jnp.arange(0, ROPE_DIM, 2, dtype=F32) * (-math.log(ROPE_THETA) / ROPE_DIM))
    ang = pos[:, None] * inv_freq[None, :]
    cos, sin = jnp.cos(ang), jnp.sin(ang)
    half = ROPE_DIM // 2
    zeros = jnp.zeros((SEQ, HEAD - ROPE_DIM), F32)
    zh = jnp.zeros((SEQ, half), F32)
    tab_c = jnp.concatenate([cos, cos, zeros + 1.0], axis=1)
    tab_a = jnp.concatenate([-sin, zh, zeros], axis=1)
    tab_b = jnp.concatenate([zh, sin, zeros], axis=1)
    return tab_c, tab_a, tab_b


def _rot(v, tc, ta, tb):
    half = ROPE_DIM // 2
    return v * tc + pltpu.roll(v, HEAD - half, 1) * ta + pltpu.roll(v, half, 1) * tb


def _rot_t(dv, tc, ta, tb):
    half = ROPE_DIM // 2
    return dv * tc + pltpu.roll(dv * ta, half, 1) + pltpu.roll(dv * tb, HEAD - half, 1)


def _band_masks():
    qi = lax.broadcasted_iota(jnp.int32, (HEAD, 2 * HEAD), 0)
    ki = lax.broadcasted_iota(jnp.int32, (HEAD, 2 * HEAD), 1)
    diff = HEAD + qi - ki
    both = (diff >= 0) & (diff <= HEAD)
    q1 = lax.broadcasted_iota(jnp.int32, (HEAD, HEAD), 0)
    k1 = lax.broadcasted_iota(jnp.int32, (HEAD, HEAD), 1)
    return q1 >= k1, both


_NT = (((1,), (1,)), ((), ()))
_TN = (((0,), (0,)), ((), ()))
_SCALE = HEAD ** -0.5


ATT_W = HEAD
ATT_HP = ATT_W // HEAD


def _res_rows(r, n, d, base=0):
    return pl.ds(base * d + r, n, stride=d) if d > 1 else pl.ds(base, n)


def _attn_load(q_ref, k_ref, v_ref, tc_ref, ta_ref, tb_ref, qs, ks, vs, d):
    L = SEQ // d
    for r in range(d):
        rows = _res_rows(r, L, d)
        tc, ta, tb = tc_ref[rows, :], ta_ref[rows, :], tb_ref[rows, :]
        dst = slice(r * L, (r + 1) * L)
        for hh in range(ATT_HP):
            sl = slice(hh * HEAD, (hh + 1) * HEAD)
            qs[dst, sl] = _rot(q_ref[rows, sl], tc, ta, tb).astype(BF16)
            ks[dst, sl] = _rot(k_ref[rows, sl], tc, ta, tb).astype(BF16)
            vs[dst, sl] = v_ref[rows, sl].astype(BF16)


def _attn_fwd(name, z3, tabs, g, d, comm=None):
    Bn = z3.shape[0]
    L = SEQ // d
    nb = L // HEAD
    W, nh = ATT_W, GROUP_W // ATT_W

    def body(q_ref, k_ref, v_ref, tc_ref, ta_ref, tb_ref, o_ref, l_ref, qs, ks, vs):
        m_first, m_both = _band_masks()
        _attn_load(q_ref, k_ref, v_ref, tc_ref, ta_ref, tb_ref, qs, ks, vs, d)
        for r in range(d):
            for hh in range(ATT_HP):
                sl = slice(hh * HEAD, (hh + 1) * HEAD)
                for n in range(nb):
                    rq = slice(r * L + n * HEAD, r * L + (n + 1) * HEAD)
                    rk = slice(r * L + max(n - 1, 0) * HEAD, r * L + (n + 1) * HEAD)
                    s = lax.dot_general(qs[rq, sl], ks[rk, sl], _NT, preferred_element_type=F32) * _SCALE
                    s = jnp.where(m_first if n == 0 else m_both, s, NEG_INF)
                    m = jnp.max(s, axis=-1, keepdims=True)
                    e = jnp.exp(s - m)
                    den = jnp.sum(e, axis=-1, keepdims=True)
                    p = (e * (1.0 / den)).astype(BF16)
                    rows = _res_rows(r, HEAD, d, n * HEAD)
                    o_ref[rows, sl] = jnp.dot(p, vs[rk, sl], preferred_element_type=F32)
                    l_ref[rows, sl] = jnp.broadcast_to(m + jnp.log(den), (HEAD, HEAD))

    def zcol(off):
        return pl.BlockSpec((None, SEQ, W), lambda b, h: (b, 0, (off + g * GROUP_W) // W + h))

    tab = pl.BlockSpec((SEQ, HEAD), lambda b, h: (0, 0))
    out = pl.BlockSpec((None, SEQ, W), lambda b, h: (b, 0, h))
    return _call(
        body, name=name, grid=(Bn, nh),
        in_specs=[zcol(0), zcol(OFF_K), zcol(OFF_V), tab, tab, tab],
        out_specs=[out, out],
        out_shape=[_sds((Bn, SEQ, GROUP_W), F32), _sds((Bn, SEQ, GROUP_W), F32)],
        scratch_shapes=[pltpu.VMEM((SEQ, W), BF16)] * 3, comm=comm,
    )(z3, z3, z3, *tabs)


def _attn_bwd(name, z3, tabs, g, d, do3, lse3, delta3, dz3, comm=None):
    Bn = z3.shape[0]
    L = SEQ // d
    nb = L // HEAD
    W, nh = ATT_W, GROUP_W // ATT_W

    def body(q_ref, k_ref, v_ref, tc_ref, ta_ref, tb_ref, do_ref, l_ref, dl_ref, dz_in, dz_ref,
             qs, ks, vs, dos, dqs, dks, dvs, nat, oq, ok, ov, sems):
        b, h = pl.program_id(0), pl.program_id(1)
        m_first, m_both = _band_masks()
        _attn_load(q_ref, k_ref, v_ref, tc_ref, ta_ref, tb_ref, qs, ks, vs, d)
        for r in range(d):
            dos[r * L:(r + 1) * L, :] = do_ref[_res_rows(r, L, d), :].astype(BF16)
        dks[...] = jnp.zeros_like(dks)
        dvs[...] = jnp.zeros_like(dvs)
        for r in range(d):
            for hh in range(ATT_HP):
                sl = slice(hh * HEAD, (hh + 1) * HEAD)
                for n in range(nb):
                    rq = slice(r * L + n * HEAD, r * L + (n + 1) * HEAD)
                    rk = slice(r * L + max(n - 1, 0) * HEAD, r * L + (n + 1) * HEAD)
                    rows = _res_rows(r, HEAD, d, n * HEAD)
                    qb, kk, vv, dob = qs[rq, sl], ks[rk, sl], vs[rk, sl], dos[rq, sl]
                    s = lax.dot_general(qb, kk, _NT, preferred_element_type=F32) * _SCALE
                    s = jnp.where(m_first if n == 0 else m_both, s, NEG_INF)
                    p = jnp.exp(s - l_ref[rows, sl][:, 0:1])
                    dp = lax.dot_general(dob, vv, _NT, preferred_element_type=F32)
                    ds = (p * (dp - dl_ref[rows, sl][:, 0:1]) * _SCALE).astype(BF16)
                    dqs[rq, sl] = jnp.dot(ds, kk, preferred_element_type=F32)
                    dks[rk, sl] += lax.dot_general(ds, qb, _TN, preferred_element_type=F32)
                    dvs[rk, sl] += lax.dot_general(p.astype(BF16), dob, _TN, preferred_element_type=F32)
        tc, ta, tb = tc_ref[...], ta_ref[...], tb_ref[...]
        step = b * nh + h

        def writes():
            base = g * GROUP_W + h * W
            return [pltpu.make_async_copy(src, dz_ref.at[b, :, pl.ds(pl.multiple_of(base + off, HEAD), W)],
                                          sems.at[i])
                    for i, (src, off) in enumerate(((oq, 0), (ok, OFF_K), (ov, OFF_V)))]

        @pl.when(step > 0)
        def _():
            for cp in writes():
                cp.wait()

        for src, dst, rotate in ((dqs, oq, True), (dks, ok, True), (dvs, ov, False)):
            for r in range(d):
                nat[_res_rows(r, L, d), :] = src[r * L:(r + 1) * L, :]
            for hh in range(ATT_HP):
                sl = slice(hh * HEAD, (hh + 1) * HEAD)
                val = nat[:, sl]
                dst[:, sl] = (_rot_t(val, tc, ta, tb) if rotate else val).astype(BF16)
        for cp in writes():
            cp.start()

        @pl.when(step == Bn * nh - 1)
        def _():
            for cp in writes():
                cp.wait()

    def zcol(off):
        return pl.BlockSpec((None, SEQ, W), lambda b, h: (b, 0, (off + g * GROUP_W) // W + h))

    tab = pl.BlockSpec((SEQ, HEAD), lambda b, h: (0, 0))
    gcol = pl.BlockSpec((None, SEQ, W), lambda b, h: (b, 0, h))
    any_spec = pl.BlockSpec(memory_space=pl.ANY)
    return _call(
        body, name=name, grid=(Bn, nh),
        in_specs=[zcol(0), zcol(OFF_K), zcol(OFF_V), tab, tab, tab, gcol, gcol, gcol, any_spec],
        out_specs=any_spec,
        out_shape=_sds((Bn, SEQ, ZW), BF16),
        scratch_shapes=[pltpu.VMEM((SEQ, W), BF16)] * 4 + [pltpu.VMEM((SEQ, W), F32)] * 4
        + [pltpu.VMEM((SEQ, W), BF16)] * 3 + [pltpu.SemaphoreType.DMA((3,))],
        aliases={9: 0}, comm=comm,
    )(z3, z3, z3, *tabs, do3, lse3, delta3, dz3)


def _merge_weights(l0, l1, l2):
    m = jnp.maximum(jnp.maximum(l0, l1), l2)
    e0, e1, e2 = jnp.exp(l0 - m), jnp.exp(l1 - m), jnp.exp(l2 - m)
    inv = 1.0 / (e0 + e1 + e2)
    return e0 * inv, e1 * inv, e2 * inv


def _merge_fwd(name, outs, lses, tr=512):
    T = outs[0].shape[0]

    def body(o0, o1, o2, l0, l1, l2, a_ref):
        w0, w1, w2 = _merge_weights(l0[...], l1[...], l2[...])
        a_ref[...] = (w0 * o0[...] + w1 * o1[...] + w2 * o2[...]).astype(BF16)

    row = pl.BlockSpec((tr, GROUP_W), lambda i: (i, 0))
    return pl.pallas_call(
        body, name=name, grid=(T // tr,), in_specs=[row] * 6, out_specs=row,
        out_shape=_sds((T, GROUP_W), BF16), compiler_params=_params(1),
    )(*outs, *lses)


def _merge_bwd(name, outs, lses, dattn, tr=512):
    T = outs[0].shape[0]

    def body(o0, o1, o2, l0, l1, l2, da_ref, d0, d1, d2, e0, e1, e2):
        w = _merge_weights(l0[...], l1[...], l2[...])
        da = da_ref[...]
        attn = w[0] * o0[...] + w[1] * o1[...] + w[2] * o2[...]
        prod = da * attn
        csum = jnp.concatenate(
            [jnp.broadcast_to(jnp.sum(prod[:, hh * HEAD:(hh + 1) * HEAD], axis=-1, keepdims=True), (tr, HEAD))
             for hh in range(GROUP_W // HEAD)], axis=1)
        for wg, d_ref, e_ref in zip(w, (d0, d1, d2), (e0, e1, e2)):
            d_ref[...] = wg * da
            e_ref[...] = wg * csum

    row = pl.BlockSpec((tr, GROUP_W), lambda i: (i, 0))
    res = pl.pallas_call(
        body, name=name, grid=(T // tr,), in_specs=[row] * 7, out_specs=[row] * 6,
        out_shape=[_sds((T, GROUP_W), F32)] * 6,
        compiler_params=_params(1),
    )(*outs, *lses, dattn)
    return res[:3], res[3:]


def _local_step(x3, p4, tgt3, vecs, ex):
    Bn = x3.shape[0]
    T = Bn * SEQ
    x = x3.reshape(T, D)
    tgt = tgt3.reshape(T, D)
    pb = p4.astype(BF16).reshape(DEPTH, T, PLE)
    tabs = _rope_tables()
    tm = 1024 if T % 1024 == 0 else 512
    nt = T // tm
    tk = 1024 if T % 1024 == 0 else 512
    ntk = T // tk
    tm5 = 512
    f32o = lambda n: _sds((T, n), F32)

    def spec(shape, fn):
        return pl.BlockSpec(shape, fn)

    def _mm(name, *args, **kwargs):
        return _mm_call(name, *args, comm=ex.hook(name), **kwargs)

    def cols4(a_ref, b_ref):
        av = a_ref[...]
        return jnp.concatenate([_dot(av, b_ref[j]) for j in range(NCHIP)], axis=1)

    def rows4(a_ref, b_ref):
        av = a_ref[...]
        return jnp.concatenate([_dot(av, b_ref[:, j * 256:(j + 1) * 256], ta=True) for j in range(NCHIP)], axis=0)

    def kchunks4(a_ref, b_ref):
        total = _dot(a_ref[:, 0:256], b_ref[0], tb=True)
        for j in range(1, NCHIP):
            total = total + _dot(a_ref[:, j * 256:(j + 1) * 256], b_ref[j], tb=True)
        return total

    saved = []
    for l in range(DEPTH):
        L = str(l)
        G = ex.weights(l)
        g_mix, g_ffn, g_ple = (vecs[k][l:l + 1] for k in ("g_mix", "g_ffn", "g_ple"))
        pscale, cb, cw = vecs["pool_scale"][l:l + 1], vecs["conv_b"][l:l + 1], vecs["conv_w"][l]
        h = _rms_fwd("rms_mix" + L, x, g_mix)
        z = _mm("mm_z" + L, h, G["in"], grid=(T // tm5, NCHIP),
                a_spec=spec((tm5, D), lambda i, n: (i, 0)),
                b_spec=spec((None, D, IN_S), lambda i, n: (n, 0, 0)),
                o_spec=spec((tm5, IN_S), lambda i, n: (i, n)), out_shape=f32o(ZW))
        z3 = z.reshape(Bn, SEQ, ZW)
        outs, lses = [], []
        for g, d in enumerate(DILATIONS):
            o_g, l_g = _attn_fwd("attn_fwd%d_%d" % (g, l), z3, tabs, g, d, comm=ex.hook("attn_fwd%d_%d" % (g, l)))
            outs.append(o_g.reshape(T, GROUP_W))
            lses.append(l_g.reshape(T, GROUP_W))
        attn = _merge_fwd("merge_fwd" + L, outs, lses)
        ya = _mm("mm_ya" + L, attn, G["sc"], grid=(nt,), compute=cols4,
                 a_spec=spec((tm, GROUP_W), lambda i: (i, 0)),
                 b_spec=spec((NCHIP, GROUP_W, 256), lambda i: (0, 0, 0)),
                 o_spec=spec((tm, D), lambda i: (i, 0)), out_shape=f32o(D))
        pooled3, ms3 = _pool_fwd("pool_fwd" + L, z3, G["sc"], pscale)
        ms = ms3.reshape(T, D)

        def row_sharded(name, a, rb, res=None, kdim=D):
            if rb is None:
                b_arr, b_spec = G["dn"], spec((NCHIP, DN_S, D), lambda i: (0, 0, 0))
            else:
                b_arr, b_spec = G["r3"], spec((NCHIP, 256, D), lambda i: (0, rb, 0))
            return _mm(name, a, b_arr, grid=(T // tm5,),
                       a_spec=spec((tm5, kdim), lambda i: (i, 0)), b_spec=b_spec,
                       o_spec=spec((tm5, D), lambda i: (i, 0)), out_shape=f32o(D),
                       res=res, res_spec=None if res is None else spec((tm5, D), lambda i: (i, 0)))

        yb = row_sharded("mm_yb" + L, ms, 0)
        merged = _gate_fwd("gate_fwd" + L, z, ya, yb)
        x1 = row_sharded("mm_o" + L, merged, 1, res=x)
        h2 = _rms_fwd("rms_ffn" + L, x1, g_ffn)
        u = _mm("mm_up" + L, h2, G["up"], grid=(T // tm5, NCHIP),
                a_spec=spec((tm5, D), lambda i, n: (i, 0)),
                b_spec=spec((None, D, UP_S), lambda i, n: (n, 0, 0)),
                o_spec=spec((tm5, UP_S), lambda i, n: (i, n)), out_shape=f32o(UW))
        u3 = u.reshape(Bn, SEQ, UW)
        act = _conv_fwd("conv_fwd" + L, u3, cw, cb, comm=ex.hook("conv_fwd" + L)).reshape(T, FF)
        x2 = row_sharded("mm_down" + L, act, None, res=x1, kdim=FF)
        h3 = _rms_fwd("rms_ple" + L, x2, g_ple)
        pg = row_sharded("mm_pg" + L, h3, 2)
        pe = _mm("mm_pe" + L, pb[l], G["sc"], grid=(nt,), compute=cols4,
                 a_spec=spec((tm, PLE), lambda i: (i, 0)),
                 b_spec=spec((NCHIP, 256, 256), lambda i: (0, 2, 0)),
                 o_spec=spec((tm, D), lambda i: (i, 0)), out_shape=f32o(D))
        x3n = _ple_fwd("ple_fwd" + L, x2, pe, pg)
        saved.append(dict(x=x, h=h, z=z, outs=outs, lses=lses, attn=attn, ya=ya, yb=yb, pooled3=pooled3, ms=ms,
                          merged=merged, x1=x1, h2=h2, u3=u3, act=act, x2=x2, h3=h3, pg=pg, pe=pe))
        x = x3n

    dx, dg_final8, sq8 = _final_loss(x, vecs["g_final"].reshape(1, D), tgt)

    gg_shape = {k: _sds(G[k].shape, F32) for k in G}
    small = {"g_final": dg_final8}

    for l in reversed(range(DEPTH)):
        L = str(l)
        sv = saved[l]
        G = ex.weights(l)
        GG = dict.fromkeys(_KEYS)
        g_mix, g_ffn, g_ple = (vecs[k][l:l + 1] for k in ("g_mix", "g_ffn", "g_ple"))
        pscale, cb, cw = vecs["pool_scale"][l:l + 1], vecs["conv_b"][l:l + 1], vecs["conv_w"][l]

        def wgrad_rows(name, a, b_arr, key, rb):
            GG[key] = _mm(name, a, b_arr, grid=(2, ntk), ta=True, k_axis=1, nk=ntk, acc_shape=(D, 512),
                          a_spec=spec((tk, D), lambda n, k: (k, 0)),
                          b_spec=spec((tk, 512), lambda n, k: (k, n)),
                          o_spec=spec((NCHIP, 256, 512), lambda n, k: (0, rb, n)),
                          out_shape=gg_shape[key], buf=GG[key])

        def dgrad_rows(name, dy, rb):
            return _mm(name, dy, G["r3"], grid=(T // tm5,), tb=True,
                       a_spec=spec((tm5, D), lambda i: (i, 0)),
                       b_spec=spec((NCHIP, 256, D), lambda i: (0, rb, 0)),
                       o_spec=spec((tm5, D), lambda i: (i, 0)), out_shape=f32o(D))

        dpe, dpg = _ple_bwd("ple_bwd" + L, dx, sv["pe"], sv["pg"])
        GG["sc"] = _mm("wg_ple" + L, pb[l], dpe, grid=(ntk,), compute=rows4, k_axis=0, nk=ntk,
                       acc_shape=(NCHIP * PLE, 256),
                       a_spec=spec((tk, PLE), lambda k: (k, 0)), b_spec=spec((tk, D), lambda k: (k, 0)),
                       o_spec=spec((NCHIP, 256, 256), lambda k: (0, 2, 0)),
                       out_shape=gg_shape["sc"], buf=GG["sc"])
        wgrad_rows("wg_pg" + L, sv["h3"], dpg, "r3", 2)
        dh3 = dgrad_rows("dg_pg" + L, dpg, 2)
        dx, small["g_ple" + L] = _rms_bwd("rms_ple_bwd" + L, sv["x2"], dh3, g_ple, dx)

        da = _mm("dg_down" + L, dx, G["dn"], grid=(T // 256,), tb=True,
                 a_spec=spec((256, D), lambda i: (i, 0)),
                 b_spec=spec((NCHIP, DN_S, D), lambda i: (0, 0, 0)),
                 o_spec=spec((256, FF), lambda i: (i, 0)), out_shape=f32o(FF))
        GG["dn"] = _mm("wg_down" + L, sv["act"], dx, grid=(2, ntk), ta=True, k_axis=1, nk=ntk,
                       acc_shape=(FF, 512),
                       a_spec=spec((tk, FF), lambda n, k: (k, 0)), b_spec=spec((tk, 512), lambda n, k: (k, n)),
                       o_spec=spec((NCHIP, DN_S, 512), lambda n, k: (0, 0, n)),
                       out_shape=gg_shape["dn"], buf=GG["dn"])
        du3, dcw, dcb = _conv_bwd("conv_bwd" + L, da.reshape(Bn, SEQ, FF), sv["u3"], cw, cb,
                                  comm=ex.hook("conv_bwd" + L))
        small["conv_w" + L], small["conv_b" + L] = dcw, dcb
        du = du3.reshape(T, UW)
        dh2 = _mm("dg_up" + L, du, G["up"], grid=(T // tm5, NCHIP), tb=True, k_axis=1, nk=NCHIP, acc_shape=(tm5, D),
                  a_spec=spec((tm5, UP_S), lambda i, k: (i, k)),
                  b_spec=spec((None, D, UP_S), lambda i, k: (k, 0, 0)),
                  o_spec=spec((tm5, D), lambda i, k: (i, 0)), out_shape=f32o(D))
        GG["up"] = _mm("wg_up" + L, sv["h2"], du, grid=(NCHIP, ntk), ta=True, k_axis=1, nk=ntk,
                       acc_shape=(D, UP_S),
                       a_spec=spec((tk, D), lambda j, k: (k, 0)),
                       b_spec=spec((tk, UP_S), lambda j, k: (k, j)),
                       o_spec=spec((None, D, UP_S), lambda j, k: (j, 0, 0)),
                       out_shape=gg_shape["up"], buf=GG["up"])
        dx, small["g_ffn" + L] = _rms_bwd("rms_ffn_bwd" + L, sv["x1"], dh2, g_ffn, dx)

        dmerged = dgrad_rows("dg_o" + L, dx, 1)
        wgrad_rows("wg_o" + L, sv["merged"], dx, "r3", 1)
        dya, dz = _gate_bwd("gate_bwd_a" + L, sv["z"], OFF_GA, sv["ya"], dmerged, None)
        dyb, dz = _gate_bwd("gate_bwd_b" + L, sv["z"], OFF_GB, sv["yb"], dmerged, dz)
        dms = dgrad_rows("dg_yb" + L, dyb, 0)
        wgrad_rows("wg_yb" + L, sv["ms"], dyb, "r3", 0)
        dz3, GG["sc"], small["pool_scale" + L] = _pool_bwd(
            "pool_bwd" + L, dms.reshape(Bn, SEQ, D), sv["pooled3"], G["sc"], pscale,
            dz.reshape(Bn, SEQ, ZW), GG["sc"])
        dattn = _mm("dg_ya" + L, dya, G["sc"], grid=(nt,), compute=kchunks4,
                    a_spec=spec((tm, D), lambda i: (i, 0)),
                    b_spec=spec((NCHIP, GROUP_W, 256), lambda i: (0, 0, 0)),
                    o_spec=spec((tm, GROUP_W), lambda i: (i, 0)), out_shape=f32o(GROUP_W))
        GG["sc"] = _mm("wg_ya" + L, sv["attn"], dya, grid=(ntk,), compute=rows4, k_axis=0, nk=ntk,
                       acc_shape=(NCHIP * GROUP_W, 256),
                       a_spec=spec((tk, GROUP_W), lambda k: (k, 0)), b_spec=spec((tk, D), lambda k: (k, 0)),
                       o_spec=spec((NCHIP, GROUP_W, 256), lambda k: (0, 0, 0)),
                       out_shape=gg_shape["sc"], buf=GG["sc"])
        ex.grads_ready(l, {k: GG[k] for k in _KEYS[1:]})
        dos, deltas = _merge_bwd("merge_bwd" + L, sv["outs"], sv["lses"], dattn)
        view3 = lambda t: t.reshape(Bn, SEQ, GROUP_W)
        sz3 = sv["z"].reshape(Bn, SEQ, ZW)
        for g, d in enumerate(DILATIONS):
            dz3 = _attn_bwd("attn_bwd%d_%d" % (g, l), sz3, tabs, g, d, view3(dos[g]), view3(sv["lses"][g]),
                            view3(deltas[g]), dz3, comm=ex.hook("attn_bwd%d_%d" % (g, l)))
        dz = dz3.reshape(T, ZW)
        dh = _mm("dg_z" + L, dz, G["in"], grid=(T // tm5, NCHIP), tb=True, k_axis=1, nk=NCHIP, acc_shape=(tm5, D),
                 a_spec=spec((tm5, IN_S), lambda i, k: (i, k)),
                 b_spec=spec((None, D, IN_S), lambda i, k: (k, 0, 0)),
                 o_spec=spec((tm5, D), lambda i, k: (i, 0)), out_shape=f32o(D))
        GG["in"] = _mm("wg_z" + L, sv["h"], dz, grid=(NCHIP, ntk), ta=True, k_axis=1, nk=ntk,
                       acc_shape=(D, IN_S),
                       a_spec=spec((tk, D), lambda n, k: (k, 0)), b_spec=spec((tk, IN_S), lambda n, k: (k, n)),
                       o_spec=spec((None, D, IN_S), lambda n, k: (n, 0, 0)),
                       out_shape=gg_shape["in"], buf=GG["in"])
        dx, small["g_mix" + L] = _rms_bwd("rms_mix_bwd" + L, sv["x"], dh, g_mix, dx)
        ex.grads_ready(l, {"in": GG["in"]})

    return sq8, dx.reshape(Bn, SEQ, D), small


_ANY = pl.BlockSpec(memory_space=pl.ANY)


def _place():
    x, y, c = lax.axis_index("x"), lax.axis_index("y"), lax.axis_index("c")
    chips = [(1 - x, y), (x, 1 - y), (1 - x, 1 - y)]
    return x, y, c, 2 * x + y, chips


def _half(rows, cc):
    return pl.ds(cc * (rows // 2), rows // 2)


def _remote(src, dst, send_sems, recv_sems, i, to):
    return pltpu.make_async_remote_copy(src_ref=src, dst_ref=dst, send_sem=send_sems.at[i], recv_sem=recv_sems.at[i],
                                        device_id=to, device_id_type=MESH)


def _exchange_gather_ici(stacks, done):
    n = len(stacks)
    rows = [t.shape[1] for t in stacks]

    def start(refs, send_sems, recv_sems):
        x, y, c, me, chips = _place()
        for k in range(n):
            part = refs[k].at[me, _half(rows[k], c)]
            for j, chip in enumerate(chips):
                _remote(part, part, send_sems, recv_sems, 3 * k + j, (*chip, c)).start()

    def wait(refs, send_sems, recv_sems):
        x, y, c, me, chips = _place()
        for k in range(n):
            for j, chip in enumerate(chips):
                part = refs[k].at[2 * chip[0] + chip[1], _half(rows[k], c)]
                _remote(part, part, send_sems, recv_sems, 3 * k + j, (*chip, c)).wait()

    return dict(arrays=list(stacks), nsem=3 * n, start=start, wait=wait, done=done)


def _exchange_gather_d2d(stacks, done):
    n = len(stacks)
    rows = [t.shape[1] for t in stacks]

    def copies(refs, send_sems, recv_sems, mine):
        x, y, c, me, chips = _place()
        cc = c if mine else 1 - c
        return [_remote(part, part, send_sems, recv_sems, 3 * k + j, (x, y, 1 - c))
                for k in range(n) for j, chip in enumerate(chips)
                for part in [refs[k].at[2 * chip[0] + chip[1], _half(rows[k], cc)]]]

    def start(refs, send_sems, recv_sems):
        for cp in copies(refs, send_sems, recv_sems, True):
            cp.start()

    def wait(refs, send_sems, recv_sems):
        for cp in copies(refs, send_sems, recv_sems, False):
            cp.wait()

    return dict(arrays=list(stacks), nsem=3 * n, start=start, wait=wait, done=done)


def _exchange_halves(g5, recv, done):
    n = len(g5)

    def copies(refs, send_sems, recv_sems):
        x, y, c, me, chips = _place()
        return [_remote(refs[k].at[:, 1 - c], refs[n + k], send_sems, recv_sems, k, (x, y, 1 - c)) for k in range(n)]

    def start(refs, send_sems, recv_sems):
        for cp in copies(refs, send_sems, recv_sems):
            cp.start()

    def wait(refs, send_sems, recv_sems):
        for cp in copies(refs, send_sems, recv_sems):
            cp.wait()

    return dict(arrays=list(g5) + list(recv), nsem=n, start=start, wait=wait, done=done)


def _exchange_chips(parts, landing, done):
    n = len(parts)

    def start(refs, send_sems, recv_sems):
        x, y, c, me, chips = _place()
        for k in range(n):
            for j, chip in enumerate(chips):
                _remote(refs[k].at[2 * chip[0] + chip[1]], refs[n + k].at[me], send_sems, recv_sems, 3 * k + j,
                        (*chip, c)).start()

    def wait(refs, send_sems, recv_sems):
        x, y, c, me, chips = _place()
        for k in range(n):
            for j, chip in enumerate(chips):
                slot = refs[n + k].at[2 * chip[0] + chip[1]]
                _remote(slot, slot, send_sems, recv_sems, 3 * k + j, (*chip, c)).wait()

    return dict(arrays=list(parts) + list(landing), nsem=3 * n, start=start, wait=wait, done=done)


def _exchange_share(full, layer, done):
    n = len(full)

    def copies(refs, send_sems, recv_sems, mine):
        x, y, c, me, chips = _place()
        cc = c if mine else 1 - c
        return [_remote(part, part, send_sems, recv_sems, k, (x, y, 1 - c))
                for k in range(n) for part in [refs[k].at[layer, cc]]]

    def start(refs, send_sems, recv_sems):
        for cp in copies(refs, send_sems, recv_sems, True):
            cp.start()

    def wait(refs, send_sems, recv_sems):
        for cp in copies(refs, send_sems, recv_sems, False):
            cp.wait()

    return dict(arrays=list(full), nsem=n, start=start, wait=wait, done=done)


def _exchange_call(name, comm):
    arrays = comm["arrays"]
    n = len(arrays)

    def body(*refs):
        outs, send_sems, recv_sems = refs[n:2 * n], refs[2 * n], refs[2 * n + 1]
        comm["start"](outs, send_sems, recv_sems)
        comm["wait"](outs, send_sems, recv_sems)

    outs = pl.pallas_call(
        body, name=name, in_specs=[_ANY] * n, out_specs=[_ANY] * n,
        out_shape=[_sds(t.shape, t.dtype) for t in arrays],
        scratch_shapes=[pltpu.SemaphoreType.DMA((comm["nsem"],))] * 2,
        input_output_aliases={i: i for i in range(n)},
    )(*arrays)
    comm["done"](outs)


def _gather_first(stacks, cw4):
    n = len(stacks)
    ici = _exchange_gather_ici(stacks, None)
    d2d = _exchange_gather_d2d(stacks, None)
    rows = [t.shape[1] for t in stacks]

    def body(*refs):
        g_refs, cwg_ref = refs[n + 1:2 * n + 1], refs[2 * n + 1]
        s_ici, r_ici, s_d2d, r_d2d, s_cw, r_cw = refs[2 * n + 2:]
        x, y, c, me, chips = _place()

        def cw_copy(j, slot, chip):
            part = cwg_ref.at[slot]
            return _remote(part, part, s_cw, r_cw, j, (*chip, c))

        ici["start"](g_refs, s_ici, r_ici)
        for j, chip in enumerate(chips):
            cw_copy(j, me, chip).start()
        for k in range(n):
            for j, chip in enumerate(chips):
                part = g_refs[k].at[2 * chip[0] + chip[1], _half(rows[k], c)]
                _remote(part, part, s_ici, r_ici, 3 * k + j, (*chip, c)).wait()
                _remote(part, part, s_d2d, r_d2d, 3 * k + j, (x, y, 1 - c)).start()
        d2d["wait"](g_refs, s_d2d, r_d2d)
        for j, chip in enumerate(chips):
            cw_copy(j, 2 * chip[0] + chip[1], chip).wait()

    outs = pl.pallas_call(
        body, name="gather_first", in_specs=[_ANY] * (n + 1), out_specs=[_ANY] * (n + 1),
        out_shape=[_sds(t.shape, t.dtype) for t in stacks] + [_sds(cw4.shape, cw4.dtype)],
        scratch_shapes=[pltpu.SemaphoreType.DMA((3 * n,))] * 4 + [pltpu.SemaphoreType.DMA((3,))] * 2,
        input_output_aliases={i: i for i in range(n + 1)},
    )(*stacks, cw4)
    return outs[:n], outs[n]


def _halves_and_small(g5, recv, small):
    n = len(g5)
    halves = _exchange_halves(g5, recv, None)

    def body(*refs):
        small_ref = refs[2 * n]
        c_refs, red_ref = refs[2 * n + 1:4 * n + 1], refs[4 * n + 1]
        gath, send_sems, recv_sems, s_send, s_recv = refs[4 * n + 2:]
        x, y, c, me, chips = _place()
        dev = 4 * x + 2 * y + c
        gath[dev] = small_ref[...]
        halves["start"](c_refs, send_sems, recv_sems)
        for r in range(1, 8):
            peer = (x ^ (r >> 2), y ^ ((r >> 1) & 1), c ^ (r & 1))
            _remote(small_ref, gath.at[dev], s_send, s_recv, r - 1, peer).start()
        for r in range(1, 8):
            peer = (x ^ (r >> 2), y ^ ((r >> 1) & 1), c ^ (r & 1))
            src = 4 * peer[0] + 2 * peer[1] + peer[2]
            _remote(small_ref, gath.at[src], s_send, s_recv, r - 1, peer).wait()
        total = gath[0]
        for i in range(1, 8):
            total = total + gath[i]
        red_ref[...] = total
        halves["wait"](c_refs, send_sems, recv_sems)

    vm = pl.BlockSpec(memory_space=pltpu.VMEM)
    arrays = list(g5) + list(recv)
    outs = pl.pallas_call(
        body, name="halves_and_small", in_specs=[_ANY] * (2 * n) + [vm], out_specs=[_ANY] * (2 * n) + [vm],
        out_shape=[_sds(t.shape, t.dtype) for t in arrays] + [_sds(small.shape, F32)],
        scratch_shapes=[pltpu.VMEM((8,) + small.shape, F32), pltpu.SemaphoreType.DMA((n,)),
                        pltpu.SemaphoreType.DMA((n,)), pltpu.SemaphoreType.DMA((7,)), pltpu.SemaphoreType.DMA((7,))],
        input_output_aliases={i: i for i in range(2 * n)},
    )(*arrays, small)
    return outs[:n], outs[n:2 * n], outs[2 * n]


def _row_tile(rh):
    for cand in (512, 384, 352, 256, 128):
        if rh % cand == 0:
            return cand
    return rh


def _add_halves(name, g5, recv, place):
    _, _, rh, cols = g5.shape
    tr = _row_tile(rh)

    def body(place_ref, g_ref, r_ref, o_ref, own_ref):
        val = (g_ref[...] + r_ref[...]).astype(BF16)
        o_ref[...] = val

        @pl.when(pl.program_id(1) == place_ref[1])
        def _():
            own_ref[...] = val

    grid_spec = pltpu.PrefetchScalarGridSpec(
        num_scalar_prefetch=1, grid=(rh // tr, NCHIP),
        in_specs=[pl.BlockSpec((None, None, tr, cols), lambda i, j, pr: (j, pr[0], i, 0)),
                  pl.BlockSpec((None, tr, cols), lambda i, j, pr: (j, i, 0))],
        out_specs=[pl.BlockSpec((None, tr, cols), lambda i, j, pr: (j, i, 0)),
                   pl.BlockSpec((None, tr, cols), lambda i, j, pr: (pr[1], i, 0))])
    return pl.pallas_call(
        body, name=name, grid_spec=grid_spec, out_shape=[_sds(recv.shape, BF16)] * 2, compiler_params=_params(2),
    )(place, g5, recv)


def _sum_chips(name, landing, place, layer, full):
    _, rh, cols = landing.shape
    tr = _row_tile(rh)
    has_full = full is not None

    def body(*refs):
        r_ref, o_ref = refs[1], refs[-1]
        total = r_ref[0].astype(F32)
        for j in range(1, NCHIP):
            total = total + r_ref[j].astype(F32)
        o_ref[...] = total

    grid_spec = pltpu.PrefetchScalarGridSpec(
        num_scalar_prefetch=1, grid=(rh // tr,),
        in_specs=[pl.BlockSpec((NCHIP, tr, cols), lambda i, pr: (0, i, 0))] + ([_ANY] if has_full else []),
        out_specs=pl.BlockSpec((None, None, tr, cols), lambda i, pr: (layer, pr[0], i, 0)))
    return pl.pallas_call(
        body, name=name, grid_spec=grid_spec, out_shape=_sds((DEPTH, 2, rh, cols), F32),
        input_output_aliases={2: 0} if has_full else {}, compiler_params=_params(1),
    )(place, landing, *([full] if has_full else []))


class _Schedule:
    FIRST, REST = ["in"], list(_KEYS[1:])

    def __init__(self, slotted, cw4, place):
        self.place = place
        self._w = [dict(zip(_KEYS, layer)) for layer in slotted]
        got, self.cw4 = _gather_first([self._w[0][k] for k in self.FIRST], cw4)
        self._w[0].update(zip(self.FIRST, got))
        self._g5, self._recv, self._parts, self._landing = [{}, {}], [{}, {}], [{}, {}], [{}, {}]
        self.full = {}
        every = list(_KEYS)
        self._hooks = {
            "mm_z0": lambda: self._gather(_exchange_gather_ici, 0, self.REST),
            "attn_fwd0_0": lambda: self._gather(_exchange_gather_d2d, 0, self.REST),
            "attn_fwd1_0": lambda: self._gather(_exchange_gather_ici, 1, self.FIRST),
            "mm_up0": lambda: self._gather(_exchange_gather_ici, 1, self.REST),
            "conv_fwd0": lambda: self._gather(_exchange_gather_d2d, 1, every),
            "dg_down0": lambda: self._halves(1, every),
            "conv_bwd0": lambda: self._chips(1, every),
            "dg_up0": lambda: self._share(1, every),
            "attn_bwd0_0": lambda: self._halves(0, self.REST),
            "dg_z0": lambda: self._chips(0, self.REST),
            "wg_z0": lambda: self._share(0, self.REST),
        }

    def weights(self, layer):
        return self._w[layer]

    def hook(self, name):
        make = self._hooks.get(name)
        return make() if make else None

    def grads_ready(self, layer, GG):
        for k, t in GG.items():
            g5 = t.reshape(NCHIP, 2, t.shape[1] // 2, t.shape[2])
            self._g5[layer][k] = g5
            self._recv[layer][k] = lax.empty((NCHIP,) + g5.shape[2:], F32)

    def _gather(self, make, layer, keys):
        def done(arrays):
            self._w[layer].update(zip(keys, arrays))
        return make([self._w[layer][k] for k in keys], done)

    def _halves(self, layer, keys):
        return _exchange_halves([self._g5[layer][k] for k in keys], [self._recv[layer][k] for k in keys],
                                lambda arrays: self._halves_done(layer, keys, arrays))

    def _halves_done(self, layer, keys, arrays):
        n = len(keys)
        for k, g, r in zip(keys, arrays[:n], arrays[n:]):
            self._parts[layer][k], self._landing[layer][k] = _add_halves(
                "add_halves%d_%s" % (layer, k), g, r, self.place)

    def _chips(self, layer, keys):
        return _exchange_chips([self._parts[layer][k] for k in keys], [self._landing[layer][k] for k in keys],
                               lambda arrays: self._chips_done(layer, keys, arrays))

    def _chips_done(self, layer, keys, arrays):
        for k, t in zip(keys, arrays[len(keys):]):
            self.full[k] = _sum_chips("sum_chips%d_%s" % (layer, k), t, self.place, layer, self.full.get(k))

    def _share(self, layer, keys):
        def done(arrays):
            self.full.update(zip(keys, arrays))
        return _exchange_share([self.full[k] for k in keys], layer, done)

    def finish(self, small):
        keys = self.FIRST
        g5, recv, small_red = _halves_and_small([self._g5[0][k] for k in keys], [self._recv[0][k] for k in keys], small)
        self._halves_done(0, keys, list(g5) + list(recv))
        _exchange_call("exchange_chips_last", self._chips(0, keys))
        _exchange_call("share_halves_last", self._share(0, keys))
        return [self.full[k] for k in _KEYS], small_red


def _adamw(name, w, g, m, v):
    shape = w.shape
    cols = shape[-1]
    rows = 1
    for s in shape[:-1]:
        rows *= s
    tr = rows
    for cand in (256, 128, 64):
        if rows > cand and rows % cand == 0:
            tr = cand
            break
    c1 = 1.0 / (1.0 - B1 ** STEP)
    c2 = 1.0 / (1.0 - B2 ** STEP)

    def body(w_ref, g_ref, m_ref, v_ref, d_ref, nm_ref, nv_ref):
        gv = g_ref[...]
        nm = B1 * m_ref[...] + (1.0 - B1) * gv
        nv = B2 * v_ref[...] + (1.0 - B2) * (gv * gv)
        nm_ref[...] = nm
        nv_ref[...] = nv
        d_ref[...] = -LR * ((nm * c1) / (jnp.sqrt(nv * c2) + ADAM_EPS) + WD * w_ref[...])

    blk = pl.BlockSpec((tr, cols), lambda i: (i, 0))
    outs = pl.pallas_call(
        body, name=name, grid=(rows // tr,), in_specs=[blk] * 4, out_specs=[blk] * 3,
        out_shape=[_sds((rows, cols), F32)] * 3, compiler_params=_params(1),
    )(*(t.reshape(rows, cols) for t in (w, g, m, v)))
    return tuple(o.reshape(shape) for o in outs)


def _pack_small(small):
    rows = [jnp.sum(small["g_mix%d" % l], axis=0, keepdims=True) for l in range(DEPTH)]
    rows += [jnp.sum(small["pool_scale%d" % l], axis=0, keepdims=True) for l in range(DEPTH)]
    rows += [jnp.sum(small["g_ffn%d" % l], axis=0, keepdims=True) for l in range(DEPTH)]
    rows += [jnp.sum(small["g_ple%d" % l], axis=0, keepdims=True) for l in range(DEPTH)]
    rows += [jnp.sum(small["g_final"], axis=0, keepdims=True)]
    flat = [small["conv_b%d" % l].reshape(-1) for l in range(DEPTH)]
    flat += [small["conv_w%d" % l].reshape(-1) for l in range(DEPTH)]
    flat = jnp.concatenate(flat).reshape(-1, D)
    packed = jnp.concatenate(rows + [flat], axis=0)
    return jnp.pad(packed, ((0, SMALL_ROWS - packed.shape[0]), (0, 0)))


def _unpack_small(red):
    g_mix, pool_scale, g_ffn, g_ple = red[0:2], red[2:4], red[4:6], red[6:8]
    g_final = red[8]
    nb = DEPTH * UW // D
    conv_b = red[9:9 + nb].reshape(DEPTH, UW)
    conv_w = red[9 + nb:9 + 4 * nb].reshape(DEPTH, 3, UW)
    return g_mix, pool_scale, g_ffn, g_ple, g_final, conv_b, conv_w


def kernel(x, p, g_mix, w_in, w_ya, w_yb, pool_w, pool_scale, w_o, g_ffn, w_up, conv_w, conv_b, w_down, g_ple, w_ple, w_ple_gate, g_final, loss_target, m_g_mix, m_w_in, m_w_ya, m_w_yb, m_pool_w, m_pool_scale, m_w_o, m_g_ffn, m_w_up, m_conv_w, m_conv_b, m_w_down, m_g_ple, m_w_ple, m_w_ple_gate, m_g_final, v_g_mix, v_w_in, v_w_ya, v_w_yb, v_pool_w, v_pool_scale, v_w_o, v_g_ffn, v_w_up, v_conv_w, v_conv_b, v_w_down, v_g_ple, v_w_ple, v_w_ple_gate, v_g_final):
    me = 2 * lax.axis_index("x") + lax.axis_index("y")
    place = jnp.stack([lax.axis_index("c"), me]).astype(jnp.int32)

    def slot(shard):
        return lax.dynamic_update_index_in_dim(lax.empty((NCHIP,) + shard.shape, shard.dtype), shard, me, 0)

    packed = [
        w_in.astype(BF16), w_up.astype(BF16),
        jnp.concatenate([w_ya, w_ple, pool_w.reshape(DEPTH, 256, 256)], axis=1).astype(BF16),
        jnp.concatenate([w_yb, w_o, w_ple_gate], axis=1).astype(BF16),
        w_down.astype(BF16),
    ]
    slotted = [[slot(t[l]) for t in packed] for l in range(DEPTH)]
    ex = _Schedule(slotted, slot(conv_w.reshape(DEPTH * 3, UP_S)), place)
    cw_full = ex.cw4.reshape(NCHIP, DEPTH, 3, UP_S).transpose(1, 2, 0, 3).reshape(DEPTH, 3, UW)

    vecs = dict(g_mix=g_mix, pool_scale=pool_scale, g_ffn=g_ffn, g_ple=g_ple, g_final=g_final, conv_b=conv_b,
                conv_w=cw_full)
    sq8, grad_x, small = _local_step(x, p, loss_target, vecs, ex)
    loss = lax.psum(jnp.sum(sq8) * (0.5 / D), ("x", "y", "c"))

    full, small_red = ex.finish(_pack_small(small))
    r_in, r_up, r_sc, r_r3, r_dn = [f.reshape(DEPTH, -1, f.shape[-1]) for f in full]
    d_g_mix, d_pool_scale, d_g_ffn, d_g_ple, d_g_final, d_conv_b, d_conv_w_full = _unpack_small(small_red)
    d_conv_w = lax.dynamic_slice_in_dim(d_conv_w_full, me * UP_S, UP_S, axis=2)

    grads = dict(
        g_mix=d_g_mix, w_in=r_in, w_ya=r_sc[:, 0:512], w_yb=r_r3[:, 0:256],
        pool_w=r_sc[:, 768:1024].reshape(DEPTH, 4, 64, 256), pool_scale=d_pool_scale, w_o=r_r3[:, 256:512],
        g_ffn=d_g_ffn, w_up=r_up, conv_w=d_conv_w, conv_b=d_conv_b, w_down=r_dn, g_ple=d_g_ple,
        w_ple=r_sc[:, 512:768], w_ple_gate=r_r3[:, 512:768], g_final=d_g_final)
    weights = dict(g_mix=g_mix, w_in=w_in, w_ya=w_ya, w_yb=w_yb, pool_w=pool_w, pool_scale=pool_scale, w_o=w_o,
                   g_ffn=g_ffn, w_up=w_up, conv_w=conv_w, conv_b=conv_b, w_down=w_down, g_ple=g_ple, w_ple=w_ple,
                   w_ple_gate=w_ple_gate, g_final=g_final)
    m_in = dict(g_mix=m_g_mix, w_in=m_w_in, w_ya=m_w_ya, w_yb=m_w_yb, pool_w=m_pool_w, pool_scale=m_pool_scale,
                w_o=m_w_o, g_ffn=m_g_ffn, w_up=m_w_up, conv_w=m_conv_w, conv_b=m_conv_b, w_down=m_w_down,
                g_ple=m_g_ple, w_ple=m_w_ple, w_ple_gate=m_w_ple_gate, g_final=m_g_final)
    v_in = dict(g_mix=v_g_mix, w_in=v_w_in, w_ya=v_w_ya, w_yb=v_w_yb, pool_w=v_pool_w, pool_scale=v_pool_scale,
                w_o=v_w_o, g_ffn=v_g_ffn, w_up=v_w_up, conv_w=v_conv_w, conv_b=v_conv_b, w_down=v_w_down,
                g_ple=v_g_ple, w_ple=v_w_ple, w_ple_gate=v_w_ple_gate, g_final=v_g_final)
    names = ["g_mix", "w_in", "w_ya", "w_yb", "pool_w", "pool_scale", "w_o", "g_ffn", "w_up", "conv_w", "conv_b",
             "w_down", "g_ple", "w_ple", "w_ple_gate", "g_final"]
    deltas, new_m, new_v = [], [], []
    for nme in names:
        gr = grads[nme].reshape(weights[nme].shape)
        grads[nme] = gr
        dlt, nm, nv = _adamw("adamw_" + nme, weights[nme], gr, m_in[nme], v_in[nme])
        deltas.append(dlt)
        new_m.append(nm)
        new_v.append(nv)
    return (loss, grad_x, *[grads[nme] for nme in names], *deltas, *new_m, *new_v)
```

```python
import math

import jax
import jax.numpy as jnp
from jax import lax
from jax.experimental import pallas as pl
from jax.experimental.pallas import tpu as pltpu

F32 = jnp.float32
BF16 = jnp.bfloat16
_KEYS = ("in", "up", "sc", "r3", "dn")
MESH = pl.DeviceIdType.MESH

D = 1024
SEQ = 2048
DEPTH = 2
HEAD = 128
GROUP_W = 512
DILATIONS = (1, 4, 16)
ROPE_DIM = 32
ROPE_THETA = 500000.0
NEG_INF = -1e30
ZW = 7680
OFF_K, OFF_V, OFF_U, OFF_GA, OFF_GB = 1536, 3072, 4608, 5632, 6656
FF = 2816
UW = 2 * FF
PLE = 256
NCHIP = 4
IN_S, UP_S, DN_S = ZW // NCHIP, UW // NCHIP, FF // NCHIP
RMS_EPS = 1e-6
LR, B1, B2, ADAM_EPS, WD, STEP = 0.001, 0.9, 0.999, 1e-08, 0.01, 10
SMALL_ROWS = 56
VMEM_CAP = 48 * 1024 * 1024


def _params(n_grid, vmem=VMEM_CAP):
    return pltpu.CompilerParams(dimension_semantics=("arbitrary",) * n_grid, vmem_limit_bytes=vmem)


def _sigmoid(v):
    return 1.0 / (1.0 + jnp.exp(-v))


def _rows8(v):
    return jnp.sum(v.reshape(v.shape[0] // 8, 8, v.shape[1]), axis=0)


def _sds(shape, dtype):
    return jax.ShapeDtypeStruct(shape, dtype)


def _dot(av, bv, ta=False, tb=False):
    dims = (((0,) if ta else (1,), (1,) if tb else (0,)), ((), ()))
    return lax.dot_general(av.astype(BF16), bv.astype(BF16), dims, preferred_element_type=F32)


def _call(body, *, name, grid, in_specs, out_specs, out_shape, scratch_shapes=(), aliases=None, comm=None):
    params = _params(len(grid))
    aliases = dict(aliases or {})
    if comm is None:
        return pl.pallas_call(body, name=name, grid=grid, in_specs=list(in_specs), out_specs=out_specs,
                              out_shape=out_shape, scratch_shapes=list(scratch_shapes),
                              input_output_aliases=aliases, compiler_params=params)
    single = not isinstance(out_shape, (list, tuple))
    out_specs_l = [out_specs] if single else list(out_specs)
    out_shape_l = [out_shape] if single else list(out_shape)
    n_in, n_out, n_c = len(in_specs), len(out_shape_l), len(comm["arrays"])

    def hosted(*refs):
        core_in, core_out = refs[:n_in], refs[n_in + n_c:n_in + n_c + n_out]
        c_refs = refs[n_in + n_c + n_out:n_in + 2 * n_c + n_out]
        scratch, (send_sems, recv_sems) = refs[n_in + 2 * n_c + n_out:-2], refs[-2:]
        ids = [pl.program_id(i) for i in range(len(grid))]
        first, last = ids[0] == 0, ids[0] == grid[0] - 1
        for i in range(1, len(grid)):
            first = jnp.logical_and(first, ids[i] == 0)
            last = jnp.logical_and(last, ids[i] == grid[i] - 1)

        @pl.when(first)
        def _():
            comm["start"](c_refs, send_sems, recv_sems)

        body(*core_in, *core_out, *scratch)

        @pl.when(last)
        def _():
            comm["wait"](c_refs, send_sems, recv_sems)

    any_spec = pl.BlockSpec(memory_space=pl.ANY)
    for i in range(n_c):
        aliases[n_in + i] = n_out + i
    call = pl.pallas_call(
        hosted, name=name, grid=grid, in_specs=list(in_specs) + [any_spec] * n_c,
        out_specs=out_specs_l + [any_spec] * n_c,
        out_shape=out_shape_l + [_sds(t.shape, t.dtype) for t in comm["arrays"]],
        scratch_shapes=list(scratch_shapes) + [pltpu.SemaphoreType.DMA((comm["nsem"],))] * 2,
        input_output_aliases=aliases, compiler_params=params)

    def run(*args):
        outs = call(*args, *comm["arrays"])
        comm["done"](outs[n_out:])
        return outs[0] if single else outs[:n_out]

    return run


def _mm_call(name, a, b, *, grid, a_spec, b_spec, o_spec, out_shape, ta=False, tb=False, k_axis=None, nk=1,
             acc_shape=None, res=None, res_spec=None, buf=None, compute=None, comm=None,
             extra_in=(), extra_out=(), epilogue=None):
    has_res, has_buf = res is not None, buf is not None
    in_place = nk > 1 and not has_res and out_shape.dtype == F32 and epilogue is None
    n_xi, n_xo = len(extra_in), len(extra_out)

    def body(*refs):
        a_ref, b_ref = refs[0], refs[1]
        pos = 2
        r_ref = None
        if has_res:
            r_ref = refs[pos]
            pos += 1
        if has_buf:
            pos += 1
        x_refs = refs[pos:pos + n_xi]
        pos += n_xi
        first_rows = pl.program_id(0) == 0
        o_ref = refs[pos]
        y_refs = refs[pos + 1:pos + 1 + n_xo]
        if compute is None:
            av = a_ref[...]
            bv = b_ref[...]
            part = _dot(av.reshape(-1, av.shape[-1]), bv.reshape(-1, bv.shape[-1]), ta, tb)
        else:
            part = compute(a_ref, b_ref)

        def finish(val):
            if r_ref is not None:
                val = val + r_ref[...]
            if epilogue is not None:
                epilogue(val, x_refs, o_ref, y_refs, first_rows)
            else:
                o_ref[...] = val.reshape(o_ref.shape).astype(o_ref.dtype)

        if nk == 1:
            finish(part)
        elif in_place:
            @pl.when(pl.program_id(k_axis) == 0)
            def _():
                o_ref[...] = jnp.zeros(o_ref.shape, F32)

            o_ref[...] += part.reshape(o_ref.shape)
        else:
            acc_ref = refs[pos + 1 + n_xo]
            k = pl.program_id(k_axis)

            @pl.when(k == 0)
            def _():
                acc_ref[...] = jnp.zeros(acc_ref.shape, F32)

            acc_ref[...] += part

            @pl.when(k == nk - 1)
            def _():
                finish(acc_ref[...])

    ins, in_specs = [a, b], [a_spec, b_spec]
    if has_res:
        ins.append(res)
        in_specs.append(res_spec)
    aliases = {}
    if has_buf:
        aliases = {len(ins): 0}
        ins.append(buf)
        in_specs.append(pl.BlockSpec(memory_space=pl.ANY))
    for arr, sp in extra_in:
        ins.append(arr)
        in_specs.append(sp)
    scratch = [pltpu.VMEM(acc_shape, F32)] if nk > 1 and not in_place else []
    if not extra_out:
        return _call(body, name=name, grid=grid, in_specs=in_specs, out_specs=o_spec, out_shape=out_shape,
                     scratch_shapes=scratch, aliases=aliases, comm=comm)(*ins)
    return _call(body, name=name, grid=grid, in_specs=in_specs, out_specs=[o_spec] + [sp for _, sp in extra_out],
                 out_shape=[out_shape] + [sh for sh, _ in extra_out], scratch_shapes=scratch, aliases=aliases,
                 comm=comm)(*ins)


def _rms_fwd(name, x, g, tr=512):
    T = x.shape[0]

    def body(x_ref, g_ref, h_ref):
        xv = x_ref[...]
        r = lax.rsqrt(jnp.mean(xv * xv, axis=-1, keepdims=True) + RMS_EPS)
        h_ref[...] = (xv * r * g_ref[...]).astype(BF16)

    return pl.pallas_call(
        body, name=name, grid=(T // tr,),
        in_specs=[pl.BlockSpec((tr, D), lambda i: (i, 0)), pl.BlockSpec((1, D), lambda i: (0, 0))],
        out_specs=pl.BlockSpec((tr, D), lambda i: (i, 0)), out_shape=_sds((T, D), BF16),
        compiler_params=_params(1),
    )(x, g)


def _rms_bwd(name, x, dh, g, dres, tr=512):
    T = x.shape[0]

    def body(x_ref, dh_ref, g_ref, dres_ref, dx_ref, dg_ref):
        xv = x_ref[...]
        r = lax.rsqrt(jnp.mean(xv * xv, axis=-1, keepdims=True) + RMS_EPS)
        xh = xv * r
        dhv = dh_ref[...]
        part = _rows8(dhv * xh)

        @pl.when(pl.program_id(0) == 0)
        def _():
            dg_ref[...] = part

        @pl.when(pl.program_id(0) > 0)
        def _():
            dg_ref[...] += part

        dxh = dhv * g_ref[...]
        dx_ref[...] = dres_ref[...] + r * (dxh - xh * jnp.mean(dxh * xh, axis=-1, keepdims=True))

    row = pl.BlockSpec((tr, D), lambda i: (i, 0))
    return pl.pallas_call(
        body, name=name, grid=(T // tr,),
        in_specs=[row, row, pl.BlockSpec((1, D), lambda i: (0, 0)), row],
        out_specs=[row, pl.BlockSpec((8, D), lambda i: (0, 0))],
        out_shape=[_sds((T, D), F32), _sds((8, D), F32)],
        compiler_params=_params(1),
    )(x, dh, g, dres)


def _final_loss(x, g, tgt, tr=512):
    T = x.shape[0]

    def body(x_ref, g_ref, t_ref, dx_ref, dg_ref, sq_ref):
        xv = x_ref[...]
        r = lax.rsqrt(jnp.mean(xv * xv, axis=-1, keepdims=True) + RMS_EPS)
        xh = xv * r
        gv = g_ref[...]
        e = xh * gv - t_ref[...]
        dy = e * (1.0 / D)
        pg = _rows8(dy * xh)
        ps = _rows8(e * e)

        @pl.when(pl.program_id(0) == 0)
        def _():
            dg_ref[...] = pg
            sq_ref[...] = ps

        @pl.when(pl.program_id(0) > 0)
        def _():
            dg_ref[...] += pg
            sq_ref[...] += ps

        dxh = dy * gv
        dx_ref[...] = r * (dxh - xh * jnp.mean(dxh * xh, axis=-1, keepdims=True))

    row = pl.BlockSpec((tr, D), lambda i: (i, 0))
    acc = pl.BlockSpec((8, D), lambda i: (0, 0))
    return pl.pallas_call(
        body, name="final_loss", grid=(T // tr,),
        in_specs=[row, pl.BlockSpec((1, D), lambda i: (0, 0)), row],
        out_specs=[row, acc, acc],
        out_shape=[_sds((T, D), F32), _sds((8, D), F32), _sds((8, D), F32)],
        compiler_params=_params(1),
    )(x, g, tgt)


def _ple_fwd(name, x, pe, pg, tr=512):
    T = x.shape[0]

    def body(x_ref, pe_ref, pg_ref, o_ref):
        o_ref[...] = x_ref[...] + pe_ref[...] * _sigmoid(pg_ref[...])

    row = pl.BlockSpec((tr, D), lambda i: (i, 0))
    return pl.pallas_call(
        body, name=name, grid=(T // tr,), in_specs=[row, row, row], out_specs=row,
        out_shape=_sds((T, D), F32), compiler_params=_params(1),
    )(x, pe, pg)


def _ple_bwd(name, dx, pe, pg, tr=512):
    T = dx.shape[0]

    def body(dx_ref, pe_ref, pg_ref, dpe_ref, dpg_ref):
        s = _sigmoid(pg_ref[...])
        dxv = dx_ref[...]
        dpe_ref[...] = (dxv * s).astype(BF16)
        dpg_ref[...] = (dxv * pe_ref[...] * s * (1.0 - s)).astype(BF16)

    row = pl.BlockSpec((tr, D), lambda i: (i, 0))
    return pl.pallas_call(
        body, name=name, grid=(T // tr,), in_specs=[row, row, row], out_specs=[row, row],
        out_shape=[_sds((T, D), BF16), _sds((T, D), BF16)], compiler_params=_params(1),
    )(dx, pe, pg)


def _gate_fwd(name, z, ya, yb, tr=512):
    T = z.shape[0]
    w = 512

    def body(ga_ref, gb_ref, ya_ref, yb_ref, o_ref):
        o_ref[...] = (_sigmoid(ga_ref[...]) * ya_ref[...] + _sigmoid(gb_ref[...]) * yb_ref[...]).astype(BF16)

    col = pl.BlockSpec((tr, w), lambda i, j: (i, j))
    return pl.pallas_call(
        body, name=name, grid=(T // tr, D // w),
        in_specs=[pl.BlockSpec((tr, w), lambda i, j: (i, OFF_GA // w + j)),
                  pl.BlockSpec((tr, w), lambda i, j: (i, OFF_GB // w + j)), col, col],
        out_specs=col, out_shape=_sds((T, D), BF16), compiler_params=_params(2),
    )(z, z, ya, yb)


def _gate_bwd(name, z, off, y, dm, dz, tr=512):
    T = z.shape[0]
    w = 512
    has_dz = dz is not None

    def body(*refs):
        g_ref, y_ref, dm_ref = refs[:3]
        dy_ref, dz_ref = refs[-2:]
        s = _sigmoid(g_ref[...])
        dmv = dm_ref[...]
        dy_ref[...] = (dmv * s).astype(BF16)
        dz_ref[...] = (dmv * y_ref[...] * s * (1.0 - s)).astype(BF16)

    col = pl.BlockSpec((tr, w), lambda i, j: (i, j))
    gcol = pl.BlockSpec((tr, w), lambda i, j: (i, off // w + j))
    ins, in_specs, aliases = [z, y, dm], [gcol, col, col], {}
    if has_dz:
        ins.append(dz)
        in_specs.append(pl.BlockSpec(memory_space=pl.ANY))
        aliases = {3: 1}
    return pl.pallas_call(
        body, name=name, grid=(T // tr, D // w), in_specs=in_specs, out_specs=[col, gcol],
        out_shape=[_sds((T, D), BF16), _sds((T, ZW), BF16)], input_output_aliases=aliases,
        compiler_params=_params(2),
    )(*ins)


def _shift_down(v, k, rows):
    return jnp.where(rows >= k, pltpu.roll(v, k, 0), 0.0)


def _shift_up(v, k, rows):
    n = v.shape[0]
    return jnp.where(rows < n - k, pltpu.roll(v, n - k, 0), 0.0)


def _pool_window(v, g, rows, shift):
    s2 = v + shift(v, 1, rows)
    s4 = s2 + shift(s2, 2, rows)
    s8 = s4 + shift(s4, 4, rows)
    s16 = s8 + shift(s8, 8, rows)
    return jnp.where(g == 0, s2, jnp.where(g == 1, s4, jnp.where(g == 2, s8, s16)))


def _pool_count(g, rows):
    wlen = jnp.left_shift(2, g).astype(F32)
    return jnp.minimum(rows.astype(F32) + 1.0, wlen)


def _pool_fwd(name, z3, g_sc, scale):
    Bn = z3.shape[0]
    gw = 256

    def body(u_ref, pw_ref, sc_ref, pooled_ref, ms_ref):
        g = pl.program_id(1)
        u = u_ref[...]
        rows = lax.broadcasted_iota(jnp.int32, u.shape, 0)
        pooled = (_pool_window(u, g, rows, _shift_down) / _pool_count(g, rows) - u).astype(BF16)
        pooled_ref[...] = pooled
        pw = pw_ref[...].reshape(gw, gw)
        mixed = jnp.dot(pooled, pw, preferred_element_type=F32)
        ms_ref[...] = (mixed * sc_ref[...]).astype(BF16)

    blk = pl.BlockSpec((None, SEQ, gw), lambda b, g: (b, 0, g))
    return pl.pallas_call(
        body, name=name, grid=(Bn, 4),
        in_specs=[pl.BlockSpec((None, SEQ, gw), lambda b, g: (b, 0, OFF_U // gw + g)),
                  pl.BlockSpec((NCHIP, 64, gw), lambda b, g: (0, 12 + g, 0)),
                  pl.BlockSpec((1, gw), lambda b, g: (0, g))],
        out_specs=[blk, blk],
        out_shape=[_sds((Bn, SEQ, D), BF16), _sds((Bn, SEQ, D), BF16)],
        compiler_params=_params(2),
    )(z3, g_sc, scale)


def _pool_bwd(name, dms3, pooled3, g_sc, scale, dz3, gg_sc):
    Bn = dms3.shape[0]
    gw = 256
    has_gg = gg_sc is not None

    def body(*refs):
        dms_ref, pooled_ref, pw_ref, sc_ref = refs[:4]
        dz_ref, dpw_ref, dsc_ref = refs[-3:]
        g, b = pl.program_id(0), pl.program_id(1)
        pooled = pooled_ref[...]
        pw = pw_ref[...].reshape(gw, gw)
        dms = dms_ref[...]
        mixed = jnp.dot(pooled, pw, preferred_element_type=F32)
        psc = _rows8(dms * mixed)
        dmixed = (dms * sc_ref[...]).astype(BF16)
        dpw = lax.dot_general(pooled, dmixed, (((0,), (0,)), ((), ())), preferred_element_type=F32)
        dpw = dpw.reshape(NCHIP, 64, gw)

        @pl.when(b == 0)
        def _():
            dsc_ref[...] = psc
            dpw_ref[...] = dpw

        @pl.when(b > 0)
        def _():
            dsc_ref[...] += psc
            dpw_ref[...] += dpw

        dpooled = lax.dot_general(dmixed, pw, (((1,), (1,)), ((), ())), preferred_element_type=F32)
        rows = lax.broadcasted_iota(jnp.int32, dpooled.shape, 0)
        dq = dpooled / _pool_count(g, rows)
        dz_ref[...] = (_pool_window(dq, g, rows, _shift_up) - dpooled).astype(BF16)

    ins = [dms3, pooled3, g_sc, scale, dz3]
    in_specs = [pl.BlockSpec((None, SEQ, gw), lambda g, b: (b, 0, g)),
                pl.BlockSpec((None, SEQ, gw), lambda g, b: (b, 0, g)),
                pl.BlockSpec((NCHIP, 64, gw), lambda g, b: (0, 12 + g, 0)),
                pl.BlockSpec((1, gw), lambda g, b: (0, g)),
                pl.BlockSpec(memory_space=pl.ANY)]
    aliases = {4: 0}
    if has_gg:
        ins.append(gg_sc)
        in_specs.append(pl.BlockSpec(memory_space=pl.ANY))
        aliases[5] = 1
    return pl.pallas_call(
        body, name=name, grid=(4, Bn), in_specs=in_specs,
        out_specs=[pl.BlockSpec((None, SEQ, gw), lambda g, b: (b, 0, OFF_U // gw + g)),
                   pl.BlockSpec((NCHIP, 64, gw), lambda g, b: (0, 12 + g, 0)),
                   pl.BlockSpec((8, gw), lambda g, b: (0, g))],
        out_shape=[_sds(dz3.shape, BF16), _sds((NCHIP, D, 256), F32), _sds((8, D), F32)],
        input_output_aliases=aliases, compiler_params=_params(2),
    )(*ins)


CT = 256
NCT = FF // CT


def _conv_pre(u, cw_ref, cb_ref, rows):
    return (cb_ref[...] + cw_ref[0:1, :] * _shift_down(u, 2, rows) + cw_ref[1:2, :] * _shift_down(u, 1, rows)
            + cw_ref[2:3, :] * u)


def _conv_fwd(name, u3, cw, cb, comm=None):
    Bn = u3.shape[0]

    def body(ug_ref, uv_ref, cwg_ref, cwv_ref, cbg_ref, cbv_ref, a_ref):
        ug, uv = ug_ref[...], uv_ref[...]
        rows = lax.broadcasted_iota(jnp.int32, ug.shape, 0)
        yg = _conv_pre(ug, cwg_ref, cbg_ref, rows)
        yv = _conv_pre(uv, cwv_ref, cbv_ref, rows)
        a_ref[...] = (yg * _sigmoid(yg) * yv).astype(BF16)

    def blk(off):
        return pl.BlockSpec((None, SEQ, CT), lambda b, c: (b, 0, off + c))

    return _call(
        body, name=name, grid=(Bn, NCT),
        in_specs=[blk(0), blk(NCT),
                  pl.BlockSpec((3, CT), lambda b, c: (0, c)), pl.BlockSpec((3, CT), lambda b, c: (0, NCT + c)),
                  pl.BlockSpec((1, CT), lambda b, c: (0, c)), pl.BlockSpec((1, CT), lambda b, c: (0, NCT + c))],
        out_specs=blk(0), out_shape=_sds((Bn, SEQ, FF), BF16), comm=comm,
    )(u3, u3, cw, cw, cb, cb)


def _conv_bwd(name, da3, u3, cw, cb, comm=None):
    Bn = u3.shape[0]
    last = NCT * Bn - 1

    def body(da_ref, ug_ref, uv_ref, cwg_ref, cwv_ref, cbg_ref, cbv_ref,
             du_ref, dcwg_ref, dcwv_ref, dcbg_ref, dcbv_ref, stage_g, stage_v, sems):
        c, b = pl.program_id(0), pl.program_id(1)
        step = c * Bn + b
        ug, uv, da = ug_ref[...], uv_ref[...], da_ref[...]
        rows = lax.broadcasted_iota(jnp.int32, ug.shape, 0)
        yg = _conv_pre(ug, cwg_ref, cbg_ref, rows)
        yv = _conv_pre(uv, cwv_ref, cbv_ref, rows)
        s = _sigmoid(yg)
        dyv = da * (yg * s)
        dyg = da * yv * (s * (1.0 + yg * (1.0 - s)))

        def writes(off_c, stage, sem):
            col = pl.multiple_of(off_c + c * CT, CT)
            return pltpu.make_async_copy(stage, du_ref.at[b, :, pl.ds(col, CT)], sem)

        @pl.when(step > 0)
        def _():
            writes(0, stage_g, sems.at[0]).wait()
            writes(FF, stage_v, sems.at[1]).wait()

        for dy, u, cw_ref, stage, dcw_ref, dcb_ref in ((dyg, ug, cwg_ref, stage_g, dcwg_ref, dcbg_ref),
                                                       (dyv, uv, cwv_ref, stage_v, dcwv_ref, dcbv_ref)):
            d1, d2 = _shift_up(dy, 1, rows), _shift_up(dy, 2, rows)
            stage[...] = (cw_ref[2:3, :] * dy + cw_ref[1:2, :] * d1 + cw_ref[0:1, :] * d2).astype(BF16)
            dcw = jnp.concatenate([jnp.sum(d2 * u, axis=0, keepdims=True), jnp.sum(d1 * u, axis=0, keepdims=True),
                                   jnp.sum(dy * u, axis=0, keepdims=True)], axis=0)
            dcb = jnp.sum(dy, axis=0, keepdims=True)

            @pl.when(b == 0)
            def _():
                dcw_ref[...] = dcw
                dcb_ref[...] = dcb

            @pl.when(b > 0)
            def _():
                dcw_ref[...] += dcw
                dcb_ref[...] += dcb

        writes(0, stage_g, sems.at[0]).start()
        writes(FF, stage_v, sems.at[1]).start()

        @pl.when(step == last)
        def _():
            writes(0, stage_g, sems.at[0]).wait()
            writes(FF, stage_v, sems.at[1]).wait()

    def blk(off):
        return pl.BlockSpec((None, SEQ, CT), lambda c, b: (b, 0, off + c))

    def vec(r, off):
        return pl.BlockSpec((r, CT), lambda c, b: (0, off + c))

    du3, dcwg, dcwv, dcbg, dcbv = _call(
        body, name=name, grid=(NCT, Bn),
        in_specs=[blk(0), blk(0), blk(NCT), vec(3, 0), vec(3, NCT), vec(1, 0), vec(1, NCT)],
        out_specs=[pl.BlockSpec(memory_space=pl.ANY), vec(3, 0), vec(3, 0), vec(1, 0), vec(1, 0)],
        out_shape=[_sds((Bn, SEQ, UW), BF16), _sds((3, FF), F32), _sds((3, FF), F32), _sds((1, FF), F32),
                   _sds((1, FF), F32)],
        scratch_shapes=[pltpu.VMEM((SEQ, CT), BF16)] * 2 + [pltpu.SemaphoreType.DMA((2,))], comm=comm,
    )(da3, u3, u3, cw, cw, cb, cb)
    return du3, jnp.concatenate([dcwg, dcwv], axis=1), jnp.concatenate([dcbg, dcbv], axis=1)


def _rope_tables():
    pos = jnp.arange(SEQ, dtype=F32)
    inv_freq = jnp.exp(jnp.arange(0, ROPE_DIM, 2, dtype=F32) * (-math.log(ROPE_THETA) / ROPE_DIM))
    ang = pos[:, None] * inv_freq[None, :]
    cos, sin = jnp.cos(ang), jnp.sin(ang)
    half = ROPE_DIM // 2
    zeros = jnp.zeros((SEQ, HEAD - ROPE_DIM), F32)
    zh = jnp.zeros((SEQ, half), F32)
    tab_c = jnp.concatenate([cos, cos, zeros + 1.0], axis=1)
    tab_a = jnp.concatenate([-sin, zh, zeros], axis=1)
    tab_b = jnp.concatenate([zh, sin, zeros], axis=1)
    return tab_c, tab_a, tab_b


def _rot(v, tc, ta, tb):
    half = ROPE_DIM // 2
    return v * tc + pltpu.roll(v, HEAD - half, 1) * ta + pltpu.roll(v, half, 1) * tb


def _rot_t(dv, tc, ta, tb):
    half = ROPE_DIM // 2
    return dv * tc + pltpu.roll(dv * ta, half, 1) + pltpu.roll(dv * tb, HEAD - half, 1)


def _band_masks():
    qi = lax.broadcasted_iota(jnp.int32, (HEAD, 2 * HEAD), 0)
    ki = lax.broadcasted_iota(jnp.int32, (HEAD, 2 * HEAD), 1)
    diff = HEAD + qi - ki
    both = (diff >= 0) & (diff <= HEAD)
    q1 = lax.broadcasted_iota(jnp.int32, (HEAD, HEAD), 0)
    k1 = lax.broadcasted_iota(jnp.int32, (HEAD, HEAD), 1)
    return q1 >= k1, both


_NT = (((1,), (1,)), ((), ()))
_TN = (((0,), (0,)), ((), ()))
_SCALE = HEAD ** -0.5


ATT_W = HEAD
ATT_HP = ATT_W // HEAD


def _res_rows(r, n, d, base=0):
    return pl.ds(base * d + r, n, stride=d) if d > 1 else pl.ds(base, n)


def _attn_load(q_ref, k_ref, v_ref, tc_ref, ta_ref, tb_ref, qs, ks, vs, d):
    L = SEQ // d
    for r in range(d):
        rows = _res_rows(r, L, d)
        tc, ta, tb = tc_ref[rows, :], ta_ref[rows, :], tb_ref[rows, :]
        dst = slice(r * L, (r + 1) * L)
        for hh in range(ATT_HP):
            sl = slice(hh * HEAD, (hh + 1) * HEAD)
            qs[dst, sl] = _rot(q_ref[rows, sl], tc, ta, tb).astype(BF16)
            ks[dst, sl] = _rot(k_ref[rows, sl], tc, ta, tb).astype(BF16)
            vs[dst, sl] = v_ref[rows, sl].astype(BF16)


def _attn_fwd(name, z3, tabs, g, d, comm=None):
    Bn = z3.shape[0]
    L = SEQ // d
    nb = L // HEAD
    W, nh = ATT_W, GROUP_W // ATT_W

    def body(q_ref, k_ref, v_ref, tc_ref, ta_ref, tb_ref, o_ref, l_ref, qs, ks, vs):
        m_first, m_both = _band_masks()
        _attn_load(q_ref, k_ref, v_ref, tc_ref, ta_ref, tb_ref, qs, ks, vs, d)
        for r in range(d):
            for hh in range(ATT_HP):
                sl = slice(hh * HEAD, (hh + 1) * HEAD)
                for n in range(nb):
                    rq = slice(r * L + n * HEAD, r * L + (n + 1) * HEAD)
                    rk = slice(r * L + max(n - 1, 0) * HEAD, r * L + (n + 1) * HEAD)
                    s = lax.dot_general(qs[rq, sl], ks[rk, sl], _NT, preferred_element_type=F32) * _SCALE
                    s = jnp.where(m_first if n == 0 else m_both, s, NEG_INF)
                    m = jnp.max(s, axis=-1, keepdims=True)
                    e = jnp.exp(s - m)
                    den = jnp.sum(e, axis=-1, keepdims=True)
                    p = (e * (1.0 / den)).astype(BF16)
                    rows = _res_rows(r, HEAD, d, n * HEAD)
                    o_ref[rows, sl] = jnp.dot(p, vs[rk, sl], preferred_element_type=F32)
                    l_ref[rows, sl] = jnp.broadcast_to(m + jnp.log(den), (HEAD, HEAD))

    def zcol(off):
        return pl.BlockSpec((None, SEQ, W), lambda b, h: (b, 0, (off + g * GROUP_W) // W + h))

    tab = pl.BlockSpec((SEQ, HEAD), lambda b, h: (0, 0))
    out = pl.BlockSpec((None, SEQ, W), lambda b, h: (b, 0, h))
    return _call(
        body, name=name, grid=(Bn, nh),
        in_specs=[zcol(0), zcol(OFF_K), zcol(OFF_V), tab, tab, tab],
        out_specs=[out, out],
        out_shape=[_sds((Bn, SEQ, GROUP_W), F32), _sds((Bn, SEQ, GROUP_W), F32)],
        scratch_shapes=[pltpu.VMEM((SEQ, W), BF16)] * 3, comm=comm,
    )(z3, z3, z3, *tabs)


def _attn_bwd(name, z3, tabs, g, d, do3, lse3, delta3, dz3, comm=None):
    Bn = z3.shape[0]
    L = SEQ // d
    nb = L // HEAD
    W, nh = ATT_W, GROUP_W // ATT_W

    def body(q_ref, k_ref, v_ref, tc_ref, ta_ref, tb_ref, do_ref, l_ref, dl_ref, dz_in, dz_ref,
             qs, ks, vs, dos, dqs, dks, dvs, nat, oq, ok, ov, sems):
        b, h = pl.program_id(0), pl.program_id(1)
        m_first, m_both = _band_masks()
        _attn_load(q_ref, k_ref, v_ref, tc_ref, ta_ref, tb_ref, qs, ks, vs, d)
        for r in range(d):
            dos[r * L:(r + 1) * L, :] = do_ref[_res_rows(r, L, d), :].astype(BF16)
        dks[...] = jnp.zeros_like(dks)
        dvs[...] = jnp.zeros_like(dvs)
        for r in range(d):
            for hh in range(ATT_HP):
                sl = slice(hh * HEAD, (hh + 1) * HEAD)
                for n in range(nb):
                    rq = slice(r * L + n * HEAD, r * L + (n + 1) * HEAD)
                    rk = slice(r * L + max(n - 1, 0) * HEAD, r * L + (n + 1) * HEAD)
                    rows = _res_rows(r, HEAD, d, n * HEAD)
                    qb, kk, vv, dob = qs[rq, sl], ks[rk, sl], vs[rk, sl], dos[rq, sl]
                    s = lax.dot_general(qb, kk, _NT, preferred_element_type=F32) * _SCALE
                    s = jnp.where(m_first if n == 0 else m_both, s, NEG_INF)
                    p = jnp.exp(s - l_ref[rows, sl][:, 0:1])
                    dp = lax.dot_general(dob, vv, _NT, preferred_element_type=F32)
                    ds = (p * (dp - dl_ref[rows, sl][:, 0:1]) * _SCALE).astype(BF16)
                    dqs[rq, sl] = jnp.dot(ds, kk, preferred_element_type=F32)
                    dks[rk, sl] += lax.dot_general(ds, qb, _TN, preferred_element_type=F32)
                    dvs[rk, sl] += lax.dot_general(p.astype(BF16), dob, _TN, preferred_element_type=F32)
        tc, ta, tb = tc_ref[...], ta_ref[...], tb_ref[...]
        step = b * nh + h

        def writes():
            base = g * GROUP_W + h * W
            return [pltpu.make_async_copy(src, dz_ref.at[b, :, pl.ds(pl.multiple_of(base + off, HEAD), W)],
                                          sems.at[i])
                    for i, (src, off) in enumerate(((oq, 0), (ok, OFF_K), (ov, OFF_V)))]

        @pl.when(step > 0)
        def _():
            for cp in writes():
                cp.wait()

        for src, dst, rotate in ((dqs, oq, True), (dks, ok, True), (dvs, ov, False)):
            for r in range(d):
                nat[_res_rows(r, L, d), :] = src[r * L:(r + 1) * L, :]
            for hh in range(ATT_HP):
                sl = slice(hh * HEAD, (hh + 1) * HEAD)
                val = nat[:, sl]
                dst[:, sl] = (_rot_t(val, tc, ta, tb) if rotate else val).astype(BF16)
        for cp in writes():
            cp.start()

        @pl.when(step == Bn * nh - 1)
        def _():
            for cp in writes():
                cp.wait()

    def zcol(off):
        return pl.BlockSpec((None, SEQ, W), lambda b, h: (b, 0, (off + g * GROUP_W) // W + h))

    tab = pl.BlockSpec((SEQ, HEAD), lambda b, h: (0, 0))
    gcol = pl.BlockSpec((None, SEQ, W), lambda b, h: (b, 0, h))
    any_spec = pl.BlockSpec(memory_space=pl.ANY)
    return _call(
        body, name=name, grid=(Bn, nh),
        in_specs=[zcol(0), zcol(OFF_K), zcol(OFF_V), tab, tab, tab, gcol, gcol, gcol, any_spec],
        out_specs=any_spec,
        out_shape=_sds((Bn, SEQ, ZW), BF16),
        scratch_shapes=[pltpu.VMEM((SEQ, W), BF16)] * 4 + [pltpu.VMEM((SEQ, W), F32)] * 4
        + [pltpu.VMEM((SEQ, W), BF16)] * 3 + [pltpu.SemaphoreType.DMA((3,))],
        aliases={9: 0}, comm=comm,
    )(z3, z3, z3, *tabs, do3, lse3, delta3, dz3)


def _merge_weights(l0, l1, l2):
    m = jnp.maximum(jnp.maximum(l0, l1), l2)
    e0, e1, e2 = jnp.exp(l0 - m), jnp.exp(l1 - m), jnp.exp(l2 - m)
    inv = 1.0 / (e0 + e1 + e2)
    return e0 * inv, e1 * inv, e2 * inv


def _merge_fwd(name, outs, lses, tr=512):
    T = outs[0].shape[0]

    def body(o0, o1, o2, l0, l1, l2, a_ref):
        w0, w1, w2 = _merge_weights(l0[...], l1[...], l2[...])
        a_ref[...] = (w0 * o0[...] + w1 * o1[...] + w2 * o2[...]).astype(BF16)

    row = pl.BlockSpec((tr, GROUP_W), lambda i: (i, 0))
    return pl.pallas_call(
        body, name=name, grid=(T // tr,), in_specs=[row] * 6, out_specs=row,
        out_shape=_sds((T, GROUP_W), BF16), compiler_params=_params(1),
    )(*outs, *lses)


def _merge_bwd(name, outs, lses, dattn, tr=512):
    T = outs[0].shape[0]

    def body(o0, o1, o2, l0, l1, l2, da_ref, d0, d1, d2, e0, e1, e2):
        w = _merge_weights(l0[...], l1[...], l2[...])
        da = da_ref[...]
        attn = w[0] * o0[...] + w[1] * o1[...] + w[2] * o2[...]
        prod = da * attn
        csum = jnp.concatenate(
            [jnp.broadcast_to(jnp.sum(prod[:, hh * HEAD:(hh + 1) * HEAD], axis=-1, keepdims=True), (tr, HEAD))
             for hh in range(GROUP_W // HEAD)], axis=1)
        for wg, d_ref, e_ref in zip(w, (d0, d1, d2), (e0, e1, e2)):
            d_ref[...] = wg * da
            e_ref[...] = wg * csum

    row = pl.BlockSpec((tr, GROUP_W), lambda i: (i, 0))
    res = pl.pallas_call(
        body, name=name, grid=(T // tr,), in_specs=[row] * 7, out_specs=[row] * 6,
        out_shape=[_sds((T, GROUP_W), F32)] * 6,
        compiler_params=_params(1),
    )(*outs, *lses, dattn)
    return res[:3], res[3:]


def _rms_rows(xv, g):
    r = lax.rsqrt(jnp.mean(xv * xv, axis=-1, keepdims=True) + RMS_EPS)
    return (xv * r * g).astype(BF16)


def _epi_norm(val, x_refs, o_ref, y_refs, first_rows):
    o_ref[...] = val
    y_refs[0][...] = _rms_rows(val, x_refs[0][...])


def _epi_ple(val, x_refs, o_ref, y_refs, first_rows):
    o_ref[...] = val
    xn = x_refs[0][...] + x_refs[1][...] * _sigmoid(val)
    y_refs[0][...] = xn
    if len(y_refs) > 1:
        y_refs[1][...] = _rms_rows(xn, x_refs[2][...])


def _epi_norm_bwd(val, x_refs, o_ref, y_refs, first_rows):
    xv = x_refs[0][...]
    r = lax.rsqrt(jnp.mean(xv * xv, axis=-1, keepdims=True) + RMS_EPS)
    xh = xv * r
    part = _rows8(val * xh)

    @pl.when(first_rows)
    def _():
        y_refs[0][...] = part

    @pl.when(jnp.logical_not(first_rows))
    def _():
        y_refs[0][...] += part

    dxh = val * x_refs[2][...]
    o_ref[...] = x_refs[1][...] + r * (dxh - xh * jnp.mean(dxh * xh, axis=-1, keepdims=True))


def _local_step(x3, p4, tgt3, vecs, ex):
    Bn = x3.shape[0]
    T = Bn * SEQ
    x = x3.reshape(T, D)
    tgt = tgt3.reshape(T, D)
    pb = p4.astype(BF16).reshape(DEPTH, T, PLE)
    tabs = _rope_tables()
    tm = 1024 if T % 1024 == 0 else 512
    nt = T // tm
    tk = 1024 if T % 1024 == 0 else 512
    ntk = T // tk
    tm5 = 512
    f32o = lambda n: _sds((T, n), F32)

    def spec(shape, fn):
        return pl.BlockSpec(shape, fn)

    def _mm(name, *args, **kwargs):
        return _mm_call(name, *args, comm=ex.hook(name), **kwargs)

    def cols4(a_ref, b_ref):
        av = a_ref[...]
        return jnp.concatenate([_dot(av, b_ref[j]) for j in range(NCHIP)], axis=1)

    def rows4(a_ref, b_ref):
        av = a_ref[...]
        return jnp.concatenate([_dot(av, b_ref[:, j * 256:(j + 1) * 256], ta=True) for j in range(NCHIP)], axis=0)

    def kchunks4(a_ref, b_ref):
        total = _dot(a_ref[:, 0:256], b_ref[0], tb=True)
        for j in range(1, NCHIP):
            total = total + _dot(a_ref[:, j * 256:(j + 1) * 256], b_ref[j], tb=True)
        return total

    row5 = spec((tm5, D), lambda i, *_: (i, 0))
    gain = spec((1, D), lambda *_: (0, 0))
    bf_rows = (_sds((T, D), BF16), row5)

    saved = []
    h = _rms_fwd("rms_mix0", x, vecs["g_mix"][0:1])
    for l in range(DEPTH):
        L = str(l)
        G = ex.weights(l)
        g_mix, g_ffn, g_ple = (vecs[k][l:l + 1] for k in ("g_mix", "g_ffn", "g_ple"))
        pscale, cb, cw = vecs["pool_scale"][l:l + 1], vecs["conv_b"][l:l + 1], vecs["conv_w"][l]
        z = _mm("mm_z" + L, h, G["in"], grid=(nt, NCHIP),
                a_spec=spec((tm, D), lambda i, n: (i, 0)),
                b_spec=spec((None, D, IN_S), lambda i, n: (n, 0, 0)),
                o_spec=spec((tm, IN_S), lambda i, n: (i, n)), out_shape=f32o(ZW))
        z3 = z.reshape(Bn, SEQ, ZW)
        outs, lses = [], []
        for g, d in enumerate(DILATIONS):
            o_g, l_g = _attn_fwd("attn_fwd%d_%d" % (g, l), z3, tabs, g, d, comm=ex.hook("attn_fwd%d_%d" % (g, l)))
            outs.append(o_g.reshape(T, GROUP_W))
            lses.append(l_g.reshape(T, GROUP_W))
        attn = _merge_fwd("merge_fwd" + L, outs, lses)
        ya = _mm("mm_ya" + L, attn, G["sc"], grid=(nt,), compute=cols4,
                 a_spec=spec((tm, GROUP_W), lambda i: (i, 0)),
                 b_spec=spec((NCHIP, GROUP_W, 256), lambda i: (0, 0, 0)),
                 o_spec=spec((tm, D), lambda i: (i, 0)), out_shape=f32o(D))
        pooled3, ms3 = _pool_fwd("pool_fwd" + L, z3, G["sc"], pscale)
        ms = ms3.reshape(T, D)

        def row_sharded(name, a, rb, res=None, kdim=D, **fused):
            if rb is None:
                b_arr, b_spec = G["dn"], spec((NCHIP, DN_S, D), lambda i: (0, 0, 0))
            else:
                b_arr, b_spec = G["r3"], spec((NCHIP, 256, D), lambda i: (0, rb, 0))
            return _mm(name, a, b_arr, grid=(T // tm5,),
                       a_spec=spec((tm5, kdim), lambda i: (i, 0)), b_spec=b_spec,
                       o_spec=row5, out_shape=f32o(D), res=res, res_spec=None if res is None else row5, **fused)

        yb = row_sharded("mm_yb" + L, ms, 0)
        merged = _gate_fwd("gate_fwd" + L, z, ya, yb)
        x1, h2 = row_sharded("mm_o" + L, merged, 1, res=x, epilogue=_epi_norm, extra_in=[(g_ffn, gain)],
                             extra_out=[bf_rows])
        u = _mm("mm_up" + L, h2, G["up"], grid=(nt, NCHIP),
                a_spec=spec((tm, D), lambda i, n: (i, 0)),
                b_spec=spec((None, D, UP_S), lambda i, n: (n, 0, 0)),
                o_spec=spec((tm, UP_S), lambda i, n: (i, n)), out_shape=f32o(UW))
        u3 = u.reshape(Bn, SEQ, UW)
        act = _conv_fwd("conv_fwd" + L, u3, cw, cb, comm=ex.hook("conv_fwd" + L)).reshape(T, FF)
        x2, h3 = row_sharded("mm_down" + L, act, None, res=x1, kdim=FF, epilogue=_epi_norm,
                             extra_in=[(g_ple, gain)], extra_out=[bf_rows])
        pe = _mm("mm_pe" + L, pb[l], G["sc"], grid=(nt,), compute=cols4,
                 a_spec=spec((tm, PLE), lambda i: (i, 0)),
                 b_spec=spec((NCHIP, 256, 256), lambda i: (0, 2, 0)),
                 o_spec=spec((tm, D), lambda i: (i, 0)), out_shape=f32o(D))
        fused_in = [(x2, row5), (pe, row5)]
        fused_out = [(f32o(D), row5)]
        if l + 1 < DEPTH:
            fused_in.append((vecs["g_mix"][l + 1:l + 2], gain))
            fused_out.append(bf_rows)
        pg, x3n, *h_next = row_sharded("mm_pg" + L, h3, 2, epilogue=_epi_ple, extra_in=fused_in,
                                       extra_out=fused_out)
        saved.append(dict(x=x, h=h, z=z, outs=outs, lses=lses, attn=attn, ya=ya, yb=yb, pooled3=pooled3, ms=ms,
                          merged=merged, x1=x1, h2=h2, u3=u3, act=act, x2=x2, h3=h3, pg=pg, pe=pe))
        x = x3n
        h = h_next[0] if h_next else None

    dx, dg_final8, sq8 = _final_loss(x, vecs["g_final"].reshape(1, D), tgt)

    gg_shape = {k: _sds(G[k].shape, F32) for k in G}
    small = {"g_final": dg_final8}

    for l in reversed(range(DEPTH)):
        L = str(l)
        sv = saved[l]
        G = ex.weights(l)
        GG = dict.fromkeys(_KEYS)
        g_mix, g_ffn, g_ple = (vecs[k][l:l + 1] for k in ("g_mix", "g_ffn", "g_ple"))
        pscale, cb, cw = vecs["pool_scale"][l:l + 1], vecs["conv_b"][l:l + 1], vecs["conv_w"][l]

        def wgrad_rows(name, a, b_arr, key, rb):
            GG[key] = _mm(name, a, b_arr, grid=(2, ntk), ta=True, k_axis=1, nk=ntk, acc_shape=(D, 512),
                          a_spec=spec((tk, D), lambda n, k: (k, 0)),
                          b_spec=spec((tk, 512), lambda n, k: (k, n)),
                          o_spec=spec((NCHIP, 256, 512), lambda n, k: (0, rb, n)),
                          out_shape=gg_shape[key], buf=GG[key])

        def dgrad_rows(name, dy, rb, **fused):
            return _mm(name, dy, G["r3"], grid=(T // tm5,), tb=True,
                       a_spec=spec((tm5, D), lambda i: (i, 0)),
                       b_spec=spec((NCHIP, 256, D), lambda i: (0, rb, 0)),
                       o_spec=row5, out_shape=f32o(D), **fused)

        def norm_bwd(xin, dres, g):
            return dict(epilogue=_epi_norm_bwd, extra_in=[(xin, row5), (dres, row5), (g, gain)],
                        extra_out=[(_sds((8, D), F32), spec((8, D), lambda *_: (0, 0)))])

        dpe, dpg = _ple_bwd("ple_bwd" + L, dx, sv["pe"], sv["pg"])
        GG["sc"] = _mm("wg_ple" + L, pb[l], dpe, grid=(ntk,), compute=rows4, k_axis=0, nk=ntk,
                       acc_shape=(NCHIP * PLE, 256),
                       a_spec=spec((tk, PLE), lambda k: (k, 0)), b_spec=spec((tk, D), lambda k: (k, 0)),
                       o_spec=spec((NCHIP, 256, 256), lambda k: (0, 2, 0)),
                       out_shape=gg_shape["sc"], buf=GG["sc"])
        wgrad_rows("wg_pg" + L, sv["h3"], dpg, "r3", 2)
        dx, small["g_ple" + L] = dgrad_rows("dg_pg" + L, dpg, 2, **norm_bwd(sv["x2"], dx, g_ple))

        da = _mm("dg_down" + L, dx, G["dn"], grid=(T // 256,), tb=True,
                 a_spec=spec((256, D), lambda i: (i, 0)),
                 b_spec=spec((NCHIP, DN_S, D), lambda i: (0, 0, 0)),
                 o_spec=spec((256, FF), lambda i: (i, 0)), out_shape=f32o(FF))
        GG["dn"] = _mm("wg_down" + L, sv["act"], dx, grid=(2, ntk), ta=True, k_axis=1, nk=ntk,
                       acc_shape=(FF, 512),
                       a_spec=spec((tk, FF), lambda n, k: (k, 0)), b_spec=spec((tk, 512), lambda n, k: (k, n)),
                       o_spec=spec((NCHIP, DN_S, 512), lambda n, k: (0, 0, n)),
                       out_shape=gg_shape["dn"], buf=GG["dn"])
        du3, dcw, dcb = _conv_bwd("conv_bwd" + L, da.reshape(Bn, SEQ, FF), sv["u3"], cw, cb,
                                  comm=ex.hook("conv_bwd" + L))
        small["conv_w" + L], small["conv_b" + L] = dcw, dcb
        du = du3.reshape(T, UW)
        dx, small["g_ffn" + L] = _mm(
            "dg_up" + L, du, G["up"], grid=(T // tm5, NCHIP), tb=True, k_axis=1, nk=NCHIP, acc_shape=(tm5, D),
            a_spec=spec((tm5, UP_S), lambda i, k: (i, k)), b_spec=spec((None, D, UP_S), lambda i, k: (k, 0, 0)),
            o_spec=spec((tm5, D), lambda i, k: (i, 0)), out_shape=f32o(D), **norm_bwd(sv["x1"], dx, g_ffn))
        GG["up"] = _mm("wg_up" + L, sv["h2"], du, grid=(NCHIP, ntk), ta=True, k_axis=1, nk=ntk,
                       acc_shape=(D, UP_S),
                       a_spec=spec((tk, D), lambda j, k: (k, 0)),
                       b_spec=spec((tk, UP_S), lambda j, k: (k, j)),
                       o_spec=spec((None, D, UP_S), lambda j, k: (j, 0, 0)),
                       out_shape=gg_shape["up"], buf=GG["up"])

        dmerged = dgrad_rows("dg_o" + L, dx, 1)
        wgrad_rows("wg_o" + L, sv["merged"], dx, "r3", 1)
        dya, dz = _gate_bwd("gate_bwd_a" + L, sv["z"], OFF_GA, sv["ya"], dmerged, None)
        dyb, dz = _gate_bwd("gate_bwd_b" + L, sv["z"], OFF_GB, sv["yb"], dmerged, dz)
        dms = dgrad_rows("dg_yb" + L, dyb, 0)
        wgrad_rows("wg_yb" + L, sv["ms"], dyb, "r3", 0)
        dz3, GG["sc"], small["pool_scale" + L] = _pool_bwd(
            "pool_bwd" + L, dms.reshape(Bn, SEQ, D), sv["pooled3"], G["sc"], pscale,
            dz.reshape(Bn, SEQ, ZW), GG["sc"])
        dattn = _mm("dg_ya" + L, dya, G["sc"], grid=(nt,), compute=kchunks4,
                    a_spec=spec((tm, D), lambda i: (i, 0)),
                    b_spec=spec((NCHIP, GROUP_W, 256), lambda i: (0, 0, 0)),
                    o_spec=spec((tm, GROUP_W), lambda i: (i, 0)), out_shape=f32o(GROUP_W))
        GG["sc"] = _mm("wg_ya" + L, sv["attn"], dya, grid=(ntk,), compute=rows4, k_axis=0, nk=ntk,
                       acc_shape=(NCHIP * GROUP_W, 256),
                       a_spec=spec((tk, GROUP_W), lambda k: (k, 0)), b_spec=spec((tk, D), lambda k: (k, 0)),
                       o_spec=spec((NCHIP, GROUP_W, 256), lambda k: (0, 0, 0)),
                       out_shape=gg_shape["sc"], buf=GG["sc"])
        ex.grads_ready(l, {k: GG[k] for k in _KEYS[1:]})
        dos, deltas = _merge_bwd("merge_bwd" + L, sv["outs"], sv["lses"], dattn)
        view3 = lambda t: t.reshape(Bn, SEQ, GROUP_W)
        sz3 = sv["z"].reshape(Bn, SEQ, ZW)
        for g, d in enumerate(DILATIONS):
            dz3 = _attn_bwd("attn_bwd%d_%d" % (g, l), sz3, tabs, g, d, view3(dos[g]), view3(sv["lses"][g]),
                            view3(deltas[g]), dz3, comm=ex.hook("attn_bwd%d_%d" % (g, l)))
        dz = dz3.reshape(T, ZW)
        dx_in = dx
        dx, small["g_mix" + L] = _mm(
            "dg_z" + L, dz, G["in"], grid=(T // tm5, NCHIP), tb=True, k_axis=1, nk=NCHIP, acc_shape=(tm5, D),
            a_spec=spec((tm5, IN_S), lambda i, k: (i, k)), b_spec=spec((None, D, IN_S), lambda i, k: (k, 0, 0)),
            o_spec=spec((tm5, D), lambda i, k: (i, 0)), out_shape=f32o(D), **norm_bwd(sv["x"], dx_in, g_mix))
        GG["in"] = _mm("wg_z" + L, sv["h"], dz, grid=(NCHIP, ntk), ta=True, k_axis=1, nk=ntk,
                       acc_shape=(D, IN_S),
                       a_spec=spec((tk, D), lambda n, k: (k, 0)), b_spec=spec((tk, IN_S), lambda n, k: (k, n)),
                       o_spec=spec((None, D, IN_S), lambda n, k: (n, 0, 0)),
                       out_shape=gg_shape["in"], buf=GG["in"])
        ex.grads_ready(l, {"in": GG["in"]})

    return sq8, dx.reshape(Bn, SEQ, D), small


_ANY = pl.BlockSpec(memory_space=pl.ANY)


def _place():
    x, y, c = lax.axis_index("x"), lax.axis_index("y"), lax.axis_index("c")
    chips = [(1 - x, y), (x, 1 - y), (1 - x, 1 - y)]
    return x, y, c, 2 * x + y, chips


def _half(rows, cc):
    return pl.ds(cc * (rows // 2), rows // 2)


def _remote(src, dst, send_sems, recv_sems, i, to):
    return pltpu.make_async_remote_copy(src_ref=src, dst_ref=dst, send_sem=send_sems.at[i], recv_sem=recv_sems.at[i],
                                        device_id=to, device_id_type=MESH)


def _exchange_gather_ici(stacks, done):
    n = len(stacks)
    rows = [t.shape[1] for t in stacks]

    def start(refs, send_sems, recv_sems):
        x, y, c, me, chips = _place()
        for k in range(n):
            part = refs[k].at[me, _half(rows[k], c)]
            for j, chip in enumerate(chips):
                _remote(part, part, send_sems, recv_sems, 3 * k + j, (*chip, c)).start()

    def wait(refs, send_sems, recv_sems):
        x, y, c, me, chips = _place()
        for k in range(n):
            for j, chip in enumerate(chips):
                part = refs[k].at[2 * chip[0] + chip[1], _half(rows[k], c)]
                _remote(part, part, send_sems, recv_sems, 3 * k + j, (*chip, c)).wait()

    return dict(arrays=list(stacks), nsem=3 * n, start=start, wait=wait, done=done)


def _exchange_gather_d2d(stacks, done):
    n = len(stacks)
    rows = [t.shape[1] for t in stacks]

    def copies(refs, send_sems, recv_sems, mine):
        x, y, c, me, chips = _place()
        cc = c if mine else 1 - c
        return [_remote(part, part, send_sems, recv_sems, 3 * k + j, (x, y, 1 - c))
                for k in range(n) for j, chip in enumerate(chips)
                for part in [refs[k].at[2 * chip[0] + chip[1], _half(rows[k], cc)]]]

    def start(refs, send_sems, recv_sems):
        for cp in copies(refs, send_sems, recv_sems, True):
            cp.start()

    def wait(refs, send_sems, recv_sems):
        for cp in copies(refs, send_sems, recv_sems, False):
            cp.wait()

    return dict(arrays=list(stacks), nsem=3 * n, start=start, wait=wait, done=done)


def _exchange_halves(g5, recv, done):
    n = len(g5)

    def copies(refs, send_sems, recv_sems):
        x, y, c, me, chips = _place()
        return [_remote(refs[k].at[:, 1 - c], refs[n + k], send_sems, recv_sems, k, (x, y, 1 - c)) for k in range(n)]

    def start(refs, send_sems, recv_sems):
        for cp in copies(refs, send_sems, recv_sems):
            cp.start()

    def wait(refs, send_sems, recv_sems):
        for cp in copies(refs, send_sems, recv_sems):
            cp.wait()

    return dict(arrays=list(g5) + list(recv), nsem=n, start=start, wait=wait, done=done)


def _exchange_chips(parts, landing, done):
    n = len(parts)

    def start(refs, send_sems, recv_sems):
        x, y, c, me, chips = _place()
        for k in range(n):
            for j, chip in enumerate(chips):
                _remote(refs[k].at[2 * chip[0] + chip[1]], refs[n + k].at[me], send_sems, recv_sems, 3 * k + j,
                        (*chip, c)).start()

    def wait(refs, send_sems, recv_sems):
        x, y, c, me, chips = _place()
        for k in range(n):
            for j, chip in enumerate(chips):
                slot = refs[n + k].at[2 * chip[0] + chip[1]]
                _remote(slot, slot, send_sems, recv_sems, 3 * k + j, (*chip, c)).wait()

    return dict(arrays=list(parts) + list(landing), nsem=3 * n, start=start, wait=wait, done=done)


def _exchange_share(full, layer, done):
    n = len(full)

    def copies(refs, send_sems, recv_sems, mine):
        x, y, c, me, chips = _place()
        cc = c if mine else 1 - c
        return [_remote(part, part, send_sems, recv_sems, k, (x, y, 1 - c))
                for k in range(n) for part in [refs[k].at[layer, cc]]]

    def start(refs, send_sems, recv_sems):
        for cp in copies(refs, send_sems, recv_sems, True):
            cp.start()

    def wait(refs, send_sems, recv_sems):
        for cp in copies(refs, send_sems, recv_sems, False):
            cp.wait()

    return dict(arrays=list(full), nsem=n, start=start, wait=wait, done=done)


def _exchange_call(name, comm):
    arrays = comm["arrays"]
    n = len(arrays)

    def body(*refs):
        outs, send_sems, recv_sems = refs[n:2 * n], refs[2 * n], refs[2 * n + 1]
        comm["start"](outs, send_sems, recv_sems)
        comm["wait"](outs, send_sems, recv_sems)

    outs = pl.pallas_call(
        body, name=name, in_specs=[_ANY] * n, out_specs=[_ANY] * n,
        out_shape=[_sds(t.shape, t.dtype) for t in arrays],
        scratch_shapes=[pltpu.SemaphoreType.DMA((comm["nsem"],))] * 2,
        input_output_aliases={i: i for i in range(n)},
    )(*arrays)
    comm["done"](outs)


def _gather_first(stacks, cw4):
    n = len(stacks)
    ici = _exchange_gather_ici(stacks, None)
    d2d = _exchange_gather_d2d(stacks, None)
    rows = [t.shape[1] for t in stacks]

    def body(*refs):
        g_refs, cwg_ref = refs[n + 1:2 * n + 1], refs[2 * n + 1]
        s_ici, r_ici, s_d2d, r_d2d, s_cw, r_cw = refs[2 * n + 2:]
        x, y, c, me, chips = _place()

        def cw_copy(j, slot, chip):
            part = cwg_ref.at[slot]
            return _remote(part, part, s_cw, r_cw, j, (*chip, c))

        ici["start"](g_refs, s_ici, r_ici)
        for j, chip in enumerate(chips):
            cw_copy(j, me, chip).start()
        for k in range(n):
            for j, chip in enumerate(chips):
                part = g_refs[k].at[2 * chip[0] + chip[1], _half(rows[k], c)]
                _remote(part, part, s_ici, r_ici, 3 * k + j, (*chip, c)).wait()
                _remote(part, part, s_d2d, r_d2d, 3 * k + j, (x, y, 1 - c)).start()
        d2d["wait"](g_refs, s_d2d, r_d2d)
        for j, chip in enumerate(chips):
            cw_copy(j, 2 * chip[0] + chip[1], chip).wait()

    outs = pl.pallas_call(
        body, name="gather_first", in_specs=[_ANY] * (n + 1), out_specs=[_ANY] * (n + 1),
        out_shape=[_sds(t.shape, t.dtype) for t in stacks] + [_sds(cw4.shape, cw4.dtype)],
        scratch_shapes=[pltpu.SemaphoreType.DMA((3 * n,))] * 4 + [pltpu.SemaphoreType.DMA((3,))] * 2,
        input_output_aliases={i: i for i in range(n + 1)},
    )(*stacks, cw4)
    return outs[:n], outs[n]


def _halves_and_small(g5, recv, small):
    n = len(g5)
    halves = _exchange_halves(g5, recv, None)

    def body(*refs):
        small_ref = refs[2 * n]
        c_refs, red_ref = refs[2 * n + 1:4 * n + 1], refs[4 * n + 1]
        gath, send_sems, recv_sems, s_send, s_recv = refs[4 * n + 2:]
        x, y, c, me, chips = _place()
        dev = 4 * x + 2 * y + c
        gath[dev] = small_ref[...]
        halves["start"](c_refs, send_sems, recv_sems)
        for r in range(1, 8):
            peer = (x ^ (r >> 2), y ^ ((r >> 1) & 1), c ^ (r & 1))
            _remote(small_ref, gath.at[dev], s_send, s_recv, r - 1, peer).start()
        for r in range(1, 8):
            peer = (x ^ (r >> 2), y ^ ((r >> 1) & 1), c ^ (r & 1))
            src = 4 * peer[0] + 2 * peer[1] + peer[2]
            _remote(small_ref, gath.at[src], s_send, s_recv, r - 1, peer).wait()
        total = gath[0]
        for i in range(1, 8):
            total = total + gath[i]
        red_ref[...] = total
        halves["wait"](c_refs, send_sems, recv_sems)

    vm = pl.BlockSpec(memory_space=pltpu.VMEM)
    arrays = list(g5) + list(recv)
    outs = pl.pallas_call(
        body, name="halves_and_small", in_specs=[_ANY] * (2 * n) + [vm], out_specs=[_ANY] * (2 * n) + [vm],
        out_shape=[_sds(t.shape, t.dtype) for t in arrays] + [_sds(small.shape, F32)],
        scratch_shapes=[pltpu.VMEM((8,) + small.shape, F32), pltpu.SemaphoreType.DMA((n,)),
                        pltpu.SemaphoreType.DMA((n,)), pltpu.SemaphoreType.DMA((7,)), pltpu.SemaphoreType.DMA((7,))],
        input_output_aliases={i: i for i in range(2 * n)},
    )(*arrays, small)
    return outs[:n], outs[n:2 * n], outs[2 * n]


def _row_tile(rh):
    for cand in (512, 384, 352, 256, 128):
        if rh % cand == 0:
            return cand
    return rh


def _add_halves(name, g5, recv, place):
    _, _, rh, cols = g5.shape
    tr = _row_tile(rh)

    def body(place_ref, g_ref, r_ref, o_ref, own_ref):
        val = (g_ref[...] + r_ref[...]).astype(BF16)
        o_ref[...] = val

        @pl.when(pl.program_id(1) == place_ref[1])
        def _():
            own_ref[...] = val

    grid_spec = pltpu.PrefetchScalarGridSpec(
        num_scalar_prefetch=1, grid=(rh // tr, NCHIP),
        in_specs=[pl.BlockSpec((None, None, tr, cols), lambda i, j, pr: (j, pr[0], i, 0)),
                  pl.BlockSpec((None, tr, cols), lambda i, j, pr: (j, i, 0))],
        out_specs=[pl.BlockSpec((None, tr, cols), lambda i, j, pr: (j, i, 0)),
                   pl.BlockSpec((None, tr, cols), lambda i, j, pr: (pr[1], i, 0))])
    return pl.pallas_call(
        body, name=name, grid_spec=grid_spec, out_shape=[_sds(recv.shape, BF16)] * 2, compiler_params=_params(2),
    )(place, g5, recv)


def _sum_chips(name, landing, place, layer, full):
    _, rh, cols = landing.shape
    tr = _row_tile(rh)
    has_full = full is not None

    def body(*refs):
        r_ref, o_ref = refs[1], refs[-1]
        total = r_ref[0].astype(F32)
        for j in range(1, NCHIP):
            total = total + r_ref[j].astype(F32)
        o_ref[...] = total

    grid_spec = pltpu.PrefetchScalarGridSpec(
        num_scalar_prefetch=1, grid=(rh // tr,),
        in_specs=[pl.BlockSpec((NCHIP, tr, cols), lambda i, pr: (0, i, 0))] + ([_ANY] if has_full else []),
        out_specs=pl.BlockSpec((None, None, tr, cols), lambda i, pr: (layer, pr[0], i, 0)))
    return pl.pallas_call(
        body, name=name, grid_spec=grid_spec, out_shape=_sds((DEPTH, 2, rh, cols), F32),
        input_output_aliases={2: 0} if has_full else {}, compiler_params=_params(1),
    )(place, landing, *([full] if has_full else []))


class _Schedule:
    FIRST, REST = ["in"], list(_KEYS[1:])

    def __init__(self, slotted, cw4, place):
        self.place = place
        self._w = [dict(zip(_KEYS, layer)) for layer in slotted]
        got, self.cw4 = _gather_first([self._w[0][k] for k in self.FIRST], cw4)
        self._w[0].update(zip(self.FIRST, got))
        self._g5, self._recv, self._parts, self._landing = [{}, {}], [{}, {}], [{}, {}], [{}, {}]
        self.full = {}
        every = list(_KEYS)
        self._hooks = {
            "mm_z0": lambda: self._gather(_exchange_gather_ici, 0, self.REST),
            "attn_fwd0_0": lambda: self._gather(_exchange_gather_d2d, 0, self.REST),
            "attn_fwd1_0": lambda: self._gather(_exchange_gather_ici, 1, self.FIRST),
            "mm_up0": lambda: self._gather(_exchange_gather_ici, 1, self.REST),
            "conv_fwd0": lambda: self._gather(_exchange_gather_d2d, 1, every),
            "dg_down0": lambda: self._halves(1, every),
            "conv_bwd0": lambda: self._chips(1, every),
            "dg_up0": lambda: self._share(1, every),
            "attn_bwd0_0": lambda: self._halves(0, self.REST),
            "dg_z0": lambda: self._chips(0, self.REST),
            "wg_z0": lambda: self._share(0, self.REST),
        }

    def weights(self, layer):
        return self._w[layer]

    def hook(self, name):
        make = self._hooks.get(name)
        return make() if make else None

    def grads_ready(self, layer, GG):
        for k, t in GG.items():
            g5 = t.reshape(NCHIP, 2, t.shape[1] // 2, t.shape[2])
            self._g5[layer][k] = g5
            self._recv[layer][k] = lax.empty((NCHIP,) + g5.shape[2:], F32)

    def _gather(self, make, layer, keys):
        def done(arrays):
            self._w[layer].update(zip(keys, arrays))
        return make([self._w[layer][k] for k in keys], done)

    def _halves(self, layer, keys):
        return _exchange_halves([self._g5[layer][k] for k in keys], [self._recv[layer][k] for k in keys],
                                lambda arrays: self._halves_done(layer, keys, arrays))

    def _halves_done(self, layer, keys, arrays):
        n = len(keys)
        for k, g, r in zip(keys, arrays[:n], arrays[n:]):
            self._parts[layer][k], self._landing[layer][k] = _add_halves(
                "add_halves%d_%s" % (layer, k), g, r, self.place)

    def _chips(self, layer, keys):
        return _exchange_chips([self._parts[layer][k] for k in keys], [self._landing[layer][k] for k in keys],
                               lambda arrays: self._chips_done(layer, keys, arrays))

    def _chips_done(self, layer, keys, arrays):
        for k, t in zip(keys, arrays[len(keys):]):
            self.full[k] = _sum_chips("sum_chips%d_%s" % (layer, k), t, self.place, layer, self.full.get(k))

    def _share(self, layer, keys):
        def done(arrays):
            self.full.update(zip(keys, arrays))
        return _exchange_share([self.full[k] for k in keys], layer, done)

    def finish(self, small):
        keys = self.FIRST
        g5, recv, small_red = _halves_and_small([self._g5[0][k] for k in keys], [self._recv[0][k] for k in keys], small)
        self._halves_done(0, keys, list(g5) + list(recv))
        _exchange_call("exchange_chips_last", self._chips(0, keys))
        _exchange_call("share_halves_last", self._share(0, keys))
        return [self.full[k] for k in _KEYS], small_red


def _adamw(name, w, g, m, v):
    shape = w.shape
    cols = shape[-1]
    rows = 1
    for s in shape[:-1]:
        rows *= s
    tr = rows
    for cand in (256, 128, 64):
        if rows > cand and rows % cand == 0:
            tr = cand
            break
    c1 = 1.0 / (1.0 - B1 ** STEP)
    c2 = 1.0 / (1.0 - B2 ** STEP)

    def body(w_ref, g_ref, m_ref, v_ref, d_ref, nm_ref, nv_ref):
        gv = g_ref[...]
        nm = B1 * m_ref[...] + (1.0 - B1) * gv
        nv = B2 * v_ref[...] + (1.0 - B2) * (gv * gv)
        nm_ref[...] = nm
        nv_ref[...] = nv
        d_ref[...] = -LR * ((nm * c1) / (jnp.sqrt(nv * c2) + ADAM_EPS) + WD * w_ref[...])

    blk = pl.BlockSpec((tr, cols), lambda i: (i, 0))
    outs = pl.pallas_call(
        body, name=name, grid=(rows // tr,), in_specs=[blk] * 4, out_specs=[blk] * 3,
        out_shape=[_sds((rows, cols), F32)] * 3, compiler_params=_params(1),
    )(*(t.reshape(rows, cols) for t in (w, g, m, v)))
    return tuple(o.reshape(shape) for o in outs)


def _pack_small(small):
    rows = [jnp.sum(small["g_mix%d" % l], axis=0, keepdims=True) for l in range(DEPTH)]
    rows += [jnp.sum(small["pool_scale%d" % l], axis=0, keepdims=True) for l in range(DEPTH)]
    rows += [jnp.sum(small["g_ffn%d" % l], axis=0, keepdims=True) for l in range(DEPTH)]
    rows += [jnp.sum(small["g_ple%d" % l], axis=0, keepdims=True) for l in range(DEPTH)]
    rows += [jnp.sum(small["g_final"], axis=0, keepdims=True)]
    flat = [small["conv_b%d" % l].reshape(-1) for l in range(DEPTH)]
    flat += [small["conv_w%d" % l].reshape(-1) for l in range(DEPTH)]
    flat = jnp.concatenate(flat).reshape(-1, D)
    packed = jnp.concatenate(rows + [flat], axis=0)
    return jnp.pad(packed, ((0, SMALL_ROWS - packed.shape[0]), (0, 0)))


def _unpack_small(red):
    g_mix, pool_scale, g_ffn, g_ple = red[0:2], red[2:4], red[4:6], red[6:8]
    g_final = red[8]
    nb = DEPTH * UW // D
    conv_b = red[9:9 + nb].reshape(DEPTH, UW)
    conv_w = red[9 + nb:9 + 4 * nb].reshape(DEPTH, 3, UW)
    return g_mix, pool_scale, g_ffn, g_ple, g_final, conv_b, conv_w


def kernel(x, p, g_mix, w_in, w_ya, w_yb, pool_w, pool_scale, w_o, g_ffn, w_up, conv_w, conv_b, w_down, g_ple, w_ple, w_ple_gate, g_final, loss_target, m_g_mix, m_w_in, m_w_ya, m_w_yb, m_pool_w, m_pool_scale, m_w_o, m_g_ffn, m_w_up, m_conv_w, m_conv_b, m_w_down, m_g_ple, m_w_ple, m_w_ple_gate, m_g_final, v_g_mix, v_w_in, v_w_ya, v_w_yb, v_pool_w, v_pool_scale, v_w_o, v_g_ffn, v_w_up, v_conv_w, v_conv_b, v_w_down, v_g_ple, v_w_ple, v_w_ple_gate, v_g_final):
    me = 2 * lax.axis_index("x") + lax.axis_index("y")
    place = jnp.stack([lax.axis_index("c"), me]).astype(jnp.int32)

    def slot(shard):
        return lax.dynamic_update_index_in_dim(lax.empty((NCHIP,) + shard.shape, shard.dtype), shard, me, 0)

    packed = [
        w_in.astype(BF16), w_up.astype(BF16),
        jnp.concatenate([w_ya, w_ple, pool_w.reshape(DEPTH, 256, 256)], axis=1).astype(BF16),
        jnp.concatenate([w_yb, w_o, w_ple_gate], axis=1).astype(BF16),
        w_down.astype(BF16),
    ]
    slotted = [[slot(t[l]) for t in packed] for l in range(DEPTH)]
    ex = _Schedule(slotted, slot(conv_w.reshape(DEPTH * 3, UP_S)), place)
    cw_full = ex.cw4.reshape(NCHIP, DEPTH, 3, UP_S).transpose(1, 2, 0, 3).reshape(DEPTH, 3, UW)

    vecs = dict(g_mix=g_mix, pool_scale=pool_scale, g_ffn=g_ffn, g_ple=g_ple, g_final=g_final, conv_b=conv_b,
                conv_w=cw_full)
    sq8, grad_x, small = _local_step(x, p, loss_target, vecs, ex)
    loss = lax.psum(jnp.sum(sq8) * (0.5 / D), ("x", "y", "c"))

    full, small_red = ex.finish(_pack_small(small))
    r_in, r_up, r_sc, r_r3, r_dn = [f.reshape(DEPTH, -1, f.shape[-1]) for f in full]
    d_g_mix, d_pool_scale, d_g_ffn, d_g_ple, d_g_final, d_conv_b, d_conv_w_full = _unpack_small(small_red)
    d_conv_w = lax.dynamic_slice_in_dim(d_conv_w_full, me * UP_S, UP_S, axis=2)

    grads = dict(
        g_mix=d_g_mix, w_in=r_in, w_ya=r_sc[:, 0:512], w_yb=r_r3[:, 0:256],
        pool_w=r_sc[:, 768:1024].reshape(DEPTH, 4, 64, 256), pool_scale=d_pool_scale, w_o=r_r3[:, 256:512],
        g_ffn=d_g_ffn, w_up=r_up, conv_w=d_conv_w, conv_b=d_conv_b, w_down=r_dn, g_ple=d_g_ple,
        w_ple=r_sc[:, 512:768], w_ple_gate=r_r3[:, 512:768], g_final=d_g_final)
    weights = dict(g_mix=g_mix, w_in=w_in, w_ya=w_ya, w_yb=w_yb, pool_w=pool_w, pool_scale=pool_scale, w_o=w_o,
                   g_ffn=g_ffn, w_up=w_up, conv_w=conv_w, conv_b=conv_b, w_down=w_down, g_ple=g_ple, w_ple=w_ple,
                   w_ple_gate=w_ple_gate, g_final=g_final)
    m_in = dict(g_mix=m_g_mix, w_in=m_w_in, w_ya=m_w_ya, w_yb=m_w_yb, pool_w=m_pool_w, pool_scale=m_pool_scale,
                w_o=m_w_o, g_ffn=m_g_ffn, w_up=m_w_up, conv_w=m_conv_w, conv_b=m_conv_b, w_down=m_w_down,
                g_ple=m_g_ple, w_ple=m_w_ple, w_ple_gate=m_w_ple_gate, g_final=m_g_final)
    v_in = dict(g_mix=v_g_mix, w_in=v_w_in, w_ya=v_w_ya, w_yb=v_w_yb, pool_w=v_pool_w, pool_scale=v_pool_scale,
                w_o=v_w_o, g_ffn=v_g_ffn, w_up=v_w_up, conv_w=v_conv_w, conv_b=v_conv_b, w_down=v_w_down,
                g_ple=v_g_ple, w_ple=v_w_ple, w_ple_gate=v_w_ple_gate, g_final=v_g_final)
    names = ["g_mix", "w_in", "w_ya", "w_yb", "pool_w", "pool_scale", "w_o", "g_ffn", "w_up", "conv_w", "conv_b",
             "w_down", "g_ple", "w_ple", "w_ple_gate", "g_final"]
    deltas, new_m, new_v = [], [], []
    for nme in names:
        gr = grads[nme].reshape(weights[nme].shape)
        grads[nme] = gr
        dlt, nm, nv = _adamw("adamw_" + nme, weights[nme], gr, m_in[nme], v_in[nme])
        deltas.append(dlt)
        new_m.append(nm)
        new_v.append(nv)
    return (loss, grad_x, *[grads[nme] for nme in names], *deltas, *new_m, *new_v)
```

```python
import math

import jax
import jax.numpy as jnp
from jax import lax
from jax.experimental import pallas as pl
from jax.experimental.pallas import tpu as pltpu

F32 = jnp.float32
BF16 = jnp.bfloat16
_KEYS = ("in", "up", "sc", "r3", "dn")
MESH = pl.DeviceIdType.MESH

D = 1024
SEQ = 2048
DEPTH = 2
HEAD = 128
GROUP_W = 512
DILATIONS = (1, 4, 16)
ROPE_DIM = 32
ROPE_THETA = 500000.0
NEG_INF = -1e30
ZW = 7680
OFF_K, OFF_V, OFF_U, OFF_GA, OFF_GB = 1536, 3072, 4608, 5632, 6656
FF = 2816
UW = 2 * FF
PLE = 256
NCHIP = 4
IN_S, UP_S, DN_S = ZW // NCHIP, UW // NCHIP, FF // NCHIP
RMS_EPS = 1e-6
LR, B1, B2, ADAM_EPS, WD, STEP = 0.001, 0.9, 0.999, 1e-08, 0.01, 10
SMALL_ROWS = 56
VMEM_CAP = 48 * 1024 * 1024


def _params(n_grid, vmem=VMEM_CAP):
    return pltpu.CompilerParams(dimension_semantics=("arbitrary",) * n_grid, vmem_limit_bytes=vmem)


def _sigmoid(v):
    return 1.0 / (1.0 + jnp.exp(-v))


def _rows8(v):
    return jnp.sum(v.reshape(v.shape[0] // 8, 8, v.shape[1]), axis=0)


def _sds(shape, dtype):
    return jax.ShapeDtypeStruct(shape, dtype)


def _dot(av, bv, ta=False, tb=False):
    dims = (((0,) if ta else (1,), (1,) if tb else (0,)), ((), ()))
    return lax.dot_general(av.astype(BF16), bv.astype(BF16), dims, preferred_element_type=F32)


def _call(body, *, name, grid, in_specs, out_specs, out_shape, scratch_shapes=(), aliases=None, comm=None):
    params = _params(len(grid))
    aliases = dict(aliases or {})
    if comm is None:
        return pl.pallas_call(body, name=name, grid=grid, in_specs=list(in_specs), out_specs=out_specs,
                              out_shape=out_shape, scratch_shapes=list(scratch_shapes),
                              input_output_aliases=aliases, compiler_params=params)
    single = not isinstance(out_shape, (list, tuple))
    out_specs_l = [out_specs] if single else list(out_specs)
    out_shape_l = [out_shape] if single else list(out_shape)
    n_in, n_out, n_c = len(in_specs), len(out_shape_l), len(comm["arrays"])

    def hosted(*refs):
        core_in, core_out = refs[:n_in], refs[n_in + n_c:n_in + n_c + n_out]
        c_refs = refs[n_in + n_c + n_out:n_in + 2 * n_c + n_out]
        scratch, (send_sems, recv_sems) = refs[n_in + 2 * n_c + n_out:-2], refs[-2:]
        ids = [pl.program_id(i) for i in range(len(grid))]
        first, last = ids[0] == 0, ids[0] == grid[0] - 1
        for i in range(1, len(grid)):
            first = jnp.logical_and(first, ids[i] == 0)
            last = jnp.logical_and(last, ids[i] == grid[i] - 1)

        @pl.when(first)
        def _():
            comm["start"](c_refs, send_sems, recv_sems)

        body(*core_in, *core_out, *scratch)

        @pl.when(last)
        def _():
            comm["wait"](c_refs, send_sems, recv_sems)

    any_spec = pl.BlockSpec(memory_space=pl.ANY)
    for i in range(n_c):
        aliases[n_in + i] = n_out + i
    call = pl.pallas_call(
        hosted, name=name, grid=grid, in_specs=list(in_specs) + [any_spec] * n_c,
        out_specs=out_specs_l + [any_spec] * n_c,
        out_shape=out_shape_l + [_sds(t.shape, t.dtype) for t in comm["arrays"]],
        scratch_shapes=list(scratch_shapes) + [pltpu.SemaphoreType.DMA((comm["nsem"],))] * 2,
        input_output_aliases=aliases, compiler_params=params)

    def run(*args):
        outs = call(*args, *comm["arrays"])
        comm["done"](outs[n_out:])
        return outs[0] if single else outs[:n_out]

    return run


def _mm_call(name, a, b, *, grid, a_spec, b_spec, o_spec, out_shape, ta=False, tb=False, k_axis=None, nk=1,
             acc_shape=None, res=None, res_spec=None, buf=None, compute=None, comm=None,
             extra_in=(), extra_out=(), epilogue=None):
    has_res, has_buf = res is not None, buf is not None
    in_place = nk > 1 and not has_res and out_shape.dtype == F32 and epilogue is None
    n_xi, n_xo = len(extra_in), len(extra_out)

    def body(*refs):
        a_ref, b_ref = refs[0], refs[1]
        pos = 2
        r_ref = None
        if has_res:
            r_ref = refs[pos]
            pos += 1
        if has_buf:
            pos += 1
        x_refs = refs[pos:pos + n_xi]
        pos += n_xi
        first_rows = pl.program_id(0) == 0
        o_ref = refs[pos]
        y_refs = refs[pos + 1:pos + 1 + n_xo]
        if compute is None:
            av = a_ref[...]
            bv = b_ref[...]
            part = _dot(av.reshape(-1, av.shape[-1]), bv.reshape(-1, bv.shape[-1]), ta, tb)
        else:
            part = compute(a_ref, b_ref)

        def finish(val):
            if r_ref is not None:
                val = val + r_ref[...]
            if epilogue is not None:
                epilogue(val, x_refs, o_ref, y_refs, first_rows)
            else:
                o_ref[...] = val.reshape(o_ref.shape).astype(o_ref.dtype)

        if nk == 1:
            finish(part)
        elif in_place:
            @pl.when(pl.program_id(k_axis) == 0)
            def _():
                o_ref[...] = jnp.zeros(o_ref.shape, F32)

            o_ref[...] += part.reshape(o_ref.shape)
        else:
            acc_ref = refs[pos + 1 + n_xo]
            k = pl.program_id(k_axis)

            @pl.when(k == 0)
            def _():
                acc_ref[...] = jnp.zeros(acc_ref.shape, F32)

            acc_ref[...] += part

            @pl.when(k == nk - 1)
            def _():
                finish(acc_ref[...])

    ins, in_specs = [a, b], [a_spec, b_spec]
    if has_res:
        ins.append(res)
        in_specs.append(res_spec)
    aliases = {}
    if has_buf:
        aliases = {len(ins): 0}
        ins.append(buf)
        in_specs.append(pl.BlockSpec(memory_space=pl.ANY))
    for arr, sp in extra_in:
        ins.append(arr)
        in_specs.append(sp)
    scratch = [pltpu.VMEM(acc_shape, F32)] if nk > 1 and not in_place else []
    if not extra_out:
        return _call(body, name=name, grid=grid, in_specs=in_specs, out_specs=o_spec, out_shape=out_shape,
                     scratch_shapes=scratch, aliases=aliases, comm=comm)(*ins)
    return _call(body, name=name, grid=grid, in_specs=in_specs, out_specs=[o_spec] + [sp for _, sp in extra_out],
                 out_shape=[out_shape] + [sh for sh, _ in extra_out], scratch_shapes=scratch, aliases=aliases,
                 comm=comm)(*ins)


def _rms_fwd(name, x, g, tr=512):
    T = x.shape[0]

    def body(x_ref, g_ref, h_ref):
        xv = x_ref[...]
        r = lax.rsqrt(jnp.mean(xv * xv, axis=-1, keepdims=True) + RMS_EPS)
        h_ref[...] = (xv * r * g_ref[...]).astype(BF16)

    return pl.pallas_call(
        body, name=name, grid=(T // tr,),
        in_specs=[pl.BlockSpec((tr, D), lambda i: (i, 0)), pl.BlockSpec((1, D), lambda i: (0, 0))],
        out_specs=pl.BlockSpec((tr, D), lambda i: (i, 0)), out_shape=_sds((T, D), BF16),
        compiler_params=_params(1),
    )(x, g)


def _rms_bwd(name, x, dh, g, dres, tr=512):
    T = x.shape[0]

    def body(x_ref, dh_ref, g_ref, dres_ref, dx_ref, dg_ref):
        xv = x_ref[...]
        r = lax.rsqrt(jnp.mean(xv * xv, axis=-1, keepdims=True) + RMS_EPS)
        xh = xv * r
        dhv = dh_ref[...]
        part = _rows8(dhv * xh)

        @pl.when(pl.program_id(0) == 0)
        def _():
            dg_ref[...] = part

        @pl.when(pl.program_id(0) > 0)
        def _():
            dg_ref[...] += part

        dxh = dhv * g_ref[...]
        dx_ref[...] = dres_ref[...] + r * (dxh - xh * jnp.mean(dxh * xh, axis=-1, keepdims=True))

    row = pl.BlockSpec((tr, D), lambda i: (i, 0))
    return pl.pallas_call(
        body, name=name, grid=(T // tr,),
        in_specs=[row, row, pl.BlockSpec((1, D), lambda i: (0, 0)), row],
        out_specs=[row, pl.BlockSpec((8, D), lambda i: (0, 0))],
        out_shape=[_sds((T, D), F32), _sds((8, D), F32)],
        compiler_params=_params(1),
    )(x, dh, g, dres)


def _final_loss(x, g, tgt, tr=512):
    T = x.shape[0]

    def body(x_ref, g_ref, t_ref, dx_ref, dg_ref, sq_ref):
        xv = x_ref[...]
        r = lax.rsqrt(jnp.mean(xv * xv, axis=-1, keepdims=True) + RMS_EPS)
        xh = xv * r
        gv = g_ref[...]
        e = xh * gv - t_ref[...]
        dy = e * (1.0 / D)
        pg = _rows8(dy * xh)
        ps = _rows8(e * e)

        @pl.when(pl.program_id(0) == 0)
        def _():
            dg_ref[...] = pg
            sq_ref[...] = ps

        @pl.when(pl.program_id(0) > 0)
        def _():
            dg_ref[...] += pg
            sq_ref[...] += ps

        dxh = dy * gv
        dx_ref[...] = r * (dxh - xh * jnp.mean(dxh * xh, axis=-1, keepdims=True))

    row = pl.BlockSpec((tr, D), lambda i: (i, 0))
    acc = pl.BlockSpec((8, D), lambda i: (0, 0))
    return pl.pallas_call(
        body, name="final_loss", grid=(T // tr,),
        in_specs=[row, pl.BlockSpec((1, D), lambda i: (0, 0)), row],
        out_specs=[row, acc, acc],
        out_shape=[_sds((T, D), F32), _sds((8, D), F32), _sds((8, D), F32)],
        compiler_params=_params(1),
    )(x, g, tgt)


def _ple_fwd(name, x, pe, pg, tr=512):
    T = x.shape[0]

    def body(x_ref, pe_ref, pg_ref, o_ref):
        o_ref[...] = x_ref[...] + pe_ref[...] * _sigmoid(pg_ref[...])

    row = pl.BlockSpec((tr, D), lambda i: (i, 0))
    return pl.pallas_call(
        body, name=name, grid=(T // tr,), in_specs=[row, row, row], out_specs=row,
        out_shape=_sds((T, D), F32), compiler_params=_params(1),
    )(x, pe, pg)


def _ple_bwd(name, dx, pe, pg, tr=512):
    T = dx.shape[0]

    def body(dx_ref, pe_ref, pg_ref, dpe_ref, dpg_ref):
        s = _sigmoid(pg_ref[...])
        dxv = dx_ref[...]
        dpe_ref[...] = (dxv * s).astype(BF16)
        dpg_ref[...] = (dxv * pe_ref[...] * s * (1.0 - s)).astype(BF16)

    row = pl.BlockSpec((tr, D), lambda i: (i, 0))
    return pl.pallas_call(
        body, name=name, grid=(T // tr,), in_specs=[row, row, row], out_specs=[row, row],
        out_shape=[_sds((T, D), BF16), _sds((T, D), BF16)], compiler_params=_params(1),
    )(dx, pe, pg)


def _gate_fwd(name, z, ya, yb, tr=512):
    T = z.shape[0]
    w = 512

    def body(ga_ref, gb_ref, ya_ref, yb_ref, o_ref):
        o_ref[...] = (_sigmoid(ga_ref[...]) * ya_ref[...] + _sigmoid(gb_ref[...]) * yb_ref[...]).astype(BF16)

    col = pl.BlockSpec((tr, w), lambda i, j: (i, j))
    return pl.pallas_call(
        body, name=name, grid=(T // tr, D // w),
        in_specs=[pl.BlockSpec((tr, w), lambda i, j: (i, OFF_GA // w + j)),
                  pl.BlockSpec((tr, w), lambda i, j: (i, OFF_GB // w + j)), col, col],
        out_specs=col, out_shape=_sds((T, D), BF16), compiler_params=_params(2),
    )(z, z, ya, yb)


def _gate_bwd(name, z, off, y, dm, dz, tr=512):
    T = z.shape[0]
    w = 512
    has_dz = dz is not None

    def body(*refs):
        g_ref, y_ref, dm_ref = refs[:3]
        dy_ref, dz_ref = refs[-2:]
        s = _sigmoid(g_ref[...])
        dmv = dm_ref[...]
        dy_ref[...] = (dmv * s).astype(BF16)
        dz_ref[...] = (dmv * y_ref[...] * s * (1.0 - s)).astype(BF16)

    col = pl.BlockSpec((tr, w), lambda i, j: (i, j))
    gcol = pl.BlockSpec((tr, w), lambda i, j: (i, off // w + j))
    ins, in_specs, aliases = [z, y, dm], [gcol, col, col], {}
    if has_dz:
        ins.append(dz)
        in_specs.append(pl.BlockSpec(memory_space=pl.ANY))
        aliases = {3: 1}
    return pl.pallas_call(
        body, name=name, grid=(T // tr, D // w), in_specs=in_specs, out_specs=[col, gcol],
        out_shape=[_sds((T, D), BF16), _sds((T, ZW), BF16)], input_output_aliases=aliases,
        compiler_params=_params(2),
    )(*ins)


def _shift_down(v, k, rows):
    return jnp.where(rows >= k, pltpu.roll(v, k, 0), 0.0)


def _shift_up(v, k, rows):
    n = v.shape[0]
    return jnp.where(rows < n - k, pltpu.roll(v, n - k, 0), 0.0)


def _pool_window(v, g, rows, shift):
    s2 = v + shift(v, 1, rows)
    s4 = s2 + shift(s2, 2, rows)
    s8 = s4 + shift(s4, 4, rows)
    s16 = s8 + shift(s8, 8, rows)
    return jnp.where(g == 0, s2, jnp.where(g == 1, s4, jnp.where(g == 2, s8, s16)))


def _pool_count(g, rows):
    wlen = jnp.left_shift(2, g).astype(F32)
    return jnp.minimum(rows.astype(F32) + 1.0, wlen)


def _pool_fwd(name, z3, g_sc, scale):
    Bn = z3.shape[0]
    gw = 256

    def body(u_ref, pw_ref, sc_ref, pooled_ref, ms_ref):
        g = pl.program_id(1)
        u = u_ref[...]
        rows = lax.broadcasted_iota(jnp.int32, u.shape, 0)
        pooled = (_pool_window(u, g, rows, _shift_down) / _pool_count(g, rows) - u).astype(BF16)
        pooled_ref[...] = pooled
        pw = pw_ref[...].reshape(gw, gw)
        mixed = jnp.dot(pooled, pw, preferred_element_type=F32)
        ms_ref[...] = (mixed * sc_ref[...]).astype(BF16)

    blk = pl.BlockSpec((None, SEQ, gw), lambda b, g: (b, 0, g))
    return pl.pallas_call(
        body, name=name, grid=(Bn, 4),
        in_specs=[pl.BlockSpec((None, SEQ, gw), lambda b, g: (b, 0, OFF_U // gw + g)),
                  pl.BlockSpec((NCHIP, 64, gw), lambda b, g: (0, 12 + g, 0)),
                  pl.BlockSpec((1, gw), lambda b, g: (0, g))],
        out_specs=[blk, blk],
        out_shape=[_sds((Bn, SEQ, D), BF16), _sds((Bn, SEQ, D), BF16)],
        compiler_params=_params(2),
    )(z3, g_sc, scale)


def _pool_bwd(name, dms3, pooled3, g_sc, scale, dz3, gg_sc):
    Bn = dms3.shape[0]
    gw = 256
    has_gg = gg_sc is not None

    def body(*refs):
        dms_ref, pooled_ref, pw_ref, sc_ref = refs[:4]
        dz_ref, dpw_ref, dsc_ref = refs[-3:]
        g, b = pl.program_id(0), pl.program_id(1)
        pooled = pooled_ref[...]
        pw = pw_ref[...].reshape(gw, gw)
        dms = dms_ref[...]
        mixed = jnp.dot(pooled, pw, preferred_element_type=F32)
        psc = _rows8(dms * mixed)
        dmixed = (dms * sc_ref[...]).astype(BF16)
        dpw = lax.dot_general(pooled, dmixed, (((0,), (0,)), ((), ())), preferred_element_type=F32)
        dpw = dpw.reshape(NCHIP, 64, gw)

        @pl.when(b == 0)
        def _():
            dsc_ref[...] = psc
            dpw_ref[...] = dpw

        @pl.when(b > 0)
        def _():
            dsc_ref[...] += psc
            dpw_ref[...] += dpw

        dpooled = lax.dot_general(dmixed, pw, (((1,), (1,)), ((), ())), preferred_element_type=F32)
        rows = lax.broadcasted_iota(jnp.int32, dpooled.shape, 0)
        dq = dpooled / _pool_count(g, rows)
        dz_ref[...] = (_pool_window(dq, g, rows, _shift_up) - dpooled).astype(BF16)

    ins = [dms3, pooled3, g_sc, scale, dz3]
    in_specs = [pl.BlockSpec((None, SEQ, gw), lambda g, b: (b, 0, g)),
                pl.BlockSpec((None, SEQ, gw), lambda g, b: (b, 0, g)),
                pl.BlockSpec((NCHIP, 64, gw), lambda g, b: (0, 12 + g, 0)),
                pl.BlockSpec((1, gw), lambda g, b: (0, g)),
                pl.BlockSpec(memory_space=pl.ANY)]
    aliases = {4: 0}
    if has_gg:
        ins.append(gg_sc)
        in_specs.append(pl.BlockSpec(memory_space=pl.ANY))
        aliases[5] = 1
    return pl.pallas_call(
        body, name=name, grid=(4, Bn), in_specs=in_specs,
        out_specs=[pl.BlockSpec((None, SEQ, gw), lambda g, b: (b, 0, OFF_U // gw + g)),
                   pl.BlockSpec((NCHIP, 64, gw), lambda g, b: (0, 12 + g, 0)),
                   pl.BlockSpec((8, gw), lambda g, b: (0, g))],
        out_shape=[_sds(dz3.shape, BF16), _sds((NCHIP, D, 256), F32), _sds((8, D), F32)],
        input_output_aliases=aliases, compiler_params=_params(2),
    )(*ins)


CT = 256
NCT = FF // CT


def _conv_pre(u, cw_ref, cb_ref, rows):
    return (cb_ref[...] + cw_ref[0:1, :] * _shift_down(u, 2, rows) + cw_ref[1:2, :] * _shift_down(u, 1, rows)
            + cw_ref[2:3, :] * u)


def _conv_fwd(name, u3, cw, cb, comm=None):
    Bn = u3.shape[0]

    def body(ug_ref, uv_ref, cwg_ref, cwv_ref, cbg_ref, cbv_ref, a_ref):
        ug, uv = ug_ref[...], uv_ref[...]
        rows = lax.broadcasted_iota(jnp.int32, ug.shape, 0)
        yg = _conv_pre(ug, cwg_ref, cbg_ref, rows)
        yv = _conv_pre(uv, cwv_ref, cbv_ref, rows)
        a_ref[...] = (yg * _sigmoid(yg) * yv).astype(BF16)

    def blk(off):
        return pl.BlockSpec((None, SEQ, CT), lambda b, c: (b, 0, off + c))

    return _call(
        body, name=name, grid=(Bn, NCT),
        in_specs=[blk(0), blk(NCT),
                  pl.BlockSpec((3, CT), lambda b, c: (0, c)), pl.BlockSpec((3, CT), lambda b, c: (0, NCT + c)),
                  pl.BlockSpec((1, CT), lambda b, c: (0, c)), pl.BlockSpec((1, CT), lambda b, c: (0, NCT + c))],
        out_specs=blk(0), out_shape=_sds((Bn, SEQ, FF), BF16), comm=comm,
    )(u3, u3, cw, cw, cb, cb)


def _conv_bwd(name, da3, u3, cw, cb, comm=None):
    Bn = u3.shape[0]
    last = NCT * Bn - 1

    def body(da_ref, ug_ref, uv_ref, cwg_ref, cwv_ref, cbg_ref, cbv_ref,
             du_ref, dcwg_ref, dcwv_ref, dcbg_ref, dcbv_ref, stage_g, stage_v, sems):
        c, b = pl.program_id(0), pl.program_id(1)
        step = c * Bn + b
        ug, uv, da = ug_ref[...], uv_ref[...], da_ref[...]
        rows = lax.broadcasted_iota(jnp.int32, ug.shape, 0)
        yg = _conv_pre(ug, cwg_ref, cbg_ref, rows)
        yv = _conv_pre(uv, cwv_ref, cbv_ref, rows)
        s = _sigmoid(yg)
        dyv = da * (yg * s)
        dyg = da * yv * (s * (1.0 + yg * (1.0 - s)))

        def writes(off_c, stage, sem):
            col = pl.multiple_of(off_c + c * CT, CT)
            return pltpu.make_async_copy(stage, du_ref.at[b, :, pl.ds(col, CT)], sem)

        @pl.when(step > 0)
        def _():
            writes(0, stage_g, sems.at[0]).wait()
            writes(FF, stage_v, sems.at[1]).wait()

        for dy, u, cw_ref, stage, dcw_ref, dcb_ref in ((dyg, ug, cwg_ref, stage_g, dcwg_ref, dcbg_ref),
                                                       (dyv, uv, cwv_ref, stage_v, dcwv_ref, dcbv_ref)):
            d1, d2 = _shift_up(dy, 1, rows), _shift_up(dy, 2, rows)
            stage[...] = (cw_ref[2:3, :] * dy + cw_ref[1:2, :] * d1 + cw_ref[0:1, :] * d2).astype(BF16)
            dcw = jnp.concatenate([jnp.sum(d2 * u, axis=0, keepdims=True), jnp.sum(d1 * u, axis=0, keepdims=True),
                                   jnp.sum(dy * u, axis=0, keepdims=True)], axis=0)
            dcb = jnp.sum(dy, axis=0, keepdims=True)

            @pl.when(b == 0)
            def _():
                dcw_ref[...] = dcw
                dcb_ref[...] = dcb

            @pl.when(b > 0)
            def _():
                dcw_ref[...] += dcw
                dcb_ref[...] += dcb

        writes(0, stage_g, sems.at[0]).start()
        writes(FF, stage_v, sems.at[1]).start()

        @pl.when(step == last)
        def _():
            writes(0, stage_g, sems.at[0]).wait()
            writes(FF, stage_v, sems.at[1]).wait()

    def blk(off):
        return pl.BlockSpec((None, SEQ, CT), lambda c, b: (b, 0, off + c))

    def vec(r, off):
        return pl.BlockSpec((r, CT), lambda c, b: (0, off + c))

    du3, dcwg, dcwv, dcbg, dcbv = _call(
        body, name=name, grid=(NCT, Bn),
        in_specs=[blk(0), blk(0), blk(NCT), vec(3, 0), vec(3, NCT), vec(1, 0), vec(1, NCT)],
        out_specs=[pl.BlockSpec(memory_space=pl.ANY), vec(3, 0), vec(3, 0), vec(1, 0), vec(1, 0)],
        out_shape=[_sds((Bn, SEQ, UW), BF16), _sds((3, FF), F32), _sds((3, FF), F32), _sds((1, FF), F32),
                   _sds((1, FF), F32)],
        scratch_shapes=[pltpu.VMEM((SEQ, CT), BF16)] * 2 + [pltpu.SemaphoreType.DMA((2,))], comm=comm,
    )(da3, u3, u3, cw, cw, cb, cb)
    return du3, jnp.concatenate([dcwg, dcwv], axis=1), jnp.concatenate([dcbg, dcbv], axis=1)


def _rope_tables():
    pos = jnp.arange(SEQ, dtype=F32)
    inv_freq = jnp.exp(jnp.arange(0, ROPE_DIM, 2, dtype=F32) * (-math.log(ROPE_THETA) / ROPE_DIM))
    ang = pos[:, None] * inv_freq[None, :]
    cos, sin = jnp.cos(ang), jnp.sin(ang)
    half = ROPE_DIM // 2
    zeros = jnp.zeros((SEQ, HEAD - ROPE_DIM), F32)
    zh = jnp.zeros((SEQ, half), F32)
    tab_c = jnp.concatenate([cos, cos, zeros + 1.0], axis=1)
    tab_a = jnp.concatenate([-sin, zh, zeros], axis=1)
    tab_b = jnp.concatenate([zh, sin, zeros], axis=1)
    return tab_c, tab_a, tab_b


def _rot(v, tc, ta, tb):
    half = ROPE_DIM // 2
    return v * tc + pltpu.roll(v, HEAD - half, 1) * ta + pltpu.roll(v, half, 1) * tb


def _rot_t(dv, tc, ta, tb):
    half = ROPE_DIM // 2
    return dv * tc + pltpu.roll(dv * ta, half, 1) + pltpu.roll(dv * tb, HEAD - half, 1)


def _band_masks():
    qi = lax.broadcasted_iota(jnp.int32, (HEAD, 2 * HEAD), 0)
    ki = lax.broadcasted_iota(jnp.int32, (HEAD, 2 * HEAD), 1)
    diff = HEAD + qi - ki
    both = (diff >= 0) & (diff <= HEAD)
    q1 = lax.broadcasted_iota(jnp.int32, (HEAD, HEAD), 0)
    k1 = lax.broadcasted_iota(jnp.int32, (HEAD, HEAD), 1)
    return q1 >= k1, both


_NT = (((1,), (1,)), ((), ()))
_TN = (((0,), (0,)), ((), ()))
_SCALE = HEAD ** -0.5


ATT_W = HEAD
ATT_HP = ATT_W // HEAD


def _res_rows(r, n, d, base=0):
    return pl.ds(base * d + r, n, stride=d) if d > 1 else pl.ds(base, n)


def _attn_load(q_ref, k_ref, v_ref, tc_ref, ta_ref, tb_ref, qs, ks, vs, d):
    L = SEQ // d
    for r in range(d):
        rows = _res_rows(r, L, d)
        dst = slice(r * L, (r + 1) * L)
        tc, ta, tb = tc_ref[dst, :], ta_ref[dst, :], tb_ref[dst, :]
        for hh in range(ATT_HP):
            sl = slice(hh * HEAD, (hh + 1) * HEAD)
            qs[dst, sl] = _rot(q_ref[rows, sl], tc, ta, tb).astype(BF16)
            ks[dst, sl] = _rot(k_ref[rows, sl], tc, ta, tb).astype(BF16)
            vs[dst, sl] = v_ref[rows, sl].astype(BF16)


def _attn_fwd(name, z3, tabs, g, d, comm=None):
    Bn = z3.shape[0]
    L = SEQ // d
    nb = L // HEAD
    W, nh = ATT_W, GROUP_W // ATT_W

    def body(q_ref, k_ref, v_ref, tc_ref, ta_ref, tb_ref, o_ref, l_ref, qs, ks, vs, sc, pc):
        m_first, m_both = _band_masks()
        _attn_load(q_ref, k_ref, v_ref, tc_ref, ta_ref, tb_ref, qs, ks, vs, d)
        blocks = [(r, n) for r in range(d) for n in range(nb)]

        def spans(r, n):
            rq = slice(r * L + n * HEAD, r * L + (n + 1) * HEAD)
            rk = slice(r * L + max(n - 1, 0) * HEAD, r * L + (n + 1) * HEAD)
            return rq, rk, slice(0, HEAD if n == 0 else 2 * HEAD)

        for i, (r, n) in enumerate(blocks):
            rq, rk, kc = spans(r, n)
            sc[i, :, kc] = lax.dot_general(qs[rq, :], ks[rk, :], _NT, preferred_element_type=F32)
        for i, (r, n) in enumerate(blocks):
            rq, rk, kc = spans(r, n)
            s = jnp.where(m_first if n == 0 else m_both, sc[i, :, kc] * _SCALE, NEG_INF)
            m = jnp.max(s, axis=-1, keepdims=True)
            e = jnp.exp(s - m)
            den = jnp.sum(e, axis=-1, keepdims=True)
            pc[i, :, kc] = (e * (1.0 / den)).astype(BF16)
            l_ref[_res_rows(r, HEAD, d, n * HEAD), :] = jnp.broadcast_to(m + jnp.log(den), (HEAD, HEAD))
        for i, (r, n) in enumerate(blocks):
            rq, rk, kc = spans(r, n)
            o_ref[_res_rows(r, HEAD, d, n * HEAD), :] = jnp.dot(pc[i, :, kc], vs[rk, :], preferred_element_type=F32)

    def zcol(off):
        return pl.BlockSpec((None, SEQ, W), lambda b, h: (b, 0, (off + g * GROUP_W) // W + h))

    tab = pl.BlockSpec((SEQ, HEAD), lambda b, h: (0, 0))
    out = pl.BlockSpec((None, SEQ, W), lambda b, h: (b, 0, h))
    return _call(
        body, name=name, grid=(Bn, nh),
        in_specs=[zcol(0), zcol(OFF_K), zcol(OFF_V), tab, tab, tab],
        out_specs=[out, out],
        out_shape=[_sds((Bn, SEQ, GROUP_W), F32), _sds((Bn, SEQ, GROUP_W), F32)],
        scratch_shapes=[pltpu.VMEM((SEQ, W), BF16)] * 3
        + [pltpu.VMEM((SEQ // HEAD, HEAD, 2 * HEAD), F32), pltpu.VMEM((SEQ // HEAD, HEAD, 2 * HEAD), BF16)],
        comm=comm,
    )(z3, z3, z3, *tabs)


def _attn_bwd(name, z3, tabs, g, d, do3, lse3, delta3, dz3, comm=None):
    Bn = z3.shape[0]
    L = SEQ // d
    nb = L // HEAD
    W, nh = ATT_W, GROUP_W // ATT_W

    def body(q_ref, k_ref, v_ref, tc_ref, ta_ref, tb_ref, do_ref, l_ref, dl_ref, dz_in, dz_ref,
             qs, ks, vs, dos, dqs, dks, dvs, nat, oq, ok, ov, sc, dpc, pc, dsc, sems):
        b, h = pl.program_id(0), pl.program_id(1)
        m_first, m_both = _band_masks()
        _attn_load(q_ref, k_ref, v_ref, tc_ref, ta_ref, tb_ref, qs, ks, vs, d)
        for r in range(d):
            dos[r * L:(r + 1) * L, :] = do_ref[_res_rows(r, L, d), :].astype(BF16)
        dks[...] = jnp.zeros_like(dks)
        dvs[...] = jnp.zeros_like(dvs)
        blocks = [(r, n) for r in range(d) for n in range(nb)]

        def spans(r, n):
            rq = slice(r * L + n * HEAD, r * L + (n + 1) * HEAD)
            rk = slice(r * L + max(n - 1, 0) * HEAD, r * L + (n + 1) * HEAD)
            return rq, rk, slice(0, HEAD if n == 0 else 2 * HEAD)

        for i, (r, n) in enumerate(blocks):
            rq, rk, kc = spans(r, n)
            sc[i, :, kc] = lax.dot_general(qs[rq, :], ks[rk, :], _NT, preferred_element_type=F32)
            dpc[i, :, kc] = lax.dot_general(dos[rq, :], vs[rk, :], _NT, preferred_element_type=F32)
        for i, (r, n) in enumerate(blocks):
            rq, rk, kc = spans(r, n)
            rows = _res_rows(r, HEAD, d, n * HEAD)
            s = jnp.where(m_first if n == 0 else m_both, sc[i, :, kc] * _SCALE, NEG_INF)
            p = jnp.exp(s - l_ref[rows, :][:, 0:1])
            pc[i, :, kc] = p.astype(BF16)
            dsc[i, :, kc] = (p * (dpc[i, :, kc] - dl_ref[rows, :][:, 0:1]) * _SCALE).astype(BF16)
        for i, (r, n) in enumerate(blocks):
            rq, rk, kc = spans(r, n)
            dqs[rq, :] = jnp.dot(dsc[i, :, kc], ks[rk, :], preferred_element_type=F32)
        for i, (r, n) in enumerate(blocks):
            rq, rk, kc = spans(r, n)
            dks[rk, :] += lax.dot_general(dsc[i, :, kc], qs[rq, :], _TN, preferred_element_type=F32)
            dvs[rk, :] += lax.dot_general(pc[i, :, kc], dos[rq, :], _TN, preferred_element_type=F32)
        step = b * nh + h

        def writes():
            base = g * GROUP_W + h * W
            return [pltpu.make_async_copy(src, dz_ref.at[b, :, pl.ds(pl.multiple_of(base + off, HEAD), W)],
                                          sems.at[i])
                    for i, (src, off) in enumerate(((oq, 0), (ok, OFF_K), (ov, OFF_V)))]

        @pl.when(step > 0)
        def _():
            for cp in writes():
                cp.wait()

        for src, dst, rotate in ((dqs, oq, True), (dks, ok, True), (dvs, ov, False)):
            for r in range(d):
                val = src[r * L:(r + 1) * L, :]
                if rotate:
                    rm = slice(r * L, (r + 1) * L)
                    val = _rot_t(val, tc_ref[rm, :], ta_ref[rm, :], tb_ref[rm, :])
                nat[_res_rows(r, L, d), :] = val
            dst[...] = nat[...].astype(BF16)
        for cp in writes():
            cp.start()

        @pl.when(step == Bn * nh - 1)
        def _():
            for cp in writes():
                cp.wait()

    def zcol(off):
        return pl.BlockSpec((None, SEQ, W), lambda b, h: (b, 0, (off + g * GROUP_W) // W + h))

    tab = pl.BlockSpec((SEQ, HEAD), lambda b, h: (0, 0))
    gcol = pl.BlockSpec((None, SEQ, W), lambda b, h: (b, 0, h))
    any_spec = pl.BlockSpec(memory_space=pl.ANY)
    return _call(
        body, name=name, grid=(Bn, nh),
        in_specs=[zcol(0), zcol(OFF_K), zcol(OFF_V), tab, tab, tab, gcol, gcol, gcol, any_spec],
        out_specs=any_spec,
        out_shape=_sds((Bn, SEQ, ZW), BF16),
        scratch_shapes=[pltpu.VMEM((SEQ, W), BF16)] * 4 + [pltpu.VMEM((SEQ, W), F32)] * 4
        + [pltpu.VMEM((SEQ, W), BF16)] * 3
        + [pltpu.VMEM((SEQ // HEAD, HEAD, 2 * HEAD), F32)] * 2 + [pltpu.VMEM((SEQ // HEAD, HEAD, 2 * HEAD), BF16)] * 2
        + [pltpu.SemaphoreType.DMA((3,))],
        aliases={9: 0}, comm=comm,
    )(z3, z3, z3, *tabs, do3, lse3, delta3, dz3)


def _merge_weights(l0, l1, l2):
    m = jnp.maximum(jnp.maximum(l0, l1), l2)
    e0, e1, e2 = jnp.exp(l0 - m), jnp.exp(l1 - m), jnp.exp(l2 - m)
    inv = 1.0 / (e0 + e1 + e2)
    return e0 * inv, e1 * inv, e2 * inv


def _merge_fwd(name, outs, lses, tr=512):
    T = outs[0].shape[0]

    def body(o0, o1, o2, l0, l1, l2, a_ref):
        w0, w1, w2 = _merge_weights(l0[...], l1[...], l2[...])
        a_ref[...] = (w0 * o0[...] + w1 * o1[...] + w2 * o2[...]).astype(BF16)

    row = pl.BlockSpec((tr, GROUP_W), lambda i: (i, 0))
    return pl.pallas_call(
        body, name=name, grid=(T // tr,), in_specs=[row] * 6, out_specs=row,
        out_shape=_sds((T, GROUP_W), BF16), compiler_params=_params(1),
    )(*outs, *lses)


def _merge_bwd(name, outs, lses, dattn, tr=512):
    T = outs[0].shape[0]

    def body(o0, o1, o2, l0, l1, l2, da_ref, d0, d1, d2, e0, e1, e2):
        w = _merge_weights(l0[...], l1[...], l2[...])
        da = da_ref[...]
        attn = w[0] * o0[...] + w[1] * o1[...] + w[2] * o2[...]
        prod = da * attn
        csum = jnp.concatenate(
            [jnp.broadcast_to(jnp.sum(prod[:, hh * HEAD:(hh + 1) * HEAD], axis=-1, keepdims=True), (tr, HEAD))
             for hh in range(GROUP_W // HEAD)], axis=1)
        for wg, d_ref, e_ref in zip(w, (d0, d1, d2), (e0, e1, e2)):
            d_ref[...] = wg * da
            e_ref[...] = wg * csum

    row = pl.BlockSpec((tr, GROUP_W), lambda i: (i, 0))
    res = pl.pallas_call(
        body, name=name, grid=(T // tr,), in_specs=[row] * 7, out_specs=[row] * 6,
        out_shape=[_sds((T, GROUP_W), F32)] * 6,
        compiler_params=_params(1),
    )(*outs, *lses, dattn)
    return res[:3], res[3:]


def _rms_rows(xv, g):
    r = lax.rsqrt(jnp.mean(xv * xv, axis=-1, keepdims=True) + RMS_EPS)
    return (xv * r * g).astype(BF16)


def _epi_norm(val, x_refs, o_ref, y_refs, first_rows):
    o_ref[...] = val
    y_refs[0][...] = _rms_rows(val, x_refs[0][...])


def _epi_ple(val, x_refs, o_ref, y_refs, first_rows):
    o_ref[...] = val
    xn = x_refs[0][...] + x_refs[1][...] * _sigmoid(val)
    y_refs[0][...] = xn
    if len(y_refs) > 1:
        y_refs[1][...] = _rms_rows(xn, x_refs[2][...])


def _epi_norm_bwd(val, x_refs, o_ref, y_refs, first_rows):
    xv = x_refs[0][...]
    r = lax.rsqrt(jnp.mean(xv * xv, axis=-1, keepdims=True) + RMS_EPS)
    xh = xv * r
    part = _rows8(val * xh)

    @pl.when(first_rows)
    def _():
        y_refs[0][...] = part

    @pl.when(jnp.logical_not(first_rows))
    def _():
        y_refs[0][...] += part

    dxh = val * x_refs[2][...]
    o_ref[...] = x_refs[1][...] + r * (dxh - xh * jnp.mean(dxh * xh, axis=-1, keepdims=True))


def _local_step(x3, p4, tgt3, vecs, ex):
    Bn = x3.shape[0]
    T = Bn * SEQ
    x = x3.reshape(T, D)
    tgt = tgt3.reshape(T, D)
    pb = p4.astype(BF16).reshape(DEPTH, T, PLE)
    tabs = {}
    for d in DILATIONS:
        tabs[d] = [t.reshape(SEQ // d, d, HEAD).transpose(1, 0, 2).reshape(SEQ, HEAD) for t in _rope_tables()]
    tm = 1024 if T % 1024 == 0 else 512
    nt = T // tm
    tk = 1024 if T % 1024 == 0 else 512
    ntk = T // tk
    tm5 = 512
    f32o = lambda n: _sds((T, n), F32)

    def spec(shape, fn):
        return pl.BlockSpec(shape, fn)

    def _mm(name, *args, **kwargs):
        return _mm_call(name, *args, comm=ex.hook(name), **kwargs)

    def cols4(a_ref, b_ref):
        av = a_ref[...]
        return jnp.concatenate([_dot(av, b_ref[j]) for j in range(NCHIP)], axis=1)

    def rows4(a_ref, b_ref):
        av = a_ref[...]
        return jnp.concatenate([_dot(av, b_ref[:, j * 256:(j + 1) * 256], ta=True) for j in range(NCHIP)], axis=0)

    def kchunks4(a_ref, b_ref):
        total = _dot(a_ref[:, 0:256], b_ref[0], tb=True)
        for j in range(1, NCHIP):
            total = total + _dot(a_ref[:, j * 256:(j + 1) * 256], b_ref[j], tb=True)
        return total

    row5 = spec((tm5, D), lambda i, *_: (i, 0))
    gain = spec((1, D), lambda *_: (0, 0))
    bf_rows = (_sds((T, D), BF16), row5)

    saved = []
    h = _rms_fwd("rms_mix0", x, vecs["g_mix"][0:1])
    for l in range(DEPTH):
        L = str(l)
        G = ex.weights(l)
        g_mix, g_ffn, g_ple = (vecs[k][l:l + 1] for k in ("g_mix", "g_ffn", "g_ple"))
        pscale, cb, cw = vecs["pool_scale"][l:l + 1], vecs["conv_b"][l:l + 1], vecs["conv_w"][l]
        z = _mm("mm_z" + L, h, G["in"], grid=(nt, NCHIP),
                a_spec=spec((tm, D), lambda i, n: (i, 0)),
                b_spec=spec((None, D, IN_S), lambda i, n: (n, 0, 0)),
                o_spec=spec((tm, IN_S), lambda i, n: (i, n)), out_shape=f32o(ZW))
        z3 = z.reshape(Bn, SEQ, ZW)
        outs, lses = [], []
        for g, d in enumerate(DILATIONS):
            o_g, l_g = _attn_fwd("attn_fwd%d_%d" % (g, l), z3, tabs[d], g, d, comm=ex.hook("attn_fwd%d_%d" % (g, l)))
            outs.append(o_g.reshape(T, GROUP_W))
            lses.append(l_g.reshape(T, GROUP_W))
        attn = _merge_fwd("merge_fwd" + L, outs, lses)
        ya = _mm("mm_ya" + L, attn, G["sc"], grid=(nt,), compute=cols4,
                 a_spec=spec((tm, GROUP_W), lambda i: (i, 0)),
                 b_spec=spec((NCHIP, GROUP_W, 256), lambda i: (0, 0, 0)),
                 o_spec=spec((tm, D), lambda i: (i, 0)), out_shape=f32o(D))
        pooled3, ms3 = _pool_fwd("pool_fwd" + L, z3, G["sc"], pscale)
        ms = ms3.reshape(T, D)

        def row_sharded(name, a, rb, res=None, kdim=D, **fused):
            if rb is None:
                b_arr, b_spec = G["dn"], spec((NCHIP, DN_S, D), lambda i: (0, 0, 0))
            else:
                b_arr, b_spec = G["r3"], spec((NCHIP, 256, D), lambda i: (0, rb, 0))
            return _mm(name, a, b_arr, grid=(T // tm5,),
                       a_spec=spec((tm5, kdim), lambda i: (i, 0)), b_spec=b_spec,
                       o_spec=row5, out_shape=f32o(D), res=res, res_spec=None if res is None else row5, **fused)

        yb = row_sharded("mm_yb" + L, ms, 0)
        merged = _gate_fwd("gate_fwd" + L, z, ya, yb)
        x1, h2 = row_sharded("mm_o" + L, merged, 1, res=x, epilogue=_epi_norm, extra_in=[(g_ffn, gain)],
                             extra_out=[bf_rows])
        u = _mm("mm_up" + L, h2, G["up"], grid=(nt, NCHIP),
                a_spec=spec((tm, D), lambda i, n: (i, 0)),
                b_spec=spec((None, D, UP_S), lambda i, n: (n, 0, 0)),
                o_spec=spec((tm, UP_S), lambda i, n: (i, n)), out_shape=f32o(UW))
        u3 = u.reshape(Bn, SEQ, UW)
        act = _conv_fwd("conv_fwd" + L, u3, cw, cb, comm=ex.hook("conv_fwd" + L)).reshape(T, FF)
        x2, h3 = row_sharded("mm_down" + L, act, None, res=x1, kdim=FF, epilogue=_epi_norm,
                             extra_in=[(g_ple, gain)], extra_out=[bf_rows])
        pe = _mm("mm_pe" + L, pb[l], G["sc"], grid=(nt,), compute=cols4,
                 a_spec=spec((tm, PLE), lambda i: (i, 0)),
                 b_spec=spec((NCHIP, 256, 256), lambda i: (0, 2, 0)),
                 o_spec=spec((tm, D), lambda i: (i, 0)), out_shape=f32o(D))
        fused_in = [(x2, row5), (pe, row5)]
        fused_out = [(f32o(D), row5)]
        if l + 1 < DEPTH:
            fused_in.append((vecs["g_mix"][l + 1:l + 2], gain))
            fused_out.append(bf_rows)
        pg, x3n, *h_next = row_sharded("mm_pg" + L, h3, 2, epilogue=_epi_ple, extra_in=fused_in,
                                       extra_out=fused_out)
        saved.append(dict(x=x, h=h, z=z, outs=outs, lses=lses, attn=attn, ya=ya, yb=yb, pooled3=pooled3, ms=ms,
                          merged=merged, x1=x1, h2=h2, u3=u3, act=act, x2=x2, h3=h3, pg=pg, pe=pe))
        x = x3n
        h = h_next[0] if h_next else None

    dx, dg_final8, sq8 = _final_loss(x, vecs["g_final"].reshape(1, D), tgt)

    gg_shape = {k: _sds(G[k].shape, F32) for k in G}
    small = {"g_final": dg_final8}

    for l in reversed(range(DEPTH)):
        L = str(l)
        sv = saved[l]
        G = ex.weights(l)
        GG = dict.fromkeys(_KEYS)
        g_mix, g_ffn, g_ple = (vecs[k][l:l + 1] for k in ("g_mix", "g_ffn", "g_ple"))
        pscale, cb, cw = vecs["pool_scale"][l:l + 1], vecs["conv_b"][l:l + 1], vecs["conv_w"][l]

        def wgrad_rows(name, a, b_arr, key, rb):
            GG[key] = _mm(name, a, b_arr, grid=(2, ntk), ta=True, k_axis=1, nk=ntk, acc_shape=(D, 512),
                          a_spec=spec((tk, D), lambda n, k: (k, 0)),
                          b_spec=spec((tk, 512), lambda n, k: (k, n)),
                          o_spec=spec((NCHIP, 256, 512), lambda n, k: (0, rb, n)),
                          out_shape=gg_shape[key], buf=GG[key])

        def dgrad_rows(name, dy, rb, **fused):
            return _mm(name, dy, G["r3"], grid=(T // tm5,), tb=True,
                       a_spec=spec((tm5, D), lambda i: (i, 0)),
                       b_spec=spec((NCHIP, 256, D), lambda i: (0, rb, 0)),
                       o_spec=row5, out_shape=f32o(D), **fused)

        def norm_bwd(xin, dres, g):
            return dict(epilogue=_epi_norm_bwd, extra_in=[(xin, row5), (dres, row5), (g, gain)],
                        extra_out=[(_sds((8, D), F32), spec((8, D), lambda *_: (0, 0)))])

        dpe, dpg = _ple_bwd("ple_bwd" + L, dx, sv["pe"], sv["pg"])
        GG["sc"] = _mm("wg_ple" + L, pb[l], dpe, grid=(ntk,), compute=rows4, k_axis=0, nk=ntk,
                       acc_shape=(NCHIP * PLE, 256),
                       a_spec=spec((tk, PLE), lambda k: (k, 0)), b_spec=spec((tk, D), lambda k: (k, 0)),
                       o_spec=spec((NCHIP, 256, 256), lambda k: (0, 2, 0)),
                       out_shape=gg_shape["sc"], buf=GG["sc"])
        wgrad_rows("wg_pg" + L, sv["h3"], dpg, "r3", 2)
        dx, small["g_ple" + L] = dgrad_rows("dg_pg" + L, dpg, 2, **norm_bwd(sv["x2"], dx, g_ple))

        da = _mm("dg_down" + L, dx, G["dn"], grid=(T // 256,), tb=True,
                 a_spec=spec((256, D), lambda i: (i, 0)),
                 b_spec=spec((NCHIP, DN_S, D), lambda i: (0, 0, 0)),
                 o_spec=spec((256, FF), lambda i: (i, 0)), out_shape=f32o(FF))
        GG["dn"] = _mm("wg_down" + L, sv["act"], dx, grid=(2, ntk), ta=True, k_axis=1, nk=ntk,
                       acc_shape=(FF, 512),
                       a_spec=spec((tk, FF), lambda n, k: (k, 0)), b_spec=spec((tk, 512), lambda n, k: (k, n)),
                       o_spec=spec((NCHIP, DN_S, 512), lambda n, k: (0, 0, n)),
                       out_shape=gg_shape["dn"], buf=GG["dn"])
        du3, dcw, dcb = _conv_bwd("conv_bwd" + L, da.reshape(Bn, SEQ, FF), sv["u3"], cw, cb,
                                  comm=ex.hook("conv_bwd" + L))
        small["conv_w" + L], small["conv_b" + L] = dcw, dcb
        du = du3.reshape(T, UW)
        dx, small["g_ffn" + L] = _mm(
            "dg_up" + L, du, G["up"], grid=(T // tm5, NCHIP), tb=True, k_axis=1, nk=NCHIP, acc_shape=(tm5, D),
            a_spec=spec((tm5, UP_S), lambda i, k: (i, k)), b_spec=spec((None, D, UP_S), lambda i, k: (k, 0, 0)),
            o_spec=spec((tm5, D), lambda i, k: (i, 0)), out_shape=f32o(D), **norm_bwd(sv["x1"], dx, g_ffn))
        GG["up"] = _mm("wg_up" + L, sv["h2"], du, grid=(NCHIP, ntk), ta=True, k_axis=1, nk=ntk,
                       acc_shape=(D, UP_S),
                       a_spec=spec((tk, D), lambda j, k: (k, 0)),
                       b_spec=spec((tk, UP_S), lambda j, k: (k, j)),
                       o_spec=spec((None, D, UP_S), lambda j, k: (j, 0, 0)),
                       out_shape=gg_shape["up"], buf=GG["up"])

        dmerged = dgrad_rows("dg_o" + L, dx, 1)
        wgrad_rows("wg_o" + L, sv["merged"], dx, "r3", 1)
        dya, dz = _gate_bwd("gate_bwd_a" + L, sv["z"], OFF_GA, sv["ya"], dmerged, None)
        dyb, dz = _gate_bwd("gate_bwd_b" + L, sv["z"], OFF_GB, sv["yb"], dmerged, dz)
        dms = dgrad_rows("dg_yb" + L, dyb, 0)
        wgrad_rows("wg_yb" + L, sv["ms"], dyb, "r3", 0)
        dz3, GG["sc"], small["pool_scale" + L] = _pool_bwd(
            "pool_bwd" + L, dms.reshape(Bn, SEQ, D), sv["pooled3"], G["sc"], pscale,
            dz.reshape(Bn, SEQ, ZW), GG["sc"])
        dattn = _mm("dg_ya" + L, dya, G["sc"], grid=(nt,), compute=kchunks4,
                    a_spec=spec((tm, D), lambda i: (i, 0)),
                    b_spec=spec((NCHIP, GROUP_W, 256), lambda i: (0, 0, 0)),
                    o_spec=spec((tm, GROUP_W), lambda i: (i, 0)), out_shape=f32o(GROUP_W))
        GG["sc"] = _mm("wg_ya" + L, sv["attn"], dya, grid=(ntk,), compute=rows4, k_axis=0, nk=ntk,
                       acc_shape=(NCHIP * GROUP_W, 256),
                       a_spec=spec((tk, GROUP_W), lambda k: (k, 0)), b_spec=spec((tk, D), lambda k: (k, 0)),
                       o_spec=spec((NCHIP, GROUP_W, 256), lambda k: (0, 0, 0)),
                       out_shape=gg_shape["sc"], buf=GG["sc"])
        ex.grads_ready(l, {k: GG[k] for k in _KEYS[1:]})
        dos, deltas = _merge_bwd("merge_bwd" + L, sv["outs"], sv["lses"], dattn)
        view3 = lambda t: t.reshape(Bn, SEQ, GROUP_W)
        sz3 = sv["z"].reshape(Bn, SEQ, ZW)
        for g, d in enumerate(DILATIONS):
            dz3 = _attn_bwd("attn_bwd%d_%d" % (g, l), sz3, tabs[d], g, d, view3(dos[g]), view3(sv["lses"][g]),
                            view3(deltas[g]), dz3, comm=ex.hook("attn_bwd%d_%d" % (g, l)))
        dz = dz3.reshape(T, ZW)
        dx_in = dx
        dx, small["g_mix" + L] = _mm(
            "dg_z" + L, dz, G["in"], grid=(T // tm5, NCHIP), tb=True, k_axis=1, nk=NCHIP, acc_shape=(tm5, D),
            a_spec=spec((tm5, IN_S), lambda i, k: (i, k)), b_spec=spec((None, D, IN_S), lambda i, k: (k, 0, 0)),
            o_spec=spec((tm5, D), lambda i, k: (i, 0)), out_shape=f32o(D), **norm_bwd(sv["x"], dx_in, g_mix))
        GG["in"] = _mm("wg_z" + L, sv["h"], dz, grid=(NCHIP, ntk), ta=True, k_axis=1, nk=ntk,
                       acc_shape=(D, IN_S),
                       a_spec=spec((tk, D), lambda n, k: (k, 0)), b_spec=spec((tk, IN_S), lambda n, k: (k, n)),
                       o_spec=spec((None, D, IN_S), lambda n, k: (n, 0, 0)),
                       out_shape=gg_shape["in"], buf=GG["in"])
        ex.grads_ready(l, {"in": GG["in"]})

    return sq8, dx.reshape(Bn, SEQ, D), small


_ANY = pl.BlockSpec(memory_space=pl.ANY)


def _place():
    x, y, c = lax.axis_index("x"), lax.axis_index("y"), lax.axis_index("c")
    chips = [(1 - x, y), (x, 1 - y), (1 - x, 1 - y)]
    return x, y, c, 2 * x + y, chips


def _half(rows, cc):
    return pl.ds(cc * (rows // 2), rows // 2)


def _remote(src, dst, send_sems, recv_sems, i, to):
    return pltpu.make_async_remote_copy(src_ref=src, dst_ref=dst, send_sem=send_sems.at[i], recv_sem=recv_sems.at[i],
                                        device_id=to, device_id_type=MESH)


def _exchange_gather_ici(stacks, done):
    n = len(stacks)
    rows = [t.shape[1] for t in stacks]

    def start(refs, send_sems, recv_sems):
        x, y, c, me, chips = _place()
        for k in range(n):
            part = refs[k].at[me, _half(rows[k], c)]
            for j, chip in enumerate(chips):
                _remote(part, part, send_sems, recv_sems, 3 * k + j, (*chip, c)).start()

    def wait(refs, send_sems, recv_sems):
        x, y, c, me, chips = _place()
        for k in range(n):
            for j, chip in enumerate(chips):
                part = refs[k].at[2 * chip[0] + chip[1], _half(rows[k], c)]
                _remote(part, part, send_sems, recv_sems, 3 * k + j, (*chip, c)).wait()

    return dict(arrays=list(stacks), nsem=3 * n, start=start, wait=wait, done=done)


def _exchange_gather_d2d(stacks, done):
    n = len(stacks)
    rows = [t.shape[1] for t in stacks]

    def copies(refs, send_sems, recv_sems, mine):
        x, y, c, me, chips = _place()
        cc = c if mine else 1 - c
        return [_remote(part, part, send_sems, recv_sems, 3 * k + j, (x, y, 1 - c))
                for k in range(n) for j, chip in enumerate(chips)
                for part in [refs[k].at[2 * chip[0] + chip[1], _half(rows[k], cc)]]]

    def start(refs, send_sems, recv_sems):
        for cp in copies(refs, send_sems, recv_sems, True):
            cp.start()

    def wait(refs, send_sems, recv_sems):
        for cp in copies(refs, send_sems, recv_sems, False):
            cp.wait()

    return dict(arrays=list(stacks), nsem=3 * n, start=start, wait=wait, done=done)


def _exchange_halves(g5, recv, done):
    n = len(g5)

    def copies(refs, send_sems, recv_sems):
        x, y, c, me, chips = _place()
        return [_remote(refs[k].at[:, 1 - c], refs[n + k], send_sems, recv_sems, k, (x, y, 1 - c)) for k in range(n)]

    def start(refs, send_sems, recv_sems):
        for cp in copies(refs, send_sems, recv_sems):
            cp.start()

    def wait(refs, send_sems, recv_sems):
        for cp in copies(refs, send_sems, recv_sems):
            cp.wait()

    return dict(arrays=list(g5) + list(recv), nsem=n, start=start, wait=wait, done=done)


def _exchange_chips(parts, landing, done):
    n = len(parts)

    def start(refs, send_sems, recv_sems):
        x, y, c, me, chips = _place()
        for k in range(n):
            for j, chip in enumerate(chips):
                _remote(refs[k].at[2 * chip[0] + chip[1]], refs[n + k].at[me], send_sems, recv_sems, 3 * k + j,
                        (*chip, c)).start()

    def wait(refs, send_sems, recv_sems):
        x, y, c, me, chips = _place()
        for k in range(n):
            for j, chip in enumerate(chips):
                slot = refs[n + k].at[2 * chip[0] + chip[1]]
                _remote(slot, slot, send_sems, recv_sems, 3 * k + j, (*chip, c)).wait()

    return dict(arrays=list(parts) + list(landing), nsem=3 * n, start=start, wait=wait, done=done)


def _exchange_share(full, layer, done):
    n = len(full)

    def copies(refs, send_sems, recv_sems, mine):
        x, y, c, me, chips = _place()
        cc = c if mine else 1 - c
        return [_remote(part, part, send_sems, recv_sems, k, (x, y, 1 - c))
                for k in range(n) for part in [refs[k].at[layer, cc]]]

    def start(refs, send_sems, recv_sems):
        for cp in copies(refs, send_sems, recv_sems, True):
            cp.start()

    def wait(refs, send_sems, recv_sems):
        for cp in copies(refs, send_sems, recv_sems, False):
            cp.wait()

    return dict(arrays=list(full), nsem=n, start=start, wait=wait, done=done)


def _exchange_call(name, comm):
    arrays = comm["arrays"]
    n = len(arrays)

    def body(*refs):
        outs, send_sems, recv_sems = refs[n:2 * n], refs[2 * n], refs[2 * n + 1]
        comm["start"](outs, send_sems, recv_sems)
        comm["wait"](outs, send_sems, recv_sems)

    outs = pl.pallas_call(
        body, name=name, in_specs=[_ANY] * n, out_specs=[_ANY] * n,
        out_shape=[_sds(t.shape, t.dtype) for t in arrays],
        scratch_shapes=[pltpu.SemaphoreType.DMA((comm["nsem"],))] * 2,
        input_output_aliases={i: i for i in range(n)},
    )(*arrays)
    comm["done"](outs)


def _gather_first(stacks, cw4):
    n = len(stacks)
    ici = _exchange_gather_ici(stacks, None)
    d2d = _exchange_gather_d2d(stacks, None)
    rows = [t.shape[1] for t in stacks]

    def body(*refs):
        g_refs, cwg_ref = refs[n + 1:2 * n + 1], refs[2 * n + 1]
        s_ici, r_ici, s_d2d, r_d2d, s_cw, r_cw = refs[2 * n + 2:]
        x, y, c, me, chips = _place()

        def cw_copy(j, slot, chip):
            part = cwg_ref.at[slot]
            return _remote(part, part, s_cw, r_cw, j, (*chip, c))

        ici["start"](g_refs, s_ici, r_ici)
        for j, chip in enumerate(chips):
            cw_copy(j, me, chip).start()
        for k in range(n):
            for j, chip in enumerate(chips):
                part = g_refs[k].at[2 * chip[0] + chip[1], _half(rows[k], c)]
                _remote(part, part, s_ici, r_ici, 3 * k + j, (*chip, c)).wait()
                _remote(part, part, s_d2d, r_d2d, 3 * k + j, (x, y, 1 - c)).start()
        d2d["wait"](g_refs, s_d2d, r_d2d)
        for j, chip in enumerate(chips):
            cw_copy(j, 2 * chip[0] + chip[1], chip).wait()

    outs = pl.pallas_call(
        body, name="gather_first", in_specs=[_ANY] * (n + 1), out_specs=[_ANY] * (n + 1),
        out_shape=[_sds(t.shape, t.dtype) for t in stacks] + [_sds(cw4.shape, cw4.dtype)],
        scratch_shapes=[pltpu.SemaphoreType.DMA((3 * n,))] * 4 + [pltpu.SemaphoreType.DMA((3,))] * 2,
        input_output_aliases={i: i for i in range(n + 1)},
    )(*stacks, cw4)
    return outs[:n], outs[n]


def _halves_and_small(g5, recv, small):
    n = len(g5)
    halves = _exchange_halves(g5, recv, None)

    def body(*refs):
        small_ref = refs[2 * n]
        c_refs, red_ref = refs[2 * n + 1:4 * n + 1], refs[4 * n + 1]
        gath, send_sems, recv_sems, s_send, s_recv = refs[4 * n + 2:]
        x, y, c, me, chips = _place()
        dev = 4 * x + 2 * y + c
        gath[dev] = small_ref[...]
        halves["start"](c_refs, send_sems, recv_sems)
        for r in range(1, 8):
            peer = (x ^ (r >> 2), y ^ ((r >> 1) & 1), c ^ (r & 1))
            _remote(small_ref, gath.at[dev], s_send, s_recv, r - 1, peer).start()
        for r in range(1, 8):
            peer = (x ^ (r >> 2), y ^ ((r >> 1) & 1), c ^ (r & 1))
            src = 4 * peer[0] + 2 * peer[1] + peer[2]
            _remote(small_ref, gath.at[src], s_send, s_recv, r - 1, peer).wait()
        total = gath[0]
        for i in range(1, 8):
            total = total + gath[i]
        red_ref[...] = total
        halves["wait"](c_refs, send_sems, recv_sems)

    vm = pl.BlockSpec(memory_space=pltpu.VMEM)
    arrays = list(g5) + list(recv)
    outs = pl.pallas_call(
        body, name="halves_and_small", in_specs=[_ANY] * (2 * n) + [vm], out_specs=[_ANY] * (2 * n) + [vm],
        out_shape=[_sds(t.shape, t.dtype) for t in arrays] + [_sds(small.shape, F32)],
        scratch_shapes=[pltpu.VMEM((8,) + small.shape, F32), pltpu.SemaphoreType.DMA((n,)),
                        pltpu.SemaphoreType.DMA((n,)), pltpu.SemaphoreType.DMA((7,)), pltpu.SemaphoreType.DMA((7,))],
        input_output_aliases={i: i for i in range(2 * n)},
    )(*arrays, small)
    return outs[:n], outs[n:2 * n], outs[2 * n]


def _row_tile(rh):
    for cand in (512, 384, 352, 256, 128):
        if rh % cand == 0:
            return cand
    return rh


def _add_halves(name, g5, recv, place):
    _, _, rh, cols = g5.shape
    tr = _row_tile(rh)

    def body(place_ref, g_ref, r_ref, o_ref, own_ref):
        val = (g_ref[...] + r_ref[...]).astype(BF16)
        o_ref[...] = val

        @pl.when(pl.program_id(1) == place_ref[1])
        def _():
            own_ref[...] = val

    grid_spec = pltpu.PrefetchScalarGridSpec(
        num_scalar_prefetch=1, grid=(rh // tr, NCHIP),
        in_specs=[pl.BlockSpec((None, None, tr, cols), lambda i, j, pr: (j, pr[0], i, 0)),
                  pl.BlockSpec((None, tr, cols), lambda i, j, pr: (j, i, 0))],
        out_specs=[pl.BlockSpec((None, tr, cols), lambda i, j, pr: (j, i, 0)),
                   pl.BlockSpec((None, tr, cols), lambda i, j, pr: (pr[1], i, 0))])
    return pl.pallas_call(
        body, name=name, grid_spec=grid_spec, out_shape=[_sds(recv.shape, BF16)] * 2, compiler_params=_params(2),
    )(place, g5, recv)


def _sum_chips(name, landing, place, layer, full):
    _, rh, cols = landing.shape
    tr = _row_tile(rh)
    has_full = full is not None

    def body(*refs):
        r_ref, o_ref = refs[1], refs[-1]
        total = r_ref[0].astype(F32)
        for j in range(1, NCHIP):
            total = total + r_ref[j].astype(F32)
        o_ref[...] = total

    grid_spec = pltpu.PrefetchScalarGridSpec(
        num_scalar_prefetch=1, grid=(rh // tr,),
        in_specs=[pl.BlockSpec((NCHIP, tr, cols), lambda i, pr: (0, i, 0))] + ([_ANY] if has_full else []),
        out_specs=pl.BlockSpec((None, None, tr, cols), lambda i, pr: (layer, pr[0], i, 0)))
    return pl.pallas_call(
        body, name=name, grid_spec=grid_spec, out_shape=_sds((DEPTH, 2, rh, cols), F32),
        input_output_aliases={2: 0} if has_full else {}, compiler_params=_params(1),
    )(place, landing, *([full] if has_full else []))


class _Schedule:
    FIRST, REST = ["in"], list(_KEYS[1:])

    def __init__(self, slotted, cw4, place):
        self.place = place
        self._w = [dict(zip(_KEYS, layer)) for layer in slotted]
        got, self.cw4 = _gather_first([self._w[0][k] for k in self.FIRST], cw4)
        self._w[0].update(zip(self.FIRST, got))
        self._g5, self._recv, self._parts, self._landing = [{}, {}], [{}, {}], [{}, {}], [{}, {}]
        self.full = {}
        every = list(_KEYS)
        self._hooks = {
            "mm_z0": lambda: self._gather(_exchange_gather_ici, 0, self.REST),
            "attn_fwd0_0": lambda: self._gather(_exchange_gather_d2d, 0, self.REST),
            "attn_fwd1_0": lambda: self._gather(_exchange_gather_ici, 1, self.FIRST),
            "mm_up0": lambda: self._gather(_exchange_gather_ici, 1, self.REST),
            "conv_fwd0": lambda: self._gather(_exchange_gather_d2d, 1, every),
            "dg_down0": lambda: self._halves(1, every),
            "conv_bwd0": lambda: self._chips(1, every),
            "dg_up0": lambda: self._share(1, every),
            "attn_bwd0_0": lambda: self._halves(0, self.REST),
            "dg_z0": lambda: self._chips(0, self.REST),
            "wg_z0": lambda: self._share(0, self.REST),
        }

    def weights(self, layer):
        return self._w[layer]

    def hook(self, name):
        make = self._hooks.get(name)
        return make() if make else None

    def grads_ready(self, layer, GG):
        for k, t in GG.items():
            g5 = t.reshape(NCHIP, 2, t.shape[1] // 2, t.shape[2])
            self._g5[layer][k] = g5
            self._recv[layer][k] = lax.empty((NCHIP,) + g5.shape[2:], F32)

    def _gather(self, make, layer, keys):
        def done(arrays):
            self._w[layer].update(zip(keys, arrays))
        return make([self._w[layer][k] for k in keys], done)

    def _halves(self, layer, keys):
        return _exchange_halves([self._g5[layer][k] for k in keys], [self._recv[layer][k] for k in keys],
                                lambda arrays: self._halves_done(layer, keys, arrays))

    def _halves_done(self, layer, keys, arrays):
        n = len(keys)
        for k, g, r in zip(keys, arrays[:n], arrays[n:]):
            self._parts[layer][k], self._landing[layer][k] = _add_halves(
                "add_halves%d_%s" % (layer, k), g, r, self.place)

    def _chips(self, layer, keys):
        return _exchange_chips([self._parts[layer][k] for k in keys], [self._landing[layer][k] for k in keys],
                               lambda arrays: self._chips_done(layer, keys, arrays))

    def _chips_done(self, layer, keys, arrays):
        for k, t in zip(keys, arrays[len(keys):]):
            self.full[k] = _sum_chips("sum_chips%d_%s" % (layer, k), t, self.place, layer, self.full.get(k))

    def _share(self, layer, keys):
        def done(arrays):
            self.full.update(zip(keys, arrays))
        return _exchange_share([self.full[k] for k in keys], layer, done)

    def finish(self, small):
        keys = self.FIRST
        g5, recv, small_red = _halves_and_small([self._g5[0][k] for k in keys], [self._recv[0][k] for k in keys], small)
        self._halves_done(0, keys, list(g5) + list(recv))
        _exchange_call("exchange_chips_last", self._chips(0, keys))
        _exchange_call("share_halves_last", self._share(0, keys))
        return [self.full[k] for k in _KEYS], small_red


def _adamw(name, w, g, m, v):
    shape = w.shape
    cols = shape[-1]
    rows = 1
    for s in shape[:-1]:
        rows *= s
    tr = rows
    for cand in (256, 128, 64):
        if rows > cand and rows % cand == 0:
            tr = cand
            break
    c1 = 1.0 / (1.0 - B1 ** STEP)
    c2 = 1.0 / (1.0 - B2 ** STEP)

    def body(w_ref, g_ref, m_ref, v_ref, d_ref, nm_ref, nv_ref):
        gv = g_ref[...]
        nm = B1 * m_ref[...] + (1.0 - B1) * gv
        nv = B2 * v_ref[...] + (1.0 - B2) * (gv * gv)
        nm_ref[...] = nm
        nv_ref[...] = nv
        d_ref[...] = -LR * ((nm * c1) / (jnp.sqrt(nv * c2) + ADAM_EPS) + WD * w_ref[...])

    blk = pl.BlockSpec((tr, cols), lambda i: (i, 0))
    outs = pl.pallas_call(
        body, name=name, grid=(rows // tr,), in_specs=[blk] * 4, out_specs=[blk] * 3,
        out_shape=[_sds((rows, cols), F32)] * 3, compiler_params=_params(1),
    )(*(t.reshape(rows, cols) for t in (w, g, m, v)))
    return tuple(o.reshape(shape) for o in outs)


def _pack_small(small):
    rows = [jnp.sum(small["g_mix%d" % l], axis=0, keepdims=True) for l in range(DEPTH)]
    rows += [jnp.sum(small["pool_scale%d" % l], axis=0, keepdims=True) for l in range(DEPTH)]
    rows += [jnp.sum(small["g_ffn%d" % l], axis=0, keepdims=True) for l in range(DEPTH)]
    rows += [jnp.sum(small["g_ple%d" % l], axis=0, keepdims=True) for l in range(DEPTH)]
    rows += [jnp.sum(small["g_final"], axis=0, keepdims=True)]
    flat = [small["conv_b%d" % l].reshape(-1) for l in range(DEPTH)]
    flat += [small["conv_w%d" % l].reshape(-1) for l in range(DEPTH)]
    flat = jnp.concatenate(flat).reshape(-1, D)
    packed = jnp.concatenate(rows + [flat], axis=0)
    return jnp.pad(packed, ((0, SMALL_ROWS - packed.shape[0]), (0, 0)))


def _unpack_small(red):
    g_mix, pool_scale, g_ffn, g_ple = red[0:2], red[2:4], red[4:6], red[6:8]
    g_final = red[8]
    nb = DEPTH * UW // D
    conv_b = red[9:9 + nb].reshape(DEPTH, UW)
    conv_w = red[9 + nb:9 + 4 * nb].reshape(DEPTH, 3, UW)
    return g_mix, pool_scale, g_ffn, g_ple, g_final, conv_b, conv_w


def kernel(x, p, g_mix, w_in, w_ya, w_yb, pool_w, pool_scale, w_o, g_ffn, w_up, conv_w, conv_b, w_down, g_ple, w_ple, w_ple_gate, g_final, loss_target, m_g_mix, m_w_in, m_w_ya, m_w_yb, m_pool_w, m_pool_scale, m_w_o, m_g_ffn, m_w_up, m_conv_w, m_conv_b, m_w_down, m_g_ple, m_w_ple, m_w_ple_gate, m_g_final, v_g_mix, v_w_in, v_w_ya, v_w_yb, v_pool_w, v_pool_scale, v_w_o, v_g_ffn, v_w_up, v_conv_w, v_conv_b, v_w_down, v_g_ple, v_w_ple, v_w_ple_gate, v_g_final):
    me = 2 * lax.axis_index("x") + lax.axis_index("y")
    place = jnp.stack([lax.axis_index("c"), me]).astype(jnp.int32)

    def slot(shard):
        return lax.dynamic_update_index_in_dim(lax.empty((NCHIP,) + shard.shape, shard.dtype), shard, me, 0)

    packed = [
        w_in.astype(BF16), w_up.astype(BF16),
        jnp.concatenate([w_ya, w_ple, pool_w.reshape(DEPTH, 256, 256)], axis=1).astype(BF16),
        jnp.concatenate([w_yb, w_o, w_ple_gate], axis=1).astype(BF16),
        w_down.astype(BF16),
    ]
    slotted = [[slot(t[l]) for t in packed] for l in range(DEPTH)]
    ex = _Schedule(slotted, slot(conv_w.reshape(DEPTH * 3, UP_S)), place)
    cw_full = ex.cw4.reshape(NCHIP, DEPTH, 3, UP_S).transpose(1, 2, 0, 3).reshape(DEPTH, 3, UW)

    vecs = dict(g_mix=g_mix, pool_scale=pool_scale, g_ffn=g_ffn, g_ple=g_ple, g_final=g_final, conv_b=conv_b,
                conv_w=cw_full)
    sq8, grad_x, small = _local_step(x, p, loss_target, vecs, ex)
    loss = lax.psum(jnp.sum(sq8) * (0.5 / D), ("x", "y", "c"))

    full, small_red = ex.finish(_pack_small(small))
    r_in, r_up, r_sc, r_r3, r_dn = [f.reshape(DEPTH, -1, f.shape[-1]) for f in full]
    d_g_mix, d_pool_scale, d_g_ffn, d_g_ple, d_g_final, d_conv_b, d_conv_w_full = _unpack_small(small_red)
    d_conv_w = lax.dynamic_slice_in_dim(d_conv_w_full, me * UP_S, UP_S, axis=2)

    grads = dict(
        g_mix=d_g_mix, w_in=r_in, w_ya=r_sc[:, 0:512], w_yb=r_r3[:, 0:256],
        pool_w=r_sc[:, 768:1024].reshape(DEPTH, 4, 64, 256), pool_scale=d_pool_scale, w_o=r_r3[:, 256:512],
        g_ffn=d_g_ffn, w_up=r_up, conv_w=d_conv_w, conv_b=d_conv_b, w_down=r_dn, g_ple=d_g_ple,
        w_ple=r_sc[:, 512:768], w_ple_gate=r_r3[:, 512:768], g_final=d_g_final)
    weights = dict(g_mix=g_mix, w_in=w_in, w_ya=w_ya, w_yb=w_yb, pool_w=pool_w, pool_scale=pool_scale, w_o=w_o,
                   g_ffn=g_ffn, w_up=w_up, conv_w=conv_w, conv_b=conv_b, w_down=w_down, g_ple=g_ple, w_ple=w_ple,
                   w_ple_gate=w_ple_gate, g_final=g_final)
    m_in = dict(g_mix=m_g_mix, w_in=m_w_in, w_ya=m_w_ya, w_yb=m_w_yb, pool_w=m_pool_w, pool_scale=m_pool_scale,
                w_o=m_w_o, g_ffn=m_g_ffn, w_up=m_w_up, conv_w=m_conv_w, conv_b=m_conv_b, w_down=m_w_down,
                g_ple=m_g_ple, w_ple=m_w_ple, w_ple_gate=m_w_ple_gate, g_final=m_g_final)
    v_in = dict(g_mix=v_g_mix, w_in=v_w_in, w_ya=v_w_ya, w_yb=v_w_yb, pool_w=v_pool_w, pool_scale=v_pool_scale,
                w_o=v_w_o, g_ffn=v_g_ffn, w_up=v_w_up, conv_w=v_conv_w, conv_b=v_conv_b, w_down=v_w_down,
                g_ple=v_g_ple, w_ple=v_w_ple, w_ple_gate=v_w_ple_gate, g_final=v_g_final)
    names = ["g_mix", "w_in", "w_ya", "w_yb", "pool_w", "pool_scale", "w_o", "g_ffn", "w_up", "conv_w", "conv_b",
             "w_down", "g_ple", "w_ple", "w_ple_gate", "g_final"]
    deltas, new_m, new_v = [], [], []
    for nme in names:
        gr = grads[nme].reshape(weights[nme].shape)
        grads[nme] = gr
        dlt, nm, nv = _adamw("adamw_" + nme, weights[nme], gr, m_in[nme], v_in[nme])
        deltas.append(dlt)
        new_m.append(nm)
        new_v.append(nv)
    return (loss, grad_x, *[grads[nme] for nme in names], *deltas, *new_m, *new_v)
```

```python
import math

import jax
import jax.numpy as jnp
from jax import lax
from jax.experimental import pallas as pl
from jax.experimental.pallas import tpu as pltpu

F32 = jnp.float32
BF16 = jnp.bfloat16
_KEYS = ("in", "up", "sc", "r3", "dn")
MESH = pl.DeviceIdType.MESH

D = 1024
SEQ = 2048
DEPTH = 2
HEAD = 128
GROUP_W = 512
DILATIONS = (1, 4, 16)
ROPE_DIM = 32
ROPE_THETA = 500000.0
NEG_INF = -1e30
ZW = 7680
OFF_K, OFF_V, OFF_U, OFF_GA, OFF_GB = 1536, 3072, 4608, 5632, 6656
FF = 2816
UW = 2 * FF
PLE = 256
NCHIP = 4
IN_S, UP_S, DN_S = ZW // NCHIP, UW // NCHIP, FF // NCHIP
RMS_EPS = 1e-6
LR, B1, B2, ADAM_EPS, WD, STEP = 0.001, 0.9, 0.999, 1e-08, 0.01, 10
SMALL_ROWS = 56
VMEM_CAP = 48 * 1024 * 1024
VMEM_BIG = 58 * 1024 * 1024


def _params(n_grid, vmem=VMEM_CAP):
    return pltpu.CompilerParams(dimension_semantics=("arbitrary",) * n_grid, vmem_limit_bytes=vmem)


def _sigmoid(v):
    return 1.0 / (1.0 + jnp.exp(-v))


def _rows8(v):
    return jnp.sum(v.reshape(v.shape[0] // 8, 8, v.shape[1]), axis=0)


def _sds(shape, dtype):
    return jax.ShapeDtypeStruct(shape, dtype)


def _dot(av, bv, ta=False, tb=False):
    dims = (((0,) if ta else (1,), (1,) if tb else (0,)), ((), ()))
    return lax.dot_general(av.astype(BF16), bv.astype(BF16), dims, preferred_element_type=F32)


def _call(body, *, name, grid, in_specs, out_specs, out_shape, scratch_shapes=(), aliases=None, comm=None,
          vmem=VMEM_CAP):
    params = _params(len(grid), vmem)
    aliases = dict(aliases or {})
    if comm is None:
        return pl.pallas_call(body, name=name, grid=grid, in_specs=list(in_specs), out_specs=out_specs,
                              out_shape=out_shape, scratch_shapes=list(scratch_shapes),
                              input_output_aliases=aliases, compiler_params=params)
    single = not isinstance(out_shape, (list, tuple))
    out_specs_l = [out_specs] if single else list(out_specs)
    out_shape_l = [out_shape] if single else list(out_shape)
    n_in, n_out, n_c = len(in_specs), len(out_shape_l), len(comm["arrays"])

    def hosted(*refs):
        core_in, core_out = refs[:n_in], refs[n_in + n_c:n_in + n_c + n_out]
        c_refs = refs[n_in + n_c + n_out:n_in + 2 * n_c + n_out]
        scratch, (send_sems, recv_sems) = refs[n_in + 2 * n_c + n_out:-2], refs[-2:]
        ids = [pl.program_id(i) for i in range(len(grid))]
        first, last = ids[0] == 0, ids[0] == grid[0] - 1
        for i in range(1, len(grid)):
            first = jnp.logical_and(first, ids[i] == 0)
            last = jnp.logical_and(last, ids[i] == grid[i] - 1)

        @pl.when(first)
        def _():
            comm["start"](c_refs, send_sems, recv_sems)

        body(*core_in, *core_out, *scratch)

        @pl.when(last)
        def _():
            comm["wait"](c_refs, send_sems, recv_sems)

    any_spec = pl.BlockSpec(memory_space=pl.ANY)
    for i in range(n_c):
        aliases[n_in + i] = n_out + i
    call = pl.pallas_call(
        hosted, name=name, grid=grid, in_specs=list(in_specs) + [any_spec] * n_c,
        out_specs=out_specs_l + [any_spec] * n_c,
        out_shape=out_shape_l + [_sds(t.shape, t.dtype) for t in comm["arrays"]],
        scratch_shapes=list(scratch_shapes) + [pltpu.SemaphoreType.DMA((comm["nsem"],))] * 2,
        input_output_aliases=aliases, compiler_params=params)

    def run(*args):
        outs = call(*args, *comm["arrays"])
        comm["done"](outs[n_out:])
        return outs[0] if single else outs[:n_out]

    return run


def _mm_call(name, a, b, *, grid, a_spec, b_spec, o_spec, out_shape, ta=False, tb=False, k_axis=None, nk=1,
             acc_shape=None, res=None, res_spec=None, buf=None, compute=None, comm=None,
             extra_in=(), extra_out=(), epilogue=None, vmem=VMEM_CAP):
    has_res, has_buf = res is not None, buf is not None
    in_place = nk > 1 and not has_res and out_shape.dtype == F32 and epilogue is None
    n_xi, n_xo = len(extra_in), len(extra_out)

    def body(*refs):
        a_ref, b_ref = refs[0], refs[1]
        pos = 2
        r_ref = None
        if has_res:
            r_ref = refs[pos]
            pos += 1
        if has_buf:
            pos += 1
        x_refs = refs[pos:pos + n_xi]
        pos += n_xi
        first_rows = pl.program_id(0) == 0
        o_ref = refs[pos]
        y_refs = refs[pos + 1:pos + 1 + n_xo]
        if compute is None:
            av = a_ref[...]
            bv = b_ref[...]
            part = _dot(av.reshape(-1, av.shape[-1]), bv.reshape(-1, bv.shape[-1]), ta, tb)
        else:
            part = compute(a_ref, b_ref)

        def finish(val):
            if r_ref is not None:
                val = val + r_ref[...]
            if epilogue is not None:
                epilogue(val, x_refs, o_ref, y_refs, first_rows)
            else:
                o_ref[...] = val.reshape(o_ref.shape).astype(o_ref.dtype)

        if nk == 1:
            finish(part)
        elif in_place:
            @pl.when(pl.program_id(k_axis) == 0)
            def _():
                o_ref[...] = jnp.zeros(o_ref.shape, F32)

            o_ref[...] += part.reshape(o_ref.shape)
        else:
            acc_ref = refs[pos + 1 + n_xo]
            k = pl.program_id(k_axis)

            @pl.when(k == 0)
            def _():
                acc_ref[...] = jnp.zeros(acc_ref.shape, F32)

            acc_ref[...] += part

            @pl.when(k == nk - 1)
            def _():
                finish(acc_ref[...])

    ins, in_specs = [a, b], [a_spec, b_spec]
    if has_res:
        ins.append(res)
        in_specs.append(res_spec)
    aliases = {}
    if has_buf:
        aliases = {len(ins): 0}
        ins.append(buf)
        in_specs.append(pl.BlockSpec(memory_space=pl.ANY))
    for arr, sp in extra_in:
        ins.append(arr)
        in_specs.append(sp)
    scratch = [pltpu.VMEM(acc_shape, F32)] if nk > 1 and not in_place else []
    if not extra_out:
        return _call(body, name=name, grid=grid, in_specs=in_specs, out_specs=o_spec, out_shape=out_shape,
                     scratch_shapes=scratch, aliases=aliases, comm=comm, vmem=vmem)(*ins)
    return _call(body, name=name, grid=grid, in_specs=in_specs, out_specs=[o_spec] + [sp for _, sp in extra_out],
                 out_shape=[out_shape] + [sh for sh, _ in extra_out], scratch_shapes=scratch, aliases=aliases,
                 comm=comm, vmem=vmem)(*ins)


def _rms_fwd(name, x, g, tr=512):
    T = x.shape[0]

    def body(x_ref, g_ref, h_ref):
        xv = x_ref[...]
        r = lax.rsqrt(jnp.mean(xv * xv, axis=-1, keepdims=True) + RMS_EPS)
        h_ref[...] = (xv * r * g_ref[...]).astype(BF16)

    return pl.pallas_call(
        body, name=name, grid=(T // tr,),
        in_specs=[pl.BlockSpec((tr, D), lambda i: (i, 0)), pl.BlockSpec((1, D), lambda i: (0, 0))],
        out_specs=pl.BlockSpec((tr, D), lambda i: (i, 0)), out_shape=_sds((T, D), BF16),
        compiler_params=_params(1),
    )(x, g)


def _rms_bwd(name, x, dh, g, dres, tr=512):
    T = x.shape[0]

    def body(x_ref, dh_ref, g_ref, dres_ref, dx_ref, dg_ref):
        xv = x_ref[...]
        r = lax.rsqrt(jnp.mean(xv * xv, axis=-1, keepdims=True) + RMS_EPS)
        xh = xv * r
        dhv = dh_ref[...]
        part = _rows8(dhv * xh)

        @pl.when(pl.program_id(0) == 0)
        def _():
            dg_ref[...] = part

        @pl.when(pl.program_id(0) > 0)
        def _():
            dg_ref[...] += part

        dxh = dhv * g_ref[...]
        dx_ref[...] = dres_ref[...] + r * (dxh - xh * jnp.mean(dxh * xh, axis=-1, keepdims=True))

    row = pl.BlockSpec((tr, D), lambda i: (i, 0))
    return pl.pallas_call(
        body, name=name, grid=(T // tr,),
        in_specs=[row, row, pl.BlockSpec((1, D), lambda i: (0, 0)), row],
        out_specs=[row, pl.BlockSpec((8, D), lambda i: (0, 0))],
        out_shape=[_sds((T, D), F32), _sds((8, D), F32)],
        compiler_params=_params(1),
    )(x, dh, g, dres)


def _final_loss(x, g, tgt, tr=512):
    T = x.shape[0]

    def body(x_ref, g_ref, t_ref, dx_ref, dg_ref, sq_ref):
        xv = x_ref[...]
        r = lax.rsqrt(jnp.mean(xv * xv, axis=-1, keepdims=True) + RMS_EPS)
        xh = xv * r
        gv = g_ref[...]
        e = xh * gv - t_ref[...]
        dy = e * (1.0 / D)
        pg = _rows8(dy * xh)
        ps = _rows8(e * e)

        @pl.when(pl.program_id(0) == 0)
        def _():
            dg_ref[...] = pg
            sq_ref[...] = ps

        @pl.when(pl.program_id(0) > 0)
        def _():
            dg_ref[...] += pg
            sq_ref[...] += ps

        dxh = dy * gv
        dx_ref[...] = r * (dxh - xh * jnp.mean(dxh * xh, axis=-1, keepdims=True))

    row = pl.BlockSpec((tr, D), lambda i: (i, 0))
    acc = pl.BlockSpec((8, D), lambda i: (0, 0))
    return pl.pallas_call(
        body, name="final_loss", grid=(T // tr,),
        in_specs=[row, pl.BlockSpec((1, D), lambda i: (0, 0)), row],
        out_specs=[row, acc, acc],
        out_shape=[_sds((T, D), F32), _sds((8, D), F32), _sds((8, D), F32)],
        compiler_params=_params(1),
    )(x, g, tgt)


def _ple_fwd(name, x, pe, pg, tr=512):
    T = x.shape[0]

    def body(x_ref, pe_ref, pg_ref, o_ref):
        o_ref[...] = x_ref[...] + pe_ref[...] * _sigmoid(pg_ref[...])

    row = pl.BlockSpec((tr, D), lambda i: (i, 0))
    return pl.pallas_call(
        body, name=name, grid=(T // tr,), in_specs=[row, row, row], out_specs=row,
        out_shape=_sds((T, D), F32), compiler_params=_params(1),
    )(x, pe, pg)


def _ple_bwd(name, dx, pe, pg, tr=512):
    T = dx.shape[0]

    def body(dx_ref, pe_ref, pg_ref, dpe_ref, dpg_ref):
        s = _sigmoid(pg_ref[...])
        dxv = dx_ref[...]
        dpe_ref[...] = (dxv * s).astype(BF16)
        dpg_ref[...] = (dxv * pe_ref[...] * s * (1.0 - s)).astype(BF16)

    row = pl.BlockSpec((tr, D), lambda i: (i, 0))
    return pl.pallas_call(
        body, name=name, grid=(T // tr,), in_specs=[row, row, row], out_specs=[row, row],
        out_shape=[_sds((T, D), BF16), _sds((T, D), BF16)], compiler_params=_params(1),
    )(dx, pe, pg)


def _gate_fwd(name, z, ya, yb, tr=512):
    T = z.shape[0]
    w = 512

    def body(ga_ref, gb_ref, ya_ref, yb_ref, o_ref):
        o_ref[...] = (_sigmoid(ga_ref[...]) * ya_ref[...].astype(F32)
                      + _sigmoid(gb_ref[...]) * yb_ref[...].astype(F32)).astype(BF16)

    col = pl.BlockSpec((tr, w), lambda i, j: (i, j))
    return pl.pallas_call(
        body, name=name, grid=(T // tr, D // w),
        in_specs=[pl.BlockSpec((tr, w), lambda i, j: (i, OFF_GA // w + j)),
                  pl.BlockSpec((tr, w), lambda i, j: (i, OFF_GB // w + j)), col, col],
        out_specs=col, out_shape=_sds((T, D), BF16), compiler_params=_params(2),
    )(z, z, ya, yb)


def _gate_bwd(name, z, off, y, dm, dz, tr=512):
    T = z.shape[0]
    w = 512
    has_dz = dz is not None

    def body(*refs):
        g_ref, y_ref, dm_ref = refs[:3]
        dy_ref, dz_ref = refs[-2:]
        s = _sigmoid(g_ref[...])
        dmv = dm_ref[...].astype(F32)
        dy_ref[...] = (dmv * s).astype(BF16)
        dz_ref[...] = (dmv * y_ref[...].astype(F32) * s * (1.0 - s)).astype(BF16)

    col = pl.BlockSpec((tr, w), lambda i, j: (i, j))
    gcol = pl.BlockSpec((tr, w), lambda i, j: (i, off // w + j))
    ins, in_specs, aliases = [z, y, dm], [gcol, col, col], {}
    if has_dz:
        ins.append(dz)
        in_specs.append(pl.BlockSpec(memory_space=pl.ANY))
        aliases = {3: 1}
    return pl.pallas_call(
        body, name=name, grid=(T // tr, D // w), in_specs=in_specs, out_specs=[col, gcol],
        out_shape=[_sds((T, D), BF16), _sds((T, ZW), BF16)], input_output_aliases=aliases,
        compiler_params=_params(2),
    )(*ins)


def _shift_down(v, k, rows):
    return jnp.where(rows >= k, pltpu.roll(v, k, 0), 0.0)


def _shift_up(v, k, rows):
    n = v.shape[0]
    return jnp.where(rows < n - k, pltpu.roll(v, n - k, 0), 0.0)


def _pool_window(v, g, rows, shift):
    s2 = v + shift(v, 1, rows)
    s4 = s2 + shift(s2, 2, rows)
    s8 = s4 + shift(s4, 4, rows)
    s16 = s8 + shift(s8, 8, rows)
    return jnp.where(g == 0, s2, jnp.where(g == 1, s4, jnp.where(g == 2, s8, s16)))


def _pool_count(g, rows):
    wlen = jnp.left_shift(2, g).astype(F32)
    return jnp.minimum(rows.astype(F32) + 1.0, wlen)


def _pool_fwd(name, z3, g_sc, scale):
    Bn = z3.shape[0]
    gw = 256

    def body(u_ref, pw_ref, sc_ref, pooled_ref, ms_ref):
        g = pl.program_id(1)
        u = u_ref[...]
        rows = lax.broadcasted_iota(jnp.int32, u.shape, 0)
        pooled = (_pool_window(u, g, rows, _shift_down) / _pool_count(g, rows) - u).astype(BF16)
        pooled_ref[...] = pooled
        pw = pw_ref[...].reshape(gw, gw)
        mixed = jnp.dot(pooled, pw, preferred_element_type=F32)
        ms_ref[...] = (mixed * sc_ref[...]).astype(BF16)

    blk = pl.BlockSpec((None, SEQ, gw), lambda b, g: (b, 0, g))
    return pl.pallas_call(
        body, name=name, grid=(Bn, 4),
        in_specs=[pl.BlockSpec((None, SEQ, gw), lambda b, g: (b, 0, OFF_U // gw + g)),
                  pl.BlockSpec((NCHIP, 64, gw), lambda b, g: (0, 12 + g, 0)),
                  pl.BlockSpec((1, gw), lambda b, g: (0, g))],
        out_specs=[blk, blk],
        out_shape=[_sds((Bn, SEQ, D), BF16), _sds((Bn, SEQ, D), BF16)],
        compiler_params=_params(2),
    )(z3, g_sc, scale)


def _pool_bwd(name, dms3, pooled3, g_sc, scale, dz3, gg_sc):
    Bn = dms3.shape[0]
    gw = 256
    has_gg = gg_sc is not None

    def body(*refs):
        dms_ref, pooled_ref, pw_ref, sc_ref = refs[:4]
        dz_ref, dpw_ref, dsc_ref = refs[-3:]
        g, b = pl.program_id(0), pl.program_id(1)
        pooled = pooled_ref[...]
        pw = pw_ref[...].reshape(gw, gw)
        dms = dms_ref[...]
        mixed = jnp.dot(pooled, pw, preferred_element_type=F32)
        psc = _rows8(dms * mixed)
        dmixed = (dms * sc_ref[...]).astype(BF16)
        dpw = lax.dot_general(pooled, dmixed, (((0,), (0,)), ((), ())), preferred_element_type=F32)
        dpw = dpw.reshape(NCHIP, 64, gw)

        @pl.when(b == 0)
        def _():
            dsc_ref[...] = psc
            dpw_ref[...] = dpw

        @pl.when(b > 0)
        def _():
            dsc_ref[...] += psc
            dpw_ref[...] += dpw

        dpooled = lax.dot_general(dmixed, pw, (((1,), (1,)), ((), ())), preferred_element_type=F32)
        rows = lax.broadcasted_iota(jnp.int32, dpooled.shape, 0)
        dq = dpooled / _pool_count(g, rows)
        dz_ref[...] = (_pool_window(dq, g, rows, _shift_up) - dpooled).astype(BF16)

    ins = [dms3, pooled3, g_sc, scale, dz3]
    in_specs = [pl.BlockSpec((None, SEQ, gw), lambda g, b: (b, 0, g)),
                pl.BlockSpec((None, SEQ, gw), lambda g, b: (b, 0, g)),
                pl.BlockSpec((NCHIP, 64, gw), lambda g, b: (0, 12 + g, 0)),
                pl.BlockSpec((1, gw), lambda g, b: (0, g)),
                pl.BlockSpec(memory_space=pl.ANY)]
    aliases = {4: 0}
    if has_gg:
        ins.append(gg_sc)
        in_specs.append(pl.BlockSpec(memory_space=pl.ANY))
        aliases[5] = 1
    return pl.pallas_call(
        body, name=name, grid=(4, Bn), in_specs=in_specs,
        out_specs=[pl.BlockSpec((None, SEQ, gw), lambda g, b: (b, 0, OFF_U // gw + g)),
                   pl.BlockSpec((NCHIP, 64, gw), lambda g, b: (0, 12 + g, 0)),
                   pl.BlockSpec((8, gw), lambda g, b: (0, g))],
        out_shape=[_sds(dz3.shape, BF16), _sds((NCHIP, D, 256), F32), _sds((8, D), F32)],
        input_output_aliases=aliases, compiler_params=_params(2),
    )(*ins)


CT = 256
NCT = FF // CT


def _conv_pre(u, cw_ref, cb_ref, rows):
    return (cb_ref[...] + cw_ref[0:1, :] * _shift_down(u, 2, rows) + cw_ref[1:2, :] * _shift_down(u, 1, rows)
            + cw_ref[2:3, :] * u)


def _conv_fwd(name, u3, cw, cb, comm=None):
    Bn = u3.shape[0]

    def body(ug_ref, uv_ref, cwg_ref, cwv_ref, cbg_ref, cbv_ref, a_ref):
        ug, uv = ug_ref[...], uv_ref[...]
        rows = lax.broadcasted_iota(jnp.int32, ug.shape, 0)
        yg = _conv_pre(ug, cwg_ref, cbg_ref, rows)
        yv = _conv_pre(uv, cwv_ref, cbv_ref, rows)
        a_ref[...] = (yg * _sigmoid(yg) * yv).astype(BF16)

    def blk(off):
        return pl.BlockSpec((None, SEQ, CT), lambda b, c: (b, 0, off + c))

    return _call(
        body, name=name, grid=(Bn, NCT),
        in_specs=[blk(0), blk(NCT),
                  pl.BlockSpec((3, CT), lambda b, c: (0, c)), pl.BlockSpec((3, CT), lambda b, c: (0, NCT + c)),
                  pl.BlockSpec((1, CT), lambda b, c: (0, c)), pl.BlockSpec((1, CT), lambda b, c: (0, NCT + c))],
        out_specs=blk(0), out_shape=_sds((Bn, SEQ, FF), BF16), comm=comm,
    )(u3, u3, cw, cw, cb, cb)


def _conv_bwd(name, da3, u3, cw, cb, comm=None):
    Bn = u3.shape[0]
    last = NCT * Bn - 1

    def body(da_ref, ug_ref, uv_ref, cwg_ref, cwv_ref, cbg_ref, cbv_ref,
             du_ref, dcwg_ref, dcwv_ref, dcbg_ref, dcbv_ref, stage_g, stage_v, sems):
        c, b = pl.program_id(0), pl.program_id(1)
        step = c * Bn + b
        ug, uv, da = ug_ref[...], uv_ref[...], da_ref[...].astype(F32)
        rows = lax.broadcasted_iota(jnp.int32, ug.shape, 0)
        yg = _conv_pre(ug, cwg_ref, cbg_ref, rows)
        yv = _conv_pre(uv, cwv_ref, cbv_ref, rows)
        s = _sigmoid(yg)
        dyv = da * (yg * s)
        dyg = da * yv * (s * (1.0 + yg * (1.0 - s)))

        def writes(off_c, stage, sem):
            col = pl.multiple_of(off_c + c * CT, CT)
            return pltpu.make_async_copy(stage, du_ref.at[b, :, pl.ds(col, CT)], sem)

        @pl.when(step > 0)
        def _():
            writes(0, stage_g, sems.at[0]).wait()
            writes(FF, stage_v, sems.at[1]).wait()

        for dy, u, cw_ref, stage, dcw_ref, dcb_ref in ((dyg, ug, cwg_ref, stage_g, dcwg_ref, dcbg_ref),
                                                       (dyv, uv, cwv_ref, stage_v, dcwv_ref, dcbv_ref)):
            d1, d2 = _shift_up(dy, 1, rows), _shift_up(dy, 2, rows)
            stage[...] = (cw_ref[2:3, :] * dy + cw_ref[1:2, :] * d1 + cw_ref[0:1, :] * d2).astype(BF16)
            dcw = jnp.concatenate([jnp.sum(d2 * u, axis=0, keepdims=True), jnp.sum(d1 * u, axis=0, keepdims=True),
                                   jnp.sum(dy * u, axis=0, keepdims=True)], axis=0)
            dcb = jnp.sum(dy, axis=0, keepdims=True)

            @pl.when(b == 0)
            def _():
                dcw_ref[...] = dcw
                dcb_ref[...] = dcb

            @pl.when(b > 0)
            def _():
                dcw_ref[...] += dcw
                dcb_ref[...] += dcb

        writes(0, stage_g, sems.at[0]).start()
        writes(FF, stage_v, sems.at[1]).start()

        @pl.when(step == last)
        def _():
            writes(0, stage_g, sems.at[0]).wait()
            writes(FF, stage_v, sems.at[1]).wait()

    def blk(off):
        return pl.BlockSpec((None, SEQ, CT), lambda c, b: (b, 0, off + c))

    def vec(r, off):
        return pl.BlockSpec((r, CT), lambda c, b: (0, off + c))

    du3, dcwg, dcwv, dcbg, dcbv = _call(
        body, name=name, grid=(NCT, Bn),
        in_specs=[blk(0), blk(0), blk(NCT), vec(3, 0), vec(3, NCT), vec(1, 0), vec(1, NCT)],
        out_specs=[pl.BlockSpec(memory_space=pl.ANY), vec(3, 0), vec(3, 0), vec(1, 0), vec(1, 0)],
        out_shape=[_sds((Bn, SEQ, UW), BF16), _sds((3, FF), F32), _sds((3, FF), F32), _sds((1, FF), F32),
                   _sds((1, FF), F32)],
        scratch_shapes=[pltpu.VMEM((SEQ, CT), BF16)] * 2 + [pltpu.SemaphoreType.DMA((2,))], comm=comm,
    )(da3, u3, u3, cw, cw, cb, cb)
    return du3, jnp.concatenate([dcwg, dcwv], axis=1), jnp.concatenate([dcbg, dcbv], axis=1)


def _rope_tables():
    pos = jnp.arange(SEQ, dtype=F32)
    inv_freq = jnp.exp(jnp.arange(0, ROPE_DIM, 2, dtype=F32) * (-math.log(ROPE_THETA) / ROPE_DIM))
    ang = pos[:, None] * inv_freq[None, :]
    cos, sin = jnp.cos(ang), jnp.sin(ang)
    half = ROPE_DIM // 2
    zeros = jnp.zeros((SEQ, HEAD - ROPE_DIM), F32)
    zh = jnp.zeros((SEQ, half), F32)
    tab_c = jnp.concatenate([cos, cos, zeros + 1.0], axis=1)
    tab_a = jnp.concatenate([-sin, zh, zeros], axis=1)
    tab_b = jnp.concatenate([zh, sin, zeros], axis=1)
    return tab_c, tab_a, tab_b


def _rot(v, tc, ta, tb):
    half = ROPE_DIM // 2
    return v * tc + pltpu.roll(v, HEAD - half, 1) * ta + pltpu.roll(v, half, 1) * tb


def _rot_t(dv, tc, ta, tb):
    half = ROPE_DIM // 2
    return dv * tc + pltpu.roll(dv * ta, half, 1) + pltpu.roll(dv * tb, HEAD - half, 1)


def _band_masks():
    qi = lax.broadcasted_iota(jnp.int32, (HEAD, 2 * HEAD), 0)
    ki = lax.broadcasted_iota(jnp.int32, (HEAD, 2 * HEAD), 1)
    diff = HEAD + qi - ki
    both = (diff >= 0) & (diff <= HEAD)
    q1 = lax.broadcasted_iota(jnp.int32, (HEAD, HEAD), 0)
    k1 = lax.broadcasted_iota(jnp.int32, (HEAD, HEAD), 1)
    return q1 >= k1, both


_NT = (((1,), (1,)), ((), ()))
_TN = (((0,), (0,)), ((), ()))
_SCALE = HEAD ** -0.5


ATT_W = HEAD
ATT_HP = ATT_W // HEAD


def _res_rows(r, n, d, base=0):
    return pl.ds(base * d + r, n, stride=d) if d > 1 else pl.ds(base, n)


def _attn_load(q_ref, k_ref, v_ref, tc_ref, ta_ref, tb_ref, qs, ks, vs, d):
    L = SEQ // d
    for r in range(d):
        rows = _res_rows(r, L, d)
        dst = slice(r * L, (r + 1) * L)
        tc, ta, tb = tc_ref[dst, :], ta_ref[dst, :], tb_ref[dst, :]
        for hh in range(ATT_HP):
            sl = slice(hh * HEAD, (hh + 1) * HEAD)
            qs[dst, sl] = _rot(q_ref[rows, sl], tc, ta, tb).astype(BF16)
            ks[dst, sl] = _rot(k_ref[rows, sl], tc, ta, tb).astype(BF16)
            vs[dst, sl] = v_ref[rows, sl].astype(BF16)


def _attn_fwd(name, z3, tabs, g, d, comm=None):
    Bn = z3.shape[0]
    L = SEQ // d
    nb = L // HEAD
    W, nh = ATT_W, GROUP_W // ATT_W

    def body(q_ref, k_ref, v_ref, tc_ref, ta_ref, tb_ref, o_ref, l_ref, qs, ks, vs, sc, pc):
        m_first, m_both = _band_masks()
        _attn_load(q_ref, k_ref, v_ref, tc_ref, ta_ref, tb_ref, qs, ks, vs, d)
        blocks = [(r, n) for r in range(d) for n in range(nb)]

        def spans(r, n):
            rq = slice(r * L + n * HEAD, r * L + (n + 1) * HEAD)
            rk = slice(r * L + max(n - 1, 0) * HEAD, r * L + (n + 1) * HEAD)
            return rq, rk, slice(0, HEAD if n == 0 else 2 * HEAD)

        for i, (r, n) in enumerate(blocks):
            rq, rk, kc = spans(r, n)
            sc[i, :, kc] = lax.dot_general(qs[rq, :], ks[rk, :], _NT, preferred_element_type=F32)
        for i, (r, n) in enumerate(blocks):
            rq, rk, kc = spans(r, n)
            s = jnp.where(m_first if n == 0 else m_both, sc[i, :, kc] * _SCALE, NEG_INF)
            m = jnp.max(s, axis=-1, keepdims=True)
            e = jnp.exp(s - m)
            den = jnp.sum(e, axis=-1, keepdims=True)
            pc[i, :, kc] = (e * (1.0 / den)).astype(BF16)
            l_ref[_res_rows(r, HEAD, d, n * HEAD), :] = jnp.broadcast_to(m + jnp.log(den), (HEAD, HEAD))
        for i, (r, n) in enumerate(blocks):
            rq, rk, kc = spans(r, n)
            o_ref[_res_rows(r, HEAD, d, n * HEAD), :] = jnp.dot(pc[i, :, kc], vs[rk, :], preferred_element_type=F32)

    def zcol(off):
        return pl.BlockSpec((None, SEQ, W), lambda b, h: (b, 0, (off + g * GROUP_W) // W + h))

    tab = pl.BlockSpec((SEQ, HEAD), lambda b, h: (0, 0))
    out = pl.BlockSpec((None, SEQ, W), lambda b, h: (b, 0, h))
    return _call(
        body, name=name, grid=(Bn, nh),
        in_specs=[zcol(0), zcol(OFF_K), zcol(OFF_V), tab, tab, tab],
        out_specs=[out, out],
        out_shape=[_sds((Bn, SEQ, GROUP_W), F32), _sds((Bn, SEQ, GROUP_W), F32)],
        scratch_shapes=[pltpu.VMEM((SEQ, W), BF16)] * 3
        + [pltpu.VMEM((SEQ // HEAD, HEAD, 2 * HEAD), F32), pltpu.VMEM((SEQ // HEAD, HEAD, 2 * HEAD), BF16)],
        comm=comm,
    )(z3, z3, z3, *tabs)


def _attn_bwd(name, z3, tabs, g, d, do3, lse3, delta3, dz3, comm=None):
    Bn = z3.shape[0]
    L = SEQ // d
    nb = L // HEAD
    W, nh = ATT_W, GROUP_W // ATT_W

    def body(q_ref, k_ref, v_ref, tc_ref, ta_ref, tb_ref, do_ref, l_ref, dl_ref, dz_in, dz_ref,
             qs, ks, vs, dos, dqs, dks, dvs, nat, oq, ok, ov, sc, dpc, pc, dsc, sems):
        b, h = pl.program_id(0), pl.program_id(1)
        m_first, m_both = _band_masks()
        _attn_load(q_ref, k_ref, v_ref, tc_ref, ta_ref, tb_ref, qs, ks, vs, d)
        for r in range(d):
            dos[r * L:(r + 1) * L, :] = do_ref[_res_rows(r, L, d), :].astype(BF16)
        dks[...] = jnp.zeros_like(dks)
        dvs[...] = jnp.zeros_like(dvs)
        blocks = [(r, n) for r in range(d) for n in range(nb)]

        def spans(r, n):
            rq = slice(r * L + n * HEAD, r * L + (n + 1) * HEAD)
            rk = slice(r * L + max(n - 1, 0) * HEAD, r * L + (n + 1) * HEAD)
            return rq, rk, slice(0, HEAD if n == 0 else 2 * HEAD)

        for i, (r, n) in enumerate(blocks):
            rq, rk, kc = spans(r, n)
            sc[i, :, kc] = lax.dot_general(qs[rq, :], ks[rk, :], _NT, preferred_element_type=F32)
            dpc[i, :, kc] = lax.dot_general(dos[rq, :], vs[rk, :], _NT, preferred_element_type=F32)
        for i, (r, n) in enumerate(blocks):
            rq, rk, kc = spans(r, n)
            rows = _res_rows(r, HEAD, d, n * HEAD)
            s = jnp.where(m_first if n == 0 else m_both, sc[i, :, kc] * _SCALE, NEG_INF)
            p = jnp.exp(s - l_ref[rows, :][:, 0:1])
            pc[i, :, kc] = p.astype(BF16)
            dsc[i, :, kc] = (p * (dpc[i, :, kc] - dl_ref[rows, :][:, 0:1]) * _SCALE).astype(BF16)
        for i, (r, n) in enumerate(blocks):
            rq, rk, kc = spans(r, n)
            dqs[rq, :] = jnp.dot(dsc[i, :, kc], ks[rk, :], preferred_element_type=F32)
        for i, (r, n) in enumerate(blocks):
            rq, rk, kc = spans(r, n)
            dks[rk, :] += lax.dot_general(dsc[i, :, kc], qs[rq, :], _TN, preferred_element_type=F32)
            dvs[rk, :] += lax.dot_general(pc[i, :, kc], dos[rq, :], _TN, preferred_element_type=F32)
        step = b * nh + h

        def writes():
            base = g * GROUP_W + h * W
            return [pltpu.make_async_copy(src, dz_ref.at[b, :, pl.ds(pl.multiple_of(base + off, HEAD), W)],
                                          sems.at[i])
                    for i, (src, off) in enumerate(((oq, 0), (ok, OFF_K), (ov, OFF_V)))]

        @pl.when(step > 0)
        def _():
            for cp in writes():
                cp.wait()

        for src, dst, rotate in ((dqs, oq, True), (dks, ok, True), (dvs, ov, False)):
            for r in range(d):
                val = src[r * L:(r + 1) * L, :]
                if rotate:
                    rm = slice(r * L, (r + 1) * L)
                    val = _rot_t(val, tc_ref[rm, :], ta_ref[rm, :], tb_ref[rm, :])
                nat[_res_rows(r, L, d), :] = val
            dst[...] = nat[...].astype(BF16)
        for cp in writes():
            cp.start()

        @pl.when(step == Bn * nh - 1)
        def _():
            for cp in writes():
                cp.wait()

    def zcol(off):
        return pl.BlockSpec((None, SEQ, W), lambda b, h: (b, 0, (off + g * GROUP_W) // W + h))

    tab = pl.BlockSpec((SEQ, HEAD), lambda b, h: (0, 0))
    gcol = pl.BlockSpec((None, SEQ, W), lambda b, h: (b, 0, h))
    any_spec = pl.BlockSpec(memory_space=pl.ANY)
    return _call(
        body, name=name, grid=(Bn, nh),
        in_specs=[zcol(0), zcol(OFF_K), zcol(OFF_V), tab, tab, tab, gcol, gcol, gcol, any_spec],
        out_specs=any_spec,
        out_shape=_sds((Bn, SEQ, ZW), BF16),
        scratch_shapes=[pltpu.VMEM((SEQ, W), BF16)] * 4 + [pltpu.VMEM((SEQ, W), F32)] * 4
        + [pltpu.VMEM((SEQ, W), BF16)] * 3
        + [pltpu.VMEM((SEQ // HEAD, HEAD, 2 * HEAD), F32)] * 2 + [pltpu.VMEM((SEQ // HEAD, HEAD, 2 * HEAD), BF16)] * 2
        + [pltpu.SemaphoreType.DMA((3,))],
        aliases={9: 0}, comm=comm,
    )(z3, z3, z3, *tabs, do3, lse3, delta3, dz3)


def _merge_weights(l0, l1, l2):
    m = jnp.maximum(jnp.maximum(l0, l1), l2)
    e0, e1, e2 = jnp.exp(l0 - m), jnp.exp(l1 - m), jnp.exp(l2 - m)
    inv = 1.0 / (e0 + e1 + e2)
    return e0 * inv, e1 * inv, e2 * inv


def _merge_fwd(name, outs, lses, tr=512):
    T = outs[0].shape[0]

    def body(o0, o1, o2, l0, l1, l2, a_ref):
        w0, w1, w2 = _merge_weights(l0[...], l1[...], l2[...])
        a_ref[...] = (w0 * o0[...] + w1 * o1[...] + w2 * o2[...]).astype(BF16)

    row = pl.BlockSpec((tr, GROUP_W), lambda i: (i, 0))
    return pl.pallas_call(
        body, name=name, grid=(T // tr,), in_specs=[row] * 6, out_specs=row,
        out_shape=_sds((T, GROUP_W), BF16), compiler_params=_params(1),
    )(*outs, *lses)


def _merge_bwd(name, outs, lses, dattn, tr=512):
    T = outs[0].shape[0]

    def body(o0, o1, o2, l0, l1, l2, da_ref, d0, d1, d2, e0, e1, e2):
        w = _merge_weights(l0[...], l1[...], l2[...])
        da = da_ref[...]
        attn = w[0] * o0[...] + w[1] * o1[...] + w[2] * o2[...]
        prod = da * attn
        csum = jnp.concatenate(
            [jnp.broadcast_to(jnp.sum(prod[:, hh * HEAD:(hh + 1) * HEAD], axis=-1, keepdims=True), (tr, HEAD))
             for hh in range(GROUP_W // HEAD)], axis=1)
        for wg, d_ref, e_ref in zip(w, (d0, d1, d2), (e0, e1, e2)):
            d_ref[...] = wg * da
            e_ref[...] = wg * csum

    row = pl.BlockSpec((tr, GROUP_W), lambda i: (i, 0))
    res = pl.pallas_call(
        body, name=name, grid=(T // tr,), in_specs=[row] * 7, out_specs=[row] * 6,
        out_shape=[_sds((T, GROUP_W), F32)] * 6,
        compiler_params=_params(1),
    )(*outs, *lses, dattn)
    return res[:3], res[3:]


def _rms_rows(xv, g):
    r = lax.rsqrt(jnp.mean(xv * xv, axis=-1, keepdims=True) + RMS_EPS)
    return (xv * r * g).astype(BF16)


def _epi_norm(val, x_refs, o_ref, y_refs, first_rows):
    o_ref[...] = val
    y_refs[0][...] = _rms_rows(val, x_refs[0][...])


def _epi_ple(val, x_refs, o_ref, y_refs, first_rows):
    o_ref[...] = val
    xn = x_refs[0][...] + x_refs[1][...] * _sigmoid(val)
    y_refs[0][...] = xn
    if len(y_refs) > 1:
        y_refs[1][...] = _rms_rows(xn, x_refs[2][...])


def _epi_norm_bwd(val, x_refs, o_ref, y_refs, first_rows):
    xv = x_refs[0][...]
    r = lax.rsqrt(jnp.mean(xv * xv, axis=-1, keepdims=True) + RMS_EPS)
    xh = xv * r
    part = _rows8(val * xh)

    @pl.when(first_rows)
    def _():
        y_refs[0][...] = part

    @pl.when(jnp.logical_not(first_rows))
    def _():
        y_refs[0][...] += part

    dxh = val * x_refs[2][...]
    o_ref[...] = x_refs[1][...] + r * (dxh - xh * jnp.mean(dxh * xh, axis=-1, keepdims=True))


def _local_step(x3, p4, tgt3, vecs, ex):
    Bn = x3.shape[0]
    T = Bn * SEQ
    x = x3.reshape(T, D)
    tgt = tgt3.reshape(T, D)
    pb = p4.astype(BF16).reshape(DEPTH, T, PLE)
    tabs = {}
    for d in DILATIONS:
        tabs[d] = [t.reshape(SEQ // d, d, HEAD).transpose(1, 0, 2).reshape(SEQ, HEAD) for t in _rope_tables()]
    tm = 1024 if T % 1024 == 0 else 512
    nt = T // tm
    tk = 1024 if T % 1024 == 0 else 512
    ntk = T // tk
    tm5 = 512
    f32o = lambda n: _sds((T, n), F32)

    def spec(shape, fn):
        return pl.BlockSpec(shape, fn)

    def _mm(name, *args, **kwargs):
        return _mm_call(name, *args, comm=ex.hook(name), **kwargs)

    def cols4(a_ref, b_ref):
        av = a_ref[...]
        return jnp.concatenate([_dot(av, b_ref[j]) for j in range(NCHIP)], axis=1)

    def rows4(a_ref, b_ref):
        av = a_ref[...]
        return jnp.concatenate([_dot(av, b_ref[:, j * 256:(j + 1) * 256], ta=True) for j in range(NCHIP)], axis=0)

    def kchunks4(a_ref, b_ref):
        total = _dot(a_ref[:, 0:256], b_ref[0], tb=True)
        for j in range(1, NCHIP):
            total = total + _dot(a_ref[:, j * 256:(j + 1) * 256], b_ref[j], tb=True)
        return total

    row5 = spec((tm5, D), lambda i, *_: (i, 0))
    rowm = spec((tm, D), lambda i, *_: (i, 0))
    gain = spec((1, D), lambda *_: (0, 0))
    bf_rows = (_sds((T, D), BF16), row5)

    saved = []
    h = _rms_fwd("rms_mix0", x, vecs["g_mix"][0:1])
    for l in range(DEPTH):
        L = str(l)
        G = ex.weights(l)
        g_mix, g_ffn, g_ple = (vecs[k][l:l + 1] for k in ("g_mix", "g_ffn", "g_ple"))
        pscale, cb, cw = vecs["pool_scale"][l:l + 1], vecs["conv_b"][l:l + 1], vecs["conv_w"][l]
        z = _mm("mm_z" + L, h, G["in"], grid=(nt, NCHIP),
                a_spec=spec((tm, D), lambda i, n: (i, 0)),
                b_spec=spec((None, D, IN_S), lambda i, n: (n, 0, 0)),
                o_spec=spec((tm, IN_S), lambda i, n: (i, n)), out_shape=f32o(ZW))
        z3 = z.reshape(Bn, SEQ, ZW)
        outs, lses = [], []
        for g, d in enumerate(DILATIONS):
            o_g, l_g = _attn_fwd("attn_fwd%d_%d" % (g, l), z3, tabs[d], g, d, comm=ex.hook("attn_fwd%d_%d" % (g, l)))
            outs.append(o_g.reshape(T, GROUP_W))
            lses.append(l_g.reshape(T, GROUP_W))
        attn = _merge_fwd("merge_fwd" + L, outs, lses)
        ya = _mm("mm_ya" + L, attn, G["sc"], grid=(nt,), compute=cols4,
                 a_spec=spec((tm, GROUP_W), lambda i: (i, 0)),
                 b_spec=spec((NCHIP, GROUP_W, 256), lambda i: (0, 0, 0)),
                 o_spec=spec((tm, D), lambda i: (i, 0)), out_shape=_sds((T, D), BF16))
        pooled3, ms3 = _pool_fwd("pool_fwd" + L, z3, G["sc"], pscale)
        ms = ms3.reshape(T, D)

        def row_sharded(name, a, rb, res=None, kdim=D, out=F32, **fused):
            if rb is None:
                b_arr, b_spec = G["dn"], spec((NCHIP, DN_S, D), lambda i: (0, 0, 0))
            else:
                b_arr, b_spec = G["r3"], spec((NCHIP, 256, D), lambda i: (0, rb, 0))
            return _mm(name, a, b_arr, grid=(T // tm5,),
                       a_spec=spec((tm5, kdim), lambda i: (i, 0)), b_spec=b_spec,
                       o_spec=row5, out_shape=_sds((T, D), out), res=res, res_spec=None if res is None else row5,
                       **fused)

        yb = row_sharded("mm_yb" + L, ms, 0, out=BF16)
        merged = _gate_fwd("gate_fwd" + L, z, ya, yb)
        x1, h2 = row_sharded("mm_o" + L, merged, 1, res=x, epilogue=_epi_norm, extra_in=[(g_ffn, gain)],
                             extra_out=[bf_rows])
        u = _mm("mm_up" + L, h2, G["up"], grid=(nt, NCHIP),
                a_spec=spec((tm, D), lambda i, n: (i, 0)),
                b_spec=spec((None, D, UP_S), lambda i, n: (n, 0, 0)),
                o_spec=spec((tm, UP_S), lambda i, n: (i, n)), out_shape=f32o(UW))
        u3 = u.reshape(Bn, SEQ, UW)
        act = _conv_fwd("conv_fwd" + L, u3, cw, cb, comm=ex.hook("conv_fwd" + L)).reshape(T, FF)
        x2, h3 = row_sharded("mm_down" + L, act, None, res=x1, kdim=FF, epilogue=_epi_norm,
                             extra_in=[(g_ple, gain)], extra_out=[bf_rows])
        pe = _mm("mm_pe" + L, pb[l], G["sc"], grid=(nt,), compute=cols4,
                 a_spec=spec((tm, PLE), lambda i: (i, 0)),
                 b_spec=spec((NCHIP, 256, 256), lambda i: (0, 2, 0)),
                 o_spec=spec((tm, D), lambda i: (i, 0)), out_shape=f32o(D))
        fused_in = [(x2, row5), (pe, row5)]
        fused_out = [(f32o(D), row5)]
        if l + 1 < DEPTH:
            fused_in.append((vecs["g_mix"][l + 1:l + 2], gain))
            fused_out.append(bf_rows)
        pg, x3n, *h_next = row_sharded("mm_pg" + L, h3, 2, epilogue=_epi_ple, extra_in=fused_in,
                                       extra_out=fused_out)
        saved.append(dict(x=x, h=h, z=z, outs=outs, lses=lses, attn=attn, ya=ya, yb=yb, pooled3=pooled3, ms=ms,
                          merged=merged, x1=x1, h2=h2, u3=u3, act=act, x2=x2, h3=h3, pg=pg, pe=pe))
        x = x3n
        h = h_next[0] if h_next else None

    dx, dg_final8, sq8 = _final_loss(x, vecs["g_final"].reshape(1, D), tgt)

    gg_shape = {k: _sds(G[k].shape, F32) for k in G}
    small = {"g_final": dg_final8}

    for l in reversed(range(DEPTH)):
        L = str(l)
        sv = saved[l]
        G = ex.weights(l)
        GG = dict.fromkeys(_KEYS)
        g_mix, g_ffn, g_ple = (vecs[k][l:l + 1] for k in ("g_mix", "g_ffn", "g_ple"))
        pscale, cb, cw = vecs["pool_scale"][l:l + 1], vecs["conv_b"][l:l + 1], vecs["conv_w"][l]

        def wgrad_rows(name, a, b_arr, key, rb):
            GG[key] = _mm(name, a, b_arr, grid=(2, ntk), ta=True, k_axis=1, nk=ntk, acc_shape=(D, 512),
                          a_spec=spec((tk, D), lambda n, k: (k, 0)),
                          b_spec=spec((tk, 512), lambda n, k: (k, n)),
                          o_spec=spec((NCHIP, 256, 512), lambda n, k: (0, rb, n)),
                          out_shape=gg_shape[key], buf=GG[key])

        def dgrad_rows(name, dy, rb, out=F32, **fused):
            return _mm(name, dy, G["r3"], grid=(T // tm5,), tb=True,
                       a_spec=spec((tm5, D), lambda i: (i, 0)),
                       b_spec=spec((NCHIP, 256, D), lambda i: (0, rb, 0)),
                       o_spec=row5, out_shape=_sds((T, D), out), **fused)

        def norm_bwd(xin, dres, g, rows=row5):
            return dict(epilogue=_epi_norm_bwd, extra_in=[(xin, rows), (dres, rows), (g, gain)],
                        extra_out=[(_sds((8, D), F32), spec((8, D), lambda *_: (0, 0)))])

        dpe, dpg = _ple_bwd("ple_bwd" + L, dx, sv["pe"], sv["pg"])
        GG["sc"] = _mm("wg_ple" + L, pb[l], dpe, grid=(ntk,), compute=rows4, k_axis=0, nk=ntk,
                       acc_shape=(NCHIP * PLE, 256),
                       a_spec=spec((tk, PLE), lambda k: (k, 0)), b_spec=spec((tk, D), lambda k: (k, 0)),
                       o_spec=spec((NCHIP, 256, 256), lambda k: (0, 2, 0)),
                       out_shape=gg_shape["sc"], buf=GG["sc"])
        wgrad_rows("wg_pg" + L, sv["h3"], dpg, "r3", 2)
        dx, small["g_ple" + L] = dgrad_rows("dg_pg" + L, dpg, 2, **norm_bwd(sv["x2"], dx, g_ple))

        da = _mm("dg_down" + L, dx, G["dn"], grid=(T // 256,), tb=True,
                 a_spec=spec((256, D), lambda i: (i, 0)),
                 b_spec=spec((NCHIP, DN_S, D), lambda i: (0, 0, 0)),
                 o_spec=spec((256, FF), lambda i: (i, 0)), out_shape=_sds((T, FF), BF16))
        GG["dn"] = _mm("wg_down" + L, sv["act"], dx, grid=(2, ntk), ta=True, k_axis=1, nk=ntk,
                       acc_shape=(FF, 512),
                       a_spec=spec((tk, FF), lambda n, k: (k, 0)), b_spec=spec((tk, 512), lambda n, k: (k, n)),
                       o_spec=spec((NCHIP, DN_S, 512), lambda n, k: (0, 0, n)),
                       out_shape=gg_shape["dn"], buf=GG["dn"])
        du3, dcw, dcb = _conv_bwd("conv_bwd" + L, da.reshape(Bn, SEQ, FF), sv["u3"], cw, cb,
                                  comm=ex.hook("conv_bwd" + L))
        small["conv_w" + L], small["conv_b" + L] = dcw, dcb
        du = du3.reshape(T, UW)
        dx, small["g_ffn" + L] = _mm(
            "dg_up" + L, du, G["up"], grid=(nt, NCHIP), tb=True, k_axis=1, nk=NCHIP, acc_shape=(tm, D),
            a_spec=spec((tm, UP_S), lambda i, k: (i, k)), b_spec=spec((None, D, UP_S), lambda i, k: (k, 0, 0)),
            o_spec=rowm, out_shape=f32o(D), vmem=VMEM_BIG, **norm_bwd(sv["x1"], dx, g_ffn, rowm))
        GG["up"] = _mm("wg_up" + L, sv["h2"], du, grid=(NCHIP, ntk), ta=True, k_axis=1, nk=ntk,
                       acc_shape=(D, UP_S),
                       a_spec=spec((tk, D), lambda j, k: (k, 0)),
                       b_spec=spec((tk, UP_S), lambda j, k: (k, j)),
                       o_spec=spec((None, D, UP_S), lambda j, k: (j, 0, 0)),
                       out_shape=gg_shape["up"], buf=GG["up"])

        dmerged = dgrad_rows("dg_o" + L, dx, 1, out=BF16)
        wgrad_rows("wg_o" + L, sv["merged"], dx, "r3", 1)
        dya, dz = _gate_bwd("gate_bwd_a" + L, sv["z"], OFF_GA, sv["ya"], dmerged, None)
        dyb, dz = _gate_bwd("gate_bwd_b" + L, sv["z"], OFF_GB, sv["yb"], dmerged, dz)
        dms = dgrad_rows("dg_yb" + L, dyb, 0)
        wgrad_rows("wg_yb" + L, sv["ms"], dyb, "r3", 0)
        dz3, GG["sc"], small["pool_scale" + L] = _pool_bwd(
            "pool_bwd" + L, dms.reshape(Bn, SEQ, D), sv["pooled3"], G["sc"], pscale,
            dz.reshape(Bn, SEQ, ZW), GG["sc"])
        dattn = _mm("dg_ya" + L, dya, G["sc"], grid=(nt,), compute=kchunks4,
                    a_spec=spec((tm, D), lambda i: (i, 0)),
                    b_spec=spec((NCHIP, GROUP_W, 256), lambda i: (0, 0, 0)),
                    o_spec=spec((tm, GROUP_W), lambda i: (i, 0)), out_shape=f32o(GROUP_W))
        GG["sc"] = _mm("wg_ya" + L, sv["attn"], dya, grid=(ntk,), compute=rows4, k_axis=0, nk=ntk,
                       acc_shape=(NCHIP * GROUP_W, 256),
                       a_spec=spec((tk, GROUP_W), lambda k: (k, 0)), b_spec=spec((tk, D), lambda k: (k, 0)),
                       o_spec=spec((NCHIP, GROUP_W, 256), lambda k: (0, 0, 0)),
                       out_shape=gg_shape["sc"], buf=GG["sc"])
        ex.grads_ready(l, {k: GG[k] for k in _KEYS[1:]})
        dos, deltas = _merge_bwd("merge_bwd" + L, sv["outs"], sv["lses"], dattn)
        view3 = lambda t: t.reshape(Bn, SEQ, GROUP_W)
        sz3 = sv["z"].reshape(Bn, SEQ, ZW)
        for g, d in enumerate(DILATIONS):
            dz3 = _attn_bwd("attn_bwd%d_%d" % (g, l), sz3, tabs[d], g, d, view3(dos[g]), view3(sv["lses"][g]),
                            view3(deltas[g]), dz3, comm=ex.hook("attn_bwd%d_%d" % (g, l)))
        dz = dz3.reshape(T, ZW)
        dx_in = dx
        dx, small["g_mix" + L] = _mm(
            "dg_z" + L, dz, G["in"], grid=(nt, NCHIP), tb=True, k_axis=1, nk=NCHIP, acc_shape=(tm, D),
            a_spec=spec((tm, IN_S), lambda i, k: (i, k)), b_spec=spec((None, D, IN_S), lambda i, k: (k, 0, 0)),
            o_spec=rowm, out_shape=f32o(D), vmem=VMEM_BIG, **norm_bwd(sv["x"], dx_in, g_mix, rowm))
        GG["in"] = _mm("wg_z" + L, sv["h"], dz, grid=(NCHIP, ntk), ta=True, k_axis=1, nk=ntk,
                       acc_shape=(D, IN_S),
                       a_spec=spec((tk, D), lambda n, k: (k, 0)), b_spec=spec((tk, IN_S), lambda n, k: (k, n)),
                       o_spec=spec((None, D, IN_S), lambda n, k: (n, 0, 0)),
                       out_shape=gg_shape["in"], buf=GG["in"])
        ex.grads_ready(l, {"in": GG["in"]})

    return sq8, dx.reshape(Bn, SEQ, D), small


_ANY = pl.BlockSpec(memory_space=pl.ANY)


def _place():
    x, y, c = lax.axis_index("x"), lax.axis_index("y"), lax.axis_index("c")
    chips = [(1 - x, y), (x, 1 - y), (1 - x, 1 - y)]
    return x, y, c, 2 * x + y, chips


def _half(rows, cc):
    return pl.ds(cc * (rows // 2), rows // 2)


def _remote(src, dst, send_sems, recv_sems, i, to):
    return pltpu.make_async_remote_copy(src_ref=src, dst_ref=dst, send_sem=send_sems.at[i], recv_sem=recv_sems.at[i],
                                        device_id=to, device_id_type=MESH)


def _exchange_gather_ici(stacks, done):
    n = len(stacks)
    rows = [t.shape[1] for t in stacks]

    def start(refs, send_sems, recv_sems):
        x, y, c, me, chips = _place()
        for k in range(n):
            part = refs[k].at[me, _half(rows[k], c)]
            for j, chip in enumerate(chips):
                _remote(part, part, send_sems, recv_sems, 3 * k + j, (*chip, c)).start()

    def wait(refs, send_sems, recv_sems):
        x, y, c, me, chips = _place()
        for k in range(n):
            for j, chip in enumerate(chips):
                part = refs[k].at[2 * chip[0] + chip[1], _half(rows[k], c)]
                _remote(part, part, send_sems, recv_sems, 3 * k + j, (*chip, c)).wait()

    return dict(arrays=list(stacks), nsem=3 * n, start=start, wait=wait, done=done)


def _exchange_gather_d2d(stacks, done):
    n = len(stacks)
    rows = [t.shape[1] for t in stacks]

    def copies(refs, send_sems, recv_sems, mine):
        x, y, c, me, chips = _place()
        cc = c if mine else 1 - c
        return [_remote(part, part, send_sems, recv_sems, 3 * k + j, (x, y, 1 - c))
                for k in range(n) for j, chip in enumerate(chips)
                for part in [refs[k].at[2 * chip[0] + chip[1], _half(rows[k], cc)]]]

    def start(refs, send_sems, recv_sems):
        for cp in copies(refs, send_sems, recv_sems, True):
            cp.start()

    def wait(refs, send_sems, recv_sems):
        for cp in copies(refs, send_sems, recv_sems, False):
            cp.wait()

    return dict(arrays=list(stacks), nsem=3 * n, start=start, wait=wait, done=done)


def _exchange_halves(g5, recv, done):
    n = len(g5)

    def copies(refs, send_sems, recv_sems):
        x, y, c, me, chips = _place()
        return [_remote(refs[k].at[:, 1 - c], refs[n + k], send_sems, recv_sems, k, (x, y, 1 - c)) for k in range(n)]

    def start(refs, send_sems, recv_sems):
        for cp in copies(refs, send_sems, recv_sems):
            cp.start()

    def wait(refs, send_sems, recv_sems):
        for cp in copies(refs, send_sems, recv_sems):
            cp.wait()

    return dict(arrays=list(g5) + list(recv), nsem=n, start=start, wait=wait, done=done)


def _exchange_chips(parts, landing, done):
    n = len(parts)

    def start(refs, send_sems, recv_sems):
        x, y, c, me, chips = _place()
        for k in range(n):
            for j, chip in enumerate(chips):
                _remote(refs[k].at[2 * chip[0] + chip[1]], refs[n + k].at[me], send_sems, recv_sems, 3 * k + j,
                        (*chip, c)).start()

    def wait(refs, send_sems, recv_sems):
        x, y, c, me, chips = _place()
        for k in range(n):
            for j, chip in enumerate(chips):
                slot = refs[n + k].at[2 * chip[0] + chip[1]]
                _remote(slot, slot, send_sems, recv_sems, 3 * k + j, (*chip, c)).wait()

    return dict(arrays=list(parts) + list(landing), nsem=3 * n, start=start, wait=wait, done=done)


def _exchange_share(full, layer, done):
    n = len(full)

    def copies(refs, send_sems, recv_sems, mine):
        x, y, c, me, chips = _place()
        cc = c if mine else 1 - c
        return [_remote(part, part, send_sems, recv_sems, k, (x, y, 1 - c))
                for k in range(n) for part in [refs[k].at[layer, cc]]]

    def start(refs, send_sems, recv_sems):
        for cp in copies(refs, send_sems, recv_sems, True):
            cp.start()

    def wait(refs, send_sems, recv_sems):
        for cp in copies(refs, send_sems, recv_sems, False):
            cp.wait()

    return dict(arrays=list(full), nsem=n, start=start, wait=wait, done=done)


def _exchange_call(name, comm):
    arrays = comm["arrays"]
    n = len(arrays)

    def body(*refs):
        outs, send_sems, recv_sems = refs[n:2 * n], refs[2 * n], refs[2 * n + 1]
        comm["start"](outs, send_sems, recv_sems)
        comm["wait"](outs, send_sems, recv_sems)

    outs = pl.pallas_call(
        body, name=name, in_specs=[_ANY] * n, out_specs=[_ANY] * n,
        out_shape=[_sds(t.shape, t.dtype) for t in arrays],
        scratch_shapes=[pltpu.SemaphoreType.DMA((comm["nsem"],))] * 2,
        input_output_aliases={i: i for i in range(n)},
    )(*arrays)
    comm["done"](outs)


def _gather_first(stacks, cw4):
    n = len(stacks)
    ici = _exchange_gather_ici(stacks, None)
    d2d = _exchange_gather_d2d(stacks, None)
    rows = [t.shape[1] for t in stacks]

    def body(*refs):
        g_refs, cwg_ref = refs[n + 1:2 * n + 1], refs[2 * n + 1]
        s_ici, r_ici, s_d2d, r_d2d, s_cw, r_cw = refs[2 * n + 2:]
        x, y, c, me, chips = _place()

        def cw_copy(j, slot, chip):
            part = cwg_ref.at[slot]
            return _remote(part, part, s_cw, r_cw, j, (*chip, c))

        ici["start"](g_refs, s_ici, r_ici)
        for j, chip in enumerate(chips):
            cw_copy(j, me, chip).start()
        for k in range(n):
            for j, chip in enumerate(chips):
                part = g_refs[k].at[2 * chip[0] + chip[1], _half(rows[k], c)]
                _remote(part, part, s_ici, r_ici, 3 * k + j, (*chip, c)).wait()
                _remote(part, part, s_d2d, r_d2d, 3 * k + j, (x, y, 1 - c)).start()
        d2d["wait"](g_refs, s_d2d, r_d2d)
        for j, chip in enumerate(chips):
            cw_copy(j, 2 * chip[0] + chip[1], chip).wait()

    outs = pl.pallas_call(
        body, name="gather_first", in_specs=[_ANY] * (n + 1), out_specs=[_ANY] * (n + 1),
        out_shape=[_sds(t.shape, t.dtype) for t in stacks] + [_sds(cw4.shape, cw4.dtype)],
        scratch_shapes=[pltpu.SemaphoreType.DMA((3 * n,))] * 4 + [pltpu.SemaphoreType.DMA((3,))] * 2,
        input_output_aliases={i: i for i in range(n + 1)},
    )(*stacks, cw4)
    return outs[:n], outs[n]


def _halves_and_small(g5, recv, small):
    n = len(g5)
    halves = _exchange_halves(g5, recv, None)

    def body(*refs):
        small_ref = refs[2 * n]
        c_refs, red_ref = refs[2 * n + 1:4 * n + 1], refs[4 * n + 1]
        gath, send_sems, recv_sems, s_send, s_recv = refs[4 * n + 2:]
        x, y, c, me, chips = _place()
        dev = 4 * x + 2 * y + c
        gath[dev] = small_ref[...]
        halves["start"](c_refs, send_sems, recv_sems)
        for r in range(1, 8):
            peer = (x ^ (r >> 2), y ^ ((r >> 1) & 1), c ^ (r & 1))
            _remote(small_ref, gath.at[dev], s_send, s_recv, r - 1, peer).start()
        for r in range(1, 8):
            peer = (x ^ (r >> 2), y ^ ((r >> 1) & 1), c ^ (r & 1))
            src = 4 * peer[0] + 2 * peer[1] + peer[2]
            _remote(small_ref, gath.at[src], s_send, s_recv, r - 1, peer).wait()
        total = gath[0]
        for i in range(1, 8):
            total = total + gath[i]
        red_ref[...] = total
        halves["wait"](c_refs, send_sems, recv_sems)

    vm = pl.BlockSpec(memory_space=pltpu.VMEM)
    arrays = list(g5) + list(recv)
    outs = pl.pallas_call(
        body, name="halves_and_small", in_specs=[_ANY] * (2 * n) + [vm], out_specs=[_ANY] * (2 * n) + [vm],
        out_shape=[_sds(t.shape, t.dtype) for t in arrays] + [_sds(small.shape, F32)],
        scratch_shapes=[pltpu.VMEM((8,) + small.shape, F32), pltpu.SemaphoreType.DMA((n,)),
                        pltpu.SemaphoreType.DMA((n,)), pltpu.SemaphoreType.DMA((7,)), pltpu.SemaphoreType.DMA((7,))],
        input_output_aliases={i: i for i in range(2 * n)},
    )(*arrays, small)
    return outs[:n], outs[n:2 * n], outs[2 * n]


def _row_tile(rh):
    for cand in (512, 384, 352, 256, 128):
        if rh % cand == 0:
            return cand
    return rh


def _add_halves(name, g5, recv, place):
    _, _, rh, cols = g5.shape
    tr = _row_tile(rh)

    def body(place_ref, g_ref, r_ref, o_ref, own_ref):
        val = (g_ref[...] + r_ref[...]).astype(BF16)
        o_ref[...] = val

        @pl.when(pl.program_id(1) == place_ref[1])
        def _():
            own_ref[...] = val

    grid_spec = pltpu.PrefetchScalarGridSpec(
        num_scalar_prefetch=1, grid=(rh // tr, NCHIP),
        in_specs=[pl.BlockSpec((None, None, tr, cols), lambda i, j, pr: (j, pr[0], i, 0)),
                  pl.BlockSpec((None, tr, cols), lambda i, j, pr: (j, i, 0))],
        out_specs=[pl.BlockSpec((None, tr, cols), lambda i, j, pr: (j, i, 0)),
                   pl.BlockSpec((None, tr, cols), lambda i, j, pr: (pr[1], i, 0))])
    return pl.pallas_call(
        body, name=name, grid_spec=grid_spec, out_shape=[_sds(recv.shape, BF16)] * 2, compiler_params=_params(2),
    )(place, g5, recv)


def _sum_chips(name, landing, place, layer, full):
    _, rh, cols = landing.shape
    tr = _row_tile(rh)
    has_full = full is not None

    def body(*refs):
        r_ref, o_ref = refs[1], refs[-1]
        total = r_ref[0].astype(F32)
        for j in range(1, NCHIP):
            total = total + r_ref[j].astype(F32)
        o_ref[...] = total

    grid_spec = pltpu.PrefetchScalarGridSpec(
        num_scalar_prefetch=1, grid=(rh // tr,),
        in_specs=[pl.BlockSpec((NCHIP, tr, cols), lambda i, pr: (0, i, 0))] + ([_ANY] if has_full else []),
        out_specs=pl.BlockSpec((None, None, tr, cols), lambda i, pr: (layer, pr[0], i, 0)))
    return pl.pallas_call(
        body, name=name, grid_spec=grid_spec, out_shape=_sds((DEPTH, 2, rh, cols), F32),
        input_output_aliases={2: 0} if has_full else {}, compiler_params=_params(1),
    )(place, landing, *([full] if has_full else []))


class _Schedule:
    FIRST, REST = ["in"], list(_KEYS[1:])

    def __init__(self, slotted, cw4, place):
        self.place = place
        self._w = [dict(zip(_KEYS, layer)) for layer in slotted]
        got, self.cw4 = _gather_first([self._w[0][k] for k in self.FIRST], cw4)
        self._w[0].update(zip(self.FIRST, got))
        self._g5, self._recv, self._parts, self._landing = [{}, {}], [{}, {}], [{}, {}], [{}, {}]
        self.full = {}
        every = list(_KEYS)
        self._hooks = {
            "mm_z0": lambda: self._gather(_exchange_gather_ici, 0, self.REST),
            "attn_fwd0_0": lambda: self._gather(_exchange_gather_d2d, 0, self.REST),
            "mm_up0": lambda: self._gather(_exchange_gather_ici, 1, self.REST),
            "conv_fwd0": lambda: self._gather(_exchange_gather_ici, 1, self.FIRST),
            "mm_down0": lambda: self._gather(_exchange_gather_d2d, 1, every),
            "dg_down0": lambda: self._halves(1, every),
            "conv_bwd0": lambda: self._chips(1, every),
            "dg_up0": lambda: self._share(1, every),
            "attn_bwd0_0": lambda: self._halves(0, self.REST),
            "dg_z0": lambda: self._chips(0, self.REST),
            "wg_z0": lambda: self._share(0, self.REST),
        }

    def weights(self, layer):
        return self._w[layer]

    def hook(self, name):
        make = self._hooks.get(name)
        return make() if make else None

    def grads_ready(self, layer, GG):
        for k, t in GG.items():
            g5 = t.reshape(NCHIP, 2, t.shape[1] // 2, t.shape[2])
            self._g5[layer][k] = g5
            self._recv[layer][k] = lax.empty((NCHIP,) + g5.shape[2:], F32)

    def _gather(self, make, layer, keys):
        def done(arrays):
            self._w[layer].update(zip(keys, arrays))
        return make([self._w[layer][k] for k in keys], done)

    def _halves(self, layer, keys):
        return _exchange_halves([self._g5[layer][k] for k in keys], [self._recv[layer][k] for k in keys],
                                lambda arrays: self._halves_done(layer, keys, arrays))

    def _halves_done(self, layer, keys, arrays):
        n = len(keys)
        for k, g, r in zip(keys, arrays[:n], arrays[n:]):
            self._parts[layer][k], self._landing[layer][k] = _add_halves(
                "add_halves%d_%s" % (layer, k), g, r, self.place)

    def _chips(self, layer, keys):
        return _exchange_chips([self._parts[layer][k] for k in keys], [self._landing[layer][k] for k in keys],
                               lambda arrays: self._chips_done(layer, keys, arrays))

    def _chips_done(self, layer, keys, arrays):
        for k, t in zip(keys, arrays[len(keys):]):
            self.full[k] = _sum_chips("sum_chips%d_%s" % (layer, k), t, self.place, layer, self.full.get(k))

    def _share(self, layer, keys):
        def done(arrays):
            self.full.update(zip(keys, arrays))
        return _exchange_share([self.full[k] for k in keys], layer, done)

    def finish(self, small):
        keys = self.FIRST
        g5, recv, small_red = _halves_and_small([self._g5[0][k] for k in keys], [self._recv[0][k] for k in keys], small)
        self._halves_done(0, keys, list(g5) + list(recv))
        _exchange_call("exchange_chips_last", self._chips(0, keys))
        _exchange_call("share_halves_last", self._share(0, keys))
        return [self.full[k] for k in _KEYS], small_red


def _adamw(name, w, g, m, v):
    shape = w.shape
    cols = shape[-1]
    rows = 1
    for s in shape[:-1]:
        rows *= s
    tr = rows
    for cand in (256, 128, 64):
        if rows > cand and rows % cand == 0:
            tr = cand
            break
    c1 = 1.0 / (1.0 - B1 ** STEP)
    c2 = 1.0 / (1.0 - B2 ** STEP)

    def body(w_ref, g_ref, m_ref, v_ref, d_ref, nm_ref, nv_ref):
        gv = g_ref[...]
        nm = B1 * m_ref[...] + (1.0 - B1) * gv
        nv = B2 * v_ref[...] + (1.0 - B2) * (gv * gv)
        nm_ref[...] = nm
        nv_ref[...] = nv
        d_ref[...] = -LR * ((nm * c1) / (jnp.sqrt(nv * c2) + ADAM_EPS) + WD * w_ref[...])

    blk = pl.BlockSpec((tr, cols), lambda i: (i, 0))
    outs = pl.pallas_call(
        body, name=name, grid=(rows // tr,), in_specs=[blk] * 4, out_specs=[blk] * 3,
        out_shape=[_sds((rows, cols), F32)] * 3, compiler_params=_params(1),
    )(*(t.reshape(rows, cols) for t in (w, g, m, v)))
    return tuple(o.reshape(shape) for o in outs)


def _pack_small(small):
    rows = [jnp.sum(small["g_mix%d" % l], axis=0, keepdims=True) for l in range(DEPTH)]
    rows += [jnp.sum(small["pool_scale%d" % l], axis=0, keepdims=True) for l in range(DEPTH)]
    rows += [jnp.sum(small["g_ffn%d" % l], axis=0, keepdims=True) for l in range(DEPTH)]
    rows += [jnp.sum(small["g_ple%d" % l], axis=0, keepdims=True) for l in range(DEPTH)]
    rows += [jnp.sum(small["g_final"], axis=0, keepdims=True)]
    flat = [small["conv_b%d" % l].reshape(-1) for l in range(DEPTH)]
    flat += [small["conv_w%d" % l].reshape(-1) for l in range(DEPTH)]
    flat = jnp.concatenate(flat).reshape(-1, D)
    packed = jnp.concatenate(rows + [flat], axis=0)
    return jnp.pad(packed, ((0, SMALL_ROWS - packed.shape[0]), (0, 0)))


def _unpack_small(red):
    g_mix, pool_scale, g_ffn, g_ple = red[0:2], red[2:4], red[4:6], red[6:8]
    g_final = red[8]
    nb = DEPTH * UW // D
    conv_b = red[9:9 + nb].reshape(DEPTH, UW)
    conv_w = red[9 + nb:9 + 4 * nb].reshape(DEPTH, 3, UW)
    return g_mix, pool_scale, g_ffn, g_ple, g_final, conv_b, conv_w


def kernel(x, p, g_mix, w_in, w_ya, w_yb, pool_w, pool_scale, w_o, g_ffn, w_up, conv_w, conv_b, w_down, g_ple, w_ple, w_ple_gate, g_final, loss_target, m_g_mix, m_w_in, m_w_ya, m_w_yb, m_pool_w, m_pool_scale, m_w_o, m_g_ffn, m_w_up, m_conv_w, m_conv_b, m_w_down, m_g_ple, m_w_ple, m_w_ple_gate, m_g_final, v_g_mix, v_w_in, v_w_ya, v_w_yb, v_pool_w, v_pool_scale, v_w_o, v_g_ffn, v_w_up, v_conv_w, v_conv_b, v_w_down, v_g_ple, v_w_ple, v_w_ple_gate, v_g_final):
    me = 2 * lax.axis_index("x") + lax.axis_index("y")
    place = jnp.stack([lax.axis_index("c"), me]).astype(jnp.int32)

    def slot(shard):
        return lax.dynamic_update_index_in_dim(lax.empty((NCHIP,) + shard.shape, shard.dtype), shard, me, 0)

    packed = [
        w_in.astype(BF16), w_up.astype(BF16),
        jnp.concatenate([w_ya, w_ple, pool_w.reshape(DEPTH, 256, 256)], axis=1).astype(BF16),
        jnp.concatenate([w_yb, w_o, w_ple_gate], axis=1).astype(BF16),
        w_down.astype(BF16),
    ]
    slotted = [[slot(t[l]) for t in packed] for l in range(DEPTH)]
    ex = _Schedule(slotted, slot(conv_w.reshape(DEPTH * 3, UP_S)), place)
    cw_full = ex.cw4.reshape(NCHIP, DEPTH, 3, UP_S).transpose(1, 2, 0, 3).reshape(DEPTH, 3, UW)

    vecs = dict(g_mix=g_mix, pool_scale=pool_scale, g_ffn=g_ffn, g_ple=g_ple, g_final=g_final, conv_b=conv_b,
                conv_w=cw_full)
    sq8, grad_x, small = _local_step(x, p, loss_target, vecs, ex)
    loss = lax.psum(jnp.sum(sq8) * (0.5 / D), ("x", "y", "c"))

    full, small_red = ex.finish(_pack_small(small))
    r_in, r_up, r_sc, r_r3, r_dn = [f.reshape(DEPTH, -1, f.shape[-1]) for f in full]
    d_g_mix, d_pool_scale, d_g_ffn, d_g_ple, d_g_final, d_conv_b, d_conv_w_full = _unpack_small(small_red)
    d_conv_w = lax.dynamic_slice_in_dim(d_conv_w_full, me * UP_S, UP_S, axis=2)

    grads = dict(
        g_mix=d_g_mix, w_in=r_in, w_ya=r_sc[:, 0:512], w_yb=r_r3[:, 0:256],
        pool_w=r_sc[:, 768:1024].reshape(DEPTH, 4, 64, 256), pool_scale=d_pool_scale, w_o=r_r3[:, 256:512],
        g_ffn=d_g_ffn, w_up=r_up, conv_w=d_conv_w, conv_b=d_conv_b, w_down=r_dn, g_ple=d_g_ple,
        w_ple=r_sc[:, 512:768], w_ple_gate=r_r3[:, 512:768], g_final=d_g_final)
    weights = dict(g_mix=g_mix, w_in=w_in, w_ya=w_ya, w_yb=w_yb, pool_w=pool_w, pool_scale=pool_scale, w_o=w_o,
                   g_ffn=g_ffn, w_up=w_up, conv_w=conv_w, conv_b=conv_b, w_down=w_down, g_ple=g_ple, w_ple=w_ple,
                   w_ple_gate=w_ple_gate, g_final=g_final)
    m_in = dict(g_mix=m_g_mix, w_in=m_w_in, w_ya=m_w_ya, w_yb=m_w_yb, pool_w=m_pool_w, pool_scale=m_pool_scale,
                w_o=m_w_o, g_ffn=m_g_ffn, w_up=m_w_up, conv_w=m_conv_w, conv_b=m_conv_b, w_down=m_w_down,
                g_ple=m_g_ple, w_ple=m_w_ple, w_ple_gate=m_w_ple_gate, g_final=m_g_final)
    v_in = dict(g_mix=v_g_mix, w_in=v_w_in, w_ya=v_w_ya, w_yb=v_w_yb, pool_w=v_pool_w, pool_scale=v_pool_scale,
                w_o=v_w_o, g_ffn=v_g_ffn, w_up=v_w_up, conv_w=v_conv_w, conv_b=v_conv_b, w_down=v_w_down,
                g_ple=v_g_ple, w_ple=v_w_ple, w_ple_gate=v_w_ple_gate, g_final=v_g_final)
    names = ["g_mix", "w_in", "w_ya", "w_yb", "pool_w", "pool_scale", "w_o", "g_ffn", "w_up", "conv_w", "conv_b",
             "w_down", "g_ple", "w_ple", "w_ple_gate", "g_final"]
    deltas, new_m, new_v = [], [], []
    for nme in names:
        gr = grads[nme].reshape(weights[nme].shape)
        grads[nme] = gr
        dlt, nm, nv = _adamw("adamw_" + nme, weights[nme], gr, m_in[nme], v_in[nme])
        deltas.append(dlt)
        new_m.append(nm)
        new_v.append(nv)
    return (loss, grad_x, *[grads[nme] for nme in names], *deltas, *new_m, *new_v)
```

```python
import math

import jax
import jax.numpy as jnp
from jax import lax
from jax.experimental import pallas as pl
from jax.experimental.pallas import tpu as pltpu

F32 = jnp.float32
BF16 = jnp.bfloat16
_KEYS = ("in", "up", "sc", "r3", "dn")
MESH = pl.DeviceIdType.MESH

D = 1024
SEQ = 2048
DEPTH = 2
HEAD = 128
GROUP_W = 512
DILATIONS = (1, 4, 16)
ROPE_DIM = 32
ROPE_THETA = 500000.0
NEG_INF = -1e30
ZW = 7680
OFF_K, OFF_V, OFF_U, OFF_GA, OFF_GB = 1536, 3072, 4608, 5632, 6656
FF = 2816
UW = 2 * FF
PLE = 256
NCHIP = 4
IN_S, UP_S, DN_S = ZW // NCHIP, UW // NCHIP, FF // NCHIP
RMS_EPS = 1e-6
LR, B1, B2, ADAM_EPS, WD, STEP = 0.001, 0.9, 0.999, 1e-08, 0.01, 10
SMALL_ROWS = 56
VMEM_CAP = 48 * 1024 * 1024
VMEM_BIG = 58 * 1024 * 1024


def _params(n_grid, vmem=VMEM_CAP):
    return pltpu.CompilerParams(dimension_semantics=("arbitrary",) * n_grid, vmem_limit_bytes=vmem)


def _sigmoid(v):
    return 1.0 / (1.0 + jnp.exp(-v))


def _rows8(v):
    return jnp.sum(v.reshape(v.shape[0] // 8, 8, v.shape[1]), axis=0)


def _sds(shape, dtype):
    return jax.ShapeDtypeStruct(shape, dtype)


def _dot(av, bv, ta=False, tb=False):
    dims = (((0,) if ta else (1,), (1,) if tb else (0,)), ((), ()))
    return lax.dot_general(av.astype(BF16), bv.astype(BF16), dims, preferred_element_type=F32)


def _call(body, *, name, grid, in_specs, out_specs, out_shape, scratch_shapes=(), aliases=None, comm=None,
          vmem=VMEM_CAP):
    params = _params(len(grid), vmem)
    aliases = dict(aliases or {})
    if comm is None:
        return pl.pallas_call(body, name=name, grid=grid, in_specs=list(in_specs), out_specs=out_specs,
                              out_shape=out_shape, scratch_shapes=list(scratch_shapes),
                              input_output_aliases=aliases, compiler_params=params)
    single = not isinstance(out_shape, (list, tuple))
    out_specs_l = [out_specs] if single else list(out_specs)
    out_shape_l = [out_shape] if single else list(out_shape)
    n_in, n_out, n_c = len(in_specs), len(out_shape_l), len(comm["arrays"])

    def hosted(*refs):
        core_in, core_out = refs[:n_in], refs[n_in + n_c:n_in + n_c + n_out]
        c_refs = refs[n_in + n_c + n_out:n_in + 2 * n_c + n_out]
        scratch, (send_sems, recv_sems) = refs[n_in + 2 * n_c + n_out:-2], refs[-2:]
        ids = [pl.program_id(i) for i in range(len(grid))]
        first, last = ids[0] == 0, ids[0] == grid[0] - 1
        for i in range(1, len(grid)):
            first = jnp.logical_and(first, ids[i] == 0)
            last = jnp.logical_and(last, ids[i] == grid[i] - 1)

        @pl.when(first)
        def _():
            comm["start"](c_refs, send_sems, recv_sems)

        body(*core_in, *core_out, *scratch)

        @pl.when(last)
        def _():
            comm["wait"](c_refs, send_sems, recv_sems)

    any_spec = pl.BlockSpec(memory_space=pl.ANY)
    for i in range(n_c):
        aliases[n_in + i] = n_out + i
    call = pl.pallas_call(
        hosted, name=name, grid=grid, in_specs=list(in_specs) + [any_spec] * n_c,
        out_specs=out_specs_l + [any_spec] * n_c,
        out_shape=out_shape_l + [_sds(t.shape, t.dtype) for t in comm["arrays"]],
        scratch_shapes=list(scratch_shapes) + [pltpu.SemaphoreType.DMA((comm["nsem"],))] * 2,
        input_output_aliases=aliases, compiler_params=params)

    def run(*args):
        outs = call(*args, *comm["arrays"])
        comm["done"](outs[n_out:])
        return outs[0] if single else outs[:n_out]

    return run


def _mm_call(name, a, b, *, grid, a_spec, b_spec, o_spec, out_shape, ta=False, tb=False, k_axis=None, nk=1,
             acc_shape=None, res=None, res_spec=None, buf=None, compute=None, comm=None,
             extra_in=(), extra_out=(), epilogue=None, vmem=VMEM_CAP):
    has_res, has_buf = res is not None, buf is not None
    in_place = nk > 1 and not has_res and out_shape.dtype == F32 and epilogue is None
    n_xi, n_xo = len(extra_in), len(extra_out)

    def body(*refs):
        a_ref, b_ref = refs[0], refs[1]
        pos = 2
        r_ref = None
        if has_res:
            r_ref = refs[pos]
            pos += 1
        if has_buf:
            pos += 1
        x_refs = refs[pos:pos + n_xi]
        pos += n_xi
        first_rows = pl.program_id(0) == 0
        o_ref = refs[pos]
        y_refs = refs[pos + 1:pos + 1 + n_xo]
        if compute is None:
            av = a_ref[...]
            bv = b_ref[...]
            part = _dot(av.reshape(-1, av.shape[-1]), bv.reshape(-1, bv.shape[-1]), ta, tb)
        else:
            part = compute(a_ref, b_ref)

        def finish(val):
            if r_ref is not None:
                val = val + r_ref[...]
            if epilogue is not None:
                epilogue(val, x_refs, o_ref, y_refs, first_rows)
            else:
                o_ref[...] = val.reshape(o_ref.shape).astype(o_ref.dtype)

        if nk == 1:
            finish(part)
        elif in_place:
            @pl.when(pl.program_id(k_axis) == 0)
            def _():
                o_ref[...] = jnp.zeros(o_ref.shape, F32)

            o_ref[...] += part.reshape(o_ref.shape)
        else:
            acc_ref = refs[pos + 1 + n_xo]
            k = pl.program_id(k_axis)

            @pl.when(k == 0)
            def _():
                acc_ref[...] = jnp.zeros(acc_ref.shape, F32)

            acc_ref[...] += part

            @pl.when(k == nk - 1)
            def _():
                finish(acc_ref[...])

    ins, in_specs = [a, b], [a_spec, b_spec]
    if has_res:
        ins.append(res)
        in_specs.append(res_spec)
    aliases = {}
    if has_buf:
        aliases = {len(ins): 0}
        ins.append(buf)
        in_specs.append(pl.BlockSpec(memory_space=pl.ANY))
    for arr, sp in extra_in:
        ins.append(arr)
        in_specs.append(sp)
    scratch = [pltpu.VMEM(acc_shape, F32)] if nk > 1 and not in_place else []
    if not extra_out:
        return _call(body, name=name, grid=grid, in_specs=in_specs, out_specs=o_spec, out_shape=out_shape,
                     scratch_shapes=scratch, aliases=aliases, comm=comm, vmem=vmem)(*ins)
    return _call(body, name=name, grid=grid, in_specs=in_specs, out_specs=[o_spec] + [sp for _, sp in extra_out],
                 out_shape=[out_shape] + [sh for sh, _ in extra_out], scratch_shapes=scratch, aliases=aliases,
                 comm=comm, vmem=vmem)(*ins)


def _rms_fwd(name, x, g, tr=512):
    T = x.shape[0]

    def body(x_ref, g_ref, h_ref):
        xv = x_ref[...]
        r = lax.rsqrt(jnp.mean(xv * xv, axis=-1, keepdims=True) + RMS_EPS)
        h_ref[...] = (xv * r * g_ref[...]).astype(BF16)

    return pl.pallas_call(
        body, name=name, grid=(T // tr,),
        in_specs=[pl.BlockSpec((tr, D), lambda i: (i, 0)), pl.BlockSpec((1, D), lambda i: (0, 0))],
        out_specs=pl.BlockSpec((tr, D), lambda i: (i, 0)), out_shape=_sds((T, D), BF16),
        compiler_params=_params(1),
    )(x, g)


def _final_loss(x, g, tgt, tr=512):
    T = x.shape[0]

    def body(x_ref, g_ref, t_ref, dx_ref, dg_ref, sq_ref):
        xv = x_ref[...]
        r = lax.rsqrt(jnp.mean(xv * xv, axis=-1, keepdims=True) + RMS_EPS)
        xh = xv * r
        gv = g_ref[...]
        e = xh * gv - t_ref[...]
        dy = e * (1.0 / D)
        pg = _rows8(dy * xh)
        ps = _rows8(e * e)

        @pl.when(pl.program_id(0) == 0)
        def _():
            dg_ref[...] = pg
            sq_ref[...] = ps

        @pl.when(pl.program_id(0) > 0)
        def _():
            dg_ref[...] += pg
            sq_ref[...] += ps

        dxh = dy * gv
        dx_ref[...] = r * (dxh - xh * jnp.mean(dxh * xh, axis=-1, keepdims=True))

    row = pl.BlockSpec((tr, D), lambda i: (i, 0))
    acc = pl.BlockSpec((8, D), lambda i: (0, 0))
    return pl.pallas_call(
        body, name="final_loss", grid=(T // tr,),
        in_specs=[row, pl.BlockSpec((1, D), lambda i: (0, 0)), row],
        out_specs=[row, acc, acc],
        out_shape=[_sds((T, D), F32), _sds((8, D), F32), _sds((8, D), F32)],
        compiler_params=_params(1),
    )(x, g, tgt)


def _ple_bwd(name, dx, pe, pg, tr=512):
    T = dx.shape[0]

    def body(dx_ref, pe_ref, pg_ref, dpe_ref, dpg_ref):
        s = _sigmoid(pg_ref[...])
        dxv = dx_ref[...]
        dpe_ref[...] = (dxv * s).astype(BF16)
        dpg_ref[...] = (dxv * pe_ref[...] * s * (1.0 - s)).astype(BF16)

    row = pl.BlockSpec((tr, D), lambda i: (i, 0))
    return pl.pallas_call(
        body, name=name, grid=(T // tr,), in_specs=[row, row, row], out_specs=[row, row],
        out_shape=[_sds((T, D), BF16), _sds((T, D), BF16)], compiler_params=_params(1),
    )(dx, pe, pg)


def _gate_fwd(name, z, ya, yb, tr=512):
    T = z.shape[0]
    w = 512

    def body(ga_ref, gb_ref, ya_ref, yb_ref, o_ref):
        o_ref[...] = (_sigmoid(ga_ref[...]) * ya_ref[...].astype(F32)
                      + _sigmoid(gb_ref[...]) * yb_ref[...].astype(F32)).astype(BF16)

    col = pl.BlockSpec((tr, w), lambda i, j: (i, j))
    return pl.pallas_call(
        body, name=name, grid=(T // tr, D // w),
        in_specs=[pl.BlockSpec((tr, w), lambda i, j: (i, OFF_GA // w + j)),
                  pl.BlockSpec((tr, w), lambda i, j: (i, OFF_GB // w + j)), col, col],
        out_specs=col, out_shape=_sds((T, D), BF16), compiler_params=_params(2),
    )(z, z, ya, yb)


def _gate_bwd(name, z, off, y, dm, dz, tr=512):
    T = z.shape[0]
    w = 512
    has_dz = dz is not None

    def body(*refs):
        g_ref, y_ref, dm_ref = refs[:3]
        dy_ref, dz_ref = refs[-2:]
        s = _sigmoid(g_ref[...])
        dmv = dm_ref[...].astype(F32)
        dy_ref[...] = (dmv * s).astype(BF16)
        dz_ref[...] = (dmv * y_ref[...].astype(F32) * s * (1.0 - s)).astype(BF16)

    col = pl.BlockSpec((tr, w), lambda i, j: (i, j))
    gcol = pl.BlockSpec((tr, w), lambda i, j: (i, off // w + j))
    ins, in_specs, aliases = [z, y, dm], [gcol, col, col], {}
    if has_dz:
        ins.append(dz)
        in_specs.append(pl.BlockSpec(memory_space=pl.ANY))
        aliases = {3: 1}
    return pl.pallas_call(
        body, name=name, grid=(T // tr, D // w), in_specs=in_specs, out_specs=[col, gcol],
        out_shape=[_sds((T, D), BF16), _sds((T, ZW), BF16)], input_output_aliases=aliases,
        compiler_params=_params(2),
    )(*ins)


def _shift_down(v, k, rows):
    return jnp.where(rows >= k, pltpu.roll(v, k, 0), 0.0)


def _shift_up(v, k, rows):
    n = v.shape[0]
    return jnp.where(rows < n - k, pltpu.roll(v, n - k, 0), 0.0)


def _pool_window(v, g, rows, shift):
    s2 = v + shift(v, 1, rows)
    s4 = s2 + shift(s2, 2, rows)
    s8 = s4 + shift(s4, 4, rows)
    s16 = s8 + shift(s8, 8, rows)
    return jnp.where(g == 0, s2, jnp.where(g == 1, s4, jnp.where(g == 2, s8, s16)))


def _pool_count(g, rows):
    wlen = jnp.left_shift(2, g).astype(F32)
    return jnp.minimum(rows.astype(F32) + 1.0, wlen)


def _pool_fwd(name, z3, g_sc, scale):
    Bn = z3.shape[0]
    gw = 256

    def body(u_ref, pw_ref, sc_ref, pooled_ref, ms_ref):
        g = pl.program_id(1)
        u = u_ref[...]
        rows = lax.broadcasted_iota(jnp.int32, u.shape, 0)
        pooled = (_pool_window(u, g, rows, _shift_down) / _pool_count(g, rows) - u).astype(BF16)
        pooled_ref[...] = pooled
        pw = pw_ref[...].reshape(gw, gw)
        mixed = jnp.dot(pooled, pw, preferred_element_type=F32)
        ms_ref[...] = (mixed * sc_ref[...]).astype(BF16)

    blk = pl.BlockSpec((None, SEQ, gw), lambda b, g: (b, 0, g))
    return pl.pallas_call(
        body, name=name, grid=(Bn, 4),
        in_specs=[pl.BlockSpec((None, SEQ, gw), lambda b, g: (b, 0, OFF_U // gw + g)),
                  pl.BlockSpec((NCHIP, 64, gw), lambda b, g: (0, 12 + g, 0)),
                  pl.BlockSpec((1, gw), lambda b, g: (0, g))],
        out_specs=[blk, blk],
        out_shape=[_sds((Bn, SEQ, D), BF16), _sds((Bn, SEQ, D), BF16)],
        compiler_params=_params(2),
    )(z3, g_sc, scale)


def _pool_bwd(name, dms3, pooled3, g_sc, scale, dz3, gg_sc):
    Bn = dms3.shape[0]
    gw = 256
    has_gg = gg_sc is not None

    def body(*refs):
        dms_ref, pooled_ref, pw_ref, sc_ref = refs[:4]
        dz_ref, dpw_ref, dsc_ref = refs[-3:]
        g, b = pl.program_id(0), pl.program_id(1)
        pooled = pooled_ref[...]
        pw = pw_ref[...].reshape(gw, gw)
        dms = dms_ref[...]
        mixed = jnp.dot(pooled, pw, preferred_element_type=F32)
        psc = _rows8(dms * mixed)
        dmixed = (dms * sc_ref[...]).astype(BF16)
        dpw = lax.dot_general(pooled, dmixed, (((0,), (0,)), ((), ())), preferred_element_type=F32)
        dpw = dpw.reshape(NCHIP, 64, gw)

        @pl.when(b == 0)
        def _():
            dsc_ref[...] = psc
            dpw_ref[...] = dpw

        @pl.when(b > 0)
        def _():
            dsc_ref[...] += psc
            dpw_ref[...] += dpw

        dpooled = lax.dot_general(dmixed, pw, (((1,), (1,)), ((), ())), preferred_element_type=F32)
        rows = lax.broadcasted_iota(jnp.int32, dpooled.shape, 0)
        dq = dpooled / _pool_count(g, rows)
        dz_ref[...] = (_pool_window(dq, g, rows, _shift_up) - dpooled).astype(BF16)

    ins = [dms3, pooled3, g_sc, scale, dz3]
    in_specs = [pl.BlockSpec((None, SEQ, gw), lambda g, b: (b, 0, g)),
                pl.BlockSpec((None, SEQ, gw), lambda g, b: (b, 0, g)),
                pl.BlockSpec((NCHIP, 64, gw), lambda g, b: (0, 12 + g, 0)),
                pl.BlockSpec((1, gw), lambda g, b: (0, g)),
                pl.BlockSpec(memory_space=pl.ANY)]
    aliases = {4: 0}
    if has_gg:
        ins.append(gg_sc)
        in_specs.append(pl.BlockSpec(memory_space=pl.ANY))
        aliases[5] = 1
    return pl.pallas_call(
        body, name=name, grid=(4, Bn), in_specs=in_specs,
        out_specs=[pl.BlockSpec((None, SEQ, gw), lambda g, b: (b, 0, OFF_U // gw + g)),
                   pl.BlockSpec((NCHIP, 64, gw), lambda g, b: (0, 12 + g, 0)),
                   pl.BlockSpec((8, gw), lambda g, b: (0, g))],
        out_shape=[_sds(dz3.shape, BF16), _sds((NCHIP, D, 256), F32), _sds((8, D), F32)],
        input_output_aliases=aliases, compiler_params=_params(2),
    )(*ins)


CT = 256
NCT = FF // CT


def _conv_pre(u, cw_ref, cb_ref, rows):
    return (cb_ref[...] + cw_ref[0:1, :] * _shift_down(u, 2, rows) + cw_ref[1:2, :] * _shift_down(u, 1, rows)
            + cw_ref[2:3, :] * u)


def _conv_fwd(name, u3, cw, cb, comm=None):
    Bn = u3.shape[0]

    def body(ug_ref, uv_ref, cwg_ref, cwv_ref, cbg_ref, cbv_ref, a_ref):
        ug, uv = ug_ref[...], uv_ref[...]
        rows = lax.broadcasted_iota(jnp.int32, ug.shape, 0)
        yg = _conv_pre(ug, cwg_ref, cbg_ref, rows)
        yv = _conv_pre(uv, cwv_ref, cbv_ref, rows)
        a_ref[...] = (yg * _sigmoid(yg) * yv).astype(BF16)

    def blk(off):
        return pl.BlockSpec((None, SEQ, CT), lambda b, c: (b, 0, off + c))

    return _call(
        body, name=name, grid=(Bn, NCT),
        in_specs=[blk(0), blk(NCT),
                  pl.BlockSpec((3, CT), lambda b, c: (0, c)), pl.BlockSpec((3, CT), lambda b, c: (0, NCT + c)),
                  pl.BlockSpec((1, CT), lambda b, c: (0, c)), pl.BlockSpec((1, CT), lambda b, c: (0, NCT + c))],
        out_specs=blk(0), out_shape=_sds((Bn, SEQ, FF), BF16), comm=comm,
    )(u3, u3, cw, cw, cb, cb)


def _conv_bwd(name, da3, u3, cw, cb, comm=None):
    Bn = u3.shape[0]
    last = NCT * Bn - 1
    R = 128

    def body(da_ref, ug_ref, uv_ref, cwg_ref, cwv_ref, cbg_ref, cbv_ref,
             du_ref, dcwg_ref, dcwv_ref, dcbg_ref, dcbv_ref, stage_g, stage_v, ugp, uvp, dap, sems):
        c, b = pl.program_id(0), pl.program_id(1)
        step = c * Bn + b
        zeros8 = jnp.zeros((8, CT), F32)
        for pad, ref in ((ugp, ug_ref), (uvp, uv_ref)):
            pad[0:8, :] = zeros8
            pad[8:SEQ + 8, :] = ref[...]
            pad[SEQ + 8:SEQ + 16, :] = zeros8
        dap[0:SEQ, :] = da_ref[...].astype(F32)
        dap[SEQ:SEQ + 8, :] = zeros8

        def writes(off_c, stage, sem):
            col = pl.multiple_of(off_c + c * CT, CT)
            return pltpu.make_async_copy(stage, du_ref.at[b, :, pl.ds(col, CT)], sem)

        @pl.when(step > 0)
        def _():
            writes(0, stage_g, sems.at[0]).wait()
            writes(FF, stage_v, sems.at[1]).wait()

        cwg, cwv, cbg, cbv = cwg_ref[...], cwv_ref[...], cbg_ref[...], cbv_ref[...]

        def pre(x, cwt, cbt):
            y = cbt + cwt[0:1, :] * pltpu.roll(x, 2, 0) + cwt[1:2, :] * pltpu.roll(x, 1, 0) + cwt[2:3, :] * x
            return y[8:R + 16, :]

        def chunk(i, acc):
            c0 = pl.multiple_of(i * R, R)
            xg, xv = ugp[pl.ds(c0, R + 16), :], uvp[pl.ds(c0, R + 16), :]
            yg, yv = pre(xg, cwg, cbg), pre(xv, cwv, cbv)
            da = dap[pl.ds(c0, R + 8), :]
            s = _sigmoid(yg)
            dyv = da * (yg * s)
            dyg = da * yv * (s * (1.0 + yg * (1.0 - s)))
            out = []
            for dy, x, cwt, stage in ((dyg, xg, cwg, stage_g), (dyv, xv, cwv, stage_v)):
                d0 = dy[0:R, :]
                d1 = pltpu.roll(dy, R + 7, 0)[0:R, :]
                d2 = pltpu.roll(dy, R + 6, 0)[0:R, :]
                stage[pl.ds(c0, R), :] = (cwt[2:3, :] * d0 + cwt[1:2, :] * d1 + cwt[0:1, :] * d2).astype(BF16)
                u = x[8:R + 8, :]
                out += [jnp.sum(d2 * u, axis=0, keepdims=True), jnp.sum(d1 * u, axis=0, keepdims=True),
                        jnp.sum(d0 * u, axis=0, keepdims=True), jnp.sum(d0, axis=0, keepdims=True)]
            return tuple(a + o for a, o in zip(acc, out))

        zero = jnp.zeros((1, CT), F32)
        acc = lax.fori_loop(0, SEQ // R, chunk, (zero,) * 8)
        for dcw_ref, dcb_ref, part in ((dcwg_ref, dcbg_ref, acc[0:4]), (dcwv_ref, dcbv_ref, acc[4:8])):
            dcw = jnp.concatenate(part[0:3], axis=0)

            @pl.when(b == 0)
            def _():
                dcw_ref[...] = dcw
                dcb_ref[...] = part[3]

            @pl.when(b > 0)
            def _():
                dcw_ref[...] += dcw
                dcb_ref[...] += part[3]

        writes(0, stage_g, sems.at[0]).start()
        writes(FF, stage_v, sems.at[1]).start()

        @pl.when(step == last)
        def _():
            writes(0, stage_g, sems.at[0]).wait()
            writes(FF, stage_v, sems.at[1]).wait()

    def blk(off):
        return pl.BlockSpec((None, SEQ, CT), lambda c, b: (b, 0, off + c))

    def vec(r, off):
        return pl.BlockSpec((r, CT), lambda c, b: (0, off + c))

    du3, dcwg, dcwv, dcbg, dcbv = _call(
        body, name=name, grid=(NCT, Bn),
        in_specs=[blk(0), blk(0), blk(NCT), vec(3, 0), vec(3, NCT), vec(1, 0), vec(1, NCT)],
        out_specs=[pl.BlockSpec(memory_space=pl.ANY), vec(3, 0), vec(3, 0), vec(1, 0), vec(1, 0)],
        out_shape=[_sds((Bn, SEQ, UW), BF16), _sds((3, FF), F32), _sds((3, FF), F32), _sds((1, FF), F32),
                   _sds((1, FF), F32)],
        scratch_shapes=[pltpu.VMEM((SEQ, CT), BF16)] * 2 + [pltpu.VMEM((SEQ + 16, CT), F32)] * 2
        + [pltpu.VMEM((SEQ + 8, CT), F32), pltpu.SemaphoreType.DMA((2,))], comm=comm,
    )(da3, u3, u3, cw, cw, cb, cb)
    return du3, jnp.concatenate([dcwg, dcwv], axis=1), jnp.concatenate([dcbg, dcbv], axis=1)


def _rope_tables():
    pos = jnp.arange(SEQ, dtype=F32)
    inv_freq = jnp.exp(jnp.arange(0, ROPE_DIM, 2, dtype=F32) * (-math.log(ROPE_THETA) / ROPE_DIM))
    ang = pos[:, None] * inv_freq[None, :]
    cos, sin = jnp.cos(ang), jnp.sin(ang)
    half = ROPE_DIM // 2
    zeros = jnp.zeros((SEQ, HEAD - ROPE_DIM), F32)
    zh = jnp.zeros((SEQ, half), F32)
    tab_c = jnp.concatenate([cos, cos, zeros + 1.0], axis=1)
    tab_a = jnp.concatenate([-sin, zh, zeros], axis=1)
    tab_b = jnp.concatenate([zh, sin, zeros], axis=1)
    return tab_c, tab_a, tab_b


def _rot(v, tc, ta, tb):
    half = ROPE_DIM // 2
    return v * tc + pltpu.roll(v, HEAD - half, 1) * ta + pltpu.roll(v, half, 1) * tb


def _rot_t(dv, tc, ta, tb):
    half = ROPE_DIM // 2
    return dv * tc + pltpu.roll(dv * ta, half, 1) + pltpu.roll(dv * tb, HEAD - half, 1)


def _band_masks():
    qi = lax.broadcasted_iota(jnp.int32, (HEAD, 2 * HEAD), 0)
    ki = lax.broadcasted_iota(jnp.int32, (HEAD, 2 * HEAD), 1)
    diff = HEAD + qi - ki
    both = (diff >= 0) & (diff <= HEAD)
    q1 = lax.broadcasted_iota(jnp.int32, (HEAD, HEAD), 0)
    k1 = lax.broadcasted_iota(jnp.int32, (HEAD, HEAD), 1)
    return q1 >= k1, both


_NT = (((1,), (1,)), ((), ()))
_TN = (((0,), (0,)), ((), ()))
_SCALE = HEAD ** -0.5


ATT_W = HEAD
ATT_HP = ATT_W // HEAD


def _res_rows(r, n, d, base=0):
    return pl.ds(base * d + r, n, stride=d) if d > 1 else pl.ds(base, n)


def _attn_load(q_ref, k_ref, v_ref, tc_ref, ta_ref, tb_ref, qs, ks, vs, d):
    L = SEQ // d
    for r in range(d):
        rows = _res_rows(r, L, d)
        dst = slice(r * L, (r + 1) * L)
        tc, ta, tb = tc_ref[dst, :], ta_ref[dst, :], tb_ref[dst, :]
        for hh in range(ATT_HP):
            sl = slice(hh * HEAD, (hh + 1) * HEAD)
            qs[dst, sl] = _rot(q_ref[rows, sl], tc, ta, tb).astype(BF16)
            ks[dst, sl] = _rot(k_ref[rows, sl], tc, ta, tb).astype(BF16)
            vs[dst, sl] = v_ref[rows, sl].astype(BF16)


def _attn_fwd(name, z3, tabs, g, d, comm=None):
    Bn = z3.shape[0]
    L = SEQ // d
    nb = L // HEAD
    W, nh = ATT_W, GROUP_W // ATT_W

    def body(q_ref, k_ref, v_ref, tc_ref, ta_ref, tb_ref, o_ref, l_ref, qs, ks, vs, sc, pc):
        m_first, m_both = _band_masks()
        _attn_load(q_ref, k_ref, v_ref, tc_ref, ta_ref, tb_ref, qs, ks, vs, d)
        blocks = [(r, n) for r in range(d) for n in range(nb)]

        def spans(r, n):
            rq = slice(r * L + n * HEAD, r * L + (n + 1) * HEAD)
            rk = slice(r * L + max(n - 1, 0) * HEAD, r * L + (n + 1) * HEAD)
            return rq, rk, slice(0, HEAD if n == 0 else 2 * HEAD)

        for i, (r, n) in enumerate(blocks):
            rq, rk, kc = spans(r, n)
            sc[i, :, kc] = lax.dot_general(qs[rq, :], ks[rk, :], _NT, preferred_element_type=F32)
        for i, (r, n) in enumerate(blocks):
            rq, rk, kc = spans(r, n)
            s = jnp.where(m_first if n == 0 else m_both, sc[i, :, kc] * _SCALE, NEG_INF)
            m = jnp.max(s, axis=-1, keepdims=True)
            e = jnp.exp(s - m)
            den = jnp.sum(e, axis=-1, keepdims=True)
            pc[i, :, kc] = (e * (1.0 / den)).astype(BF16)
            l_ref[_res_rows(r, HEAD, d, n * HEAD), :] = jnp.broadcast_to(m + jnp.log(den), (HEAD, HEAD))
        for i, (r, n) in enumerate(blocks):
            rq, rk, kc = spans(r, n)
            o_ref[_res_rows(r, HEAD, d, n * HEAD), :] = jnp.dot(pc[i, :, kc], vs[rk, :], preferred_element_type=F32)

    def zcol(off):
        return pl.BlockSpec((None, SEQ, W), lambda b, h: (b, 0, (off + g * GROUP_W) // W + h))

    tab = pl.BlockSpec((SEQ, HEAD), lambda b, h: (0, 0))
    out = pl.BlockSpec((None, SEQ, W), lambda b, h: (b, 0, h))
    return _call(
        body, name=name, grid=(Bn, nh),
        in_specs=[zcol(0), zcol(OFF_K), zcol(OFF_V), tab, tab, tab],
        out_specs=[out, out],
        out_shape=[_sds((Bn, SEQ, GROUP_W), F32), _sds((Bn, SEQ, GROUP_W), F32)],
        scratch_shapes=[pltpu.VMEM((SEQ, W), BF16)] * 3
        + [pltpu.VMEM((SEQ // HEAD, HEAD, 2 * HEAD), F32), pltpu.VMEM((SEQ // HEAD, HEAD, 2 * HEAD), BF16)],
        comm=comm,
    )(z3, z3, z3, *tabs)


def _attn_bwd(name, z3, tabs, g, d, do3, lse3, delta3, dz3, comm=None):
    Bn = z3.shape[0]
    L = SEQ // d
    nb = L // HEAD
    W, nh = ATT_W, GROUP_W // ATT_W

    def body(q_ref, k_ref, v_ref, tc_ref, ta_ref, tb_ref, do_ref, l_ref, dl_ref, dz_in, dz_ref,
             qs, ks, vs, dos, dqs, dks, dvs, nat, oq, ok, ov, sc, dpc, pc, dsc, sems):
        b, h = pl.program_id(0), pl.program_id(1)
        m_first, m_both = _band_masks()
        _attn_load(q_ref, k_ref, v_ref, tc_ref, ta_ref, tb_ref, qs, ks, vs, d)
        for r in range(d):
            dos[r * L:(r + 1) * L, :] = do_ref[_res_rows(r, L, d), :].astype(BF16)
        dks[...] = jnp.zeros_like(dks)
        dvs[...] = jnp.zeros_like(dvs)
        blocks = [(r, n) for r in range(d) for n in range(nb)]

        def spans(r, n):
            rq = slice(r * L + n * HEAD, r * L + (n + 1) * HEAD)
            rk = slice(r * L + max(n - 1, 0) * HEAD, r * L + (n + 1) * HEAD)
            return rq, rk, slice(0, HEAD if n == 0 else 2 * HEAD)

        for i, (r, n) in enumerate(blocks):
            rq, rk, kc = spans(r, n)
            sc[i, :, kc] = lax.dot_general(qs[rq, :], ks[rk, :], _NT, preferred_element_type=F32)
            dpc[i, :, kc] = lax.dot_general(dos[rq, :], vs[rk, :], _NT, preferred_element_type=F32)
        for i, (r, n) in enumerate(blocks):
            rq, rk, kc = spans(r, n)
            rows = _res_rows(r, HEAD, d, n * HEAD)
            s = jnp.where(m_first if n == 0 else m_both, sc[i, :, kc] * _SCALE, NEG_INF)
            p = jnp.exp(s - l_ref[rows, :][:, 0:1])
            pc[i, :, kc] = p.astype(BF16)
            dsc[i, :, kc] = (p * (dpc[i, :, kc] - dl_ref[rows, :][:, 0:1]) * _SCALE).astype(BF16)
        for i, (r, n) in enumerate(blocks):
            rq, rk, kc = spans(r, n)
            dqs[rq, :] = jnp.dot(dsc[i, :, kc], ks[rk, :], preferred_element_type=F32)
        for i, (r, n) in enumerate(blocks):
            rq, rk, kc = spans(r, n)
            dks[rk, :] += lax.dot_general(dsc[i, :, kc], qs[rq, :], _TN, preferred_element_type=F32)
            dvs[rk, :] += lax.dot_general(pc[i, :, kc], dos[rq, :], _TN, preferred_element_type=F32)
        step = b * nh + h

        def writes():
            base = g * GROUP_W + h * W
            return [pltpu.make_async_copy(src, dz_ref.at[b, :, pl.ds(pl.multiple_of(base + off, HEAD), W)],
                                          sems.at[i])
                    for i, (src, off) in enumerate(((oq, 0), (ok, OFF_K), (ov, OFF_V)))]

        @pl.when(step > 0)
        def _():
            for cp in writes():
                cp.wait()

        for src, dst, rotate in ((dqs, oq, True), (dks, ok, True), (dvs, ov, False)):
            for r in range(d):
                val = src[r * L:(r + 1) * L, :]
                if rotate:
                    rm = slice(r * L, (r + 1) * L)
                    val = _rot_t(val, tc_ref[rm, :], ta_ref[rm, :], tb_ref[rm, :])
                nat[_res_rows(r, L, d), :] = val
            dst[...] = nat[...].astype(BF16)
        for cp in writes():
            cp.start()

        @pl.when(step == Bn * nh - 1)
        def _():
            for cp in writes():
                cp.wait()

    def zcol(off):
        return pl.BlockSpec((None, SEQ, W), lambda b, h: (b, 0, (off + g * GROUP_W) // W + h))

    tab = pl.BlockSpec((SEQ, HEAD), lambda b, h: (0, 0))
    gcol = pl.BlockSpec((None, SEQ, W), lambda b, h: (b, 0, h))
    any_spec = pl.BlockSpec(memory_space=pl.ANY)
    return _call(
        body, name=name, grid=(Bn, nh),
        in_specs=[zcol(0), zcol(OFF_K), zcol(OFF_V), tab, tab, tab, gcol, gcol, gcol, any_spec],
        out_specs=any_spec,
        out_shape=_sds((Bn, SEQ, ZW), BF16),
        scratch_shapes=[pltpu.VMEM((SEQ, W), BF16)] * 4 + [pltpu.VMEM((SEQ, W), F32)] * 4
        + [pltpu.VMEM((SEQ, W), BF16)] * 3
        + [pltpu.VMEM((SEQ // HEAD, HEAD, 2 * HEAD), F32)] * 2 + [pltpu.VMEM((SEQ // HEAD, HEAD, 2 * HEAD), BF16)] * 2
        + [pltpu.SemaphoreType.DMA((3,))],
        aliases={9: 0}, comm=comm,
    )(z3, z3, z3, *tabs, do3, lse3, delta3, dz3)


def _merge_weights(l0, l1, l2):
    m = jnp.maximum(jnp.maximum(l0, l1), l2)
    e0, e1, e2 = jnp.exp(l0 - m), jnp.exp(l1 - m), jnp.exp(l2 - m)
    inv = 1.0 / (e0 + e1 + e2)
    return e0 * inv, e1 * inv, e2 * inv


def _merge_fwd(name, outs, lses, tr=512):
    T = outs[0].shape[0]

    def body(o0, o1, o2, l0, l1, l2, a_ref):
        w0, w1, w2 = _merge_weights(l0[...], l1[...], l2[...])
        a_ref[...] = (w0 * o0[...] + w1 * o1[...] + w2 * o2[...]).astype(BF16)

    row = pl.BlockSpec((tr, GROUP_W), lambda i: (i, 0))
    return pl.pallas_call(
        body, name=name, grid=(T // tr,), in_specs=[row] * 6, out_specs=row,
        out_shape=_sds((T, GROUP_W), BF16), compiler_params=_params(1),
    )(*outs, *lses)


def _merge_bwd(name, outs, lses, dattn, tr=512):
    T = outs[0].shape[0]

    def body(o0, o1, o2, l0, l1, l2, da_ref, d0, d1, d2, e0, e1, e2):
        w = _merge_weights(l0[...], l1[...], l2[...])
        da = da_ref[...]
        attn = w[0] * o0[...] + w[1] * o1[...] + w[2] * o2[...]
        prod = da * attn
        csum = jnp.concatenate(
            [jnp.broadcast_to(jnp.sum(prod[:, hh * HEAD:(hh + 1) * HEAD], axis=-1, keepdims=True), (tr, HEAD))
             for hh in range(GROUP_W // HEAD)], axis=1)
        for wg, d_ref, e_ref in zip(w, (d0, d1, d2), (e0, e1, e2)):
            d_ref[...] = wg * da
            e_ref[...] = wg * csum

    row = pl.BlockSpec((tr, GROUP_W), lambda i: (i, 0))
    res = pl.pallas_call(
        body, name=name, grid=(T // tr,), in_specs=[row] * 7, out_specs=[row] * 6,
        out_shape=[_sds((T, GROUP_W), F32)] * 6,
        compiler_params=_params(1),
    )(*outs, *lses, dattn)
    return res[:3], res[3:]


def _rms_rows(xv, g):
    r = lax.rsqrt(jnp.mean(xv * xv, axis=-1, keepdims=True) + RMS_EPS)
    return (xv * r * g).astype(BF16)


def _epi_norm(val, x_refs, o_ref, y_refs, first_rows):
    o_ref[...] = val
    y_refs[0][...] = _rms_rows(val, x_refs[0][...])


def _epi_ple(val, x_refs, o_ref, y_refs, first_rows):
    o_ref[...] = val
    xn = x_refs[0][...] + x_refs[1][...] * _sigmoid(val)
    y_refs[0][...] = xn
    if len(y_refs) > 1:
        y_refs[1][...] = _rms_rows(xn, x_refs[2][...])


def _epi_norm_bwd(val, x_refs, o_ref, y_refs, first_rows):
    xv = x_refs[0][...]
    r = lax.rsqrt(jnp.mean(xv * xv, axis=-1, keepdims=True) + RMS_EPS)
    xh = xv * r
    part = _rows8(val * xh)

    @pl.when(first_rows)
    def _():
        y_refs[0][...] = part

    @pl.when(jnp.logical_not(first_rows))
    def _():
        y_refs[0][...] += part

    dxh = val * x_refs[2][...]
    o_ref[...] = x_refs[1][...] + r * (dxh - xh * jnp.mean(dxh * xh, axis=-1, keepdims=True))


def _local_step(x3, p4, tgt3, vecs, ex):
    Bn = x3.shape[0]
    T = Bn * SEQ
    x = x3.reshape(T, D)
    tgt = tgt3.reshape(T, D)
    pb = p4.astype(BF16).reshape(DEPTH, T, PLE)
    tabs = {}
    for d in DILATIONS:
        tabs[d] = [t.reshape(SEQ // d, d, HEAD).transpose(1, 0, 2).reshape(SEQ, HEAD) for t in _rope_tables()]
    tm = 1024 if T % 1024 == 0 else 512
    nt = T // tm
    tk = 1024 if T % 1024 == 0 else 512
    ntk = T // tk
    tm5 = 512
    f32o = lambda n: _sds((T, n), F32)

    def spec(shape, fn):
        return pl.BlockSpec(shape, fn)

    def _mm(name, *args, **kwargs):
        return _mm_call(name, *args, comm=ex.hook(name), **kwargs)

    def cols4(a_ref, b_ref):
        av = a_ref[...]
        return jnp.concatenate([_dot(av, b_ref[j]) for j in range(NCHIP)], axis=1)

    def rows4(a_ref, b_ref):
        av = a_ref[...]
        return jnp.concatenate([_dot(av, b_ref[:, j * 256:(j + 1) * 256], ta=True) for j in range(NCHIP)], axis=0)

    def kchunks4(a_ref, b_ref):
        total = _dot(a_ref[:, 0:256], b_ref[0], tb=True)
        for j in range(1, NCHIP):
            total = total + _dot(a_ref[:, j * 256:(j + 1) * 256], b_ref[j], tb=True)
        return total

    row5 = spec((tm5, D), lambda i, *_: (i, 0))
    rowm = spec((tm, D), lambda i, *_: (i, 0))
    gain = spec((1, D), lambda *_: (0, 0))
    bf_rows = (_sds((T, D), BF16), row5)

    saved = []
    h = _rms_fwd("rms_mix0", x, vecs["g_mix"][0:1])
    for l in range(DEPTH):
        L = str(l)
        G = ex.weights(l)
        g_mix, g_ffn, g_ple = (vecs[k][l:l + 1] for k in ("g_mix", "g_ffn", "g_ple"))
        pscale, cb, cw = vecs["pool_scale"][l:l + 1], vecs["conv_b"][l:l + 1], vecs["conv_w"][l]
        z = _mm("mm_z" + L, h, G["in"], grid=(nt, NCHIP),
                a_spec=spec((tm, D), lambda i, n: (i, 0)),
                b_spec=spec((None, D, IN_S), lambda i, n: (n, 0, 0)),
                o_spec=spec((tm, IN_S), lambda i, n: (i, n)), out_shape=f32o(ZW))
        z3 = z.reshape(Bn, SEQ, ZW)
        outs, lses = [], []
        for g, d in enumerate(DILATIONS):
            o_g, l_g = _attn_fwd("attn_fwd%d_%d" % (g, l), z3, tabs[d], g, d, comm=ex.hook("attn_fwd%d_%d" % (g, l)))
            outs.append(o_g.reshape(T, GROUP_W))
            lses.append(l_g.reshape(T, GROUP_W))
        attn = _merge_fwd("merge_fwd" + L, outs, lses)
        ya = _mm("mm_ya" + L, attn, G["sc"], grid=(nt,), compute=cols4,
                 a_spec=spec((tm, GROUP_W), lambda i: (i, 0)),
                 b_spec=spec((NCHIP, GROUP_W, 256), lambda i: (0, 0, 0)),
                 o_spec=spec((tm, D), lambda i: (i, 0)), out_shape=_sds((T, D), BF16))
        pooled3, ms3 = _pool_fwd("pool_fwd" + L, z3, G["sc"], pscale)
        ms = ms3.reshape(T, D)

        def row_sharded(name, a, rb, res=None, kdim=D, out=F32, **fused):
            if rb is None:
                b_arr, b_spec = G["dn"], spec((NCHIP, DN_S, D), lambda i: (0, 0, 0))
            else:
                b_arr, b_spec = G["r3"], spec((NCHIP, 256, D), lambda i: (0, rb, 0))
            return _mm(name, a, b_arr, grid=(T // tm5,),
                       a_spec=spec((tm5, kdim), lambda i: (i, 0)), b_spec=b_spec,
                       o_spec=row5, out_shape=_sds((T, D), out), res=res, res_spec=None if res is None else row5,
                       **fused)

        yb = row_sharded("mm_yb" + L, ms, 0, out=BF16)
        merged = _gate_fwd("gate_fwd" + L, z, ya, yb)
        x1, h2 = row_sharded("mm_o" + L, merged, 1, res=x, epilogue=_epi_norm, extra_in=[(g_ffn, gain)],
                             extra_out=[bf_rows])
        u = _mm("mm_up" + L, h2, G["up"], grid=(nt, NCHIP),
                a_spec=spec((tm, D), lambda i, n: (i, 0)),
                b_spec=spec((None, D, UP_S), lambda i, n: (n, 0, 0)),
                o_spec=spec((tm, UP_S), lambda i, n: (i, n)), out_shape=f32o(UW))
        u3 = u.reshape(Bn, SEQ, UW)
        act = _conv_fwd("conv_fwd" + L, u3, cw, cb, comm=ex.hook("conv_fwd" + L)).reshape(T, FF)
        x2, h3 = row_sharded("mm_down" + L, act, None, res=x1, kdim=FF, epilogue=_epi_norm,
                             extra_in=[(g_ple, gain)], extra_out=[bf_rows])
        pe = _mm("mm_pe" + L, pb[l], G["sc"], grid=(nt,), compute=cols4,
                 a_spec=spec((tm, PLE), lambda i: (i, 0)),
                 b_spec=spec((NCHIP, 256, 256), lambda i: (0, 2, 0)),
                 o_spec=spec((tm, D), lambda i: (i, 0)), out_shape=f32o(D))
        fused_in = [(x2, row5), (pe, row5)]
        fused_out = [(f32o(D), row5)]
        if l + 1 < DEPTH:
            fused_in.append((vecs["g_mix"][l + 1:l + 2], gain))
            fused_out.append(bf_rows)
        pg, x3n, *h_next = row_sharded("mm_pg" + L, h3, 2, epilogue=_epi_ple, extra_in=fused_in,
                                       extra_out=fused_out)
        saved.append(dict(x=x, h=h, z=z, outs=outs, lses=lses, attn=attn, ya=ya, yb=yb, pooled3=pooled3, ms=ms,
                          merged=merged, x1=x1, h2=h2, u3=u3, act=act, x2=x2, h3=h3, pg=pg, pe=pe))
        x = x3n
        h = h_next[0] if h_next else None

    dx, dg_final8, sq8 = _final_loss(x, vecs["g_final"].reshape(1, D), tgt)

    gg_shape = {k: _sds(G[k].shape, F32) for k in G}
    small = {"g_final": dg_final8}

    for l in reversed(range(DEPTH)):
        L = str(l)
        sv = saved[l]
        G = ex.weights(l)
        GG = dict.fromkeys(_KEYS)
        g_mix, g_ffn, g_ple = (vecs[k][l:l + 1] for k in ("g_mix", "g_ffn", "g_ple"))
        pscale, cb, cw = vecs["pool_scale"][l:l + 1], vecs["conv_b"][l:l + 1], vecs["conv_w"][l]

        def wgrad_rows(name, a, b_arr, key, rb):
            GG[key] = _mm(name, a, b_arr, grid=(2, ntk), ta=True, k_axis=1, nk=ntk, acc_shape=(D, 512),
                          a_spec=spec((tk, D), lambda n, k: (k, 0)),
                          b_spec=spec((tk, 512), lambda n, k: (k, n)),
                          o_spec=spec((NCHIP, 256, 512), lambda n, k: (0, rb, n)),
                          out_shape=gg_shape[key], buf=GG[key])

        def dgrad_rows(name, dy, rb, out=F32, **fused):
            return _mm(name, dy, G["r3"], grid=(T // tm5,), tb=True,
                       a_spec=spec((tm5, D), lambda i: (i, 0)),
                       b_spec=spec((NCHIP, 256, D), lambda i: (0, rb, 0)),
                       o_spec=row5, out_shape=_sds((T, D), out), **fused)

        def norm_bwd(xin, dres, g, rows=row5):
            return dict(epilogue=_epi_norm_bwd, extra_in=[(xin, rows), (dres, rows), (g, gain)],
                        extra_out=[(_sds((8, D), F32), spec((8, D), lambda *_: (0, 0)))])

        dpe, dpg = _ple_bwd("ple_bwd" + L, dx, sv["pe"], sv["pg"])
        GG["sc"] = _mm("wg_ple" + L, pb[l], dpe, grid=(ntk,), compute=rows4, k_axis=0, nk=ntk,
                       acc_shape=(NCHIP * PLE, 256),
                       a_spec=spec((tk, PLE), lambda k: (k, 0)), b_spec=spec((tk, D), lambda k: (k, 0)),
                       o_spec=spec((NCHIP, 256, 256), lambda k: (0, 2, 0)),
                       out_shape=gg_shape["sc"], buf=GG["sc"])
        wgrad_rows("wg_pg" + L, sv["h3"], dpg, "r3", 2)
        dx, small["g_ple" + L] = dgrad_rows("dg_pg" + L, dpg, 2, **norm_bwd(sv["x2"], dx, g_ple))

        da = _mm("dg_down" + L, dx, G["dn"], grid=(T // 256,), tb=True,
                 a_spec=spec((256, D), lambda i: (i, 0)),
                 b_spec=spec((NCHIP, DN_S, D), lambda i: (0, 0, 0)),
                 o_spec=spec((256, FF), lambda i: (i, 0)), out_shape=_sds((T, FF), BF16))
        GG["dn"] = _mm("wg_down" + L, sv["act"], dx, grid=(2, ntk), ta=True, k_axis=1, nk=ntk,
                       acc_shape=(FF, 512),
                       a_spec=spec((tk, FF), lambda n, k: (k, 0)), b_spec=spec((tk, 512), lambda n, k: (k, n)),
                       o_spec=spec((NCHIP, DN_S, 512), lambda n, k: (0, 0, n)),
                       out_shape=gg_shape["dn"], buf=GG["dn"])
        du3, dcw, dcb = _conv_bwd("conv_bwd" + L, da.reshape(Bn, SEQ, FF), sv["u3"], cw, cb,
                                  comm=ex.hook("conv_bwd" + L))
        small["conv_w" + L], small["conv_b" + L] = dcw, dcb
        du = du3.reshape(T, UW)
        dx, small["g_ffn" + L] = _mm(
            "dg_up" + L, du, G["up"], grid=(nt, NCHIP), tb=True, k_axis=1, nk=NCHIP, acc_shape=(tm, D),
            a_spec=spec((tm, UP_S), lambda i, k: (i, k)), b_spec=spec((None, D, UP_S), lambda i, k: (k, 0, 0)),
            o_spec=rowm, out_shape=f32o(D), vmem=VMEM_BIG, **norm_bwd(sv["x1"], dx, g_ffn, rowm))
        GG["up"] = _mm("wg_up" + L, sv["h2"], du, grid=(NCHIP, ntk), ta=True, k_axis=1, nk=ntk,
                       acc_shape=(D, UP_S),
                       a_spec=spec((tk, D), lambda j, k: (k, 0)),
                       b_spec=spec((tk, UP_S), lambda j, k: (k, j)),
                       o_spec=spec((None, D, UP_S), lambda j, k: (j, 0, 0)),
                       out_shape=gg_shape["up"], buf=GG["up"])

        dmerged = dgrad_rows("dg_o" + L, dx, 1, out=BF16)
        wgrad_rows("wg_o" + L, sv["merged"], dx, "r3", 1)
        dya, dz = _gate_bwd("gate_bwd_a" + L, sv["z"], OFF_GA, sv["ya"], dmerged, None)
        dyb, dz = _gate_bwd("gate_bwd_b" + L, sv["z"], OFF_GB, sv["yb"], dmerged, dz)
        dms = dgrad_rows("dg_yb" + L, dyb, 0)
        wgrad_rows("wg_yb" + L, sv["ms"], dyb, "r3", 0)
        dz3, GG["sc"], small["pool_scale" + L] = _pool_bwd(
            "pool_bwd" + L, dms.reshape(Bn, SEQ, D), sv["pooled3"], G["sc"], pscale,
            dz.reshape(Bn, SEQ, ZW), GG["sc"])
        dattn = _mm("dg_ya" + L, dya, G["sc"], grid=(nt,), compute=kchunks4,
                    a_spec=spec((tm, D), lambda i: (i, 0)),
                    b_spec=spec((NCHIP, GROUP_W, 256), lambda i: (0, 0, 0)),
                    o_spec=spec((tm, GROUP_W), lambda i: (i, 0)), out_shape=f32o(GROUP_W))
        GG["sc"] = _mm("wg_ya" + L, sv["attn"], dya, grid=(ntk,), compute=rows4, k_axis=0, nk=ntk,
                       acc_shape=(NCHIP * GROUP_W, 256),
                       a_spec=spec((tk, GROUP_W), lambda k: (k, 0)), b_spec=spec((tk, D), lambda k: (k, 0)),
                       o_spec=spec((NCHIP, GROUP_W, 256), lambda k: (0, 0, 0)),
                       out_shape=gg_shape["sc"], buf=GG["sc"])
        ex.grads_ready(l, {k: GG[k] for k in _KEYS[1:]})
        dos, deltas = _merge_bwd("merge_bwd" + L, sv["outs"], sv["lses"], dattn)
        view3 = lambda t: t.reshape(Bn, SEQ, GROUP_W)
        sz3 = sv["z"].reshape(Bn, SEQ, ZW)
        for g, d in enumerate(DILATIONS):
            dz3 = _attn_bwd("attn_bwd%d_%d" % (g, l), sz3, tabs[d], g, d, view3(dos[g]), view3(sv["lses"][g]),
                            view3(deltas[g]), dz3, comm=ex.hook("attn_bwd%d_%d" % (g, l)))
        dz = dz3.reshape(T, ZW)
        dx_in = dx
        dx, small["g_mix" + L] = _mm(
            "dg_z" + L, dz, G["in"], grid=(nt, NCHIP), tb=True, k_axis=1, nk=NCHIP, acc_shape=(tm, D),
            a_spec=spec((tm, IN_S), lambda i, k: (i, k)), b_spec=spec((None, D, IN_S), lambda i, k: (k, 0, 0)),
            o_spec=rowm, out_shape=f32o(D), vmem=VMEM_BIG, **norm_bwd(sv["x"], dx_in, g_mix, rowm))
        GG["in"] = _mm("wg_z" + L, sv["h"], dz, grid=(NCHIP, ntk), ta=True, k_axis=1, nk=ntk,
                       acc_shape=(D, IN_S),
                       a_spec=spec((tk, D), lambda n, k: (k, 0)), b_spec=spec((tk, IN_S), lambda n, k: (k, n)),
                       o_spec=spec((None, D, IN_S), lambda n, k: (n, 0, 0)),
                       out_shape=gg_shape["in"], buf=GG["in"])
        ex.grads_ready(l, {"in": GG["in"]})

    return sq8, dx.reshape(Bn, SEQ, D), small


_ANY = pl.BlockSpec(memory_space=pl.ANY)


def _place():
    x, y, c = lax.axis_index("x"), lax.axis_index("y"), lax.axis_index("c")
    chips = [(1 - x, y), (x, 1 - y), (1 - x, 1 - y)]
    return x, y, c, 2 * x + y, chips


def _half(rows, cc):
    return pl.ds(cc * (rows // 2), rows // 2)


def _remote(src, dst, send_sems, recv_sems, i, to):
    return pltpu.make_async_remote_copy(src_ref=src, dst_ref=dst, send_sem=send_sems.at[i], recv_sem=recv_sems.at[i],
                                        device_id=to, device_id_type=MESH)


def _exchange_gather_ici(stacks, done):
    n = len(stacks)
    rows = [t.shape[1] for t in stacks]

    def start(refs, send_sems, recv_sems):
        x, y, c, me, chips = _place()
        for k in range(n):
            part = refs[k].at[me, _half(rows[k], c)]
            for j, chip in enumerate(chips):
                _remote(part, part, send_sems, recv_sems, 3 * k + j, (*chip, c)).start()

    def wait(refs, send_sems, recv_sems):
        x, y, c, me, chips = _place()
        for k in range(n):
            for j, chip in enumerate(chips):
                part = refs[k].at[2 * chip[0] + chip[1], _half(rows[k], c)]
                _remote(part, part, send_sems, recv_sems, 3 * k + j, (*chip, c)).wait()

    return dict(arrays=list(stacks), nsem=3 * n, start=start, wait=wait, done=done)


def _exchange_gather_d2d(stacks, done):
    n = len(stacks)
    rows = [t.shape[1] for t in stacks]

    def copies(refs, send_sems, recv_sems, mine):
        x, y, c, me, chips = _place()
        cc = c if mine else 1 - c
        return [_remote(part, part, send_sems, recv_sems, 3 * k + j, (x, y, 1 - c))
                for k in range(n) for j, chip in enumerate(chips)
                for part in [refs[k].at[2 * chip[0] + chip[1], _half(rows[k], cc)]]]

    def start(refs, send_sems, recv_sems):
        for cp in copies(refs, send_sems, recv_sems, True):
            cp.start()

    def wait(refs, send_sems, recv_sems):
        for cp in copies(refs, send_sems, recv_sems, False):
            cp.wait()

    return dict(arrays=list(stacks), nsem=3 * n, start=start, wait=wait, done=done)


def _exchange_halves(g5, recv, done):
    n = len(g5)

    def copies(refs, send_sems, recv_sems):
        x, y, c, me, chips = _place()
        return [_remote(refs[k].at[:, 1 - c], refs[n + k], send_sems, recv_sems, k, (x, y, 1 - c)) for k in range(n)]

    def start(refs, send_sems, recv_sems):
        for cp in copies(refs, send_sems, recv_sems):
            cp.start()

    def wait(refs, send_sems, recv_sems):
        for cp in copies(refs, send_sems, recv_sems):
            cp.wait()

    return dict(arrays=list(g5) + list(recv), nsem=n, start=start, wait=wait, done=done)


def _exchange_chips(parts, landing, done):
    n = len(parts)

    def start(refs, send_sems, recv_sems):
        x, y, c, me, chips = _place()
        for k in range(n):
            for j, chip in enumerate(chips):
                _remote(refs[k].at[2 * chip[0] + chip[1]], refs[n + k].at[me], send_sems, recv_sems, 3 * k + j,
                        (*chip, c)).start()

    def wait(refs, send_sems, recv_sems):
        x, y, c, me, chips = _place()
        for k in range(n):
            for j, chip in enumerate(chips):
                slot = refs[n + k].at[2 * chip[0] + chip[1]]
                _remote(slot, slot, send_sems, recv_sems, 3 * k + j, (*chip, c)).wait()

    return dict(arrays=list(parts) + list(landing), nsem=3 * n, start=start, wait=wait, done=done)


def _exchange_share(full, layer, done):
    n = len(full)

    def copies(refs, send_sems, recv_sems, mine):
        x, y, c, me, chips = _place()
        cc = c if mine else 1 - c
        return [_remote(part, part, send_sems, recv_sems, k, (x, y, 1 - c))
                for k in range(n) for part in [refs[k].at[layer, cc]]]

    def start(refs, send_sems, recv_sems):
        for cp in copies(refs, send_sems, recv_sems, True):
            cp.start()

    def wait(refs, send_sems, recv_sems):
        for cp in copies(refs, send_sems, recv_sems, False):
            cp.wait()

    return dict(arrays=list(full), nsem=n, start=start, wait=wait, done=done)


def _exchange_call(name, comm):
    arrays = comm["arrays"]
    n = len(arrays)

    def body(*refs):
        outs, send_sems, recv_sems = refs[n:2 * n], refs[2 * n], refs[2 * n + 1]
        comm["start"](outs, send_sems, recv_sems)
        comm["wait"](outs, send_sems, recv_sems)

    outs = pl.pallas_call(
        body, name=name, in_specs=[_ANY] * n, out_specs=[_ANY] * n,
        out_shape=[_sds(t.shape, t.dtype) for t in arrays],
        scratch_shapes=[pltpu.SemaphoreType.DMA((comm["nsem"],))] * 2,
        input_output_aliases={i: i for i in range(n)},
    )(*arrays)
    comm["done"](outs)


def _gather_first(stacks, cw4):
    n = len(stacks)
    ici = _exchange_gather_ici(stacks, None)
    d2d = _exchange_gather_d2d(stacks, None)
    rows = [t.shape[1] for t in stacks]

    def body(*refs):
        g_refs, cwg_ref = refs[n + 1:2 * n + 1], refs[2 * n + 1]
        s_ici, r_ici, s_d2d, r_d2d, s_cw, r_cw = refs[2 * n + 2:]
        x, y, c, me, chips = _place()

        def cw_copy(j, slot, chip):
            part = cwg_ref.at[slot]
            return _remote(part, part, s_cw, r_cw, j, (*chip, c))

        ici["start"](g_refs, s_ici, r_ici)
        for j, chip in enumerate(chips):
            cw_copy(j, me, chip).start()
        for k in range(n):
            for j, chip in enumerate(chips):
                part = g_refs[k].at[2 * chip[0] + chip[1], _half(rows[k], c)]
                _remote(part, part, s_ici, r_ici, 3 * k + j, (*chip, c)).wait()
                _remote(part, part, s_d2d, r_d2d, 3 * k + j, (x, y, 1 - c)).start()
        d2d["wait"](g_refs, s_d2d, r_d2d)
        for j, chip in enumerate(chips):
            cw_copy(j, 2 * chip[0] + chip[1], chip).wait()

    outs = pl.pallas_call(
        body, name="gather_first", in_specs=[_ANY] * (n + 1), out_specs=[_ANY] * (n + 1),
        out_shape=[_sds(t.shape, t.dtype) for t in stacks] + [_sds(cw4.shape, cw4.dtype)],
        scratch_shapes=[pltpu.SemaphoreType.DMA((3 * n,))] * 4 + [pltpu.SemaphoreType.DMA((3,))] * 2,
        input_output_aliases={i: i for i in range(n + 1)},
    )(*stacks, cw4)
    return outs[:n], outs[n]


def _halves_and_small(g5, recv, small):
    n = len(g5)
    halves = _exchange_halves(g5, recv, None)

    def body(*refs):
        small_ref = refs[2 * n]
        c_refs, red_ref = refs[2 * n + 1:4 * n + 1], refs[4 * n + 1]
        gath, send_sems, recv_sems, s_send, s_recv = refs[4 * n + 2:]
        x, y, c, me, chips = _place()
        dev = 4 * x + 2 * y + c
        gath[dev] = small_ref[...]
        halves["start"](c_refs, send_sems, recv_sems)
        for r in range(1, 8):
            peer = (x ^ (r >> 2), y ^ ((r >> 1) & 1), c ^ (r & 1))
            _remote(small_ref, gath.at[dev], s_send, s_recv, r - 1, peer).start()
        for r in range(1, 8):
            peer = (x ^ (r >> 2), y ^ ((r >> 1) & 1), c ^ (r & 1))
            src = 4 * peer[0] + 2 * peer[1] + peer[2]
            _remote(small_ref, gath.at[src], s_send, s_recv, r - 1, peer).wait()
        total = gath[0]
        for i in range(1, 8):
            total = total + gath[i]
        red_ref[...] = total
        halves["wait"](c_refs, send_sems, recv_sems)

    vm = pl.BlockSpec(memory_space=pltpu.VMEM)
    arrays = list(g5) + list(recv)
    outs = pl.pallas_call(
        body, name="halves_and_small", in_specs=[_ANY] * (2 * n) + [vm], out_specs=[_ANY] * (2 * n) + [vm],
        out_shape=[_sds(t.shape, t.dtype) for t in arrays] + [_sds(small.shape, F32)],
        scratch_shapes=[pltpu.VMEM((8,) + small.shape, F32), pltpu.SemaphoreType.DMA((n,)),
                        pltpu.SemaphoreType.DMA((n,)), pltpu.SemaphoreType.DMA((7,)), pltpu.SemaphoreType.DMA((7,))],
        input_output_aliases={i: i for i in range(2 * n)},
    )(*arrays, small)
    return outs[:n], outs[n:2 * n], outs[2 * n]


def _row_tile(rh):
    for cand in (512, 384, 352, 256, 128):
        if rh % cand == 0:
            return cand
    return rh


def _add_halves(name, g5, recv, place):
    _, _, rh, cols = g5.shape
    tr = _row_tile(rh)

    def body(place_ref, g_ref, r_ref, o_ref, own_ref):
        val = (g_ref[...] + r_ref[...]).astype(BF16)
        o_ref[...] = val

        @pl.when(pl.program_id(1) == place_ref[1])
        def _():
            own_ref[...] = val

    grid_spec = pltpu.PrefetchScalarGridSpec(
        num_scalar_prefetch=1, grid=(rh // tr, NCHIP),
        in_specs=[pl.BlockSpec((None, None, tr, cols), lambda i, j, pr: (j, pr[0], i, 0)),
                  pl.BlockSpec((None, tr, cols), lambda i, j, pr: (j, i, 0))],
        out_specs=[pl.BlockSpec((None, tr, cols), lambda i, j, pr: (j, i, 0)),
                   pl.BlockSpec((None, tr, cols), lambda i, j, pr: (pr[1], i, 0))])
    return pl.pallas_call(
        body, name=name, grid_spec=grid_spec, out_shape=[_sds(recv.shape, BF16)] * 2, compiler_params=_params(2),
    )(place, g5, recv)


def _sum_chips(name, landing, place, layer, full):
    _, rh, cols = landing.shape
    tr = _row_tile(rh)
    has_full = full is not None

    def body(*refs):
        r_ref, o_ref = refs[1], refs[-1]
        total = r_ref[0].astype(F32)
        for j in range(1, NCHIP):
            total = total + r_ref[j].astype(F32)
        o_ref[...] = total

    grid_spec = pltpu.PrefetchScalarGridSpec(
        num_scalar_prefetch=1, grid=(rh // tr,),
        in_specs=[pl.BlockSpec((NCHIP, tr, cols), lambda i, pr: (0, i, 0))] + ([_ANY] if has_full else []),
        out_specs=pl.BlockSpec((None, None, tr, cols), lambda i, pr: (layer, pr[0], i, 0)))
    return pl.pallas_call(
        body, name=name, grid_spec=grid_spec, out_shape=_sds((DEPTH, 2, rh, cols), F32),
        input_output_aliases={2: 0} if has_full else {}, compiler_params=_params(1),
    )(place, landing, *([full] if has_full else []))


class _Schedule:
    FIRST, REST = ["in"], list(_KEYS[1:])

    def __init__(self, slotted, cw4, place):
        self.place = place
        self._w = [dict(zip(_KEYS, layer)) for layer in slotted]
        got, self.cw4 = _gather_first([self._w[0][k] for k in self.FIRST], cw4)
        self._w[0].update(zip(self.FIRST, got))
        self._g5, self._recv, self._parts, self._landing = [{}, {}], [{}, {}], [{}, {}], [{}, {}]
        self.full = {}
        every = list(_KEYS)
        self._hooks = {
            "mm_z0": lambda: self._gather(_exchange_gather_ici, 0, self.REST),
            "attn_fwd0_0": lambda: self._gather(_exchange_gather_d2d, 0, self.REST),
            "mm_up0": lambda: self._gather(_exchange_gather_ici, 1, self.REST),
            "conv_fwd0": lambda: self._gather(_exchange_gather_ici, 1, self.FIRST),
            "mm_down0": lambda: self._gather(_exchange_gather_d2d, 1, every),
            "dg_down0": lambda: self._halves(1, every),
            "conv_bwd0": lambda: self._chips(1, every),
            "dg_up0": lambda: self._share(1, every),
            "attn_bwd0_0": lambda: self._halves(0, self.REST),
            "dg_z0": lambda: self._chips(0, self.REST),
            "wg_z0": lambda: self._share(0, self.REST),
        }

    def weights(self, layer):
        return self._w[layer]

    def hook(self, name):
        make = self._hooks.get(name)
        return make() if make else None

    def grads_ready(self, layer, GG):
        for k, t in GG.items():
            g5 = t.reshape(NCHIP, 2, t.shape[1] // 2, t.shape[2])
            self._g5[layer][k] = g5
            self._recv[layer][k] = lax.empty((NCHIP,) + g5.shape[2:], F32)

    def _gather(self, make, layer, keys):
        def done(arrays):
            self._w[layer].update(zip(keys, arrays))
        return make([self._w[layer][k] for k in keys], done)

    def _halves(self, layer, keys):
        return _exchange_halves([self._g5[layer][k] for k in keys], [self._recv[layer][k] for k in keys],
                                lambda arrays: self._halves_done(layer, keys, arrays))

    def _halves_done(self, layer, keys, arrays):
        n = len(keys)
        for k, g, r in zip(keys, arrays[:n], arrays[n:]):
            self._parts[layer][k], self._landing[layer][k] = _add_halves(
                "add_halves%d_%s" % (layer, k), g, r, self.place)

    def _chips(self, layer, keys):
        return _exchange_chips([self._parts[layer][k] for k in keys], [self._landing[layer][k] for k in keys],
                               lambda arrays: self._chips_done(layer, keys, arrays))

    def _chips_done(self, layer, keys, arrays):
        for k, t in zip(keys, arrays[len(keys):]):
            self.full[k] = _sum_chips("sum_chips%d_%s" % (layer, k), t, self.place, layer, self.full.get(k))

    def _share(self, layer, keys):
        def done(arrays):
            self.full.update(zip(keys, arrays))
        return _exchange_share([self.full[k] for k in keys], layer, done)

    def finish(self, small):
        keys = self.FIRST
        g5, recv, small_red = _halves_and_small([self._g5[0][k] for k in keys], [self._recv[0][k] for k in keys], small)
        self._halves_done(0, keys, list(g5) + list(recv))
        _exchange_call("exchange_chips_last", self._chips(0, keys))
        _exchange_call("share_halves_last", self._share(0, keys))
        return [self.full[k] for k in _KEYS], small_red


def _adamw(name, w, g, m, v):
    shape = w.shape
    cols = shape[-1]
    rows = 1
    for s in shape[:-1]:
        rows *= s
    tr = rows
    for cand in (256, 128, 64):
        if rows > cand and rows % cand == 0:
            tr = cand
            break
    c1 = 1.0 / (1.0 - B1 ** STEP)
    c2 = 1.0 / (1.0 - B2 ** STEP)

    def body(w_ref, g_ref, m_ref, v_ref, go_ref, d_ref, nm_ref, nv_ref):
        gv = g_ref[...]
        go_ref[...] = gv
        nm = B1 * m_ref[...] + (1.0 - B1) * gv
        nv = B2 * v_ref[...] + (1.0 - B2) * (gv * gv)
        nm_ref[...] = nm
        nv_ref[...] = nv
        d_ref[...] = -LR * ((nm * c1) / (jnp.sqrt(nv * c2) + ADAM_EPS) + WD * w_ref[...])

    blk = pl.BlockSpec((tr, cols), lambda i: (i, 0))
    outs = pl.pallas_call(
        body, name=name, grid=(rows // tr,), in_specs=[blk] * 4, out_specs=[blk] * 4,
        out_shape=[_sds((rows, cols), F32)] * 4, compiler_params=_params(1),
    )(*(t.reshape(rows, cols) for t in (w, g, m, v)))
    return tuple(o.reshape(shape) for o in outs)


def _pack_small(small):
    rows = [jnp.sum(small["g_mix%d" % l], axis=0, keepdims=True) for l in range(DEPTH)]
    rows += [jnp.sum(small["pool_scale%d" % l], axis=0, keepdims=True) for l in range(DEPTH)]
    rows += [jnp.sum(small["g_ffn%d" % l], axis=0, keepdims=True) for l in range(DEPTH)]
    rows += [jnp.sum(small["g_ple%d" % l], axis=0, keepdims=True) for l in range(DEPTH)]
    rows += [jnp.sum(small["g_final"], axis=0, keepdims=True)]
    flat = [small["conv_b%d" % l].reshape(-1) for l in range(DEPTH)]
    flat += [small["conv_w%d" % l].reshape(-1) for l in range(DEPTH)]
    flat = jnp.concatenate(flat).reshape(-1, D)
    packed = jnp.concatenate(rows + [flat], axis=0)
    return jnp.pad(packed, ((0, SMALL_ROWS - packed.shape[0]), (0, 0)))


def _unpack_small(red):
    g_mix, pool_scale, g_ffn, g_ple = red[0:2], red[2:4], red[4:6], red[6:8]
    g_final = red[8]
    nb = DEPTH * UW // D
    conv_b = red[9:9 + nb].reshape(DEPTH, UW)
    conv_w = red[9 + nb:9 + 4 * nb].reshape(DEPTH, 3, UW)
    return g_mix, pool_scale, g_ffn, g_ple, g_final, conv_b, conv_w


def kernel(x, p, g_mix, w_in, w_ya, w_yb, pool_w, pool_scale, w_o, g_ffn, w_up, conv_w, conv_b, w_down, g_ple, w_ple, w_ple_gate, g_final, loss_target, m_g_mix, m_w_in, m_w_ya, m_w_yb, m_pool_w, m_pool_scale, m_w_o, m_g_ffn, m_w_up, m_conv_w, m_conv_b, m_w_down, m_g_ple, m_w_ple, m_w_ple_gate, m_g_final, v_g_mix, v_w_in, v_w_ya, v_w_yb, v_pool_w, v_pool_scale, v_w_o, v_g_ffn, v_w_up, v_conv_w, v_conv_b, v_w_down, v_g_ple, v_w_ple, v_w_ple_gate, v_g_final):
    me = 2 * lax.axis_index("x") + lax.axis_index("y")
    place = jnp.stack([lax.axis_index("c"), me]).astype(jnp.int32)

    def slot(shard):
        return lax.dynamic_update_index_in_dim(lax.empty((NCHIP,) + shard.shape, shard.dtype), shard, me, 0)

    packed = [
        w_in.astype(BF16), w_up.astype(BF16),
        jnp.concatenate([w_ya, w_ple, pool_w.reshape(DEPTH, 256, 256)], axis=1).astype(BF16),
        jnp.concatenate([w_yb, w_o, w_ple_gate], axis=1).astype(BF16),
        w_down.astype(BF16),
    ]
    slotted = [[slot(t[l]) for t in packed] for l in range(DEPTH)]
    ex = _Schedule(slotted, slot(conv_w.reshape(DEPTH * 3, UP_S)), place)
    cw_full = ex.cw4.reshape(NCHIP, DEPTH, 3, UP_S).transpose(1, 2, 0, 3).reshape(DEPTH, 3, UW)

    vecs = dict(g_mix=g_mix, pool_scale=pool_scale, g_ffn=g_ffn, g_ple=g_ple, g_final=g_final, conv_b=conv_b,
                conv_w=cw_full)
    sq8, grad_x, small = _local_step(x, p, loss_target, vecs, ex)
    loss = lax.psum(jnp.sum(sq8) * (0.5 / D), ("x", "y", "c"))

    full, small_red = ex.finish(_pack_small(small))
    r_in, r_up, r_sc, r_r3, r_dn = [f.reshape(DEPTH, -1, f.shape[-1]) for f in full]
    d_g_mix, d_pool_scale, d_g_ffn, d_g_ple, d_g_final, d_conv_b, d_conv_w_full = _unpack_small(small_red)
    d_conv_w = lax.dynamic_slice_in_dim(d_conv_w_full, me * UP_S, UP_S, axis=2)

    grads = dict(
        g_mix=d_g_mix, w_in=r_in, w_ya=r_sc[:, 0:512], w_yb=r_r3[:, 0:256],
        pool_w=r_sc[:, 768:1024].reshape(DEPTH, 4, 64, 256), pool_scale=d_pool_scale, w_o=r_r3[:, 256:512],
        g_ffn=d_g_ffn, w_up=r_up, conv_w=d_conv_w, conv_b=d_conv_b, w_down=r_dn, g_ple=d_g_ple,
        w_ple=r_sc[:, 512:768], w_ple_gate=r_r3[:, 512:768], g_final=d_g_final)
    weights = dict(g_mix=g_mix, w_in=w_in, w_ya=w_ya, w_yb=w_yb, pool_w=pool_w, pool_scale=pool_scale, w_o=w_o,
                   g_ffn=g_ffn, w_up=w_up, conv_w=conv_w, conv_b=conv_b, w_down=w_down, g_ple=g_ple, w_ple=w_ple,
                   w_ple_gate=w_ple_gate, g_final=g_final)
    m_in = dict(g_mix=m_g_mix, w_in=m_w_in, w_ya=m_w_ya, w_yb=m_w_yb, pool_w=m_pool_w, pool_scale=m_pool_scale,
                w_o=m_w_o, g_ffn=m_g_ffn, w_up=m_w_up, conv_w=m_conv_w, conv_b=m_conv_b, w_down=m_w_down,
                g_ple=m_g_ple, w_ple=m_w_ple, w_ple_gate=m_w_ple_gate, g_final=m_g_final)
    v_in = dict(g_mix=v_g_mix, w_in=v_w_in, w_ya=v_w_ya, w_yb=v_w_yb, pool_w=v_pool_w, pool_scale=v_pool_scale,
                w_o=v_w_o, g_ffn=v_g_ffn, w_up=v_w_up, conv_w=v_conv_w, conv_b=v_conv_b, w_down=v_w_down,
                g_ple=v_g_ple, w_ple=v_w_ple, w_ple_gate=v_w_ple_gate, g_final=v_g_final)
    names = ["g_mix", "w_in", "w_ya", "w_yb", "pool_w", "pool_scale", "w_o", "g_ffn", "w_up", "conv_w", "conv_b",
             "w_down", "g_ple", "w_ple", "w_ple_gate", "g_final"]
    deltas, new_m, new_v = [], [], []
    for nme in names:
        gr = grads[nme].reshape(weights[nme].shape)
        grads[nme], dlt, nm, nv = _adamw("adamw_" + nme, weights[nme], gr, m_in[nme], v_in[nme])
        deltas.append(dlt)
        new_m.append(nm)
        new_v.append(nv)
    return (loss, grad_x, *[grads[nme] for nme in names], *deltas, *new_m, *new_v)
```

```python
import math

import jax
import jax.numpy as jnp
from jax import lax
from jax.experimental import pallas as pl
from jax.experimental.pallas import tpu as pltpu

F32 = jnp.float32
BF16 = jnp.bfloat16
_KEYS = ("in", "up", "sc", "r3", "dn")
MESH = pl.DeviceIdType.MESH

D = 1024
SEQ = 2048
DEPTH = 2
HEAD = 128
GROUP_W = 512
DILATIONS = (1, 4, 16)
ROPE_DIM = 32
ROPE_THETA = 500000.0
NEG_INF = -1e30
ZW = 7680
OFF_K, OFF_V, OFF_U, OFF_GA, OFF_GB = 1536, 3072, 4608, 5632, 6656
FF = 2816
UW = 2 * FF
PLE = 256
NCHIP = 4
IN_S, UP_S, DN_S = ZW // NCHIP, UW // NCHIP, FF // NCHIP
RMS_EPS = 1e-6
LR, B1, B2, ADAM_EPS, WD, STEP = 0.001, 0.9, 0.999, 1e-08, 0.01, 10
SMALL_ROWS = 56
VMEM_CAP = 48 * 1024 * 1024
VMEM_BIG = 58 * 1024 * 1024


def _params(n_grid, vmem=VMEM_CAP):
    return pltpu.CompilerParams(dimension_semantics=("arbitrary",) * n_grid, vmem_limit_bytes=vmem)


def _sigmoid(v):
    return 1.0 / (1.0 + jnp.exp(-v))


def _rows8(v):
    return jnp.sum(v.reshape(v.shape[0] // 8, 8, v.shape[1]), axis=0)


def _sds(shape, dtype):
    return jax.ShapeDtypeStruct(shape, dtype)


def _dot(av, bv, ta=False, tb=False):
    dims = (((0,) if ta else (1,), (1,) if tb else (0,)), ((), ()))
    return lax.dot_general(av.astype(BF16), bv.astype(BF16), dims, preferred_element_type=F32)


def _call(body, *, name, grid, in_specs, out_specs, out_shape, scratch_shapes=(), aliases=None, comm=None,
          vmem=VMEM_CAP):
    params = _params(len(grid), vmem)
    aliases = dict(aliases or {})
    if comm is None:
        return pl.pallas_call(body, name=name, grid=grid, in_specs=list(in_specs), out_specs=out_specs,
                              out_shape=out_shape, scratch_shapes=list(scratch_shapes),
                              input_output_aliases=aliases, compiler_params=params)
    single = not isinstance(out_shape, (list, tuple))
    out_specs_l = [out_specs] if single else list(out_specs)
    out_shape_l = [out_shape] if single else list(out_shape)
    n_in, n_out, n_c = len(in_specs), len(out_shape_l), len(comm["arrays"])

    def hosted(*refs):
        core_in, core_out = refs[:n_in], refs[n_in + n_c:n_in + n_c + n_out]
        c_refs = refs[n_in + n_c + n_out:n_in + 2 * n_c + n_out]
        scratch, (send_sems, recv_sems) = refs[n_in + 2 * n_c + n_out:-2], refs[-2:]
        ids = [pl.program_id(i) for i in range(len(grid))]
        first, last = ids[0] == 0, ids[0] == grid[0] - 1
        for i in range(1, len(grid)):
            first = jnp.logical_and(first, ids[i] == 0)
            last = jnp.logical_and(last, ids[i] == grid[i] - 1)

        @pl.when(first)
        def _():
            comm["start"](c_refs, send_sems, recv_sems)

        body(*core_in, *core_out, *scratch)

        @pl.when(last)
        def _():
            comm["wait"](c_refs, send_sems, recv_sems)

    any_spec = pl.BlockSpec(memory_space=pl.ANY)
    for i in range(n_c):
        aliases[n_in + i] = n_out + i
    call = pl.pallas_call(
        hosted, name=name, grid=grid, in_specs=list(in_specs) + [any_spec] * n_c,
        out_specs=out_specs_l + [any_spec] * n_c,
        out_shape=out_shape_l + [_sds(t.shape, t.dtype) for t in comm["arrays"]],
        scratch_shapes=list(scratch_shapes) + [pltpu.SemaphoreType.DMA((comm["nsem"],))] * 2,
        input_output_aliases=aliases, compiler_params=params)

    def run(*args):
        outs = call(*args, *comm["arrays"])
        comm["done"](outs[n_out:])
        return outs[0] if single else outs[:n_out]

    return run


def _mm_call(name, a, b, *, grid, a_spec, b_spec, o_spec, out_shape, ta=False, tb=False, k_axis=None, nk=1,
             acc_shape=None, res=None, res_spec=None, buf=None, compute=None, comm=None,
             extra_in=(), extra_out=(), epilogue=None, vmem=VMEM_CAP):
    has_res, has_buf = res is not None, buf is not None
    in_place = nk > 1 and not has_res and out_shape.dtype == F32 and epilogue is None
    n_xi, n_xo = len(extra_in), len(extra_out)

    def body(*refs):
        a_ref, b_ref = refs[0], refs[1]
        pos = 2
        r_ref = None
        if has_res:
            r_ref = refs[pos]
            pos += 1
        if has_buf:
            pos += 1
        x_refs = refs[pos:pos + n_xi]
        pos += n_xi
        first_rows = pl.program_id(0) == 0
        o_ref = refs[pos]
        y_refs = refs[pos + 1:pos + 1 + n_xo]
        if compute is None:
            av = a_ref[...]
            bv = b_ref[...]
            part = _dot(av.reshape(-1, av.shape[-1]), bv.reshape(-1, bv.shape[-1]), ta, tb)
        else:
            part = compute(a_ref, b_ref)

        def finish(val):
            if r_ref is not None:
                val = val + r_ref[...]
            if epilogue is not None:
                epilogue(val, x_refs, o_ref, y_refs, first_rows)
            else:
                o_ref[...] = val.reshape(o_ref.shape).astype(o_ref.dtype)

        if nk == 1:
            finish(part)
        elif in_place:
            @pl.when(pl.program_id(k_axis) == 0)
            def _():
                o_ref[...] = jnp.zeros(o_ref.shape, F32)

            o_ref[...] += part.reshape(o_ref.shape)
        else:
            acc_ref = refs[pos + 1 + n_xo]
            k = pl.program_id(k_axis)

            @pl.when(k == 0)
            def _():
                acc_ref[...] = jnp.zeros(acc_ref.shape, F32)

            acc_ref[...] += part

            @pl.when(k == nk - 1)
            def _():
                finish(acc_ref[...])

    ins, in_specs = [a, b], [a_spec, b_spec]
    if has_res:
        ins.append(res)
        in_specs.append(res_spec)
    aliases = {}
    if has_buf:
        aliases = {len(ins): 0}
        ins.append(buf)
        in_specs.append(pl.BlockSpec(memory_space=pl.ANY))
    for arr, sp in extra_in:
        ins.append(arr)
        in_specs.append(sp)
    scratch = [pltpu.VMEM(acc_shape, F32)] if nk > 1 and not in_place else []
    if not extra_out:
        return _call(body, name=name, grid=grid, in_specs=in_specs, out_specs=o_spec, out_shape=out_shape,
                     scratch_shapes=scratch, aliases=aliases, comm=comm, vmem=vmem)(*ins)
    return _call(body, name=name, grid=grid, in_specs=in_specs, out_specs=[o_spec] + [sp for _, sp in extra_out],
                 out_shape=[out_shape] + [sh for sh, _ in extra_out], scratch_shapes=scratch, aliases=aliases,
                 comm=comm, vmem=vmem)(*ins)


def _rms_fwd(name, x, g, tr=512):
    T = x.shape[0]

    def body(x_ref, g_ref, h_ref):
        xv = x_ref[...]
        r = lax.rsqrt(jnp.mean(xv * xv, axis=-1, keepdims=True) + RMS_EPS)
        h_ref[...] = (xv * r * g_ref[...]).astype(BF16)

    return pl.pallas_call(
        body, name=name, grid=(T // tr,),
        in_specs=[pl.BlockSpec((tr, D), lambda i: (i, 0)), pl.BlockSpec((1, D), lambda i: (0, 0))],
        out_specs=pl.BlockSpec((tr, D), lambda i: (i, 0)), out_shape=_sds((T, D), BF16),
        compiler_params=_params(1),
    )(x, g)


def _final_loss(x, g, tgt, tr=512):
    T = x.shape[0]

    def body(x_ref, g_ref, t_ref, dx_ref, dg_ref, sq_ref):
        xv = x_ref[...]
        r = lax.rsqrt(jnp.mean(xv * xv, axis=-1, keepdims=True) + RMS_EPS)
        xh = xv * r
        gv = g_ref[...]
        e = xh * gv - t_ref[...]
        dy = e * (1.0 / D)
        pg = _rows8(dy * xh)
        ps = _rows8(e * e)

        @pl.when(pl.program_id(0) == 0)
        def _():
            dg_ref[...] = pg
            sq_ref[...] = ps

        @pl.when(pl.program_id(0) > 0)
        def _():
            dg_ref[...] += pg
            sq_ref[...] += ps

        dxh = dy * gv
        dx_ref[...] = r * (dxh - xh * jnp.mean(dxh * xh, axis=-1, keepdims=True))

    row = pl.BlockSpec((tr, D), lambda i: (i, 0))
    acc = pl.BlockSpec((8, D), lambda i: (0, 0))
    return pl.pallas_call(
        body, name="final_loss", grid=(T // tr,),
        in_specs=[row, pl.BlockSpec((1, D), lambda i: (0, 0)), row],
        out_specs=[row, acc, acc],
        out_shape=[_sds((T, D), F32), _sds((8, D), F32), _sds((8, D), F32)],
        compiler_params=_params(1),
    )(x, g, tgt)


def _ple_bwd(name, dx, pe, pg, tr=512):
    T = dx.shape[0]

    def body(dx_ref, pe_ref, pg_ref, dpe_ref, dpg_ref):
        s = _sigmoid(pg_ref[...])
        dxv = dx_ref[...]
        dpe_ref[...] = (dxv * s).astype(BF16)
        dpg_ref[...] = (dxv * pe_ref[...] * s * (1.0 - s)).astype(BF16)

    row = pl.BlockSpec((tr, D), lambda i: (i, 0))
    return pl.pallas_call(
        body, name=name, grid=(T // tr,), in_specs=[row, row, row], out_specs=[row, row],
        out_shape=[_sds((T, D), BF16), _sds((T, D), BF16)], compiler_params=_params(1),
    )(dx, pe, pg)


def _gate_fwd(name, z, ya, yb, tr=512):
    T = z.shape[0]
    w = 512

    def body(ga_ref, gb_ref, ya_ref, yb_ref, o_ref):
        o_ref[...] = (_sigmoid(ga_ref[...]) * ya_ref[...].astype(F32)
                      + _sigmoid(gb_ref[...]) * yb_ref[...].astype(F32)).astype(BF16)

    col = pl.BlockSpec((tr, w), lambda i, j: (i, j))
    return pl.pallas_call(
        body, name=name, grid=(T // tr, D // w),
        in_specs=[pl.BlockSpec((tr, w), lambda i, j: (i, OFF_GA // w + j)),
                  pl.BlockSpec((tr, w), lambda i, j: (i, OFF_GB // w + j)), col, col],
        out_specs=col, out_shape=_sds((T, D), BF16), compiler_params=_params(2),
    )(z, z, ya, yb)


def _gate_bwd(name, z, off, y, dm, dz, tr=512):
    T = z.shape[0]
    w = 512
    has_dz = dz is not None

    def body(*refs):
        g_ref, y_ref, dm_ref = refs[:3]
        dy_ref, dz_ref = refs[-2:]
        s = _sigmoid(g_ref[...])
        dmv = dm_ref[...].astype(F32)
        dy_ref[...] = (dmv * s).astype(BF16)
        dz_ref[...] = (dmv * y_ref[...].astype(F32) * s * (1.0 - s)).astype(BF16)

    col = pl.BlockSpec((tr, w), lambda i, j: (i, j))
    gcol = pl.BlockSpec((tr, w), lambda i, j: (i, off // w + j))
    ins, in_specs, aliases = [z, y, dm], [gcol, col, col], {}
    if has_dz:
        ins.append(dz)
        in_specs.append(pl.BlockSpec(memory_space=pl.ANY))
        aliases = {3: 1}
    return pl.pallas_call(
        body, name=name, grid=(T // tr, D // w), in_specs=in_specs, out_specs=[col, gcol],
        out_shape=[_sds((T, D), BF16), _sds((T, ZW), BF16)], input_output_aliases=aliases,
        compiler_params=_params(2),
    )(*ins)


def _shift_down(v, k, rows):
    return jnp.where(rows >= k, pltpu.roll(v, k, 0), 0.0)


def _shift_up(v, k, rows):
    n = v.shape[0]
    return jnp.where(rows < n - k, pltpu.roll(v, n - k, 0), 0.0)


def _pool_window(v, g, rows, shift):
    s2 = v + shift(v, 1, rows)
    s4 = s2 + shift(s2, 2, rows)
    s8 = s4 + shift(s4, 4, rows)
    s16 = s8 + shift(s8, 8, rows)
    return jnp.where(g == 0, s2, jnp.where(g == 1, s4, jnp.where(g == 2, s8, s16)))


def _pool_count(g, rows):
    wlen = jnp.left_shift(2, g).astype(F32)
    return jnp.minimum(rows.astype(F32) + 1.0, wlen)


def _pool_fwd(name, z3, g_sc, scale):
    Bn = z3.shape[0]
    gw = 256

    def body(u_ref, pw_ref, sc_ref, pooled_ref, ms_ref):
        g = pl.program_id(1)
        u = u_ref[...]
        rows = lax.broadcasted_iota(jnp.int32, u.shape, 0)
        pooled = (_pool_window(u, g, rows, _shift_down) / _pool_count(g, rows) - u).astype(BF16)
        pooled_ref[...] = pooled
        pw = pw_ref[...].reshape(gw, gw)
        mixed = jnp.dot(pooled, pw, preferred_element_type=F32)
        ms_ref[...] = (mixed * sc_ref[...]).astype(BF16)

    blk = pl.BlockSpec((None, SEQ, gw), lambda b, g: (b, 0, g))
    return pl.pallas_call(
        body, name=name, grid=(Bn, 4),
        in_specs=[pl.BlockSpec((None, SEQ, gw), lambda b, g: (b, 0, OFF_U // gw + g)),
                  pl.BlockSpec((NCHIP, 64, gw), lambda b, g: (0, 12 + g, 0)),
                  pl.BlockSpec((1, gw), lambda b, g: (0, g))],
        out_specs=[blk, blk],
        out_shape=[_sds((Bn, SEQ, D), BF16), _sds((Bn, SEQ, D), BF16)],
        compiler_params=_params(2),
    )(z3, g_sc, scale)


def _pool_bwd(name, dms3, pooled3, g_sc, scale, dz3, gg_sc):
    Bn = dms3.shape[0]
    gw = 256
    has_gg = gg_sc is not None

    def body(*refs):
        dms_ref, pooled_ref, pw_ref, sc_ref = refs[:4]
        dz_ref, dpw_ref, dsc_ref = refs[-3:]
        g, b = pl.program_id(0), pl.program_id(1)
        pooled = pooled_ref[...]
        pw = pw_ref[...].reshape(gw, gw)
        dms = dms_ref[...]
        mixed = jnp.dot(pooled, pw, preferred_element_type=F32)
        psc = _rows8(dms * mixed)
        dmixed = (dms * sc_ref[...]).astype(BF16)
        dpw = lax.dot_general(pooled, dmixed, (((0,), (0,)), ((), ())), preferred_element_type=F32)
        dpw = dpw.reshape(NCHIP, 64, gw)

        @pl.when(b == 0)
        def _():
            dsc_ref[...] = psc
            dpw_ref[...] = dpw

        @pl.when(b > 0)
        def _():
            dsc_ref[...] += psc
            dpw_ref[...] += dpw

        dpooled = lax.dot_general(dmixed, pw, (((1,), (1,)), ((), ())), preferred_element_type=F32)
        rows = lax.broadcasted_iota(jnp.int32, dpooled.shape, 0)
        dq = dpooled / _pool_count(g, rows)
        dz_ref[...] = (_pool_window(dq, g, rows, _shift_up) - dpooled).astype(BF16)

    ins = [dms3, pooled3, g_sc, scale, dz3]
    in_specs = [pl.BlockSpec((None, SEQ, gw), lambda g, b: (b, 0, g)),
                pl.BlockSpec((None, SEQ, gw), lambda g, b: (b, 0, g)),
                pl.BlockSpec((NCHIP, 64, gw), lambda g, b: (0, 12 + g, 0)),
                pl.BlockSpec((1, gw), lambda g, b: (0, g)),
                pl.BlockSpec(memory_space=pl.ANY)]
    aliases = {4: 0}
    if has_gg:
        ins.append(gg_sc)
        in_specs.append(pl.BlockSpec(memory_space=pl.ANY))
        aliases[5] = 1
    return pl.pallas_call(
        body, name=name, grid=(4, Bn), in_specs=in_specs,
        out_specs=[pl.BlockSpec((None, SEQ, gw), lambda g, b: (b, 0, OFF_U // gw + g)),
                   pl.BlockSpec((NCHIP, 64, gw), lambda g, b: (0, 12 + g, 0)),
                   pl.BlockSpec((8, gw), lambda g, b: (0, g))],
        out_shape=[_sds(dz3.shape, BF16), _sds((NCHIP, D, 256), F32), _sds((8, D), F32)],
        input_output_aliases=aliases, compiler_params=_params(2),
    )(*ins)


CT = 256
NCT = FF // CT


def _conv_pre(u, cw_ref, cb_ref, rows):
    return (cb_ref[...] + cw_ref[0:1, :] * _shift_down(u, 2, rows) + cw_ref[1:2, :] * _shift_down(u, 1, rows)
            + cw_ref[2:3, :] * u)


def _conv_fwd(name, u3, cw, cb, comm=None):
    Bn = u3.shape[0]

    def body(ug_ref, uv_ref, cwg_ref, cwv_ref, cbg_ref, cbv_ref, a_ref):
        ug, uv = ug_ref[...], uv_ref[...]
        rows = lax.broadcasted_iota(jnp.int32, ug.shape, 0)
        yg = _conv_pre(ug, cwg_ref, cbg_ref, rows)
        yv = _conv_pre(uv, cwv_ref, cbv_ref, rows)
        a_ref[...] = (yg * _sigmoid(yg) * yv).astype(BF16)

    def blk(off):
        return pl.BlockSpec((None, SEQ, CT), lambda b, c: (b, 0, off + c))

    return _call(
        body, name=name, grid=(Bn, NCT),
        in_specs=[blk(0), blk(NCT),
                  pl.BlockSpec((3, CT), lambda b, c: (0, c)), pl.BlockSpec((3, CT), lambda b, c: (0, NCT + c)),
                  pl.BlockSpec((1, CT), lambda b, c: (0, c)), pl.BlockSpec((1, CT), lambda b, c: (0, NCT + c))],
        out_specs=blk(0), out_shape=_sds((Bn, SEQ, FF), BF16), comm=comm,
    )(u3, u3, cw, cw, cb, cb)


def _conv_bwd(name, da3, u3, cw, cb, comm=None):
    Bn = u3.shape[0]
    last = NCT * Bn - 1
    R = 128

    def body(da_ref, ug_ref, uv_ref, cwg_ref, cwv_ref, cbg_ref, cbv_ref,
             du_ref, dcwg_ref, dcwv_ref, dcbg_ref, dcbv_ref, stage_g, stage_v, ugp, uvp, dap, sems):
        c, b = pl.program_id(0), pl.program_id(1)
        step = c * Bn + b
        zeros8 = jnp.zeros((8, CT), F32)
        for pad, ref in ((ugp, ug_ref), (uvp, uv_ref)):
            pad[0:8, :] = zeros8
            pad[8:SEQ + 8, :] = ref[...]
            pad[SEQ + 8:SEQ + 16, :] = zeros8
        dap[0:SEQ, :] = da_ref[...].astype(F32)
        dap[SEQ:SEQ + 8, :] = zeros8

        def writes(off_c, stage, sem):
            col = pl.multiple_of(off_c + c * CT, CT)
            return pltpu.make_async_copy(stage, du_ref.at[b, :, pl.ds(col, CT)], sem)

        @pl.when(step > 0)
        def _():
            writes(0, stage_g, sems.at[0]).wait()
            writes(FF, stage_v, sems.at[1]).wait()

        cwg, cwv, cbg, cbv = cwg_ref[...], cwv_ref[...], cbg_ref[...], cbv_ref[...]

        def pre(x, cwt, cbt):
            y = cbt + cwt[0:1, :] * pltpu.roll(x, 2, 0) + cwt[1:2, :] * pltpu.roll(x, 1, 0) + cwt[2:3, :] * x
            return y[8:R + 16, :]

        def chunk(i, acc):
            c0 = pl.multiple_of(i * R, R)
            xg, xv = ugp[pl.ds(c0, R + 16), :], uvp[pl.ds(c0, R + 16), :]
            yg, yv = pre(xg, cwg, cbg), pre(xv, cwv, cbv)
            da = dap[pl.ds(c0, R + 8), :]
            s = _sigmoid(yg)
            dyv = da * (yg * s)
            dyg = da * yv * (s * (1.0 + yg * (1.0 - s)))
            out = []
            for dy, x, cwt, stage in ((dyg, xg, cwg, stage_g), (dyv, xv, cwv, stage_v)):
                d0 = dy[0:R, :]
                d1 = pltpu.roll(dy, R + 7, 0)[0:R, :]
                d2 = pltpu.roll(dy, R + 6, 0)[0:R, :]
                stage[pl.ds(c0, R), :] = (cwt[2:3, :] * d0 + cwt[1:2, :] * d1 + cwt[0:1, :] * d2).astype(BF16)
                u = x[8:R + 8, :]
                out += [jnp.sum(d2 * u, axis=0, keepdims=True), jnp.sum(d1 * u, axis=0, keepdims=True),
                        jnp.sum(d0 * u, axis=0, keepdims=True), jnp.sum(d0, axis=0, keepdims=True)]
            return tuple(a + o for a, o in zip(acc, out))

        zero = jnp.zeros((1, CT), F32)
        acc = lax.fori_loop(0, SEQ // R, chunk, (zero,) * 8)
        for dcw_ref, dcb_ref, part in ((dcwg_ref, dcbg_ref, acc[0:4]), (dcwv_ref, dcbv_ref, acc[4:8])):
            dcw = jnp.concatenate(part[0:3], axis=0)

            @pl.when(b == 0)
            def _():
                dcw_ref[...] = dcw
                dcb_ref[...] = part[3]

            @pl.when(b > 0)
            def _():
                dcw_ref[...] += dcw
                dcb_ref[...] += part[3]

        writes(0, stage_g, sems.at[0]).start()
        writes(FF, stage_v, sems.at[1]).start()

        @pl.when(step == last)
        def _():
            writes(0, stage_g, sems.at[0]).wait()
            writes(FF, stage_v, sems.at[1]).wait()

    def blk(off):
        return pl.BlockSpec((None, SEQ, CT), lambda c, b: (b, 0, off + c))

    def vec(r, off):
        return pl.BlockSpec((r, CT), lambda c, b: (0, off + c))

    du3, dcwg, dcwv, dcbg, dcbv = _call(
        body, name=name, grid=(NCT, Bn),
        in_specs=[blk(0), blk(0), blk(NCT), vec(3, 0), vec(3, NCT), vec(1, 0), vec(1, NCT)],
        out_specs=[pl.BlockSpec(memory_space=pl.ANY), vec(3, 0), vec(3, 0), vec(1, 0), vec(1, 0)],
        out_shape=[_sds((Bn, SEQ, UW), BF16), _sds((3, FF), F32), _sds((3, FF), F32), _sds((1, FF), F32),
                   _sds((1, FF), F32)],
        scratch_shapes=[pltpu.VMEM((SEQ, CT), BF16)] * 2 + [pltpu.VMEM((SEQ + 16, CT), F32)] * 2
        + [pltpu.VMEM((SEQ + 8, CT), F32), pltpu.SemaphoreType.DMA((2,))], comm=comm,
    )(da3, u3, u3, cw, cw, cb, cb)
    return du3, jnp.concatenate([dcwg, dcwv], axis=1), jnp.concatenate([dcbg, dcbv], axis=1)


def _rope_tables():
    pos = jnp.arange(SEQ, dtype=F32)
    inv_freq = jnp.exp(jnp.arange(0, ROPE_DIM, 2, dtype=F32) * (-math.log(ROPE_THETA) / ROPE_DIM))
    ang = pos[:, None] * inv_freq[None, :]
    cos, sin = jnp.cos(ang), jnp.sin(ang)
    half = ROPE_DIM // 2
    zeros = jnp.zeros((SEQ, HEAD - ROPE_DIM), F32)
    zh = jnp.zeros((SEQ, half), F32)
    tab_c = jnp.concatenate([cos, cos, zeros + 1.0], axis=1)
    tab_a = jnp.concatenate([-sin, zh, zeros], axis=1)
    tab_b = jnp.concatenate([zh, sin, zeros], axis=1)
    return tab_c, tab_a, tab_b


def _rot(v, tc, ta, tb):
    half = ROPE_DIM // 2
    return v * tc + pltpu.roll(v, HEAD - half, 1) * ta + pltpu.roll(v, half, 1) * tb


def _rot_t(dv, tc, ta, tb):
    half = ROPE_DIM // 2
    return dv * tc + pltpu.roll(dv * ta, half, 1) + pltpu.roll(dv * tb, HEAD - half, 1)


def _band_masks():
    qi = lax.broadcasted_iota(jnp.int32, (HEAD, 2 * HEAD), 0)
    ki = lax.broadcasted_iota(jnp.int32, (HEAD, 2 * HEAD), 1)
    diff = HEAD + qi - ki
    both = (diff >= 0) & (diff <= HEAD)
    q1 = lax.broadcasted_iota(jnp.int32, (HEAD, HEAD), 0)
    k1 = lax.broadcasted_iota(jnp.int32, (HEAD, HEAD), 1)
    return q1 >= k1, both


_NT = (((1,), (1,)), ((), ()))
_TN = (((0,), (0,)), ((), ()))
_SCALE = HEAD ** -0.5


ATT_W = HEAD
ATT_HP = ATT_W // HEAD


def _res_rows(r, n, d, base=0):
    return pl.ds(base * d + r, n, stride=d) if d > 1 else pl.ds(base, n)


def _attn_load(q_ref, k_ref, v_ref, tc_ref, ta_ref, tb_ref, qs, ks, vs, d):
    L = SEQ // d
    for r in range(d):
        rows = _res_rows(r, L, d)
        dst = slice(r * L, (r + 1) * L)
        tc, ta, tb = tc_ref[dst, :], ta_ref[dst, :], tb_ref[dst, :]
        for hh in range(ATT_HP):
            sl = slice(hh * HEAD, (hh + 1) * HEAD)
            qs[dst, sl] = _rot(q_ref[rows, sl], tc, ta, tb).astype(BF16)
            ks[dst, sl] = _rot(k_ref[rows, sl], tc, ta, tb).astype(BF16)
            vs[dst, sl] = v_ref[rows, sl].astype(BF16)


def _attn_fwd(name, z3, tabs, g, d, comm=None):
    Bn = z3.shape[0]
    L = SEQ // d
    nb = L // HEAD
    W, nh = ATT_W, GROUP_W // ATT_W

    def body(q_ref, k_ref, v_ref, tc_ref, ta_ref, tb_ref, o_ref, l_ref, qs, ks, vs, sc, pc):
        m_first, m_both = _band_masks()
        _attn_load(q_ref, k_ref, v_ref, tc_ref, ta_ref, tb_ref, qs, ks, vs, d)
        blocks = [(r, n) for r in range(d) for n in range(nb)]

        def spans(r, n):
            rq = slice(r * L + n * HEAD, r * L + (n + 1) * HEAD)
            rk = slice(r * L + max(n - 1, 0) * HEAD, r * L + (n + 1) * HEAD)
            return rq, rk, slice(0, HEAD if n == 0 else 2 * HEAD)

        for i, (r, n) in enumerate(blocks):
            rq, rk, kc = spans(r, n)
            sc[i, :, kc] = lax.dot_general(qs[rq, :], ks[rk, :], _NT, preferred_element_type=F32)
        for i, (r, n) in enumerate(blocks):
            rq, rk, kc = spans(r, n)
            s = jnp.where(m_first if n == 0 else m_both, sc[i, :, kc] * _SCALE, NEG_INF)
            m = jnp.max(s, axis=-1, keepdims=True)
            e = jnp.exp(s - m)
            den = jnp.sum(e, axis=-1, keepdims=True)
            pc[i, :, kc] = (e * (1.0 / den)).astype(BF16)
            l_ref[_res_rows(r, HEAD, d, n * HEAD), :] = jnp.broadcast_to(m + jnp.log(den), (HEAD, HEAD))
        for i, (r, n) in enumerate(blocks):
            rq, rk, kc = spans(r, n)
            o_ref[_res_rows(r, HEAD, d, n * HEAD), :] = jnp.dot(pc[i, :, kc], vs[rk, :], preferred_element_type=F32)

    def zcol(off):
        return pl.BlockSpec((None, SEQ, W), lambda b, h: (b, 0, (off + g * GROUP_W) // W + h))

    tab = pl.BlockSpec((SEQ, HEAD), lambda b, h: (0, 0))
    out = pl.BlockSpec((None, SEQ, W), lambda b, h: (b, 0, h))
    return _call(
        body, name=name, grid=(Bn, nh),
        in_specs=[zcol(0), zcol(OFF_K), zcol(OFF_V), tab, tab, tab],
        out_specs=[out, out],
        out_shape=[_sds((Bn, SEQ, GROUP_W), F32), _sds((Bn, SEQ, GROUP_W), F32)],
        scratch_shapes=[pltpu.VMEM((SEQ, W), BF16)] * 3
        + [pltpu.VMEM((SEQ // HEAD, HEAD, 2 * HEAD), F32), pltpu.VMEM((SEQ // HEAD, HEAD, 2 * HEAD), BF16)],
        comm=comm,
    )(z3, z3, z3, *tabs)


def _attn_bwd(name, z3, tabs, g, d, do3, lse3, delta3, dz3, comm=None):
    Bn = z3.shape[0]
    L = SEQ // d
    nb = L // HEAD
    W, nh = ATT_W, GROUP_W // ATT_W

    def body(q_ref, k_ref, v_ref, tc_ref, ta_ref, tb_ref, do_ref, l_ref, dl_ref, dz_in, dz_ref,
             qs, ks, vs, dos, dqs, dks, dvs, nat, oq, ok, ov, sc, dpc, pc, dsc, sems):
        b, h = pl.program_id(0), pl.program_id(1)
        m_first, m_both = _band_masks()
        _attn_load(q_ref, k_ref, v_ref, tc_ref, ta_ref, tb_ref, qs, ks, vs, d)
        for r in range(d):
            dos[r * L:(r + 1) * L, :] = do_ref[_res_rows(r, L, d), :].astype(BF16)
        dks[...] = jnp.zeros_like(dks)
        dvs[...] = jnp.zeros_like(dvs)
        blocks = [(r, n) for r in range(d) for n in range(nb)]

        def spans(r, n):
            rq = slice(r * L + n * HEAD, r * L + (n + 1) * HEAD)
            rk = slice(r * L + max(n - 1, 0) * HEAD, r * L + (n + 1) * HEAD)
            return rq, rk, slice(0, HEAD if n == 0 else 2 * HEAD)

        for i, (r, n) in enumerate(blocks):
            rq, rk, kc = spans(r, n)
            sc[i, :, kc] = lax.dot_general(qs[rq, :], ks[rk, :], _NT, preferred_element_type=F32)
            dpc[i, :, kc] = lax.dot_general(dos[rq, :], vs[rk, :], _NT, preferred_element_type=F32)
        for i, (r, n) in enumerate(blocks):
            rq, rk, kc = spans(r, n)
            rows = _res_rows(r, HEAD, d, n * HEAD)
            s = jnp.where(m_first if n == 0 else m_both, sc[i, :, kc] * _SCALE, NEG_INF)
            p = jnp.exp(s - l_ref[rows, :][:, 0:1])
            pc[i, :, kc] = p.astype(BF16)
            dsc[i, :, kc] = (p * (dpc[i, :, kc] - dl_ref[rows, :][:, 0:1]) * _SCALE).astype(BF16)
        for i, (r, n) in enumerate(blocks):
            rq, rk, kc = spans(r, n)
            dqs[rq, :] = jnp.dot(dsc[i, :, kc], ks[rk, :], preferred_element_type=F32)
        for i, (r, n) in enumerate(blocks):
            rq, rk, kc = spans(r, n)
            dks[rk, :] += lax.dot_general(dsc[i, :, kc], qs[rq, :], _TN, preferred_element_type=F32)
            dvs[rk, :] += lax.dot_general(pc[i, :, kc], dos[rq, :], _TN, preferred_element_type=F32)
        step = b * nh + h

        def writes():
            base = g * GROUP_W + h * W
            return [pltpu.make_async_copy(src, dz_ref.at[b, :, pl.ds(pl.multiple_of(base + off, HEAD), W)],
                                          sems.at[i])
                    for i, (src, off) in enumerate(((oq, 0), (ok, OFF_K), (ov, OFF_V)))]

        @pl.when(step > 0)
        def _():
            for cp in writes():
                cp.wait()

        for src, dst, rotate in ((dqs, oq, True), (dks, ok, True), (dvs, ov, False)):
            for r in range(d):
                val = src[r * L:(r + 1) * L, :]
                if rotate:
                    rm = slice(r * L, (r + 1) * L)
                    val = _rot_t(val, tc_ref[rm, :], ta_ref[rm, :], tb_ref[rm, :])
                nat[_res_rows(r, L, d), :] = val
            dst[...] = nat[...].astype(BF16)
        for cp in writes():
            cp.start()

        @pl.when(step == Bn * nh - 1)
        def _():
            for cp in writes():
                cp.wait()

    def zcol(off):
        return pl.BlockSpec((None, SEQ, W), lambda b, h: (b, 0, (off + g * GROUP_W) // W + h))

    tab = pl.BlockSpec((SEQ, HEAD), lambda b, h: (0, 0))
    gcol = pl.BlockSpec((None, SEQ, W), lambda b, h: (b, 0, h))
    any_spec = pl.BlockSpec(memory_space=pl.ANY)
    return _call(
        body, name=name, grid=(Bn, nh),
        in_specs=[zcol(0), zcol(OFF_K), zcol(OFF_V), tab, tab, tab, gcol, gcol, gcol, any_spec],
        out_specs=any_spec,
        out_shape=_sds((Bn, SEQ, ZW), BF16),
        scratch_shapes=[pltpu.VMEM((SEQ, W), BF16)] * 4 + [pltpu.VMEM((SEQ, W), F32)] * 4
        + [pltpu.VMEM((SEQ, W), BF16)] * 3
        + [pltpu.VMEM((SEQ // HEAD, HEAD, 2 * HEAD), F32)] * 2 + [pltpu.VMEM((SEQ // HEAD, HEAD, 2 * HEAD), BF16)] * 2
        + [pltpu.SemaphoreType.DMA((3,))],
        aliases={9: 0}, comm=comm,
    )(z3, z3, z3, *tabs, do3, lse3, delta3, dz3)


def _merge_weights(l0, l1, l2):
    m = jnp.maximum(jnp.maximum(l0, l1), l2)
    e0, e1, e2 = jnp.exp(l0 - m), jnp.exp(l1 - m), jnp.exp(l2 - m)
    inv = 1.0 / (e0 + e1 + e2)
    return e0 * inv, e1 * inv, e2 * inv


def _merge_fwd(name, outs, lses, tr=512):
    T = outs[0].shape[0]

    def body(o0, o1, o2, l0, l1, l2, a_ref):
        w0, w1, w2 = _merge_weights(l0[...], l1[...], l2[...])
        a_ref[...] = (w0 * o0[...] + w1 * o1[...] + w2 * o2[...]).astype(BF16)

    row = pl.BlockSpec((tr, GROUP_W), lambda i: (i, 0))
    return pl.pallas_call(
        body, name=name, grid=(T // tr,), in_specs=[row] * 6, out_specs=row,
        out_shape=_sds((T, GROUP_W), BF16), compiler_params=_params(1),
    )(*outs, *lses)


def _merge_bwd(name, outs, lses, dattn, tr=512):
    T = outs[0].shape[0]

    def body(o0, o1, o2, l0, l1, l2, da_ref, d0, d1, d2, e0, e1, e2):
        w = _merge_weights(l0[...], l1[...], l2[...])
        da = da_ref[...]
        attn = w[0] * o0[...] + w[1] * o1[...] + w[2] * o2[...]
        prod = da * attn
        csum = jnp.concatenate(
            [jnp.broadcast_to(jnp.sum(prod[:, hh * HEAD:(hh + 1) * HEAD], axis=-1, keepdims=True), (tr, HEAD))
             for hh in range(GROUP_W // HEAD)], axis=1)
        for wg, d_ref, e_ref in zip(w, (d0, d1, d2), (e0, e1, e2)):
            d_ref[...] = wg * da
            e_ref[...] = wg * csum

    row = pl.BlockSpec((tr, GROUP_W), lambda i: (i, 0))
    res = pl.pallas_call(
        body, name=name, grid=(T // tr,), in_specs=[row] * 7, out_specs=[row] * 6,
        out_shape=[_sds((T, GROUP_W), F32)] * 6,
        compiler_params=_params(1),
    )(*outs, *lses, dattn)
    return res[:3], res[3:]


def _rms_rows(xv, g):
    r = lax.rsqrt(jnp.mean(xv * xv, axis=-1, keepdims=True) + RMS_EPS)
    return (xv * r * g).astype(BF16)


def _epi_norm(val, x_refs, o_ref, y_refs, first_rows):
    o_ref[...] = val
    y_refs[0][...] = _rms_rows(val, x_refs[0][...])


def _epi_ple(val, x_refs, o_ref, y_refs, first_rows):
    o_ref[...] = val
    xn = x_refs[0][...] + x_refs[1][...] * _sigmoid(val)
    y_refs[0][...] = xn
    if len(y_refs) > 1:
        y_refs[1][...] = _rms_rows(xn, x_refs[2][...])


def _epi_norm_bwd(val, x_refs, o_ref, y_refs, first_rows):
    xv = x_refs[0][...]
    r = lax.rsqrt(jnp.mean(xv * xv, axis=-1, keepdims=True) + RMS_EPS)
    xh = xv * r
    part = _rows8(val * xh)

    @pl.when(first_rows)
    def _():
        y_refs[0][...] = part

    @pl.when(jnp.logical_not(first_rows))
    def _():
        y_refs[0][...] += part

    dxh = val * x_refs[2][...]
    o_ref[...] = x_refs[1][...] + r * (dxh - xh * jnp.mean(dxh * xh, axis=-1, keepdims=True))


def _local_step(x3, p4, tgt3, vecs, ex):
    Bn = x3.shape[0]
    T = Bn * SEQ
    x = x3.reshape(T, D)
    tgt = tgt3.reshape(T, D)
    pb = p4.astype(BF16).reshape(DEPTH, T, PLE)
    tabs = {}
    for d in DILATIONS:
        tabs[d] = [t.reshape(SEQ // d, d, HEAD).transpose(1, 0, 2).reshape(SEQ, HEAD) for t in _rope_tables()]
    tm = 1024 if T % 1024 == 0 else 512
    nt = T // tm
    tk = 1024 if T % 1024 == 0 else 512
    ntk = T // tk
    tm5 = 512
    f32o = lambda n: _sds((T, n), F32)

    def spec(shape, fn):
        return pl.BlockSpec(shape, fn)

    def _mm(name, *args, **kwargs):
        return _mm_call(name, *args, comm=ex.hook(name), **kwargs)

    def cols4(a_ref, b_ref):
        av = a_ref[...]
        return jnp.concatenate([_dot(av, b_ref[j]) for j in range(NCHIP)], axis=1)

    def rows4(a_ref, b_ref):
        av = a_ref[...]
        return jnp.concatenate([_dot(av, b_ref[:, j * 256:(j + 1) * 256], ta=True) for j in range(NCHIP)], axis=0)

    def kchunks4(a_ref, b_ref):
        total = _dot(a_ref[:, 0:256], b_ref[0], tb=True)
        for j in range(1, NCHIP):
            total = total + _dot(a_ref[:, j * 256:(j + 1) * 256], b_ref[j], tb=True)
        return total

    row5 = spec((tm5, D), lambda i, *_: (i, 0))
    rowm = spec((tm, D), lambda i, *_: (i, 0))
    gain = spec((1, D), lambda *_: (0, 0))
    bf_rows = (_sds((T, D), BF16), row5)

    saved = []
    h = _rms_fwd("rms_mix0", x, vecs["g_mix"][0:1])
    for l in range(DEPTH):
        L = str(l)
        G = ex.weights(l)
        g_mix, g_ffn, g_ple = (vecs[k][l:l + 1] for k in ("g_mix", "g_ffn", "g_ple"))
        pscale, cb, cw = vecs["pool_scale"][l:l + 1], vecs["conv_b"][l:l + 1], vecs["conv_w"][l]
        z = _mm("mm_z" + L, h, G["in"], grid=(nt, NCHIP),
                a_spec=spec((tm, D), lambda i, n: (i, 0)),
                b_spec=spec((None, D, IN_S), lambda i, n: (n, 0, 0)),
                o_spec=spec((tm, IN_S), lambda i, n: (i, n)), out_shape=f32o(ZW))
        z3 = z.reshape(Bn, SEQ, ZW)
        outs, lses = [], []
        for g, d in enumerate(DILATIONS):
            o_g, l_g = _attn_fwd("attn_fwd%d_%d" % (g, l), z3, tabs[d], g, d, comm=ex.hook("attn_fwd%d_%d" % (g, l)))
            outs.append(o_g.reshape(T, GROUP_W))
            lses.append(l_g.reshape(T, GROUP_W))
        attn = _merge_fwd("merge_fwd" + L, outs, lses)
        ya = _mm("mm_ya" + L, attn, G["sc"], grid=(nt,), compute=cols4,
                 a_spec=spec((tm, GROUP_W), lambda i: (i, 0)),
                 b_spec=spec((NCHIP, GROUP_W, 256), lambda i: (0, 0, 0)),
                 o_spec=spec((tm, D), lambda i: (i, 0)), out_shape=_sds((T, D), BF16))
        pooled3, ms3 = _pool_fwd("pool_fwd" + L, z3, G["sc"], pscale)
        ms = ms3.reshape(T, D)

        def row_sharded(name, a, rb, res=None, kdim=D, out=F32, **fused):
            if rb is None:
                b_arr, b_spec = G["dn"], spec((NCHIP, DN_S, D), lambda i: (0, 0, 0))
            else:
                b_arr, b_spec = G["r3"], spec((NCHIP, 256, D), lambda i: (0, rb, 0))
            return _mm(name, a, b_arr, grid=(T // tm5,),
                       a_spec=spec((tm5, kdim), lambda i: (i, 0)), b_spec=b_spec,
                       o_spec=row5, out_shape=_sds((T, D), out), res=res, res_spec=None if res is None else row5,
                       **fused)

        yb = row_sharded("mm_yb" + L, ms, 0, out=BF16)
        merged = _gate_fwd("gate_fwd" + L, z, ya, yb)
        x1, h2 = row_sharded("mm_o" + L, merged, 1, res=x, epilogue=_epi_norm, extra_in=[(g_ffn, gain)],
                             extra_out=[bf_rows])
        u = _mm("mm_up" + L, h2, G["up"], grid=(nt, NCHIP),
                a_spec=spec((tm, D), lambda i, n: (i, 0)),
                b_spec=spec((None, D, UP_S), lambda i, n: (n, 0, 0)),
                o_spec=spec((tm, UP_S), lambda i, n: (i, n)), out_shape=f32o(UW))
        u3 = u.reshape(Bn, SEQ, UW)
        act = _conv_fwd("conv_fwd" + L, u3, cw, cb, comm=ex.hook("conv_fwd" + L)).reshape(T, FF)
        x2, h3 = row_sharded("mm_down" + L, act, None, res=x1, kdim=FF, epilogue=_epi_norm,
                             extra_in=[(g_ple, gain)], extra_out=[bf_rows])
        pe = _mm("mm_pe" + L, pb[l], G["sc"], grid=(nt,), compute=cols4,
                 a_spec=spec((tm, PLE), lambda i: (i, 0)),
                 b_spec=spec((NCHIP, 256, 256), lambda i: (0, 2, 0)),
                 o_spec=spec((tm, D), lambda i: (i, 0)), out_shape=f32o(D))
        fused_in = [(x2, row5), (pe, row5)]
        fused_out = [(f32o(D), row5)]
        if l + 1 < DEPTH:
            fused_in.append((vecs["g_mix"][l + 1:l + 2], gain))
            fused_out.append(bf_rows)
        pg, x3n, *h_next = row_sharded("mm_pg" + L, h3, 2, epilogue=_epi_ple, extra_in=fused_in,
                                       extra_out=fused_out)
        saved.append(dict(x=x, h=h, z=z, outs=outs, lses=lses, attn=attn, ya=ya, yb=yb, pooled3=pooled3, ms=ms,
                          merged=merged, x1=x1, h2=h2, u3=u3, act=act, x2=x2, h3=h3, pg=pg, pe=pe))
        x = x3n
        h = h_next[0] if h_next else None

    dx, dg_final8, sq8 = _final_loss(x, vecs["g_final"].reshape(1, D), tgt)

    gg_shape = {k: _sds(G[k].shape, F32) for k in G}
    small = {"g_final": dg_final8}

    for l in reversed(range(DEPTH)):
        L = str(l)
        sv = saved[l]
        G = ex.weights(l)
        GG = dict.fromkeys(_KEYS)
        g_mix, g_ffn, g_ple = (vecs[k][l:l + 1] for k in ("g_mix", "g_ffn", "g_ple"))
        pscale, cb, cw = vecs["pool_scale"][l:l + 1], vecs["conv_b"][l:l + 1], vecs["conv_w"][l]

        def wgrad_rows(name, a, b_arr, key, rb):
            GG[key] = _mm(name, a, b_arr, grid=(2, ntk), ta=True, k_axis=1, nk=ntk, acc_shape=(D, 512),
                          a_spec=spec((tk, D), lambda n, k: (k, 0)),
                          b_spec=spec((tk, 512), lambda n, k: (k, n)),
                          o_spec=spec((NCHIP, 256, 512), lambda n, k: (0, rb, n)),
                          out_shape=gg_shape[key], buf=GG[key])

        def dgrad_rows(name, dy, rb, out=F32, **fused):
            return _mm(name, dy, G["r3"], grid=(T // tm5,), tb=True,
                       a_spec=spec((tm5, D), lambda i: (i, 0)),
                       b_spec=spec((NCHIP, 256, D), lambda i: (0, rb, 0)),
                       o_spec=row5, out_shape=_sds((T, D), out), **fused)

        def norm_bwd(xin, dres, g, rows=row5):
            return dict(epilogue=_epi_norm_bwd, extra_in=[(xin, rows), (dres, rows), (g, gain)],
                        extra_out=[(_sds((8, D), F32), spec((8, D), lambda *_: (0, 0)))])

        dpe, dpg = _ple_bwd("ple_bwd" + L, dx, sv["pe"], sv["pg"])
        GG["sc"] = _mm("wg_ple" + L, pb[l], dpe, grid=(ntk,), compute=rows4, k_axis=0, nk=ntk,
                       acc_shape=(NCHIP * PLE, 256),
                       a_spec=spec((tk, PLE), lambda k: (k, 0)), b_spec=spec((tk, D), lambda k: (k, 0)),
                       o_spec=spec((NCHIP, 256, 256), lambda k: (0, 2, 0)),
                       out_shape=gg_shape["sc"], buf=GG["sc"])
        wgrad_rows("wg_pg" + L, sv["h3"], dpg, "r3", 2)
        dx, small["g_ple" + L] = dgrad_rows("dg_pg" + L, dpg, 2, **norm_bwd(sv["x2"], dx, g_ple))

        da = _mm("dg_down" + L, dx, G["dn"], grid=(T // 256,), tb=True,
                 a_spec=spec((256, D), lambda i: (i, 0)),
                 b_spec=spec((NCHIP, DN_S, D), lambda i: (0, 0, 0)),
                 o_spec=spec((256, FF), lambda i: (i, 0)), out_shape=_sds((T, FF), BF16))
        GG["dn"] = _mm("wg_down" + L, sv["act"], dx, grid=(2, ntk), ta=True, k_axis=1, nk=ntk,
                       acc_shape=(FF, 512),
                       a_spec=spec((tk, FF), lambda n, k: (k, 0)), b_spec=spec((tk, 512), lambda n, k: (k, n)),
                       o_spec=spec((NCHIP, DN_S, 512), lambda n, k: (0, 0, n)),
                       out_shape=gg_shape["dn"], buf=GG["dn"])
        du3, dcw, dcb = _conv_bwd("conv_bwd" + L, da.reshape(Bn, SEQ, FF), sv["u3"], cw, cb,
                                  comm=ex.hook("conv_bwd" + L))
        small["conv_w" + L], small["conv_b" + L] = dcw, dcb
        du = du3.reshape(T, UW)
        dx, small["g_ffn" + L] = _mm(
            "dg_up" + L, du, G["up"], grid=(nt, NCHIP), tb=True, k_axis=1, nk=NCHIP, acc_shape=(tm, D),
            a_spec=spec((tm, UP_S), lambda i, k: (i, k)), b_spec=spec((None, D, UP_S), lambda i, k: (k, 0, 0)),
            o_spec=rowm, out_shape=f32o(D), vmem=VMEM_BIG, **norm_bwd(sv["x1"], dx, g_ffn, rowm))
        GG["up"] = _mm("wg_up" + L, sv["h2"], du, grid=(NCHIP, ntk), ta=True, k_axis=1, nk=ntk,
                       acc_shape=(D, UP_S),
                       a_spec=spec((tk, D), lambda j, k: (k, 0)),
                       b_spec=spec((tk, UP_S), lambda j, k: (k, j)),
                       o_spec=spec((None, D, UP_S), lambda j, k: (j, 0, 0)),
                       out_shape=gg_shape["up"], buf=GG["up"])

        dmerged = dgrad_rows("dg_o" + L, dx, 1, out=BF16)
        wgrad_rows("wg_o" + L, sv["merged"], dx, "r3", 1)
        dya, dz = _gate_bwd("gate_bwd_a" + L, sv["z"], OFF_GA, sv["ya"], dmerged, None)
        dyb, dz = _gate_bwd("gate_bwd_b" + L, sv["z"], OFF_GB, sv["yb"], dmerged, dz)
        dms = dgrad_rows("dg_yb" + L, dyb, 0)
        wgrad_rows("wg_yb" + L, sv["ms"], dyb, "r3", 0)
        dz3, GG["sc"], small["pool_scale" + L] = _pool_bwd(
            "pool_bwd" + L, dms.reshape(Bn, SEQ, D), sv["pooled3"], G["sc"], pscale,
            dz.reshape(Bn, SEQ, ZW), GG["sc"])
        dattn = _mm("dg_ya" + L, dya, G["sc"], grid=(nt,), compute=kchunks4,
                    a_spec=spec((tm, D), lambda i: (i, 0)),
                    b_spec=spec((NCHIP, GROUP_W, 256), lambda i: (0, 0, 0)),
                    o_spec=spec((tm, GROUP_W), lambda i: (i, 0)), out_shape=f32o(GROUP_W))
        GG["sc"] = _mm("wg_ya" + L, sv["attn"], dya, grid=(ntk,), compute=rows4, k_axis=0, nk=ntk,
                       acc_shape=(NCHIP * GROUP_W, 256),
                       a_spec=spec((tk, GROUP_W), lambda k: (k, 0)), b_spec=spec((tk, D), lambda k: (k, 0)),
                       o_spec=spec((NCHIP, GROUP_W, 256), lambda k: (0, 0, 0)),
                       out_shape=gg_shape["sc"], buf=GG["sc"])
        ex.grads_ready(l, {k: GG[k] for k in _KEYS[1:]})
        dos, deltas = _merge_bwd("merge_bwd" + L, sv["outs"], sv["lses"], dattn)
        view3 = lambda t: t.reshape(Bn, SEQ, GROUP_W)
        sz3 = sv["z"].reshape(Bn, SEQ, ZW)
        for g, d in enumerate(DILATIONS):
            dz3 = _attn_bwd("attn_bwd%d_%d" % (g, l), sz3, tabs[d], g, d, view3(dos[g]), view3(sv["lses"][g]),
                            view3(deltas[g]), dz3, comm=ex.hook("attn_bwd%d_%d" % (g, l)))
        dz = dz3.reshape(T, ZW)
        GG["in"] = _mm("wg_z" + L, sv["h"], dz, grid=(NCHIP, ntk), ta=True, k_axis=1, nk=ntk,
                       acc_shape=(D, IN_S),
                       a_spec=spec((tk, D), lambda n, k: (k, 0)), b_spec=spec((tk, IN_S), lambda n, k: (k, n)),
                       o_spec=spec((None, D, IN_S), lambda n, k: (n, 0, 0)),
                       out_shape=gg_shape["in"], buf=GG["in"])
        ex.grads_ready(l, {"in": GG["in"]})

        def dgrad_z(name, first, count, dres, buf):
            rows = spec((tm, D), lambda i, *_: (i + first, 0))
            return _mm(name, dz, G["in"], grid=(count, NCHIP), tb=True, k_axis=1, nk=NCHIP, acc_shape=(tm, D),
                       a_spec=spec((tm, IN_S), lambda i, k: (i + first, k)),
                       b_spec=spec((None, D, IN_S), lambda i, k: (k, 0, 0)),
                       o_spec=rows, out_shape=f32o(D), vmem=VMEM_BIG, buf=buf, **norm_bwd(sv["x"], dres, g_mix, rows))

        if l == 0 and nt % 2 == 0:
            dx_a, part_a = dgrad_z("dg_z0a", 0, nt // 2, dx, None)
            dx, part_b = dgrad_z("dg_z0b", nt // 2, nt // 2, dx, dx_a)
            small["g_mix" + L] = part_a + part_b
        else:
            dx, small["g_mix" + L] = dgrad_z("dg_z" + L, 0, nt, dx, None)

    return sq8, dx.reshape(Bn, SEQ, D), small


_ANY = pl.BlockSpec(memory_space=pl.ANY)


def _place():
    x, y, c = lax.axis_index("x"), lax.axis_index("y"), lax.axis_index("c")
    chips = [(1 - x, y), (x, 1 - y), (1 - x, 1 - y)]
    return x, y, c, 2 * x + y, chips


def _half(rows, cc):
    return pl.ds(cc * (rows // 2), rows // 2)


def _remote(src, dst, send_sems, recv_sems, i, to):
    return pltpu.make_async_remote_copy(src_ref=src, dst_ref=dst, send_sem=send_sems.at[i], recv_sem=recv_sems.at[i],
                                        device_id=to, device_id_type=MESH)


def _exchange_gather_ici(stacks, done):
    n = len(stacks)
    rows = [t.shape[1] for t in stacks]

    def start(refs, send_sems, recv_sems):
        x, y, c, me, chips = _place()
        for k in range(n):
            part = refs[k].at[me, _half(rows[k], c)]
            for j, chip in enumerate(chips):
                _remote(part, part, send_sems, recv_sems, 3 * k + j, (*chip, c)).start()

    def wait(refs, send_sems, recv_sems):
        x, y, c, me, chips = _place()
        for k in range(n):
            for j, chip in enumerate(chips):
                part = refs[k].at[2 * chip[0] + chip[1], _half(rows[k], c)]
                _remote(part, part, send_sems, recv_sems, 3 * k + j, (*chip, c)).wait()

    return dict(arrays=list(stacks), nsem=3 * n, start=start, wait=wait, done=done)


def _exchange_gather_d2d(stacks, done):
    n = len(stacks)
    rows = [t.shape[1] for t in stacks]

    def copies(refs, send_sems, recv_sems, mine):
        x, y, c, me, chips = _place()
        cc = c if mine else 1 - c
        return [_remote(part, part, send_sems, recv_sems, 3 * k + j, (x, y, 1 - c))
                for k in range(n) for j, chip in enumerate(chips)
                for part in [refs[k].at[2 * chip[0] + chip[1], _half(rows[k], cc)]]]

    def start(refs, send_sems, recv_sems):
        for cp in copies(refs, send_sems, recv_sems, True):
            cp.start()

    def wait(refs, send_sems, recv_sems):
        for cp in copies(refs, send_sems, recv_sems, False):
            cp.wait()

    return dict(arrays=list(stacks), nsem=3 * n, start=start, wait=wait, done=done)


def _exchange_halves(g5, recv, done):
    n = len(g5)

    def copies(refs, send_sems, recv_sems):
        x, y, c, me, chips = _place()
        return [_remote(refs[k].at[:, 1 - c], refs[n + k], send_sems, recv_sems, k, (x, y, 1 - c)) for k in range(n)]

    def start(refs, send_sems, recv_sems):
        for cp in copies(refs, send_sems, recv_sems):
            cp.start()

    def wait(refs, send_sems, recv_sems):
        for cp in copies(refs, send_sems, recv_sems):
            cp.wait()

    return dict(arrays=list(g5) + list(recv), nsem=n, start=start, wait=wait, done=done)


def _exchange_chips(parts, landing, done):
    n = len(parts)

    def start(refs, send_sems, recv_sems):
        x, y, c, me, chips = _place()
        for k in range(n):
            for j, chip in enumerate(chips):
                _remote(refs[k].at[2 * chip[0] + chip[1]], refs[n + k].at[me], send_sems, recv_sems, 3 * k + j,
                        (*chip, c)).start()

    def wait(refs, send_sems, recv_sems):
        x, y, c, me, chips = _place()
        for k in range(n):
            for j, chip in enumerate(chips):
                slot = refs[n + k].at[2 * chip[0] + chip[1]]
                _remote(slot, slot, send_sems, recv_sems, 3 * k + j, (*chip, c)).wait()

    return dict(arrays=list(parts) + list(landing), nsem=3 * n, start=start, wait=wait, done=done)


def _exchange_share(full, layer, done):
    n = len(full)

    def copies(refs, send_sems, recv_sems, mine):
        x, y, c, me, chips = _place()
        cc = c if mine else 1 - c
        return [_remote(part, part, send_sems, recv_sems, k, (x, y, 1 - c))
                for k in range(n) for part in [refs[k].at[layer, cc]]]

    def start(refs, send_sems, recv_sems):
        for cp in copies(refs, send_sems, recv_sems, True):
            cp.start()

    def wait(refs, send_sems, recv_sems):
        for cp in copies(refs, send_sems, recv_sems, False):
            cp.wait()

    return dict(arrays=list(full), nsem=n, start=start, wait=wait, done=done)


def _exchange_call(name, comm):
    arrays = comm["arrays"]
    n = len(arrays)

    def body(*refs):
        outs, send_sems, recv_sems = refs[n:2 * n], refs[2 * n], refs[2 * n + 1]
        comm["start"](outs, send_sems, recv_sems)
        comm["wait"](outs, send_sems, recv_sems)

    outs = pl.pallas_call(
        body, name=name, in_specs=[_ANY] * n, out_specs=[_ANY] * n,
        out_shape=[_sds(t.shape, t.dtype) for t in arrays],
        scratch_shapes=[pltpu.SemaphoreType.DMA((comm["nsem"],))] * 2,
        input_output_aliases={i: i for i in range(n)},
    )(*arrays)
    comm["done"](outs)


def _gather_first(stacks, cw4):
    n = len(stacks)
    ici = _exchange_gather_ici(stacks, None)
    d2d = _exchange_gather_d2d(stacks, None)
    rows = [t.shape[1] for t in stacks]

    def body(*refs):
        g_refs, cwg_ref = refs[n + 1:2 * n + 1], refs[2 * n + 1]
        s_ici, r_ici, s_d2d, r_d2d, s_cw, r_cw = refs[2 * n + 2:]
        x, y, c, me, chips = _place()

        def cw_copy(j, slot, chip):
            part = cwg_ref.at[slot]
            return _remote(part, part, s_cw, r_cw, j, (*chip, c))

        ici["start"](g_refs, s_ici, r_ici)
        for j, chip in enumerate(chips):
            cw_copy(j, me, chip).start()
        for k in range(n):
            for j, chip in enumerate(chips):
                part = g_refs[k].at[2 * chip[0] + chip[1], _half(rows[k], c)]
                _remote(part, part, s_ici, r_ici, 3 * k + j, (*chip, c)).wait()
                _remote(part, part, s_d2d, r_d2d, 3 * k + j, (x, y, 1 - c)).start()
        d2d["wait"](g_refs, s_d2d, r_d2d)
        for j, chip in enumerate(chips):
            cw_copy(j, 2 * chip[0] + chip[1], chip).wait()

    outs = pl.pallas_call(
        body, name="gather_first", in_specs=[_ANY] * (n + 1), out_specs=[_ANY] * (n + 1),
        out_shape=[_sds(t.shape, t.dtype) for t in stacks] + [_sds(cw4.shape, cw4.dtype)],
        scratch_shapes=[pltpu.SemaphoreType.DMA((3 * n,))] * 4 + [pltpu.SemaphoreType.DMA((3,))] * 2,
        input_output_aliases={i: i for i in range(n + 1)},
    )(*stacks, cw4)
    return outs[:n], outs[n]


def _small_allreduce(small):
    def body(small_ref, red_ref, gath, s_send, s_recv):
        x, y, c, me, chips = _place()
        dev = 4 * x + 2 * y + c
        gath[dev] = small_ref[...]
        for r in range(1, 8):
            peer = (x ^ (r >> 2), y ^ ((r >> 1) & 1), c ^ (r & 1))
            _remote(small_ref, gath.at[dev], s_send, s_recv, r - 1, peer).start()
        for r in range(1, 8):
            peer = (x ^ (r >> 2), y ^ ((r >> 1) & 1), c ^ (r & 1))
            src = 4 * peer[0] + 2 * peer[1] + peer[2]
            _remote(small_ref, gath.at[src], s_send, s_recv, r - 1, peer).wait()
        total = gath[0]
        for i in range(1, 8):
            total = total + gath[i]
        red_ref[...] = total

    vm = pl.BlockSpec(memory_space=pltpu.VMEM)
    return pl.pallas_call(
        body, name="small_allreduce", in_specs=[vm], out_specs=vm, out_shape=_sds(small.shape, F32),
        scratch_shapes=[pltpu.VMEM((8,) + small.shape, F32), pltpu.SemaphoreType.DMA((7,)),
                        pltpu.SemaphoreType.DMA((7,))],
    )(small)


def _row_tile(rh):
    for cand in (512, 384, 352, 256, 128):
        if rh % cand == 0:
            return cand
    return rh


def _add_halves(name, g5, recv, place):
    _, _, rh, cols = g5.shape
    tr = _row_tile(rh)

    def body(place_ref, g_ref, r_ref, o_ref, own_ref):
        val = (g_ref[...] + r_ref[...]).astype(BF16)
        o_ref[...] = val

        @pl.when(pl.program_id(1) == place_ref[1])
        def _():
            own_ref[...] = val

    grid_spec = pltpu.PrefetchScalarGridSpec(
        num_scalar_prefetch=1, grid=(rh // tr, NCHIP),
        in_specs=[pl.BlockSpec((None, None, tr, cols), lambda i, j, pr: (j, pr[0], i, 0)),
                  pl.BlockSpec((None, tr, cols), lambda i, j, pr: (j, i, 0))],
        out_specs=[pl.BlockSpec((None, tr, cols), lambda i, j, pr: (j, i, 0)),
                   pl.BlockSpec((None, tr, cols), lambda i, j, pr: (pr[1], i, 0))])
    return pl.pallas_call(
        body, name=name, grid_spec=grid_spec, out_shape=[_sds(recv.shape, BF16)] * 2, compiler_params=_params(2),
    )(place, g5, recv)


def _sum_chips(name, landing, place, layer, full):
    _, rh, cols = landing.shape
    tr = _row_tile(rh)
    has_full = full is not None

    def body(*refs):
        r_ref, o_ref = refs[1], refs[-1]
        total = r_ref[0].astype(F32)
        for j in range(1, NCHIP):
            total = total + r_ref[j].astype(F32)
        o_ref[...] = total

    grid_spec = pltpu.PrefetchScalarGridSpec(
        num_scalar_prefetch=1, grid=(rh // tr,),
        in_specs=[pl.BlockSpec((NCHIP, tr, cols), lambda i, pr: (0, i, 0))] + ([_ANY] if has_full else []),
        out_specs=pl.BlockSpec((None, None, tr, cols), lambda i, pr: (layer, pr[0], i, 0)))
    return pl.pallas_call(
        body, name=name, grid_spec=grid_spec, out_shape=_sds((DEPTH, 2, rh, cols), F32),
        input_output_aliases={2: 0} if has_full else {}, compiler_params=_params(1),
    )(place, landing, *([full] if has_full else []))


class _Schedule:
    FIRST, REST = ["in"], list(_KEYS[1:])

    def __init__(self, slotted, cw4, place):
        self.place = place
        self._w = [dict(zip(_KEYS, layer)) for layer in slotted]
        got, self.cw4 = _gather_first([self._w[0][k] for k in self.FIRST], cw4)
        self._w[0].update(zip(self.FIRST, got))
        self._g5, self._recv, self._parts, self._landing = [{}, {}], [{}, {}], [{}, {}], [{}, {}]
        self.full = {}
        every = list(_KEYS)
        self._hooks = {
            "mm_z0": lambda: self._gather(_exchange_gather_ici, 0, self.REST),
            "attn_fwd0_0": lambda: self._gather(_exchange_gather_d2d, 0, self.REST),
            "mm_up0": lambda: self._gather(_exchange_gather_ici, 1, self.REST),
            "conv_fwd0": lambda: self._gather(_exchange_gather_ici, 1, self.FIRST),
            "mm_down0": lambda: self._gather(_exchange_gather_d2d, 1, every),
            "dg_down0": lambda: self._halves(1, every),
            "conv_bwd0": lambda: self._chips(1, every),
            "dg_up0": lambda: self._share(1, every),
            "attn_bwd0_0": lambda: self._halves(0, self.REST),
            "wg_z0": lambda: self._chips(0, self.REST),
            "dg_z0a": lambda: self._halves(0, self.FIRST),
            "dg_z0b": lambda: self._chips(0, self.FIRST),
        }

    def weights(self, layer):
        return self._w[layer]

    def hook(self, name):
        make = self._hooks.get(name)
        return make() if make else None

    def grads_ready(self, layer, GG):
        for k, t in GG.items():
            g5 = t.reshape(NCHIP, 2, t.shape[1] // 2, t.shape[2])
            self._g5[layer][k] = g5
            self._recv[layer][k] = lax.empty((NCHIP,) + g5.shape[2:], F32)

    def _gather(self, make, layer, keys):
        def done(arrays):
            self._w[layer].update(zip(keys, arrays))
        return make([self._w[layer][k] for k in keys], done)

    def _halves(self, layer, keys):
        return _exchange_halves([self._g5[layer][k] for k in keys], [self._recv[layer][k] for k in keys],
                                lambda arrays: self._halves_done(layer, keys, arrays))

    def _halves_done(self, layer, keys, arrays):
        n = len(keys)
        for k, g, r in zip(keys, arrays[:n], arrays[n:]):
            self._parts[layer][k], self._landing[layer][k] = _add_halves(
                "add_halves%d_%s" % (layer, k), g, r, self.place)

    def _chips(self, layer, keys):
        return _exchange_chips([self._parts[layer][k] for k in keys], [self._landing[layer][k] for k in keys],
                               lambda arrays: self._chips_done(layer, keys, arrays))

    def _chips_done(self, layer, keys, arrays):
        for k, t in zip(keys, arrays[len(keys):]):
            self.full[k] = _sum_chips("sum_chips%d_%s" % (layer, k), t, self.place, layer, self.full.get(k))

    def _share(self, layer, keys):
        def done(arrays):
            self.full.update(zip(keys, arrays))
        return _exchange_share([self.full[k] for k in keys], layer, done)

    def finish(self, small):
        small_red = _small_allreduce(small)
        _exchange_call("share_halves_last", self._share(0, list(_KEYS)))
        return [self.full[k] for k in _KEYS], small_red


def _adamw(name, w, g, m, v):
    shape = w.shape
    cols = shape[-1]
    rows = 1
    for s in shape[:-1]:
        rows *= s
    tr = rows
    for cand in (256, 128, 64):
        if rows > cand and rows % cand == 0:
            tr = cand
            break
    c1 = 1.0 / (1.0 - B1 ** STEP)
    c2 = 1.0 / (1.0 - B2 ** STEP)

    def body(w_ref, g_ref, m_ref, v_ref, go_ref, d_ref, nm_ref, nv_ref):
        gv = g_ref[...]
        go_ref[...] = gv
        nm = B1 * m_ref[...] + (1.0 - B1) * gv
        nv = B2 * v_ref[...] + (1.0 - B2) * (gv * gv)
        nm_ref[...] = nm
        nv_ref[...] = nv
        d_ref[...] = -LR * ((nm * c1) / (jnp.sqrt(nv * c2) + ADAM_EPS) + WD * w_ref[...])

    blk = pl.BlockSpec((tr, cols), lambda i: (i, 0))
    outs = pl.pallas_call(
        body, name=name, grid=(rows // tr,), in_specs=[blk] * 4, out_specs=[blk] * 4,
        out_shape=[_sds((rows, cols), F32)] * 4, compiler_params=_params(1),
    )(*(t.reshape(rows, cols) for t in (w, g, m, v)))
    return tuple(o.reshape(shape) for o in outs)


def _pack_small(small):
    rows = [jnp.sum(small["g_mix%d" % l], axis=0, keepdims=True) for l in range(DEPTH)]
    rows += [jnp.sum(small["pool_scale%d" % l], axis=0, keepdims=True) for l in range(DEPTH)]
    rows += [jnp.sum(small["g_ffn%d" % l], axis=0, keepdims=True) for l in range(DEPTH)]
    rows += [jnp.sum(small["g_ple%d" % l], axis=0, keepdims=True) for l in range(DEPTH)]
    rows += [jnp.sum(small["g_final"], axis=0, keepdims=True)]
    flat = [small["conv_b%d" % l].reshape(-1) for l in range(DEPTH)]
    flat += [small["conv_w%d" % l].reshape(-1) for l in range(DEPTH)]
    flat = jnp.concatenate(flat).reshape(-1, D)
    packed = jnp.concatenate(rows + [flat], axis=0)
    return jnp.pad(packed, ((0, SMALL_ROWS - packed.shape[0]), (0, 0)))


def _unpack_small(red):
    g_mix, pool_scale, g_ffn, g_ple = red[0:2], red[2:4], red[4:6], red[6:8]
    g_final = red[8]
    nb = DEPTH * UW // D
    conv_b = red[9:9 + nb].reshape(DEPTH, UW)
    conv_w = red[9 + nb:9 + 4 * nb].reshape(DEPTH, 3, UW)
    return g_mix, pool_scale, g_ffn, g_ple, g_final, conv_b, conv_w


def kernel(x, p, g_mix, w_in, w_ya, w_yb, pool_w, pool_scale, w_o, g_ffn, w_up, conv_w, conv_b, w_down, g_ple, w_ple, w_ple_gate, g_final, loss_target, m_g_mix, m_w_in, m_w_ya, m_w_yb, m_pool_w, m_pool_scale, m_w_o, m_g_ffn, m_w_up, m_conv_w, m_conv_b, m_w_down, m_g_ple, m_w_ple, m_w_ple_gate, m_g_final, v_g_mix, v_w_in, v_w_ya, v_w_yb, v_pool_w, v_pool_scale, v_w_o, v_g_ffn, v_w_up, v_conv_w, v_conv_b, v_w_down, v_g_ple, v_w_ple, v_w_ple_gate, v_g_final):
    me = 2 * lax.axis_index("x") + lax.axis_index("y")
    place = jnp.stack([lax.axis_index("c"), me]).astype(jnp.int32)

    def slot(shard):
        return lax.dynamic_update_index_in_dim(lax.empty((NCHIP,) + shard.shape, shard.dtype), shard, me, 0)

    packed = [
        w_in.astype(BF16), w_up.astype(BF16),
        jnp.concatenate([w_ya, w_ple, pool_w.reshape(DEPTH, 256, 256)], axis=1).astype(BF16),
        jnp.concatenate([w_yb, w_o, w_ple_gate], axis=1).astype(BF16),
        w_down.astype(BF16),
    ]
    slotted = [[slot(t[l]) for t in packed] for l in range(DEPTH)]
    ex = _Schedule(slotted, slot(conv_w.reshape(DEPTH * 3, UP_S)), place)
    cw_full = ex.cw4.reshape(NCHIP, DEPTH, 3, UP_S).transpose(1, 2, 0, 3).reshape(DEPTH, 3, UW)

    vecs = dict(g_mix=g_mix, pool_scale=pool_scale, g_ffn=g_ffn, g_ple=g_ple, g_final=g_final, conv_b=conv_b,
                conv_w=cw_full)
    sq8, grad_x, small = _local_step(x, p, loss_target, vecs, ex)
    loss = lax.psum(jnp.sum(sq8) * (0.5 / D), ("x", "y", "c"))

    full, small_red = ex.finish(_pack_small(small))
    r_in, r_up, r_sc, r_r3, r_dn = [f.reshape(DEPTH, -1, f.shape[-1]) for f in full]
    d_g_mix, d_pool_scale, d_g_ffn, d_g_ple, d_g_final, d_conv_b, d_conv_w_full = _unpack_small(small_red)
    d_conv_w = lax.dynamic_slice_in_dim(d_conv_w_full, me * UP_S, UP_S, axis=2)

    grads = dict(
        g_mix=d_g_mix, w_in=r_in, w_ya=r_sc[:, 0:512], w_yb=r_r3[:, 0:256],
        pool_w=r_sc[:, 768:1024].reshape(DEPTH, 4, 64, 256), pool_scale=d_pool_scale, w_o=r_r3[:, 256:512],
        g_ffn=d_g_ffn, w_up=r_up, conv_w=d_conv_w, conv_b=d_conv_b, w_down=r_dn, g_ple=d_g_ple,
        w_ple=r_sc[:, 512:768], w_ple_gate=r_r3[:, 512:768], g_final=d_g_final)
    weights = dict(g_mix=g_mix, w_in=w_in, w_ya=w_ya, w_yb=w_yb, pool_w=pool_w, pool_scale=pool_scale, w_o=w_o,
                   g_ffn=g_ffn, w_up=w_up, conv_w=conv_w, conv_b=conv_b, w_down=w_down, g_ple=g_ple, w_ple=w_ple,
                   w_ple_gate=w_ple_gate, g_final=g_final)
    m_in = dict(g_mix=m_g_mix, w_in=m_w_in, w_ya=m_w_ya, w_yb=m_w_yb, pool_w=m_pool_w, pool_scale=m_pool_scale,
                w_o=m_w_o, g_ffn=m_g_ffn, w_up=m_w_up, conv_w=m_conv_w, conv_b=m_conv_b, w_down=m_w_down,
                g_ple=m_g_ple, w_ple=m_w_ple, w_ple_gate=m_w_ple_gate, g_final=m_g_final)
    v_in = dict(g_mix=v_g_mix, w_in=v_w_in, w_ya=v_w_ya, w_yb=v_w_yb, pool_w=v_pool_w, pool_scale=v_pool_scale,
                w_o=v_w_o, g_ffn=v_g_ffn, w_up=v_w_up, conv_w=v_conv_w, conv_b=v_conv_b, w_down=v_w_down,
                g_ple=v_g_ple, w_ple=v_w_ple, w_ple_gate=v_w_ple_gate, g_final=v_g_final)
    names = ["g_mix", "w_in", "w_ya", "w_yb", "pool_w", "pool_scale", "w_o", "g_ffn", "w_up", "conv_w", "conv_b",
             "w_down", "g_ple", "w_ple", "w_ple_gate", "g_final"]
    deltas, new_m, new_v = [], [], []
    for nme in names:
        gr = grads[nme].reshape(weights[nme].shape)
        grads[nme], dlt, nm, nv = _adamw("adamw_" + nme, weights[nme], gr, m_in[nme], v_in[nme])
        deltas.append(dlt)
        new_m.append(nm)
        new_v.append(nv)
    return (loss, grad_x, *[grads[nme] for nme in names], *deltas, *new_m, *new_v)
```

```python
import math

import jax
import jax.numpy as jnp
from jax import lax
from jax.experimental import pallas as pl
from jax.experimental.pallas import tpu as pltpu

F32 = jnp.float32
BF16 = jnp.bfloat16
_KEYS = ("in", "up", "sc", "r3", "dn")
MESH = pl.DeviceIdType.MESH

D = 1024
SEQ = 2048
DEPTH = 2
HEAD = 128
GROUP_W = 512
DILATIONS = (1, 4, 16)
ROPE_DIM = 32
ROPE_THETA = 500000.0
NEG_INF = -1e30
ZW = 7680
OFF_K, OFF_V, OFF_U, OFF_GA, OFF_GB = 1536, 3072, 4608, 5632, 6656
FF = 2816
UW = 2 * FF
PLE = 256
NCHIP = 4
IN_S, UP_S, DN_S = ZW // NCHIP, UW // NCHIP, FF // NCHIP
RMS_EPS = 1e-6
LR, B1, B2, ADAM_EPS, WD, STEP = 0.001, 0.9, 0.999, 1e-08, 0.01, 10
SMALL_ROWS = 56
VMEM_CAP = 48 * 1024 * 1024
VMEM_BIG = 58 * 1024 * 1024


def _params(n_grid, vmem=VMEM_CAP):
    return pltpu.CompilerParams(dimension_semantics=("arbitrary",) * n_grid, vmem_limit_bytes=vmem)


def _sigmoid(v):
    return 1.0 / (1.0 + jnp.exp(-v))


def _rows8(v):
    return jnp.sum(v.reshape(v.shape[0] // 8, 8, v.shape[1]), axis=0)


def _sds(shape, dtype):
    return jax.ShapeDtypeStruct(shape, dtype)


def _dot(av, bv, ta=False, tb=False):
    dims = (((0,) if ta else (1,), (1,) if tb else (0,)), ((), ()))
    return lax.dot_general(av.astype(BF16), bv.astype(BF16), dims, preferred_element_type=F32)


def _call(body, *, name, grid, in_specs, out_specs, out_shape, scratch_shapes=(), aliases=None, comm=None,
          vmem=VMEM_CAP):
    params = _params(len(grid), vmem)
    aliases = dict(aliases or {})
    if comm is None:
        return pl.pallas_call(body, name=name, grid=grid, in_specs=list(in_specs), out_specs=out_specs,
                              out_shape=out_shape, scratch_shapes=list(scratch_shapes),
                              input_output_aliases=aliases, compiler_params=params)
    single = not isinstance(out_shape, (list, tuple))
    out_specs_l = [out_specs] if single else list(out_specs)
    out_shape_l = [out_shape] if single else list(out_shape)
    n_in, n_out, n_c = len(in_specs), len(out_shape_l), len(comm["arrays"])

    def hosted(*refs):
        core_in, core_out = refs[:n_in], refs[n_in + n_c:n_in + n_c + n_out]
        c_refs = refs[n_in + n_c + n_out:n_in + 2 * n_c + n_out]
        scratch, (send_sems, recv_sems) = refs[n_in + 2 * n_c + n_out:-2], refs[-2:]
        ids = [pl.program_id(i) for i in range(len(grid))]
        first, last = ids[0] == 0, ids[0] == grid[0] - 1
        for i in range(1, len(grid)):
            first = jnp.logical_and(first, ids[i] == 0)
            last = jnp.logical_and(last, ids[i] == grid[i] - 1)

        @pl.when(first)
        def _():
            comm["start"](c_refs, send_sems, recv_sems)

        body(*core_in, *core_out, *scratch)

        @pl.when(last)
        def _():
            comm["wait"](c_refs, send_sems, recv_sems)

    any_spec = pl.BlockSpec(memory_space=pl.ANY)
    for i in range(n_c):
        aliases[n_in + i] = n_out + i
    call = pl.pallas_call(
        hosted, name=name, grid=grid, in_specs=list(in_specs) + [any_spec] * n_c,
        out_specs=out_specs_l + [any_spec] * n_c,
        out_shape=out_shape_l + [_sds(t.shape, t.dtype) for t in comm["arrays"]],
        scratch_shapes=list(scratch_shapes) + [pltpu.SemaphoreType.DMA((comm["nsem"],))] * 2,
        input_output_aliases=aliases, compiler_params=params)

    def run(*args):
        outs = call(*args, *comm["arrays"])
        comm["done"](outs[n_out:])
        return outs[0] if single else outs[:n_out]

    return run


def _mm_call(name, a, b, *, grid, a_spec, b_spec, o_spec, out_shape, ta=False, tb=False, k_axis=None, nk=1,
             acc_shape=None, res=None, res_spec=None, buf=None, compute=None, comm=None,
             extra_in=(), extra_out=(), epilogue=None, vmem=VMEM_CAP):
    has_res, has_buf = res is not None, buf is not None
    in_place = nk > 1 and not has_res and out_shape.dtype == F32 and epilogue is None
    n_xi, n_xo = len(extra_in), len(extra_out)

    def body(*refs):
        a_ref, b_ref = refs[0], refs[1]
        pos = 2
        r_ref = None
        if has_res:
            r_ref = refs[pos]
            pos += 1
        if has_buf:
            pos += 1
        x_refs = refs[pos:pos + n_xi]
        pos += n_xi
        first_rows = pl.program_id(0) == 0
        o_ref = refs[pos]
        y_refs = refs[pos + 1:pos + 1 + n_xo]
        if compute is None:
            av = a_ref[...]
            bv = b_ref[...]
            part = _dot(av.reshape(-1, av.shape[-1]), bv.reshape(-1, bv.shape[-1]), ta, tb)
        else:
            part = compute(a_ref, b_ref)

        def finish(val):
            if r_ref is not None:
                val = val + r_ref[...]
            if epilogue is not None:
                epilogue(val, x_refs, o_ref, y_refs, first_rows)
            else:
                o_ref[...] = val.reshape(o_ref.shape).astype(o_ref.dtype)

        if nk == 1:
            finish(part)
        elif in_place:
            @pl.when(pl.program_id(k_axis) == 0)
            def _():
                o_ref[...] = jnp.zeros(o_ref.shape, F32)

            o_ref[...] += part.reshape(o_ref.shape)
        else:
            acc_ref = refs[pos + 1 + n_xo]
            k = pl.program_id(k_axis)

            @pl.when(k == 0)
            def _():
                acc_ref[...] = jnp.zeros(acc_ref.shape, F32)

            acc_ref[...] += part

            @pl.when(k == nk - 1)
            def _():
                finish(acc_ref[...])

    ins, in_specs = [a, b], [a_spec, b_spec]
    if has_res:
        ins.append(res)
        in_specs.append(res_spec)
    aliases = {}
    if has_buf:
        aliases = {len(ins): 0}
        ins.append(buf)
        in_specs.append(pl.BlockSpec(memory_space=pl.ANY))
    for arr, sp in extra_in:
        ins.append(arr)
        in_specs.append(sp)
    scratch = [pltpu.VMEM(acc_shape, F32)] if nk > 1 and not in_place else []
    if not extra_out:
        return _call(body, name=name, grid=grid, in_specs=in_specs, out_specs=o_spec, out_shape=out_shape,
                     scratch_shapes=scratch, aliases=aliases, comm=comm, vmem=vmem)(*ins)
    return _call(body, name=name, grid=grid, in_specs=in_specs, out_specs=[o_spec] + [sp for _, sp in extra_out],
                 out_shape=[out_shape] + [sh for sh, _ in extra_out], scratch_shapes=scratch, aliases=aliases,
                 comm=comm, vmem=vmem)(*ins)


def _rms_fwd(name, x, g, tr=512):
    T = x.shape[0]

    def body(x_ref, g_ref, h_ref):
        xv = x_ref[...]
        r = lax.rsqrt(jnp.mean(xv * xv, axis=-1, keepdims=True) + RMS_EPS)
        h_ref[...] = (xv * r * g_ref[...]).astype(BF16)

    return pl.pallas_call(
        body, name=name, grid=(T // tr,),
        in_specs=[pl.BlockSpec((tr, D), lambda i: (i, 0)), pl.BlockSpec((1, D), lambda i: (0, 0))],
        out_specs=pl.BlockSpec((tr, D), lambda i: (i, 0)), out_shape=_sds((T, D), BF16),
        compiler_params=_params(1),
    )(x, g)


def _final_loss(x, g, tgt, tr=512):
    T = x.shape[0]

    def body(x_ref, g_ref, t_ref, dx_ref, dg_ref, sq_ref):
        xv = x_ref[...]
        r = lax.rsqrt(jnp.mean(xv * xv, axis=-1, keepdims=True) + RMS_EPS)
        xh = xv * r
        gv = g_ref[...]
        e = xh * gv - t_ref[...]
        dy = e * (1.0 / D)
        pg = _rows8(dy * xh)
        ps = _rows8(e * e)

        @pl.when(pl.program_id(0) == 0)
        def _():
            dg_ref[...] = pg
            sq_ref[...] = ps

        @pl.when(pl.program_id(0) > 0)
        def _():
            dg_ref[...] += pg
            sq_ref[...] += ps

        dxh = dy * gv
        dx_ref[...] = r * (dxh - xh * jnp.mean(dxh * xh, axis=-1, keepdims=True))

    row = pl.BlockSpec((tr, D), lambda i: (i, 0))
    acc = pl.BlockSpec((8, D), lambda i: (0, 0))
    return pl.pallas_call(
        body, name="final_loss", grid=(T // tr,),
        in_specs=[row, pl.BlockSpec((1, D), lambda i: (0, 0)), row],
        out_specs=[row, acc, acc],
        out_shape=[_sds((T, D), F32), _sds((8, D), F32), _sds((8, D), F32)],
        compiler_params=_params(1),
    )(x, g, tgt)


def _ple_bwd(name, dx, pe, pg, tr=512):
    T = dx.shape[0]

    def body(dx_ref, pe_ref, pg_ref, dpe_ref, dpg_ref):
        s = _sigmoid(pg_ref[...])
        dxv = dx_ref[...]
        dpe_ref[...] = (dxv * s).astype(BF16)
        dpg_ref[...] = (dxv * pe_ref[...] * s * (1.0 - s)).astype(BF16)

    row = pl.BlockSpec((tr, D), lambda i: (i, 0))
    return pl.pallas_call(
        body, name=name, grid=(T // tr,), in_specs=[row, row, row], out_specs=[row, row],
        out_shape=[_sds((T, D), BF16), _sds((T, D), BF16)], compiler_params=_params(1),
    )(dx, pe, pg)


def _gate_fwd(name, z, ya, yb, tr=512):
    T = z.shape[0]
    w = 512

    def body(ga_ref, gb_ref, ya_ref, yb_ref, o_ref):
        o_ref[...] = (_sigmoid(ga_ref[...]) * ya_ref[...].astype(F32)
                      + _sigmoid(gb_ref[...]) * yb_ref[...].astype(F32)).astype(BF16)

    col = pl.BlockSpec((tr, w), lambda i, j: (i, j))
    return pl.pallas_call(
        body, name=name, grid=(T // tr, D // w),
        in_specs=[pl.BlockSpec((tr, w), lambda i, j: (i, OFF_GA // w + j)),
                  pl.BlockSpec((tr, w), lambda i, j: (i, OFF_GB // w + j)), col, col],
        out_specs=col, out_shape=_sds((T, D), BF16), compiler_params=_params(2),
    )(z, z, ya, yb)


def _gate_bwd(name, z, off, y, dm, dz, tr=512):
    T = z.shape[0]
    w = 512
    has_dz = dz is not None

    def body(*refs):
        g_ref, y_ref, dm_ref = refs[:3]
        dy_ref, dz_ref = refs[-2:]
        s = _sigmoid(g_ref[...])
        dmv = dm_ref[...].astype(F32)
        dy_ref[...] = (dmv * s).astype(BF16)
        dz_ref[...] = (dmv * y_ref[...].astype(F32) * s * (1.0 - s)).astype(BF16)

    col = pl.BlockSpec((tr, w), lambda i, j: (i, j))
    gcol = pl.BlockSpec((tr, w), lambda i, j: (i, off // w + j))
    ins, in_specs, aliases = [z, y, dm], [gcol, col, col], {}
    if has_dz:
        ins.append(dz)
        in_specs.append(pl.BlockSpec(memory_space=pl.ANY))
        aliases = {3: 1}
    return pl.pallas_call(
        body, name=name, grid=(T // tr, D // w), in_specs=in_specs, out_specs=[col, gcol],
        out_shape=[_sds((T, D), BF16), _sds((T, ZW), BF16)], input_output_aliases=aliases,
        compiler_params=_params(2),
    )(*ins)


def _shift_down(v, k, rows):
    return jnp.where(rows >= k, pltpu.roll(v, k, 0), 0.0)


def _shift_up(v, k, rows):
    n = v.shape[0]
    return jnp.where(rows < n - k, pltpu.roll(v, n - k, 0), 0.0)


def _pool_window(v, g, rows, shift):
    s2 = v + shift(v, 1, rows)
    s4 = s2 + shift(s2, 2, rows)
    s8 = s4 + shift(s4, 4, rows)
    s16 = s8 + shift(s8, 8, rows)
    return jnp.where(g == 0, s2, jnp.where(g == 1, s4, jnp.where(g == 2, s8, s16)))


def _pool_count(g, rows):
    wlen = jnp.left_shift(2, g).astype(F32)
    return jnp.minimum(rows.astype(F32) + 1.0, wlen)


def _pool_fwd(name, z3, g_sc, scale):
    Bn = z3.shape[0]
    gw = 256

    def body(u_ref, pw_ref, sc_ref, pooled_ref, ms_ref):
        g = pl.program_id(1)
        u = u_ref[...]
        rows = lax.broadcasted_iota(jnp.int32, u.shape, 0)
        pooled = (_pool_window(u, g, rows, _shift_down) / _pool_count(g, rows) - u).astype(BF16)
        pooled_ref[...] = pooled
        pw = pw_ref[...].reshape(gw, gw)
        mixed = jnp.dot(pooled, pw, preferred_element_type=F32)
        ms_ref[...] = (mixed * sc_ref[...]).astype(BF16)

    blk = pl.BlockSpec((None, SEQ, gw), lambda b, g: (b, 0, g))
    return pl.pallas_call(
        body, name=name, grid=(Bn, 4),
        in_specs=[pl.BlockSpec((None, SEQ, gw), lambda b, g: (b, 0, OFF_U // gw + g)),
                  pl.BlockSpec((NCHIP, 64, gw), lambda b, g: (0, 12 + g, 0)),
                  pl.BlockSpec((1, gw), lambda b, g: (0, g))],
        out_specs=[blk, blk],
        out_shape=[_sds((Bn, SEQ, D), BF16), _sds((Bn, SEQ, D), BF16)],
        compiler_params=_params(2),
    )(z3, g_sc, scale)


def _pool_bwd(name, dms3, pooled3, g_sc, scale, dz3, gg_sc):
    Bn = dms3.shape[0]
    gw = 256
    has_gg = gg_sc is not None

    def body(*refs):
        dms_ref, pooled_ref, pw_ref, sc_ref = refs[:4]
        dz_ref, dpw_ref, dsc_ref = refs[-3:]
        g, b = pl.program_id(0), pl.program_id(1)
        pooled = pooled_ref[...]
        pw = pw_ref[...].reshape(gw, gw)
        dms = dms_ref[...]
        mixed = jnp.dot(pooled, pw, preferred_element_type=F32)
        psc = _rows8(dms * mixed)
        dmixed = (dms * sc_ref[...]).astype(BF16)
        dpw = lax.dot_general(pooled, dmixed, (((0,), (0,)), ((), ())), preferred_element_type=F32)
        dpw = dpw.reshape(NCHIP, 64, gw)

        @pl.when(b == 0)
        def _():
            dsc_ref[...] = psc
            dpw_ref[...] = dpw

        @pl.when(b > 0)
        def _():
            dsc_ref[...] += psc
            dpw_ref[...] += dpw

        dpooled = lax.dot_general(dmixed, pw, (((1,), (1,)), ((), ())), preferred_element_type=F32)
        rows = lax.broadcasted_iota(jnp.int32, dpooled.shape, 0)
        dq = dpooled / _pool_count(g, rows)
        dz_ref[...] = (_pool_window(dq, g, rows, _shift_up) - dpooled).astype(BF16)

    ins = [dms3, pooled3, g_sc, scale, dz3]
    in_specs = [pl.BlockSpec((None, SEQ, gw), lambda g, b: (b, 0, g)),
                pl.BlockSpec((None, SEQ, gw), lambda g, b: (b, 0, g)),
                pl.BlockSpec((NCHIP, 64, gw), lambda g, b: (0, 12 + g, 0)),
                pl.BlockSpec((1, gw), lambda g, b: (0, g)),
                pl.BlockSpec(memory_space=pl.ANY)]
    aliases = {4: 0}
    if has_gg:
        ins.append(gg_sc)
        in_specs.append(pl.BlockSpec(memory_space=pl.ANY))
        aliases[5] = 1
    return pl.pallas_call(
        body, name=name, grid=(4, Bn), in_specs=in_specs,
        out_specs=[pl.BlockSpec((None, SEQ, gw), lambda g, b: (b, 0, OFF_U // gw + g)),
                   pl.BlockSpec((NCHIP, 64, gw), lambda g, b: (0, 12 + g, 0)),
                   pl.BlockSpec((8, gw), lambda g, b: (0, g))],
        out_shape=[_sds(dz3.shape, BF16), _sds((NCHIP, D, 256), F32), _sds((8, D), F32)],
        input_output_aliases=aliases, compiler_params=_params(2),
    )(*ins)


CT = 256
NCT = FF // CT


def _conv_pre(u, cw_ref, cb_ref, rows):
    return (cb_ref[...] + cw_ref[0:1, :] * _shift_down(u, 2, rows) + cw_ref[1:2, :] * _shift_down(u, 1, rows)
            + cw_ref[2:3, :] * u)


def _conv_fwd(name, u3, cw, cb, comm=None):
    Bn = u3.shape[0]

    def body(ug_ref, uv_ref, cwg_ref, cwv_ref, cbg_ref, cbv_ref, a_ref, yg_ref, yv_ref):
        ug, uv = ug_ref[...], uv_ref[...]
        rows = lax.broadcasted_iota(jnp.int32, ug.shape, 0)
        yg = _conv_pre(ug, cwg_ref, cbg_ref, rows)
        yv = _conv_pre(uv, cwv_ref, cbv_ref, rows)
        yg_ref[...] = yg
        yv_ref[...] = yv
        a_ref[...] = (yg * _sigmoid(yg) * yv).astype(BF16)

    def blk(off):
        return pl.BlockSpec((None, SEQ, CT), lambda b, c: (b, 0, off + c))

    return _call(
        body, name=name, grid=(Bn, NCT),
        in_specs=[blk(0), blk(NCT),
                  pl.BlockSpec((3, CT), lambda b, c: (0, c)), pl.BlockSpec((3, CT), lambda b, c: (0, NCT + c)),
                  pl.BlockSpec((1, CT), lambda b, c: (0, c)), pl.BlockSpec((1, CT), lambda b, c: (0, NCT + c))],
        out_specs=[blk(0)] * 3,
        out_shape=[_sds((Bn, SEQ, FF), BF16), _sds((Bn, SEQ, FF), F32), _sds((Bn, SEQ, FF), F32)], comm=comm,
    )(u3, u3, cw, cw, cb, cb)


def _conv_bwd(name, da3, u3, yg3, yv3, cw, comm=None):
    Bn = u3.shape[0]
    last = NCT * Bn - 1
    R = 128

    def body(da_ref, ug_ref, uv_ref, yg_ref, yv_ref, cwg_ref, cwv_ref,
             du_ref, dcwg_ref, dcwv_ref, dcbg_ref, dcbv_ref, stage_g, stage_v, sems):
        c, b = pl.program_id(0), pl.program_id(1)
        step = c * Bn + b

        def writes(off_c, stage, sem):
            col = pl.multiple_of(off_c + c * CT, CT)
            return pltpu.make_async_copy(stage, du_ref.at[b, :, pl.ds(col, CT)], sem)

        @pl.when(step > 0)
        def _():
            writes(0, stage_g, sems.at[0]).wait()
            writes(FF, stage_v, sems.at[1]).wait()

        cwg, cwv = cwg_ref[...], cwv_ref[...]
        tail_rows = lax.broadcasted_iota(jnp.int32, (R, CT), 0)

        def chunk(c0, acc, at_end):
            n = R if at_end else R + 8
            yg, yv = yg_ref[pl.ds(c0, n), :], yv_ref[pl.ds(c0, n), :]
            da = da_ref[pl.ds(c0, n), :].astype(F32)
            s = _sigmoid(yg)
            dyv = da * (yg * s)
            dyg = da * yv * (s * (1.0 + yg * (1.0 - s)))
            out = []
            for dy, u_ref, cwt, stage in ((dyg, ug_ref, cwg, stage_g), (dyv, uv_ref, cwv, stage_v)):
                if at_end:
                    d0 = dy
                    d1 = jnp.where(tail_rows < R - 1, pltpu.roll(dy, R - 1, 0), 0.0)
                    d2 = jnp.where(tail_rows < R - 2, pltpu.roll(dy, R - 2, 0), 0.0)
                else:
                    d0 = dy[0:R, :]
                    d1 = pltpu.roll(dy, R + 7, 0)[0:R, :]
                    d2 = pltpu.roll(dy, R + 6, 0)[0:R, :]
                stage[pl.ds(c0, R), :] = (cwt[2:3, :] * d0 + cwt[1:2, :] * d1 + cwt[0:1, :] * d2).astype(BF16)
                u = u_ref[pl.ds(c0, R), :]
                out += [jnp.sum(d2 * u, axis=0, keepdims=True), jnp.sum(d1 * u, axis=0, keepdims=True),
                        jnp.sum(d0 * u, axis=0, keepdims=True), jnp.sum(d0, axis=0, keepdims=True)]
            return tuple(a + o for a, o in zip(acc, out))

        zero = jnp.zeros((1, CT), F32)
        acc = lax.fori_loop(0, SEQ // R - 1, lambda i, acc: chunk(pl.multiple_of(i * R, R), acc, False), (zero,) * 8)
        acc = chunk(SEQ - R, acc, True)
        for dcw_ref, dcb_ref, part in ((dcwg_ref, dcbg_ref, acc[0:4]), (dcwv_ref, dcbv_ref, acc[4:8])):
            dcw = jnp.concatenate(part[0:3], axis=0)

            @pl.when(b == 0)
            def _():
                dcw_ref[...] = dcw
                dcb_ref[...] = part[3]

            @pl.when(b > 0)
            def _():
                dcw_ref[...] += dcw
                dcb_ref[...] += part[3]

        writes(0, stage_g, sems.at[0]).start()
        writes(FF, stage_v, sems.at[1]).start()

        @pl.when(step == last)
        def _():
            writes(0, stage_g, sems.at[0]).wait()
            writes(FF, stage_v, sems.at[1]).wait()

    def blk(off):
        return pl.BlockSpec((None, SEQ, CT), lambda c, b: (b, 0, off + c))

    def vec(r, off):
        return pl.BlockSpec((r, CT), lambda c, b: (0, off + c))

    du3, dcwg, dcwv, dcbg, dcbv = _call(
        body, name=name, grid=(NCT, Bn),
        in_specs=[blk(0), blk(0), blk(NCT), blk(0), blk(0), vec(3, 0), vec(3, NCT)],
        out_specs=[pl.BlockSpec(memory_space=pl.ANY), vec(3, 0), vec(3, 0), vec(1, 0), vec(1, 0)],
        out_shape=[_sds((Bn, SEQ, UW), BF16), _sds((3, FF), F32), _sds((3, FF), F32), _sds((1, FF), F32),
                   _sds((1, FF), F32)],
        scratch_shapes=[pltpu.VMEM((SEQ, CT), BF16)] * 2 + [pltpu.SemaphoreType.DMA((2,))], comm=comm,
    )(da3, u3, u3, yg3, yv3, cw, cw)
    return du3, jnp.concatenate([dcwg, dcwv], axis=1), jnp.concatenate([dcbg, dcbv], axis=1)


def _rope_tables():
    pos = jnp.arange(SEQ, dtype=F32)
    inv_freq = jnp.exp(jnp.arange(0, ROPE_DIM, 2, dtype=F32) * (-math.log(ROPE_THETA) / ROPE_DIM))
    ang = pos[:, None] * inv_freq[None, :]
    cos, sin = jnp.cos(ang), jnp.sin(ang)
    half = ROPE_DIM // 2
    zeros = jnp.zeros((SEQ, HEAD - ROPE_DIM), F32)
    zh = jnp.zeros((SEQ, half), F32)
    tab_c = jnp.concatenate([cos, cos, zeros + 1.0], axis=1)
    tab_a = jnp.concatenate([-sin, zh, zeros], axis=1)
    tab_b = jnp.concatenate([zh, sin, zeros], axis=1)
    return tab_c, tab_a, tab_b


def _rot(v, tc, ta, tb):
    half = ROPE_DIM // 2
    return v * tc + pltpu.roll(v, HEAD - half, 1) * ta + pltpu.roll(v, half, 1) * tb


def _rot_t(dv, tc, ta, tb):
    half = ROPE_DIM // 2
    return dv * tc + pltpu.roll(dv * ta, half, 1) + pltpu.roll(dv * tb, HEAD - half, 1)


def _band_masks():
    qi = lax.broadcasted_iota(jnp.int32, (HEAD, 2 * HEAD), 0)
    ki = lax.broadcasted_iota(jnp.int32, (HEAD, 2 * HEAD), 1)
    diff = HEAD + qi - ki
    both = (diff >= 0) & (diff <= HEAD)
    q1 = lax.broadcasted_iota(jnp.int32, (HEAD, HEAD), 0)
    k1 = lax.broadcasted_iota(jnp.int32, (HEAD, HEAD), 1)
    return q1 >= k1, both


_NT = (((1,), (1,)), ((), ()))
_TN = (((0,), (0,)), ((), ()))
_SCALE = HEAD ** -0.5


ATT_W = HEAD
ATT_HP = ATT_W // HEAD


def _res_rows(r, n, d, base=0):
    return pl.ds(base * d + r, n, stride=d) if d > 1 else pl.ds(base, n)


def _attn_load(q_ref, k_ref, v_ref, tc_ref, ta_ref, tb_ref, qs, ks, vs, d):
    L = SEQ // d
    for r in range(d):
        rows = _res_rows(r, L, d)
        dst = slice(r * L, (r + 1) * L)
        tc, ta, tb = tc_ref[dst, :], ta_ref[dst, :], tb_ref[dst, :]
        for hh in range(ATT_HP):
            sl = slice(hh * HEAD, (hh + 1) * HEAD)
            qs[dst, sl] = _rot(q_ref[rows, sl], tc, ta, tb).astype(BF16)
            ks[dst, sl] = _rot(k_ref[rows, sl], tc, ta, tb).astype(BF16)
            vs[dst, sl] = v_ref[rows, sl].astype(BF16)


def _attn_fwd(name, z3, tabs, g, d, comm=None):
    Bn = z3.shape[0]
    L = SEQ // d
    nb = L // HEAD
    W, nh = ATT_W, GROUP_W // ATT_W

    def body(q_ref, k_ref, v_ref, tc_ref, ta_ref, tb_ref, o_ref, l_ref, qs, ks, vs, sc, pc):
        m_first, m_both = _band_masks()
        _attn_load(q_ref, k_ref, v_ref, tc_ref, ta_ref, tb_ref, qs, ks, vs, d)
        blocks = [(r, n) for r in range(d) for n in range(nb)]

        def spans(r, n):
            rq = slice(r * L + n * HEAD, r * L + (n + 1) * HEAD)
            rk = slice(r * L + max(n - 1, 0) * HEAD, r * L + (n + 1) * HEAD)
            return rq, rk, slice(0, HEAD if n == 0 else 2 * HEAD)

        for i, (r, n) in enumerate(blocks):
            rq, rk, kc = spans(r, n)
            sc[i, :, kc] = lax.dot_general(qs[rq, :], ks[rk, :], _NT, preferred_element_type=F32)
        for i, (r, n) in enumerate(blocks):
            rq, rk, kc = spans(r, n)
            s = jnp.where(m_first if n == 0 else m_both, sc[i, :, kc] * _SCALE, NEG_INF)
            m = jnp.max(s, axis=-1, keepdims=True)
            e = jnp.exp(s - m)
            den = jnp.sum(e, axis=-1, keepdims=True)
            pc[i, :, kc] = (e * (1.0 / den)).astype(BF16)
            l_ref[_res_rows(r, HEAD, d, n * HEAD), :] = jnp.broadcast_to(m + jnp.log(den), (HEAD, HEAD))
        for i, (r, n) in enumerate(blocks):
            rq, rk, kc = spans(r, n)
            o_ref[_res_rows(r, HEAD, d, n * HEAD), :] = jnp.dot(pc[i, :, kc], vs[rk, :], preferred_element_type=F32)

    def zcol(off):
        return pl.BlockSpec((None, SEQ, W), lambda b, h: (b, 0, (off + g * GROUP_W) // W + h))

    tab = pl.BlockSpec((SEQ, HEAD), lambda b, h: (0, 0))
    out = pl.BlockSpec((None, SEQ, W), lambda b, h: (b, 0, h))
    return _call(
        body, name=name, grid=(Bn, nh),
        in_specs=[zcol(0), zcol(OFF_K), zcol(OFF_V), tab, tab, tab],
        out_specs=[out, out],
        out_shape=[_sds((Bn, SEQ, GROUP_W), F32), _sds((Bn, SEQ, GROUP_W), F32)],
        scratch_shapes=[pltpu.VMEM((SEQ, W), BF16)] * 3
        + [pltpu.VMEM((SEQ // HEAD, HEAD, 2 * HEAD), F32), pltpu.VMEM((SEQ // HEAD, HEAD, 2 * HEAD), BF16)],
        comm=comm,
    )(z3, z3, z3, *tabs)


def _attn_bwd(name, z3, tabs, g, d, do3, lse3, delta3, dz3, comm=None):
    Bn = z3.shape[0]
    L = SEQ // d
    nb = L // HEAD
    W, nh = ATT_W, GROUP_W // ATT_W

    def body(q_ref, k_ref, v_ref, tc_ref, ta_ref, tb_ref, do_ref, l_ref, dl_ref, dz_in, dz_ref,
             qs, ks, vs, dos, dqs, dks, dvs, nat, oq, ok, ov, sc, dpc, pc, dsc, sems):
        b, h = pl.program_id(0), pl.program_id(1)
        m_first, m_both = _band_masks()
        _attn_load(q_ref, k_ref, v_ref, tc_ref, ta_ref, tb_ref, qs, ks, vs, d)
        for r in range(d):
            dos[r * L:(r + 1) * L, :] = do_ref[_res_rows(r, L, d), :].astype(BF16)
        dks[...] = jnp.zeros_like(dks)
        dvs[...] = jnp.zeros_like(dvs)
        blocks = [(r, n) for r in range(d) for n in range(nb)]

        def spans(r, n):
            rq = slice(r * L + n * HEAD, r * L + (n + 1) * HEAD)
            rk = slice(r * L + max(n - 1, 0) * HEAD, r * L + (n + 1) * HEAD)
            return rq, rk, slice(0, HEAD if n == 0 else 2 * HEAD)

        for i, (r, n) in enumerate(blocks):
            rq, rk, kc = spans(r, n)
            sc[i, :, kc] = lax.dot_general(qs[rq, :], ks[rk, :], _NT, preferred_element_type=F32)
            dpc[i, :, kc] = lax.dot_general(dos[rq, :], vs[rk, :], _NT, preferred_element_type=F32)
        for i, (r, n) in enumerate(blocks):
            rq, rk, kc = spans(r, n)
            rows = _res_rows(r, HEAD, d, n * HEAD)
            s = jnp.where(m_first if n == 0 else m_both, sc[i, :, kc] * _SCALE, NEG_INF)
            p = jnp.exp(s - l_ref[rows, :][:, 0:1])
            pc[i, :, kc] = p.astype(BF16)
            dsc[i, :, kc] = (p * (dpc[i, :, kc] - dl_ref[rows, :][:, 0:1]) * _SCALE).astype(BF16)
        for i, (r, n) in enumerate(blocks):
            rq, rk, kc = spans(r, n)
            dqs[rq, :] = jnp.dot(dsc[i, :, kc], ks[rk, :], preferred_element_type=F32)
        for i, (r, n) in enumerate(blocks):
            rq, rk, kc = spans(r, n)
            dks[rk, :] += lax.dot_general(dsc[i, :, kc], qs[rq, :], _TN, preferred_element_type=F32)
            dvs[rk, :] += lax.dot_general(pc[i, :, kc], dos[rq, :], _TN, preferred_element_type=F32)
        step = b * nh + h

        def writes():
            base = g * GROUP_W + h * W
            return [pltpu.make_async_copy(src, dz_ref.at[b, :, pl.ds(pl.multiple_of(base + off, HEAD), W)],
                                          sems.at[i])
                    for i, (src, off) in enumerate(((oq, 0), (ok, OFF_K), (ov, OFF_V)))]

        @pl.when(step > 0)
        def _():
            for cp in writes():
                cp.wait()

        for src, dst, rotate in ((dqs, oq, True), (dks, ok, True), (dvs, ov, False)):
            for r in range(d):
                val = src[r * L:(r + 1) * L, :]
                if rotate:
                    rm = slice(r * L, (r + 1) * L)
                    val = _rot_t(val, tc_ref[rm, :], ta_ref[rm, :], tb_ref[rm, :])
                nat[_res_rows(r, L, d), :] = val
            dst[...] = nat[...].astype(BF16)
        for cp in writes():
            cp.start()

        @pl.when(step == Bn * nh - 1)
        def _():
            for cp in writes():
                cp.wait()

    def zcol(off):
        return pl.BlockSpec((None, SEQ, W), lambda b, h: (b, 0, (off + g * GROUP_W) // W + h))

    tab = pl.BlockSpec((SEQ, HEAD), lambda b, h: (0, 0))
    gcol = pl.BlockSpec((None, SEQ, W), lambda b, h: (b, 0, h))
    any_spec = pl.BlockSpec(memory_space=pl.ANY)
    return _call(
        body, name=name, grid=(Bn, nh),
        in_specs=[zcol(0), zcol(OFF_K), zcol(OFF_V), tab, tab, tab, gcol, gcol, gcol, any_spec],
        out_specs=any_spec,
        out_shape=_sds((Bn, SEQ, ZW), BF16),
        scratch_shapes=[pltpu.VMEM((SEQ, W), BF16)] * 4 + [pltpu.VMEM((SEQ, W), F32)] * 4
        + [pltpu.VMEM((SEQ, W), BF16)] * 3
        + [pltpu.VMEM((SEQ // HEAD, HEAD, 2 * HEAD), F32)] * 2 + [pltpu.VMEM((SEQ // HEAD, HEAD, 2 * HEAD), BF16)] * 2
        + [pltpu.SemaphoreType.DMA((3,))],
        aliases={9: 0}, comm=comm,
    )(z3, z3, z3, *tabs, do3, lse3, delta3, dz3)


def _merge_weights(l0, l1, l2):
    m = jnp.maximum(jnp.maximum(l0, l1), l2)
    e0, e1, e2 = jnp.exp(l0 - m), jnp.exp(l1 - m), jnp.exp(l2 - m)
    inv = 1.0 / (e0 + e1 + e2)
    return e0 * inv, e1 * inv, e2 * inv


def _merge_fwd(name, outs, lses, tr=512):
    T = outs[0].shape[0]

    def body(o0, o1, o2, l0, l1, l2, a_ref):
        w0, w1, w2 = _merge_weights(l0[...], l1[...], l2[...])
        a_ref[...] = (w0 * o0[...] + w1 * o1[...] + w2 * o2[...]).astype(BF16)

    row = pl.BlockSpec((tr, GROUP_W), lambda i: (i, 0))
    return pl.pallas_call(
        body, name=name, grid=(T // tr,), in_specs=[row] * 6, out_specs=row,
        out_shape=_sds((T, GROUP_W), BF16), compiler_params=_params(1),
    )(*outs, *lses)


def _merge_bwd(name, outs, lses, dattn, tr=512):
    T = outs[0].shape[0]

    def body(o0, o1, o2, l0, l1, l2, da_ref, d0, d1, d2, e0, e1, e2):
        w = _merge_weights(l0[...], l1[...], l2[...])
        da = da_ref[...]
        attn = w[0] * o0[...] + w[1] * o1[...] + w[2] * o2[...]
        prod = da * attn
        csum = jnp.concatenate(
            [jnp.broadcast_to(jnp.sum(prod[:, hh * HEAD:(hh + 1) * HEAD], axis=-1, keepdims=True), (tr, HEAD))
             for hh in range(GROUP_W // HEAD)], axis=1)
        for wg, d_ref, e_ref in zip(w, (d0, d1, d2), (e0, e1, e2)):
            d_ref[...] = wg * da
            e_ref[...] = wg * csum

    row = pl.BlockSpec((tr, GROUP_W), lambda i: (i, 0))
    res = pl.pallas_call(
        body, name=name, grid=(T // tr,), in_specs=[row] * 7, out_specs=[row] * 6,
        out_shape=[_sds((T, GROUP_W), F32)] * 6,
        compiler_params=_params(1),
    )(*outs, *lses, dattn)
    return res[:3], res[3:]


def _rms_rows(xv, g):
    r = lax.rsqrt(jnp.mean(xv * xv, axis=-1, keepdims=True) + RMS_EPS)
    return (xv * r * g).astype(BF16)


def _epi_norm(val, x_refs, o_ref, y_refs, first_rows):
    o_ref[...] = val
    y_refs[0][...] = _rms_rows(val, x_refs[0][...])


def _epi_ple(val, x_refs, o_ref, y_refs, first_rows):
    o_ref[...] = val
    xn = x_refs[0][...] + x_refs[1][...] * _sigmoid(val)
    y_refs[0][...] = xn
    if len(y_refs) > 1:
        y_refs[1][...] = _rms_rows(xn, x_refs[2][...])


def _epi_norm_bwd(val, x_refs, o_ref, y_refs, first_rows):
    xv = x_refs[0][...]
    r = lax.rsqrt(jnp.mean(xv * xv, axis=-1, keepdims=True) + RMS_EPS)
    xh = xv * r
    part = _rows8(val * xh)

    @pl.when(first_rows)
    def _():
        y_refs[0][...] = part

    @pl.when(jnp.logical_not(first_rows))
    def _():
        y_refs[0][...] += part

    dxh = val * x_refs[2][...]
    o_ref[...] = x_refs[1][...] + r * (dxh - xh * jnp.mean(dxh * xh, axis=-1, keepdims=True))


def _local_step(x3, p4, tgt3, vecs, ex):
    Bn = x3.shape[0]
    T = Bn * SEQ
    x = x3.reshape(T, D)
    tgt = tgt3.reshape(T, D)
    pb = p4.astype(BF16).reshape(DEPTH, T, PLE)
    tabs = {}
    for d in DILATIONS:
        tabs[d] = [t.reshape(SEQ // d, d, HEAD).transpose(1, 0, 2).reshape(SEQ, HEAD) for t in _rope_tables()]
    tm = 1024 if T % 1024 == 0 else 512
    nt = T // tm
    tk = 1024 if T % 1024 == 0 else 512
    ntk = T // tk
    tm5 = 512
    f32o = lambda n: _sds((T, n), F32)

    def spec(shape, fn):
        return pl.BlockSpec(shape, fn)

    def _mm(name, *args, **kwargs):
        return _mm_call(name, *args, comm=ex.hook(name), **kwargs)

    def cols4(a_ref, b_ref):
        av = a_ref[...]
        return jnp.concatenate([_dot(av, b_ref[j]) for j in range(NCHIP)], axis=1)

    def rows4(a_ref, b_ref):
        av = a_ref[...]
        return jnp.concatenate([_dot(av, b_ref[:, j * 256:(j + 1) * 256], ta=True) for j in range(NCHIP)], axis=0)

    def kchunks4(a_ref, b_ref):
        total = _dot(a_ref[:, 0:256], b_ref[0], tb=True)
        for j in range(1, NCHIP):
            total = total + _dot(a_ref[:, j * 256:(j + 1) * 256], b_ref[j], tb=True)
        return total

    row5 = spec((tm5, D), lambda i, *_: (i, 0))
    rowm = spec((tm, D), lambda i, *_: (i, 0))
    gain = spec((1, D), lambda *_: (0, 0))
    bf_rows = (_sds((T, D), BF16), row5)

    saved = []
    h = _rms_fwd("rms_mix0", x, vecs["g_mix"][0:1])
    for l in range(DEPTH):
        L = str(l)
        G = ex.weights(l)
        g_mix, g_ffn, g_ple = (vecs[k][l:l + 1] for k in ("g_mix", "g_ffn", "g_ple"))
        pscale, cb, cw = vecs["pool_scale"][l:l + 1], vecs["conv_b"][l:l + 1], vecs["conv_w"][l]
        z = _mm("mm_z" + L, h, G["in"], grid=(nt, NCHIP),
                a_spec=spec((tm, D), lambda i, n: (i, 0)),
                b_spec=spec((None, D, IN_S), lambda i, n: (n, 0, 0)),
                o_spec=spec((tm, IN_S), lambda i, n: (i, n)), out_shape=f32o(ZW))
        z3 = z.reshape(Bn, SEQ, ZW)
        outs, lses = [], []
        for g, d in enumerate(DILATIONS):
            o_g, l_g = _attn_fwd("attn_fwd%d_%d" % (g, l), z3, tabs[d], g, d, comm=ex.hook("attn_fwd%d_%d" % (g, l)))
            outs.append(o_g.reshape(T, GROUP_W))
            lses.append(l_g.reshape(T, GROUP_W))
        attn = _merge_fwd("merge_fwd" + L, outs, lses)
        ya = _mm("mm_ya" + L, attn, G["sc"], grid=(nt,), compute=cols4,
                 a_spec=spec((tm, GROUP_W), lambda i: (i, 0)),
                 b_spec=spec((NCHIP, GROUP_W, 256), lambda i: (0, 0, 0)),
                 o_spec=spec((tm, D), lambda i: (i, 0)), out_shape=_sds((T, D), BF16))
        pooled3, ms3 = _pool_fwd("pool_fwd" + L, z3, G["sc"], pscale)
        ms = ms3.reshape(T, D)

        def row_sharded(name, a, rb, res=None, kdim=D, out=F32, **fused):
            if rb is None:
                b_arr, b_spec = G["dn"], spec((NCHIP, DN_S, D), lambda i: (0, 0, 0))
            else:
                b_arr, b_spec = G["r3"], spec((NCHIP, 256, D), lambda i: (0, rb, 0))
            return _mm(name, a, b_arr, grid=(T // tm5,),
                       a_spec=spec((tm5, kdim), lambda i: (i, 0)), b_spec=b_spec,
                       o_spec=row5, out_shape=_sds((T, D), out), res=res, res_spec=None if res is None else row5,
                       **fused)

        yb = row_sharded("mm_yb" + L, ms, 0, out=BF16)
        merged = _gate_fwd("gate_fwd" + L, z, ya, yb)
        x1, h2 = row_sharded("mm_o" + L, merged, 1, res=x, epilogue=_epi_norm, extra_in=[(g_ffn, gain)],
                             extra_out=[bf_rows])
        u = _mm("mm_up" + L, h2, G["up"], grid=(nt, NCHIP),
                a_spec=spec((tm, D), lambda i, n: (i, 0)),
                b_spec=spec((None, D, UP_S), lambda i, n: (n, 0, 0)),
                o_spec=spec((tm, UP_S), lambda i, n: (i, n)), out_shape=f32o(UW))
        u3 = u.reshape(Bn, SEQ, UW)
        act3, yg3, yv3 = _conv_fwd("conv_fwd" + L, u3, cw, cb, comm=ex.hook("conv_fwd" + L))
        act = act3.reshape(T, FF)
        x2, h3 = row_sharded("mm_down" + L, act, None, res=x1, kdim=FF, epilogue=_epi_norm,
                             extra_in=[(g_ple, gain)], extra_out=[bf_rows])
        pe = _mm("mm_pe" + L, pb[l], G["sc"], grid=(nt,), compute=cols4,
                 a_spec=spec((tm, PLE), lambda i: (i, 0)),
                 b_spec=spec((NCHIP, 256, 256), lambda i: (0, 2, 0)),
                 o_spec=spec((tm, D), lambda i: (i, 0)), out_shape=f32o(D))
        fused_in = [(x2, row5), (pe, row5)]
        fused_out = [(f32o(D), row5)]
        if l + 1 < DEPTH:
            fused_in.append((vecs["g_mix"][l + 1:l + 2], gain))
            fused_out.append(bf_rows)
        pg, x3n, *h_next = row_sharded("mm_pg" + L, h3, 2, epilogue=_epi_ple, extra_in=fused_in,
                                       extra_out=fused_out)
        saved.append(dict(x=x, h=h, z=z, outs=outs, lses=lses, attn=attn, ya=ya, yb=yb, pooled3=pooled3, ms=ms,
                          merged=merged, x1=x1, h2=h2, u3=u3, yg3=yg3, yv3=yv3, act=act, x2=x2, h3=h3, pg=pg, pe=pe))
        x = x3n
        h = h_next[0] if h_next else None

    dx, dg_final8, sq8 = _final_loss(x, vecs["g_final"].reshape(1, D), tgt)

    gg_shape = {k: _sds(G[k].shape, F32) for k in G}
    small = {"g_final": dg_final8}

    for l in reversed(range(DEPTH)):
        L = str(l)
        sv = saved[l]
        G = ex.weights(l)
        GG = dict.fromkeys(_KEYS)
        g_mix, g_ffn, g_ple = (vecs[k][l:l + 1] for k in ("g_mix", "g_ffn", "g_ple"))
        pscale, cb, cw = vecs["pool_scale"][l:l + 1], vecs["conv_b"][l:l + 1], vecs["conv_w"][l]

        def wgrad_rows(name, a, b_arr, key, rb):
            GG[key] = _mm(name, a, b_arr, grid=(2, ntk), ta=True, k_axis=1, nk=ntk, acc_shape=(D, 512),
                          a_spec=spec((tk, D), lambda n, k: (k, 0)),
                          b_spec=spec((tk, 512), lambda n, k: (k, n)),
                          o_spec=spec((NCHIP, 256, 512), lambda n, k: (0, rb, n)),
                          out_shape=gg_shape[key], buf=GG[key])

        def dgrad_rows(name, dy, rb, out=F32, **fused):
            return _mm(name, dy, G["r3"], grid=(T // tm5,), tb=True,
                       a_spec=spec((tm5, D), lambda i: (i, 0)),
                       b_spec=spec((NCHIP, 256, D), lambda i: (0, rb, 0)),
                       o_spec=row5, out_shape=_sds((T, D), out), **fused)

        def norm_bwd(xin, dres, g, rows=row5):
            return dict(epilogue=_epi_norm_bwd, extra_in=[(xin, rows), (dres, rows), (g, gain)],
                        extra_out=[(_sds((8, D), F32), spec((8, D), lambda *_: (0, 0)))])

        dpe, dpg = _ple_bwd("ple_bwd" + L, dx, sv["pe"], sv["pg"])
        GG["sc"] = _mm("wg_ple" + L, pb[l], dpe, grid=(ntk,), compute=rows4, k_axis=0, nk=ntk,
                       acc_shape=(NCHIP * PLE, 256),
                       a_spec=spec((tk, PLE), lambda k: (k, 0)), b_spec=spec((tk, D), lambda k: (k, 0)),
                       o_spec=spec((NCHIP, 256, 256), lambda k: (0, 2, 0)),
                       out_shape=gg_shape["sc"], buf=GG["sc"])
        wgrad_rows("wg_pg" + L, sv["h3"], dpg, "r3", 2)
        dx, small["g_ple" + L] = dgrad_rows("dg_pg" + L, dpg, 2, **norm_bwd(sv["x2"], dx, g_ple))

        da = _mm("dg_down" + L, dx, G["dn"], grid=(T // 256,), tb=True,
                 a_spec=spec((256, D), lambda i: (i, 0)),
                 b_spec=spec((NCHIP, DN_S, D), lambda i: (0, 0, 0)),
                 o_spec=spec((256, FF), lambda i: (i, 0)), out_shape=_sds((T, FF), BF16))
        GG["dn"] = _mm("wg_down" + L, sv["act"], dx, grid=(2, ntk), ta=True, k_axis=1, nk=ntk,
                       acc_shape=(FF, 512),
                       a_spec=spec((tk, FF), lambda n, k: (k, 0)), b_spec=spec((tk, 512), lambda n, k: (k, n)),
                       o_spec=spec((NCHIP, DN_S, 512), lambda n, k: (0, 0, n)),
                       out_shape=gg_shape["dn"], buf=GG["dn"])
        du3, dcw, dcb = _conv_bwd("conv_bwd" + L, da.reshape(Bn, SEQ, FF), sv["u3"], sv["yg3"], sv["yv3"], cw,
                                  comm=ex.hook("conv_bwd" + L))
        small["conv_w" + L], small["conv_b" + L] = dcw, dcb
        du = du3.reshape(T, UW)
        dx, small["g_ffn" + L] = _mm(
            "dg_up" + L, du, G["up"], grid=(nt, NCHIP), tb=True, k_axis=1, nk=NCHIP, acc_shape=(tm, D),
            a_spec=spec((tm, UP_S), lambda i, k: (i, k)), b_spec=spec((None, D, UP_S), lambda i, k: (k, 0, 0)),
            o_spec=rowm, out_shape=f32o(D), vmem=VMEM_BIG, **norm_bwd(sv["x1"], dx, g_ffn, rowm))
        GG["up"] = _mm("wg_up" + L, sv["h2"], du, grid=(NCHIP, ntk), ta=True, k_axis=1, nk=ntk,
                       acc_shape=(D, UP_S),
                       a_spec=spec((tk, D), lambda j, k: (k, 0)),
                       b_spec=spec((tk, UP_S), lambda j, k: (k, j)),
                       o_spec=spec((None, D, UP_S), lambda j, k: (j, 0, 0)),
                       out_shape=gg_shape["up"], buf=GG["up"])

        dmerged = dgrad_rows("dg_o" + L, dx, 1, out=BF16)
        wgrad_rows("wg_o" + L, sv["merged"], dx, "r3", 1)
        dya, dz = _gate_bwd("gate_bwd_a" + L, sv["z"], OFF_GA, sv["ya"], dmerged, None)
        dyb, dz = _gate_bwd("gate_bwd_b" + L, sv["z"], OFF_GB, sv["yb"], dmerged, dz)
        dms = dgrad_rows("dg_yb" + L, dyb, 0)
        wgrad_rows("wg_yb" + L, sv["ms"], dyb, "r3", 0)
        dz3, GG["sc"], small["pool_scale" + L] = _pool_bwd(
            "pool_bwd" + L, dms.reshape(Bn, SEQ, D), sv["pooled3"], G["sc"], pscale,
            dz.reshape(Bn, SEQ, ZW), GG["sc"])
        dattn = _mm("dg_ya" + L, dya, G["sc"], grid=(nt,), compute=kchunks4,
                    a_spec=spec((tm, D), lambda i: (i, 0)),
                    b_spec=spec((NCHIP, GROUP_W, 256), lambda i: (0, 0, 0)),
                    o_spec=spec((tm, GROUP_W), lambda i: (i, 0)), out_shape=f32o(GROUP_W))
        GG["sc"] = _mm("wg_ya" + L, sv["attn"], dya, grid=(ntk,), compute=rows4, k_axis=0, nk=ntk,
                       acc_shape=(NCHIP * GROUP_W, 256),
                       a_spec=spec((tk, GROUP_W), lambda k: (k, 0)), b_spec=spec((tk, D), lambda k: (k, 0)),
                       o_spec=spec((NCHIP, GROUP_W, 256), lambda k: (0, 0, 0)),
                       out_shape=gg_shape["sc"], buf=GG["sc"])
        ex.grads_ready(l, {k: GG[k] for k in _KEYS[1:]})
        dos, deltas = _merge_bwd("merge_bwd" + L, sv["outs"], sv["lses"], dattn)
        view3 = lambda t: t.reshape(Bn, SEQ, GROUP_W)
        sz3 = sv["z"].reshape(Bn, SEQ, ZW)
        for g, d in enumerate(DILATIONS):
            dz3 = _attn_bwd("attn_bwd%d_%d" % (g, l), sz3, tabs[d], g, d, view3(dos[g]), view3(sv["lses"][g]),
                            view3(deltas[g]), dz3, comm=ex.hook("attn_bwd%d_%d" % (g, l)))
        dz = dz3.reshape(T, ZW)
        GG["in"] = _mm("wg_z" + L, sv["h"], dz, grid=(NCHIP, ntk), ta=True, k_axis=1, nk=ntk,
                       acc_shape=(D, IN_S),
                       a_spec=spec((tk, D), lambda n, k: (k, 0)), b_spec=spec((tk, IN_S), lambda n, k: (k, n)),
                       o_spec=spec((None, D, IN_S), lambda n, k: (n, 0, 0)),
                       out_shape=gg_shape["in"], buf=GG["in"])
        ex.grads_ready(l, {"in": GG["in"]})

        def dgrad_z(name, first, count, dres, buf):
            rows = spec((tm, D), lambda i, *_: (i + first, 0))
            return _mm(name, dz, G["in"], grid=(count, NCHIP), tb=True, k_axis=1, nk=NCHIP, acc_shape=(tm, D),
                       a_spec=spec((tm, IN_S), lambda i, k: (i + first, k)),
                       b_spec=spec((None, D, IN_S), lambda i, k: (k, 0, 0)),
                       o_spec=rows, out_shape=f32o(D), vmem=VMEM_BIG, buf=buf, **norm_bwd(sv["x"], dres, g_mix, rows))

        if l == 0 and nt % 2 == 0:
            dx_a, part_a = dgrad_z("dg_z0a", 0, nt // 2, dx, None)
            dx, part_b = dgrad_z("dg_z0b", nt // 2, nt // 2, dx, dx_a)
            small["g_mix" + L] = part_a + part_b
        else:
            dx, small["g_mix" + L] = dgrad_z("dg_z" + L, 0, nt, dx, None)

    return sq8, dx.reshape(Bn, SEQ, D), small


_ANY = pl.BlockSpec(memory_space=pl.ANY)


def _place():
    x, y, c = lax.axis_index("x"), lax.axis_index("y"), lax.axis_index("c")
    chips = [(1 - x, y), (x, 1 - y), (1 - x, 1 - y)]
    return x, y, c, 2 * x + y, chips


def _half(rows, cc):
    return pl.ds(cc * (rows // 2), rows // 2)


def _remote(src, dst, send_sems, recv_sems, i, to):
    return pltpu.make_async_remote_copy(src_ref=src, dst_ref=dst, send_sem=send_sems.at[i], recv_sem=recv_sems.at[i],
                                        device_id=to, device_id_type=MESH)


def _exchange_gather_ici(stacks, done):
    n = len(stacks)
    rows = [t.shape[1] for t in stacks]

    def start(refs, send_sems, recv_sems):
        x, y, c, me, chips = _place()
        for k in range(n):
            part = refs[k].at[me, _half(rows[k], c)]
            for j, chip in enumerate(chips):
                _remote(part, part, send_sems, recv_sems, 3 * k + j, (*chip, c)).start()

    def wait(refs, send_sems, recv_sems):
        x, y, c, me, chips = _place()
        for k in range(n):
            for j, chip in enumerate(chips):
                part = refs[k].at[2 * chip[0] + chip[1], _half(rows[k], c)]
                _remote(part, part, send_sems, recv_sems, 3 * k + j, (*chip, c)).wait()

    return dict(arrays=list(stacks), nsem=3 * n, start=start, wait=wait, done=done)


def _exchange_gather_d2d(stacks, done):
    n = len(stacks)
    rows = [t.shape[1] for t in stacks]

    def copies(refs, send_sems, recv_sems, mine):
        x, y, c, me, chips = _place()
        cc = c if mine else 1 - c
        return [_remote(part, part, send_sems, recv_sems, 3 * k + j, (x, y, 1 - c))
                for k in range(n) for j, chip in enumerate(chips)
                for part in [refs[k].at[2 * chip[0] + chip[1], _half(rows[k], cc)]]]

    def start(refs, send_sems, recv_sems):
        for cp in copies(refs, send_sems, recv_sems, True):
            cp.start()

    def wait(refs, send_sems, recv_sems):
        for cp in copies(refs, send_sems, recv_sems, False):
            cp.wait()

    return dict(arrays=list(stacks), nsem=3 * n, start=start, wait=wait, done=done)


def _exchange_halves(g5, recv, done):
    n = len(g5)

    def copies(refs, send_sems, recv_sems):
        x, y, c, me, chips = _place()
        return [_remote(refs[k].at[:, 1 - c], refs[n + k], send_sems, recv_sems, k, (x, y, 1 - c)) for k in range(n)]

    def start(refs, send_sems, recv_sems):
        for cp in copies(refs, send_sems, recv_sems):
            cp.start()

    def wait(refs, send_sems, recv_sems):
        for cp in copies(refs, send_sems, recv_sems):
            cp.wait()

    return dict(arrays=list(g5) + list(recv), nsem=n, start=start, wait=wait, done=done)


def _exchange_chips(parts, landing, done):
    n = len(parts)

    def start(refs, send_sems, recv_sems):
        x, y, c, me, chips = _place()
        for k in range(n):
            for j, chip in enumerate(chips):
                _remote(refs[k].at[2 * chip[0] + chip[1]], refs[n + k].at[me], send_sems, recv_sems, 3 * k + j,
                        (*chip, c)).start()

    def wait(refs, send_sems, recv_sems):
        x, y, c, me, chips = _place()
        for k in range(n):
            for j, chip in enumerate(chips):
                slot = refs[n + k].at[2 * chip[0] + chip[1]]
                _remote(slot, slot, send_sems, recv_sems, 3 * k + j, (*chip, c)).wait()

    return dict(arrays=list(parts) + list(landing), nsem=3 * n, start=start, wait=wait, done=done)


def _exchange_share(full, layer, done):
    n = len(full)

    def copies(refs, send_sems, recv_sems, mine):
        x, y, c, me, chips = _place()
        cc = c if mine else 1 - c
        return [_remote(part, part, send_sems, recv_sems, k, (x, y, 1 - c))
                for k in range(n) for part in [refs[k].at[layer, cc]]]

    def start(refs, send_sems, recv_sems):
        for cp in copies(refs, send_sems, recv_sems, True):
            cp.start()

    def wait(refs, send_sems, recv_sems):
        for cp in copies(refs, send_sems, recv_sems, False):
            cp.wait()

    return dict(arrays=list(full), nsem=n, start=start, wait=wait, done=done)


def _exchange_call(name, comm):
    arrays = comm["arrays"]
    n = len(arrays)

    def body(*refs):
        outs, send_sems, recv_sems = refs[n:2 * n], refs[2 * n], refs[2 * n + 1]
        comm["start"](outs, send_sems, recv_sems)
        comm["wait"](outs, send_sems, recv_sems)

    outs = pl.pallas_call(
        body, name=name, in_specs=[_ANY] * n, out_specs=[_ANY] * n,
        out_shape=[_sds(t.shape, t.dtype) for t in arrays],
        scratch_shapes=[pltpu.SemaphoreType.DMA((comm["nsem"],))] * 2,
        input_output_aliases={i: i for i in range(n)},
    )(*arrays)
    comm["done"](outs)


def _gather_first(stacks, cw4):
    n = len(stacks)
    ici = _exchange_gather_ici(stacks, None)
    d2d = _exchange_gather_d2d(stacks, None)
    rows = [t.shape[1] for t in stacks]

    def body(*refs):
        g_refs, cwg_ref = refs[n + 1:2 * n + 1], refs[2 * n + 1]
        s_ici, r_ici, s_d2d, r_d2d, s_cw, r_cw = refs[2 * n + 2:]
        x, y, c, me, chips = _place()

        def cw_copy(j, slot, chip):
            part = cwg_ref.at[slot]
            return _remote(part, part, s_cw, r_cw, j, (*chip, c))

        ici["start"](g_refs, s_ici, r_ici)
        for j, chip in enumerate(chips):
            cw_copy(j, me, chip).start()
        for k in range(n):
            for j, chip in enumerate(chips):
                part = g_refs[k].at[2 * chip[0] + chip[1], _half(rows[k], c)]
                _remote(part, part, s_ici, r_ici, 3 * k + j, (*chip, c)).wait()
                _remote(part, part, s_d2d, r_d2d, 3 * k + j, (x, y, 1 - c)).start()
        d2d["wait"](g_refs, s_d2d, r_d2d)
        for j, chip in enumerate(chips):
            cw_copy(j, 2 * chip[0] + chip[1], chip).wait()

    outs = pl.pallas_call(
        body, name="gather_first", in_specs=[_ANY] * (n + 1), out_specs=[_ANY] * (n + 1),
        out_shape=[_sds(t.shape, t.dtype) for t in stacks] + [_sds(cw4.shape, cw4.dtype)],
        scratch_shapes=[pltpu.SemaphoreType.DMA((3 * n,))] * 4 + [pltpu.SemaphoreType.DMA((3,))] * 2,
        input_output_aliases={i: i for i in range(n + 1)},
    )(*stacks, cw4)
    return outs[:n], outs[n]


def _small_allreduce(small):
    def body(small_ref, red_ref, gath, s_send, s_recv):
        x, y, c, me, chips = _place()
        dev = 4 * x + 2 * y + c
        gath[dev] = small_ref[...]
        for r in range(1, 8):
            peer = (x ^ (r >> 2), y ^ ((r >> 1) & 1), c ^ (r & 1))
            _remote(small_ref, gath.at[dev], s_send, s_recv, r - 1, peer).start()
        for r in range(1, 8):
            peer = (x ^ (r >> 2), y ^ ((r >> 1) & 1), c ^ (r & 1))
            src = 4 * peer[0] + 2 * peer[1] + peer[2]
            _remote(small_ref, gath.at[src], s_send, s_recv, r - 1, peer).wait()
        total = gath[0]
        for i in range(1, 8):
            total = total + gath[i]
        red_ref[...] = total

    vm = pl.BlockSpec(memory_space=pltpu.VMEM)
    return pl.pallas_call(
        body, name="small_allreduce", in_specs=[vm], out_specs=vm, out_shape=_sds(small.shape, F32),
        scratch_shapes=[pltpu.VMEM((8,) + small.shape, F32), pltpu.SemaphoreType.DMA((7,)),
                        pltpu.SemaphoreType.DMA((7,))],
    )(small)


def _row_tile(rh):
    for cand in (512, 384, 352, 256, 128):
        if rh % cand == 0:
            return cand
    return rh


def _add_halves(name, g5, recv, place):
    _, _, rh, cols = g5.shape
    tr = _row_tile(rh)

    def body(place_ref, g_ref, r_ref, o_ref, own_ref):
        val = (g_ref[...] + r_ref[...]).astype(BF16)
        o_ref[...] = val

        @pl.when(pl.program_id(1) == place_ref[1])
        def _():
            own_ref[...] = val

    grid_spec = pltpu.PrefetchScalarGridSpec(
        num_scalar_prefetch=1, grid=(rh // tr, NCHIP),
        in_specs=[pl.BlockSpec((None, None, tr, cols), lambda i, j, pr: (j, pr[0], i, 0)),
                  pl.BlockSpec((None, tr, cols), lambda i, j, pr: (j, i, 0))],
        out_specs=[pl.BlockSpec((None, tr, cols), lambda i, j, pr: (j, i, 0)),
                   pl.BlockSpec((None, tr, cols), lambda i, j, pr: (pr[1], i, 0))])
    return pl.pallas_call(
        body, name=name, grid_spec=grid_spec, out_shape=[_sds(recv.shape, BF16)] * 2, compiler_params=_params(2),
    )(place, g5, recv)


def _sum_chips(name, landing, place, layer, full):
    _, rh, cols = landing.shape
    tr = _row_tile(rh)
    has_full = full is not None

    def body(*refs):
        r_ref, o_ref = refs[1], refs[-1]
        total = r_ref[0].astype(F32)
        for j in range(1, NCHIP):
            total = total + r_ref[j].astype(F32)
        o_ref[...] = total

    grid_spec = pltpu.PrefetchScalarGridSpec(
        num_scalar_prefetch=1, grid=(rh // tr,),
        in_specs=[pl.BlockSpec((NCHIP, tr, cols), lambda i, pr: (0, i, 0))] + ([_ANY] if has_full else []),
        out_specs=pl.BlockSpec((None, None, tr, cols), lambda i, pr: (layer, pr[0], i, 0)))
    return pl.pallas_call(
        body, name=name, grid_spec=grid_spec, out_shape=_sds((DEPTH, 2, rh, cols), F32),
        input_output_aliases={2: 0} if has_full else {}, compiler_params=_params(1),
    )(place, landing, *([full] if has_full else []))


class _Schedule:
    FIRST, REST = ["in"], list(_KEYS[1:])

    def __init__(self, slotted, cw4, place):
        self.place = place
        self._w = [dict(zip(_KEYS, layer)) for layer in slotted]
        got, self.cw4 = _gather_first([self._w[0][k] for k in self.FIRST], cw4)
        self._w[0].update(zip(self.FIRST, got))
        self._g5, self._recv, self._parts, self._landing = [{}, {}], [{}, {}], [{}, {}], [{}, {}]
        self.full = {}
        every = list(_KEYS)
        self._hooks = {
            "mm_z0": lambda: self._gather(_exchange_gather_ici, 0, self.REST),
            "attn_fwd0_0": lambda: self._gather(_exchange_gather_d2d, 0, self.REST),
            "mm_up0": lambda: self._gather(_exchange_gather_ici, 1, self.REST),
            "conv_fwd0": lambda: self._gather(_exchange_gather_ici, 1, self.FIRST),
            "mm_down0": lambda: self._gather(_exchange_gather_d2d, 1, every),
            "dg_down0": lambda: self._halves(1, every),
            "conv_bwd0": lambda: self._chips(1, every),
            "dg_up0": lambda: self._share(1, every),
            "attn_bwd0_0": lambda: self._halves(0, self.REST),
            "wg_z0": lambda: self._chips(0, self.REST),
            "dg_z0a": lambda: self._halves(0, self.FIRST),
            "dg_z0b": lambda: self._chips(0, self.FIRST),
        }

    def weights(self, layer):
        return self._w[layer]

    def hook(self, name):
        make = self._hooks.get(name)
        return make() if make else None

    def grads_ready(self, layer, GG):
        for k, t in GG.items():
            g5 = t.reshape(NCHIP, 2, t.shape[1] // 2, t.shape[2])
            self._g5[layer][k] = g5
            self._recv[layer][k] = lax.empty((NCHIP,) + g5.shape[2:], F32)

    def _gather(self, make, layer, keys):
        def done(arrays):
            self._w[layer].update(zip(keys, arrays))
        return make([self._w[layer][k] for k in keys], done)

    def _halves(self, layer, keys):
        return _exchange_halves([self._g5[layer][k] for k in keys], [self._recv[layer][k] for k in keys],
                                lambda arrays: self._halves_done(layer, keys, arrays))

    def _halves_done(self, layer, keys, arrays):
        n = len(keys)
        for k, g, r in zip(keys, arrays[:n], arrays[n:]):
            self._parts[layer][k], self._landing[layer][k] = _add_halves(
                "add_halves%d_%s" % (layer, k), g, r, self.place)

    def _chips(self, layer, keys):
        return _exchange_chips([self._parts[layer][k] for k in keys], [self._landing[layer][k] for k in keys],
                               lambda arrays: self._chips_done(layer, keys, arrays))

    def _chips_done(self, layer, keys, arrays):
        for k, t in zip(keys, arrays[len(keys):]):
            self.full[k] = _sum_chips("sum_chips%d_%s" % (layer, k), t, self.place, layer, self.full.get(k))

    def _share(self, layer, keys):
        def done(arrays):
            self.full.update(zip(keys, arrays))
        return _exchange_share([self.full[k] for k in keys], layer, done)

    def finish(self, small):
        small_red = _small_allreduce(small)
        _exchange_call("share_halves_last", self._share(0, list(_KEYS)))
        return [self.full[k] for k in _KEYS], small_red


def _adamw(name, w, g, m, v):
    shape = w.shape
    cols = shape[-1]
    rows = 1
    for s in shape[:-1]:
        rows *= s
    tr = rows
    for cand in (256, 128, 64):
        if rows > cand and rows % cand == 0:
            tr = cand
            break
    c1 = 1.0 / (1.0 - B1 ** STEP)
    c2 = 1.0 / (1.0 - B2 ** STEP)

    def body(w_ref, g_ref, m_ref, v_ref, go_ref, d_ref, nm_ref, nv_ref):
        gv = g_ref[...]
        go_ref[...] = gv
        nm = B1 * m_ref[...] + (1.0 - B1) * gv
        nv = B2 * v_ref[...] + (1.0 - B2) * (gv * gv)
        nm_ref[...] = nm
        nv_ref[...] = nv
        d_ref[...] = -LR * ((nm * c1) / (jnp.sqrt(nv * c2) + ADAM_EPS) + WD * w_ref[...])

    blk = pl.BlockSpec((tr, cols), lambda i: (i, 0))
    outs = pl.pallas_call(
        body, name=name, grid=(rows // tr,), in_specs=[blk] * 4, out_specs=[blk] * 4,
        out_shape=[_sds((rows, cols), F32)] * 4, compiler_params=_params(1),
    )(*(t.reshape(rows, cols) for t in (w, g, m, v)))
    return tuple(o.reshape(shape) for o in outs)


def _pack_small(small):
    rows = [jnp.sum(small["g_mix%d" % l], axis=0, keepdims=True) for l in range(DEPTH)]
    rows += [jnp.sum(small["pool_scale%d" % l], axis=0, keepdims=True) for l in range(DEPTH)]
    rows += [jnp.sum(small["g_ffn%d" % l], axis=0, keepdims=True) for l in range(DEPTH)]
    rows += [jnp.sum(small["g_ple%d" % l], axis=0, keepdims=True) for l in range(DEPTH)]
    rows += [jnp.sum(small["g_final"], axis=0, keepdims=True)]
    flat = [small["conv_b%d" % l].reshape(-1) for l in range(DEPTH)]
    flat += [small["conv_w%d" % l].reshape(-1) for l in range(DEPTH)]
    flat = jnp.concatenate(flat).reshape(-1, D)
    packed = jnp.concatenate(rows + [flat], axis=0)
    return jnp.pad(packed, ((0, SMALL_ROWS - packed.shape[0]), (0, 0)))


def _unpack_small(red):
    g_mix, pool_scale, g_ffn, g_ple = red[0:2], red[2:4], red[4:6], red[6:8]
    g_final = red[8]
    nb = DEPTH * UW // D
    conv_b = red[9:9 + nb].reshape(DEPTH, UW)
    conv_w = red[9 + nb:9 + 4 * nb].reshape(DEPTH, 3, UW)
    return g_mix, pool_scale, g_ffn, g_ple, g_final, conv_b, conv_w


def kernel(x, p, g_mix, w_in, w_ya, w_yb, pool_w, pool_scale, w_o, g_ffn, w_up, conv_w, conv_b, w_down, g_ple, w_ple, w_ple_gate, g_final, loss_target, m_g_mix, m_w_in, m_w_ya, m_w_yb, m_pool_w, m_pool_scale, m_w_o, m_g_ffn, m_w_up, m_conv_w, m_conv_b, m_w_down, m_g_ple, m_w_ple, m_w_ple_gate, m_g_final, v_g_mix, v_w_in, v_w_ya, v_w_yb, v_pool_w, v_pool_scale, v_w_o, v_g_ffn, v_w_up, v_conv_w, v_conv_b, v_w_down, v_g_ple, v_w_ple, v_w_ple_gate, v_g_final):
    me = 2 * lax.axis_index("x") + lax.axis_index("y")
    place = jnp.stack([lax.axis_index("c"), me]).astype(jnp.int32)

    def slot(shard):
        return lax.dynamic_update_index_in_dim(lax.empty((NCHIP,) + shard.shape, shard.dtype), shard, me, 0)

    packed = [
        w_in.astype(BF16), w_up.astype(BF16),
        jnp.concatenate([w_ya, w_ple, pool_w.reshape(DEPTH, 256, 256)], axis=1).astype(BF16),
        jnp.concatenate([w_yb, w_o, w_ple_gate], axis=1).astype(BF16),
        w_down.astype(BF16),
    ]
    slotted = [[slot(t[l]) for t in packed] for l in range(DEPTH)]
    ex = _Schedule(slotted, slot(conv_w.reshape(DEPTH * 3, UP_S)), place)
    cw_full = ex.cw4.reshape(NCHIP, DEPTH, 3, UP_S).transpose(1, 2, 0, 3).reshape(DEPTH, 3, UW)

    vecs = dict(g_mix=g_mix, pool_scale=pool_scale, g_ffn=g_ffn, g_ple=g_ple, g_final=g_final, conv_b=conv_b,
                conv_w=cw_full)
    sq8, grad_x, small = _local_step(x, p, loss_target, vecs, ex)
    loss = lax.psum(jnp.sum(sq8) * (0.5 / D), ("x", "y", "c"))

    full, small_red = ex.finish(_pack_small(small))
    r_in, r_up, r_sc, r_r3, r_dn = [f.reshape(DEPTH, -1, f.shape[-1]) for f in full]
    d_g_mix, d_pool_scale, d_g_ffn, d_g_ple, d_g_final, d_conv_b, d_conv_w_full = _unpack_small(small_red)
    d_conv_w = lax.dynamic_slice_in_dim(d_conv_w_full, me * UP_S, UP_S, axis=2)

    grads = dict(
        g_mix=d_g_mix, w_in=r_in, w_ya=r_sc[:, 0:512], w_yb=r_r3[:, 0:256],
        pool_w=r_sc[:, 768:1024].reshape(DEPTH, 4, 64, 256), pool_scale=d_pool_scale, w_o=r_r3[:, 256:512],
        g_ffn=d_g_ffn, w_up=r_up, conv_w=d_conv_w, conv_b=d_conv_b, w_down=r_dn, g_ple=d_g_ple,
        w_ple=r_sc[:, 512:768], w_ple_gate=r_r3[:, 512:768], g_final=d_g_final)
    weights = dict(g_mix=g_mix, w_in=w_in, w_ya=w_ya, w_yb=w_yb, pool_w=pool_w, pool_scale=pool_scale, w_o=w_o,
                   g_ffn=g_ffn, w_up=w_up, conv_w=conv_w, conv_b=conv_b, w_down=w_down, g_ple=g_ple, w_ple=w_ple,
                   w_ple_gate=w_ple_gate, g_final=g_final)
    m_in = dict(g_mix=m_g_mix, w_in=m_w_in, w_ya=m_w_ya, w_yb=m_w_yb, pool_w=m_pool_w, pool_scale=m_pool_scale,
                w_o=m_w_o, g_ffn=m_g_ffn, w_up=m_w_up, conv_w=m_conv_w, conv_b=m_conv_b, w_down=m_w_down,
                g_ple=m_g_ple, w_ple=m_w_ple, w_ple_gate=m_w_ple_gate, g_final=m_g_final)
    v_in = dict(g_mix=v_g_mix, w_in=v_w_in, w_ya=v_w_ya, w_yb=v_w_yb, pool_w=v_pool_w, pool_scale=v_pool_scale,
                w_o=v_w_o, g_ffn=v_g_ffn, w_up=v_w_up, conv_w=v_conv_w, conv_b=v_conv_b, w_down=v_w_down,
                g_ple=v_g_ple, w_ple=v_w_ple, w_ple_gate=v_w_ple_gate, g_final=v_g_final)
    names = ["g_mix", "w_in", "w_ya", "w_yb", "pool_w", "pool_scale", "w_o", "g_ffn", "w_up", "conv_w", "conv_b",
             "w_down", "g_ple", "w_ple", "w_ple_gate", "g_final"]
    deltas, new_m, new_v = [], [], []
    for nme in names:
        gr = grads[nme].reshape(weights[nme].shape)
        grads[nme], dlt, nm, nv = _adamw("adamw_" + nme, weights[nme], gr, m_in[nme], v_in[nme])
        deltas.append(dlt)
        new_m.append(nm)
        new_v.append(nv)
    return (loss, grad_x, *[grads[nme] for nme in names], *deltas, *new_m, *new_v)
```

```python
import math

import jax
import jax.numpy as jnp
from jax import lax
from jax.experimental import pallas as pl
from jax.experimental.pallas import tpu as pltpu

F32 = jnp.float32
BF16 = jnp.bfloat16
_KEYS = ("in", "up", "sc", "r3", "dn")
MESH = pl.DeviceIdType.MESH

D = 1024
SEQ = 2048
DEPTH = 2
HEAD = 128
GROUP_W = 512
DILATIONS = (1, 4, 16)
ROPE_DIM = 32
ROPE_THETA = 500000.0
NEG_INF = -1e30
ZW = 7680
OFF_K, OFF_V, OFF_U, OFF_GA, OFF_GB = 1536, 3072, 4608, 5632, 6656
FF = 2816
UW = 2 * FF
PLE = 256
NCHIP = 4
IN_S, UP_S, DN_S = ZW // NCHIP, UW // NCHIP, FF // NCHIP
RMS_EPS = 1e-6
LR, B1, B2, ADAM_EPS, WD, STEP = 0.001, 0.9, 0.999, 1e-08, 0.01, 10
SMALL_ROWS = 56
VMEM_CAP = 48 * 1024 * 1024
VMEM_BIG = 58 * 1024 * 1024


def _params(n_grid, vmem=VMEM_CAP):
    return pltpu.CompilerParams(dimension_semantics=("arbitrary",) * n_grid, vmem_limit_bytes=vmem)


def _sigmoid(v):
    return 1.0 / (1.0 + jnp.exp(-v))


def _rows8(v):
    return jnp.sum(v.reshape(v.shape[0] // 8, 8, v.shape[1]), axis=0)


def _sds(shape, dtype):
    return jax.ShapeDtypeStruct(shape, dtype)


def _dot(av, bv, ta=False, tb=False):
    dims = (((0,) if ta else (1,), (1,) if tb else (0,)), ((), ()))
    return lax.dot_general(av.astype(BF16), bv.astype(BF16), dims, preferred_element_type=F32)


def _call(body, *, name, grid, in_specs, out_specs, out_shape, scratch_shapes=(), aliases=None, comm=None,
          vmem=VMEM_CAP):
    params = _params(len(grid), vmem)
    aliases = dict(aliases or {})
    if comm is None:
        return pl.pallas_call(body, name=name, grid=grid, in_specs=list(in_specs), out_specs=out_specs,
                              out_shape=out_shape, scratch_shapes=list(scratch_shapes),
                              input_output_aliases=aliases, compiler_params=params)
    single = not isinstance(out_shape, (list, tuple))
    out_specs_l = [out_specs] if single else list(out_specs)
    out_shape_l = [out_shape] if single else list(out_shape)
    n_in, n_out, n_c = len(in_specs), len(out_shape_l), len(comm["arrays"])

    def hosted(*refs):
        core_in, core_out = refs[:n_in], refs[n_in + n_c:n_in + n_c + n_out]
        c_refs = refs[n_in + n_c + n_out:n_in + 2 * n_c + n_out]
        scratch, (send_sems, recv_sems) = refs[n_in + 2 * n_c + n_out:-2], refs[-2:]
        ids = [pl.program_id(i) for i in range(len(grid))]
        first, last = ids[0] == 0, ids[0] == grid[0] - 1
        for i in range(1, len(grid)):
            first = jnp.logical_and(first, ids[i] == 0)
            last = jnp.logical_and(last, ids[i] == grid[i] - 1)

        @pl.when(first)
        def _():
            comm["start"](c_refs, send_sems, recv_sems)

        body(*core_in, *core_out, *scratch)

        @pl.when(last)
        def _():
            comm["wait"](c_refs, send_sems, recv_sems)

    any_spec = pl.BlockSpec(memory_space=pl.ANY)
    for i in range(n_c):
        aliases[n_in + i] = n_out + i
    call = pl.pallas_call(
        hosted, name=name, grid=grid, in_specs=list(in_specs) + [any_spec] * n_c,
        out_specs=out_specs_l + [any_spec] * n_c,
        out_shape=out_shape_l + [_sds(t.shape, t.dtype) for t in comm["arrays"]],
        scratch_shapes=list(scratch_shapes) + [pltpu.SemaphoreType.DMA((comm["nsem"],))] * 2,
        input_output_aliases=aliases, compiler_params=params)

    def run(*args):
        outs = call(*args, *comm["arrays"])
        comm["done"](outs[n_out:])
        return outs[0] if single else outs[:n_out]

    return run


def _mm_call(name, a, b, *, grid, a_spec, b_spec, o_spec, out_shape, ta=False, tb=False, k_axis=None, nk=1,
             acc_shape=None, res=None, res_spec=None, buf=None, compute=None, comm=None,
             extra_in=(), extra_out=(), epilogue=None, vmem=VMEM_CAP):
    has_res, has_buf = res is not None, buf is not None
    in_place = nk > 1 and not has_res and out_shape.dtype == F32 and epilogue is None
    n_xi, n_xo = len(extra_in), len(extra_out)

    def body(*refs):
        a_ref, b_ref = refs[0], refs[1]
        pos = 2
        r_ref = None
        if has_res:
            r_ref = refs[pos]
            pos += 1
        if has_buf:
            pos += 1
        x_refs = refs[pos:pos + n_xi]
        pos += n_xi
        first_rows = pl.program_id(0) == 0
        o_ref = refs[pos]
        y_refs = refs[pos + 1:pos + 1 + n_xo]
        if compute is None:
            av = a_ref[...]
            bv = b_ref[...]
            part = _dot(av.reshape(-1, av.shape[-1]), bv.reshape(-1, bv.shape[-1]), ta, tb)
        else:
            part = compute(a_ref, b_ref)

        def finish(val):
            if r_ref is not None:
                val = val + r_ref[...]
            if epilogue is not None:
                epilogue(val, x_refs, o_ref, y_refs, first_rows)
            else:
                o_ref[...] = val.reshape(o_ref.shape).astype(o_ref.dtype)

        if nk == 1:
            finish(part)
        elif in_place:
            @pl.when(pl.program_id(k_axis) == 0)
            def _():
                o_ref[...] = jnp.zeros(o_ref.shape, F32)

            o_ref[...] += part.reshape(o_ref.shape)
        else:
            acc_ref = refs[pos + 1 + n_xo]
            k = pl.program_id(k_axis)

            @pl.when(k == 0)
            def _():
                acc_ref[...] = jnp.zeros(acc_ref.shape, F32)

            acc_ref[...] += part

            @pl.when(k == nk - 1)
            def _():
                finish(acc_ref[...])

    ins, in_specs = [a, b], [a_spec, b_spec]
    if has_res:
        ins.append(res)
        in_specs.append(res_spec)
    aliases = {}
    if has_buf:
        aliases = {len(ins): 0}
        ins.append(buf)
        in_specs.append(pl.BlockSpec(memory_space=pl.ANY))
    for arr, sp in extra_in:
        ins.append(arr)
        in_specs.append(sp)
    scratch = [pltpu.VMEM(acc_shape, F32)] if nk > 1 and not in_place else []
    if not extra_out:
        return _call(body, name=name, grid=grid, in_specs=in_specs, out_specs=o_spec, out_shape=out_shape,
                     scratch_shapes=scratch, aliases=aliases, comm=comm, vmem=vmem)(*ins)
    return _call(body, name=name, grid=grid, in_specs=in_specs, out_specs=[o_spec] + [sp for _, sp in extra_out],
                 out_shape=[out_shape] + [sh for sh, _ in extra_out], scratch_shapes=scratch, aliases=aliases,
                 comm=comm, vmem=vmem)(*ins)


def _rms_fwd(name, x, g, tr=512):
    T = x.shape[0]

    def body(x_ref, g_ref, h_ref):
        xv = x_ref[...]
        r = lax.rsqrt(jnp.mean(xv * xv, axis=-1, keepdims=True) + RMS_EPS)
        h_ref[...] = (xv * r * g_ref[...]).astype(BF16)

    return pl.pallas_call(
        body, name=name, grid=(T // tr,),
        in_specs=[pl.BlockSpec((tr, D), lambda i: (i, 0)), pl.BlockSpec((1, D), lambda i: (0, 0))],
        out_specs=pl.BlockSpec((tr, D), lambda i: (i, 0)), out_shape=_sds((T, D), BF16),
        compiler_params=_params(1),
    )(x, g)


def _final_loss(x, g, tgt, tr=512):
    T = x.shape[0]

    def body(x_ref, g_ref, t_ref, dx_ref, dg_ref, sq_ref):
        xv = x_ref[...]
        r = lax.rsqrt(jnp.mean(xv * xv, axis=-1, keepdims=True) + RMS_EPS)
        xh = xv * r
        gv = g_ref[...]
        e = xh * gv - t_ref[...]
        dy = e * (1.0 / D)
        pg = _rows8(dy * xh)
        ps = _rows8(e * e)

        @pl.when(pl.program_id(0) == 0)
        def _():
            dg_ref[...] = pg
            sq_ref[...] = ps

        @pl.when(pl.program_id(0) > 0)
        def _():
            dg_ref[...] += pg
            sq_ref[...] += ps

        dxh = dy * gv
        dx_ref[...] = r * (dxh - xh * jnp.mean(dxh * xh, axis=-1, keepdims=True))

    row = pl.BlockSpec((tr, D), lambda i: (i, 0))
    acc = pl.BlockSpec((8, D), lambda i: (0, 0))
    return pl.pallas_call(
        body, name="final_loss", grid=(T // tr,),
        in_specs=[row, pl.BlockSpec((1, D), lambda i: (0, 0)), row],
        out_specs=[row, acc, acc],
        out_shape=[_sds((T, D), F32), _sds((8, D), F32), _sds((8, D), F32)],
        compiler_params=_params(1),
    )(x, g, tgt)


def _ple_bwd(name, dx, pe, pg, tr=512):
    T = dx.shape[0]

    def body(dx_ref, pe_ref, pg_ref, dpe_ref, dpg_ref):
        s = _sigmoid(pg_ref[...])
        dxv = dx_ref[...]
        dpe_ref[...] = (dxv * s).astype(BF16)
        dpg_ref[...] = (dxv * pe_ref[...] * s * (1.0 - s)).astype(BF16)

    row = pl.BlockSpec((tr, D), lambda i: (i, 0))
    return pl.pallas_call(
        body, name=name, grid=(T // tr,), in_specs=[row, row, row], out_specs=[row, row],
        out_shape=[_sds((T, D), BF16), _sds((T, D), BF16)], compiler_params=_params(1),
    )(dx, pe, pg)


def _gate_fwd(name, z, ya, yb, tr=512):
    T = z.shape[0]
    w = 512

    def body(ga_ref, gb_ref, ya_ref, yb_ref, o_ref):
        o_ref[...] = (_sigmoid(ga_ref[...]) * ya_ref[...].astype(F32)
                      + _sigmoid(gb_ref[...]) * yb_ref[...].astype(F32)).astype(BF16)

    col = pl.BlockSpec((tr, w), lambda i, j: (i, j))
    return pl.pallas_call(
        body, name=name, grid=(T // tr, D // w),
        in_specs=[pl.BlockSpec((tr, w), lambda i, j: (i, OFF_GA // w + j)),
                  pl.BlockSpec((tr, w), lambda i, j: (i, OFF_GB // w + j)), col, col],
        out_specs=col, out_shape=_sds((T, D), BF16), compiler_params=_params(2),
    )(z, z, ya, yb)


def _gate_bwd(name, z, off, y, dm, dz, tr=512):
    T = z.shape[0]
    w = 512
    has_dz = dz is not None

    def body(*refs):
        g_ref, y_ref, dm_ref = refs[:3]
        dy_ref, dz_ref = refs[-2:]
        s = _sigmoid(g_ref[...])
        dmv = dm_ref[...].astype(F32)
        dy_ref[...] = (dmv * s).astype(BF16)
        dz_ref[...] = (dmv * y_ref[...].astype(F32) * s * (1.0 - s)).astype(BF16)

    col = pl.BlockSpec((tr, w), lambda i, j: (i, j))
    gcol = pl.BlockSpec((tr, w), lambda i, j: (i, off // w + j))
    ins, in_specs, aliases = [z, y, dm], [gcol, col, col], {}
    if has_dz:
        ins.append(dz)
        in_specs.append(pl.BlockSpec(memory_space=pl.ANY))
        aliases = {3: 1}
    return pl.pallas_call(
        body, name=name, grid=(T // tr, D // w), in_specs=in_specs, out_specs=[col, gcol],
        out_shape=[_sds((T, D), BF16), _sds((T, ZW), BF16)], input_output_aliases=aliases,
        compiler_params=_params(2),
    )(*ins)


def _shift_down(v, k, rows):
    return jnp.where(rows >= k, pltpu.roll(v, k, 0), 0.0)


def _shift_up(v, k, rows):
    n = v.shape[0]
    return jnp.where(rows < n - k, pltpu.roll(v, n - k, 0), 0.0)


def _pool_window(v, g, rows, shift):
    s2 = v + shift(v, 1, rows)
    s4 = s2 + shift(s2, 2, rows)
    s8 = s4 + shift(s4, 4, rows)
    s16 = s8 + shift(s8, 8, rows)
    return jnp.where(g == 0, s2, jnp.where(g == 1, s4, jnp.where(g == 2, s8, s16)))


def _pool_count(g, rows):
    wlen = jnp.left_shift(2, g).astype(F32)
    return jnp.minimum(rows.astype(F32) + 1.0, wlen)


def _pool_fwd(name, z3, g_sc, scale):
    Bn = z3.shape[0]
    gw = 256

    def body(u_ref, pw_ref, sc_ref, pooled_ref, ms_ref):
        g = pl.program_id(1)
        u = u_ref[...]
        rows = lax.broadcasted_iota(jnp.int32, u.shape, 0)
        pooled = (_pool_window(u, g, rows, _shift_down) / _pool_count(g, rows) - u).astype(BF16)
        pooled_ref[...] = pooled
        pw = pw_ref[...].reshape(gw, gw)
        mixed = jnp.dot(pooled, pw, preferred_element_type=F32)
        ms_ref[...] = (mixed * sc_ref[...]).astype(BF16)

    blk = pl.BlockSpec((None, SEQ, gw), lambda b, g: (b, 0, g))
    return pl.pallas_call(
        body, name=name, grid=(Bn, 4),
        in_specs=[pl.BlockSpec((None, SEQ, gw), lambda b, g: (b, 0, OFF_U // gw + g)),
                  pl.BlockSpec((NCHIP, 64, gw), lambda b, g: (0, 12 + g, 0)),
                  pl.BlockSpec((1, gw), lambda b, g: (0, g))],
        out_specs=[blk, blk],
        out_shape=[_sds((Bn, SEQ, D), BF16), _sds((Bn, SEQ, D), BF16)],
        compiler_params=_params(2),
    )(z3, g_sc, scale)


def _pool_bwd(name, dms3, pooled3, g_sc, scale, dz3, gg_sc):
    Bn = dms3.shape[0]
    gw = 256
    has_gg = gg_sc is not None

    def body(*refs):
        dms_ref, pooled_ref, pw_ref, sc_ref = refs[:4]
        dz_ref, dpw_ref, dsc_ref = refs[-3:]
        g, b = pl.program_id(0), pl.program_id(1)
        pooled = pooled_ref[...]
        pw = pw_ref[...].reshape(gw, gw)
        dms = dms_ref[...]
        mixed = jnp.dot(pooled, pw, preferred_element_type=F32)
        psc = _rows8(dms * mixed)
        dmixed = (dms * sc_ref[...]).astype(BF16)
        dpw = lax.dot_general(pooled, dmixed, (((0,), (0,)), ((), ())), preferred_element_type=F32)
        dpw = dpw.reshape(NCHIP, 64, gw)

        @pl.when(b == 0)
        def _():
            dsc_ref[...] = psc
            dpw_ref[...] = dpw

        @pl.when(b > 0)
        def _():
            dsc_ref[...] += psc
            dpw_ref[...] += dpw

        dpooled = lax.dot_general(dmixed, pw, (((1,), (1,)), ((), ())), preferred_element_type=F32)
        rows = lax.broadcasted_iota(jnp.int32, dpooled.shape, 0)
        dq = dpooled / _pool_count(g, rows)
        dz_ref[...] = (_pool_window(dq, g, rows, _shift_up) - dpooled).astype(BF16)

    ins = [dms3, pooled3, g_sc, scale, dz3]
    in_specs = [pl.BlockSpec((None, SEQ, gw), lambda g, b: (b, 0, g)),
                pl.BlockSpec((None, SEQ, gw), lambda g, b: (b, 0, g)),
                pl.BlockSpec((NCHIP, 64, gw), lambda g, b: (0, 12 + g, 0)),
                pl.BlockSpec((1, gw), lambda g, b: (0, g)),
                pl.BlockSpec(memory_space=pl.ANY)]
    aliases = {4: 0}
    if has_gg:
        ins.append(gg_sc)
        in_specs.append(pl.BlockSpec(memory_space=pl.ANY))
        aliases[5] = 1
    return pl.pallas_call(
        body, name=name, grid=(4, Bn), in_specs=in_specs,
        out_specs=[pl.BlockSpec((None, SEQ, gw), lambda g, b: (b, 0, OFF_U // gw + g)),
                   pl.BlockSpec((NCHIP, 64, gw), lambda g, b: (0, 12 + g, 0)),
                   pl.BlockSpec((8, gw), lambda g, b: (0, g))],
        out_shape=[_sds(dz3.shape, BF16), _sds((NCHIP, D, 256), F32), _sds((8, D), F32)],
        input_output_aliases=aliases, compiler_params=_params(2),
    )(*ins)


CT = 256
NCT = FF // CT


def _conv_pre(u, cw_ref, cb_ref, rows):
    return (cb_ref[...] + cw_ref[0:1, :] * _shift_down(u, 2, rows) + cw_ref[1:2, :] * _shift_down(u, 1, rows)
            + cw_ref[2:3, :] * u)


def _conv_fwd(name, u3, cw, cb, comm=None):
    Bn = u3.shape[0]

    def body(ug_ref, uv_ref, cwg_ref, cwv_ref, cbg_ref, cbv_ref, a_ref, yg_ref, yv_ref):
        ug, uv = ug_ref[...], uv_ref[...]
        rows = lax.broadcasted_iota(jnp.int32, ug.shape, 0)
        yg = _conv_pre(ug, cwg_ref, cbg_ref, rows)
        yv = _conv_pre(uv, cwv_ref, cbv_ref, rows)
        yg_ref[...] = yg.astype(BF16)
        yv_ref[...] = yv.astype(BF16)
        a_ref[...] = (yg * _sigmoid(yg) * yv).astype(BF16)

    def blk(off):
        return pl.BlockSpec((None, SEQ, CT), lambda b, c: (b, 0, off + c))

    return _call(
        body, name=name, grid=(Bn, NCT),
        in_specs=[blk(0), blk(NCT),
                  pl.BlockSpec((3, CT), lambda b, c: (0, c)), pl.BlockSpec((3, CT), lambda b, c: (0, NCT + c)),
                  pl.BlockSpec((1, CT), lambda b, c: (0, c)), pl.BlockSpec((1, CT), lambda b, c: (0, NCT + c))],
        out_specs=[blk(0)] * 3,
        out_shape=[_sds((Bn, SEQ, FF), BF16)] * 3, comm=comm,
    )(u3, u3, cw, cw, cb, cb)


def _conv_bwd(name, da3, u3, yg3, yv3, cw, comm=None):
    Bn = u3.shape[0]
    last = NCT * Bn - 1
    R = 128

    def body(da_ref, ug_ref, uv_ref, yg_ref, yv_ref, cwg_ref, cwv_ref,
             du_ref, dcwg_ref, dcwv_ref, dcbg_ref, dcbv_ref, stage_g, stage_v, sems):
        c, b = pl.program_id(0), pl.program_id(1)
        step = c * Bn + b

        def writes(off_c, stage, sem):
            col = pl.multiple_of(off_c + c * CT, CT)
            return pltpu.make_async_copy(stage, du_ref.at[b, :, pl.ds(col, CT)], sem)

        @pl.when(step > 0)
        def _():
            writes(0, stage_g, sems.at[0]).wait()
            writes(FF, stage_v, sems.at[1]).wait()

        cwg, cwv = cwg_ref[...], cwv_ref[...]
        tail_rows = lax.broadcasted_iota(jnp.int32, (R, CT), 0)

        def chunk(c0, acc, at_end):
            n = R if at_end else R + 16
            yg, yv = yg_ref[pl.ds(c0, n), :].astype(F32), yv_ref[pl.ds(c0, n), :].astype(F32)
            da = da_ref[pl.ds(c0, n), :].astype(F32)
            s = _sigmoid(yg)
            dyv = da * (yg * s)
            dyg = da * yv * (s * (1.0 + yg * (1.0 - s)))
            out = []
            for dy, u_ref, cwt, stage in ((dyg, ug_ref, cwg, stage_g), (dyv, uv_ref, cwv, stage_v)):
                if at_end:
                    d0 = dy
                    d1 = jnp.where(tail_rows < R - 1, pltpu.roll(dy, R - 1, 0), 0.0)
                    d2 = jnp.where(tail_rows < R - 2, pltpu.roll(dy, R - 2, 0), 0.0)
                else:
                    d0 = dy[0:R, :]
                    d1 = pltpu.roll(dy, n - 1, 0)[0:R, :]
                    d2 = pltpu.roll(dy, n - 2, 0)[0:R, :]
                stage[pl.ds(c0, R), :] = (cwt[2:3, :] * d0 + cwt[1:2, :] * d1 + cwt[0:1, :] * d2).astype(BF16)
                u = u_ref[pl.ds(c0, R), :]
                out += [jnp.sum(d2 * u, axis=0, keepdims=True), jnp.sum(d1 * u, axis=0, keepdims=True),
                        jnp.sum(d0 * u, axis=0, keepdims=True), jnp.sum(d0, axis=0, keepdims=True)]
            return tuple(a + o for a, o in zip(acc, out))

        zero = jnp.zeros((1, CT), F32)
        acc = lax.fori_loop(0, SEQ // R - 1, lambda i, acc: chunk(pl.multiple_of(i * R, R), acc, False), (zero,) * 8)
        acc = chunk(SEQ - R, acc, True)
        for dcw_ref, dcb_ref, part in ((dcwg_ref, dcbg_ref, acc[0:4]), (dcwv_ref, dcbv_ref, acc[4:8])):
            dcw = jnp.concatenate(part[0:3], axis=0)

            @pl.when(b == 0)
            def _():
                dcw_ref[...] = dcw
                dcb_ref[...] = part[3]

            @pl.when(b > 0)
            def _():
                dcw_ref[...] += dcw
                dcb_ref[...] += part[3]

        writes(0, stage_g, sems.at[0]).start()
        writes(FF, stage_v, sems.at[1]).start()

        @pl.when(step == last)
        def _():
            writes(0, stage_g, sems.at[0]).wait()
            writes(FF, stage_v, sems.at[1]).wait()

    def blk(off):
        return pl.BlockSpec((None, SEQ, CT), lambda c, b: (b, 0, off + c))

    def vec(r, off):
        return pl.BlockSpec((r, CT), lambda c, b: (0, off + c))

    du3, dcwg, dcwv, dcbg, dcbv = _call(
        body, name=name, grid=(NCT, Bn),
        in_specs=[blk(0), blk(0), blk(NCT), blk(0), blk(0), vec(3, 0), vec(3, NCT)],
        out_specs=[pl.BlockSpec(memory_space=pl.ANY), vec(3, 0), vec(3, 0), vec(1, 0), vec(1, 0)],
        out_shape=[_sds((Bn, SEQ, UW), BF16), _sds((3, FF), F32), _sds((3, FF), F32), _sds((1, FF), F32),
                   _sds((1, FF), F32)],
        scratch_shapes=[pltpu.VMEM((SEQ, CT), BF16)] * 2 + [pltpu.SemaphoreType.DMA((2,))], comm=comm,
    )(da3, u3, u3, yg3, yv3, cw, cw)
    return du3, jnp.concatenate([dcwg, dcwv], axis=1), jnp.concatenate([dcbg, dcbv], axis=1)


def _rope_tables():
    pos = jnp.arange(SEQ, dtype=F32)
    inv_freq = jnp.exp(jnp.arange(0, ROPE_DIM, 2, dtype=F32) * (-math.log(ROPE_THETA) / ROPE_DIM))
    ang = pos[:, None] * inv_freq[None, :]
    cos, sin = jnp.cos(ang), jnp.sin(ang)
    half = ROPE_DIM // 2
    zeros = jnp.zeros((SEQ, HEAD - ROPE_DIM), F32)
    zh = jnp.zeros((SEQ, half), F32)
    tab_c = jnp.concatenate([cos, cos, zeros + 1.0], axis=1)
    tab_a = jnp.concatenate([-sin, zh, zeros], axis=1)
    tab_b = jnp.concatenate([zh, sin, zeros], axis=1)
    return tab_c, tab_a, tab_b


def _rot(v, tc, ta, tb):
    half = ROPE_DIM // 2
    return v * tc + pltpu.roll(v, HEAD - half, 1) * ta + pltpu.roll(v, half, 1) * tb


def _rot_t(dv, tc, ta, tb):
    half = ROPE_DIM // 2
    return dv * tc + pltpu.roll(dv * ta, half, 1) + pltpu.roll(dv * tb, HEAD - half, 1)


def _band_masks():
    qi = lax.broadcasted_iota(jnp.int32, (HEAD, 2 * HEAD), 0)
    ki = lax.broadcasted_iota(jnp.int32, (HEAD, 2 * HEAD), 1)
    diff = HEAD + qi - ki
    both = (diff >= 0) & (diff <= HEAD)
    q1 = lax.broadcasted_iota(jnp.int32, (HEAD, HEAD), 0)
    k1 = lax.broadcasted_iota(jnp.int32, (HEAD, HEAD), 1)
    return q1 >= k1, both


_NT = (((1,), (1,)), ((), ()))
_TN = (((0,), (0,)), ((), ()))
_SCALE = HEAD ** -0.5


ATT_W = HEAD
ATT_HP = ATT_W // HEAD


def _res_rows(r, n, d, base=0):
    return pl.ds(base * d + r, n, stride=d) if d > 1 else pl.ds(base, n)


def _attn_load(q_ref, k_ref, v_ref, tc_ref, ta_ref, tb_ref, qs, ks, vs, d):
    L = SEQ // d
    for r in range(d):
        rows = _res_rows(r, L, d)
        dst = slice(r * L, (r + 1) * L)
        tc, ta, tb = tc_ref[dst, :], ta_ref[dst, :], tb_ref[dst, :]
        for hh in range(ATT_HP):
            sl = slice(hh * HEAD, (hh + 1) * HEAD)
            qs[dst, sl] = _rot(q_ref[rows, sl], tc, ta, tb).astype(BF16)
            ks[dst, sl] = _rot(k_ref[rows, sl], tc, ta, tb).astype(BF16)
            vs[dst, sl] = v_ref[rows, sl].astype(BF16)


def _attn_fwd(name, z3, tabs, g, d, comm=None):
    Bn = z3.shape[0]
    L = SEQ // d
    nb = L // HEAD
    W, nh = ATT_W, GROUP_W // ATT_W

    def body(q_ref, k_ref, v_ref, tc_ref, ta_ref, tb_ref, o_ref, l_ref, qs, ks, vs, sc, pc):
        m_first, m_both = _band_masks()
        _attn_load(q_ref, k_ref, v_ref, tc_ref, ta_ref, tb_ref, qs, ks, vs, d)
        blocks = [(r, n) for r in range(d) for n in range(nb)]

        def spans(r, n):
            rq = slice(r * L + n * HEAD, r * L + (n + 1) * HEAD)
            rk = slice(r * L + max(n - 1, 0) * HEAD, r * L + (n + 1) * HEAD)
            return rq, rk, slice(0, HEAD if n == 0 else 2 * HEAD)

        for i, (r, n) in enumerate(blocks):
            rq, rk, kc = spans(r, n)
            sc[i, :, kc] = lax.dot_general(qs[rq, :], ks[rk, :], _NT, preferred_element_type=F32)
        for i, (r, n) in enumerate(blocks):
            rq, rk, kc = spans(r, n)
            s = jnp.where(m_first if n == 0 else m_both, sc[i, :, kc] * _SCALE, NEG_INF)
            m = jnp.max(s, axis=-1, keepdims=True)
            e = jnp.exp(s - m)
            den = jnp.sum(e, axis=-1, keepdims=True)
            pc[i, :, kc] = (e * (1.0 / den)).astype(BF16)
            l_ref[_res_rows(r, HEAD, d, n * HEAD), :] = jnp.broadcast_to(m + jnp.log(den), (HEAD, HEAD))
        for i, (r, n) in enumerate(blocks):
            rq, rk, kc = spans(r, n)
            o_ref[_res_rows(r, HEAD, d, n * HEAD), :] = jnp.dot(pc[i, :, kc], vs[rk, :], preferred_element_type=F32)

    def zcol(off):
        return pl.BlockSpec((None, SEQ, W), lambda b, h: (b, 0, (off + g * GROUP_W) // W + h))

    tab = pl.BlockSpec((SEQ, HEAD), lambda b, h: (0, 0))
    out = pl.BlockSpec((None, SEQ, W), lambda b, h: (b, 0, h))
    return _call(
        body, name=name, grid=(Bn, nh),
        in_specs=[zcol(0), zcol(OFF_K), zcol(OFF_V), tab, tab, tab],
        out_specs=[out, out],
        out_shape=[_sds((Bn, SEQ, GROUP_W), F32), _sds((Bn, SEQ, GROUP_W), F32)],
        scratch_shapes=[pltpu.VMEM((SEQ, W), BF16)] * 3
        + [pltpu.VMEM((SEQ // HEAD, HEAD, 2 * HEAD), F32), pltpu.VMEM((SEQ // HEAD, HEAD, 2 * HEAD), BF16)],
        comm=comm,
    )(z3, z3, z3, *tabs)


def _attn_bwd(name, z3, tabs, g, d, do3, lse3, delta3, dz3, comm=None):
    Bn = z3.shape[0]
    L = SEQ // d
    nb = L // HEAD
    W, nh = ATT_W, GROUP_W // ATT_W

    def body(q_ref, k_ref, v_ref, tc_ref, ta_ref, tb_ref, do_ref, l_ref, dl_ref, dz_in, dz_ref,
             qs, ks, vs, dos, dqs, dks, dvs, nat, oq, ok, ov, sc, dpc, pc, dsc, sems):
        b, h = pl.program_id(0), pl.program_id(1)
        m_first, m_both = _band_masks()
        _attn_load(q_ref, k_ref, v_ref, tc_ref, ta_ref, tb_ref, qs, ks, vs, d)
        for r in range(d):
            dos[r * L:(r + 1) * L, :] = do_ref[_res_rows(r, L, d), :].astype(BF16)
        dks[...] = jnp.zeros_like(dks)
        dvs[...] = jnp.zeros_like(dvs)
        blocks = [(r, n) for r in range(d) for n in range(nb)]

        def spans(r, n):
            rq = slice(r * L + n * HEAD, r * L + (n + 1) * HEAD)
            rk = slice(r * L + max(n - 1, 0) * HEAD, r * L + (n + 1) * HEAD)
            return rq, rk, slice(0, HEAD if n == 0 else 2 * HEAD)

        for i, (r, n) in enumerate(blocks):
            rq, rk, kc = spans(r, n)
            sc[i, :, kc] = lax.dot_general(qs[rq, :], ks[rk, :], _NT, preferred_element_type=F32)
            dpc[i, :, kc] = lax.dot_general(dos[rq, :], vs[rk, :], _NT, preferred_element_type=F32)
        for i, (r, n) in enumerate(blocks):
            rq, rk, kc = spans(r, n)
            rows = _res_rows(r, HEAD, d, n * HEAD)
            s = jnp.where(m_first if n == 0 else m_both, sc[i, :, kc] * _SCALE, NEG_INF)
            p = jnp.exp(s - l_ref[rows, :][:, 0:1])
            pc[i, :, kc] = p.astype(BF16)
            dsc[i, :, kc] = (p * (dpc[i, :, kc] - dl_ref[rows, :][:, 0:1]) * _SCALE).astype(BF16)
        for i, (r, n) in enumerate(blocks):
            rq, rk, kc = spans(r, n)
            dqs[rq, :] = jnp.dot(dsc[i, :, kc], ks[rk, :], preferred_element_type=F32)
        for i, (r, n) in enumerate(blocks):
            rq, rk, kc = spans(r, n)
            dks[rk, :] += lax.dot_general(dsc[i, :, kc], qs[rq, :], _TN, preferred_element_type=F32)
            dvs[rk, :] += lax.dot_general(pc[i, :, kc], dos[rq, :], _TN, preferred_element_type=F32)
        step = b * nh + h

        def writes():
            base = g * GROUP_W + h * W
            return [pltpu.make_async_copy(src, dz_ref.at[b, :, pl.ds(pl.multiple_of(base + off, HEAD), W)],
                                          sems.at[i])
                    for i, (src, off) in enumerate(((oq, 0), (ok, OFF_K), (ov, OFF_V)))]

        @pl.when(step > 0)
        def _():
            for cp in writes():
                cp.wait()

        for src, dst, rotate in ((dqs, oq, True), (dks, ok, True), (dvs, ov, False)):
            for r in range(d):
                val = src[r * L:(r + 1) * L, :]
                if rotate:
                    rm = slice(r * L, (r + 1) * L)
                    val = _rot_t(val, tc_ref[rm, :], ta_ref[rm, :], tb_ref[rm, :])
                nat[_res_rows(r, L, d), :] = val
            dst[...] = nat[...].astype(BF16)
        for cp in writes():
            cp.start()

        @pl.when(step == Bn * nh - 1)
        def _():
            for cp in writes():
                cp.wait()

    def zcol(off):
        return pl.BlockSpec((None, SEQ, W), lambda b, h: (b, 0, (off + g * GROUP_W) // W + h))

    tab = pl.BlockSpec((SEQ, HEAD), lambda b, h: (0, 0))
    gcol = pl.BlockSpec((None, SEQ, W), lambda b, h: (b, 0, h))
    any_spec = pl.BlockSpec(memory_space=pl.ANY)
    return _call(
        body, name=name, grid=(Bn, nh),
        in_specs=[zcol(0), zcol(OFF_K), zcol(OFF_V), tab, tab, tab, gcol, gcol, gcol, any_spec],
        out_specs=any_spec,
        out_shape=_sds((Bn, SEQ, ZW), BF16),
        scratch_shapes=[pltpu.VMEM((SEQ, W), BF16)] * 4 + [pltpu.VMEM((SEQ, W), F32)] * 4
        + [pltpu.VMEM((SEQ, W), BF16)] * 3
        + [pltpu.VMEM((SEQ // HEAD, HEAD, 2 * HEAD), F32)] * 2 + [pltpu.VMEM((SEQ // HEAD, HEAD, 2 * HEAD), BF16)] * 2
        + [pltpu.SemaphoreType.DMA((3,))],
        aliases={9: 0}, comm=comm,
    )(z3, z3, z3, *tabs, do3, lse3, delta3, dz3)


def _merge_weights(l0, l1, l2):
    m = jnp.maximum(jnp.maximum(l0, l1), l2)
    e0, e1, e2 = jnp.exp(l0 - m), jnp.exp(l1 - m), jnp.exp(l2 - m)
    inv = 1.0 / (e0 + e1 + e2)
    return e0 * inv, e1 * inv, e2 * inv


def _merge_fwd(name, outs, lses, tr=512):
    T = outs[0].shape[0]

    def body(o0, o1, o2, l0, l1, l2, a_ref):
        w0, w1, w2 = _merge_weights(l0[...], l1[...], l2[...])
        a_ref[...] = (w0 * o0[...] + w1 * o1[...] + w2 * o2[...]).astype(BF16)

    row = pl.BlockSpec((tr, GROUP_W), lambda i: (i, 0))
    return pl.pallas_call(
        body, name=name, grid=(T // tr,), in_specs=[row] * 6, out_specs=row,
        out_shape=_sds((T, GROUP_W), BF16), compiler_params=_params(1),
    )(*outs, *lses)


def _merge_bwd(name, outs, lses, dattn, tr=512):
    T = outs[0].shape[0]

    def body(o0, o1, o2, l0, l1, l2, da_ref, d0, d1, d2, e0, e1, e2):
        w = _merge_weights(l0[...], l1[...], l2[...])
        da = da_ref[...]
        attn = w[0] * o0[...] + w[1] * o1[...] + w[2] * o2[...]
        prod = da * attn
        csum = jnp.concatenate(
            [jnp.broadcast_to(jnp.sum(prod[:, hh * HEAD:(hh + 1) * HEAD], axis=-1, keepdims=True), (tr, HEAD))
             for hh in range(GROUP_W // HEAD)], axis=1)
        for wg, d_ref, e_ref in zip(w, (d0, d1, d2), (e0, e1, e2)):
            d_ref[...] = wg * da
            e_ref[...] = wg * csum

    row = pl.BlockSpec((tr, GROUP_W), lambda i: (i, 0))
    res = pl.pallas_call(
        body, name=name, grid=(T // tr,), in_specs=[row] * 7, out_specs=[row] * 6,
        out_shape=[_sds((T, GROUP_W), F32)] * 6,
        compiler_params=_params(1),
    )(*outs, *lses, dattn)
    return res[:3], res[3:]


def _rms_rows(xv, g):
    r = lax.rsqrt(jnp.mean(xv * xv, axis=-1, keepdims=True) + RMS_EPS)
    return (xv * r * g).astype(BF16)


def _epi_norm(val, x_refs, o_ref, y_refs, first_rows):
    o_ref[...] = val
    y_refs[0][...] = _rms_rows(val, x_refs[0][...])


def _epi_ple(val, x_refs, o_ref, y_refs, first_rows):
    o_ref[...] = val
    xn = x_refs[0][...] + x_refs[1][...] * _sigmoid(val)
    y_refs[0][...] = xn
    if len(y_refs) > 1:
        y_refs[1][...] = _rms_rows(xn, x_refs[2][...])


def _epi_norm_bwd(val, x_refs, o_ref, y_refs, first_rows):
    xv = x_refs[0][...]
    r = lax.rsqrt(jnp.mean(xv * xv, axis=-1, keepdims=True) + RMS_EPS)
    xh = xv * r
    part = _rows8(val * xh)

    @pl.when(first_rows)
    def _():
        y_refs[0][...] = part

    @pl.when(jnp.logical_not(first_rows))
    def _():
        y_refs[0][...] += part

    dxh = val * x_refs[2][...]
    o_ref[...] = x_refs[1][...] + r * (dxh - xh * jnp.mean(dxh * xh, axis=-1, keepdims=True))


def _local_step(x3, p4, tgt3, vecs, ex):
    Bn = x3.shape[0]
    T = Bn * SEQ
    x = x3.reshape(T, D)
    tgt = tgt3.reshape(T, D)
    pb = p4.astype(BF16).reshape(DEPTH, T, PLE)
    tabs = {}
    for d in DILATIONS:
        tabs[d] = [t.reshape(SEQ // d, d, HEAD).transpose(1, 0, 2).reshape(SEQ, HEAD) for t in _rope_tables()]
    tm = 1024 if T % 1024 == 0 else 512
    nt = T // tm
    tk = 1024 if T % 1024 == 0 else 512
    ntk = T // tk
    tm5 = 512
    f32o = lambda n: _sds((T, n), F32)

    def spec(shape, fn):
        return pl.BlockSpec(shape, fn)

    def _mm(name, *args, **kwargs):
        return _mm_call(name, *args, comm=ex.hook(name), **kwargs)

    def cols4(a_ref, b_ref):
        av = a_ref[...]
        return jnp.concatenate([_dot(av, b_ref[j]) for j in range(NCHIP)], axis=1)

    def rows4(a_ref, b_ref):
        av = a_ref[...]
        return jnp.concatenate([_dot(av, b_ref[:, j * 256:(j + 1) * 256], ta=True) for j in range(NCHIP)], axis=0)

    def kchunks4(a_ref, b_ref):
        total = _dot(a_ref[:, 0:256], b_ref[0], tb=True)
        for j in range(1, NCHIP):
            total = total + _dot(a_ref[:, j * 256:(j + 1) * 256], b_ref[j], tb=True)
        return total

    row5 = spec((tm5, D), lambda i, *_: (i, 0))
    rowm = spec((tm, D), lambda i, *_: (i, 0))
    gain = spec((1, D), lambda *_: (0, 0))
    bf_rows = (_sds((T, D), BF16), row5)

    saved = []
    h = _rms_fwd("rms_mix0", x, vecs["g_mix"][0:1])
    for l in range(DEPTH):
        L = str(l)
        G = ex.weights(l)
        g_mix, g_ffn, g_ple = (vecs[k][l:l + 1] for k in ("g_mix", "g_ffn", "g_ple"))
        pscale, cb, cw = vecs["pool_scale"][l:l + 1], vecs["conv_b"][l:l + 1], vecs["conv_w"][l]
        z = _mm("mm_z" + L, h, G["in"], grid=(nt, NCHIP),
                a_spec=spec((tm, D), lambda i, n: (i, 0)),
                b_spec=spec((None, D, IN_S), lambda i, n: (n, 0, 0)),
                o_spec=spec((tm, IN_S), lambda i, n: (i, n)), out_shape=f32o(ZW))
        z3 = z.reshape(Bn, SEQ, ZW)
        outs, lses = [], []
        for g, d in enumerate(DILATIONS):
            o_g, l_g = _attn_fwd("attn_fwd%d_%d" % (g, l), z3, tabs[d], g, d, comm=ex.hook("attn_fwd%d_%d" % (g, l)))
            outs.append(o_g.reshape(T, GROUP_W))
            lses.append(l_g.reshape(T, GROUP_W))
        attn = _merge_fwd("merge_fwd" + L, outs, lses)
        ya = _mm("mm_ya" + L, attn, G["sc"], grid=(nt,), compute=cols4,
                 a_spec=spec((tm, GROUP_W), lambda i: (i, 0)),
                 b_spec=spec((NCHIP, GROUP_W, 256), lambda i: (0, 0, 0)),
                 o_spec=spec((tm, D), lambda i: (i, 0)), out_shape=_sds((T, D), BF16))
        pooled3, ms3 = _pool_fwd("pool_fwd" + L, z3, G["sc"], pscale)
        ms = ms3.reshape(T, D)

        def row_sharded(name, a, rb, res=None, kdim=D, out=F32, **fused):
            if rb is None:
                b_arr, b_spec = G["dn"], spec((NCHIP, DN_S, D), lambda i: (0, 0, 0))
            else:
                b_arr, b_spec = G["r3"], spec((NCHIP, 256, D), lambda i: (0, rb, 0))
            return _mm(name, a, b_arr, grid=(T // tm5,),
                       a_spec=spec((tm5, kdim), lambda i: (i, 0)), b_spec=b_spec,
                       o_spec=row5, out_shape=_sds((T, D), out), res=res, res_spec=None if res is None else row5,
                       **fused)

        yb = row_sharded("mm_yb" + L, ms, 0, out=BF16)
        merged = _gate_fwd("gate_fwd" + L, z, ya, yb)
        x1, h2 = row_sharded("mm_o" + L, merged, 1, res=x, epilogue=_epi_norm, extra_in=[(g_ffn, gain)],
                             extra_out=[bf_rows])
        u = _mm("mm_up" + L, h2, G["up"], grid=(nt, NCHIP),
                a_spec=spec((tm, D), lambda i, n: (i, 0)),
                b_spec=spec((None, D, UP_S), lambda i, n: (n, 0, 0)),
                o_spec=spec((tm, UP_S), lambda i, n: (i, n)), out_shape=f32o(UW))
        u3 = u.reshape(Bn, SEQ, UW)
        act3, yg3, yv3 = _conv_fwd("conv_fwd" + L, u3, cw, cb, comm=ex.hook("conv_fwd" + L))
        act = act3.reshape(T, FF)
        x2, h3 = row_sharded("mm_down" + L, act, None, res=x1, kdim=FF, epilogue=_epi_norm,
                             extra_in=[(g_ple, gain)], extra_out=[bf_rows])
        pe = _mm("mm_pe" + L, pb[l], G["sc"], grid=(nt,), compute=cols4,
                 a_spec=spec((tm, PLE), lambda i: (i, 0)),
                 b_spec=spec((NCHIP, 256, 256), lambda i: (0, 2, 0)),
                 o_spec=spec((tm, D), lambda i: (i, 0)), out_shape=f32o(D))
        fused_in = [(x2, row5), (pe, row5)]
        fused_out = [(f32o(D), row5)]
        if l + 1 < DEPTH:
            fused_in.append((vecs["g_mix"][l + 1:l + 2], gain))
            fused_out.append(bf_rows)
        pg, x3n, *h_next = row_sharded("mm_pg" + L, h3, 2, epilogue=_epi_ple, extra_in=fused_in,
                                       extra_out=fused_out)
        saved.append(dict(x=x, h=h, z=z, outs=outs, lses=lses, attn=attn, ya=ya, yb=yb, pooled3=pooled3, ms=ms,
                          merged=merged, x1=x1, h2=h2, u3=u3, yg3=yg3, yv3=yv3, act=act, x2=x2, h3=h3, pg=pg, pe=pe))
        x = x3n
        h = h_next[0] if h_next else None

    dx, dg_final8, sq8 = _final_loss(x, vecs["g_final"].reshape(1, D), tgt)

    gg_shape = {k: _sds(G[k].shape, F32) for k in G}
    small = {"g_final": dg_final8}

    for l in reversed(range(DEPTH)):
        L = str(l)
        sv = saved[l]
        G = ex.weights(l)
        GG = dict.fromkeys(_KEYS)
        g_mix, g_ffn, g_ple = (vecs[k][l:l + 1] for k in ("g_mix", "g_ffn", "g_ple"))
        pscale, cb, cw = vecs["pool_scale"][l:l + 1], vecs["conv_b"][l:l + 1], vecs["conv_w"][l]

        def wgrad_rows(name, a, b_arr, key, rb):
            GG[key] = _mm(name, a, b_arr, grid=(2, ntk), ta=True, k_axis=1, nk=ntk, acc_shape=(D, 512),
                          a_spec=spec((tk, D), lambda n, k: (k, 0)),
                          b_spec=spec((tk, 512), lambda n, k: (k, n)),
                          o_spec=spec((NCHIP, 256, 512), lambda n, k: (0, rb, n)),
                          out_shape=gg_shape[key], buf=GG[key])

        def dgrad_rows(name, dy, rb, out=F32, **fused):
            return _mm(name, dy, G["r3"], grid=(T // tm5,), tb=True,
                       a_spec=spec((tm5, D), lambda i: (i, 0)),
                       b_spec=spec((NCHIP, 256, D), lambda i: (0, rb, 0)),
                       o_spec=row5, out_shape=_sds((T, D), out), **fused)

        def norm_bwd(xin, dres, g, rows=row5):
            return dict(epilogue=_epi_norm_bwd, extra_in=[(xin, rows), (dres, rows), (g, gain)],
                        extra_out=[(_sds((8, D), F32), spec((8, D), lambda *_: (0, 0)))])

        dpe, dpg = _ple_bwd("ple_bwd" + L, dx, sv["pe"], sv["pg"])
        GG["sc"] = _mm("wg_ple" + L, pb[l], dpe, grid=(ntk,), compute=rows4, k_axis=0, nk=ntk,
                       acc_shape=(NCHIP * PLE, 256),
                       a_spec=spec((tk, PLE), lambda k: (k, 0)), b_spec=spec((tk, D), lambda k: (k, 0)),
                       o_spec=spec((NCHIP, 256, 256), lambda k: (0, 2, 0)),
                       out_shape=gg_shape["sc"], buf=GG["sc"])
        wgrad_rows("wg_pg" + L, sv["h3"], dpg, "r3", 2)
        dx, small["g_ple" + L] = dgrad_rows("dg_pg" + L, dpg, 2, **norm_bwd(sv["x2"], dx, g_ple))

        da = _mm("dg_down" + L, dx, G["dn"], grid=(T // 256,), tb=True,
                 a_spec=spec((256, D), lambda i: (i, 0)),
                 b_spec=spec((NCHIP, DN_S, D), lambda i: (0, 0, 0)),
                 o_spec=spec((256, FF), lambda i: (i, 0)), out_shape=_sds((T, FF), BF16))
        GG["dn"] = _mm("wg_down" + L, sv["act"], dx, grid=(2, ntk), ta=True, k_axis=1, nk=ntk,
                       acc_shape=(FF, 512),
                       a_spec=spec((tk, FF), lambda n, k: (k, 0)), b_spec=spec((tk, 512), lambda n, k: (k, n)),
                       o_spec=spec((NCHIP, DN_S, 512), lambda n, k: (0, 0, n)),
                       out_shape=gg_shape["dn"], buf=GG["dn"])
        du3, dcw, dcb = _conv_bwd("conv_bwd" + L, da.reshape(Bn, SEQ, FF), sv["u3"], sv["yg3"], sv["yv3"], cw,
                                  comm=ex.hook("conv_bwd" + L))
        small["conv_w" + L], small["conv_b" + L] = dcw, dcb
        du = du3.reshape(T, UW)
        dx, small["g_ffn" + L] = _mm(
            "dg_up" + L, du, G["up"], grid=(nt, NCHIP), tb=True, k_axis=1, nk=NCHIP, acc_shape=(tm, D),
            a_spec=spec((tm, UP_S), lambda i, k: (i, k)), b_spec=spec((None, D, UP_S), lambda i, k: (k, 0, 0)),
            o_spec=rowm, out_shape=f32o(D), vmem=VMEM_BIG, **norm_bwd(sv["x1"], dx, g_ffn, rowm))
        GG["up"] = _mm("wg_up" + L, sv["h2"], du, grid=(NCHIP, ntk), ta=True, k_axis=1, nk=ntk,
                       acc_shape=(D, UP_S),
                       a_spec=spec((tk, D), lambda j, k: (k, 0)),
                       b_spec=spec((tk, UP_S), lambda j, k: (k, j)),
                       o_spec=spec((None, D, UP_S), lambda j, k: (j, 0, 0)),
                       out_shape=gg_shape["up"], buf=GG["up"])

        dmerged = dgrad_rows("dg_o" + L, dx, 1, out=BF16)
        wgrad_rows("wg_o" + L, sv["merged"], dx, "r3", 1)
        dya, dz = _gate_bwd("gate_bwd_a" + L, sv["z"], OFF_GA, sv["ya"], dmerged, None)
        dyb, dz = _gate_bwd("gate_bwd_b" + L, sv["z"], OFF_GB, sv["yb"], dmerged, dz)
        dms = dgrad_rows("dg_yb" + L, dyb, 0)
        wgrad_rows("wg_yb" + L, sv["ms"], dyb, "r3", 0)
        dz3, GG["sc"], small["pool_scale" + L] = _pool_bwd(
            "pool_bwd" + L, dms.reshape(Bn, SEQ, D), sv["pooled3"], G["sc"], pscale,
            dz.reshape(Bn, SEQ, ZW), GG["sc"])
        dattn = _mm("dg_ya" + L, dya, G["sc"], grid=(nt,), compute=kchunks4,
                    a_spec=spec((tm, D), lambda i: (i, 0)),
                    b_spec=spec((NCHIP, GROUP_W, 256), lambda i: (0, 0, 0)),
                    o_spec=spec((tm, GROUP_W), lambda i: (i, 0)), out_shape=f32o(GROUP_W))
        GG["sc"] = _mm("wg_ya" + L, sv["attn"], dya, grid=(ntk,), compute=rows4, k_axis=0, nk=ntk,
                       acc_shape=(NCHIP * GROUP_W, 256),
                       a_spec=spec((tk, GROUP_W), lambda k: (k, 0)), b_spec=spec((tk, D), lambda k: (k, 0)),
                       o_spec=spec((NCHIP, GROUP_W, 256), lambda k: (0, 0, 0)),
                       out_shape=gg_shape["sc"], buf=GG["sc"])
        ex.grads_ready(l, {k: GG[k] for k in _KEYS[1:]})
        dos, deltas = _merge_bwd("merge_bwd" + L, sv["outs"], sv["lses"], dattn)
        view3 = lambda t: t.reshape(Bn, SEQ, GROUP_W)
        sz3 = sv["z"].reshape(Bn, SEQ, ZW)
        for g, d in enumerate(DILATIONS):
            dz3 = _attn_bwd("attn_bwd%d_%d" % (g, l), sz3, tabs[d], g, d, view3(dos[g]), view3(sv["lses"][g]),
                            view3(deltas[g]), dz3, comm=ex.hook("attn_bwd%d_%d" % (g, l)))
        dz = dz3.reshape(T, ZW)
        GG["in"] = _mm("wg_z" + L, sv["h"], dz, grid=(NCHIP, ntk), ta=True, k_axis=1, nk=ntk,
                       acc_shape=(D, IN_S),
                       a_spec=spec((tk, D), lambda n, k: (k, 0)), b_spec=spec((tk, IN_S), lambda n, k: (k, n)),
                       o_spec=spec((None, D, IN_S), lambda n, k: (n, 0, 0)),
                       out_shape=gg_shape["in"], buf=GG["in"])
        ex.grads_ready(l, {"in": GG["in"]})

        def dgrad_z(name, first, count, dres, buf):
            rows = spec((tm, D), lambda i, *_: (i + first, 0))
            return _mm(name, dz, G["in"], grid=(count, NCHIP), tb=True, k_axis=1, nk=NCHIP, acc_shape=(tm, D),
                       a_spec=spec((tm, IN_S), lambda i, k: (i + first, k)),
                       b_spec=spec((None, D, IN_S), lambda i, k: (k, 0, 0)),
                       o_spec=rows, out_shape=f32o(D), vmem=VMEM_BIG, buf=buf, **norm_bwd(sv["x"], dres, g_mix, rows))

        if l == 0 and nt % 2 == 0:
            dx_a, part_a = dgrad_z("dg_z0a", 0, nt // 2, dx, None)
            dx, part_b = dgrad_z("dg_z0b", nt // 2, nt // 2, dx, dx_a)
            small["g_mix" + L] = part_a + part_b
        else:
            dx, small["g_mix" + L] = dgrad_z("dg_z" + L, 0, nt, dx, None)

    return sq8, dx.reshape(Bn, SEQ, D), small


_ANY = pl.BlockSpec(memory_space=pl.ANY)


def _place():
    x, y, c = lax.axis_index("x"), lax.axis_index("y"), lax.axis_index("c")
    chips = [(1 - x, y), (x, 1 - y), (1 - x, 1 - y)]
    return x, y, c, 2 * x + y, chips


def _half(rows, cc):
    return pl.ds(cc * (rows // 2), rows // 2)


def _remote(src, dst, send_sems, recv_sems, i, to):
    return pltpu.make_async_remote_copy(src_ref=src, dst_ref=dst, send_sem=send_sems.at[i], recv_sem=recv_sems.at[i],
                                        device_id=to, device_id_type=MESH)


def _exchange_gather_ici(stacks, done):
    n = len(stacks)
    rows = [t.shape[1] for t in stacks]

    def start(refs, send_sems, recv_sems):
        x, y, c, me, chips = _place()
        for k in range(n):
            part = refs[k].at[me, _half(rows[k], c)]
            for j, chip in enumerate(chips):
                _remote(part, part, send_sems, recv_sems, 3 * k + j, (*chip, c)).start()

    def wait(refs, send_sems, recv_sems):
        x, y, c, me, chips = _place()
        for k in range(n):
            for j, chip in enumerate(chips):
                part = refs[k].at[2 * chip[0] + chip[1], _half(rows[k], c)]
                _remote(part, part, send_sems, recv_sems, 3 * k + j, (*chip, c)).wait()

    return dict(arrays=list(stacks), nsem=3 * n, start=start, wait=wait, done=done)


def _exchange_gather_d2d(stacks, done):
    n = len(stacks)
    rows = [t.shape[1] for t in stacks]

    def copies(refs, send_sems, recv_sems, mine):
        x, y, c, me, chips = _place()
        cc = c if mine else 1 - c
        return [_remote(part, part, send_sems, recv_sems, 3 * k + j, (x, y, 1 - c))
                for k in range(n) for j, chip in enumerate(chips)
                for part in [refs[k].at[2 * chip[0] + chip[1], _half(rows[k], cc)]]]

    def start(refs, send_sems, recv_sems):
        for cp in copies(refs, send_sems, recv_sems, True):
            cp.start()

    def wait(refs, send_sems, recv_sems):
        for cp in copies(refs, send_sems, recv_sems, False):
            cp.wait()

    return dict(arrays=list(stacks), nsem=3 * n, start=start, wait=wait, done=done)


def _exchange_halves(g5, recv, done):
    n = len(g5)

    def copies(refs, send_sems, recv_sems):
        x, y, c, me, chips = _place()
        return [_remote(refs[k].at[:, 1 - c], refs[n + k], send_sems, recv_sems, k, (x, y, 1 - c)) for k in range(n)]

    def start(refs, send_sems, recv_sems):
        for cp in copies(refs, send_sems, recv_sems):
            cp.start()

    def wait(refs, send_sems, recv_sems):
        for cp in copies(refs, send_sems, recv_sems):
            cp.wait()

    return dict(arrays=list(g5) + list(recv), nsem=n, start=start, wait=wait, done=done)


def _exchange_chips(parts, landing, done):
    n = len(parts)

    def start(refs, send_sems, recv_sems):
        x, y, c, me, chips = _place()
        for k in range(n):
            for j, chip in enumerate(chips):
                _remote(refs[k].at[2 * chip[0] + chip[1]], refs[n + k].at[me], send_sems, recv_sems, 3 * k + j,
                        (*chip, c)).start()

    def wait(refs, send_sems, recv_sems):
        x, y, c, me, chips = _place()
        for k in range(n):
            for j, chip in enumerate(chips):
                slot = refs[n + k].at[2 * chip[0] + chip[1]]
                _remote(slot, slot, send_sems, recv_sems, 3 * k + j, (*chip, c)).wait()

    return dict(arrays=list(parts) + list(landing), nsem=3 * n, start=start, wait=wait, done=done)


def _exchange_share(full, layer, done):
    n = len(full)

    def copies(refs, send_sems, recv_sems, mine):
        x, y, c, me, chips = _place()
        cc = c if mine else 1 - c
        return [_remote(part, part, send_sems, recv_sems, k, (x, y, 1 - c))
                for k in range(n) for part in [refs[k].at[layer, cc]]]

    def start(refs, send_sems, recv_sems):
        for cp in copies(refs, send_sems, recv_sems, True):
            cp.start()

    def wait(refs, send_sems, recv_sems):
        for cp in copies(refs, send_sems, recv_sems, False):
            cp.wait()

    return dict(arrays=list(full), nsem=n, start=start, wait=wait, done=done)


def _exchange_call(name, comm):
    arrays = comm["arrays"]
    n = len(arrays)

    def body(*refs):
        outs, send_sems, recv_sems = refs[n:2 * n], refs[2 * n], refs[2 * n + 1]
        comm["start"](outs, send_sems, recv_sems)
        comm["wait"](outs, send_sems, recv_sems)

    outs = pl.pallas_call(
        body, name=name, in_specs=[_ANY] * n, out_specs=[_ANY] * n,
        out_shape=[_sds(t.shape, t.dtype) for t in arrays],
        scratch_shapes=[pltpu.SemaphoreType.DMA((comm["nsem"],))] * 2,
        input_output_aliases={i: i for i in range(n)},
    )(*arrays)
    comm["done"](outs)


def _gather_first(stacks, cw4):
    n = len(stacks)
    ici = _exchange_gather_ici(stacks, None)
    d2d = _exchange_gather_d2d(stacks, None)
    rows = [t.shape[1] for t in stacks]

    def body(*refs):
        g_refs, cwg_ref = refs[n + 1:2 * n + 1], refs[2 * n + 1]
        s_ici, r_ici, s_d2d, r_d2d, s_cw, r_cw = refs[2 * n + 2:]
        x, y, c, me, chips = _place()

        def cw_copy(j, slot, chip):
            part = cwg_ref.at[slot]
            return _remote(part, part, s_cw, r_cw, j, (*chip, c))

        ici["start"](g_refs, s_ici, r_ici)
        for j, chip in enumerate(chips):
            cw_copy(j, me, chip).start()
        for k in range(n):
            for j, chip in enumerate(chips):
                part = g_refs[k].at[2 * chip[0] + chip[1], _half(rows[k], c)]
                _remote(part, part, s_ici, r_ici, 3 * k + j, (*chip, c)).wait()
                _remote(part, part, s_d2d, r_d2d, 3 * k + j, (x, y, 1 - c)).start()
        d2d["wait"](g_refs, s_d2d, r_d2d)
        for j, chip in enumerate(chips):
            cw_copy(j, 2 * chip[0] + chip[1], chip).wait()

    outs = pl.pallas_call(
        body, name="gather_first", in_specs=[_ANY] * (n + 1), out_specs=[_ANY] * (n + 1),
        out_shape=[_sds(t.shape, t.dtype) for t in stacks] + [_sds(cw4.shape, cw4.dtype)],
        scratch_shapes=[pltpu.SemaphoreType.DMA((3 * n,))] * 4 + [pltpu.SemaphoreType.DMA((3,))] * 2,
        input_output_aliases={i: i for i in range(n + 1)},
    )(*stacks, cw4)
    return outs[:n], outs[n]


def _small_allreduce(small):
    def body(small_ref, red_ref, gath, s_send, s_recv):
        x, y, c, me, chips = _place()
        dev = 4 * x + 2 * y + c
        gath[dev] = small_ref[...]
        for r in range(1, 8):
            peer = (x ^ (r >> 2), y ^ ((r >> 1) & 1), c ^ (r & 1))
            _remote(small_ref, gath.at[dev], s_send, s_recv, r - 1, peer).start()
        for r in range(1, 8):
            peer = (x ^ (r >> 2), y ^ ((r >> 1) & 1), c ^ (r & 1))
            src = 4 * peer[0] + 2 * peer[1] + peer[2]
            _remote(small_ref, gath.at[src], s_send, s_recv, r - 1, peer).wait()
        total = gath[0]
        for i in range(1, 8):
            total = total + gath[i]
        red_ref[...] = total

    vm = pl.BlockSpec(memory_space=pltpu.VMEM)
    return pl.pallas_call(
        body, name="small_allreduce", in_specs=[vm], out_specs=vm, out_shape=_sds(small.shape, F32),
        scratch_shapes=[pltpu.VMEM((8,) + small.shape, F32), pltpu.SemaphoreType.DMA((7,)),
                        pltpu.SemaphoreType.DMA((7,))],
    )(small)


def _row_tile(rh):
    for cand in (512, 384, 352, 256, 128):
        if rh % cand == 0:
            return cand
    return rh


def _add_halves(name, g5, recv, place):
    _, _, rh, cols = g5.shape
    tr = _row_tile(rh)

    def body(place_ref, g_ref, r_ref, o_ref, own_ref):
        val = (g_ref[...] + r_ref[...]).astype(BF16)
        o_ref[...] = val

        @pl.when(pl.program_id(1) == place_ref[1])
        def _():
            own_ref[...] = val

    grid_spec = pltpu.PrefetchScalarGridSpec(
        num_scalar_prefetch=1, grid=(rh // tr, NCHIP),
        in_specs=[pl.BlockSpec((None, None, tr, cols), lambda i, j, pr: (j, pr[0], i, 0)),
                  pl.BlockSpec((None, tr, cols), lambda i, j, pr: (j, i, 0))],
        out_specs=[pl.BlockSpec((None, tr, cols), lambda i, j, pr: (j, i, 0)),
                   pl.BlockSpec((None, tr, cols), lambda i, j, pr: (pr[1], i, 0))])
    return pl.pallas_call(
        body, name=name, grid_spec=grid_spec, out_shape=[_sds(recv.shape, BF16)] * 2, compiler_params=_params(2),
    )(place, g5, recv)


def _sum_chips(name, landing, place, layer, full):
    _, rh, cols = landing.shape
    tr = _row_tile(rh)
    has_full = full is not None

    def body(*refs):
        r_ref, o_ref = refs[1], refs[-1]
        total = r_ref[0].astype(F32)
        for j in range(1, NCHIP):
            total = total + r_ref[j].astype(F32)
        o_ref[...] = total

    grid_spec = pltpu.PrefetchScalarGridSpec(
        num_scalar_prefetch=1, grid=(rh // tr,),
        in_specs=[pl.BlockSpec((NCHIP, tr, cols), lambda i, pr: (0, i, 0))] + ([_ANY] if has_full else []),
        out_specs=pl.BlockSpec((None, None, tr, cols), lambda i, pr: (layer, pr[0], i, 0)))
    return pl.pallas_call(
        body, name=name, grid_spec=grid_spec, out_shape=_sds((DEPTH, 2, rh, cols), F32),
        input_output_aliases={2: 0} if has_full else {}, compiler_params=_params(1),
    )(place, landing, *([full] if has_full else []))


class _Schedule:
    FIRST, REST = ["in"], list(_KEYS[1:])

    def __init__(self, slotted, cw4, place):
        self.place = place
        self._w = [dict(zip(_KEYS, layer)) for layer in slotted]
        got, self.cw4 = _gather_first([self._w[0][k] for k in self.FIRST], cw4)
        self._w[0].update(zip(self.FIRST, got))
        self._g5, self._recv, self._parts, self._landing = [{}, {}], [{}, {}], [{}, {}], [{}, {}]
        self.full = {}
        every = list(_KEYS)
        self._hooks = {
            "mm_z0": lambda: self._gather(_exchange_gather_ici, 0, self.REST),
            "attn_fwd0_0": lambda: self._gather(_exchange_gather_d2d, 0, self.REST),
            "mm_up0": lambda: self._gather(_exchange_gather_ici, 1, self.REST),
            "conv_fwd0": lambda: self._gather(_exchange_gather_ici, 1, self.FIRST),
            "mm_down0": lambda: self._gather(_exchange_gather_d2d, 1, every),
            "dg_down0": lambda: self._halves(1, every),
            "conv_bwd0": lambda: self._chips(1, every),
            "dg_up0": lambda: self._share(1, every),
            "attn_bwd0_0": lambda: self._halves(0, self.REST),
            "wg_z0": lambda: self._chips(0, self.REST),
            "dg_z0a": lambda: self._halves(0, self.FIRST),
            "dg_z0b": lambda: self._chips(0, self.FIRST),
        }

    def weights(self, layer):
        return self._w[layer]

    def hook(self, name):
        make = self._hooks.get(name)
        return make() if make else None

    def grads_ready(self, layer, GG):
        for k, t in GG.items():
            g5 = t.reshape(NCHIP, 2, t.shape[1] // 2, t.shape[2])
            self._g5[layer][k] = g5
            self._recv[layer][k] = lax.empty((NCHIP,) + g5.shape[2:], F32)

    def _gather(self, make, layer, keys):
        def done(arrays):
            self._w[layer].update(zip(keys, arrays))
        return make([self._w[layer][k] for k in keys], done)

    def _halves(self, layer, keys):
        return _exchange_halves([self._g5[layer][k] for k in keys], [self._recv[layer][k] for k in keys],
                                lambda arrays: self._halves_done(layer, keys, arrays))

    def _halves_done(self, layer, keys, arrays):
        n = len(keys)
        for k, g, r in zip(keys, arrays[:n], arrays[n:]):
            self._parts[layer][k], self._landing[layer][k] = _add_halves(
                "add_halves%d_%s" % (layer, k), g, r, self.place)

    def _chips(self, layer, keys):
        return _exchange_chips([self._parts[layer][k] for k in keys], [self._landing[layer][k] for k in keys],
                               lambda arrays: self._chips_done(layer, keys, arrays))

    def _chips_done(self, layer, keys, arrays):
        for k, t in zip(keys, arrays[len(keys):]):
            self.full[k] = _sum_chips("sum_chips%d_%s" % (layer, k), t, self.place, layer, self.full.get(k))

    def _share(self, layer, keys):
        def done(arrays):
            self.full.update(zip(keys, arrays))
        return _exchange_share([self.full[k] for k in keys], layer, done)

    def finish(self, small):
        small_red = _small_allreduce(small)
        _exchange_call("share_halves_last", self._share(0, list(_KEYS)))
        return [self.full[k] for k in _KEYS], small_red


def _adamw(name, w, g, m, v):
    shape = w.shape
    cols = shape[-1]
    rows = 1
    for s in shape[:-1]:
        rows *= s
    tr = rows
    for cand in (256, 128, 64):
        if rows > cand and rows % cand == 0:
            tr = cand
            break
    c1 = 1.0 / (1.0 - B1 ** STEP)
    c2 = 1.0 / (1.0 - B2 ** STEP)

    def body(w_ref, g_ref, m_ref, v_ref, go_ref, d_ref, nm_ref, nv_ref):
        gv = g_ref[...]
        go_ref[...] = gv
        nm = B1 * m_ref[...] + (1.0 - B1) * gv
        nv = B2 * v_ref[...] + (1.0 - B2) * (gv * gv)
        nm_ref[...] = nm
        nv_ref[...] = nv
        d_ref[...] = -LR * ((nm * c1) / (jnp.sqrt(nv * c2) + ADAM_EPS) + WD * w_ref[...])

    blk = pl.BlockSpec((tr, cols), lambda i: (i, 0))
    outs = pl.pallas_call(
        body, name=name, grid=(rows // tr,), in_specs=[blk] * 4, out_specs=[blk] * 4,
        out_shape=[_sds((rows, cols), F32)] * 4, compiler_params=_params(1),
    )(*(t.reshape(rows, cols) for t in (w, g, m, v)))
    return tuple(o.reshape(shape) for o in outs)


def _pack_small(small):
    rows = [jnp.sum(small["g_mix%d" % l], axis=0, keepdims=True) for l in range(DEPTH)]
    rows += [jnp.sum(small["pool_scale%d" % l], axis=0, keepdims=True) for l in range(DEPTH)]
    rows += [jnp.sum(small["g_ffn%d" % l], axis=0, keepdims=True) for l in range(DEPTH)]
    rows += [jnp.sum(small["g_ple%d" % l], axis=0, keepdims=True) for l in range(DEPTH)]
    rows += [jnp.sum(small["g_final"], axis=0, keepdims=True)]
    flat = [small["conv_b%d" % l].reshape(-1) for l in range(DEPTH)]
    flat += [small["conv_w%d" % l].reshape(-1) for l in range(DEPTH)]
    flat = jnp.concatenate(flat).reshape(-1, D)
    packed = jnp.concatenate(rows + [flat], axis=0)
    return jnp.pad(packed, ((0, SMALL_ROWS - packed.shape[0]), (0, 0)))


def _unpack_small(red):
    g_mix, pool_scale, g_ffn, g_ple = red[0:2], red[2:4], red[4:6], red[6:8]
    g_final = red[8]
    nb = DEPTH * UW // D
    conv_b = red[9:9 + nb].reshape(DEPTH, UW)
    conv_w = red[9 + nb:9 + 4 * nb].reshape(DEPTH, 3, UW)
    return g_mix, pool_scale, g_ffn, g_ple, g_final, conv_b, conv_w


def kernel(x, p, g_mix, w_in, w_ya, w_yb, pool_w, pool_scale, w_o, g_ffn, w_up, conv_w, conv_b, w_down, g_ple, w_ple, w_ple_gate, g_final, loss_target, m_g_mix, m_w_in, m_w_ya, m_w_yb, m_pool_w, m_pool_scale, m_w_o, m_g_ffn, m_w_up, m_conv_w, m_conv_b, m_w_down, m_g_ple, m_w_ple, m_w_ple_gate, m_g_final, v_g_mix, v_w_in, v_w_ya, v_w_yb, v_pool_w, v_pool_scale, v_w_o, v_g_ffn, v_w_up, v_conv_w, v_conv_b, v_w_down, v_g_ple, v_w_ple, v_w_ple_gate, v_g_final):
    me = 2 * lax.axis_index("x") + lax.axis_index("y")
    place = jnp.stack([lax.axis_index("c"), me]).astype(jnp.int32)

    def slot(shard):
        return lax.dynamic_update_index_in_dim(lax.empty((NCHIP,) + shard.shape, shard.dtype), shard, me, 0)

    packed = [
        w_in.astype(BF16), w_up.astype(BF16),
        jnp.concatenate([w_ya, w_ple, pool_w.reshape(DEPTH, 256, 256)], axis=1).astype(BF16),
        jnp.concatenate([w_yb, w_o, w_ple_gate], axis=1).astype(BF16),
        w_down.astype(BF16),
    ]
    slotted = [[slot(t[l]) for t in packed] for l in range(DEPTH)]
    ex = _Schedule(slotted, slot(conv_w.reshape(DEPTH * 3, UP_S)), place)
    cw_full = ex.cw4.reshape(NCHIP, DEPTH, 3, UP_S).transpose(1, 2, 0, 3).reshape(DEPTH, 3, UW)

    vecs = dict(g_mix=g_mix, pool_scale=pool_scale, g_ffn=g_ffn, g_ple=g_ple, g_final=g_final, conv_b=conv_b,
                conv_w=cw_full)
    sq8, grad_x, small = _local_step(x, p, loss_target, vecs, ex)
    loss = lax.psum(jnp.sum(sq8) * (0.5 / D), ("x", "y", "c"))

    full, small_red = ex.finish(_pack_small(small))
    r_in, r_up, r_sc, r_r3, r_dn = [f.reshape(DEPTH, -1, f.shape[-1]) for f in full]
    d_g_mix, d_pool_scale, d_g_ffn, d_g_ple, d_g_final, d_conv_b, d_conv_w_full = _unpack_small(small_red)
    d_conv_w = lax.dynamic_slice_in_dim(d_conv_w_full, me * UP_S, UP_S, axis=2)

    grads = dict(
        g_mix=d_g_mix, w_in=r_in, w_ya=r_sc[:, 0:512], w_yb=r_r3[:, 0:256],
        pool_w=r_sc[:, 768:1024].reshape(DEPTH, 4, 64, 256), pool_scale=d_pool_scale, w_o=r_r3[:, 256:512],
        g_ffn=d_g_ffn, w_up=r_up, conv_w=d_conv_w, conv_b=d_conv_b, w_down=r_dn, g_ple=d_g_ple,
        w_ple=r_sc[:, 512:768], w_ple_gate=r_r3[:, 512:768], g_final=d_g_final)
    weights = dict(g_mix=g_mix, w_in=w_in, w_ya=w_ya, w_yb=w_yb, pool_w=pool_w, pool_scale=pool_scale, w_o=w_o,
                   g_ffn=g_ffn, w_up=w_up, conv_w=conv_w, conv_b=conv_b, w_down=w_down, g_ple=g_ple, w_ple=w_ple,
                   w_ple_gate=w_ple_gate, g_final=g_final)
    m_in = dict(g_mix=m_g_mix, w_in=m_w_in, w_ya=m_w_ya, w_yb=m_w_yb, pool_w=m_pool_w, pool_scale=m_pool_scale,
                w_o=m_w_o, g_ffn=m_g_ffn, w_up=m_w_up, conv_w=m_conv_w, conv_b=m_conv_b, w_down=m_w_down,
                g_ple=m_g_ple, w_ple=m_w_ple, w_ple_gate=m_w_ple_gate, g_final=m_g_final)
    v_in = dict(g_mix=v_g_mix, w_in=v_w_in, w_ya=v_w_ya, w_yb=v_w_yb, pool_w=v_pool_w, pool_scale=v_pool_scale,
                w_o=v_w_o, g_ffn=v_g_ffn, w_up=v_w_up, conv_w=v_conv_w, conv_b=v_conv_b, w_down=v_w_down,
                g_ple=v_g_ple, w_ple=v_w_ple, w_ple_gate=v_w_ple_gate, g_final=v_g_final)
    names = ["g_mix", "w_in", "w_ya", "w_yb", "pool_w", "pool_scale", "w_o", "g_ffn", "w_up", "conv_w", "conv_b",
             "w_down", "g_ple", "w_ple", "w_ple_gate", "g_final"]
    deltas, new_m, new_v = [], [], []
    for nme in names:
        gr = grads[nme].reshape(weights[nme].shape)
        grads[nme], dlt, nm, nv = _adamw("adamw_" + nme, weights[nme], gr, m_in[nme], v_in[nme])
        deltas.append(dlt)
        new_m.append(nm)
        new_v.append(nv)
    return (loss, grad_x, *[grads[nme] for nme in names], *deltas, *new_m, *new_v)
```

```python
import math

import jax
import jax.numpy as jnp
from jax import lax
from jax.experimental import pallas as pl
from jax.experimental.pallas import tpu as pltpu

F32 = jnp.float32
BF16 = jnp.bfloat16
_KEYS = ("in", "up", "sc", "r3", "dn")
MESH = pl.DeviceIdType.MESH

D = 1024
SEQ = 2048
DEPTH = 2
HEAD = 128
GROUP_W = 512
DILATIONS = (1, 4, 16)
ROPE_DIM = 32
ROPE_THETA = 500000.0
NEG_INF = -1e30
ZW = 7680
OFF_K, OFF_V, OFF_U, OFF_GA, OFF_GB = 1536, 3072, 4608, 5632, 6656
FF = 2816
UW = 2 * FF
PLE = 256
NCHIP = 4
IN_S, UP_S, DN_S = ZW // NCHIP, UW // NCHIP, FF // NCHIP
RMS_EPS = 1e-6
LR, B1, B2, ADAM_EPS, WD, STEP = 0.001, 0.9, 0.999, 1e-08, 0.01, 10
SMALL_ROWS = 56
VMEM_CAP = 48 * 1024 * 1024
VMEM_BIG = 58 * 1024 * 1024


def _params(n_grid, vmem=VMEM_CAP):
    return pltpu.CompilerParams(dimension_semantics=("arbitrary",) * n_grid, vmem_limit_bytes=vmem)


def _sigmoid(v):
    return 1.0 / (1.0 + jnp.exp(-v))


def _rows8(v):
    return jnp.sum(v.reshape(v.shape[0] // 8, 8, v.shape[1]), axis=0)


def _sds(shape, dtype):
    return jax.ShapeDtypeStruct(shape, dtype)


def _dot(av, bv, ta=False, tb=False):
    dims = (((0,) if ta else (1,), (1,) if tb else (0,)), ((), ()))
    return lax.dot_general(av.astype(BF16), bv.astype(BF16), dims, preferred_element_type=F32)


def _call(body, *, name, grid, in_specs, out_specs, out_shape, scratch_shapes=(), aliases=None, comm=None,
          vmem=VMEM_CAP):
    params = _params(len(grid), vmem)
    aliases = dict(aliases or {})
    if comm is None:
        return pl.pallas_call(body, name=name, grid=grid, in_specs=list(in_specs), out_specs=out_specs,
                              out_shape=out_shape, scratch_shapes=list(scratch_shapes),
                              input_output_aliases=aliases, compiler_params=params)
    single = not isinstance(out_shape, (list, tuple))
    out_specs_l = [out_specs] if single else list(out_specs)
    out_shape_l = [out_shape] if single else list(out_shape)
    n_in, n_out, n_c = len(in_specs), len(out_shape_l), len(comm["arrays"])

    def hosted(*refs):
        core_in, core_out = refs[:n_in], refs[n_in + n_c:n_in + n_c + n_out]
        c_refs = refs[n_in + n_c + n_out:n_in + 2 * n_c + n_out]
        scratch, (send_sems, recv_sems) = refs[n_in + 2 * n_c + n_out:-2], refs[-2:]
        ids = [pl.program_id(i) for i in range(len(grid))]
        first, last = ids[0] == 0, ids[0] == grid[0] - 1
        for i in range(1, len(grid)):
            first = jnp.logical_and(first, ids[i] == 0)
            last = jnp.logical_and(last, ids[i] == grid[i] - 1)

        @pl.when(first)
        def _():
            comm["start"](c_refs, send_sems, recv_sems)

        body(*core_in, *core_out, *scratch)

        @pl.when(last)
        def _():
            comm["wait"](c_refs, send_sems, recv_sems)

    any_spec = pl.BlockSpec(memory_space=pl.ANY)
    for i in range(n_c):
        aliases[n_in + i] = n_out + i
    call = pl.pallas_call(
        hosted, name=name, grid=grid, in_specs=list(in_specs) + [any_spec] * n_c,
        out_specs=out_specs_l + [any_spec] * n_c,
        out_shape=out_shape_l + [_sds(t.shape, t.dtype) for t in comm["arrays"]],
        scratch_shapes=list(scratch_shapes) + [pltpu.SemaphoreType.DMA((comm["nsem"],))] * 2,
        input_output_aliases=aliases, compiler_params=params)

    def run(*args):
        outs = call(*args, *comm["arrays"])
        comm["done"](outs[n_out:])
        return outs[0] if single else outs[:n_out]

    return run


def _mm_call(name, a, b, *, grid, a_spec, b_spec, o_spec, out_shape, ta=False, tb=False, k_axis=None, nk=1,
             acc_shape=None, res=None, res_spec=None, buf=None, compute=None, comm=None,
             extra_in=(), extra_out=(), epilogue=None, vmem=VMEM_CAP):
    has_res, has_buf = res is not None, buf is not None
    in_place = nk > 1 and not has_res and out_shape.dtype == F32 and epilogue is None
    n_xi, n_xo = len(extra_in), len(extra_out)

    def body(*refs):
        a_ref, b_ref = refs[0], refs[1]
        pos = 2
        r_ref = None
        if has_res:
            r_ref = refs[pos]
            pos += 1
        if has_buf:
            pos += 1
        x_refs = refs[pos:pos + n_xi]
        pos += n_xi
        first_rows = pl.program_id(0) == 0
        o_ref = refs[pos]
        y_refs = refs[pos + 1:pos + 1 + n_xo]
        if compute is None:
            av = a_ref[...]
            bv = b_ref[...]
            part = _dot(av.reshape(-1, av.shape[-1]), bv.reshape(-1, bv.shape[-1]), ta, tb)
        else:
            part = compute(a_ref, b_ref)

        def finish(val):
            if r_ref is not None:
                val = val + r_ref[...]
            if epilogue is not None:
                epilogue(val, x_refs, o_ref, y_refs, first_rows)
            else:
                o_ref[...] = val.reshape(o_ref.shape).astype(o_ref.dtype)

        if nk == 1:
            finish(part)
        elif in_place:
            @pl.when(pl.program_id(k_axis) == 0)
            def _():
                o_ref[...] = jnp.zeros(o_ref.shape, F32)

            o_ref[...] += part.reshape(o_ref.shape)
        else:
            acc_ref = refs[pos + 1 + n_xo]
            k = pl.program_id(k_axis)

            @pl.when(k == 0)
            def _():
                acc_ref[...] = jnp.zeros(acc_ref.shape, F32)

            acc_ref[...] += part

            @pl.when(k == nk - 1)
            def _():
                finish(acc_ref[...])

    ins, in_specs = [a, b], [a_spec, b_spec]
    if has_res:
        ins.append(res)
        in_specs.append(res_spec)
    aliases = {}
    if has_buf:
        aliases = {len(ins): 0}
        ins.append(buf)
        in_specs.append(pl.BlockSpec(memory_space=pl.ANY))
    for arr, sp in extra_in:
        ins.append(arr)
        in_specs.append(sp)
    scratch = [pltpu.VMEM(acc_shape, F32)] if nk > 1 and not in_place else []
    if not extra_out:
        return _call(body, name=name, grid=grid, in_specs=in_specs, out_specs=o_spec, out_shape=out_shape,
                     scratch_shapes=scratch, aliases=aliases, comm=comm, vmem=vmem)(*ins)
    return _call(body, name=name, grid=grid, in_specs=in_specs, out_specs=[o_spec] + [sp for _, sp in extra_out],
                 out_shape=[out_shape] + [sh for sh, _ in extra_out], scratch_shapes=scratch, aliases=aliases,
                 comm=comm, vmem=vmem)(*ins)


def _rms_fwd(name, x, g, tr=512):
    T = x.shape[0]

    def body(x_ref, g_ref, h_ref):
        xv = x_ref[...]
        r = lax.rsqrt(jnp.mean(xv * xv, axis=-1, keepdims=True) + RMS_EPS)
        h_ref[...] = (xv * r * g_ref[...]).astype(BF16)

    return pl.pallas_call(
        body, name=name, grid=(T // tr,),
        in_specs=[pl.BlockSpec((tr, D), lambda i: (i, 0)), pl.BlockSpec((1, D), lambda i: (0, 0))],
        out_specs=pl.BlockSpec((tr, D), lambda i: (i, 0)), out_shape=_sds((T, D), BF16),
        compiler_params=_params(1),
    )(x, g)


def _final_loss(x, g, tgt, tr=512):
    T = x.shape[0]

    def body(x_ref, g_ref, t_ref, dx_ref, dg_ref, sq_ref):
        xv = x_ref[...]
        r = lax.rsqrt(jnp.mean(xv * xv, axis=-1, keepdims=True) + RMS_EPS)
        xh = xv * r
        gv = g_ref[...]
        e = xh * gv - t_ref[...]
        dy = e * (1.0 / D)
        pg = _rows8(dy * xh)
        ps = _rows8(e * e)

        @pl.when(pl.program_id(0) == 0)
        def _():
            dg_ref[...] = pg
            sq_ref[...] = ps

        @pl.when(pl.program_id(0) > 0)
        def _():
            dg_ref[...] += pg
            sq_ref[...] += ps

        dxh = dy * gv
        dx_ref[...] = r * (dxh - xh * jnp.mean(dxh * xh, axis=-1, keepdims=True))

    row = pl.BlockSpec((tr, D), lambda i: (i, 0))
    acc = pl.BlockSpec((8, D), lambda i: (0, 0))
    return pl.pallas_call(
        body, name="final_loss", grid=(T // tr,),
        in_specs=[row, pl.BlockSpec((1, D), lambda i: (0, 0)), row],
        out_specs=[row, acc, acc],
        out_shape=[_sds((T, D), F32), _sds((8, D), F32), _sds((8, D), F32)],
        compiler_params=_params(1),
    )(x, g, tgt)


def _ple_bwd(name, dx, pe, pg, tr=512):
    T = dx.shape[0]

    def body(dx_ref, pe_ref, pg_ref, dpe_ref, dpg_ref):
        s = _sigmoid(pg_ref[...])
        dxv = dx_ref[...]
        dpe_ref[...] = (dxv * s).astype(BF16)
        dpg_ref[...] = (dxv * pe_ref[...].astype(F32) * s * (1.0 - s)).astype(BF16)

    row = pl.BlockSpec((tr, D), lambda i: (i, 0))
    return pl.pallas_call(
        body, name=name, grid=(T // tr,), in_specs=[row, row, row], out_specs=[row, row],
        out_shape=[_sds((T, D), BF16), _sds((T, D), BF16)], compiler_params=_params(1),
    )(dx, pe, pg)


def _gate_fwd(name, z, ya, yb, tr=512):
    T = z.shape[0]
    w = 512

    def body(ga_ref, gb_ref, ya_ref, yb_ref, o_ref):
        o_ref[...] = (_sigmoid(ga_ref[...]) * ya_ref[...].astype(F32)
                      + _sigmoid(gb_ref[...]) * yb_ref[...].astype(F32)).astype(BF16)

    col = pl.BlockSpec((tr, w), lambda i, j: (i, j))
    return pl.pallas_call(
        body, name=name, grid=(T // tr, D // w),
        in_specs=[pl.BlockSpec((tr, w), lambda i, j: (i, OFF_GA // w + j)),
                  pl.BlockSpec((tr, w), lambda i, j: (i, OFF_GB // w + j)), col, col],
        out_specs=col, out_shape=_sds((T, D), BF16), compiler_params=_params(2),
    )(z, z, ya, yb)


def _gate_bwd(name, z, off, y, dm, dz, tr=512):
    T = z.shape[0]
    w = 512
    has_dz = dz is not None

    def body(*refs):
        g_ref, y_ref, dm_ref = refs[:3]
        dy_ref, dz_ref = refs[-2:]
        s = _sigmoid(g_ref[...])
        dmv = dm_ref[...].astype(F32)
        dy_ref[...] = (dmv * s).astype(BF16)
        dz_ref[...] = (dmv * y_ref[...].astype(F32) * s * (1.0 - s)).astype(BF16)

    col = pl.BlockSpec((tr, w), lambda i, j: (i, j))
    gcol = pl.BlockSpec((tr, w), lambda i, j: (i, off // w + j))
    ins, in_specs, aliases = [z, y, dm], [gcol, col, col], {}
    if has_dz:
        ins.append(dz)
        in_specs.append(pl.BlockSpec(memory_space=pl.ANY))
        aliases = {3: 1}
    return pl.pallas_call(
        body, name=name, grid=(T // tr, D // w), in_specs=in_specs, out_specs=[col, gcol],
        out_shape=[_sds((T, D), BF16), _sds((T, ZW), BF16)], input_output_aliases=aliases,
        compiler_params=_params(2),
    )(*ins)


def _shift_down(v, k, rows):
    return jnp.where(rows >= k, pltpu.roll(v, k, 0), 0.0)


def _shift_up(v, k, rows):
    n = v.shape[0]
    return jnp.where(rows < n - k, pltpu.roll(v, n - k, 0), 0.0)


def _pool_window(v, g, rows, shift):
    s2 = v + shift(v, 1, rows)
    s4 = s2 + shift(s2, 2, rows)
    s8 = s4 + shift(s4, 4, rows)
    s16 = s8 + shift(s8, 8, rows)
    return jnp.where(g == 0, s2, jnp.where(g == 1, s4, jnp.where(g == 2, s8, s16)))


def _pool_count(g, rows):
    wlen = jnp.left_shift(2, g).astype(F32)
    return jnp.minimum(rows.astype(F32) + 1.0, wlen)


def _pool_fwd(name, z3, g_sc, scale):
    Bn = z3.shape[0]
    gw = 256

    def body(u_ref, pw_ref, sc_ref, pooled_ref, ms_ref):
        g = pl.program_id(1)
        u = u_ref[...]
        rows = lax.broadcasted_iota(jnp.int32, u.shape, 0)
        pooled = (_pool_window(u, g, rows, _shift_down) / _pool_count(g, rows) - u).astype(BF16)
        pooled_ref[...] = pooled
        pw = pw_ref[...].reshape(gw, gw)
        mixed = jnp.dot(pooled, pw, preferred_element_type=F32)
        ms_ref[...] = (mixed * sc_ref[...]).astype(BF16)

    blk = pl.BlockSpec((None, SEQ, gw), lambda b, g: (b, 0, g))
    return pl.pallas_call(
        body, name=name, grid=(Bn, 4),
        in_specs=[pl.BlockSpec((None, SEQ, gw), lambda b, g: (b, 0, OFF_U // gw + g)),
                  pl.BlockSpec((NCHIP, 64, gw), lambda b, g: (0, 12 + g, 0)),
                  pl.BlockSpec((1, gw), lambda b, g: (0, g))],
        out_specs=[blk, blk],
        out_shape=[_sds((Bn, SEQ, D), BF16), _sds((Bn, SEQ, D), BF16)],
        compiler_params=_params(2),
    )(z3, g_sc, scale)


def _pool_bwd(name, dms3, pooled3, g_sc, scale, dz3, gg_sc):
    Bn = dms3.shape[0]
    gw = 256
    has_gg = gg_sc is not None

    def body(*refs):
        dms_ref, pooled_ref, pw_ref, sc_ref = refs[:4]
        dz_ref, dpw_ref, dsc_ref = refs[-3:]
        g, b = pl.program_id(0), pl.program_id(1)
        pooled = pooled_ref[...]
        pw = pw_ref[...].reshape(gw, gw)
        dms = dms_ref[...]
        mixed = jnp.dot(pooled, pw, preferred_element_type=F32)
        psc = _rows8(dms * mixed)
        dmixed = (dms * sc_ref[...]).astype(BF16)
        dpw = lax.dot_general(pooled, dmixed, (((0,), (0,)), ((), ())), preferred_element_type=F32)
        dpw = dpw.reshape(NCHIP, 64, gw)

        @pl.when(b == 0)
        def _():
            dsc_ref[...] = psc
            dpw_ref[...] = dpw

        @pl.when(b > 0)
        def _():
            dsc_ref[...] += psc
            dpw_ref[...] += dpw

        dpooled = lax.dot_general(dmixed, pw, (((1,), (1,)), ((), ())), preferred_element_type=F32)
        rows = lax.broadcasted_iota(jnp.int32, dpooled.shape, 0)
        dq = dpooled / _pool_count(g, rows)
        dz_ref[...] = (_pool_window(dq, g, rows, _shift_up) - dpooled).astype(BF16)

    ins = [dms3, pooled3, g_sc, scale, dz3]
    in_specs = [pl.BlockSpec((None, SEQ, gw), lambda g, b: (b, 0, g)),
                pl.BlockSpec((None, SEQ, gw), lambda g, b: (b, 0, g)),
                pl.BlockSpec((NCHIP, 64, gw), lambda g, b: (0, 12 + g, 0)),
                pl.BlockSpec((1, gw), lambda g, b: (0, g)),
                pl.BlockSpec(memory_space=pl.ANY)]
    aliases = {4: 0}
    if has_gg:
        ins.append(gg_sc)
        in_specs.append(pl.BlockSpec(memory_space=pl.ANY))
        aliases[5] = 1
    return pl.pallas_call(
        body, name=name, grid=(4, Bn), in_specs=in_specs,
        out_specs=[pl.BlockSpec((None, SEQ, gw), lambda g, b: (b, 0, OFF_U // gw + g)),
                   pl.BlockSpec((NCHIP, 64, gw), lambda g, b: (0, 12 + g, 0)),
                   pl.BlockSpec((8, gw), lambda g, b: (0, g))],
        out_shape=[_sds(dz3.shape, BF16), _sds((NCHIP, D, 256), F32), _sds((8, D), F32)],
        input_output_aliases=aliases, compiler_params=_params(2),
    )(*ins)


CT = 256
NCT = FF // CT


def _conv_pre(u, cw_ref, cb_ref, rows):
    return (cb_ref[...] + cw_ref[0:1, :] * _shift_down(u, 2, rows) + cw_ref[1:2, :] * _shift_down(u, 1, rows)
            + cw_ref[2:3, :] * u)


def _conv_fwd(name, u3, cw, cb, comm=None):
    Bn = u3.shape[0]

    def body(ug_ref, uv_ref, cwg_ref, cwv_ref, cbg_ref, cbv_ref, a_ref, yg_ref, yv_ref):
        ug, uv = ug_ref[...].astype(F32), uv_ref[...].astype(F32)
        rows = lax.broadcasted_iota(jnp.int32, ug.shape, 0)
        yg = _conv_pre(ug, cwg_ref, cbg_ref, rows)
        yv = _conv_pre(uv, cwv_ref, cbv_ref, rows)
        yg_ref[...] = yg.astype(BF16)
        yv_ref[...] = yv.astype(BF16)
        a_ref[...] = (yg * _sigmoid(yg) * yv).astype(BF16)

    def blk(off):
        return pl.BlockSpec((None, SEQ, CT), lambda b, c: (b, 0, off + c))

    return _call(
        body, name=name, grid=(Bn, NCT),
        in_specs=[blk(0), blk(NCT),
                  pl.BlockSpec((3, CT), lambda b, c: (0, c)), pl.BlockSpec((3, CT), lambda b, c: (0, NCT + c)),
                  pl.BlockSpec((1, CT), lambda b, c: (0, c)), pl.BlockSpec((1, CT), lambda b, c: (0, NCT + c))],
        out_specs=[blk(0)] * 3,
        out_shape=[_sds((Bn, SEQ, FF), BF16)] * 3, comm=comm,
    )(u3, u3, cw, cw, cb, cb)


def _conv_bwd(name, da3, u3, yg3, yv3, cw, comm=None):
    Bn = u3.shape[0]
    last = NCT * Bn - 1
    R = 128

    def body(da_ref, ug_ref, uv_ref, yg_ref, yv_ref, cwg_ref, cwv_ref,
             du_ref, dcwg_ref, dcwv_ref, dcbg_ref, dcbv_ref, stage_g, stage_v, sems):
        c, b = pl.program_id(0), pl.program_id(1)
        step = c * Bn + b

        def writes(off_c, stage, sem):
            col = pl.multiple_of(off_c + c * CT, CT)
            return pltpu.make_async_copy(stage, du_ref.at[b, :, pl.ds(col, CT)], sem)

        @pl.when(step > 0)
        def _():
            writes(0, stage_g, sems.at[0]).wait()
            writes(FF, stage_v, sems.at[1]).wait()

        cwg, cwv = cwg_ref[...], cwv_ref[...]
        tail_rows = lax.broadcasted_iota(jnp.int32, (R, CT), 0)

        def chunk(c0, acc, at_end):
            n = R if at_end else R + 16
            yg, yv = yg_ref[pl.ds(c0, n), :].astype(F32), yv_ref[pl.ds(c0, n), :].astype(F32)
            da = da_ref[pl.ds(c0, n), :].astype(F32)
            s = _sigmoid(yg)
            dyv = da * (yg * s)
            dyg = da * yv * (s * (1.0 + yg * (1.0 - s)))
            out = []
            for dy, u_ref, cwt, stage in ((dyg, ug_ref, cwg, stage_g), (dyv, uv_ref, cwv, stage_v)):
                if at_end:
                    d0 = dy
                    d1 = jnp.where(tail_rows < R - 1, pltpu.roll(dy, R - 1, 0), 0.0)
                    d2 = jnp.where(tail_rows < R - 2, pltpu.roll(dy, R - 2, 0), 0.0)
                else:
                    d0 = dy[0:R, :]
                    d1 = pltpu.roll(dy, n - 1, 0)[0:R, :]
                    d2 = pltpu.roll(dy, n - 2, 0)[0:R, :]
                stage[pl.ds(c0, R), :] = (cwt[2:3, :] * d0 + cwt[1:2, :] * d1 + cwt[0:1, :] * d2).astype(BF16)
                u = u_ref[pl.ds(c0, R), :].astype(F32)
                out += [jnp.sum(d2 * u, axis=0, keepdims=True), jnp.sum(d1 * u, axis=0, keepdims=True),
                        jnp.sum(d0 * u, axis=0, keepdims=True), jnp.sum(d0, axis=0, keepdims=True)]
            return tuple(a + o for a, o in zip(acc, out))

        zero = jnp.zeros((1, CT), F32)
        acc = lax.fori_loop(0, SEQ // R - 1, lambda i, acc: chunk(pl.multiple_of(i * R, R), acc, False), (zero,) * 8)
        acc = chunk(SEQ - R, acc, True)
        for dcw_ref, dcb_ref, part in ((dcwg_ref, dcbg_ref, acc[0:4]), (dcwv_ref, dcbv_ref, acc[4:8])):
            dcw = jnp.concatenate(part[0:3], axis=0)

            @pl.when(b == 0)
            def _():
                dcw_ref[...] = dcw
                dcb_ref[...] = part[3]

            @pl.when(b > 0)
            def _():
                dcw_ref[...] += dcw
                dcb_ref[...] += part[3]

        writes(0, stage_g, sems.at[0]).start()
        writes(FF, stage_v, sems.at[1]).start()

        @pl.when(step == last)
        def _():
            writes(0, stage_g, sems.at[0]).wait()
            writes(FF, stage_v, sems.at[1]).wait()

    def blk(off):
        return pl.BlockSpec((None, SEQ, CT), lambda c, b: (b, 0, off + c))

    def vec(r, off):
        return pl.BlockSpec((r, CT), lambda c, b: (0, off + c))

    du3, dcwg, dcwv, dcbg, dcbv = _call(
        body, name=name, grid=(NCT, Bn),
        in_specs=[blk(0), blk(0), blk(NCT), blk(0), blk(0), vec(3, 0), vec(3, NCT)],
        out_specs=[pl.BlockSpec(memory_space=pl.ANY), vec(3, 0), vec(3, 0), vec(1, 0), vec(1, 0)],
        out_shape=[_sds((Bn, SEQ, UW), BF16), _sds((3, FF), F32), _sds((3, FF), F32), _sds((1, FF), F32),
                   _sds((1, FF), F32)],
        scratch_shapes=[pltpu.VMEM((SEQ, CT), BF16)] * 2 + [pltpu.SemaphoreType.DMA((2,))], comm=comm,
    )(da3, u3, u3, yg3, yv3, cw, cw)
    return du3, jnp.concatenate([dcwg, dcwv], axis=1), jnp.concatenate([dcbg, dcbv], axis=1)


def _rope_tables():
    pos = jnp.arange(SEQ, dtype=F32)
    inv_freq = jnp.exp(jnp.arange(0, ROPE_DIM, 2, dtype=F32) * (-math.log(ROPE_THETA) / ROPE_DIM))
    ang = pos[:, None] * inv_freq[None, :]
    cos, sin = jnp.cos(ang), jnp.sin(ang)
    half = ROPE_DIM // 2
    zeros = jnp.zeros((SEQ, HEAD - ROPE_DIM), F32)
    zh = jnp.zeros((SEQ, half), F32)
    tab_c = jnp.concatenate([cos, cos, zeros + 1.0], axis=1)
    tab_a = jnp.concatenate([-sin, zh, zeros], axis=1)
    tab_b = jnp.concatenate([zh, sin, zeros], axis=1)
    return tab_c, tab_a, tab_b


def _rot(v, tc, ta, tb):
    half = ROPE_DIM // 2
    return v * tc + pltpu.roll(v, HEAD - half, 1) * ta + pltpu.roll(v, half, 1) * tb


def _rot_t(dv, tc, ta, tb):
    half = ROPE_DIM // 2
    return dv * tc + pltpu.roll(dv * ta, half, 1) + pltpu.roll(dv * tb, HEAD - half, 1)


def _band_masks():
    qi = lax.broadcasted_iota(jnp.int32, (HEAD, 2 * HEAD), 0)
    ki = lax.broadcasted_iota(jnp.int32, (HEAD, 2 * HEAD), 1)
    diff = HEAD + qi - ki
    both = (diff >= 0) & (diff <= HEAD)
    q1 = lax.broadcasted_iota(jnp.int32, (HEAD, HEAD), 0)
    k1 = lax.broadcasted_iota(jnp.int32, (HEAD, HEAD), 1)
    return q1 >= k1, both


_NT = (((1,), (1,)), ((), ()))
_TN = (((0,), (0,)), ((), ()))
_SCALE = HEAD ** -0.5


ATT_W = HEAD
ATT_HP = ATT_W // HEAD


def _res_rows(r, n, d, base=0):
    return pl.ds(base * d + r, n, stride=d) if d > 1 else pl.ds(base, n)


def _attn_load(q_ref, k_ref, v_ref, tc_ref, ta_ref, tb_ref, qs, ks, vs, d):
    L = SEQ // d
    for r in range(d):
        rows = _res_rows(r, L, d)
        dst = slice(r * L, (r + 1) * L)
        tc, ta, tb = tc_ref[dst, :], ta_ref[dst, :], tb_ref[dst, :]
        for hh in range(ATT_HP):
            sl = slice(hh * HEAD, (hh + 1) * HEAD)
            qs[dst, sl] = _rot(q_ref[rows, sl], tc, ta, tb).astype(BF16)
            ks[dst, sl] = _rot(k_ref[rows, sl], tc, ta, tb).astype(BF16)
            vs[dst, sl] = v_ref[rows, sl].astype(BF16)


def _attn_fwd(name, z3, tabs, g, d, comm=None):
    Bn = z3.shape[0]
    L = SEQ // d
    nb = L // HEAD
    W, nh = ATT_W, GROUP_W // ATT_W

    def body(q_ref, k_ref, v_ref, tc_ref, ta_ref, tb_ref, o_ref, l_ref, qs, ks, vs, sc, pc):
        m_first, m_both = _band_masks()
        _attn_load(q_ref, k_ref, v_ref, tc_ref, ta_ref, tb_ref, qs, ks, vs, d)
        blocks = [(r, n) for r in range(d) for n in range(nb)]

        def spans(r, n):
            rq = slice(r * L + n * HEAD, r * L + (n + 1) * HEAD)
            rk = slice(r * L + max(n - 1, 0) * HEAD, r * L + (n + 1) * HEAD)
            return rq, rk, slice(0, HEAD if n == 0 else 2 * HEAD)

        for i, (r, n) in enumerate(blocks):
            rq, rk, kc = spans(r, n)
            sc[i, :, kc] = lax.dot_general(qs[rq, :], ks[rk, :], _NT, preferred_element_type=F32)
        for i, (r, n) in enumerate(blocks):
            rq, rk, kc = spans(r, n)
            s = jnp.where(m_first if n == 0 else m_both, sc[i, :, kc] * _SCALE, NEG_INF)
            m = jnp.max(s, axis=-1, keepdims=True)
            e = jnp.exp(s - m)
            den = jnp.sum(e, axis=-1, keepdims=True)
            pc[i, :, kc] = (e * (1.0 / den)).astype(BF16)
            l_ref[_res_rows(r, HEAD, d, n * HEAD), :] = jnp.broadcast_to(m + jnp.log(den), (HEAD, HEAD))
        for i, (r, n) in enumerate(blocks):
            rq, rk, kc = spans(r, n)
            o_ref[_res_rows(r, HEAD, d, n * HEAD), :] = jnp.dot(pc[i, :, kc], vs[rk, :], preferred_element_type=F32)

    def zcol(off):
        return pl.BlockSpec((None, SEQ, W), lambda b, h: (b, 0, (off + g * GROUP_W) // W + h))

    tab = pl.BlockSpec((SEQ, HEAD), lambda b, h: (0, 0))
    out = pl.BlockSpec((None, SEQ, W), lambda b, h: (b, 0, h))
    return _call(
        body, name=name, grid=(Bn, nh),
        in_specs=[zcol(0), zcol(OFF_K), zcol(OFF_V), tab, tab, tab],
        out_specs=[out, out],
        out_shape=[_sds((Bn, SEQ, GROUP_W), F32), _sds((Bn, SEQ, GROUP_W), F32)],
        scratch_shapes=[pltpu.VMEM((SEQ, W), BF16)] * 3
        + [pltpu.VMEM((SEQ // HEAD, HEAD, 2 * HEAD), F32), pltpu.VMEM((SEQ // HEAD, HEAD, 2 * HEAD), BF16)],
        comm=comm,
    )(z3, z3, z3, *tabs)


def _attn_bwd(name, z3, tabs, g, d, do3, lse3, delta3, dz3, comm=None):
    Bn = z3.shape[0]
    L = SEQ // d
    nb = L // HEAD
    W, nh = ATT_W, GROUP_W // ATT_W

    def body(q_ref, k_ref, v_ref, tc_ref, ta_ref, tb_ref, do_ref, l_ref, dl_ref, dz_in, dz_ref,
             qs, ks, vs, dos, dqs, dks, dvs, nat, oq, ok, ov, sc, dpc, pc, dsc, sems):
        b, h = pl.program_id(0), pl.program_id(1)
        m_first, m_both = _band_masks()
        _attn_load(q_ref, k_ref, v_ref, tc_ref, ta_ref, tb_ref, qs, ks, vs, d)
        for r in range(d):
            dos[r * L:(r + 1) * L, :] = do_ref[_res_rows(r, L, d), :].astype(BF16)
        dks[...] = jnp.zeros_like(dks)
        dvs[...] = jnp.zeros_like(dvs)
        blocks = [(r, n) for r in range(d) for n in range(nb)]

        def spans(r, n):
            rq = slice(r * L + n * HEAD, r * L + (n + 1) * HEAD)
            rk = slice(r * L + max(n - 1, 0) * HEAD, r * L + (n + 1) * HEAD)
            return rq, rk, slice(0, HEAD if n == 0 else 2 * HEAD)

        for i, (r, n) in enumerate(blocks):
            rq, rk, kc = spans(r, n)
            sc[i, :, kc] = lax.dot_general(qs[rq, :], ks[rk, :], _NT, preferred_element_type=F32)
            dpc[i, :, kc] = lax.dot_general(dos[rq, :], vs[rk, :], _NT, preferred_element_type=F32)
        for i, (r, n) in enumerate(blocks):
            rq, rk, kc = spans(r, n)
            rows = _res_rows(r, HEAD, d, n * HEAD)
            s = jnp.where(m_first if n == 0 else m_both, sc[i, :, kc] * _SCALE, NEG_INF)
            p = jnp.exp(s - l_ref[rows, :][:, 0:1])
            pc[i, :, kc] = p.astype(BF16)
            dsc[i, :, kc] = (p * (dpc[i, :, kc] - dl_ref[rows, :][:, 0:1]) * _SCALE).astype(BF16)
        for i, (r, n) in enumerate(blocks):
            rq, rk, kc = spans(r, n)
            dqs[rq, :] = jnp.dot(dsc[i, :, kc], ks[rk, :], preferred_element_type=F32)
        for i, (r, n) in enumerate(blocks):
            rq, rk, kc = spans(r, n)
            dks[rk, :] += lax.dot_general(dsc[i, :, kc], qs[rq, :], _TN, preferred_element_type=F32)
            dvs[rk, :] += lax.dot_general(pc[i, :, kc], dos[rq, :], _TN, preferred_element_type=F32)
        step = b * nh + h

        def writes():
            base = g * GROUP_W + h * W
            return [pltpu.make_async_copy(src, dz_ref.at[b, :, pl.ds(pl.multiple_of(base + off, HEAD), W)],
                                          sems.at[i])
                    for i, (src, off) in enumerate(((oq, 0), (ok, OFF_K), (ov, OFF_V)))]

        @pl.when(step > 0)
        def _():
            for cp in writes():
                cp.wait()

        for src, dst, rotate in ((dqs, oq, True), (dks, ok, True), (dvs, ov, False)):
            for r in range(d):
                val = src[r * L:(r + 1) * L, :]
                if rotate:
                    rm = slice(r * L, (r + 1) * L)
                    val = _rot_t(val, tc_ref[rm, :], ta_ref[rm, :], tb_ref[rm, :])
                nat[_res_rows(r, L, d), :] = val
            dst[...] = nat[...].astype(BF16)
        for cp in writes():
            cp.start()

        @pl.when(step == Bn * nh - 1)
        def _():
            for cp in writes():
                cp.wait()

    def zcol(off):
        return pl.BlockSpec((None, SEQ, W), lambda b, h: (b, 0, (off + g * GROUP_W) // W + h))

    tab = pl.BlockSpec((SEQ, HEAD), lambda b, h: (0, 0))
    gcol = pl.BlockSpec((None, SEQ, W), lambda b, h: (b, 0, h))
    any_spec = pl.BlockSpec(memory_space=pl.ANY)
    return _call(
        body, name=name, grid=(Bn, nh),
        in_specs=[zcol(0), zcol(OFF_K), zcol(OFF_V), tab, tab, tab, gcol, gcol, gcol, any_spec],
        out_specs=any_spec,
        out_shape=_sds((Bn, SEQ, ZW), BF16),
        scratch_shapes=[pltpu.VMEM((SEQ, W), BF16)] * 4 + [pltpu.VMEM((SEQ, W), F32)] * 4
        + [pltpu.VMEM((SEQ, W), BF16)] * 3
        + [pltpu.VMEM((SEQ // HEAD, HEAD, 2 * HEAD), F32)] * 2 + [pltpu.VMEM((SEQ // HEAD, HEAD, 2 * HEAD), BF16)] * 2
        + [pltpu.SemaphoreType.DMA((3,))],
        aliases={9: 0}, comm=comm,
    )(z3, z3, z3, *tabs, do3, lse3, delta3, dz3)


def _merge_weights(l0, l1, l2):
    m = jnp.maximum(jnp.maximum(l0, l1), l2)
    e0, e1, e2 = jnp.exp(l0 - m), jnp.exp(l1 - m), jnp.exp(l2 - m)
    inv = 1.0 / (e0 + e1 + e2)
    return e0 * inv, e1 * inv, e2 * inv


def _merge_fwd(name, outs, lses, tr=512):
    T = outs[0].shape[0]

    def body(o0, o1, o2, l0, l1, l2, a_ref):
        w0, w1, w2 = _merge_weights(l0[...], l1[...], l2[...])
        a_ref[...] = (w0 * o0[...] + w1 * o1[...] + w2 * o2[...]).astype(BF16)

    row = pl.BlockSpec((tr, GROUP_W), lambda i: (i, 0))
    return pl.pallas_call(
        body, name=name, grid=(T // tr,), in_specs=[row] * 6, out_specs=row,
        out_shape=_sds((T, GROUP_W), BF16), compiler_params=_params(1),
    )(*outs, *lses)


def _merge_bwd(name, outs, lses, dattn, tr=512):
    T = outs[0].shape[0]

    def body(o0, o1, o2, l0, l1, l2, da_ref, d0, d1, d2, e0, e1, e2):
        w = _merge_weights(l0[...], l1[...], l2[...])
        da = da_ref[...]
        attn = w[0] * o0[...] + w[1] * o1[...] + w[2] * o2[...]
        prod = da * attn
        csum = jnp.concatenate(
            [jnp.broadcast_to(jnp.sum(prod[:, hh * HEAD:(hh + 1) * HEAD], axis=-1, keepdims=True), (tr, HEAD))
             for hh in range(GROUP_W // HEAD)], axis=1)
        for wg, d_ref, e_ref in zip(w, (d0, d1, d2), (e0, e1, e2)):
            d_ref[...] = wg * da
            e_ref[...] = wg * csum

    row = pl.BlockSpec((tr, GROUP_W), lambda i: (i, 0))
    res = pl.pallas_call(
        body, name=name, grid=(T // tr,), in_specs=[row] * 7, out_specs=[row] * 6,
        out_shape=[_sds((T, GROUP_W), F32)] * 6,
        compiler_params=_params(1),
    )(*outs, *lses, dattn)
    return res[:3], res[3:]


def _rms_rows(xv, g):
    r = lax.rsqrt(jnp.mean(xv * xv, axis=-1, keepdims=True) + RMS_EPS)
    return (xv * r * g).astype(BF16)


def _epi_norm(val, x_refs, o_ref, y_refs, first_rows):
    o_ref[...] = val
    y_refs[0][...] = _rms_rows(val, x_refs[0][...])


def _epi_ple(val, x_refs, o_ref, y_refs, first_rows):
    o_ref[...] = val
    xn = x_refs[0][...] + x_refs[1][...].astype(F32) * _sigmoid(val)
    y_refs[0][...] = xn
    if len(y_refs) > 1:
        y_refs[1][...] = _rms_rows(xn, x_refs[2][...])


def _epi_norm_bwd(val, x_refs, o_ref, y_refs, first_rows):
    xv = x_refs[0][...]
    r = lax.rsqrt(jnp.mean(xv * xv, axis=-1, keepdims=True) + RMS_EPS)
    xh = xv * r
    part = _rows8(val * xh)

    @pl.when(first_rows)
    def _():
        y_refs[0][...] = part

    @pl.when(jnp.logical_not(first_rows))
    def _():
        y_refs[0][...] += part

    dxh = val * x_refs[2][...]
    o_ref[...] = x_refs[1][...] + r * (dxh - xh * jnp.mean(dxh * xh, axis=-1, keepdims=True))


def _local_step(x3, p4, tgt3, vecs, ex):
    Bn = x3.shape[0]
    T = Bn * SEQ
    x = x3.reshape(T, D)
    tgt = tgt3.reshape(T, D)
    pb = p4.astype(BF16).reshape(DEPTH, T, PLE)
    tabs = {}
    for d in DILATIONS:
        tabs[d] = [t.reshape(SEQ // d, d, HEAD).transpose(1, 0, 2).reshape(SEQ, HEAD) for t in _rope_tables()]
    tm = 1024 if T % 1024 == 0 else 512
    nt = T // tm
    tk = 1024 if T % 1024 == 0 else 512
    ntk = T // tk
    tm5 = 512
    f32o = lambda n: _sds((T, n), F32)

    def spec(shape, fn):
        return pl.BlockSpec(shape, fn)

    def _mm(name, *args, **kwargs):
        return _mm_call(name, *args, comm=ex.hook(name), **kwargs)

    def cols4(a_ref, b_ref):
        av = a_ref[...]
        return jnp.concatenate([_dot(av, b_ref[j]) for j in range(NCHIP)], axis=1)

    def rows4(a_ref, b_ref):
        av = a_ref[...]
        return jnp.concatenate([_dot(av, b_ref[:, j * 256:(j + 1) * 256], ta=True) for j in range(NCHIP)], axis=0)

    def kchunks4(a_ref, b_ref):
        total = _dot(a_ref[:, 0:256], b_ref[0], tb=True)
        for j in range(1, NCHIP):
            total = total + _dot(a_ref[:, j * 256:(j + 1) * 256], b_ref[j], tb=True)
        return total

    row5 = spec((tm5, D), lambda i, *_: (i, 0))
    rowm = spec((tm, D), lambda i, *_: (i, 0))
    gain = spec((1, D), lambda *_: (0, 0))
    bf_rows = (_sds((T, D), BF16), row5)

    saved = []
    h = _rms_fwd("rms_mix0", x, vecs["g_mix"][0:1])
    for l in range(DEPTH):
        L = str(l)
        G = ex.weights(l)
        g_mix, g_ffn, g_ple = (vecs[k][l:l + 1] for k in ("g_mix", "g_ffn", "g_ple"))
        pscale, cb, cw = vecs["pool_scale"][l:l + 1], vecs["conv_b"][l:l + 1], vecs["conv_w"][l]
        z = _mm("mm_z" + L, h, G["in"], grid=(nt, NCHIP),
                a_spec=spec((tm, D), lambda i, n: (i, 0)),
                b_spec=spec((None, D, IN_S), lambda i, n: (n, 0, 0)),
                o_spec=spec((tm, IN_S), lambda i, n: (i, n)), out_shape=f32o(ZW))
        z3 = z.reshape(Bn, SEQ, ZW)
        outs, lses = [], []
        for g, d in enumerate(DILATIONS):
            o_g, l_g = _attn_fwd("attn_fwd%d_%d" % (g, l), z3, tabs[d], g, d, comm=ex.hook("attn_fwd%d_%d" % (g, l)))
            outs.append(o_g.reshape(T, GROUP_W))
            lses.append(l_g.reshape(T, GROUP_W))
        attn = _merge_fwd("merge_fwd" + L, outs, lses)
        ya = _mm("mm_ya" + L, attn, G["sc"], grid=(nt,), compute=cols4,
                 a_spec=spec((tm, GROUP_W), lambda i: (i, 0)),
                 b_spec=spec((NCHIP, GROUP_W, 256), lambda i: (0, 0, 0)),
                 o_spec=spec((tm, D), lambda i: (i, 0)), out_shape=_sds((T, D), BF16))
        pooled3, ms3 = _pool_fwd("pool_fwd" + L, z3, G["sc"], pscale)
        ms = ms3.reshape(T, D)

        def row_sharded(name, a, rb, res=None, kdim=D, out=F32, **fused):
            if rb is None:
                b_arr, b_spec = G["dn"], spec((NCHIP, DN_S, D), lambda i: (0, 0, 0))
            else:
                b_arr, b_spec = G["r3"], spec((NCHIP, 256, D), lambda i: (0, rb, 0))
            return _mm(name, a, b_arr, grid=(T // tm5,),
                       a_spec=spec((tm5, kdim), lambda i: (i, 0)), b_spec=b_spec,
                       o_spec=row5, out_shape=_sds((T, D), out), res=res, res_spec=None if res is None else row5,
                       **fused)

        yb = row_sharded("mm_yb" + L, ms, 0, out=BF16)
        merged = _gate_fwd("gate_fwd" + L, z, ya, yb)
        x1, h2 = row_sharded("mm_o" + L, merged, 1, res=x, epilogue=_epi_norm, extra_in=[(g_ffn, gain)],
                             extra_out=[bf_rows])
        u = _mm("mm_up" + L, h2, G["up"], grid=(nt, NCHIP),
                a_spec=spec((tm, D), lambda i, n: (i, 0)),
                b_spec=spec((None, D, UP_S), lambda i, n: (n, 0, 0)),
                o_spec=spec((tm, UP_S), lambda i, n: (i, n)), out_shape=_sds((T, UW), BF16))
        u3 = u.reshape(Bn, SEQ, UW)
        act3, yg3, yv3 = _conv_fwd("conv_fwd" + L, u3, cw, cb, comm=ex.hook("conv_fwd" + L))
        act = act3.reshape(T, FF)
        x2, h3 = row_sharded("mm_down" + L, act, None, res=x1, kdim=FF, epilogue=_epi_norm,
                             extra_in=[(g_ple, gain)], extra_out=[bf_rows])
        pe = _mm("mm_pe" + L, pb[l], G["sc"], grid=(nt,), compute=cols4,
                 a_spec=spec((tm, PLE), lambda i: (i, 0)),
                 b_spec=spec((NCHIP, 256, 256), lambda i: (0, 2, 0)),
                 o_spec=spec((tm, D), lambda i: (i, 0)), out_shape=_sds((T, D), BF16))
        fused_in = [(x2, row5), (pe, row5)]
        fused_out = [(f32o(D), row5)]
        if l + 1 < DEPTH:
            fused_in.append((vecs["g_mix"][l + 1:l + 2], gain))
            fused_out.append(bf_rows)
        pg, x3n, *h_next = row_sharded("mm_pg" + L, h3, 2, epilogue=_epi_ple, extra_in=fused_in,
                                       extra_out=fused_out)
        saved.append(dict(x=x, h=h, z=z, outs=outs, lses=lses, attn=attn, ya=ya, yb=yb, pooled3=pooled3, ms=ms,
                          merged=merged, x1=x1, h2=h2, u3=u3, yg3=yg3, yv3=yv3, act=act, x2=x2, h3=h3, pg=pg, pe=pe))
        x = x3n
        h = h_next[0] if h_next else None

    dx, dg_final8, sq8 = _final_loss(x, vecs["g_final"].reshape(1, D), tgt)

    gg_shape = {k: _sds(G[k].shape, F32) for k in G}
    small = {"g_final": dg_final8}

    for l in reversed(range(DEPTH)):
        L = str(l)
        sv = saved[l]
        G = ex.weights(l)
        GG = dict.fromkeys(_KEYS)
        g_mix, g_ffn, g_ple = (vecs[k][l:l + 1] for k in ("g_mix", "g_ffn", "g_ple"))
        pscale, cb, cw = vecs["pool_scale"][l:l + 1], vecs["conv_b"][l:l + 1], vecs["conv_w"][l]

        def wgrad_rows(name, a, b_arr, key, rb):
            GG[key] = _mm(name, a, b_arr, grid=(2, ntk), ta=True, k_axis=1, nk=ntk, acc_shape=(D, 512),
                          a_spec=spec((tk, D), lambda n, k: (k, 0)),
                          b_spec=spec((tk, 512), lambda n, k: (k, n)),
                          o_spec=spec((NCHIP, 256, 512), lambda n, k: (0, rb, n)),
                          out_shape=gg_shape[key], buf=GG[key])

        def dgrad_rows(name, dy, rb, out=F32, **fused):
            return _mm(name, dy, G["r3"], grid=(T // tm5,), tb=True,
                       a_spec=spec((tm5, D), lambda i: (i, 0)),
                       b_spec=spec((NCHIP, 256, D), lambda i: (0, rb, 0)),
                       o_spec=row5, out_shape=_sds((T, D), out), **fused)

        def norm_bwd(xin, dres, g, rows=row5):
            return dict(epilogue=_epi_norm_bwd, extra_in=[(xin, rows), (dres, rows), (g, gain)],
                        extra_out=[(_sds((8, D), F32), spec((8, D), lambda *_: (0, 0)))])

        dpe, dpg = _ple_bwd("ple_bwd" + L, dx, sv["pe"], sv["pg"])
        GG["sc"] = _mm("wg_ple" + L, pb[l], dpe, grid=(ntk,), compute=rows4, k_axis=0, nk=ntk,
                       acc_shape=(NCHIP * PLE, 256),
                       a_spec=spec((tk, PLE), lambda k: (k, 0)), b_spec=spec((tk, D), lambda k: (k, 0)),
                       o_spec=spec((NCHIP, 256, 256), lambda k: (0, 2, 0)),
                       out_shape=gg_shape["sc"], buf=GG["sc"])
        wgrad_rows("wg_pg" + L, sv["h3"], dpg, "r3", 2)
        dx, small["g_ple" + L] = dgrad_rows("dg_pg" + L, dpg, 2, **norm_bwd(sv["x2"], dx, g_ple))

        da = _mm("dg_down" + L, dx, G["dn"], grid=(T // 256,), tb=True,
                 a_spec=spec((256, D), lambda i: (i, 0)),
                 b_spec=spec((NCHIP, DN_S, D), lambda i: (0, 0, 0)),
                 o_spec=spec((256, FF), lambda i: (i, 0)), out_shape=_sds((T, FF), BF16))
        GG["dn"] = _mm("wg_down" + L, sv["act"], dx, grid=(2, ntk), ta=True, k_axis=1, nk=ntk,
                       acc_shape=(FF, 512),
                       a_spec=spec((tk, FF), lambda n, k: (k, 0)), b_spec=spec((tk, 512), lambda n, k: (k, n)),
                       o_spec=spec((NCHIP, DN_S, 512), lambda n, k: (0, 0, n)),
                       out_shape=gg_shape["dn"], buf=GG["dn"])
        du3, dcw, dcb = _conv_bwd("conv_bwd" + L, da.reshape(Bn, SEQ, FF), sv["u3"], sv["yg3"], sv["yv3"], cw,
                                  comm=ex.hook("conv_bwd" + L))
        small["conv_w" + L], small["conv_b" + L] = dcw, dcb
        du = du3.reshape(T, UW)
        dx, small["g_ffn" + L] = _mm(
            "dg_up" + L, du, G["up"], grid=(nt, NCHIP), tb=True, k_axis=1, nk=NCHIP, acc_shape=(tm, D),
            a_spec=spec((tm, UP_S), lambda i, k: (i, k)), b_spec=spec((None, D, UP_S), lambda i, k: (k, 0, 0)),
            o_spec=rowm, out_shape=f32o(D), vmem=VMEM_BIG, **norm_bwd(sv["x1"], dx, g_ffn, rowm))
        GG["up"] = _mm("wg_up" + L, sv["h2"], du, grid=(NCHIP, ntk), ta=True, k_axis=1, nk=ntk,
                       acc_shape=(D, UP_S),
                       a_spec=spec((tk, D), lambda j, k: (k, 0)),
                       b_spec=spec((tk, UP_S), lambda j, k: (k, j)),
                       o_spec=spec((None, D, UP_S), lambda j, k: (j, 0, 0)),
                       out_shape=gg_shape["up"], buf=GG["up"])

        dmerged = dgrad_rows("dg_o" + L, dx, 1, out=BF16)
        wgrad_rows("wg_o" + L, sv["merged"], dx, "r3", 1)
        dya, dz = _gate_bwd("gate_bwd_a" + L, sv["z"], OFF_GA, sv["ya"], dmerged, None)
        dyb, dz = _gate_bwd("gate_bwd_b" + L, sv["z"], OFF_GB, sv["yb"], dmerged, dz)
        dms = dgrad_rows("dg_yb" + L, dyb, 0)
        wgrad_rows("wg_yb" + L, sv["ms"], dyb, "r3", 0)
        dz3, GG["sc"], small["pool_scale" + L] = _pool_bwd(
            "pool_bwd" + L, dms.reshape(Bn, SEQ, D), sv["pooled3"], G["sc"], pscale,
            dz.reshape(Bn, SEQ, ZW), GG["sc"])
        dattn = _mm("dg_ya" + L, dya, G["sc"], grid=(nt,), compute=kchunks4,
                    a_spec=spec((tm, D), lambda i: (i, 0)),
                    b_spec=spec((NCHIP, GROUP_W, 256), lambda i: (0, 0, 0)),
                    o_spec=spec((tm, GROUP_W), lambda i: (i, 0)), out_shape=f32o(GROUP_W))
        GG["sc"] = _mm("wg_ya" + L, sv["attn"], dya, grid=(ntk,), compute=rows4, k_axis=0, nk=ntk,
                       acc_shape=(NCHIP * GROUP_W, 256),
                       a_spec=spec((tk, GROUP_W), lambda k: (k, 0)), b_spec=spec((tk, D), lambda k: (k, 0)),
                       o_spec=spec((NCHIP, GROUP_W, 256), lambda k: (0, 0, 0)),
                       out_shape=gg_shape["sc"], buf=GG["sc"])
        ex.grads_ready(l, {k: GG[k] for k in _KEYS[1:]})
        dos, deltas = _merge_bwd("merge_bwd" + L, sv["outs"], sv["lses"], dattn)
        view3 = lambda t: t.reshape(Bn, SEQ, GROUP_W)
        sz3 = sv["z"].reshape(Bn, SEQ, ZW)
        for g, d in enumerate(DILATIONS):
            dz3 = _attn_bwd("attn_bwd%d_%d" % (g, l), sz3, tabs[d], g, d, view3(dos[g]), view3(sv["lses"][g]),
                            view3(deltas[g]), dz3, comm=ex.hook("attn_bwd%d_%d" % (g, l)))
        dz = dz3.reshape(T, ZW)
        GG["in"] = _mm("wg_z" + L, sv["h"], dz, grid=(NCHIP, ntk), ta=True, k_axis=1, nk=ntk,
                       acc_shape=(D, IN_S),
                       a_spec=spec((tk, D), lambda n, k: (k, 0)), b_spec=spec((tk, IN_S), lambda n, k: (k, n)),
                       o_spec=spec((None, D, IN_S), lambda n, k: (n, 0, 0)),
                       out_shape=gg_shape["in"], buf=GG["in"])
        ex.grads_ready(l, {"in": GG["in"]})

        def dgrad_z(name, first, count, dres, buf):
            rows = spec((tm, D), lambda i, *_: (i + first, 0))
            return _mm(name, dz, G["in"], grid=(count, NCHIP), tb=True, k_axis=1, nk=NCHIP, acc_shape=(tm, D),
                       a_spec=spec((tm, IN_S), lambda i, k: (i + first, k)),
                       b_spec=spec((None, D, IN_S), lambda i, k: (k, 0, 0)),
                       o_spec=rows, out_shape=f32o(D), vmem=VMEM_BIG, buf=buf, **norm_bwd(sv["x"], dres, g_mix, rows))

        if l == 0 and nt % 2 == 0:
            dx_a, part_a = dgrad_z("dg_z0a", 0, nt // 2, dx, None)
            dx, part_b = dgrad_z("dg_z0b", nt // 2, nt // 2, dx, dx_a)
            small["g_mix" + L] = part_a + part_b
        else:
            dx, small["g_mix" + L] = dgrad_z("dg_z" + L, 0, nt, dx, None)

    return sq8, dx.reshape(Bn, SEQ, D), small


_ANY = pl.BlockSpec(memory_space=pl.ANY)


def _place():
    x, y, c = lax.axis_index("x"), lax.axis_index("y"), lax.axis_index("c")
    chips = [(1 - x, y), (x, 1 - y), (1 - x, 1 - y)]
    return x, y, c, 2 * x + y, chips


def _half(rows, cc):
    return pl.ds(cc * (rows // 2), rows // 2)


def _remote(src, dst, send_sems, recv_sems, i, to):
    return pltpu.make_async_remote_copy(src_ref=src, dst_ref=dst, send_sem=send_sems.at[i], recv_sem=recv_sems.at[i],
                                        device_id=to, device_id_type=MESH)


def _exchange_gather_ici(stacks, done):
    n = len(stacks)
    rows = [t.shape[1] for t in stacks]

    def start(refs, send_sems, recv_sems):
        x, y, c, me, chips = _place()
        for k in range(n):
            part = refs[k].at[me, _half(rows[k], c)]
            for j, chip in enumerate(chips):
                _remote(part, part, send_sems, recv_sems, 3 * k + j, (*chip, c)).start()

    def wait(refs, send_sems, recv_sems):
        x, y, c, me, chips = _place()
        for k in range(n):
            for j, chip in enumerate(chips):
                part = refs[k].at[2 * chip[0] + chip[1], _half(rows[k], c)]
                _remote(part, part, send_sems, recv_sems, 3 * k + j, (*chip, c)).wait()

    return dict(arrays=list(stacks), nsem=3 * n, start=start, wait=wait, done=done)


def _exchange_gather_d2d(stacks, done):
    n = len(stacks)
    rows = [t.shape[1] for t in stacks]

    def copies(refs, send_sems, recv_sems, mine):
        x, y, c, me, chips = _place()
        cc = c if mine else 1 - c
        return [_remote(part, part, send_sems, recv_sems, 3 * k + j, (x, y, 1 - c))
                for k in range(n) for j, chip in enumerate(chips)
                for part in [refs[k].at[2 * chip[0] + chip[1], _half(rows[k], cc)]]]

    def start(refs, send_sems, recv_sems):
        for cp in copies(refs, send_sems, recv_sems, True):
            cp.start()

    def wait(refs, send_sems, recv_sems):
        for cp in copies(refs, send_sems, recv_sems, False):
            cp.wait()

    return dict(arrays=list(stacks), nsem=3 * n, start=start, wait=wait, done=done)


def _exchange_halves(g5, recv, done):
    n = len(g5)

    def copies(refs, send_sems, recv_sems):
        x, y, c, me, chips = _place()
        return [_remote(refs[k].at[:, 1 - c], refs[n + k], send_sems, recv_sems, k, (x, y, 1 - c)) for k in range(n)]

    def start(refs, send_sems, recv_sems):
        for cp in copies(refs, send_sems, recv_sems):
            cp.start()

    def wait(refs, send_sems, recv_sems):
        for cp in copies(refs, send_sems, recv_sems):
            cp.wait()

    return dict(arrays=list(g5) + list(recv), nsem=n, start=start, wait=wait, done=done)


def _exchange_chips(parts, landing, done):
    n = len(parts)

    def start(refs, send_sems, recv_sems):
        x, y, c, me, chips = _place()
        for k in range(n):
            for j, chip in enumerate(chips):
                _remote(refs[k].at[2 * chip[0] + chip[1]], refs[n + k].at[me], send_sems, recv_sems, 3 * k + j,
                        (*chip, c)).start()

    def wait(refs, send_sems, recv_sems):
        x, y, c, me, chips = _place()
        for k in range(n):
            for j, chip in enumerate(chips):
                slot = refs[n + k].at[2 * chip[0] + chip[1]]
                _remote(slot, slot, send_sems, recv_sems, 3 * k + j, (*chip, c)).wait()

    return dict(arrays=list(parts) + list(landing), nsem=3 * n, start=start, wait=wait, done=done)


def _exchange_share(full, layer, done):
    n = len(full)

    def copies(refs, send_sems, recv_sems, mine):
        x, y, c, me, chips = _place()
        cc = c if mine else 1 - c
        return [_remote(part, part, send_sems, recv_sems, k, (x, y, 1 - c))
                for k in range(n) for part in [refs[k].at[layer, cc]]]

    def start(refs, send_sems, recv_sems):
        for cp in copies(refs, send_sems, recv_sems, True):
            cp.start()

    def wait(refs, send_sems, recv_sems):
        for cp in copies(refs, send_sems, recv_sems, False):
            cp.wait()

    return dict(arrays=list(full), nsem=n, start=start, wait=wait, done=done)


def _exchange_call(name, comm):
    arrays = comm["arrays"]
    n = len(arrays)

    def body(*refs):
        outs, send_sems, recv_sems = refs[n:2 * n], refs[2 * n], refs[2 * n + 1]
        comm["start"](outs, send_sems, recv_sems)
        comm["wait"](outs, send_sems, recv_sems)

    outs = pl.pallas_call(
        body, name=name, in_specs=[_ANY] * n, out_specs=[_ANY] * n,
        out_shape=[_sds(t.shape, t.dtype) for t in arrays],
        scratch_shapes=[pltpu.SemaphoreType.DMA((comm["nsem"],))] * 2,
        input_output_aliases={i: i for i in range(n)},
    )(*arrays)
    comm["done"](outs)


def _gather_first(stacks, cw4):
    n = len(stacks)
    ici = _exchange_gather_ici(stacks, None)
    d2d = _exchange_gather_d2d(stacks, None)
    rows = [t.shape[1] for t in stacks]

    def body(*refs):
        g_refs, cwg_ref = refs[n + 1:2 * n + 1], refs[2 * n + 1]
        s_ici, r_ici, s_d2d, r_d2d, s_cw, r_cw = refs[2 * n + 2:]
        x, y, c, me, chips = _place()

        def cw_copy(j, slot, chip):
            part = cwg_ref.at[slot]
            return _remote(part, part, s_cw, r_cw, j, (*chip, c))

        ici["start"](g_refs, s_ici, r_ici)
        for j, chip in enumerate(chips):
            cw_copy(j, me, chip).start()
        for k in range(n):
            for j, chip in enumerate(chips):
                part = g_refs[k].at[2 * chip[0] + chip[1], _half(rows[k], c)]
                _remote(part, part, s_ici, r_ici, 3 * k + j, (*chip, c)).wait()
                _remote(part, part, s_d2d, r_d2d, 3 * k + j, (x, y, 1 - c)).start()
        d2d["wait"](g_refs, s_d2d, r_d2d)
        for j, chip in enumerate(chips):
            cw_copy(j, 2 * chip[0] + chip[1], chip).wait()

    outs = pl.pallas_call(
        body, name="gather_first", in_specs=[_ANY] * (n + 1), out_specs=[_ANY] * (n + 1),
        out_shape=[_sds(t.shape, t.dtype) for t in stacks] + [_sds(cw4.shape, cw4.dtype)],
        scratch_shapes=[pltpu.SemaphoreType.DMA((3 * n,))] * 4 + [pltpu.SemaphoreType.DMA((3,))] * 2,
        input_output_aliases={i: i for i in range(n + 1)},
    )(*stacks, cw4)
    return outs[:n], outs[n]


def _small_allreduce(small):
    def body(small_ref, red_ref, gath, s_send, s_recv):
        x, y, c, me, chips = _place()
        dev = 4 * x + 2 * y + c
        gath[dev] = small_ref[...]
        for r in range(1, 8):
            peer = (x ^ (r >> 2), y ^ ((r >> 1) & 1), c ^ (r & 1))
            _remote(small_ref, gath.at[dev], s_send, s_recv, r - 1, peer).start()
        for r in range(1, 8):
            peer = (x ^ (r >> 2), y ^ ((r >> 1) & 1), c ^ (r & 1))
            src = 4 * peer[0] + 2 * peer[1] + peer[2]
            _remote(small_ref, gath.at[src], s_send, s_recv, r - 1, peer).wait()
        total = gath[0]
        for i in range(1, 8):
            total = total + gath[i]
        red_ref[...] = total

    vm = pl.BlockSpec(memory_space=pltpu.VMEM)
    return pl.pallas_call(
        body, name="small_allreduce", in_specs=[vm], out_specs=vm, out_shape=_sds(small.shape, F32),
        scratch_shapes=[pltpu.VMEM((8,) + small.shape, F32), pltpu.SemaphoreType.DMA((7,)),
                        pltpu.SemaphoreType.DMA((7,))],
    )(small)


def _row_tile(rh):
    for cand in (512, 384, 352, 256, 128):
        if rh % cand == 0:
            return cand
    return rh


def _add_halves(name, g5, recv, place):
    _, _, rh, cols = g5.shape
    tr = _row_tile(rh)

    def body(place_ref, g_ref, r_ref, o_ref, own_ref):
        val = (g_ref[...] + r_ref[...]).astype(BF16)
        o_ref[...] = val

        @pl.when(pl.program_id(1) == place_ref[1])
        def _():
            own_ref[...] = val

    grid_spec = pltpu.PrefetchScalarGridSpec(
        num_scalar_prefetch=1, grid=(rh // tr, NCHIP),
        in_specs=[pl.BlockSpec((None, None, tr, cols), lambda i, j, pr: (j, pr[0], i, 0)),
                  pl.BlockSpec((None, tr, cols), lambda i, j, pr: (j, i, 0))],
        out_specs=[pl.BlockSpec((None, tr, cols), lambda i, j, pr: (j, i, 0)),
                   pl.BlockSpec((None, tr, cols), lambda i, j, pr: (pr[1], i, 0))])
    return pl.pallas_call(
        body, name=name, grid_spec=grid_spec, out_shape=[_sds(recv.shape, BF16)] * 2, compiler_params=_params(2),
    )(place, g5, recv)


def _sum_chips(name, landing, place, layer, full):
    _, rh, cols = landing.shape
    tr = _row_tile(rh)
    has_full = full is not None

    def body(*refs):
        r_ref, o_ref = refs[1], refs[-1]
        total = r_ref[0].astype(F32)
        for j in range(1, NCHIP):
            total = total + r_ref[j].astype(F32)
        o_ref[...] = total

    grid_spec = pltpu.PrefetchScalarGridSpec(
        num_scalar_prefetch=1, grid=(rh // tr,),
        in_specs=[pl.BlockSpec((NCHIP, tr, cols), lambda i, pr: (0, i, 0))] + ([_ANY] if has_full else []),
        out_specs=pl.BlockSpec((None, None, tr, cols), lambda i, pr: (layer, pr[0], i, 0)))
    return pl.pallas_call(
        body, name=name, grid_spec=grid_spec, out_shape=_sds((DEPTH, 2, rh, cols), F32),
        input_output_aliases={2: 0} if has_full else {}, compiler_params=_params(1),
    )(place, landing, *([full] if has_full else []))


class _Schedule:
    FIRST, REST = ["in"], list(_KEYS[1:])

    def __init__(self, slotted, cw4, place):
        self.place = place
        self._w = [dict(zip(_KEYS, layer)) for layer in slotted]
        got, self.cw4 = _gather_first([self._w[0][k] for k in self.FIRST], cw4)
        self._w[0].update(zip(self.FIRST, got))
        self._g5, self._recv, self._parts, self._landing = [{}, {}], [{}, {}], [{}, {}], [{}, {}]
        self.full = {}
        every = list(_KEYS)
        self._hooks = {
            "mm_z0": lambda: self._gather(_exchange_gather_ici, 0, self.REST),
            "attn_fwd0_0": lambda: self._gather(_exchange_gather_d2d, 0, self.REST),
            "mm_up0": lambda: self._gather(_exchange_gather_ici, 1, self.REST),
            "conv_fwd0": lambda: self._gather(_exchange_gather_ici, 1, self.FIRST),
            "mm_down0": lambda: self._gather(_exchange_gather_d2d, 1, every),
            "dg_down0": lambda: self._halves(1, every),
            "conv_bwd0": lambda: self._chips(1, every),
            "dg_up0": lambda: self._share(1, every),
            "attn_bwd0_0": lambda: self._halves(0, self.REST),
            "wg_z0": lambda: self._chips(0, self.REST),
            "dg_z0a": lambda: self._halves(0, self.FIRST),
            "dg_z0b": lambda: self._chips(0, self.FIRST),
        }

    def weights(self, layer):
        return self._w[layer]

    def hook(self, name):
        make = self._hooks.get(name)
        return make() if make else None

    def grads_ready(self, layer, GG):
        for k, t in GG.items():
            g5 = t.reshape(NCHIP, 2, t.shape[1] // 2, t.shape[2])
            self._g5[layer][k] = g5
            self._recv[layer][k] = lax.empty((NCHIP,) + g5.shape[2:], F32)

    def _gather(self, make, layer, keys):
        def done(arrays):
            self._w[layer].update(zip(keys, arrays))
        return make([self._w[layer][k] for k in keys], done)

    def _halves(self, layer, keys):
        return _exchange_halves([self._g5[layer][k] for k in keys], [self._recv[layer][k] for k in keys],
                                lambda arrays: self._halves_done(layer, keys, arrays))

    def _halves_done(self, layer, keys, arrays):
        n = len(keys)
        for k, g, r in zip(keys, arrays[:n], arrays[n:]):
            self._parts[layer][k], self._landing[layer][k] = _add_halves(
                "add_halves%d_%s" % (layer, k), g, r, self.place)

    def _chips(self, layer, keys):
        return _exchange_chips([self._parts[layer][k] for k in keys], [self._landing[layer][k] for k in keys],
                               lambda arrays: self._chips_done(layer, keys, arrays))

    def _chips_done(self, layer, keys, arrays):
        for k, t in zip(keys, arrays[len(keys):]):
            self.full[k] = _sum_chips("sum_chips%d_%s" % (layer, k), t, self.place, layer, self.full.get(k))

    def _share(self, layer, keys):
        def done(arrays):
            self.full.update(zip(keys, arrays))
        return _exchange_share([self.full[k] for k in keys], layer, done)

    def finish(self, small):
        small_red = _small_allreduce(small)
        _exchange_call("share_halves_last", self._share(0, list(_KEYS)))
        return [self.full[k] for k in _KEYS], small_red


def _adamw(name, w, g, m, v):
    shape = w.shape
    cols = shape[-1]
    rows = 1
    for s in shape[:-1]:
        rows *= s
    tr = rows
    for cand in (256, 128, 64):
        if rows > cand and rows % cand == 0:
            tr = cand
            break
    c1 = 1.0 / (1.0 - B1 ** STEP)
    c2 = 1.0 / (1.0 - B2 ** STEP)

    def body(w_ref, g_ref, m_ref, v_ref, go_ref, d_ref, nm_ref, nv_ref):
        gv = g_ref[...]
        go_ref[...] = gv
        nm = B1 * m_ref[...] + (1.0 - B1) * gv
        nv = B2 * v_ref[...] + (1.0 - B2) * (gv * gv)
        nm_ref[...] = nm
        nv_ref[...] = nv
        d_ref[...] = -LR * ((nm * c1) / (jnp.sqrt(nv * c2) + ADAM_EPS) + WD * w_ref[...])

    blk = pl.BlockSpec((tr, cols), lambda i: (i, 0))
    outs = pl.pallas_call(
        body, name=name, grid=(rows // tr,), in_specs=[blk] * 4, out_specs=[blk] * 4,
        out_shape=[_sds((rows, cols), F32)] * 4, compiler_params=_params(1),
    )(*(t.reshape(rows, cols) for t in (w, g, m, v)))
    return tuple(o.reshape(shape) for o in outs)


def _pack_small(small):
    rows = [jnp.sum(small["g_mix%d" % l], axis=0, keepdims=True) for l in range(DEPTH)]
    rows += [jnp.sum(small["pool_scale%d" % l], axis=0, keepdims=True) for l in range(DEPTH)]
    rows += [jnp.sum(small["g_ffn%d" % l], axis=0, keepdims=True) for l in range(DEPTH)]
    rows += [jnp.sum(small["g_ple%d" % l], axis=0, keepdims=True) for l in range(DEPTH)]
    rows += [jnp.sum(small["g_final"], axis=0, keepdims=True)]
    flat = [small["conv_b%d" % l].reshape(-1) for l in range(DEPTH)]
    flat += [small["conv_w%d" % l].reshape(-1) for l in range(DEPTH)]
    flat = jnp.concatenate(flat).reshape(-1, D)
    packed = jnp.concatenate(rows + [flat], axis=0)
    return jnp.pad(packed, ((0, SMALL_ROWS - packed.shape[0]), (0, 0)))


def _unpack_small(red):
    g_mix, pool_scale, g_ffn, g_ple = red[0:2], red[2:4], red[4:6], red[6:8]
    g_final = red[8]
    nb = DEPTH * UW // D
    conv_b = red[9:9 + nb].reshape(DEPTH, UW)
    conv_w = red[9 + nb:9 + 4 * nb].reshape(DEPTH, 3, UW)
    return g_mix, pool_scale, g_ffn, g_ple, g_final, conv_b, conv_w


def kernel(x, p, g_mix, w_in, w_ya, w_yb, pool_w, pool_scale, w_o, g_ffn, w_up, conv_w, conv_b, w_down, g_ple, w_ple, w_ple_gate, g_final, loss_target, m_g_mix, m_w_in, m_w_ya, m_w_yb, m_pool_w, m_pool_scale, m_w_o, m_g_ffn, m_w_up, m_conv_w, m_conv_b, m_w_down, m_g_ple, m_w_ple, m_w_ple_gate, m_g_final, v_g_mix, v_w_in, v_w_ya, v_w_yb, v_pool_w, v_pool_scale, v_w_o, v_g_ffn, v_w_up, v_conv_w, v_conv_b, v_w_down, v_g_ple, v_w_ple, v_w_ple_gate, v_g_final):
    me = 2 * lax.axis_index("x") + lax.axis_index("y")
    place = jnp.stack([lax.axis_index("c"), me]).astype(jnp.int32)

    def slot(shard):
        return lax.dynamic_update_index_in_dim(lax.empty((NCHIP,) + shard.shape, shard.dtype), shard, me, 0)

    packed = [
        w_in.astype(BF16), w_up.astype(BF16),
        jnp.concatenate([w_ya, w_ple, pool_w.reshape(DEPTH, 256, 256)], axis=1).astype(BF16),
        jnp.concatenate([w_yb, w_o, w_ple_gate], axis=1).astype(BF16),
        w_down.astype(BF16),
    ]
    slotted = [[slot(t[l]) for t in packed] for l in range(DEPTH)]
    ex = _Schedule(slotted, slot(conv_w.reshape(DEPTH * 3, UP_S)), place)
    cw_full = ex.cw4.reshape(NCHIP, DEPTH, 3, UP_S).transpose(1, 2, 0, 3).reshape(DEPTH, 3, UW)

    vecs = dict(g_mix=g_mix, pool_scale=pool_scale, g_ffn=g_ffn, g_ple=g_ple, g_final=g_final, conv_b=conv_b,
                conv_w=cw_full)
    sq8, grad_x, small = _local_step(x, p, loss_target, vecs, ex)
    loss = lax.psum(jnp.sum(sq8) * (0.5 / D), ("x", "y", "c"))

    full, small_red = ex.finish(_pack_small(small))
    r_in, r_up, r_sc, r_r3, r_dn = [f.reshape(DEPTH, -1, f.shape[-1]) for f in full]
    d_g_mix, d_pool_scale, d_g_ffn, d_g_ple, d_g_final, d_conv_b, d_conv_w_full = _unpack_small(small_red)
    d_conv_w = lax.dynamic_slice_in_dim(d_conv_w_full, me * UP_S, UP_S, axis=2)

    grads = dict(
        g_mix=d_g_mix, w_in=r_in, w_ya=r_sc[:, 0:512], w_yb=r_r3[:, 0:256],
        pool_w=r_sc[:, 768:1024].reshape(DEPTH, 4, 64, 256), pool_scale=d_pool_scale, w_o=r_r3[:, 256:512],
        g_ffn=d_g_ffn, w_up=r_up, conv_w=d_conv_w, conv_b=d_conv_b, w_down=r_dn, g_ple=d_g_ple,
        w_ple=r_sc[:, 512:768], w_ple_gate=r_r3[:, 512:768], g_final=d_g_final)
    weights = dict(g_mix=g_mix, w_in=w_in, w_ya=w_ya, w_yb=w_yb, pool_w=pool_w, pool_scale=pool_scale, w_o=w_o,
                   g_ffn=g_ffn, w_up=w_up, conv_w=conv_w, conv_b=conv_b, w_down=w_down, g_ple=g_ple, w_ple=w_ple,
                   w_ple_gate=w_ple_gate, g_final=g_final)
    m_in = dict(g_mix=m_g_mix, w_in=m_w_in, w_ya=m_w_ya, w_yb=m_w_yb, pool_w=m_pool_w, pool_scale=m_pool_scale,
                w_o=m_w_o, g_ffn=m_g_ffn, w_up=m_w_up, conv_w=m_conv_w, conv_b=m_conv_b, w_down=m_w_down,
                g_ple=m_g_ple, w_ple=m_w_ple, w_ple_gate=m_w_ple_gate, g_final=m_g_final)
    v_in = dict(g_mix=v_g_mix, w_in=v_w_in, w_ya=v_w_ya, w_yb=v_w_yb, pool_w=v_pool_w, pool_scale=v_pool_scale,
                w_o=v_w_o, g_ffn=v_g_ffn, w_up=v_w_up, conv_w=v_conv_w, conv_b=v_conv_b, w_down=v_w_down,
                g_ple=v_g_ple, w_ple=v_w_ple, w_ple_gate=v_w_ple_gate, g_final=v_g_final)
    names = ["g_mix", "w_in", "w_ya", "w_yb", "pool_w", "pool_scale", "w_o", "g_ffn", "w_up", "conv_w", "conv_b",
             "w_down", "g_ple", "w_ple", "w_ple_gate", "g_final"]
    deltas, new_m, new_v = [], [], []
    for nme in names:
        gr = grads[nme].reshape(weights[nme].shape)
        grads[nme], dlt, nm, nv = _adamw("adamw_" + nme, weights[nme], gr, m_in[nme], v_in[nme])
        deltas.append(dlt)
        new_m.append(nm)
        new_v.append(nv)
    return (loss, grad_x, *[grads[nme] for nme in names], *deltas, *new_m, *new_v)
```

```python
import math

import jax
import jax.numpy as jnp
from jax import lax
from jax.experimental import pallas as pl
from jax.experimental.pallas import tpu as pltpu

F32 = jnp.float32
BF16 = jnp.bfloat16
_KEYS = ("in", "up", "sc", "r3", "dn")
MESH = pl.DeviceIdType.MESH

D = 1024
SEQ = 2048
DEPTH = 2
HEAD = 128
GROUP_W = 512
DILATIONS = (1, 4, 16)
ROPE_DIM = 32
ROPE_THETA = 500000.0
NEG_INF = -1e30
ZW = 7680
OFF_K, OFF_V, OFF_U, OFF_GA, OFF_GB = 1536, 3072, 4608, 5632, 6656
FF = 2816
UW = 2 * FF
PLE = 256
NCHIP = 4
IN_S, UP_S, DN_S = ZW // NCHIP, UW // NCHIP, FF // NCHIP
RMS_EPS = 1e-6
LR, B1, B2, ADAM_EPS, WD, STEP = 0.001, 0.9, 0.999, 1e-08, 0.01, 10
SMALL_ROWS = 56
VMEM_CAP = 48 * 1024 * 1024
VMEM_BIG = 58 * 1024 * 1024


def _params(n_grid, vmem=VMEM_CAP):
    return pltpu.CompilerParams(dimension_semantics=("arbitrary",) * n_grid, vmem_limit_bytes=vmem)


def _sigmoid(v):
    return 1.0 / (1.0 + jnp.exp(-v))


def _rows8(v):
    return jnp.sum(v.reshape(v.shape[0] // 8, 8, v.shape[1]), axis=0)


def _sds(shape, dtype):
    return jax.ShapeDtypeStruct(shape, dtype)


def _dot(av, bv, ta=False, tb=False):
    dims = (((0,) if ta else (1,), (1,) if tb else (0,)), ((), ()))
    return lax.dot_general(av.astype(BF16), bv.astype(BF16), dims, preferred_element_type=F32)


def _call(body, *, name, grid, in_specs, out_specs, out_shape, scratch_shapes=(), aliases=None, comm=None,
          vmem=VMEM_CAP):
    params = _params(len(grid), vmem)
    aliases = dict(aliases or {})
    if comm is None:
        return pl.pallas_call(body, name=name, grid=grid, in_specs=list(in_specs), out_specs=out_specs,
                              out_shape=out_shape, scratch_shapes=list(scratch_shapes),
                              input_output_aliases=aliases, compiler_params=params)
    single = not isinstance(out_shape, (list, tuple))
    out_specs_l = [out_specs] if single else list(out_specs)
    out_shape_l = [out_shape] if single else list(out_shape)
    n_in, n_out, n_c = len(in_specs), len(out_shape_l), len(comm["arrays"])

    def hosted(*refs):
        core_in, core_out = refs[:n_in], refs[n_in + n_c:n_in + n_c + n_out]
        c_refs = refs[n_in + n_c + n_out:n_in + 2 * n_c + n_out]
        scratch, (send_sems, recv_sems) = refs[n_in + 2 * n_c + n_out:-2], refs[-2:]
        ids = [pl.program_id(i) for i in range(len(grid))]
        first, last = ids[0] == 0, ids[0] == grid[0] - 1
        for i in range(1, len(grid)):
            first = jnp.logical_and(first, ids[i] == 0)
            last = jnp.logical_and(last, ids[i] == grid[i] - 1)

        @pl.when(first)
        def _():
            comm["start"](c_refs, send_sems, recv_sems)

        body(*core_in, *core_out, *scratch)

        @pl.when(last)
        def _():
            comm["wait"](c_refs, send_sems, recv_sems)

    any_spec = pl.BlockSpec(memory_space=pl.ANY)
    for i in range(n_c):
        aliases[n_in + i] = n_out + i
    call = pl.pallas_call(
        hosted, name=name, grid=grid, in_specs=list(in_specs) + [any_spec] * n_c,
        out_specs=out_specs_l + [any_spec] * n_c,
        out_shape=out_shape_l + [_sds(t.shape, t.dtype) for t in comm["arrays"]],
        scratch_shapes=list(scratch_shapes) + [pltpu.SemaphoreType.DMA((comm["nsem"],))] * 2,
        input_output_aliases=aliases, compiler_params=params)

    def run(*args):
        outs = call(*args, *comm["arrays"])
        comm["done"](outs[n_out:])
        return outs[0] if single else outs[:n_out]

    return run


def _mm_call(name, a, b, *, grid, a_spec, b_spec, o_spec, out_shape, ta=False, tb=False, k_axis=None, nk=1,
             acc_shape=None, res=None, res_spec=None, buf=None, compute=None, comm=None,
             extra_in=(), extra_out=(), epilogue=None, vmem=VMEM_CAP):
    has_res, has_buf = res is not None, buf is not None
    in_place = nk > 1 and not has_res and out_shape.dtype == F32 and epilogue is None
    n_xi, n_xo = len(extra_in), len(extra_out)

    def body(*refs):
        a_ref, b_ref = refs[0], refs[1]
        pos = 2
        r_ref = None
        if has_res:
            r_ref = refs[pos]
            pos += 1
        if has_buf:
            pos += 1
        x_refs = refs[pos:pos + n_xi]
        pos += n_xi
        first_rows = pl.program_id(0) == 0
        o_ref = refs[pos]
        y_refs = refs[pos + 1:pos + 1 + n_xo]
        if compute is None:
            av = a_ref[...]
            bv = b_ref[...]
            part = _dot(av.reshape(-1, av.shape[-1]), bv.reshape(-1, bv.shape[-1]), ta, tb)
        else:
            part = compute(a_ref, b_ref)

        def finish(val):
            if r_ref is not None:
                val = val + r_ref[...]
            if epilogue is not None:
                epilogue(val, x_refs, o_ref, y_refs, first_rows)
            else:
                o_ref[...] = val.reshape(o_ref.shape).astype(o_ref.dtype)

        if nk == 1:
            finish(part)
        elif in_place:
            @pl.when(pl.program_id(k_axis) == 0)
            def _():
                o_ref[...] = jnp.zeros(o_ref.shape, F32)

            o_ref[...] += part.reshape(o_ref.shape)
        else:
            acc_ref = refs[pos + 1 + n_xo]
            k = pl.program_id(k_axis)

            @pl.when(k == 0)
            def _():
                acc_ref[...] = jnp.zeros(acc_ref.shape, F32)

            acc_ref[...] += part

            @pl.when(k == nk - 1)
            def _():
                finish(acc_ref[...])

    ins, in_specs = [a, b], [a_spec, b_spec]
    if has_res:
        ins.append(res)
        in_specs.append(res_spec)
    aliases = {}
    if has_buf:
        aliases = {len(ins): 0}
        ins.append(buf)
        in_specs.append(pl.BlockSpec(memory_space=pl.ANY))
    for arr, sp in extra_in:
        ins.append(arr)
        in_specs.append(sp)
    scratch = [pltpu.VMEM(acc_shape, F32)] if nk > 1 and not in_place else []
    if not extra_out:
        return _call(body, name=name, grid=grid, in_specs=in_specs, out_specs=o_spec, out_shape=out_shape,
                     scratch_shapes=scratch, aliases=aliases, comm=comm, vmem=vmem)(*ins)
    return _call(body, name=name, grid=grid, in_specs=in_specs, out_specs=[o_spec] + [sp for _, sp in extra_out],
                 out_shape=[out_shape] + [sh for sh, _ in extra_out], scratch_shapes=scratch, aliases=aliases,
                 comm=comm, vmem=vmem)(*ins)


def _rms_fwd(name, x, g, tr=512):
    T = x.shape[0]

    def body(x_ref, g_ref, h_ref):
        xv = x_ref[...]
        r = lax.rsqrt(jnp.mean(xv * xv, axis=-1, keepdims=True) + RMS_EPS)
        h_ref[...] = (xv * r * g_ref[...]).astype(BF16)

    return pl.pallas_call(
        body, name=name, grid=(T // tr,),
        in_specs=[pl.BlockSpec((tr, D), lambda i: (i, 0)), pl.BlockSpec((1, D), lambda i: (0, 0))],
        out_specs=pl.BlockSpec((tr, D), lambda i: (i, 0)), out_shape=_sds((T, D), BF16),
        compiler_params=_params(1),
    )(x, g)


def _final_loss(x, g, tgt, tr=512):
    T = x.shape[0]

    def body(x_ref, g_ref, t_ref, dx_ref, dg_ref, sq_ref):
        xv = x_ref[...]
        r = lax.rsqrt(jnp.mean(xv * xv, axis=-1, keepdims=True) + RMS_EPS)
        xh = xv * r
        gv = g_ref[...]
        e = xh * gv - t_ref[...]
        dy = e * (1.0 / D)
        pg = _rows8(dy * xh)
        ps = _rows8(e * e)

        @pl.when(pl.program_id(0) == 0)
        def _():
            dg_ref[...] = pg
            sq_ref[...] = ps

        @pl.when(pl.program_id(0) > 0)
        def _():
            dg_ref[...] += pg
            sq_ref[...] += ps

        dxh = dy * gv
        dx_ref[...] = r * (dxh - xh * jnp.mean(dxh * xh, axis=-1, keepdims=True))

    row = pl.BlockSpec((tr, D), lambda i: (i, 0))
    acc = pl.BlockSpec((8, D), lambda i: (0, 0))
    return pl.pallas_call(
        body, name="final_loss", grid=(T // tr,),
        in_specs=[row, pl.BlockSpec((1, D), lambda i: (0, 0)), row],
        out_specs=[row, acc, acc],
        out_shape=[_sds((T, D), F32), _sds((8, D), F32), _sds((8, D), F32)],
        compiler_params=_params(1),
    )(x, g, tgt)


def _ple_bwd(name, dx, pe, pg, tr=512):
    T = dx.shape[0]

    def body(dx_ref, pe_ref, pg_ref, dpe_ref, dpg_ref):
        s = _sigmoid(pg_ref[...])
        dxv = dx_ref[...]
        dpe_ref[...] = (dxv * s).astype(BF16)
        dpg_ref[...] = (dxv * pe_ref[...] * s * (1.0 - s)).astype(BF16)

    row = pl.BlockSpec((tr, D), lambda i: (i, 0))
    return pl.pallas_call(
        body, name=name, grid=(T // tr,), in_specs=[row, row, row], out_specs=[row, row],
        out_shape=[_sds((T, D), BF16), _sds((T, D), BF16)], compiler_params=_params(1),
    )(dx, pe, pg)


def _gate_fwd(name, z, ya, yb, tr=512):
    T = z.shape[0]
    w = 512

    def body(ga_ref, gb_ref, ya_ref, yb_ref, o_ref):
        o_ref[...] = (_sigmoid(ga_ref[...]) * ya_ref[...].astype(F32)
                      + _sigmoid(gb_ref[...]) * yb_ref[...].astype(F32)).astype(BF16)

    col = pl.BlockSpec((tr, w), lambda i, j: (i, j))
    return pl.pallas_call(
        body, name=name, grid=(T // tr, D // w),
        in_specs=[pl.BlockSpec((tr, w), lambda i, j: (i, OFF_GA // w + j)),
                  pl.BlockSpec((tr, w), lambda i, j: (i, OFF_GB // w + j)), col, col],
        out_specs=col, out_shape=_sds((T, D), BF16), compiler_params=_params(2),
    )(z, z, ya, yb)


def _gate_bwd(name, z, off, y, dm, dz, tr=512):
    T = z.shape[0]
    w = 512
    has_dz = dz is not None

    def body(*refs):
        g_ref, y_ref, dm_ref = refs[:3]
        dy_ref, dz_ref = refs[-2:]
        s = _sigmoid(g_ref[...])
        dmv = dm_ref[...].astype(F32)
        dy_ref[...] = (dmv * s).astype(BF16)
        dz_ref[...] = (dmv * y_ref[...].astype(F32) * s * (1.0 - s)).astype(BF16)

    col = pl.BlockSpec((tr, w), lambda i, j: (i, j))
    gcol = pl.BlockSpec((tr, w), lambda i, j: (i, off // w + j))
    ins, in_specs, aliases = [z, y, dm], [gcol, col, col], {}
    if has_dz:
        ins.append(dz)
        in_specs.append(pl.BlockSpec(memory_space=pl.ANY))
        aliases = {3: 1}
    return pl.pallas_call(
        body, name=name, grid=(T // tr, D // w), in_specs=in_specs, out_specs=[col, gcol],
        out_shape=[_sds((T, D), BF16), _sds((T, ZW), BF16)], input_output_aliases=aliases,
        compiler_params=_params(2),
    )(*ins)


def _shift_down(v, k, rows):
    return jnp.where(rows >= k, pltpu.roll(v, k, 0), 0.0)


def _shift_up(v, k, rows):
    n = v.shape[0]
    return jnp.where(rows < n - k, pltpu.roll(v, n - k, 0), 0.0)


def _pool_window(v, g, rows, shift):
    s2 = v + shift(v, 1, rows)
    s4 = s2 + shift(s2, 2, rows)
    s8 = s4 + shift(s4, 4, rows)
    s16 = s8 + shift(s8, 8, rows)
    return jnp.where(g == 0, s2, jnp.where(g == 1, s4, jnp.where(g == 2, s8, s16)))


def _pool_count(g, rows):
    wlen = jnp.left_shift(2, g).astype(F32)
    return jnp.minimum(rows.astype(F32) + 1.0, wlen)


def _pool_fwd(name, z3, g_sc, scale):
    Bn = z3.shape[0]
    gw = 256

    def body(u_ref, pw_ref, sc_ref, pooled_ref, ms_ref):
        g = pl.program_id(1)
        u = u_ref[...]
        rows = lax.broadcasted_iota(jnp.int32, u.shape, 0)
        pooled = (_pool_window(u, g, rows, _shift_down) / _pool_count(g, rows) - u).astype(BF16)
        pooled_ref[...] = pooled
        pw = pw_ref[...].reshape(gw, gw)
        mixed = jnp.dot(pooled, pw, preferred_element_type=F32)
        ms_ref[...] = (mixed * sc_ref[...]).astype(BF16)

    blk = pl.BlockSpec((None, SEQ, gw), lambda b, g: (b, 0, g))
    return pl.pallas_call(
        body, name=name, grid=(Bn, 4),
        in_specs=[pl.BlockSpec((None, SEQ, gw), lambda b, g: (b, 0, OFF_U // gw + g)),
                  pl.BlockSpec((NCHIP, 64, gw), lambda b, g: (0, 12 + g, 0)),
                  pl.BlockSpec((1, gw), lambda b, g: (0, g))],
        out_specs=[blk, blk],
        out_shape=[_sds((Bn, SEQ, D), BF16), _sds((Bn, SEQ, D), BF16)],
        compiler_params=_params(2),
    )(z3, g_sc, scale)


def _pool_bwd(name, dms3, pooled3, g_sc, scale, dz3, gg_sc):
    Bn = dms3.shape[0]
    gw = 256
    has_gg = gg_sc is not None

    def body(*refs):
        dms_ref, pooled_ref, pw_ref, sc_ref = refs[:4]
        dz_ref, dpw_ref, dsc_ref = refs[-3:]
        g, b = pl.program_id(0), pl.program_id(1)
        pooled = pooled_ref[...]
        pw = pw_ref[...].reshape(gw, gw)
        dms = dms_ref[...]
        mixed = jnp.dot(pooled, pw, preferred_element_type=F32)
        psc = _rows8(dms * mixed)
        dmixed = (dms * sc_ref[...]).astype(BF16)
        dpw = lax.dot_general(pooled, dmixed, (((0,), (0,)), ((), ())), preferred_element_type=F32)
        dpw = dpw.reshape(NCHIP, 64, gw)

        @pl.when(b == 0)
        def _():
            dsc_ref[...] = psc
            dpw_ref[...] = dpw

        @pl.when(b > 0)
        def _():
            dsc_ref[...] += psc
            dpw_ref[...] += dpw

        dpooled = lax.dot_general(dmixed, pw, (((1,), (1,)), ((), ())), preferred_element_type=F32)
        rows = lax.broadcasted_iota(jnp.int32, dpooled.shape, 0)
        dq = dpooled / _pool_count(g, rows)
        dz_ref[...] = (_pool_window(dq, g, rows, _shift_up) - dpooled).astype(BF16)

    ins = [dms3, pooled3, g_sc, scale, dz3]
    in_specs = [pl.BlockSpec((None, SEQ, gw), lambda g, b: (b, 0, g)),
                pl.BlockSpec((None, SEQ, gw), lambda g, b: (b, 0, g)),
                pl.BlockSpec((NCHIP, 64, gw), lambda g, b: (0, 12 + g, 0)),
                pl.BlockSpec((1, gw), lambda g, b: (0, g)),
                pl.BlockSpec(memory_space=pl.ANY)]
    aliases = {4: 0}
    if has_gg:
        ins.append(gg_sc)
        in_specs.append(pl.BlockSpec(memory_space=pl.ANY))
        aliases[5] = 1
    return pl.pallas_call(
        body, name=name, grid=(4, Bn), in_specs=in_specs,
        out_specs=[pl.BlockSpec((None, SEQ, gw), lambda g, b: (b, 0, OFF_U // gw + g)),
                   pl.BlockSpec((NCHIP, 64, gw), lambda g, b: (0, 12 + g, 0)),
                   pl.BlockSpec((8, gw), lambda g, b: (0, g))],
        out_shape=[_sds(dz3.shape, BF16), _sds((NCHIP, D, 256), F32), _sds((8, D), F32)],
        input_output_aliases=aliases, compiler_params=_params(2),
    )(*ins)


CT = 256
NCT = FF // CT


def _conv_pre(u, cw_ref, cb_ref, rows):
    return (cb_ref[...] + cw_ref[0:1, :] * _shift_down(u, 2, rows) + cw_ref[1:2, :] * _shift_down(u, 1, rows)
            + cw_ref[2:3, :] * u)


def _conv_fwd(name, u3, cw, cb, comm=None):
    Bn = u3.shape[0]

    def body(ug_ref, uv_ref, cwg_ref, cwv_ref, cbg_ref, cbv_ref, a_ref, yg_ref, yv_ref):
        ug, uv = ug_ref[...], uv_ref[...]
        rows = lax.broadcasted_iota(jnp.int32, ug.shape, 0)
        yg = _conv_pre(ug, cwg_ref, cbg_ref, rows)
        yv = _conv_pre(uv, cwv_ref, cbv_ref, rows)
        yg_ref[...] = yg.astype(BF16)
        yv_ref[...] = yv.astype(BF16)
        a_ref[...] = (yg * _sigmoid(yg) * yv).astype(BF16)

    def blk(off):
        return pl.BlockSpec((None, SEQ, CT), lambda b, c: (b, 0, off + c))

    return _call(
        body, name=name, grid=(Bn, NCT),
        in_specs=[blk(0), blk(NCT),
                  pl.BlockSpec((3, CT), lambda b, c: (0, c)), pl.BlockSpec((3, CT), lambda b, c: (0, NCT + c)),
                  pl.BlockSpec((1, CT), lambda b, c: (0, c)), pl.BlockSpec((1, CT), lambda b, c: (0, NCT + c))],
        out_specs=[blk(0)] * 3,
        out_shape=[_sds((Bn, SEQ, FF), BF16)] * 3, comm=comm,
    )(u3, u3, cw, cw, cb, cb)


def _conv_bwd(name, da3, u3, yg3, yv3, cw, comm=None):
    Bn = u3.shape[0]
    last = NCT * Bn - 1
    R = 128

    def body(da_ref, ug_ref, uv_ref, yg_ref, yv_ref, cwg_ref, cwv_ref,
             du_ref, dcwg_ref, dcwv_ref, dcbg_ref, dcbv_ref, stage_g, stage_v, sems):
        c, b = pl.program_id(0), pl.program_id(1)
        step = c * Bn + b

        def writes(off_c, stage, sem):
            col = pl.multiple_of(off_c + c * CT, CT)
            return pltpu.make_async_copy(stage, du_ref.at[b, :, pl.ds(col, CT)], sem)

        @pl.when(step > 0)
        def _():
            writes(0, stage_g, sems.at[0]).wait()
            writes(FF, stage_v, sems.at[1]).wait()

        cwg, cwv = cwg_ref[...], cwv_ref[...]
        tail_rows = lax.broadcasted_iota(jnp.int32, (R, CT), 0)

        def chunk(c0, acc, at_end):
            n = R if at_end else R + 16
            yg, yv = yg_ref[pl.ds(c0, n), :].astype(F32), yv_ref[pl.ds(c0, n), :].astype(F32)
            da = da_ref[pl.ds(c0, n), :].astype(F32)
            s = _sigmoid(yg)
            dyv = da * (yg * s)
            dyg = da * yv * (s * (1.0 + yg * (1.0 - s)))
            out = []
            for dy, u_ref, cwt, stage in ((dyg, ug_ref, cwg, stage_g), (dyv, uv_ref, cwv, stage_v)):
                if at_end:
                    d0 = dy
                    d1 = jnp.where(tail_rows < R - 1, pltpu.roll(dy, R - 1, 0), 0.0)
                    d2 = jnp.where(tail_rows < R - 2, pltpu.roll(dy, R - 2, 0), 0.0)
                else:
                    d0 = dy[0:R, :]
                    d1 = pltpu.roll(dy, n - 1, 0)[0:R, :]
                    d2 = pltpu.roll(dy, n - 2, 0)[0:R, :]
                stage[pl.ds(c0, R), :] = (cwt[2:3, :] * d0 + cwt[1:2, :] * d1 + cwt[0:1, :] * d2).astype(BF16)
                u = u_ref[pl.ds(c0, R), :]
                out += [jnp.sum(d2 * u, axis=0, keepdims=True), jnp.sum(d1 * u, axis=0, keepdims=True),
                        jnp.sum(d0 * u, axis=0, keepdims=True), jnp.sum(d0, axis=0, keepdims=True)]
            return tuple(a + o for a, o in zip(acc, out))

        zero = jnp.zeros((1, CT), F32)
        acc = lax.fori_loop(0, SEQ // R - 1, lambda i, acc: chunk(pl.multiple_of(i * R, R), acc, False), (zero,) * 8)
        acc = chunk(SEQ - R, acc, True)
        for dcw_ref, dcb_ref, part in ((dcwg_ref, dcbg_ref, acc[0:4]), (dcwv_ref, dcbv_ref, acc[4:8])):
            dcw = jnp.concatenate(part[0:3], axis=0)

            @pl.when(b == 0)
            def _():
                dcw_ref[...] = dcw
                dcb_ref[...] = part[3]

            @pl.when(b > 0)
            def _():
                dcw_ref[...] += dcw
                dcb_ref[...] += part[3]

        writes(0, stage_g, sems.at[0]).start()
        writes(FF, stage_v, sems.at[1]).start()

        @pl.when(step == last)
        def _():
            writes(0, stage_g, sems.at[0]).wait()
            writes(FF, stage_v, sems.at[1]).wait()

    def blk(off):
        return pl.BlockSpec((None, SEQ, CT), lambda c, b: (b, 0, off + c))

    def vec(r, off):
        return pl.BlockSpec((r, CT), lambda c, b: (0, off + c))

    du3, dcwg, dcwv, dcbg, dcbv = _call(
        body, name=name, grid=(NCT, Bn),
        in_specs=[blk(0), blk(0), blk(NCT), blk(0), blk(0), vec(3, 0), vec(3, NCT)],
        out_specs=[pl.BlockSpec(memory_space=pl.ANY), vec(3, 0), vec(3, 0), vec(1, 0), vec(1, 0)],
        out_shape=[_sds((Bn, SEQ, UW), BF16), _sds((3, FF), F32), _sds((3, FF), F32), _sds((1, FF), F32),
                   _sds((1, FF), F32)],
        scratch_shapes=[pltpu.VMEM((SEQ, CT), BF16)] * 2 + [pltpu.SemaphoreType.DMA((2,))], comm=comm,
    )(da3, u3, u3, yg3, yv3, cw, cw)
    return du3, jnp.concatenate([dcwg, dcwv], axis=1), jnp.concatenate([dcbg, dcbv], axis=1)


def _rope_tables():
    pos = jnp.arange(SEQ, dtype=F32)
    inv_freq = jnp.exp(jnp.arange(0, ROPE_DIM, 2, dtype=F32) * (-math.log(ROPE_THETA) / ROPE_DIM))
    ang = pos[:, None] * inv_freq[None, :]
    cos, sin = jnp.cos(ang), jnp.sin(ang)
    half = ROPE_DIM // 2
    zeros = jnp.zeros((SEQ, HEAD - ROPE_DIM), F32)
    zh = jnp.zeros((SEQ, half), F32)
    tab_c = jnp.concatenate([cos, cos, zeros + 1.0], axis=1)
    tab_a = jnp.concatenate([-sin, zh, zeros], axis=1)
    tab_b = jnp.concatenate([zh, sin, zeros], axis=1)
    return tab_c, tab_a, tab_b


def _rot(v, tc, ta, tb):
    half = ROPE_DIM // 2
    return v * tc + pltpu.roll(v, HEAD - half, 1) * ta + pltpu.roll(v, half, 1) * tb


def _rot_t(dv, tc, ta, tb):
    half = ROPE_DIM // 2
    return dv * tc + pltpu.roll(dv * ta, half, 1) + pltpu.roll(dv * tb, HEAD - half, 1)


def _band_masks():
    qi = lax.broadcasted_iota(jnp.int32, (HEAD, 2 * HEAD), 0)
    ki = lax.broadcasted_iota(jnp.int32, (HEAD, 2 * HEAD), 1)
    diff = HEAD + qi - ki
    both = (diff >= 0) & (diff <= HEAD)
    q1 = lax.broadcasted_iota(jnp.int32, (HEAD, HEAD), 0)
    k1 = lax.broadcasted_iota(jnp.int32, (HEAD, HEAD), 1)
    return q1 >= k1, both


_NT = (((1,), (1,)), ((), ()))
_TN = (((0,), (0,)), ((), ()))
_SCALE = HEAD ** -0.5


ATT_W = HEAD
ATT_HP = ATT_W // HEAD


def _res_rows(r, n, d, base=0):
    return pl.ds(base * d + r, n, stride=d) if d > 1 else pl.ds(base, n)


def _attn_load(q_ref, k_ref, v_ref, tc_ref, ta_ref, tb_ref, qs, ks, vs, d):
    L = SEQ // d
    for r in range(d):
        rows = _res_rows(r, L, d)
        dst = slice(r * L, (r + 1) * L)
        tc, ta, tb = tc_ref[dst, :], ta_ref[dst, :], tb_ref[dst, :]
        for hh in range(ATT_HP):
            sl = slice(hh * HEAD, (hh + 1) * HEAD)
            qs[dst, sl] = _rot(q_ref[rows, sl], tc, ta, tb).astype(BF16)
            ks[dst, sl] = _rot(k_ref[rows, sl], tc, ta, tb).astype(BF16)
            vs[dst, sl] = v_ref[rows, sl].astype(BF16)


def _attn_fwd(name, z3, tabs, g, d, comm=None):
    Bn = z3.shape[0]
    L = SEQ // d
    nb = L // HEAD
    W, nh = ATT_W, GROUP_W // ATT_W

    def body(q_ref, k_ref, v_ref, tc_ref, ta_ref, tb_ref, o_ref, l_ref, qs, ks, vs, sc, pc):
        m_first, m_both = _band_masks()
        _attn_load(q_ref, k_ref, v_ref, tc_ref, ta_ref, tb_ref, qs, ks, vs, d)
        blocks = [(r, n) for r in range(d) for n in range(nb)]

        def spans(r, n):
            rq = slice(r * L + n * HEAD, r * L + (n + 1) * HEAD)
            rk = slice(r * L + max(n - 1, 0) * HEAD, r * L + (n + 1) * HEAD)
            return rq, rk, slice(0, HEAD if n == 0 else 2 * HEAD)

        for i, (r, n) in enumerate(blocks):
            rq, rk, kc = spans(r, n)
            sc[i, :, kc] = lax.dot_general(qs[rq, :], ks[rk, :], _NT, preferred_element_type=F32)
        for i, (r, n) in enumerate(blocks):
            rq, rk, kc = spans(r, n)
            s = jnp.where(m_first if n == 0 else m_both, sc[i, :, kc] * _SCALE, NEG_INF)
            m = jnp.max(s, axis=-1, keepdims=True)
            e = jnp.exp(s - m)
            den = jnp.sum(e, axis=-1, keepdims=True)
            pc[i, :, kc] = (e * (1.0 / den)).astype(BF16)
            l_ref[_res_rows(r, HEAD, d, n * HEAD), :] = jnp.broadcast_to(m + jnp.log(den), (HEAD, HEAD))
        for i, (r, n) in enumerate(blocks):
            rq, rk, kc = spans(r, n)
            o_ref[_res_rows(r, HEAD, d, n * HEAD), :] = jnp.dot(pc[i, :, kc], vs[rk, :], preferred_element_type=F32)

    def zcol(off):
        return pl.BlockSpec((None, SEQ, W), lambda b, h: (b, 0, (off + g * GROUP_W) // W + h))

    tab = pl.BlockSpec((SEQ, HEAD), lambda b, h: (0, 0))
    out = pl.BlockSpec((None, SEQ, W), lambda b, h: (b, 0, h))
    return _call(
        body, name=name, grid=(Bn, nh),
        in_specs=[zcol(0), zcol(OFF_K), zcol(OFF_V), tab, tab, tab],
        out_specs=[out, out],
        out_shape=[_sds((Bn, SEQ, GROUP_W), F32), _sds((Bn, SEQ, GROUP_W), F32)],
        scratch_shapes=[pltpu.VMEM((SEQ, W), BF16)] * 3
        + [pltpu.VMEM((SEQ // HEAD, HEAD, 2 * HEAD), F32), pltpu.VMEM((SEQ // HEAD, HEAD, 2 * HEAD), BF16)],
        comm=comm,
    )(z3, z3, z3, *tabs)


def _attn_bwd(name, z3, tabs, g, d, do3, lse3, delta3, dz3, comm=None):
    Bn = z3.shape[0]
    L = SEQ // d
    nb = L // HEAD
    W, nh = ATT_W, GROUP_W // ATT_W

    def body(q_ref, k_ref, v_ref, tc_ref, ta_ref, tb_ref, do_ref, l_ref, dl_ref, dz_in, dz_ref,
             qs, ks, vs, dos, dqs, dks, dvs, nat, oq, ok, ov, sc, dpc, pc, dsc, sems):
        b, h = pl.program_id(0), pl.program_id(1)
        m_first, m_both = _band_masks()
        _attn_load(q_ref, k_ref, v_ref, tc_ref, ta_ref, tb_ref, qs, ks, vs, d)
        for r in range(d):
            dos[r * L:(r + 1) * L, :] = do_ref[_res_rows(r, L, d), :].astype(BF16)
        dks[...] = jnp.zeros_like(dks)
        dvs[...] = jnp.zeros_like(dvs)
        blocks = [(r, n) for r in range(d) for n in range(nb)]

        def spans(r, n):
            rq = slice(r * L + n * HEAD, r * L + (n + 1) * HEAD)
            rk = slice(r * L + max(n - 1, 0) * HEAD, r * L + (n + 1) * HEAD)
            return rq, rk, slice(0, HEAD if n == 0 else 2 * HEAD)

        for i, (r, n) in enumerate(blocks):
            rq, rk, kc = spans(r, n)
            sc[i, :, kc] = lax.dot_general(qs[rq, :], ks[rk, :], _NT, preferred_element_type=F32)
            dpc[i, :, kc] = lax.dot_general(dos[rq, :], vs[rk, :], _NT, preferred_element_type=F32)
        for i, (r, n) in enumerate(blocks):
            rq, rk, kc = spans(r, n)
            rows = _res_rows(r, HEAD, d, n * HEAD)
            s = jnp.where(m_first if n == 0 else m_both, sc[i, :, kc] * _SCALE, NEG_INF)
            p = jnp.exp(s - l_ref[rows, :][:, 0:1])
            pc[i, :, kc] = p.astype(BF16)
            dsc[i, :, kc] = (p * (dpc[i, :, kc] - dl_ref[rows, :][:, 0:1]) * _SCALE).astype(BF16)
        for i, (r, n) in enumerate(blocks):
            rq, rk, kc = spans(r, n)
            dqs[rq, :] = jnp.dot(dsc[i, :, kc], ks[rk, :], preferred_element_type=F32)
        for i, (r, n) in enumerate(blocks):
            rq, rk, kc = spans(r, n)
            dks[rk, :] += lax.dot_general(dsc[i, :, kc], qs[rq, :], _TN, preferred_element_type=F32)
            dvs[rk, :] += lax.dot_general(pc[i, :, kc], dos[rq, :], _TN, preferred_element_type=F32)
        step = b * nh + h

        def writes():
            base = g * GROUP_W + h * W
            return [pltpu.make_async_copy(src, dz_ref.at[b, :, pl.ds(pl.multiple_of(base + off, HEAD), W)],
                                          sems.at[i])
                    for i, (src, off) in enumerate(((oq, 0), (ok, OFF_K), (ov, OFF_V)))]

        @pl.when(step > 0)
        def _():
            for cp in writes():
                cp.wait()

        for src, dst, rotate in ((dqs, oq, True), (dks, ok, True), (dvs, ov, False)):
            for r in range(d):
                val = src[r * L:(r + 1) * L, :]
                if rotate:
                    rm = slice(r * L, (r + 1) * L)
                    val = _rot_t(val, tc_ref[rm, :], ta_ref[rm, :], tb_ref[rm, :])
                nat[_res_rows(r, L, d), :] = val
            dst[...] = nat[...].astype(BF16)
        for cp in writes():
            cp.start()

        @pl.when(step == Bn * nh - 1)
        def _():
            for cp in writes():
                cp.wait()

    def zcol(off):
        return pl.BlockSpec((None, SEQ, W), lambda b, h: (b, 0, (off + g * GROUP_W) // W + h))

    tab = pl.BlockSpec((SEQ, HEAD), lambda b, h: (0, 0))
    gcol = pl.BlockSpec((None, SEQ, W), lambda b, h: (b, 0, h))
    any_spec = pl.BlockSpec(memory_space=pl.ANY)
    return _call(
        body, name=name, grid=(Bn, nh),
        in_specs=[zcol(0), zcol(OFF_K), zcol(OFF_V), tab, tab, tab, gcol, gcol, gcol, any_spec],
        out_specs=any_spec,
        out_shape=_sds((Bn, SEQ, ZW), BF16),
        scratch_shapes=[pltpu.VMEM((SEQ, W), BF16)] * 4 + [pltpu.VMEM((SEQ, W), F32)] * 4
        + [pltpu.VMEM((SEQ, W), BF16)] * 3
        + [pltpu.VMEM((SEQ // HEAD, HEAD, 2 * HEAD), F32)] * 2 + [pltpu.VMEM((SEQ // HEAD, HEAD, 2 * HEAD), BF16)] * 2
        + [pltpu.SemaphoreType.DMA((3,))],
        aliases={9: 0}, comm=comm,
    )(z3, z3, z3, *tabs, do3, lse3, delta3, dz3)


def _merge_weights(l0, l1, l2):
    m = jnp.maximum(jnp.maximum(l0, l1), l2)
    e0, e1, e2 = jnp.exp(l0 - m), jnp.exp(l1 - m), jnp.exp(l2 - m)
    inv = 1.0 / (e0 + e1 + e2)
    return e0 * inv, e1 * inv, e2 * inv


def _merge_fwd(name, outs, lses, tr=512):
    T = outs[0].shape[0]

    def body(o0, o1, o2, l0, l1, l2, a_ref):
        w0, w1, w2 = _merge_weights(l0[...], l1[...], l2[...])
        a_ref[...] = (w0 * o0[...] + w1 * o1[...] + w2 * o2[...]).astype(BF16)

    row = pl.BlockSpec((tr, GROUP_W), lambda i: (i, 0))
    return pl.pallas_call(
        body, name=name, grid=(T // tr,), in_specs=[row] * 6, out_specs=row,
        out_shape=_sds((T, GROUP_W), BF16), compiler_params=_params(1),
    )(*outs, *lses)


def _merge_bwd(name, outs, lses, dattn, tr=512):
    T = outs[0].shape[0]

    def body(o0, o1, o2, l0, l1, l2, da_ref, d0, d1, d2, e0, e1, e2):
        w = _merge_weights(l0[...], l1[...], l2[...])
        da = da_ref[...]
        attn = w[0] * o0[...] + w[1] * o1[...] + w[2] * o2[...]
        prod = da * attn
        csum = jnp.concatenate(
            [jnp.broadcast_to(jnp.sum(prod[:, hh * HEAD:(hh + 1) * HEAD], axis=-1, keepdims=True), (tr, HEAD))
             for hh in range(GROUP_W // HEAD)], axis=1)
        for wg, d_ref, e_ref in zip(w, (d0, d1, d2), (e0, e1, e2)):
            d_ref[...] = wg * da
            e_ref[...] = wg * csum

    row = pl.BlockSpec((tr, GROUP_W), lambda i: (i, 0))
    res = pl.pallas_call(
        body, name=name, grid=(T // tr,), in_specs=[row] * 7, out_specs=[row] * 6,
        out_shape=[_sds((T, GROUP_W), F32)] * 6,
        compiler_params=_params(1),
    )(*outs, *lses, dattn)
    return res[:3], res[3:]


def _rms_rows(xv, g):
    r = lax.rsqrt(jnp.mean(xv * xv, axis=-1, keepdims=True) + RMS_EPS)
    return (xv * r * g).astype(BF16)


def _epi_norm(val, x_refs, o_ref, y_refs, first_rows):
    o_ref[...] = val
    y_refs[0][...] = _rms_rows(val, x_refs[0][...])


def _epi_ple(val, x_refs, o_ref, y_refs, first_rows):
    o_ref[...] = val
    xn = x_refs[0][...] + x_refs[1][...] * _sigmoid(val)
    y_refs[0][...] = xn
    if len(y_refs) > 1:
        y_refs[1][...] = _rms_rows(xn, x_refs[2][...])


def _epi_norm_bwd(val, x_refs, o_ref, y_refs, first_rows):
    xv = x_refs[0][...]
    r = lax.rsqrt(jnp.mean(xv * xv, axis=-1, keepdims=True) + RMS_EPS)
    xh = xv * r
    part = _rows8(val * xh)

    @pl.when(first_rows)
    def _():
        y_refs[0][...] = part

    @pl.when(jnp.logical_not(first_rows))
    def _():
        y_refs[0][...] += part

    dxh = val * x_refs[2][...]
    o_ref[...] = x_refs[1][...] + r * (dxh - xh * jnp.mean(dxh * xh, axis=-1, keepdims=True))


def _local_step(x3, p4, tgt3, vecs, ex):
    Bn = x3.shape[0]
    T = Bn * SEQ
    x = x3.reshape(T, D)
    tgt = tgt3.reshape(T, D)
    pb = p4.astype(BF16).reshape(DEPTH, T, PLE)
    tabs = {}
    for d in DILATIONS:
        tabs[d] = [t.reshape(SEQ // d, d, HEAD).transpose(1, 0, 2).reshape(SEQ, HEAD) for t in _rope_tables()]
    tm = 1024 if T % 1024 == 0 else 512
    nt = T // tm
    tk = 1024 if T % 1024 == 0 else 512
    ntk = T // tk
    tm5 = 512
    f32o = lambda n: _sds((T, n), F32)

    def spec(shape, fn):
        return pl.BlockSpec(shape, fn)

    def _mm(name, *args, **kwargs):
        return _mm_call(name, *args, comm=ex.hook(name), **kwargs)

    def cols4(a_ref, b_ref):
        av = a_ref[...]
        return jnp.concatenate([_dot(av, b_ref[j]) for j in range(NCHIP)], axis=1)

    def rows4(a_ref, b_ref):
        av = a_ref[...]
        return jnp.concatenate([_dot(av, b_ref[:, j * 256:(j + 1) * 256], ta=True) for j in range(NCHIP)], axis=0)

    def kchunks4(a_ref, b_ref):
        total = _dot(a_ref[:, 0:256], b_ref[0], tb=True)
        for j in range(1, NCHIP):
            total = total + _dot(a_ref[:, j * 256:(j + 1) * 256], b_ref[j], tb=True)
        return total

    row5 = spec((tm5, D), lambda i, *_: (i, 0))
    rowm = spec((tm, D), lambda i, *_: (i, 0))
    gain = spec((1, D), lambda *_: (0, 0))
    bf_rows = (_sds((T, D), BF16), row5)

    saved = []
    h = _rms_fwd("rms_mix0", x, vecs["g_mix"][0:1])
    for l in range(DEPTH):
        L = str(l)
        G = ex.weights(l)
        g_mix, g_ffn, g_ple = (vecs[k][l:l + 1] for k in ("g_mix", "g_ffn", "g_ple"))
        pscale, cb, cw = vecs["pool_scale"][l:l + 1], vecs["conv_b"][l:l + 1], vecs["conv_w"][l]
        z = _mm("mm_z" + L, h, G["in"], grid=(nt, NCHIP),
                a_spec=spec((tm, D), lambda i, n: (i, 0)),
                b_spec=spec((None, D, IN_S), lambda i, n: (n, 0, 0)),
                o_spec=spec((tm, IN_S), lambda i, n: (i, n)), out_shape=f32o(ZW))
        z3 = z.reshape(Bn, SEQ, ZW)
        outs, lses = [], []
        for g, d in enumerate(DILATIONS):
            o_g, l_g = _attn_fwd("attn_fwd%d_%d" % (g, l), z3, tabs[d], g, d, comm=ex.hook("attn_fwd%d_%d" % (g, l)))
            outs.append(o_g.reshape(T, GROUP_W))
            lses.append(l_g.reshape(T, GROUP_W))
        attn = _merge_fwd("merge_fwd" + L, outs, lses)
        ya = _mm("mm_ya" + L, attn, G["sc"], grid=(nt,), compute=cols4,
                 a_spec=spec((tm, GROUP_W), lambda i: (i, 0)),
                 b_spec=spec((NCHIP, GROUP_W, 256), lambda i: (0, 0, 0)),
                 o_spec=spec((tm, D), lambda i: (i, 0)), out_shape=_sds((T, D), BF16))
        pooled3, ms3 = _pool_fwd("pool_fwd" + L, z3, G["sc"], pscale)
        ms = ms3.reshape(T, D)

        def row_sharded(name, a, rb, res=None, kdim=D, out=F32, rows=tm5, **fused):
            if rb is None:
                b_arr, b_spec = G["dn"], spec((NCHIP, DN_S, D), lambda i: (0, 0, 0))
            else:
                b_arr, b_spec = G["r3"], spec((NCHIP, 256, D), lambda i: (0, rb, 0))
            tile = spec((rows, D), lambda i, *_: (i, 0))
            return _mm(name, a, b_arr, grid=(T // rows,),
                       a_spec=spec((rows, kdim), lambda i: (i, 0)), b_spec=b_spec,
                       o_spec=tile, out_shape=_sds((T, D), out), res=res, res_spec=None if res is None else tile,
                       **fused)

        yb = row_sharded("mm_yb" + L, ms, 0, out=BF16, rows=tm)
        merged = _gate_fwd("gate_fwd" + L, z, ya, yb)
        x1, h2 = row_sharded("mm_o" + L, merged, 1, res=x, epilogue=_epi_norm, extra_in=[(g_ffn, gain)],
                             extra_out=[bf_rows])
        u = _mm("mm_up" + L, h2, G["up"], grid=(nt, NCHIP),
                a_spec=spec((tm, D), lambda i, n: (i, 0)),
                b_spec=spec((None, D, UP_S), lambda i, n: (n, 0, 0)),
                o_spec=spec((tm, UP_S), lambda i, n: (i, n)), out_shape=f32o(UW))
        u3 = u.reshape(Bn, SEQ, UW)
        act3, yg3, yv3 = _conv_fwd("conv_fwd" + L, u3, cw, cb, comm=ex.hook("conv_fwd" + L))
        act = act3.reshape(T, FF)
        x2, h3 = row_sharded("mm_down" + L, act, None, res=x1, kdim=FF, epilogue=_epi_norm,
                             extra_in=[(g_ple, gain)], extra_out=[bf_rows])
        pe = _mm("mm_pe" + L, pb[l], G["sc"], grid=(nt,), compute=cols4,
                 a_spec=spec((tm, PLE), lambda i: (i, 0)),
                 b_spec=spec((NCHIP, 256, 256), lambda i: (0, 2, 0)),
                 o_spec=spec((tm, D), lambda i: (i, 0)), out_shape=f32o(D))
        fused_in = [(x2, row5), (pe, row5)]
        fused_out = [(f32o(D), row5)]
        if l + 1 < DEPTH:
            fused_in.append((vecs["g_mix"][l + 1:l + 2], gain))
            fused_out.append(bf_rows)
        pg, x3n, *h_next = row_sharded("mm_pg" + L, h3, 2, epilogue=_epi_ple, extra_in=fused_in,
                                       extra_out=fused_out)
        saved.append(dict(x=x, h=h, z=z, outs=outs, lses=lses, attn=attn, ya=ya, yb=yb, pooled3=pooled3, ms=ms,
                          merged=merged, x1=x1, h2=h2, u3=u3, yg3=yg3, yv3=yv3, act=act, x2=x2, h3=h3, pg=pg, pe=pe))
        x = x3n
        h = h_next[0] if h_next else None

    dx, dg_final8, sq8 = _final_loss(x, vecs["g_final"].reshape(1, D), tgt)

    gg_shape = {k: _sds(G[k].shape, F32) for k in G}
    small = {"g_final": dg_final8}

    for l in reversed(range(DEPTH)):
        L = str(l)
        sv = saved[l]
        G = ex.weights(l)
        GG = dict.fromkeys(_KEYS)
        g_mix, g_ffn, g_ple = (vecs[k][l:l + 1] for k in ("g_mix", "g_ffn", "g_ple"))
        pscale, cb, cw = vecs["pool_scale"][l:l + 1], vecs["conv_b"][l:l + 1], vecs["conv_w"][l]

        def wgrad_rows(name, a, b_arr, key, rb):
            GG[key] = _mm(name, a, b_arr, grid=(ntk,), ta=True, k_axis=0, nk=ntk, acc_shape=(D, D),
                          a_spec=spec((tk, D), lambda k: (k, 0)),
                          b_spec=spec((tk, D), lambda k: (k, 0)),
                          o_spec=spec((NCHIP, 256, D), lambda k: (0, rb, 0)),
                          out_shape=gg_shape[key], buf=GG[key])

        def dgrad_rows(name, dy, rb, out=F32, rows=tm5, **fused):
            return _mm(name, dy, G["r3"], grid=(T // rows,), tb=True,
                       a_spec=spec((rows, D), lambda i: (i, 0)),
                       b_spec=spec((NCHIP, 256, D), lambda i: (0, rb, 0)),
                       o_spec=spec((rows, D), lambda i, *_: (i, 0)), out_shape=_sds((T, D), out), **fused)

        def norm_bwd(xin, dres, g, rows=row5):
            return dict(epilogue=_epi_norm_bwd, extra_in=[(xin, rows), (dres, rows), (g, gain)],
                        extra_out=[(_sds((8, D), F32), spec((8, D), lambda *_: (0, 0)))])

        dpe, dpg = _ple_bwd("ple_bwd" + L, dx, sv["pe"], sv["pg"])
        GG["sc"] = _mm("wg_ple" + L, pb[l], dpe, grid=(ntk,), compute=rows4, k_axis=0, nk=ntk,
                       acc_shape=(NCHIP * PLE, 256),
                       a_spec=spec((tk, PLE), lambda k: (k, 0)), b_spec=spec((tk, D), lambda k: (k, 0)),
                       o_spec=spec((NCHIP, 256, 256), lambda k: (0, 2, 0)),
                       out_shape=gg_shape["sc"], buf=GG["sc"])
        wgrad_rows("wg_pg" + L, sv["h3"], dpg, "r3", 2)
        dx, small["g_ple" + L] = dgrad_rows("dg_pg" + L, dpg, 2, **norm_bwd(sv["x2"], dx, g_ple))

        da = _mm("dg_down" + L, dx, G["dn"], grid=(T // 256,), tb=True,
                 a_spec=spec((256, D), lambda i: (i, 0)),
                 b_spec=spec((NCHIP, DN_S, D), lambda i: (0, 0, 0)),
                 o_spec=spec((256, FF), lambda i: (i, 0)), out_shape=_sds((T, FF), BF16))
        GG["dn"] = _mm("wg_down" + L, sv["act"], dx, grid=(2, ntk), ta=True, k_axis=1, nk=ntk,
                       acc_shape=(FF, 512),
                       a_spec=spec((tk, FF), lambda n, k: (k, 0)), b_spec=spec((tk, 512), lambda n, k: (k, n)),
                       o_spec=spec((NCHIP, DN_S, 512), lambda n, k: (0, 0, n)),
                       out_shape=gg_shape["dn"], buf=GG["dn"])
        du3, dcw, dcb = _conv_bwd("conv_bwd" + L, da.reshape(Bn, SEQ, FF), sv["u3"], sv["yg3"], sv["yv3"], cw,
                                  comm=ex.hook("conv_bwd" + L))
        small["conv_w" + L], small["conv_b" + L] = dcw, dcb
        du = du3.reshape(T, UW)
        dx, small["g_ffn" + L] = _mm(
            "dg_up" + L, du, G["up"], grid=(nt, NCHIP), tb=True, k_axis=1, nk=NCHIP, acc_shape=(tm, D),
            a_spec=spec((tm, UP_S), lambda i, k: (i, k)), b_spec=spec((None, D, UP_S), lambda i, k: (k, 0, 0)),
            o_spec=rowm, out_shape=f32o(D), vmem=VMEM_BIG, **norm_bwd(sv["x1"], dx, g_ffn, rowm))
        GG["up"] = _mm("wg_up" + L, sv["h2"], du, grid=(NCHIP, ntk), ta=True, k_axis=1, nk=ntk,
                       acc_shape=(D, UP_S),
                       a_spec=spec((tk, D), lambda j, k: (k, 0)),
                       b_spec=spec((tk, UP_S), lambda j, k: (k, j)),
                       o_spec=spec((None, D, UP_S), lambda j, k: (j, 0, 0)),
                       out_shape=gg_shape["up"], buf=GG["up"])

        dmerged = dgrad_rows("dg_o" + L, dx, 1, out=BF16, rows=tm)
        wgrad_rows("wg_o" + L, sv["merged"], dx, "r3", 1)
        dya, dz = _gate_bwd("gate_bwd_a" + L, sv["z"], OFF_GA, sv["ya"], dmerged, None)
        dyb, dz = _gate_bwd("gate_bwd_b" + L, sv["z"], OFF_GB, sv["yb"], dmerged, dz)
        dms = dgrad_rows("dg_yb" + L, dyb, 0, rows=tm)
        wgrad_rows("wg_yb" + L, sv["ms"], dyb, "r3", 0)
        dz3, GG["sc"], small["pool_scale" + L] = _pool_bwd(
            "pool_bwd" + L, dms.reshape(Bn, SEQ, D), sv["pooled3"], G["sc"], pscale,
            dz.reshape(Bn, SEQ, ZW), GG["sc"])
        dattn = _mm("dg_ya" + L, dya, G["sc"], grid=(nt,), compute=kchunks4,
                    a_spec=spec((tm, D), lambda i: (i, 0)),
                    b_spec=spec((NCHIP, GROUP_W, 256), lambda i: (0, 0, 0)),
                    o_spec=spec((tm, GROUP_W), lambda i: (i, 0)), out_shape=f32o(GROUP_W))
        GG["sc"] = _mm("wg_ya" + L, sv["attn"], dya, grid=(ntk,), compute=rows4, k_axis=0, nk=ntk,
                       acc_shape=(NCHIP * GROUP_W, 256),
                       a_spec=spec((tk, GROUP_W), lambda k: (k, 0)), b_spec=spec((tk, D), lambda k: (k, 0)),
                       o_spec=spec((NCHIP, GROUP_W, 256), lambda k: (0, 0, 0)),
                       out_shape=gg_shape["sc"], buf=GG["sc"])
        ex.grads_ready(l, {k: GG[k] for k in _KEYS[1:]})
        dos, deltas = _merge_bwd("merge_bwd" + L, sv["outs"], sv["lses"], dattn)
        view3 = lambda t: t.reshape(Bn, SEQ, GROUP_W)
        sz3 = sv["z"].reshape(Bn, SEQ, ZW)
        for g, d in enumerate(DILATIONS):
            dz3 = _attn_bwd("attn_bwd%d_%d" % (g, l), sz3, tabs[d], g, d, view3(dos[g]), view3(sv["lses"][g]),
                            view3(deltas[g]), dz3, comm=ex.hook("attn_bwd%d_%d" % (g, l)))
        dz = dz3.reshape(T, ZW)
        GG["in"] = _mm("wg_z" + L, sv["h"], dz, grid=(NCHIP, ntk), ta=True, k_axis=1, nk=ntk,
                       acc_shape=(D, IN_S),
                       a_spec=spec((tk, D), lambda n, k: (k, 0)), b_spec=spec((tk, IN_S), lambda n, k: (k, n)),
                       o_spec=spec((None, D, IN_S), lambda n, k: (n, 0, 0)),
                       out_shape=gg_shape["in"], buf=GG["in"])
        ex.grads_ready(l, {"in": GG["in"]})

        def dgrad_z(name, first, count, dres, buf):
            rows = spec((tm, D), lambda i, *_: (i + first, 0))
            return _mm(name, dz, G["in"], grid=(count, NCHIP), tb=True, k_axis=1, nk=NCHIP, acc_shape=(tm, D),
                       a_spec=spec((tm, IN_S), lambda i, k: (i + first, k)),
                       b_spec=spec((None, D, IN_S), lambda i, k: (k, 0, 0)),
                       o_spec=rows, out_shape=f32o(D), vmem=VMEM_BIG, buf=buf, **norm_bwd(sv["x"], dres, g_mix, rows))

        if l == 0 and nt % 2 == 0:
            dx_a, part_a = dgrad_z("dg_z0a", 0, nt // 2, dx, None)
            dx, part_b = dgrad_z("dg_z0b", nt // 2, nt // 2, dx, dx_a)
            small["g_mix" + L] = part_a + part_b
        else:
            dx, small["g_mix" + L] = dgrad_z("dg_z" + L, 0, nt, dx, None)

    return sq8, dx.reshape(Bn, SEQ, D), small


_ANY = pl.BlockSpec(memory_space=pl.ANY)


def _place():
    x, y, c = lax.axis_index("x"), lax.axis_index("y"), lax.axis_index("c")
    chips = [(1 - x, y), (x, 1 - y), (1 - x, 1 - y)]
    return x, y, c, 2 * x + y, chips


def _half(rows, cc):
    return pl.ds(cc * (rows // 2), rows // 2)


def _remote(src, dst, send_sems, recv_sems, i, to):
    return pltpu.make_async_remote_copy(src_ref=src, dst_ref=dst, send_sem=send_sems.at[i], recv_sem=recv_sems.at[i],
                                        device_id=to, device_id_type=MESH)


def _exchange_gather_ici(stacks, done):
    n = len(stacks)
    rows = [t.shape[1] for t in stacks]

    def start(refs, send_sems, recv_sems):
        x, y, c, me, chips = _place()
        for k in range(n):
            part = refs[k].at[me, _half(rows[k], c)]
            for j, chip in enumerate(chips):
                _remote(part, part, send_sems, recv_sems, 3 * k + j, (*chip, c)).start()

    def wait(refs, send_sems, recv_sems):
        x, y, c, me, chips = _place()
        for k in range(n):
            for j, chip in enumerate(chips):
                part = refs[k].at[2 * chip[0] + chip[1], _half(rows[k], c)]
                _remote(part, part, send_sems, recv_sems, 3 * k + j, (*chip, c)).wait()

    return dict(arrays=list(stacks), nsem=3 * n, start=start, wait=wait, done=done)


def _exchange_gather_d2d(stacks, done):
    n = len(stacks)
    rows = [t.shape[1] for t in stacks]

    def copies(refs, send_sems, recv_sems, mine):
        x, y, c, me, chips = _place()
        cc = c if mine else 1 - c
        return [_remote(part, part, send_sems, recv_sems, 3 * k + j, (x, y, 1 - c))
                for k in range(n) for j, chip in enumerate(chips)
                for part in [refs[k].at[2 * chip[0] + chip[1], _half(rows[k], cc)]]]

    def start(refs, send_sems, recv_sems):
        for cp in copies(refs, send_sems, recv_sems, True):
            cp.start()

    def wait(refs, send_sems, recv_sems):
        for cp in copies(refs, send_sems, recv_sems, False):
            cp.wait()

    return dict(arrays=list(stacks), nsem=3 * n, start=start, wait=wait, done=done)


def _exchange_halves(g5, recv, done):
    n = len(g5)

    def copies(refs, send_sems, recv_sems):
        x, y, c, me, chips = _place()
        return [_remote(refs[k].at[:, 1 - c], refs[n + k], send_sems, recv_sems, k, (x, y, 1 - c)) for k in range(n)]

    def start(refs, send_sems, recv_sems):
        for cp in copies(refs, send_sems, recv_sems):
            cp.start()

    def wait(refs, send_sems, recv_sems):
        for cp in copies(refs, send_sems, recv_sems):
            cp.wait()

    return dict(arrays=list(g5) + list(recv), nsem=n, start=start, wait=wait, done=done)


def _exchange_chips(parts, landing, done):
    n = len(parts)

    def start(refs, send_sems, recv_sems):
        x, y, c, me, chips = _place()
        for k in range(n):
            for j, chip in enumerate(chips):
                _remote(refs[k].at[2 * chip[0] + chip[1]], refs[n + k].at[me], send_sems, recv_sems, 3 * k + j,
                        (*chip, c)).start()

    def wait(refs, send_sems, recv_sems):
        x, y, c, me, chips = _place()
        for k in range(n):
            for j, chip in enumerate(chips):
                slot = refs[n + k].at[2 * chip[0] + chip[1]]
                _remote(slot, slot, send_sems, recv_sems, 3 * k + j, (*chip, c)).wait()

    return dict(arrays=list(parts) + list(landing), nsem=3 * n, start=start, wait=wait, done=done)


def _exchange_share(full, layer, done):
    n = len(full)

    def copies(refs, send_sems, recv_sems, mine):
        x, y, c, me, chips = _place()
        cc = c if mine else 1 - c
        return [_remote(part, part, send_sems, recv_sems, k, (x, y, 1 - c))
                for k in range(n) for part in [refs[k].at[layer, cc]]]

    def start(refs, send_sems, recv_sems):
        for cp in copies(refs, send_sems, recv_sems, True):
            cp.start()

    def wait(refs, send_sems, recv_sems):
        for cp in copies(refs, send_sems, recv_sems, False):
            cp.wait()

    return dict(arrays=list(full), nsem=n, start=start, wait=wait, done=done)


def _exchange_call(name, comm):
    arrays = comm["arrays"]
    n = len(arrays)

    def body(*refs):
        outs, send_sems, recv_sems = refs[n:2 * n], refs[2 * n], refs[2 * n + 1]
        comm["start"](outs, send_sems, recv_sems)
        comm["wait"](outs, send_sems, recv_sems)

    outs = pl.pallas_call(
        body, name=name, in_specs=[_ANY] * n, out_specs=[_ANY] * n,
        out_shape=[_sds(t.shape, t.dtype) for t in arrays],
        scratch_shapes=[pltpu.SemaphoreType.DMA((comm["nsem"],))] * 2,
        input_output_aliases={i: i for i in range(n)},
    )(*arrays)
    comm["done"](outs)


def _gather_first(stacks, cw4):
    n = len(stacks)
    ici = _exchange_gather_ici(stacks, None)
    d2d = _exchange_gather_d2d(stacks, None)
    rows = [t.shape[1] for t in stacks]

    def body(*refs):
        g_refs, cwg_ref = refs[n + 1:2 * n + 1], refs[2 * n + 1]
        s_ici, r_ici, s_d2d, r_d2d, s_cw, r_cw = refs[2 * n + 2:]
        x, y, c, me, chips = _place()

        def cw_copy(j, slot, chip):
            part = cwg_ref.at[slot]
            return _remote(part, part, s_cw, r_cw, j, (*chip, c))

        ici["start"](g_refs, s_ici, r_ici)
        for j, chip in enumerate(chips):
            cw_copy(j, me, chip).start()
        for k in range(n):
            for j, chip in enumerate(chips):
                part = g_refs[k].at[2 * chip[0] + chip[1], _half(rows[k], c)]
                _remote(part, part, s_ici, r_ici, 3 * k + j, (*chip, c)).wait()
                _remote(part, part, s_d2d, r_d2d, 3 * k + j, (x, y, 1 - c)).start()
        d2d["wait"](g_refs, s_d2d, r_d2d)
        for j, chip in enumerate(chips):
            cw_copy(j, 2 * chip[0] + chip[1], chip).wait()

    outs = pl.pallas_call(
        body, name="gather_first", in_specs=[_ANY] * (n + 1), out_specs=[_ANY] * (n + 1),
        out_shape=[_sds(t.shape, t.dtype) for t in stacks] + [_sds(cw4.shape, cw4.dtype)],
        scratch_shapes=[pltpu.SemaphoreType.DMA((3 * n,))] * 4 + [pltpu.SemaphoreType.DMA((3,))] * 2,
        input_output_aliases={i: i for i in range(n + 1)},
    )(*stacks, cw4)
    return outs[:n], outs[n]


def _small_allreduce(small):
    def body(small_ref, red_ref, gath, s_send, s_recv):
        x, y, c, me, chips = _place()
        dev = 4 * x + 2 * y + c
        gath[dev] = small_ref[...]
        for r in range(1, 8):
            peer = (x ^ (r >> 2), y ^ ((r >> 1) & 1), c ^ (r & 1))
            _remote(small_ref, gath.at[dev], s_send, s_recv, r - 1, peer).start()
        for r in range(1, 8):
            peer = (x ^ (r >> 2), y ^ ((r >> 1) & 1), c ^ (r & 1))
            src = 4 * peer[0] + 2 * peer[1] + peer[2]
            _remote(small_ref, gath.at[src], s_send, s_recv, r - 1, peer).wait()
        total = gath[0]
        for i in range(1, 8):
            total = total + gath[i]
        red_ref[...] = total

    vm = pl.BlockSpec(memory_space=pltpu.VMEM)
    return pl.pallas_call(
        body, name="small_allreduce", in_specs=[vm], out_specs=vm, out_shape=_sds(small.shape, F32),
        scratch_shapes=[pltpu.VMEM((8,) + small.shape, F32), pltpu.SemaphoreType.DMA((7,)),
                        pltpu.SemaphoreType.DMA((7,))],
    )(small)


def _row_tile(rh):
    for cand in (512, 384, 352, 256, 128):
        if rh % cand == 0:
            return cand
    return rh


def _add_halves(name, g5, recv, place):
    _, _, rh, cols = g5.shape
    tr = _row_tile(rh)

    def body(place_ref, g_ref, r_ref, o_ref, own_ref):
        val = (g_ref[...] + r_ref[...]).astype(BF16)
        o_ref[...] = val

        @pl.when(pl.program_id(1) == place_ref[1])
        def _():
            own_ref[...] = val

    grid_spec = pltpu.PrefetchScalarGridSpec(
        num_scalar_prefetch=1, grid=(rh // tr, NCHIP),
        in_specs=[pl.BlockSpec((None, None, tr, cols), lambda i, j, pr: (j, pr[0], i, 0)),
                  pl.BlockSpec((None, tr, cols), lambda i, j, pr: (j, i, 0))],
        out_specs=[pl.BlockSpec((None, tr, cols), lambda i, j, pr: (j, i, 0)),
                   pl.BlockSpec((None, tr, cols), lambda i, j, pr: (pr[1], i, 0))])
    return pl.pallas_call(
        body, name=name, grid_spec=grid_spec, out_shape=[_sds(recv.shape, BF16)] * 2, compiler_params=_params(2),
    )(place, g5, recv)


def _sum_chips(name, landing, place, layer, full):
    _, rh, cols = landing.shape
    tr = _row_tile(rh)
    has_full = full is not None

    def body(*refs):
        r_ref, o_ref = refs[1], refs[-1]
        total = r_ref[0].astype(F32)
        for j in range(1, NCHIP):
            total = total + r_ref[j].astype(F32)
        o_ref[...] = total

    grid_spec = pltpu.PrefetchScalarGridSpec(
        num_scalar_prefetch=1, grid=(rh // tr,),
        in_specs=[pl.BlockSpec((NCHIP, tr, cols), lambda i, pr: (0, i, 0))] + ([_ANY] if has_full else []),
        out_specs=pl.BlockSpec((None, None, tr, cols), lambda i, pr: (layer, pr[0], i, 0)))
    return pl.pallas_call(
        body, name=name, grid_spec=grid_spec, out_shape=_sds((DEPTH, 2, rh, cols), F32),
        input_output_aliases={2: 0} if has_full else {}, compiler_params=_params(1),
    )(place, landing, *([full] if has_full else []))


class _Schedule:
    FIRST, REST = ["in"], list(_KEYS[1:])

    def __init__(self, slotted, cw4, place):
        self.place = place
        self._w = [dict(zip(_KEYS, layer)) for layer in slotted]
        got, self.cw4 = _gather_first([self._w[0][k] for k in self.FIRST], cw4)
        self._w[0].update(zip(self.FIRST, got))
        self._g5, self._recv, self._parts, self._landing = [{}, {}], [{}, {}], [{}, {}], [{}, {}]
        self.full = {}
        every = list(_KEYS)
        self._hooks = {
            "mm_z0": lambda: self._gather(_exchange_gather_ici, 0, self.REST),
            "attn_fwd0_0": lambda: self._gather(_exchange_gather_d2d, 0, self.REST),
            "mm_up0": lambda: self._gather(_exchange_gather_ici, 1, self.REST),
            "conv_fwd0": lambda: self._gather(_exchange_gather_ici, 1, self.FIRST),
            "mm_down0": lambda: self._gather(_exchange_gather_d2d, 1, every),
            "dg_down0": lambda: self._halves(1, every),
            "conv_bwd0": lambda: self._chips(1, every),
            "dg_up0": lambda: self._share(1, every),
            "attn_bwd0_0": lambda: self._halves(0, self.REST),
            "wg_z0": lambda: self._chips(0, self.REST),
            "dg_z0a": lambda: self._halves(0, self.FIRST),
            "dg_z0b": lambda: self._chips(0, self.FIRST),
        }

    def weights(self, layer):
        return self._w[layer]

    def hook(self, name):
        make = self._hooks.get(name)
        return make() if make else None

    def grads_ready(self, layer, GG):
        for k, t in GG.items():
            g5 = t.reshape(NCHIP, 2, t.shape[1] // 2, t.shape[2])
            self._g5[layer][k] = g5
            self._recv[layer][k] = lax.empty((NCHIP,) + g5.shape[2:], F32)

    def _gather(self, make, layer, keys):
        def done(arrays):
            self._w[layer].update(zip(keys, arrays))
        return make([self._w[layer][k] for k in keys], done)

    def _halves(self, layer, keys):
        return _exchange_halves([self._g5[layer][k] for k in keys], [self._recv[layer][k] for k in keys],
                                lambda arrays: self._halves_done(layer, keys, arrays))

    def _halves_done(self, layer, keys, arrays):
        n = len(keys)
        for k, g, r in zip(keys, arrays[:n], arrays[n:]):
            self._parts[layer][k], self._landing[layer][k] = _add_halves(
                "add_halves%d_%s" % (layer, k), g, r, self.place)

    def _chips(self, layer, keys):
        return _exchange_chips([self._parts[layer][k] for k in keys], [self._landing[layer][k] for k in keys],
                               lambda arrays: self._chips_done(layer, keys, arrays))

    def _chips_done(self, layer, keys, arrays):
        for k, t in zip(keys, arrays[len(keys):]):
            self.full[k] = _sum_chips("sum_chips%d_%s" % (layer, k), t, self.place, layer, self.full.get(k))

    def _share(self, layer, keys):
        def done(arrays):
            self.full.update(zip(keys, arrays))
        return _exchange_share([self.full[k] for k in keys], layer, done)

    def finish(self, small):
        small_red = _small_allreduce(small)
        _exchange_call("share_halves_last", self._share(0, list(_KEYS)))
        return [self.full[k] for k in _KEYS], small_red


def _adamw(name, w, g, m, v):
    shape = w.shape
    cols = shape[-1]
    rows = 1
    for s in shape[:-1]:
        rows *= s
    tr = rows
    for cand in (256, 128, 64):
        if rows > cand and rows % cand == 0:
            tr = cand
            break
    c1 = 1.0 / (1.0 - B1 ** STEP)
    c2 = 1.0 / (1.0 - B2 ** STEP)

    def body(w_ref, g_ref, m_ref, v_ref, go_ref, d_ref, nm_ref, nv_ref):
        gv = g_ref[...]
        go_ref[...] = gv
        nm = B1 * m_ref[...] + (1.0 - B1) * gv
        nv = B2 * v_ref[...] + (1.0 - B2) * (gv * gv)
        nm_ref[...] = nm
        nv_ref[...] = nv
        d_ref[...] = -LR * ((nm * c1) / (jnp.sqrt(nv * c2) + ADAM_EPS) + WD * w_ref[...])

    blk = pl.BlockSpec((tr, cols), lambda i: (i, 0))
    outs = pl.pallas_call(
        body, name=name, grid=(rows // tr,), in_specs=[blk] * 4, out_specs=[blk] * 4,
        out_shape=[_sds((rows, cols), F32)] * 4, compiler_params=_params(1),
    )(*(t.reshape(rows, cols) for t in (w, g, m, v)))
    return tuple(o.reshape(shape) for o in outs)


def _pack_small(small):
    rows = [jnp.sum(small["g_mix%d" % l], axis=0, keepdims=True) for l in range(DEPTH)]
    rows += [jnp.sum(small["pool_scale%d" % l], axis=0, keepdims=True) for l in range(DEPTH)]
    rows += [jnp.sum(small["g_ffn%d" % l], axis=0, keepdims=True) for l in range(DEPTH)]
    rows += [jnp.sum(small["g_ple%d" % l], axis=0, keepdims=True) for l in range(DEPTH)]
    rows += [jnp.sum(small["g_final"], axis=0, keepdims=True)]
    flat = [small["conv_b%d" % l].reshape(-1) for l in range(DEPTH)]
    flat += [small["conv_w%d" % l].reshape(-1) for l in range(DEPTH)]
    flat = jnp.concatenate(flat).reshape(-1, D)
    packed = jnp.concatenate(rows + [flat], axis=0)
    return jnp.pad(packed, ((0, SMALL_ROWS - packed.shape[0]), (0, 0)))


def _unpack_small(red):
    g_mix, pool_scale, g_ffn, g_ple = red[0:2], red[2:4], red[4:6], red[6:8]
    g_final = red[8]
    nb = DEPTH * UW // D
    conv_b = red[9:9 + nb].reshape(DEPTH, UW)
    conv_w = red[9 + nb:9 + 4 * nb].reshape(DEPTH, 3, UW)
    return g_mix, pool_scale, g_ffn, g_ple, g_final, conv_b, conv_w


def kernel(x, p, g_mix, w_in, w_ya, w_yb, pool_w, pool_scale, w_o, g_ffn, w_up, conv_w, conv_b, w_down, g_ple, w_ple, w_ple_gate, g_final, loss_target, m_g_mix, m_w_in, m_w_ya, m_w_yb, m_pool_w, m_pool_scale, m_w_o, m_g_ffn, m_w_up, m_conv_w, m_conv_b, m_w_down, m_g_ple, m_w_ple, m_w_ple_gate, m_g_final, v_g_mix, v_w_in, v_w_ya, v_w_yb, v_pool_w, v_pool_scale, v_w_o, v_g_ffn, v_w_up, v_conv_w, v_conv_b, v_w_down, v_g_ple, v_w_ple, v_w_ple_gate, v_g_final):
    me = 2 * lax.axis_index("x") + lax.axis_index("y")
    place = jnp.stack([lax.axis_index("c"), me]).astype(jnp.int32)

    def slot(shard):
        return lax.dynamic_update_index_in_dim(lax.empty((NCHIP,) + shard.shape, shard.dtype), shard, me, 0)

    packed = [
        w_in.astype(BF16), w_up.astype(BF16),
        jnp.concatenate([w_ya, w_ple, pool_w.reshape(DEPTH, 256, 256)], axis=1).astype(BF16),
        jnp.concatenate([w_yb, w_o, w_ple_gate], axis=1).astype(BF16),
        w_down.astype(BF16),
    ]
    slotted = [[slot(t[l]) for t in packed] for l in range(DEPTH)]
    ex = _Schedule(slotted, slot(conv_w.reshape(DEPTH * 3, UP_S)), place)
    cw_full = ex.cw4.reshape(NCHIP, DEPTH, 3, UP_S).transpose(1, 2, 0, 3).reshape(DEPTH, 3, UW)

    vecs = dict(g_mix=g_mix, pool_scale=pool_scale, g_ffn=g_ffn, g_ple=g_ple, g_final=g_final, conv_b=conv_b,
                conv_w=cw_full)
    sq8, grad_x, small = _local_step(x, p, loss_target, vecs, ex)
    loss = lax.psum(jnp.sum(sq8) * (0.5 / D), ("x", "y", "c"))

    full, small_red = ex.finish(_pack_small(small))
    r_in, r_up, r_sc, r_r3, r_dn = [f.reshape(DEPTH, -1, f.shape[-1]) for f in full]
    d_g_mix, d_pool_scale, d_g_ffn, d_g_ple, d_g_final, d_conv_b, d_conv_w_full = _unpack_small(small_red)
    d_conv_w = lax.dynamic_slice_in_dim(d_conv_w_full, me * UP_S, UP_S, axis=2)

    grads = dict(
        g_mix=d_g_mix, w_in=r_in, w_ya=r_sc[:, 0:512], w_yb=r_r3[:, 0:256],
        pool_w=r_sc[:, 768:1024].reshape(DEPTH, 4, 64, 256), pool_scale=d_pool_scale, w_o=r_r3[:, 256:512],
        g_ffn=d_g_ffn, w_up=r_up, conv_w=d_conv_w, conv_b=d_conv_b, w_down=r_dn, g_ple=d_g_ple,
        w_ple=r_sc[:, 512:768], w_ple_gate=r_r3[:, 512:768], g_final=d_g_final)
    weights = dict(g_mix=g_mix, w_in=w_in, w_ya=w_ya, w_yb=w_yb, pool_w=pool_w, pool_scale=pool_scale, w_o=w_o,
                   g_ffn=g_ffn, w_up=w_up, conv_w=conv_w, conv_b=conv_b, w_down=w_down, g_ple=g_ple, w_ple=w_ple,
                   w_ple_gate=w_ple_gate, g_final=g_final)
    m_in = dict(g_mix=m_g_mix, w_in=m_w_in, w_ya=m_w_ya, w_yb=m_w_yb, pool_w=m_pool_w, pool_scale=m_pool_scale,
                w_o=m_w_o, g_ffn=m_g_ffn, w_up=m_w_up, conv_w=m_conv_w, conv_b=m_conv_b, w_down=m_w_down,
                g_ple=m_g_ple, w_ple=m_w_ple, w_ple_gate=m_w_ple_gate, g_final=m_g_final)
    v_in = dict(g_mix=v_g_mix, w_in=v_w_in, w_ya=v_w_ya, w_yb=v_w_yb, pool_w=v_pool_w, pool_scale=v_pool_scale,
                w_o=v_w_o, g_ffn=v_g_ffn, w_up=v_w_up, conv_w=v_conv_w, conv_b=v_conv_b, w_down=v_w_down,
                g_ple=v_g_ple, w_ple=v_w_ple, w_ple_gate=v_w_ple_gate, g_final=v_g_final)
    names = ["g_mix", "w_in", "w_ya", "w_yb", "pool_w", "pool_scale", "w_o", "g_ffn", "w_up", "conv_w", "conv_b",
             "w_down", "g_ple", "w_ple", "w_ple_gate", "g_final"]
    deltas, new_m, new_v = [], [], []
    for nme in names:
        gr = grads[nme].reshape(weights[nme].shape)
        grads[nme], dlt, nm, nv = _adamw("adamw_" + nme, weights[nme], gr, m_in[nme], v_in[nme])
        deltas.append(dlt)
        new_m.append(nm)
        new_v.append(nv)
    return (loss, grad_x, *[grads[nme] for nme in names], *deltas, *new_m, *new_v)
```

```python
import math

import jax
import jax.numpy as jnp
from jax import lax
from jax.experimental import pallas as pl
from jax.experimental.pallas import tpu as pltpu

F32 = jnp.float32
BF16 = jnp.bfloat16
_KEYS = ("in", "up", "sc", "r3", "dn")
MESH = pl.DeviceIdType.MESH

D = 1024
SEQ = 2048
DEPTH = 2
HEAD = 128
GROUP_W = 512
DILATIONS = (1, 4, 16)
ROPE_DIM = 32
ROPE_THETA = 500000.0
NEG_INF = -1e30
ZW = 7680
OFF_K, OFF_V, OFF_U, OFF_GA, OFF_GB = 1536, 3072, 4608, 5632, 6656
FF = 2816
UW = 2 * FF
PLE = 256
NCHIP = 4
IN_S, UP_S, DN_S = ZW // NCHIP, UW // NCHIP, FF // NCHIP
RMS_EPS = 1e-6
LR, B1, B2, ADAM_EPS, WD, STEP = 0.001, 0.9, 0.999, 1e-08, 0.01, 10
SMALL_ROWS = 56
VMEM_CAP = 48 * 1024 * 1024
VMEM_BIG = 58 * 1024 * 1024


def _params(n_grid, vmem=VMEM_CAP):
    return pltpu.CompilerParams(dimension_semantics=("arbitrary",) * n_grid, vmem_limit_bytes=vmem)


def _sigmoid(v):
    return 1.0 / (1.0 + jnp.exp(-v))


def _rows8(v):
    return jnp.sum(v.reshape(v.shape[0] // 8, 8, v.shape[1]), axis=0)


def _sds(shape, dtype):
    return jax.ShapeDtypeStruct(shape, dtype)


def _dot(av, bv, ta=False, tb=False):
    dims = (((0,) if ta else (1,), (1,) if tb else (0,)), ((), ()))
    return lax.dot_general(av.astype(BF16), bv.astype(BF16), dims, preferred_element_type=F32)


def _call(body, *, name, grid, in_specs, out_specs, out_shape, scratch_shapes=(), aliases=None, comm=None,
          vmem=VMEM_CAP):
    params = _params(len(grid), vmem)
    aliases = dict(aliases or {})
    if comm is None:
        return pl.pallas_call(body, name=name, grid=grid, in_specs=list(in_specs), out_specs=out_specs,
                              out_shape=out_shape, scratch_shapes=list(scratch_shapes),
                              input_output_aliases=aliases, compiler_params=params)
    single = not isinstance(out_shape, (list, tuple))
    out_specs_l = [out_specs] if single else list(out_specs)
    out_shape_l = [out_shape] if single else list(out_shape)
    n_in, n_out, n_c = len(in_specs), len(out_shape_l), len(comm["arrays"])

    def hosted(*refs):
        core_in, core_out = refs[:n_in], refs[n_in + n_c:n_in + n_c + n_out]
        c_refs = refs[n_in + n_c + n_out:n_in + 2 * n_c + n_out]
        scratch, (send_sems, recv_sems) = refs[n_in + 2 * n_c + n_out:-2], refs[-2:]
        ids = [pl.program_id(i) for i in range(len(grid))]
        first, last = ids[0] == 0, ids[0] == grid[0] - 1
        for i in range(1, len(grid)):
            first = jnp.logical_and(first, ids[i] == 0)
            last = jnp.logical_and(last, ids[i] == grid[i] - 1)

        @pl.when(first)
        def _():
            comm["start"](c_refs, send_sems, recv_sems)

        body(*core_in, *core_out, *scratch)

        @pl.when(last)
        def _():
            comm["wait"](c_refs, send_sems, recv_sems)

    any_spec = pl.BlockSpec(memory_space=pl.ANY)
    for i in range(n_c):
        aliases[n_in + i] = n_out + i
    call = pl.pallas_call(
        hosted, name=name, grid=grid, in_specs=list(in_specs) + [any_spec] * n_c,
        out_specs=out_specs_l + [any_spec] * n_c,
        out_shape=out_shape_l + [_sds(t.shape, t.dtype) for t in comm["arrays"]],
        scratch_shapes=list(scratch_shapes) + [pltpu.SemaphoreType.DMA((comm["nsem"],))] * 2,
        input_output_aliases=aliases, compiler_params=params)

    def run(*args):
        outs = call(*args, *comm["arrays"])
        comm["done"](outs[n_out:])
        return outs[0] if single else outs[:n_out]

    return run


def _mm_call(name, a, b, *, grid, a_spec, b_spec, o_spec, out_shape, ta=False, tb=False, k_axis=None, nk=1,
             acc_shape=None, res=None, res_spec=None, buf=None, compute=None, comm=None,
             extra_in=(), extra_out=(), epilogue=None, vmem=VMEM_CAP):
    has_res, has_buf = res is not None, buf is not None
    in_place = nk > 1 and not has_res and out_shape.dtype == F32 and epilogue is None
    n_xi, n_xo = len(extra_in), len(extra_out)

    def body(*refs):
        a_ref, b_ref = refs[0], refs[1]
        pos = 2
        r_ref = None
        if has_res:
            r_ref = refs[pos]
            pos += 1
        if has_buf:
            pos += 1
        x_refs = refs[pos:pos + n_xi]
        pos += n_xi
        first_rows = pl.program_id(0) == 0
        o_ref = refs[pos]
        y_refs = refs[pos + 1:pos + 1 + n_xo]
        if compute is None:
            av = a_ref[...]
            bv = b_ref[...]
            part = _dot(av.reshape(-1, av.shape[-1]), bv.reshape(-1, bv.shape[-1]), ta, tb)
        else:
            part = compute(a_ref, b_ref)

        def finish(val):
            if r_ref is not None:
                val = val + r_ref[...]
            if epilogue is not None:
                epilogue(val, x_refs, o_ref, y_refs, first_rows)
            else:
                o_ref[...] = val.reshape(o_ref.shape).astype(o_ref.dtype)

        if nk == 1:
            finish(part)
        elif in_place:
            @pl.when(pl.program_id(k_axis) == 0)
            def _():
                o_ref[...] = jnp.zeros(o_ref.shape, F32)

            o_ref[...] += part.reshape(o_ref.shape)
        else:
            acc_ref = refs[pos + 1 + n_xo]
            k = pl.program_id(k_axis)

            @pl.when(k == 0)
            def _():
                acc_ref[...] = jnp.zeros(acc_ref.shape, F32)

            acc_ref[...] += part

            @pl.when(k == nk - 1)
            def _():
                finish(acc_ref[...])

    ins, in_specs = [a, b], [a_spec, b_spec]
    if has_res:
        ins.append(res)
        in_specs.append(res_spec)
    aliases = {}
    if has_buf:
        aliases = {len(ins): 0}
        ins.append(buf)
        in_specs.append(pl.BlockSpec(memory_space=pl.ANY))
    for arr, sp in extra_in:
        ins.append(arr)
        in_specs.append(sp)
    scratch = [pltpu.VMEM(acc_shape, F32)] if nk > 1 and not in_place else []
    if not extra_out:
        return _call(body, name=name, grid=grid, in_specs=in_specs, out_specs=o_spec, out_shape=out_shape,
                     scratch_shapes=scratch, aliases=aliases, comm=comm, vmem=vmem)(*ins)
    return _call(body, name=name, grid=grid, in_specs=in_specs, out_specs=[o_spec] + [sp for _, sp in extra_out],
                 out_shape=[out_shape] + [sh for sh, _ in extra_out], scratch_shapes=scratch, aliases=aliases,
                 comm=comm, vmem=vmem)(*ins)


def _rms_fwd(name, x, g, tr=512):
    T = x.shape[0]

    def body(x_ref, g_ref, h_ref):
        xv = x_ref[...]
        r = lax.rsqrt(jnp.mean(xv * xv, axis=-1, keepdims=True) + RMS_EPS)
        h_ref[...] = (xv * r * g_ref[...]).astype(BF16)

    return pl.pallas_call(
        body, name=name, grid=(T // tr,),
        in_specs=[pl.BlockSpec((tr, D), lambda i: (i, 0)), pl.BlockSpec((1, D), lambda i: (0, 0))],
        out_specs=pl.BlockSpec((tr, D), lambda i: (i, 0)), out_shape=_sds((T, D), BF16),
        compiler_params=_params(1),
    )(x, g)


def _final_loss(x, g, tgt, tr=512):
    T = x.shape[0]

    def body(x_ref, g_ref, t_ref, dx_ref, dg_ref, sq_ref):
        xv = x_ref[...]
        r = lax.rsqrt(jnp.mean(xv * xv, axis=-1, keepdims=True) + RMS_EPS)
        xh = xv * r
        gv = g_ref[...]
        e = xh * gv - t_ref[...]
        dy = e * (1.0 / D)
        pg = _rows8(dy * xh)
        ps = _rows8(e * e)

        @pl.when(pl.program_id(0) == 0)
        def _():
            dg_ref[...] = pg
            sq_ref[...] = ps

        @pl.when(pl.program_id(0) > 0)
        def _():
            dg_ref[...] += pg
            sq_ref[...] += ps

        dxh = dy * gv
        dx_ref[...] = r * (dxh - xh * jnp.mean(dxh * xh, axis=-1, keepdims=True))

    row = pl.BlockSpec((tr, D), lambda i: (i, 0))
    acc = pl.BlockSpec((8, D), lambda i: (0, 0))
    return pl.pallas_call(
        body, name="final_loss", grid=(T // tr,),
        in_specs=[row, pl.BlockSpec((1, D), lambda i: (0, 0)), row],
        out_specs=[row, acc, acc],
        out_shape=[_sds((T, D), F32), _sds((8, D), F32), _sds((8, D), F32)],
        compiler_params=_params(1),
    )(x, g, tgt)


def _ple_bwd(name, dx, pe, pg, tr=512):
    T = dx.shape[0]

    def body(dx_ref, pe_ref, pg_ref, dpe_ref, dpg_ref):
        s = _sigmoid(pg_ref[...])
        dxv = dx_ref[...]
        dpe_ref[...] = (dxv * s).astype(BF16)
        dpg_ref[...] = (dxv * pe_ref[...] * s * (1.0 - s)).astype(BF16)

    row = pl.BlockSpec((tr, D), lambda i: (i, 0))
    return pl.pallas_call(
        body, name=name, grid=(T // tr,), in_specs=[row, row, row], out_specs=[row, row],
        out_shape=[_sds((T, D), BF16), _sds((T, D), BF16)], compiler_params=_params(1),
    )(dx, pe, pg)


def _gate_fwd(name, z, ya, yb, tr=512):
    T = z.shape[0]
    w = 512

    def body(ga_ref, gb_ref, ya_ref, yb_ref, o_ref):
        o_ref[...] = (_sigmoid(ga_ref[...]) * ya_ref[...].astype(F32)
                      + _sigmoid(gb_ref[...]) * yb_ref[...].astype(F32)).astype(BF16)

    col = pl.BlockSpec((tr, w), lambda i, j: (i, j))
    return pl.pallas_call(
        body, name=name, grid=(T // tr, D // w),
        in_specs=[pl.BlockSpec((tr, w), lambda i, j: (i, OFF_GA // w + j)),
                  pl.BlockSpec((tr, w), lambda i, j: (i, OFF_GB // w + j)), col, col],
        out_specs=col, out_shape=_sds((T, D), BF16), compiler_params=_params(2),
    )(z, z, ya, yb)


def _gate_bwd(name, z, off, y, dm, dz, tr=512):
    T = z.shape[0]
    w = 512
    has_dz = dz is not None

    def body(*refs):
        g_ref, y_ref, dm_ref = refs[:3]
        dy_ref, dz_ref = refs[-2:]
        s = _sigmoid(g_ref[...])
        dmv = dm_ref[...].astype(F32)
        dy_ref[...] = (dmv * s).astype(BF16)
        dz_ref[...] = (dmv * y_ref[...].astype(F32) * s * (1.0 - s)).astype(BF16)

    col = pl.BlockSpec((tr, w), lambda i, j: (i, j))
    gcol = pl.BlockSpec((tr, w), lambda i, j: (i, off // w + j))
    ins, in_specs, aliases = [z, y, dm], [gcol, col, col], {}
    if has_dz:
        ins.append(dz)
        in_specs.append(pl.BlockSpec(memory_space=pl.ANY))
        aliases = {3: 1}
    return pl.pallas_call(
        body, name=name, grid=(T // tr, D // w), in_specs=in_specs, out_specs=[col, gcol],
        out_shape=[_sds((T, D), BF16), _sds((T, ZW), BF16)], input_output_aliases=aliases,
        compiler_params=_params(2),
    )(*ins)


def _shift_down(v, k, rows):
    return jnp.where(rows >= k, pltpu.roll(v, k, 0), 0.0)


def _shift_up(v, k, rows):
    n = v.shape[0]
    return jnp.where(rows < n - k, pltpu.roll(v, n - k, 0), 0.0)


def _pool_window(v, g, rows, shift):
    s2 = v + shift(v, 1, rows)
    s4 = s2 + shift(s2, 2, rows)
    s8 = s4 + shift(s4, 4, rows)
    s16 = s8 + shift(s8, 8, rows)
    return jnp.where(g == 0, s2, jnp.where(g == 1, s4, jnp.where(g == 2, s8, s16)))


def _pool_count(g, rows):
    wlen = jnp.left_shift(2, g).astype(F32)
    return jnp.minimum(rows.astype(F32) + 1.0, wlen)


def _pool_fwd(name, z3, g_sc, scale):
    Bn = z3.shape[0]
    gw = 256

    def body(u_ref, pw_ref, sc_ref, pooled_ref, ms_ref):
        g = pl.program_id(1)
        u = u_ref[...]
        rows = lax.broadcasted_iota(jnp.int32, u.shape, 0)
        pooled = (_pool_window(u, g, rows, _shift_down) / _pool_count(g, rows) - u).astype(BF16)
        pooled_ref[...] = pooled
        pw = pw_ref[...].reshape(gw, gw)
        mixed = jnp.dot(pooled, pw, preferred_element_type=F32)
        ms_ref[...] = (mixed * sc_ref[...]).astype(BF16)

    blk = pl.BlockSpec((None, SEQ, gw), lambda b, g: (b, 0, g))
    return pl.pallas_call(
        body, name=name, grid=(Bn, 4),
        in_specs=[pl.BlockSpec((None, SEQ, gw), lambda b, g: (b, 0, OFF_U // gw + g)),
                  pl.BlockSpec((NCHIP, 64, gw), lambda b, g: (0, 12 + g, 0)),
                  pl.BlockSpec((1, gw), lambda b, g: (0, g))],
        out_specs=[blk, blk],
        out_shape=[_sds((Bn, SEQ, D), BF16), _sds((Bn, SEQ, D), BF16)],
        compiler_params=_params(2),
    )(z3, g_sc, scale)


def _pool_bwd(name, dms3, pooled3, g_sc, scale, dz3, gg_sc):
    Bn = dms3.shape[0]
    gw = 256
    has_gg = gg_sc is not None

    def body(*refs):
        dms_ref, pooled_ref, pw_ref, sc_ref = refs[:4]
        dz_ref, dpw_ref, dsc_ref = refs[-3:]
        g, b = pl.program_id(0), pl.program_id(1)
        pooled = pooled_ref[...]
        pw = pw_ref[...].reshape(gw, gw)
        dms = dms_ref[...]
        mixed = jnp.dot(pooled, pw, preferred_element_type=F32)
        psc = _rows8(dms * mixed)
        dmixed = (dms * sc_ref[...]).astype(BF16)
        dpw = lax.dot_general(pooled, dmixed, (((0,), (0,)), ((), ())), preferred_element_type=F32)
        dpw = dpw.reshape(NCHIP, 64, gw)

        @pl.when(b == 0)
        def _():
            dsc_ref[...] = psc
            dpw_ref[...] = dpw

        @pl.when(b > 0)
        def _():
            dsc_ref[...] += psc
            dpw_ref[...] += dpw

        dpooled = lax.dot_general(dmixed, pw, (((1,), (1,)), ((), ())), preferred_element_type=F32)
        rows = lax.broadcasted_iota(jnp.int32, dpooled.shape, 0)
        dq = dpooled / _pool_count(g, rows)
        dz_ref[...] = (_pool_window(dq, g, rows, _shift_up) - dpooled).astype(BF16)

    ins = [dms3, pooled3, g_sc, scale, dz3]
    in_specs = [pl.BlockSpec((None, SEQ, gw), lambda g, b: (b, 0, g)),
                pl.BlockSpec((None, SEQ, gw), lambda g, b: (b, 0, g)),
                pl.BlockSpec((NCHIP, 64, gw), lambda g, b: (0, 12 + g, 0)),
                pl.BlockSpec((1, gw), lambda g, b: (0, g)),
                pl.BlockSpec(memory_space=pl.ANY)]
    aliases = {4: 0}
    if has_gg:
        ins.append(gg_sc)
        in_specs.append(pl.BlockSpec(memory_space=pl.ANY))
        aliases[5] = 1
    return pl.pallas_call(
        body, name=name, grid=(4, Bn), in_specs=in_specs,
        out_specs=[pl.BlockSpec((None, SEQ, gw), lambda g, b: (b, 0, OFF_U // gw + g)),
                   pl.BlockSpec((NCHIP, 64, gw), lambda g, b: (0, 12 + g, 0)),
                   pl.BlockSpec((8, gw), lambda g, b: (0, g))],
        out_shape=[_sds(dz3.shape, BF16), _sds((NCHIP, D, 256), F32), _sds((8, D), F32)],
        input_output_aliases=aliases, compiler_params=_params(2),
    )(*ins)


CT = 256
NCT = FF // CT


def _conv_pre(u, cw_ref, cb_ref, rows):
    return (cb_ref[...] + cw_ref[0:1, :] * _shift_down(u, 2, rows) + cw_ref[1:2, :] * _shift_down(u, 1, rows)
            + cw_ref[2:3, :] * u)


def _conv_fwd(name, u3, cw, cb, comm=None):
    Bn = u3.shape[0]

    def body(ug_ref, uv_ref, cwg_ref, cwv_ref, cbg_ref, cbv_ref, a_ref, yg_ref, yv_ref):
        ug, uv = ug_ref[...], uv_ref[...]
        rows = lax.broadcasted_iota(jnp.int32, ug.shape, 0)
        yg = _conv_pre(ug, cwg_ref, cbg_ref, rows)
        yv = _conv_pre(uv, cwv_ref, cbv_ref, rows)
        yg_ref[...] = yg.astype(BF16)
        yv_ref[...] = yv.astype(BF16)
        a_ref[...] = (yg * _sigmoid(yg) * yv).astype(BF16)

    def blk(off):
        return pl.BlockSpec((None, SEQ, CT), lambda b, c: (b, 0, off + c))

    return _call(
        body, name=name, grid=(Bn, NCT),
        in_specs=[blk(0), blk(NCT),
                  pl.BlockSpec((3, CT), lambda b, c: (0, c)), pl.BlockSpec((3, CT), lambda b, c: (0, NCT + c)),
                  pl.BlockSpec((1, CT), lambda b, c: (0, c)), pl.BlockSpec((1, CT), lambda b, c: (0, NCT + c))],
        out_specs=[blk(0)] * 3,
        out_shape=[_sds((Bn, SEQ, FF), BF16)] * 3, comm=comm,
    )(u3, u3, cw, cw, cb, cb)


def _conv_bwd(name, da3, u3, yg3, yv3, cw, comm=None):
    Bn = u3.shape[0]
    last = NCT * Bn - 1
    R = 128

    def body(da_ref, ug_ref, uv_ref, yg_ref, yv_ref, cwg_ref, cwv_ref,
             du_ref, dcwg_ref, dcwv_ref, dcbg_ref, dcbv_ref, stage_g, stage_v, sems):
        c, b = pl.program_id(0), pl.program_id(1)
        step = c * Bn + b

        def writes(off_c, stage, sem):
            col = pl.multiple_of(off_c + c * CT, CT)
            return pltpu.make_async_copy(stage, du_ref.at[b, :, pl.ds(col, CT)], sem)

        @pl.when(step > 0)
        def _():
            writes(0, stage_g, sems.at[0]).wait()
            writes(FF, stage_v, sems.at[1]).wait()

        cwg, cwv = cwg_ref[...], cwv_ref[...]
        tail_rows = lax.broadcasted_iota(jnp.int32, (R, CT), 0)

        def chunk(c0, acc, at_end):
            n = R if at_end else R + 16
            yg, yv = yg_ref[pl.ds(c0, n), :].astype(F32), yv_ref[pl.ds(c0, n), :].astype(F32)
            da = da_ref[pl.ds(c0, n), :].astype(F32)
            s = _sigmoid(yg)
            dyv = da * (yg * s)
            dyg = da * yv * (s * (1.0 + yg * (1.0 - s)))
            out = []
            for dy, u_ref, cwt, stage in ((dyg, ug_ref, cwg, stage_g), (dyv, uv_ref, cwv, stage_v)):
                if at_end:
                    d0 = dy
                    d1 = jnp.where(tail_rows < R - 1, pltpu.roll(dy, R - 1, 0), 0.0)
                    d2 = jnp.where(tail_rows < R - 2, pltpu.roll(dy, R - 2, 0), 0.0)
                else:
                    d0 = dy[0:R, :]
                    d1 = pltpu.roll(dy, n - 1, 0)[0:R, :]
                    d2 = pltpu.roll(dy, n - 2, 0)[0:R, :]
                stage[pl.ds(c0, R), :] = (cwt[2:3, :] * d0 + cwt[1:2, :] * d1 + cwt[0:1, :] * d2).astype(BF16)
                u = u_ref[pl.ds(c0, R), :]
                out += [jnp.sum(d2 * u, axis=0, keepdims=True), jnp.sum(d1 * u, axis=0, keepdims=True),
                        jnp.sum(d0 * u, axis=0, keepdims=True), jnp.sum(d0, axis=0, keepdims=True)]
            return tuple(a + o for a, o in zip(acc, out))

        zero = jnp.zeros((1, CT), F32)
        acc = lax.fori_loop(0, SEQ // R - 1, lambda i, acc: chunk(pl.multiple_of(i * R, R), acc, False), (zero,) * 8)
        acc = chunk(SEQ - R, acc, True)
        for dcw_ref, dcb_ref, part in ((dcwg_ref, dcbg_ref, acc[0:4]), (dcwv_ref, dcbv_ref, acc[4:8])):
            dcw = jnp.concatenate(part[0:3], axis=0)

            @pl.when(b == 0)
            def _():
                dcw_ref[...] = dcw
                dcb_ref[...] = part[3]

            @pl.when(b > 0)
            def _():
                dcw_ref[...] += dcw
                dcb_ref[...] += part[3]

        writes(0, stage_g, sems.at[0]).start()
        writes(FF, stage_v, sems.at[1]).start()

        @pl.when(step == last)
        def _():
            writes(0, stage_g, sems.at[0]).wait()
            writes(FF, stage_v, sems.at[1]).wait()

    def blk(off):
        return pl.BlockSpec((None, SEQ, CT), lambda c, b: (b, 0, off + c))

    def vec(r, off):
        return pl.BlockSpec((r, CT), lambda c, b: (0, off + c))

    du3, dcwg, dcwv, dcbg, dcbv = _call(
        body, name=name, grid=(NCT, Bn),
        in_specs=[blk(0), blk(0), blk(NCT), blk(0), blk(0), vec(3, 0), vec(3, NCT)],
        out_specs=[pl.BlockSpec(memory_space=pl.ANY), vec(3, 0), vec(3, 0), vec(1, 0), vec(1, 0)],
        out_shape=[_sds((Bn, SEQ, UW), BF16), _sds((3, FF), F32), _sds((3, FF), F32), _sds((1, FF), F32),
                   _sds((1, FF), F32)],
        scratch_shapes=[pltpu.VMEM((SEQ, CT), BF16)] * 2 + [pltpu.SemaphoreType.DMA((2,))], comm=comm,
    )(da3, u3, u3, yg3, yv3, cw, cw)
    return du3, jnp.concatenate([dcwg, dcwv], axis=1), jnp.concatenate([dcbg, dcbv], axis=1)


def _rope_tables():
    pos = jnp.arange(SEQ, dtype=F32)
    inv_freq = jnp.exp(jnp.arange(0, ROPE_DIM, 2, dtype=F32) * (-math.log(ROPE_THETA) / ROPE_DIM))
    ang = pos[:, None] * inv_freq[None, :]
    cos, sin = jnp.cos(ang), jnp.sin(ang)
    half = ROPE_DIM // 2
    zeros = jnp.zeros((SEQ, HEAD - ROPE_DIM), F32)
    zh = jnp.zeros((SEQ, half), F32)
    tab_c = jnp.concatenate([cos, cos, zeros + 1.0], axis=1)
    tab_a = jnp.concatenate([-sin, zh, zeros], axis=1)
    tab_b = jnp.concatenate([zh, sin, zeros], axis=1)
    return tab_c, tab_a, tab_b


def _rot(v, tc, ta, tb):
    half = ROPE_DIM // 2
    return v * tc + pltpu.roll(v, HEAD - half, 1) * ta + pltpu.roll(v, half, 1) * tb


def _rot_t(dv, tc, ta, tb):
    half = ROPE_DIM // 2
    return dv * tc + pltpu.roll(dv * ta, half, 1) + pltpu.roll(dv * tb, HEAD - half, 1)


def _band_masks():
    qi = lax.broadcasted_iota(jnp.int32, (HEAD, 2 * HEAD), 0)
    ki = lax.broadcasted_iota(jnp.int32, (HEAD, 2 * HEAD), 1)
    diff = HEAD + qi - ki
    both = (diff >= 0) & (diff <= HEAD)
    q1 = lax.broadcasted_iota(jnp.int32, (HEAD, HEAD), 0)
    k1 = lax.broadcasted_iota(jnp.int32, (HEAD, HEAD), 1)
    return q1 >= k1, both


_NT = (((1,), (1,)), ((), ()))
_TN = (((0,), (0,)), ((), ()))
_SCALE = HEAD ** -0.5


ATT_W = HEAD
ATT_HP = ATT_W // HEAD


def _res_rows(r, n, d, base=0):
    return pl.ds(base * d + r, n, stride=d) if d > 1 else pl.ds(base, n)


def _attn_load(q_ref, k_ref, v_ref, tc_ref, ta_ref, tb_ref, qs, ks, vs, d):
    L = SEQ // d
    for r in range(d):
        rows = _res_rows(r, L, d)
        dst = slice(r * L, (r + 1) * L)
        tc, ta, tb = tc_ref[dst, :], ta_ref[dst, :], tb_ref[dst, :]
        for hh in range(ATT_HP):
            sl = slice(hh * HEAD, (hh + 1) * HEAD)
            qs[dst, sl] = _rot(q_ref[rows, sl], tc, ta, tb).astype(BF16)
            ks[dst, sl] = _rot(k_ref[rows, sl], tc, ta, tb).astype(BF16)
            vs[dst, sl] = v_ref[rows, sl].astype(BF16)


def _attn_fwd(name, z3, tabs, g, d, comm=None):
    Bn = z3.shape[0]
    L = SEQ // d
    nb = L // HEAD
    W, nh = ATT_W, GROUP_W // ATT_W

    def body(q_ref, k_ref, v_ref, tc_ref, ta_ref, tb_ref, o_ref, l_ref, qs, ks, vs, sc, pc):
        m_first, m_both = _band_masks()
        _attn_load(q_ref, k_ref, v_ref, tc_ref, ta_ref, tb_ref, qs, ks, vs, d)
        blocks = [(r, n) for r in range(d) for n in range(nb)]

        def spans(r, n):
            rq = slice(r * L + n * HEAD, r * L + (n + 1) * HEAD)
            rk = slice(r * L + max(n - 1, 0) * HEAD, r * L + (n + 1) * HEAD)
            return rq, rk, slice(0, HEAD if n == 0 else 2 * HEAD)

        for i, (r, n) in enumerate(blocks):
            rq, rk, kc = spans(r, n)
            sc[i, :, kc] = lax.dot_general(qs[rq, :], ks[rk, :], _NT, preferred_element_type=F32)
        for i, (r, n) in enumerate(blocks):
            rq, rk, kc = spans(r, n)
            s = jnp.where(m_first if n == 0 else m_both, sc[i, :, kc] * _SCALE, NEG_INF)
            m = jnp.max(s, axis=-1, keepdims=True)
            e = jnp.exp(s - m)
            den = jnp.sum(e, axis=-1, keepdims=True)
            pc[i, :, kc] = (e * (1.0 / den)).astype(BF16)
            l_ref[_res_rows(r, HEAD, d, n * HEAD), :] = jnp.broadcast_to(m + jnp.log(den), (HEAD, HEAD))
        for i, (r, n) in enumerate(blocks):
            rq, rk, kc = spans(r, n)
            o_ref[_res_rows(r, HEAD, d, n * HEAD), :] = jnp.dot(pc[i, :, kc], vs[rk, :], preferred_element_type=F32)

    def zcol(off):
        return pl.BlockSpec((None, SEQ, W), lambda b, h: (b, 0, (off + g * GROUP_W) // W + h))

    tab = pl.BlockSpec((SEQ, HEAD), lambda b, h: (0, 0))
    out = pl.BlockSpec((None, SEQ, W), lambda b, h: (b, 0, h))
    return _call(
        body, name=name, grid=(Bn, nh),
        in_specs=[zcol(0), zcol(OFF_K), zcol(OFF_V), tab, tab, tab],
        out_specs=[out, out],
        out_shape=[_sds((Bn, SEQ, GROUP_W), F32), _sds((Bn, SEQ, GROUP_W), F32)],
        scratch_shapes=[pltpu.VMEM((SEQ, W), BF16)] * 3
        + [pltpu.VMEM((SEQ // HEAD, HEAD, 2 * HEAD), F32), pltpu.VMEM((SEQ // HEAD, HEAD, 2 * HEAD), BF16)],
        comm=comm,
    )(z3, z3, z3, *tabs)


def _attn_bwd(name, z3, tabs, g, d, do3, lse3, delta3, dz3, comm=None):
    Bn = z3.shape[0]
    L = SEQ // d
    nb = L // HEAD
    W, nh = ATT_W, GROUP_W // ATT_W

    def body(q_ref, k_ref, v_ref, tc_ref, ta_ref, tb_ref, do_ref, l_ref, dl_ref, dz_in, dz_ref,
             qs, ks, vs, dos, dqs, dks, dvs, nat, oq, ok, ov, sc, dpc, pc, dsc, sems):
        b, h = pl.program_id(0), pl.program_id(1)
        m_first, m_both = _band_masks()
        _attn_load(q_ref, k_ref, v_ref, tc_ref, ta_ref, tb_ref, qs, ks, vs, d)
        for r in range(d):
            dos[r * L:(r + 1) * L, :] = do_ref[_res_rows(r, L, d), :].astype(BF16)
        dks[...] = jnp.zeros_like(dks)
        dvs[...] = jnp.zeros_like(dvs)
        blocks = [(r, n) for r in range(d) for n in range(nb)]

        def spans(r, n):
            rq = slice(r * L + n * HEAD, r * L + (n + 1) * HEAD)
            rk = slice(r * L + max(n - 1, 0) * HEAD, r * L + (n + 1) * HEAD)
            return rq, rk, slice(0, HEAD if n == 0 else 2 * HEAD)

        for i, (r, n) in enumerate(blocks):
            rq, rk, kc = spans(r, n)
            sc[i, :, kc] = lax.dot_general(qs[rq, :], ks[rk, :], _NT, preferred_element_type=F32)
            dpc[i, :, kc] = lax.dot_general(dos[rq, :], vs[rk, :], _NT, preferred_element_type=F32)
        for i, (r, n) in enumerate(blocks):
            rq, rk, kc = spans(r, n)
            rows = _res_rows(r, HEAD, d, n * HEAD)
            s = jnp.where(m_first if n == 0 else m_both, sc[i, :, kc] * _SCALE, NEG_INF)
            p = jnp.exp(s - l_ref[rows, :][:, 0:1])
            pc[i, :, kc] = p.astype(BF16)
            dsc[i, :, kc] = (p * (dpc[i, :, kc] - dl_ref[rows, :][:, 0:1]) * _SCALE).astype(BF16)
        for i, (r, n) in enumerate(blocks):
            rq, rk, kc = spans(r, n)
            dqs[rq, :] = jnp.dot(dsc[i, :, kc], ks[rk, :], preferred_element_type=F32)
        for i, (r, n) in enumerate(blocks):
            rq, rk, kc = spans(r, n)
            dks[rk, :] += lax.dot_general(dsc[i, :, kc], qs[rq, :], _TN, preferred_element_type=F32)
            dvs[rk, :] += lax.dot_general(pc[i, :, kc], dos[rq, :], _TN, preferred_element_type=F32)
        step = b * nh + h

        def writes():
            base = g * GROUP_W + h * W
            return [pltpu.make_async_copy(src, dz_ref.at[b, :, pl.ds(pl.multiple_of(base + off, HEAD), W)],
                                          sems.at[i])
                    for i, (src, off) in enumerate(((oq, 0), (ok, OFF_K), (ov, OFF_V)))]

        @pl.when(step > 0)
        def _():
            for cp in writes():
                cp.wait()

        for src, dst, rotate in ((dqs, oq, True), (dks, ok, True), (dvs, ov, False)):
            for r in range(d):
                val = src[r * L:(r + 1) * L, :]
                if rotate:
                    rm = slice(r * L, (r + 1) * L)
                    val = _rot_t(val, tc_ref[rm, :], ta_ref[rm, :], tb_ref[rm, :])
                nat[_res_rows(r, L, d), :] = val
            dst[...] = nat[...].astype(BF16)
        for cp in writes():
            cp.start()

        @pl.when(step == Bn * nh - 1)
        def _():
            for cp in writes():
                cp.wait()

    def zcol(off):
        return pl.BlockSpec((None, SEQ, W), lambda b, h: (b, 0, (off + g * GROUP_W) // W + h))

    tab = pl.BlockSpec((SEQ, HEAD), lambda b, h: (0, 0))
    gcol = pl.BlockSpec((None, SEQ, W), lambda b, h: (b, 0, h))
    any_spec = pl.BlockSpec(memory_space=pl.ANY)
    return _call(
        body, name=name, grid=(Bn, nh),
        in_specs=[zcol(0), zcol(OFF_K), zcol(OFF_V), tab, tab, tab, gcol, gcol, gcol, any_spec],
        out_specs=any_spec,
        out_shape=_sds((Bn, SEQ, ZW), BF16),
        scratch_shapes=[pltpu.VMEM((SEQ, W), BF16)] * 4 + [pltpu.VMEM((SEQ, W), F32)] * 4
        + [pltpu.VMEM((SEQ, W), BF16)] * 3
        + [pltpu.VMEM((SEQ // HEAD, HEAD, 2 * HEAD), F32)] * 2 + [pltpu.VMEM((SEQ // HEAD, HEAD, 2 * HEAD), BF16)] * 2
        + [pltpu.SemaphoreType.DMA((3,))],
        aliases={9: 0}, comm=comm,
    )(z3, z3, z3, *tabs, do3, lse3, delta3, dz3)


def _merge_weights(l0, l1, l2):
    m = jnp.maximum(jnp.maximum(l0, l1), l2)
    e0, e1, e2 = jnp.exp(l0 - m), jnp.exp(l1 - m), jnp.exp(l2 - m)
    inv = 1.0 / (e0 + e1 + e2)
    return e0 * inv, e1 * inv, e2 * inv


def _merge_fwd(name, outs, lses, tr=512):
    T = outs[0].shape[0]

    def body(o0, o1, o2, l0, l1, l2, a_ref):
        w0, w1, w2 = _merge_weights(l0[...], l1[...], l2[...])
        a_ref[...] = (w0 * o0[...] + w1 * o1[...] + w2 * o2[...]).astype(BF16)

    row = pl.BlockSpec((tr, GROUP_W), lambda i: (i, 0))
    return pl.pallas_call(
        body, name=name, grid=(T // tr,), in_specs=[row] * 6, out_specs=row,
        out_shape=_sds((T, GROUP_W), BF16), compiler_params=_params(1),
    )(*outs, *lses)


def _merge_bwd(name, outs, lses, dattn, tr=512):
    T = outs[0].shape[0]

    def body(o0, o1, o2, l0, l1, l2, da_ref, d0, d1, d2, e0, e1, e2):
        w = _merge_weights(l0[...], l1[...], l2[...])
        da = da_ref[...]
        attn = w[0] * o0[...] + w[1] * o1[...] + w[2] * o2[...]
        prod = da * attn
        csum = jnp.concatenate(
            [jnp.broadcast_to(jnp.sum(prod[:, hh * HEAD:(hh + 1) * HEAD], axis=-1, keepdims=True), (tr, HEAD))
             for hh in range(GROUP_W // HEAD)], axis=1)
        for wg, d_ref, e_ref in zip(w, (d0, d1, d2), (e0, e1, e2)):
            d_ref[...] = wg * da
            e_ref[...] = wg * csum

    row = pl.BlockSpec((tr, GROUP_W), lambda i: (i, 0))
    res = pl.pallas_call(
        body, name=name, grid=(T // tr,), in_specs=[row] * 7, out_specs=[row] * 6,
        out_shape=[_sds((T, GROUP_W), F32)] * 6,
        compiler_params=_params(1),
    )(*outs, *lses, dattn)
    return res[:3], res[3:]


def _rms_rows(xv, g):
    r = lax.rsqrt(jnp.mean(xv * xv, axis=-1, keepdims=True) + RMS_EPS)
    return (xv * r * g).astype(BF16)


def _epi_norm(val, x_refs, o_ref, y_refs, first_rows):
    o_ref[...] = val
    y_refs[0][...] = _rms_rows(val, x_refs[0][...])


def _epi_ple(val, x_refs, o_ref, y_refs, first_rows):
    o_ref[...] = val
    xn = x_refs[0][...] + x_refs[1][...] * _sigmoid(val)
    y_refs[0][...] = xn
    if len(y_refs) > 1:
        y_refs[1][...] = _rms_rows(xn, x_refs[2][...])


def _epi_norm_bwd(val, x_refs, o_ref, y_refs, first_rows):
    xv = x_refs[0][...]
    r = lax.rsqrt(jnp.mean(xv * xv, axis=-1, keepdims=True) + RMS_EPS)
    xh = xv * r
    part = _rows8(val * xh)

    @pl.when(first_rows)
    def _():
        y_refs[0][...] = part

    @pl.when(jnp.logical_not(first_rows))
    def _():
        y_refs[0][...] += part

    dxh = val * x_refs[2][...]
    o_ref[...] = x_refs[1][...] + r * (dxh - xh * jnp.mean(dxh * xh, axis=-1, keepdims=True))


def _local_step(x3, p4, tgt3, vecs, ex):
    Bn = x3.shape[0]
    T = Bn * SEQ
    x = x3.reshape(T, D)
    tgt = tgt3.reshape(T, D)
    pb = p4.astype(BF16).reshape(DEPTH, T, PLE)
    tabs = {}
    for d in DILATIONS:
        tabs[d] = [t.reshape(SEQ // d, d, HEAD).transpose(1, 0, 2).reshape(SEQ, HEAD) for t in _rope_tables()]
    tm = 1024 if T % 1024 == 0 else 512
    nt = T // tm
    tk = 1024 if T % 1024 == 0 else 512
    ntk = T // tk
    tm5 = 512
    f32o = lambda n: _sds((T, n), F32)

    def spec(shape, fn):
        return pl.BlockSpec(shape, fn)

    def _mm(name, *args, **kwargs):
        return _mm_call(name, *args, comm=ex.hook(name), **kwargs)

    def cols4(a_ref, b_ref):
        av = a_ref[...]
        return jnp.concatenate([_dot(av, b_ref[j]) for j in range(NCHIP)], axis=1)

    def rows4(a_ref, b_ref):
        av = a_ref[...]
        return jnp.concatenate([_dot(av, b_ref[:, j * 256:(j + 1) * 256], ta=True) for j in range(NCHIP)], axis=0)

    def kchunks4(a_ref, b_ref):
        total = _dot(a_ref[:, 0:256], b_ref[0], tb=True)
        for j in range(1, NCHIP):
            total = total + _dot(a_ref[:, j * 256:(j + 1) * 256], b_ref[j], tb=True)
        return total

    row5 = spec((tm5, D), lambda i, *_: (i, 0))
    rowm = spec((tm, D), lambda i, *_: (i, 0))
    gain = spec((1, D), lambda *_: (0, 0))
    bf_rows = (_sds((T, D), BF16), row5)

    saved = []
    h = _rms_fwd("rms_mix0", x, vecs["g_mix"][0:1])
    for l in range(DEPTH):
        L = str(l)
        G = ex.weights(l)
        g_mix, g_ffn, g_ple = (vecs[k][l:l + 1] for k in ("g_mix", "g_ffn", "g_ple"))
        pscale, cb, cw = vecs["pool_scale"][l:l + 1], vecs["conv_b"][l:l + 1], vecs["conv_w"][l]
        z = _mm("mm_z" + L, h, G["in"], grid=(nt, NCHIP),
                a_spec=spec((tm, D), lambda i, n: (i, 0)),
                b_spec=spec((None, D, IN_S), lambda i, n: (n, 0, 0)),
                o_spec=spec((tm, IN_S), lambda i, n: (i, n)), out_shape=f32o(ZW))
        z3 = z.reshape(Bn, SEQ, ZW)
        outs, lses = [], []
        for g, d in enumerate(DILATIONS):
            o_g, l_g = _attn_fwd("attn_fwd%d_%d" % (g, l), z3, tabs[d], g, d, comm=ex.hook("attn_fwd%d_%d" % (g, l)))
            outs.append(o_g.reshape(T, GROUP_W))
            lses.append(l_g.reshape(T, GROUP_W))
        attn = _merge_fwd("merge_fwd" + L, outs, lses)
        ya = _mm("mm_ya" + L, attn, G["sc"], grid=(nt,), compute=cols4,
                 a_spec=spec((tm, GROUP_W), lambda i: (i, 0)),
                 b_spec=spec((NCHIP, GROUP_W, 256), lambda i: (0, 0, 0)),
                 o_spec=spec((tm, D), lambda i: (i, 0)), out_shape=_sds((T, D), BF16))
        pooled3, ms3 = _pool_fwd("pool_fwd" + L, z3, G["sc"], pscale)
        ms = ms3.reshape(T, D)

        def row_sharded(name, a, rb, res=None, kdim=D, out=F32, rows=tm5, **fused):
            if rb is None:
                b_arr, b_spec = G["dn"], spec((NCHIP, DN_S, D), lambda i: (0, 0, 0))
            else:
                b_arr, b_spec = G["r3"], spec((NCHIP, 256, D), lambda i: (0, rb, 0))
            tile = spec((rows, D), lambda i, *_: (i, 0))
            return _mm(name, a, b_arr, grid=(T // rows,),
                       a_spec=spec((rows, kdim), lambda i: (i, 0)), b_spec=b_spec,
                       o_spec=tile, out_shape=_sds((T, D), out), res=res, res_spec=None if res is None else tile,
                       **fused)

        yb = row_sharded("mm_yb" + L, ms, 0, out=BF16, rows=tm)
        merged = _gate_fwd("gate_fwd" + L, z, ya, yb)
        x1, h2 = row_sharded("mm_o" + L, merged, 1, res=x, epilogue=_epi_norm, extra_in=[(g_ffn, gain)],
                             extra_out=[bf_rows])
        u = _mm("mm_up" + L, h2, G["up"], grid=(nt, NCHIP),
                a_spec=spec((tm, D), lambda i, n: (i, 0)),
                b_spec=spec((None, D, UP_S), lambda i, n: (n, 0, 0)),
                o_spec=spec((tm, UP_S), lambda i, n: (i, n)), out_shape=f32o(UW))
        u3 = u.reshape(Bn, SEQ, UW)
        act3, yg3, yv3 = _conv_fwd("conv_fwd" + L, u3, cw, cb, comm=ex.hook("conv_fwd" + L))
        act = act3.reshape(T, FF)
        x2, h3 = row_sharded("mm_down" + L, act, None, res=x1, kdim=FF, epilogue=_epi_norm,
                             extra_in=[(g_ple, gain)], extra_out=[bf_rows])
        pe = _mm("mm_pe" + L, pb[l], G["sc"], grid=(nt,), compute=cols4,
                 a_spec=spec((tm, PLE), lambda i: (i, 0)),
                 b_spec=spec((NCHIP, 256, 256), lambda i: (0, 2, 0)),
                 o_spec=spec((tm, D), lambda i: (i, 0)), out_shape=f32o(D))
        fused_in = [(x2, row5), (pe, row5)]
        fused_out = [(f32o(D), row5)]
        if l + 1 < DEPTH:
            fused_in.append((vecs["g_mix"][l + 1:l + 2], gain))
            fused_out.append(bf_rows)
        pg, x3n, *h_next = row_sharded("mm_pg" + L, h3, 2, epilogue=_epi_ple, extra_in=fused_in,
                                       extra_out=fused_out)
        saved.append(dict(x=x, h=h, z=z, outs=outs, lses=lses, attn=attn, ya=ya, yb=yb, pooled3=pooled3, ms=ms,
                          merged=merged, x1=x1, h2=h2, u3=u3, yg3=yg3, yv3=yv3, act=act, x2=x2, h3=h3, pg=pg, pe=pe))
        x = x3n
        h = h_next[0] if h_next else None

    dx, dg_final8, sq8 = _final_loss(x, vecs["g_final"].reshape(1, D), tgt)

    gg_shape = {k: _sds(G[k].shape, F32) for k in G}
    small = {"g_final": dg_final8}

    for l in reversed(range(DEPTH)):
        L = str(l)
        sv = saved[l]
        G = ex.weights(l)
        GG = dict.fromkeys(_KEYS)
        g_mix, g_ffn, g_ple = (vecs[k][l:l + 1] for k in ("g_mix", "g_ffn", "g_ple"))
        pscale, cb, cw = vecs["pool_scale"][l:l + 1], vecs["conv_b"][l:l + 1], vecs["conv_w"][l]

        def wgrad_rows(name, a, b_arr, key, rb):
            GG[key] = _mm(name, a, b_arr, grid=(ntk,), ta=True, k_axis=0, nk=ntk, acc_shape=(D, D),
                          a_spec=spec((tk, D), lambda k: (k, 0)),
                          b_spec=spec((tk, D), lambda k: (k, 0)),
                          o_spec=spec((NCHIP, 256, D), lambda k: (0, rb, 0)),
                          out_shape=gg_shape[key], buf=GG[key])

        def dgrad_rows(name, dy, rb, out=F32, rows=tm5, **fused):
            return _mm(name, dy, G["r3"], grid=(T // rows,), tb=True,
                       a_spec=spec((rows, D), lambda i: (i, 0)),
                       b_spec=spec((NCHIP, 256, D), lambda i: (0, rb, 0)),
                       o_spec=spec((rows, D), lambda i, *_: (i, 0)), out_shape=_sds((T, D), out), **fused)

        def norm_bwd(xin, dres, g, rows=row5):
            return dict(epilogue=_epi_norm_bwd, extra_in=[(xin, rows), (dres, rows), (g, gain)],
                        extra_out=[(_sds((8, D), F32), spec((8, D), lambda *_: (0, 0)))])

        dpe, dpg = _ple_bwd("ple_bwd" + L, dx, sv["pe"], sv["pg"])
        GG["sc"] = _mm("wg_ple" + L, pb[l], dpe, grid=(ntk,), compute=rows4, k_axis=0, nk=ntk,
                       acc_shape=(NCHIP * PLE, 256),
                       a_spec=spec((tk, PLE), lambda k: (k, 0)), b_spec=spec((tk, D), lambda k: (k, 0)),
                       o_spec=spec((NCHIP, 256, 256), lambda k: (0, 2, 0)),
                       out_shape=gg_shape["sc"], buf=GG["sc"])
        wgrad_rows("wg_pg" + L, sv["h3"], dpg, "r3", 2)
        dx, small["g_ple" + L] = dgrad_rows("dg_pg" + L, dpg, 2, **norm_bwd(sv["x2"], dx, g_ple))

        da = _mm("dg_down" + L, dx, G["dn"], grid=(T // 256,), tb=True,
                 a_spec=spec((256, D), lambda i: (i, 0)),
                 b_spec=spec((NCHIP, DN_S, D), lambda i: (0, 0, 0)),
                 o_spec=spec((256, FF), lambda i: (i, 0)), out_shape=_sds((T, FF), BF16))
        GG["dn"] = _mm("wg_down" + L, sv["act"], dx, grid=(2, ntk), ta=True, k_axis=1, nk=ntk,
                       acc_shape=(FF, 512),
                       a_spec=spec((tk, FF), lambda n, k: (k, 0)), b_spec=spec((tk, 512), lambda n, k: (k, n)),
                       o_spec=spec((NCHIP, DN_S, 512), lambda n, k: (0, 0, n)),
                       out_shape=gg_shape["dn"], buf=GG["dn"])
        du3, dcw, dcb = _conv_bwd("conv_bwd" + L, da.reshape(Bn, SEQ, FF), sv["u3"], sv["yg3"], sv["yv3"], cw,
                                  comm=ex.hook("conv_bwd" + L))
        small["conv_w" + L], small["conv_b" + L] = dcw, dcb
        du = du3.reshape(T, UW)
        dx, small["g_ffn" + L] = _mm(
            "dg_up" + L, du, G["up"], grid=(nt, NCHIP), tb=True, k_axis=1, nk=NCHIP, acc_shape=(tm, D),
            a_spec=spec((tm, UP_S), lambda i, k: (i, k)), b_spec=spec((None, D, UP_S), lambda i, k: (k, 0, 0)),
            o_spec=rowm, out_shape=f32o(D), vmem=VMEM_BIG, **norm_bwd(sv["x1"], dx, g_ffn, rowm))
        GG["up"] = _mm("wg_up" + L, sv["h2"], du, grid=(NCHIP, ntk), ta=True, k_axis=1, nk=ntk,
                       acc_shape=(D, UP_S),
                       a_spec=spec((tk, D), lambda j, k: (k, 0)),
                       b_spec=spec((tk, UP_S), lambda j, k: (k, j)),
                       o_spec=spec((None, D, UP_S), lambda j, k: (j, 0, 0)),
                       out_shape=gg_shape["up"], buf=GG["up"])

        dmerged = dgrad_rows("dg_o" + L, dx, 1, out=BF16, rows=tm)
        wgrad_rows("wg_o" + L, sv["merged"], dx, "r3", 1)
        dya, dz = _gate_bwd("gate_bwd_a" + L, sv["z"], OFF_GA, sv["ya"], dmerged, None)
        dyb, dz = _gate_bwd("gate_bwd_b" + L, sv["z"], OFF_GB, sv["yb"], dmerged, dz)
        dms = dgrad_rows("dg_yb" + L, dyb, 0, rows=tm)
        wgrad_rows("wg_yb" + L, sv["ms"], dyb, "r3", 0)
        dz3, GG["sc"], small["pool_scale" + L] = _pool_bwd(
            "pool_bwd" + L, dms.reshape(Bn, SEQ, D), sv["pooled3"], G["sc"], pscale,
            dz.reshape(Bn, SEQ, ZW), GG["sc"])
        dattn = _mm("dg_ya" + L, dya, G["sc"], grid=(nt,), compute=kchunks4,
                    a_spec=spec((tm, D), lambda i: (i, 0)),
                    b_spec=spec((NCHIP, GROUP_W, 256), lambda i: (0, 0, 0)),
                    o_spec=spec((tm, GROUP_W), lambda i: (i, 0)), out_shape=f32o(GROUP_W))
        GG["sc"] = _mm("wg_ya" + L, sv["attn"], dya, grid=(ntk,), compute=rows4, k_axis=0, nk=ntk,
                       acc_shape=(NCHIP * GROUP_W, 256),
                       a_spec=spec((tk, GROUP_W), lambda k: (k, 0)), b_spec=spec((tk, D), lambda k: (k, 0)),
                       o_spec=spec((NCHIP, GROUP_W, 256), lambda k: (0, 0, 0)),
                       out_shape=gg_shape["sc"], buf=GG["sc"])
        ex.grads_ready(l, {k: GG[k] for k in _KEYS[1:]})
        dos, deltas = _merge_bwd("merge_bwd" + L, sv["outs"], sv["lses"], dattn)
        view3 = lambda t: t.reshape(Bn, SEQ, GROUP_W)
        sz3 = sv["z"].reshape(Bn, SEQ, ZW)
        for g, d in enumerate(DILATIONS):
            dz3 = _attn_bwd("attn_bwd%d_%d" % (g, l), sz3, tabs[d], g, d, view3(dos[g]), view3(sv["lses"][g]),
                            view3(deltas[g]), dz3, comm=ex.hook("attn_bwd%d_%d" % (g, l)))
        dz = dz3.reshape(T, ZW)
        GG["in"] = _mm("wg_z" + L, sv["h"], dz, grid=(NCHIP, ntk), ta=True, k_axis=1, nk=ntk,
                       acc_shape=(D, IN_S),
                       a_spec=spec((tk, D), lambda n, k: (k, 0)), b_spec=spec((tk, IN_S), lambda n, k: (k, n)),
                       o_spec=spec((None, D, IN_S), lambda n, k: (n, 0, 0)),
                       out_shape=gg_shape["in"], buf=GG["in"])
        ex.grads_ready(l, {"in": GG["in"]})

        def dgrad_z(name, first, count, dres, buf):
            rows = spec((tm, D), lambda i, *_: (i + first, 0))
            return _mm(name, dz, G["in"], grid=(count, NCHIP), tb=True, k_axis=1, nk=NCHIP, acc_shape=(tm, D),
                       a_spec=spec((tm, IN_S), lambda i, k: (i + first, k)),
                       b_spec=spec((None, D, IN_S), lambda i, k: (k, 0, 0)),
                       o_spec=rows, out_shape=f32o(D), vmem=VMEM_BIG, buf=buf, **norm_bwd(sv["x"], dres, g_mix, rows))

        if l == 0 and nt % 2 == 0:
            dx_a, part_a = dgrad_z("dg_z0a", 0, nt // 2, dx, None)
            dx, part_b = dgrad_z("dg_z0b", nt // 2, nt // 2, dx, dx_a)
            small["g_mix" + L] = part_a + part_b
        else:
            dx, small["g_mix" + L] = dgrad_z("dg_z" + L, 0, nt, dx, None)

    return sq8, dx.reshape(Bn, SEQ, D), small


_ANY = pl.BlockSpec(memory_space=pl.ANY)


def _place():
    x, y, c = lax.axis_index("x"), lax.axis_index("y"), lax.axis_index("c")
    chips = [(1 - x, y), (x, 1 - y), (1 - x, 1 - y)]
    return x, y, c, 2 * x + y, chips


def _half(rows, cc):
    return pl.ds(cc * (rows // 2), rows // 2)


def _remote(src, dst, send_sems, recv_sems, i, to):
    return pltpu.make_async_remote_copy(src_ref=src, dst_ref=dst, send_sem=send_sems.at[i], recv_sem=recv_sems.at[i],
                                        device_id=to, device_id_type=MESH)


def _exchange_gather_ici(stacks, done):
    n = len(stacks)
    rows = [t.shape[1] for t in stacks]

    def start(refs, send_sems, recv_sems):
        x, y, c, me, chips = _place()
        for k in range(n):
            part = refs[k].at[me, _half(rows[k], c)]
            for j, chip in enumerate(chips):
                _remote(part, part, send_sems, recv_sems, 3 * k + j, (*chip, c)).start()

    def wait(refs, send_sems, recv_sems):
        x, y, c, me, chips = _place()
        for k in range(n):
            for j, chip in enumerate(chips):
                part = refs[k].at[2 * chip[0] + chip[1], _half(rows[k], c)]
                _remote(part, part, send_sems, recv_sems, 3 * k + j, (*chip, c)).wait()

    return dict(arrays=list(stacks), nsem=3 * n, start=start, wait=wait, done=done)


def _exchange_gather_d2d(stacks, done):
    n = len(stacks)
    rows = [t.shape[1] for t in stacks]

    def copies(refs, send_sems, recv_sems, mine):
        x, y, c, me, chips = _place()
        cc = c if mine else 1 - c
        return [_remote(part, part, send_sems, recv_sems, 3 * k + j, (x, y, 1 - c))
                for k in range(n) for j, chip in enumerate(chips)
                for part in [refs[k].at[2 * chip[0] + chip[1], _half(rows[k], cc)]]]

    def start(refs, send_sems, recv_sems):
        for cp in copies(refs, send_sems, recv_sems, True):
            cp.start()

    def wait(refs, send_sems, recv_sems):
        for cp in copies(refs, send_sems, recv_sems, False):
            cp.wait()

    return dict(arrays=list(stacks), nsem=3 * n, start=start, wait=wait, done=done)


def _exchange_halves(g5, recv, done):
    n = len(g5)

    def copies(refs, send_sems, recv_sems):
        x, y, c, me, chips = _place()
        return [_remote(refs[k].at[:, 1 - c], refs[n + k], send_sems, recv_sems, k, (x, y, 1 - c)) for k in range(n)]

    def start(refs, send_sems, recv_sems):
        for cp in copies(refs, send_sems, recv_sems):
            cp.start()

    def wait(refs, send_sems, recv_sems):
        for cp in copies(refs, send_sems, recv_sems):
            cp.wait()

    return dict(arrays=list(g5) + list(recv), nsem=n, start=start, wait=wait, done=done)


def _exchange_chips(parts, landing, done):
    n = len(parts)

    def start(refs, send_sems, recv_sems):
        x, y, c, me, chips = _place()
        for k in range(n):
            for j, chip in enumerate(chips):
                _remote(refs[k].at[2 * chip[0] + chip[1]], refs[n + k].at[me], send_sems, recv_sems, 3 * k + j,
                        (*chip, c)).start()

    def wait(refs, send_sems, recv_sems):
        x, y, c, me, chips = _place()
        for k in range(n):
            for j, chip in enumerate(chips):
                slot = refs[n + k].at[2 * chip[0] + chip[1]]
                _remote(slot, slot, send_sems, recv_sems, 3 * k + j, (*chip, c)).wait()

    return dict(arrays=list(parts) + list(landing), nsem=3 * n, start=start, wait=wait, done=done)


def _exchange_share(full, layer, done):
    n = len(full)

    def copies(refs, send_sems, recv_sems, mine):
        x, y, c, me, chips = _place()
        cc = c if mine else 1 - c
        return [_remote(part, part, send_sems, recv_sems, k, (x, y, 1 - c))
                for k in range(n) for part in [refs[k].at[layer, cc]]]

    def start(refs, send_sems, recv_sems):
        for cp in copies(refs, send_sems, recv_sems, True):
            cp.start()

    def wait(refs, send_sems, recv_sems):
        for cp in copies(refs, send_sems, recv_sems, False):
            cp.wait()

    return dict(arrays=list(full), nsem=n, start=start, wait=wait, done=done)


def _exchange_call(name, comm):
    arrays = comm["arrays"]
    n = len(arrays)

    def body(*refs):
        outs, send_sems, recv_sems = refs[n:2 * n], refs[2 * n], refs[2 * n + 1]
        comm["start"](outs, send_sems, recv_sems)
        comm["wait"](outs, send_sems, recv_sems)

    outs = pl.pallas_call(
        body, name=name, in_specs=[_ANY] * n, out_specs=[_ANY] * n,
        out_shape=[_sds(t.shape, t.dtype) for t in arrays],
        scratch_shapes=[pltpu.SemaphoreType.DMA((comm["nsem"],))] * 2,
        input_output_aliases={i: i for i in range(n)},
    )(*arrays)
    comm["done"](outs)


def _gather_first(stacks, cw4):
    n = len(stacks)
    ici = _exchange_gather_ici(stacks, None)
    d2d = _exchange_gather_d2d(stacks, None)
    rows = [t.shape[1] for t in stacks]

    def body(*refs):
        g_refs, cwg_ref = refs[n + 1:2 * n + 1], refs[2 * n + 1]
        s_ici, r_ici, s_d2d, r_d2d, s_cw, r_cw = refs[2 * n + 2:]
        x, y, c, me, chips = _place()

        def cw_copy(j, slot, chip):
            part = cwg_ref.at[slot]
            return _remote(part, part, s_cw, r_cw, j, (*chip, c))

        ici["start"](g_refs, s_ici, r_ici)
        for j, chip in enumerate(chips):
            cw_copy(j, me, chip).start()
        for k in range(n):
            for j, chip in enumerate(chips):
                part = g_refs[k].at[2 * chip[0] + chip[1], _half(rows[k], c)]
                _remote(part, part, s_ici, r_ici, 3 * k + j, (*chip, c)).wait()
                _remote(part, part, s_d2d, r_d2d, 3 * k + j, (x, y, 1 - c)).start()
        d2d["wait"](g_refs, s_d2d, r_d2d)
        for j, chip in enumerate(chips):
            cw_copy(j, 2 * chip[0] + chip[1], chip).wait()

    outs = pl.pallas_call(
        body, name="gather_first", in_specs=[_ANY] * (n + 1), out_specs=[_ANY] * (n + 1),
        out_shape=[_sds(t.shape, t.dtype) for t in stacks] + [_sds(cw4.shape, cw4.dtype)],
        scratch_shapes=[pltpu.SemaphoreType.DMA((3 * n,))] * 4 + [pltpu.SemaphoreType.DMA((3,))] * 2,
        input_output_aliases={i: i for i in range(n + 1)},
    )(*stacks, cw4)
    return outs[:n], outs[n]


def _small_allreduce(small, comm):
    arrays = comm["arrays"]
    n = len(arrays)

    def body(*refs):
        small_ref, c_refs, red_ref = refs[n], refs[n + 1:2 * n + 1], refs[2 * n + 1]
        gath, s_send, s_recv, c_send, c_recv = refs[2 * n + 2:]
        x, y, c, me, chips = _place()
        dev = 4 * x + 2 * y + c
        gath[dev] = small_ref[...]
        comm["start"](c_refs, c_send, c_recv)
        for r in range(1, 8):
            peer = (x ^ (r >> 2), y ^ ((r >> 1) & 1), c ^ (r & 1))
            _remote(small_ref, gath.at[dev], s_send, s_recv, r - 1, peer).start()
        for r in range(1, 8):
            peer = (x ^ (r >> 2), y ^ ((r >> 1) & 1), c ^ (r & 1))
            src = 4 * peer[0] + 2 * peer[1] + peer[2]
            _remote(small_ref, gath.at[src], s_send, s_recv, r - 1, peer).wait()
        total = gath[0]
        for i in range(1, 8):
            total = total + gath[i]
        red_ref[...] = total
        comm["wait"](c_refs, c_send, c_recv)

    vm = pl.BlockSpec(memory_space=pltpu.VMEM)
    outs = pl.pallas_call(
        body, name="small_allreduce_and_share", in_specs=[_ANY] * n + [vm], out_specs=[_ANY] * n + [vm],
        out_shape=[_sds(t.shape, t.dtype) for t in arrays] + [_sds(small.shape, F32)],
        scratch_shapes=[pltpu.VMEM((8,) + small.shape, F32), pltpu.SemaphoreType.DMA((7,)),
                        pltpu.SemaphoreType.DMA((7,)), pltpu.SemaphoreType.DMA((comm["nsem"],)),
                        pltpu.SemaphoreType.DMA((comm["nsem"],))],
        input_output_aliases={i: i for i in range(n)},
    )(*arrays, small)
    comm["done"](outs[:n])
    return outs[n]


def _row_tile(rh):
    for cand in (512, 384, 352, 256, 128):
        if rh % cand == 0:
            return cand
    return rh


def _add_halves(name, g5, recv, place):
    _, _, rh, cols = g5.shape
    tr = _row_tile(rh)

    def body(place_ref, g_ref, r_ref, o_ref, own_ref):
        val = (g_ref[...] + r_ref[...]).astype(BF16)
        o_ref[...] = val

        @pl.when(pl.program_id(1) == place_ref[1])
        def _():
            own_ref[...] = val

    grid_spec = pltpu.PrefetchScalarGridSpec(
        num_scalar_prefetch=1, grid=(rh // tr, NCHIP),
        in_specs=[pl.BlockSpec((None, None, tr, cols), lambda i, j, pr: (j, pr[0], i, 0)),
                  pl.BlockSpec((None, tr, cols), lambda i, j, pr: (j, i, 0))],
        out_specs=[pl.BlockSpec((None, tr, cols), lambda i, j, pr: (j, i, 0)),
                   pl.BlockSpec((None, tr, cols), lambda i, j, pr: (pr[1], i, 0))])
    return pl.pallas_call(
        body, name=name, grid_spec=grid_spec, out_shape=[_sds(recv.shape, BF16)] * 2, compiler_params=_params(2),
    )(place, g5, recv)


def _sum_chips(name, landing, place, layer, full):
    _, rh, cols = landing.shape
    tr = _row_tile(rh)
    has_full = full is not None

    def body(*refs):
        r_ref, o_ref = refs[1], refs[-1]
        total = r_ref[0].astype(F32)
        for j in range(1, NCHIP):
            total = total + r_ref[j].astype(F32)
        o_ref[...] = total

    grid_spec = pltpu.PrefetchScalarGridSpec(
        num_scalar_prefetch=1, grid=(rh // tr,),
        in_specs=[pl.BlockSpec((NCHIP, tr, cols), lambda i, pr: (0, i, 0))] + ([_ANY] if has_full else []),
        out_specs=pl.BlockSpec((None, None, tr, cols), lambda i, pr: (layer, pr[0], i, 0)))
    return pl.pallas_call(
        body, name=name, grid_spec=grid_spec, out_shape=_sds((DEPTH, 2, rh, cols), F32),
        input_output_aliases={2: 0} if has_full else {}, compiler_params=_params(1),
    )(place, landing, *([full] if has_full else []))


class _Schedule:
    FIRST, REST = ["in"], list(_KEYS[1:])

    def __init__(self, slotted, cw4, place):
        self.place = place
        self._w = [dict(zip(_KEYS, layer)) for layer in slotted]
        got, self.cw4 = _gather_first([self._w[0][k] for k in self.FIRST], cw4)
        self._w[0].update(zip(self.FIRST, got))
        self._g5, self._recv, self._parts, self._landing = [{}, {}], [{}, {}], [{}, {}], [{}, {}]
        self.full = {}
        every = list(_KEYS)
        self._hooks = {
            "mm_z0": lambda: self._gather(_exchange_gather_ici, 0, self.REST),
            "attn_fwd0_0": lambda: self._gather(_exchange_gather_d2d, 0, self.REST),
            "mm_up0": lambda: self._gather(_exchange_gather_ici, 1, self.REST),
            "conv_fwd0": lambda: self._gather(_exchange_gather_ici, 1, self.FIRST),
            "mm_down0": lambda: self._gather(_exchange_gather_d2d, 1, every),
            "dg_down0": lambda: self._halves(1, every),
            "conv_bwd0": lambda: self._chips(1, every),
            "dg_up0": lambda: self._share(1, every),
            "attn_bwd0_0": lambda: self._halves(0, self.REST),
            "wg_z0": lambda: self._chips(0, self.REST),
            "dg_z0a": lambda: self._halves(0, self.FIRST),
            "dg_z0b": lambda: self._chips(0, self.FIRST),
        }

    def weights(self, layer):
        return self._w[layer]

    def hook(self, name):
        make = self._hooks.get(name)
        return make() if make else None

    def grads_ready(self, layer, GG):
        for k, t in GG.items():
            g5 = t.reshape(NCHIP, 2, t.shape[1] // 2, t.shape[2])
            self._g5[layer][k] = g5
            self._recv[layer][k] = lax.empty((NCHIP,) + g5.shape[2:], F32)

    def _gather(self, make, layer, keys):
        def done(arrays):
            self._w[layer].update(zip(keys, arrays))
        return make([self._w[layer][k] for k in keys], done)

    def _halves(self, layer, keys):
        return _exchange_halves([self._g5[layer][k] for k in keys], [self._recv[layer][k] for k in keys],
                                lambda arrays: self._halves_done(layer, keys, arrays))

    def _halves_done(self, layer, keys, arrays):
        n = len(keys)
        for k, g, r in zip(keys, arrays[:n], arrays[n:]):
            self._parts[layer][k], self._landing[layer][k] = _add_halves(
                "add_halves%d_%s" % (layer, k), g, r, self.place)

    def _chips(self, layer, keys):
        return _exchange_chips([self._parts[layer][k] for k in keys], [self._landing[layer][k] for k in keys],
                               lambda arrays: self._chips_done(layer, keys, arrays))

    def _chips_done(self, layer, keys, arrays):
        for k, t in zip(keys, arrays[len(keys):]):
            self.full[k] = _sum_chips("sum_chips%d_%s" % (layer, k), t, self.place, layer, self.full.get(k))

    def _share(self, layer, keys):
        def done(arrays):
            self.full.update(zip(keys, arrays))
        return _exchange_share([self.full[k] for k in keys], layer, done)

    def finish(self, small):
        small_red = _small_allreduce(small, self._share(0, list(_KEYS)))
        return [self.full[k] for k in _KEYS], small_red


def _adamw(name, w, g, m, v):
    shape = w.shape
    cols = shape[-1]
    rows = 1
    for s in shape[:-1]:
        rows *= s
    tr = rows
    for cand in (256, 128, 64):
        if rows > cand and rows % cand == 0:
            tr = cand
            break
    c1 = 1.0 / (1.0 - B1 ** STEP)
    c2 = 1.0 / (1.0 - B2 ** STEP)

    def body(w_ref, g_ref, m_ref, v_ref, go_ref, d_ref, nm_ref, nv_ref):
        gv = g_ref[...]
        go_ref[...] = gv
        nm = B1 * m_ref[...] + (1.0 - B1) * gv
        nv = B2 * v_ref[...] + (1.0 - B2) * (gv * gv)
        nm_ref[...] = nm
        nv_ref[...] = nv
        d_ref[...] = -LR * ((nm * c1) / (jnp.sqrt(nv * c2) + ADAM_EPS) + WD * w_ref[...])

    blk = pl.BlockSpec((tr, cols), lambda i: (i, 0))
    outs = pl.pallas_call(
        body, name=name, grid=(rows // tr,), in_specs=[blk] * 4, out_specs=[blk] * 4,
        out_shape=[_sds((rows, cols), F32)] * 4, compiler_params=_params(1),
    )(*(t.reshape(rows, cols) for t in (w, g, m, v)))
    return tuple(o.reshape(shape) for o in outs)


def _pack_small(small):
    rows = [jnp.sum(small["g_mix%d" % l], axis=0, keepdims=True) for l in range(DEPTH)]
    rows += [jnp.sum(small["pool_scale%d" % l], axis=0, keepdims=True) for l in range(DEPTH)]
    rows += [jnp.sum(small["g_ffn%d" % l], axis=0, keepdims=True) for l in range(DEPTH)]
    rows += [jnp.sum(small["g_ple%d" % l], axis=0, keepdims=True) for l in range(DEPTH)]
    rows += [jnp.sum(small["g_final"], axis=0, keepdims=True)]
    flat = [small["conv_b%d" % l].reshape(-1) for l in range(DEPTH)]
    flat += [small["conv_w%d" % l].reshape(-1) for l in range(DEPTH)]
    flat = jnp.concatenate(flat).reshape(-1, D)
    packed = jnp.concatenate(rows + [flat], axis=0)
    return jnp.pad(packed, ((0, SMALL_ROWS - packed.shape[0]), (0, 0)))


def _unpack_small(red):
    g_mix, pool_scale, g_ffn, g_ple = red[0:2], red[2:4], red[4:6], red[6:8]
    g_final = red[8]
    nb = DEPTH * UW // D
    conv_b = red[9:9 + nb].reshape(DEPTH, UW)
    conv_w = red[9 + nb:9 + 4 * nb].reshape(DEPTH, 3, UW)
    return g_mix, pool_scale, g_ffn, g_ple, g_final, conv_b, conv_w


def kernel(x, p, g_mix, w_in, w_ya, w_yb, pool_w, pool_scale, w_o, g_ffn, w_up, conv_w, conv_b, w_down, g_ple, w_ple, w_ple_gate, g_final, loss_target, m_g_mix, m_w_in, m_w_ya, m_w_yb, m_pool_w, m_pool_scale, m_w_o, m_g_ffn, m_w_up, m_conv_w, m_conv_b, m_w_down, m_g_ple, m_w_ple, m_w_ple_gate, m_g_final, v_g_mix, v_w_in, v_w_ya, v_w_yb, v_pool_w, v_pool_scale, v_w_o, v_g_ffn, v_w_up, v_conv_w, v_conv_b, v_w_down, v_g_ple, v_w_ple, v_w_ple_gate, v_g_final):
    me = 2 * lax.axis_index("x") + lax.axis_index("y")
    place = jnp.stack([lax.axis_index("c"), me]).astype(jnp.int32)

    def slot(shard):
        return lax.dynamic_update_index_in_dim(lax.empty((NCHIP,) + shard.shape, shard.dtype), shard, me, 0)

    packed = [
        w_in.astype(BF16), w_up.astype(BF16),
        jnp.concatenate([w_ya, w_ple, pool_w.reshape(DEPTH, 256, 256)], axis=1).astype(BF16),
        jnp.concatenate([w_yb, w_o, w_ple_gate], axis=1).astype(BF16),
        w_down.astype(BF16),
    ]
    slotted = [[slot(t[l]) for t in packed] for l in range(DEPTH)]
    ex = _Schedule(slotted, slot(conv_w.reshape(DEPTH * 3, UP_S)), place)
    cw_full = ex.cw4.reshape(NCHIP, DEPTH, 3, UP_S).transpose(1, 2, 0, 3).reshape(DEPTH, 3, UW)

    vecs = dict(g_mix=g_mix, pool_scale=pool_scale, g_ffn=g_ffn, g_ple=g_ple, g_final=g_final, conv_b=conv_b,
                conv_w=cw_full)
    sq8, grad_x, small = _local_step(x, p, loss_target, vecs, ex)
    loss = lax.psum(jnp.sum(sq8) * (0.5 / D), ("x", "y", "c"))

    full, small_red = ex.finish(_pack_small(small))
    r_in, r_up, r_sc, r_r3, r_dn = [f.reshape(DEPTH, -1, f.shape[-1]) for f in full]
    d_g_mix, d_pool_scale, d_g_ffn, d_g_ple, d_g_final, d_conv_b, d_conv_w_full = _unpack_small(small_red)
    d_conv_w = lax.dynamic_slice_in_dim(d_conv_w_full, me * UP_S, UP_S, axis=2)

    grads = dict(
        g_mix=d_g_mix, w_in=r_in, w_ya=r_sc[:, 0:512], w_yb=r_r3[:, 0:256],
        pool_w=r_sc[:, 768:1024].reshape(DEPTH, 4, 64, 256), pool_scale=d_pool_scale, w_o=r_r3[:, 256:512],
        g_ffn=d_g_ffn, w_up=r_up, conv_w=d_conv_w, conv_b=d_conv_b, w_down=r_dn, g_ple=d_g_ple,
        w_ple=r_sc[:, 512:768], w_ple_gate=r_r3[:, 512:768], g_final=d_g_final)
    weights = dict(g_mix=g_mix, w_in=w_in, w_ya=w_ya, w_yb=w_yb, pool_w=pool_w, pool_scale=pool_scale, w_o=w_o,
                   g_ffn=g_ffn, w_up=w_up, conv_w=conv_w, conv_b=conv_b, w_down=w_down, g_ple=g_ple, w_ple=w_ple,
                   w_ple_gate=w_ple_gate, g_final=g_final)
    m_in = dict(g_mix=m_g_mix, w_in=m_w_in, w_ya=m_w_ya, w_yb=m_w_yb, pool_w=m_pool_w, pool_scale=m_pool_scale,
                w_o=m_w_o, g_ffn=m_g_ffn, w_up=m_w_up, conv_w=m_conv_w, conv_b=m_conv_b, w_down=m_w_down,
                g_ple=m_g_ple, w_ple=m_w_ple, w_ple_gate=m_w_ple_gate, g_final=m_g_final)
    v_in = dict(g_mix=v_g_mix, w_in=v_w_in, w_ya=v_w_ya, w_yb=v_w_yb, pool_w=v_pool_w, pool_scale=v_pool_scale,
                w_o=v_w_o, g_ffn=v_g_ffn, w_up=v_w_up, conv_w=v_conv_w, conv_b=v_conv_b, w_down=v_w_down,
                g_ple=v_g_ple, w_ple=v_w_ple, w_ple_gate=v_w_ple_gate, g_final=v_g_final)
    names = ["g_mix", "w_in", "w_ya", "w_yb", "pool_w", "pool_scale", "w_o", "g_ffn", "w_up", "conv_w", "conv_b",
             "w_down", "g_ple", "w_ple", "w_ple_gate", "g_final"]
    deltas, new_m, new_v = [], [], []
    for nme in names:
        gr = grads[nme].reshape(weights[nme].shape)
        grads[nme], dlt, nm, nv = _adamw("adamw_" + nme, weights[nme], gr, m_in[nme], v_in[nme])
        deltas.append(dlt)
        new_m.append(nm)
        new_v.append(nv)
    return (loss, grad_x, *[grads[nme] for nme in names], *deltas, *new_m, *new_v)
```
